```python
import jax, jax.numpy as jnp
from jax import lax
import numpy as np

D_MODEL = 4096
BATCH = 16
SEQ = 2048
DEPTH = 1

SG_WIDTH = 2048
SG_GROUPS = 8
SG_GROUP_DIM = SG_WIDTH // SG_GROUPS
CHUNK = 128
LRU_WIDTH = 4096
LRU_HEADS = 16
LRU_HEAD_DIM = LRU_WIDTH // LRU_HEADS
LRU_CONV = 4
LRU_C = 8.0
FFN_WIDTH = 3 * D_MODEL
FFN_CONV = 3
EPS = 1e-6
IN_SPLITS = [SG_WIDTH, 2 * SG_WIDTH, 2 * SG_WIDTH + LRU_WIDTH, 2 * SG_WIDTH + 2 * LRU_WIDTH,
             2 * SG_WIDTH + 2 * LRU_WIDTH + D_MODEL]
IN_COLS = 2 * SG_WIDTH + 2 * LRU_WIDTH + 2 * D_MODEL

kernel_name = "hybrid_gmlp_rglru_gated_merge"


def rmsnorm(x, g):
    xf = x.astype(jnp.float32)
    var = jnp.mean(xf * xf, axis=-1, keepdims=True)
    return (xf * lax.rsqrt(var + EPS) * g.astype(jnp.float32)).astype(x.dtype)


def layernorm(x, g, b):
    xf = x.astype(jnp.float32)
    mu = jnp.mean(xf, axis=-1, keepdims=True)
    xc = xf - mu
    var = jnp.mean(xc * xc, axis=-1, keepdims=True)
    return (xc * lax.rsqrt(var + EPS) * g.astype(jnp.float32) + b.astype(jnp.float32)).astype(x.dtype)


def causal_dwconv(x, w, b):
    K = w.shape[0]
    S = x.shape[1]
    xp = jnp.pad(x, ((0, 0), (K - 1, 0), (0, 0)))
    w = w.astype(x.dtype)
    out = xp[:, 0:S] * w[0]
    for k in range(1, K):
        out = out + xp[:, k:k + S] * w[k]
    return out + b.astype(x.dtype)


def spatial_gating(u, v, ln_g, ln_b, w_s, b_s):
    B, S, _ = u.shape
    vn = layernorm(v, ln_g, ln_b).reshape(B, S // CHUNK, CHUNK, SG_GROUPS, SG_GROUP_DIM)
    mask = jnp.tril(jnp.ones((CHUNK, CHUNK), dtype=bool))
    w = jnp.where(mask, w_s, 0.0).astype(vn.dtype)
    mixed = jnp.einsum('gts,bcsgd->bctgd', w, vn) + b_s.T.astype(vn.dtype)[:, :, None]
    return u * mixed.reshape(B, S, SG_WIDTH)


def rg_lru(x, w_a, b_a, w_x, b_x, lam):
    B, S, W = x.shape
    xf = x.astype(jnp.float32)
    xh = xf.reshape(B, S, LRU_HEADS, LRU_HEAD_DIM)
    rec_gate = jax.nn.sigmoid(jnp.einsum('bshi,hij->bshj', xh, w_a.astype(jnp.float32)).reshape(B, S, W)
                              + b_a.astype(jnp.float32))
    in_gate = jax.nn.sigmoid(jnp.einsum('bshi,hij->bshj', xh, w_x.astype(jnp.float32)).reshape(B, S, W)
                             + b_x.astype(jnp.float32))
    log_a = -LRU_C * rec_gate * jax.nn.softplus(-lam.astype(jnp.float32))
    a = jnp.exp(log_a)
    gated_x = jnp.sqrt(-jnp.expm1(2.0 * log_a)) * (in_gate * xf)

    def combine(left, right):
        a1, b1 = left
        a2, b2 = right
        return a1 * a2, a2 * b1 + b2

    _, h = lax.associative_scan(combine, (a, gated_x), axis=1)
    return h.astype(x.dtype)


def hybrid_layer(x, g_mix, w_in, sg_ln_g, sg_ln_b, sg_w, sg_b, lru_conv_w, lru_conv_b,
                 lru_wa, lru_ba, lru_wx, lru_bx, lru_lam, p_sg, p_lru, w_out,
                 g_ffn, w_up, ffn_conv_w, ffn_conv_b, w_down):
    h = rmsnorm(x, g_mix)
    proj = h @ w_in.astype(h.dtype)
    z_u, z_v, x_r, y_r, gate_a, gate_b = jnp.split(proj, IN_SPLITS, axis=-1)
    y_a = spatial_gating(jax.nn.gelu(z_u), jax.nn.gelu(z_v), sg_ln_g, sg_ln_b, sg_w, sg_b)
    x_r = causal_dwconv(x_r, lru_conv_w, lru_conv_b)
    y_b = rg_lru(x_r, lru_wa, lru_ba, lru_wx, lru_bx, lru_lam) * jax.nn.gelu(y_r)
    merged = (jax.nn.sigmoid(gate_a) * (y_a @ p_sg.astype(y_a.dtype))
              + jax.nn.sigmoid(gate_b) * (y_b @ p_lru.astype(y_b.dtype)))
    x = x + merged @ w_out.astype(merged.dtype)
    h = rmsnorm(x, g_ffn)
    up = causal_dwconv(h @ w_up.astype(h.dtype), ffn_conv_w, ffn_conv_b)
    c_gate, c_val = jnp.split(up, 2, axis=-1)
    x = x + (jax.nn.gelu(c_gate) * c_val) @ w_down.astype(up.dtype)
    return x


def _fwd_setup_inputs(seed: int = 0) -> dict:
    key = jax.random.key(seed)
    ks = jax.random.split(key, 24)
    f32 = jnp.float32
    L = DEPTH

    def nrm(k, shape, scale):
        return jax.random.normal(k, shape, f32) * scale

    a0 = jax.random.uniform(ks[14], (L, LRU_WIDTH), f32, 0.9, 0.999)
    p = a0 ** (1.0 / LRU_C)
    lru_lam = jnp.log(p) - jnp.log1p(-p)
    return {
        "x": nrm(ks[0], (BATCH, SEQ, D_MODEL), 1.0),
        "g_mix": 1.0 + nrm(ks[1], (L, D_MODEL), 0.02),
        "w_in": nrm(ks[2], (L, D_MODEL, IN_COLS), D_MODEL ** -0.5),
        "sg_ln_g": 1.0 + nrm(ks[3], (L, SG_WIDTH), 0.02),
        "sg_ln_b": nrm(ks[4], (L, SG_WIDTH), 0.02),
        "sg_w": nrm(ks[5], (L, SG_GROUPS, CHUNK, CHUNK), CHUNK ** -0.5),
        "sg_b": 1.0 + nrm(ks[6], (L, SG_GROUPS, CHUNK), 0.02),
        "lru_conv_w": nrm(ks[7], (L, LRU_CONV, LRU_WIDTH), LRU_CONV ** -0.5),
        "lru_conv_b": nrm(ks[8], (L, LRU_WIDTH), 0.02),
        "lru_wa": nrm(ks[9], (L, LRU_HEADS, LRU_HEAD_DIM, LRU_HEAD_DIM), LRU_HEAD_DIM ** -0.5),
        "lru_ba": nrm(ks[10], (L, LRU_WIDTH), 0.02),
        "lru_wx": nrm(ks[11], (L, LRU_HEADS, LRU_HEAD_DIM, LRU_HEAD_DIM), LRU_HEAD_DIM ** -0.5),
        "lru_bx": nrm(ks[12], (L, LRU_WIDTH), 0.02),
        "lru_lam": lru_lam,
        "p_sg": nrm(ks[15], (L, SG_WIDTH, D_MODEL), SG_WIDTH ** -0.5),
        "p_lru": nrm(ks[16], (L, LRU_WIDTH, D_MODEL), LRU_WIDTH ** -0.5),
        "w_out": nrm(ks[17], (L, D_MODEL, D_MODEL), D_MODEL ** -0.5),
        "g_ffn": 1.0 + nrm(ks[18], (L, D_MODEL), 0.02),
        "w_up": nrm(ks[19], (L, D_MODEL, 2 * FFN_WIDTH), D_MODEL ** -0.5),
        "ffn_conv_w": nrm(ks[20], (L, FFN_CONV, 2 * FFN_WIDTH), FFN_CONV ** -0.5),
        "ffn_conv_b": nrm(ks[21], (L, 2 * FFN_WIDTH), 0.02),
        "w_down": nrm(ks[22], (L, FFN_WIDTH, D_MODEL), FFN_WIDTH ** -0.5),
        "g_final": 1.0 + nrm(ks[23], (D_MODEL,), 0.02),
    }


def _fwd_reference(x, g_mix, w_in, sg_ln_g, sg_ln_b, sg_w, sg_b, lru_conv_w, lru_conv_b,
              lru_wa, lru_ba, lru_wx, lru_bx, lru_lam, p_sg, p_lru, w_out,
              g_ffn, w_up, ffn_conv_w, ffn_conv_b, w_down, g_final):
    for l in range(DEPTH):
        x = hybrid_layer(x, g_mix[l], w_in[l], sg_ln_g[l], sg_ln_b[l], sg_w[l], sg_b[l],
                         lru_conv_w[l], lru_conv_b[l], lru_wa[l], lru_ba[l], lru_wx[l], lru_bx[l],
                         lru_lam[l], p_sg[l], p_lru[l], w_out[l], g_ffn[l], w_up[l],
                         ffn_conv_w[l], ffn_conv_b[l], w_down[l])
    return rmsnorm(x, g_final)


import jax as _jax
import jax.numpy as _jnp

TWIN_FORMAT = 'train_step'
FWD_PARAMS = ['x', 'g_mix', 'w_in', 'sg_ln_g', 'sg_ln_b', 'sg_w', 'sg_b', 'lru_conv_w', 'lru_conv_b', 'lru_wa', 'lru_ba', 'lru_wx', 'lru_bx', 'lru_lam', 'p_sg', 'p_lru', 'w_out', 'g_ffn', 'w_up', 'ffn_conv_w', 'ffn_conv_b', 'w_down', 'g_final']
TWIN_WEIGHTS = ['g_mix', 'w_in', 'sg_ln_g', 'sg_ln_b', 'sg_w', 'sg_b', 'lru_conv_w', 'lru_conv_b', 'lru_wa', 'lru_ba', 'lru_wx', 'lru_bx', 'lru_lam', 'p_sg', 'p_lru', 'w_out', 'g_ffn', 'w_up', 'ffn_conv_w', 'ffn_conv_b', 'w_down', 'g_final']
TWIN_DIFF_INPUT = 'x'
TWIN_INPUTS = ['x', 'g_mix', 'w_in', 'sg_ln_g', 'sg_ln_b', 'sg_w', 'sg_b', 'lru_conv_w', 'lru_conv_b', 'lru_wa', 'lru_ba', 'lru_wx', 'lru_bx', 'lru_lam', 'p_sg', 'p_lru', 'w_out', 'g_ffn', 'w_up', 'ffn_conv_w', 'ffn_conv_b', 'w_down', 'g_final', 'loss_target', 'm_g_mix', 'm_w_in', 'm_sg_ln_g', 'm_sg_ln_b', 'm_sg_w', 'm_sg_b', 'm_lru_conv_w', 'm_lru_conv_b', 'm_lru_wa', 'm_lru_ba', 'm_lru_wx', 'm_lru_bx', 'm_lru_lam', 'm_p_sg', 'm_p_lru', 'm_w_out', 'm_g_ffn', 'm_w_up', 'm_ffn_conv_w', 'm_ffn_conv_b', 'm_w_down', 'm_g_final', 'v_g_mix', 'v_w_in', 'v_sg_ln_g', 'v_sg_ln_b', 'v_sg_w', 'v_sg_b', 'v_lru_conv_w', 'v_lru_conv_b', 'v_lru_wa', 'v_lru_ba', 'v_lru_wx', 'v_lru_bx', 'v_lru_lam', 'v_p_sg', 'v_p_lru', 'v_w_out', 'v_g_ffn', 'v_w_up', 'v_ffn_conv_w', 'v_ffn_conv_b', 'v_w_down', 'v_g_final']
TWIN_OUTPUTS = ['loss', 'grad_x', 'grad_g_mix', 'grad_w_in', 'grad_sg_ln_g', 'grad_sg_ln_b', 'grad_sg_w', 'grad_sg_b', 'grad_lru_conv_w', 'grad_lru_conv_b', 'grad_lru_wa', 'grad_lru_ba', 'grad_lru_wx', 'grad_lru_bx', 'grad_lru_lam', 'grad_p_sg', 'grad_p_lru', 'grad_w_out', 'grad_g_ffn', 'grad_w_up', 'grad_ffn_conv_w', 'grad_ffn_conv_b', 'grad_w_down', 'grad_g_final', 'delta_g_mix', 'delta_w_in', 'delta_sg_ln_g', 'delta_sg_ln_b', 'delta_sg_w', 'delta_sg_b', 'delta_lru_conv_w', 'delta_lru_conv_b', 'delta_lru_wa', 'delta_lru_ba', 'delta_lru_wx', 'delta_lru_bx', 'delta_lru_lam', 'delta_p_sg', 'delta_p_lru', 'delta_w_out', 'delta_g_ffn', 'delta_w_up', 'delta_ffn_conv_w', 'delta_ffn_conv_b', 'delta_w_down', 'delta_g_final', 'new_m_g_mix', 'new_m_w_in', 'new_m_sg_ln_g', 'new_m_sg_ln_b', 'new_m_sg_w', 'new_m_sg_b', 'new_m_lru_conv_w', 'new_m_lru_conv_b', 'new_m_lru_wa', 'new_m_lru_ba', 'new_m_lru_wx', 'new_m_lru_bx', 'new_m_lru_lam', 'new_m_p_sg', 'new_m_p_lru', 'new_m_w_out', 'new_m_g_ffn', 'new_m_w_up', 'new_m_ffn_conv_w', 'new_m_ffn_conv_b', 'new_m_w_down', 'new_m_g_final', 'new_v_g_mix', 'new_v_w_in', 'new_v_sg_ln_g', 'new_v_sg_ln_b', 'new_v_sg_w', 'new_v_sg_b', 'new_v_lru_conv_w', 'new_v_lru_conv_b', 'new_v_lru_wa', 'new_v_lru_ba', 'new_v_lru_wx', 'new_v_lru_bx', 'new_v_lru_lam', 'new_v_p_sg', 'new_v_p_lru', 'new_v_w_out', 'new_v_g_ffn', 'new_v_w_up', 'new_v_ffn_conv_w', 'new_v_ffn_conv_b', 'new_v_w_down', 'new_v_g_final']
TWIN_LEAF_KINDS = {'loss': 'loss', 'grad_x': 'grad_x', 'grad_g_mix': 'grad_w', 'grad_w_in': 'grad_w', 'grad_sg_ln_g': 'grad_w', 'grad_sg_ln_b': 'grad_w', 'grad_sg_w': 'grad_w', 'grad_sg_b': 'grad_w', 'grad_lru_conv_w': 'grad_w', 'grad_lru_conv_b': 'grad_w', 'grad_lru_wa': 'grad_w', 'grad_lru_ba': 'grad_w', 'grad_lru_wx': 'grad_w', 'grad_lru_bx': 'grad_w', 'grad_lru_lam': 'grad_w', 'grad_p_sg': 'grad_w', 'grad_p_lru': 'grad_w', 'grad_w_out': 'grad_w', 'grad_g_ffn': 'grad_w', 'grad_w_up': 'grad_w', 'grad_ffn_conv_w': 'grad_w', 'grad_ffn_conv_b': 'grad_w', 'grad_w_down': 'grad_w', 'grad_g_final': 'grad_w', 'delta_g_mix': 'delta_w', 'delta_w_in': 'delta_w', 'delta_sg_ln_g': 'delta_w', 'delta_sg_ln_b': 'delta_w', 'delta_sg_w': 'delta_w', 'delta_sg_b': 'delta_w', 'delta_lru_conv_w': 'delta_w', 'delta_lru_conv_b': 'delta_w', 'delta_lru_wa': 'delta_w', 'delta_lru_ba': 'delta_w', 'delta_lru_wx': 'delta_w', 'delta_lru_bx': 'delta_w', 'delta_lru_lam': 'delta_w', 'delta_p_sg': 'delta_w', 'delta_p_lru': 'delta_w', 'delta_w_out': 'delta_w', 'delta_g_ffn': 'delta_w', 'delta_w_up': 'delta_w', 'delta_ffn_conv_w': 'delta_w', 'delta_ffn_conv_b': 'delta_w', 'delta_w_down': 'delta_w', 'delta_g_final': 'delta_w', 'new_m_g_mix': 'new_m', 'new_m_w_in': 'new_m', 'new_m_sg_ln_g': 'new_m', 'new_m_sg_ln_b': 'new_m', 'new_m_sg_w': 'new_m', 'new_m_sg_b': 'new_m', 'new_m_lru_conv_w': 'new_m', 'new_m_lru_conv_b': 'new_m', 'new_m_lru_wa': 'new_m', 'new_m_lru_ba': 'new_m', 'new_m_lru_wx': 'new_m', 'new_m_lru_bx': 'new_m', 'new_m_lru_lam': 'new_m', 'new_m_p_sg': 'new_m', 'new_m_p_lru': 'new_m', 'new_m_w_out': 'new_m', 'new_m_g_ffn': 'new_m', 'new_m_w_up': 'new_m', 'new_m_ffn_conv_w': 'new_m', 'new_m_ffn_conv_b': 'new_m', 'new_m_w_down': 'new_m', 'new_m_g_final': 'new_m', 'new_v_g_mix': 'new_v', 'new_v_w_in': 'new_v', 'new_v_sg_ln_g': 'new_v', 'new_v_sg_ln_b': 'new_v', 'new_v_sg_w': 'new_v', 'new_v_sg_b': 'new_v', 'new_v_lru_conv_w': 'new_v', 'new_v_lru_conv_b': 'new_v', 'new_v_lru_wa': 'new_v', 'new_v_lru_ba': 'new_v', 'new_v_lru_wx': 'new_v', 'new_v_lru_bx': 'new_v', 'new_v_lru_lam': 'new_v', 'new_v_p_sg': 'new_v', 'new_v_p_lru': 'new_v', 'new_v_w_out': 'new_v', 'new_v_g_ffn': 'new_v', 'new_v_w_up': 'new_v', 'new_v_ffn_conv_w': 'new_v', 'new_v_ffn_conv_b': 'new_v', 'new_v_w_down': 'new_v', 'new_v_g_final': 'new_v'}


def _forward(args):
    return _fwd_reference(*[args[k] for k in FWD_PARAMS])


def _output_shape():
    def fwd():
        inp = _fwd_setup_inputs(0)
        return _fwd_reference(*[inp[k] for k in FWD_PARAMS])
    out = _jax.eval_shape(fwd)
    return out.shape, out.dtype

N_MICROBATCH = 1
ADAM_LR = 0.001
ADAM_B1 = 0.9
ADAM_B2 = 0.999
ADAM_EPS = 1e-08
ADAM_WD = 0.01
ADAM_STEP = 10
PER_EXAMPLE_BATCH_AXIS = {'x': 0, 'loss_target': 0}
SHARED_INPUTS = []
_WEIGHT_DTYPES = {'g_mix': _jnp.float32, 'w_in': _jnp.float32, 'sg_ln_g': _jnp.float32, 'sg_ln_b': _jnp.float32, 'sg_w': _jnp.float32, 'sg_b': _jnp.float32, 'lru_conv_w': _jnp.float32, 'lru_conv_b': _jnp.float32, 'lru_wa': _jnp.float32, 'lru_ba': _jnp.float32, 'lru_wx': _jnp.float32, 'lru_bx': _jnp.float32, 'lru_lam': _jnp.float32, 'p_sg': _jnp.float32, 'p_lru': _jnp.float32, 'w_out': _jnp.float32, 'g_ffn': _jnp.float32, 'w_up': _jnp.float32, 'ffn_conv_w': _jnp.float32, 'ffn_conv_b': _jnp.float32, 'w_down': _jnp.float32, 'g_final': _jnp.float32}
MOMENT_SCALE = {'g_mix': 2.912539e-02, 'w_in': 1.300010e-02, 'sg_ln_g': 1.616018e-02, 'sg_ln_b': 1.521020e-02, 'sg_w': 2.255740e-02, 'sg_b': 3.298574e-02, 'lru_conv_w': 1.016352e-02, 'lru_conv_b': 1.145596e-01, 'lru_wa': 2.802288e-03, 'lru_ba': 2.443819e-03, 'lru_wx': 4.939274e-03, 'lru_bx': 3.716052e-03, 'lru_lam': 5.172153e-03, 'p_sg': 1.962510e-02, 'p_lru': 9.907285e-03, 'w_out': 2.196392e-02, 'g_ffn': 3.292436e-02, 'w_up': 1.336338e-02, 'ffn_conv_w': 1.331443e-02, 'ffn_conv_b': 1.313066e-02, 'w_down': 2.282030e-02, 'g_final': 7.997697e+00}


def _to_microbatches(a, axis):
    t = _jnp.moveaxis(a, axis, 0)
    t = t.reshape((N_MICROBATCH, t.shape[0] // N_MICROBATCH) + t.shape[1:])
    return _jnp.moveaxis(t, 1, axis + 1)


def setup_inputs(seed: int = 0) -> dict:
    inp = _fwd_setup_inputs(seed)
    key = _jax.random.fold_in(_jax.random.key(seed), 7919)
    shape, _ = _output_shape()
    out = dict(inp)
    out["loss_target"] = _jax.random.normal(_jax.random.fold_in(key, 0), shape, _jnp.float32)
    for i, name in enumerate(TWIN_WEIGHTS):
        w = inp[name].astype(_jnp.float32)
        if MOMENT_SCALE is None:
            s = _jnp.sqrt(_jnp.mean(_jnp.square(w)) + 1e-30)
        else:
            s = MOMENT_SCALE[name]
        km, kv = _jax.random.split(_jax.random.fold_in(key, i + 1))
        out[name] = w
        out["m_" + name] = s * _jax.random.normal(km, w.shape, _jnp.float32)
        out["v_" + name] = (s * s) * _jax.random.uniform(kv, w.shape, _jnp.float32, 0.5, 1.5)
    if N_MICROBATCH > 1:
        for name, axis in PER_EXAMPLE_BATCH_AXIS.items():
            out[name] = _to_microbatches(out[name], axis)
    return {'x': out['x'], 'g_mix': out['g_mix'], 'w_in': out['w_in'], 'sg_ln_g': out['sg_ln_g'], 'sg_ln_b': out['sg_ln_b'], 'sg_w': out['sg_w'], 'sg_b': out['sg_b'], 'lru_conv_w': out['lru_conv_w'], 'lru_conv_b': out['lru_conv_b'], 'lru_wa': out['lru_wa'], 'lru_ba': out['lru_ba'], 'lru_wx': out['lru_wx'], 'lru_bx': out['lru_bx'], 'lru_lam': out['lru_lam'], 'p_sg': out['p_sg'], 'p_lru': out['p_lru'], 'w_out': out['w_out'], 'g_ffn': out['g_ffn'], 'w_up': out['w_up'], 'ffn_conv_w': out['ffn_conv_w'], 'ffn_conv_b': out['ffn_conv_b'], 'w_down': out['w_down'], 'g_final': out['g_final'], 'loss_target': out['loss_target'], 'm_g_mix': out['m_g_mix'], 'm_w_in': out['m_w_in'], 'm_sg_ln_g': out['m_sg_ln_g'], 'm_sg_ln_b': out['m_sg_ln_b'], 'm_sg_w': out['m_sg_w'], 'm_sg_b': out['m_sg_b'], 'm_lru_conv_w': out['m_lru_conv_w'], 'm_lru_conv_b': out['m_lru_conv_b'], 'm_lru_wa': out['m_lru_wa'], 'm_lru_ba': out['m_lru_ba'], 'm_lru_wx': out['m_lru_wx'], 'm_lru_bx': out['m_lru_bx'], 'm_lru_lam': out['m_lru_lam'], 'm_p_sg': out['m_p_sg'], 'm_p_lru': out['m_p_lru'], 'm_w_out': out['m_w_out'], 'm_g_ffn': out['m_g_ffn'], 'm_w_up': out['m_w_up'], 'm_ffn_conv_w': out['m_ffn_conv_w'], 'm_ffn_conv_b': out['m_ffn_conv_b'], 'm_w_down': out['m_w_down'], 'm_g_final': out['m_g_final'], 'v_g_mix': out['v_g_mix'], 'v_w_in': out['v_w_in'], 'v_sg_ln_g': out['v_sg_ln_g'], 'v_sg_ln_b': out['v_sg_ln_b'], 'v_sg_w': out['v_sg_w'], 'v_sg_b': out['v_sg_b'], 'v_lru_conv_w': out['v_lru_conv_w'], 'v_lru_conv_b': out['v_lru_conv_b'], 'v_lru_wa': out['v_lru_wa'], 'v_lru_ba': out['v_lru_ba'], 'v_lru_wx': out['v_lru_wx'], 'v_lru_bx': out['v_lru_bx'], 'v_lru_lam': out['v_lru_lam'], 'v_p_sg': out['v_p_sg'], 'v_p_lru': out['v_p_lru'], 'v_w_out': out['v_w_out'], 'v_g_ffn': out['v_g_ffn'], 'v_w_up': out['v_w_up'], 'v_ffn_conv_w': out['v_ffn_conv_w'], 'v_ffn_conv_b': out['v_ffn_conv_b'], 'v_w_down': out['v_w_down'], 'v_g_final': out['v_g_final']}


def _loss(weights, diff, rest, loss_target):
    with _jax.named_scope("forward"):
        args = {**rest, TWIN_DIFF_INPUT: diff, **{k: w.astype(_WEIGHT_DTYPES[k]) for k, w in weights.items()}}
        y = _forward(args)
    with _jax.named_scope("loss_head"):
        err = _jnp.square(y.astype(_jnp.float32) - loss_target)
        return 0.5 * _jnp.sum(_jnp.mean(err, axis=-1)) if err.ndim else 0.5 * err


def _adamw(w, g, m, v):
    m = ADAM_B1 * m + (1.0 - ADAM_B1) * g
    v = ADAM_B2 * v + (1.0 - ADAM_B2) * _jnp.square(g)
    m_hat = m / (1.0 - ADAM_B1 ** ADAM_STEP)
    v_hat = v / (1.0 - ADAM_B2 ** ADAM_STEP)
    delta = -ADAM_LR * (m_hat / (_jnp.sqrt(v_hat) + ADAM_EPS) + ADAM_WD * w)
    return delta, m, v


def reference(x, g_mix, w_in, sg_ln_g, sg_ln_b, sg_w, sg_b, lru_conv_w, lru_conv_b, lru_wa, lru_ba, lru_wx, lru_bx, lru_lam, p_sg, p_lru, w_out, g_ffn, w_up, ffn_conv_w, ffn_conv_b, w_down, g_final, loss_target, m_g_mix, m_w_in, m_sg_ln_g, m_sg_ln_b, m_sg_w, m_sg_b, m_lru_conv_w, m_lru_conv_b, m_lru_wa, m_lru_ba, m_lru_wx, m_lru_bx, m_lru_lam, m_p_sg, m_p_lru, m_w_out, m_g_ffn, m_w_up, m_ffn_conv_w, m_ffn_conv_b, m_w_down, m_g_final, v_g_mix, v_w_in, v_sg_ln_g, v_sg_ln_b, v_sg_w, v_sg_b, v_lru_conv_w, v_lru_conv_b, v_lru_wa, v_lru_ba, v_lru_wx, v_lru_bx, v_lru_lam, v_p_sg, v_p_lru, v_w_out, v_g_ffn, v_w_up, v_ffn_conv_w, v_ffn_conv_b, v_w_down, v_g_final):
    given = dict(x=x, g_mix=g_mix, w_in=w_in, sg_ln_g=sg_ln_g, sg_ln_b=sg_ln_b, sg_w=sg_w, sg_b=sg_b, lru_conv_w=lru_conv_w, lru_conv_b=lru_conv_b, lru_wa=lru_wa, lru_ba=lru_ba, lru_wx=lru_wx, lru_bx=lru_bx, lru_lam=lru_lam, p_sg=p_sg, p_lru=p_lru, w_out=w_out, g_ffn=g_ffn, w_up=w_up, ffn_conv_w=ffn_conv_w, ffn_conv_b=ffn_conv_b, w_down=w_down, g_final=g_final, loss_target=loss_target, m_g_mix=m_g_mix, m_w_in=m_w_in, m_sg_ln_g=m_sg_ln_g, m_sg_ln_b=m_sg_ln_b, m_sg_w=m_sg_w, m_sg_b=m_sg_b, m_lru_conv_w=m_lru_conv_w, m_lru_conv_b=m_lru_conv_b, m_lru_wa=m_lru_wa, m_lru_ba=m_lru_ba, m_lru_wx=m_lru_wx, m_lru_bx=m_lru_bx, m_lru_lam=m_lru_lam, m_p_sg=m_p_sg, m_p_lru=m_p_lru, m_w_out=m_w_out, m_g_ffn=m_g_ffn, m_w_up=m_w_up, m_ffn_conv_w=m_ffn_conv_w, m_ffn_conv_b=m_ffn_conv_b, m_w_down=m_w_down, m_g_final=m_g_final, v_g_mix=v_g_mix, v_w_in=v_w_in, v_sg_ln_g=v_sg_ln_g, v_sg_ln_b=v_sg_ln_b, v_sg_w=v_sg_w, v_sg_b=v_sg_b, v_lru_conv_w=v_lru_conv_w, v_lru_conv_b=v_lru_conv_b, v_lru_wa=v_lru_wa, v_lru_ba=v_lru_ba, v_lru_wx=v_lru_wx, v_lru_bx=v_lru_bx, v_lru_lam=v_lru_lam, v_p_sg=v_p_sg, v_p_lru=v_p_lru, v_w_out=v_w_out, v_g_ffn=v_g_ffn, v_w_up=v_w_up, v_ffn_conv_w=v_ffn_conv_w, v_ffn_conv_b=v_ffn_conv_b, v_w_down=v_w_down, v_g_final=v_g_final)
    weights = {n: given[n] for n in TWIN_WEIGHTS}
    shared = {n: given[n] for n in SHARED_INPUTS}
    per_example = {n: given[n] for n in ['x']}
    grad_fn = _jax.value_and_grad(_loss, argnums=(0, 1))

    def one_microbatch(ex, loss_target):
        ex = dict(ex)
        diff = ex.pop(TWIN_DIFF_INPUT)
        return grad_fn(weights, diff, {**shared, **ex}, loss_target)

    if N_MICROBATCH == 1:
        loss, (grad_w, grad_x) = one_microbatch(per_example, given["loss_target"])
    else:
        def body(carry, xs):
            loss_sum, grad_sum = carry
            l_k, (gw_k, gx_k) = one_microbatch(xs[0], xs[1])
            with _jax.named_scope("update"):
                return (loss_sum + l_k, _jax.tree.map(_jnp.add, grad_sum, gw_k)), gx_k

        init = (_jnp.zeros((), _jnp.float32), _jax.tree.map(_jnp.zeros_like, weights))
        (loss, grad_w), grad_x = _jax.lax.scan(body, init, (per_example, given["loss_target"]))
    with _jax.named_scope("update"):
        delta_w, new_m, new_v = {}, {}, {}
        for n in TWIN_WEIGHTS:
            delta_w[n], new_m[n], new_v[n] = _adamw(weights[n], grad_w[n], given["m_" + n], given["v_" + n])
    return (loss, grad_x, *[grad_w[n] for n in TWIN_WEIGHTS], *[delta_w[n] for n in TWIN_WEIGHTS],
            *[new_m[n] for n in TWIN_WEIGHTS], *[new_v[n] for n in TWIN_WEIGHTS])
```

```python
import functools
import math

import jax
import jax.numpy as jnp
from jax import lax
from jax.experimental import pallas as pl
from jax.experimental.pallas import tpu as pltpu

F32 = jnp.float32
BF16 = jnp.bfloat16
MESH = pl.DeviceIdType.MESH
ANY = pl.BlockSpec(memory_space=pl.ANY)

N_DEV = 8
EPS = 1e-6
CHUNK = 128
SG_GROUPS = 8
LRU_HEADS = 16
LRU_C = 8.0
ADAM_LR = 0.001
ADAM_B1 = 0.9
ADAM_B2 = 0.999
ADAM_EPS = 1e-08
ADAM_WD = 0.01
ADAM_STEP = 10

V7X_VMEM_LIMIT = 56 * 1024 * 1024
LANES = 128
SUBLANES = 8
MXU = 256

_GELU_C0 = math.sqrt(2.0 / math.pi)
_GELU_C1 = 0.044715


def _params(n_axes):
    return pltpu.CompilerParams(dimension_semantics=("arbitrary",) * n_axes, vmem_limit_bytes=V7X_VMEM_LIMIT)


def _tile(dim, pref, align):
    t = (min(pref, dim) // align) * align
    while t >= align:
        if dim % t == 0:
            return t
        t -= align
    return dim


def _gelu(x):
    return x * (0.5 * (1.0 + jnp.tanh(_GELU_C0 * (x + _GELU_C1 * (x * x * x)))))


def _gelu_and_grad(x):
    t = jnp.tanh(_GELU_C0 * (x + _GELU_C1 * (x * x * x)))
    cdf = 0.5 * (1.0 + t)
    dcdf = 0.5 * (1.0 - t * t) * (_GELU_C0 * (1.0 + 3.0 * _GELU_C1 * (x * x)))
    return x * cdf, cdf + x * dcdf


def _sigmoid(x):
    return 1.0 / (1.0 + jnp.exp(-x))


def _shift_down(x, d):
    if d == 0:
        return x
    row = lax.broadcasted_iota(jnp.int32, x.shape, 0)
    return jnp.where(row >= d, pltpu.roll(x, d, 0), 0.0)


def _shift_up(x, d):
    if d == 0:
        return x
    s = x.shape[0]
    row = lax.broadcasted_iota(jnp.int32, x.shape, 0)
    return jnp.where(row < s - d, pltpu.roll(x, s - d, 0), 0.0)


def _causal_conv(x, w, b):
    k_taps = w.shape[0]
    out = _shift_down(x, k_taps - 1) * w[0:1, :]
    for k in range(1, k_taps):
        out = out + _shift_down(x, k_taps - 1 - k) * w[k:k + 1, :]
    return out + b


def _causal_conv_bwd_x(d_out, w):
    k_taps = w.shape[0]
    d_x = _shift_up(d_out, k_taps - 1) * w[0:1, :]
    for k in range(1, k_taps):
        d_x = d_x + _shift_up(d_out, k_taps - 1 - k) * w[k:k + 1, :]
    return d_x


def _causal_conv_bwd_w(d_out, x, k_taps):
    rows = [jnp.sum(d_out * _shift_down(x, k_taps - 1 - k), axis=0, keepdims=True) for k in range(k_taps)]
    return jnp.concatenate(rows, axis=0)


def _mm_nn(a, w, *, out_dtype, name, residual=None):
    m, k = a.shape
    nb, _, n_blk = w.shape
    tm, tn, tk = _tile(m, 1024, MXU), _tile(n_blk, 1536, MXU), _tile(k, 512, MXU)
    per = n_blk // tn
    nk = k // tk

    def kern(*refs):
        if residual is None:
            a_ref, w_ref, o_ref, acc = refs
        else:
            a_ref, w_ref, r_ref, o_ref, acc = refs
        kk = pl.program_id(2)

        @pl.when(kk == 0)
        def _():
            acc[...] = jnp.zeros_like(acc)

        acc[...] += jnp.dot(a_ref[...], w_ref[...], preferred_element_type=F32)

        @pl.when(kk == nk - 1)
        def _():
            out = acc[...] if residual is None else acc[...] + r_ref[...]
            o_ref[...] = out.astype(o_ref.dtype)

    in_specs = [pl.BlockSpec((tm, tk), lambda i, j, kk: (i, kk)),
                pl.BlockSpec((None, tk, tn), lambda i, j, kk: (j // per, kk, j % per))]
    args = [a, w]
    if residual is not None:
        in_specs.append(pl.BlockSpec((tm, tn), lambda i, j, kk: (i, j)))
        args.append(residual)
    return pl.pallas_call(
        kern, name=name, grid=(m // tm, nb * per, nk), in_specs=in_specs,
        out_specs=pl.BlockSpec((tm, tn), lambda i, j, kk: (i, j)),
        out_shape=jax.ShapeDtypeStruct((m, nb * n_blk), out_dtype),
        scratch_shapes=[pltpu.VMEM((tm, tn), F32)], compiler_params=_params(3))(*args)


def _mm_nt(g, w, *, name):
    m, n = g.shape
    nb, k, n_blk = w.shape
    tm, tko, tn = _tile(m, 1024, MXU), _tile(k, 1024, MXU), _tile(n_blk, 512, MXU)
    per = n_blk // tn
    nn = n // tn

    def kern(g_ref, w_ref, o_ref, acc):
        jn = pl.program_id(2)

        @pl.when(jn == 0)
        def _():
            acc[...] = jnp.zeros_like(acc)

        acc[...] += lax.dot_general(g_ref[...], w_ref[...], (((1,), (1,)), ((), ())), preferred_element_type=F32)

        @pl.when(jn == nn - 1)
        def _():
            o_ref[...] = acc[...]

    return pl.pallas_call(
        kern, name=name, grid=(m // tm, k // tko, nn),
        in_specs=[pl.BlockSpec((tm, tn), lambda i, j, jn: (i, jn)),
                  pl.BlockSpec((None, tko, tn), lambda i, j, jn: (jn // per, j, jn % per))],
        out_specs=pl.BlockSpec((tm, tko), lambda i, j, jn: (i, j)),
        out_shape=jax.ShapeDtypeStruct((m, k), F32),
        scratch_shapes=[pltpu.VMEM((tm, tko), F32)], compiler_params=_params(3))(g, w)


def _mm_tn(a, g, nb, *, name):
    m, k = a.shape
    n = g.shape[1]
    n_blk = n // nb
    tko, tn, tm = _tile(k, 1024, MXU), _tile(n_blk, 1536, MXU), _tile(m, 512, MXU)
    per = n_blk // tn
    nm = m // tm

    def kern(a_ref, g_ref, o_ref, acc):
        im = pl.program_id(2)

        @pl.when(im == 0)
        def _():
            acc[...] = jnp.zeros_like(acc)

        acc[...] += lax.dot_general(a_ref[...], g_ref[...], (((0,), (0,)), ((), ())), preferred_element_type=F32)

        @pl.when(im == nm - 1)
        def _():
            o_ref[...] = acc[...].astype(o_ref.dtype)

    return pl.pallas_call(
        kern, name=name, grid=(k // tko, nb * per, nm),
        in_specs=[pl.BlockSpec((tm, tko), lambda i, j, im: (im, i)),
                  pl.BlockSpec((tm, tn), lambda i, j, im: (im, j))],
        out_specs=pl.BlockSpec((None, tko, tn), lambda i, j, im: (j // per, i, j % per)),
        out_shape=jax.ShapeDtypeStruct((nb, k, n_blk), BF16),
        scratch_shapes=[pltpu.VMEM((tko, tn), F32)], compiler_params=_params(3))(a, g)


ROW_TILE = 128


def _rmsnorm_fwd(x, g, *, name):
    t, d = x.shape
    tr = _tile(t, ROW_TILE, SUBLANES)

    def kern(x_ref, g_ref, h_ref):
        xv = x_ref[...]
        r = lax.rsqrt(jnp.mean(xv * xv, axis=-1, keepdims=True) + EPS)
        h_ref[...] = (xv * r * g_ref[...]).astype(BF16)

    return pl.pallas_call(
        kern, name=name, grid=(t // tr,),
        in_specs=[pl.BlockSpec((tr, d), lambda i: (i, 0)), pl.BlockSpec((1, d), lambda i: (0, 0))],
        out_specs=pl.BlockSpec((tr, d), lambda i: (i, 0)),
        out_shape=jax.ShapeDtypeStruct((t, d), BF16), compiler_params=_params(1))(x, g)


def _rmsnorm_bwd(x, g, d_h, d_res, *, name):
    t, d = x.shape
    tr = _tile(t, ROW_TILE, SUBLANES)

    def kern(x_ref, g_ref, dh_ref, dres_ref, dx_ref, dxb_ref, dg_ref):
        xv = x_ref[...]
        r = lax.rsqrt(jnp.mean(xv * xv, axis=-1, keepdims=True) + EPS)
        dh = dh_ref[...]
        gy = dh * g_ref[...]
        dx = dres_ref[...] + r * gy - xv * (r * r * r) * jnp.mean(gy * xv, axis=-1, keepdims=True)
        dx_ref[...] = dx
        dxb_ref[...] = dx.astype(BF16)

        @pl.when(pl.program_id(0) == 0)
        def _():
            dg_ref[...] = jnp.zeros_like(dg_ref)

        dg_ref[...] += jnp.sum(dh * (xv * r), axis=0, keepdims=True)

    row = pl.BlockSpec((tr, d), lambda i: (i, 0))
    vec = pl.BlockSpec((1, d), lambda i: (0, 0))
    return pl.pallas_call(
        kern, name=name, grid=(t // tr,), in_specs=[row, vec, row, row], out_specs=[row, row, vec],
        out_shape=[jax.ShapeDtypeStruct((t, d), F32), jax.ShapeDtypeStruct((t, d), BF16),
                   jax.ShapeDtypeStruct((1, d), F32)], compiler_params=_params(1))(x, g, d_h, d_res)


def _loss_head(x, g, target, *, name):
    t, d = x.shape
    tr = _tile(t, ROW_TILE, SUBLANES)

    def kern(x_ref, g_ref, t_ref, dx_ref, dxb_ref, dg_ref, loss_ref):
        xv = x_ref[...]
        gv = g_ref[...]
        r = lax.rsqrt(jnp.mean(xv * xv, axis=-1, keepdims=True) + EPS)
        diff = xv * r * gv - t_ref[...]
        dy = diff * (1.0 / d)
        gy = dy * gv
        dx = r * gy - xv * (r * r * r) * jnp.mean(gy * xv, axis=-1, keepdims=True)
        dx_ref[...] = dx
        dxb_ref[...] = dx.astype(BF16)

        @pl.when(pl.program_id(0) == 0)
        def _():
            dg_ref[...] = jnp.zeros_like(dg_ref)
            loss_ref[...] = jnp.zeros_like(loss_ref)

        dg_ref[...] += jnp.sum(dy * (xv * r), axis=0, keepdims=True)
        part = 0.5 * jnp.sum(jnp.mean(diff * diff, axis=-1, keepdims=True), axis=0, keepdims=True)
        loss_ref[...] += jnp.broadcast_to(part, loss_ref.shape)

    row = pl.BlockSpec((tr, d), lambda i: (i, 0))
    vec = pl.BlockSpec((1, d), lambda i: (0, 0))
    return pl.pallas_call(
        kern, name=name, grid=(t // tr,), in_specs=[row, vec, row],
        out_specs=[row, row, vec, pl.BlockSpec((1, LANES), lambda i: (0, 0))],
        out_shape=[jax.ShapeDtypeStruct((t, d), F32), jax.ShapeDtypeStruct((t, d), BF16),
                   jax.ShapeDtypeStruct((1, d), F32), jax.ShapeDtypeStruct((1, LANES), F32)],
        compiler_params=_params(1))(x, g, target)


def _merge_fwd(m_a, m_b, proj, gate_col, *, name):
    t, d = m_a.shape
    tr = _tile(t, ROW_TILE, SUBLANES)

    def kern(ma_ref, mb_ref, ga_ref, gb_ref, o_ref):
        o_ref[...] = (_sigmoid(ga_ref[...]) * ma_ref[...] + _sigmoid(gb_ref[...]) * mb_ref[...]).astype(BF16)

    row = pl.BlockSpec((tr, d), lambda i: (i, 0))
    return pl.pallas_call(
        kern, name=name, grid=(t // tr,),
        in_specs=[row, row, pl.BlockSpec((tr, d), lambda i: (i, gate_col)),
                  pl.BlockSpec((tr, d), lambda i: (i, gate_col + 1))],
        out_specs=row, out_shape=jax.ShapeDtypeStruct((t, d), BF16), compiler_params=_params(1))(m_a, m_b, proj, proj)


def _merge_bwd(d_merged, m_a, m_b, proj, gate_col, *, name):
    t, d = m_a.shape
    tr = _tile(t, ROW_TILE, SUBLANES)

    def kern(dm_ref, ma_ref, mb_ref, ga_ref, gb_ref, dma_ref, dmb_ref, dg_ref):
        dm = dm_ref[...]
        sa = _sigmoid(ga_ref[...])
        sb = _sigmoid(gb_ref[...])
        dma_ref[...] = (dm * sa).astype(BF16)
        dmb_ref[...] = (dm * sb).astype(BF16)
        dg_ref[:, 0:d] = (dm * ma_ref[...] * (sa * (1.0 - sa))).astype(BF16)
        dg_ref[:, d:2 * d] = (dm * mb_ref[...] * (sb * (1.0 - sb))).astype(BF16)

    row = pl.BlockSpec((tr, d), lambda i: (i, 0))
    return pl.pallas_call(
        kern, name=name, grid=(t // tr,),
        in_specs=[row, row, row, pl.BlockSpec((tr, d), lambda i: (i, gate_col)),
                  pl.BlockSpec((tr, d), lambda i: (i, gate_col + 1))],
        out_specs=[row, row, pl.BlockSpec((tr, 2 * d), lambda i: (i, 0))],
        out_shape=[jax.ShapeDtypeStruct((t, d), BF16), jax.ShapeDtypeStruct((t, d), BF16),
                   jax.ShapeDtypeStruct((t, 2 * d), BF16)], compiler_params=_params(1))(d_merged, m_a, m_b, proj, proj)


def _tril_bf16(w, transposed):
    row = lax.broadcasted_iota(jnp.int32, w.shape, 0)
    col = lax.broadcasted_iota(jnp.int32, w.shape, 1)
    keep = (row <= col) if transposed else (row >= col)
    return jnp.where(keep, w, 0.0).astype(BF16)


def _layernorm_stats(v):
    mu = jnp.mean(v, axis=-1, keepdims=True)
    vc = v - mu
    rstd = lax.rsqrt(jnp.mean(vc * vc, axis=-1, keepdims=True) + EPS)
    return vc * rstd, rstd


def _mixer_a_fwd(proj, ln_g, ln_b, sg_w, sg_b_t, sgw, *, name):
    t = proj.shape[0]
    gd = sgw // SG_GROUPS

    def kern(zu_ref, zv_ref, g_ref, b_ref, w_ref, bt_ref, o_ref):
        xhat, _ = _layernorm_stats(_gelu(zv_ref[...]))
        vn = (xhat * g_ref[...] + b_ref[...]).astype(BF16)
        for g in range(SG_GROUPS):
            cols = slice(g * gd, (g + 1) * gd)
            mixed = jnp.dot(_tril_bf16(w_ref[g], False), vn[:, cols], preferred_element_type=F32) + bt_ref[:, g:g + 1]
            o_ref[:, cols] = (_gelu(zu_ref[:, cols]) * mixed).astype(BF16)

    vec = pl.BlockSpec((1, sgw), lambda i: (0, 0))
    return pl.pallas_call(
        kern, name=name, grid=(t // CHUNK,),
        in_specs=[pl.BlockSpec((CHUNK, sgw), lambda i: (i, 0)), pl.BlockSpec((CHUNK, sgw), lambda i: (i, 1)), vec, vec,
                  pl.BlockSpec((SG_GROUPS, CHUNK, CHUNK), lambda i: (0, 0, 0)),
                  pl.BlockSpec((CHUNK, SG_GROUPS), lambda i: (0, 0))],
        out_specs=pl.BlockSpec((CHUNK, sgw), lambda i: (i, 0)),
        out_shape=jax.ShapeDtypeStruct((t, sgw), BF16), compiler_params=_params(1))(proj, proj, ln_g, ln_b, sg_w, sg_b_t)


def _mixer_a_bwd(proj, d_ya, ln_g, ln_b, sg_w, sg_w_t, sg_b_t, sgw, *, name):
    t = proj.shape[0]
    gd = sgw // SG_GROUPS

    def kern(zu_ref, zv_ref, dy_ref, g_ref, b_ref, w_ref, wt_ref, bt_ref, dz_ref, dw_ref, dbt_ref, dg_ref, db_ref, dvn):
        @pl.when(pl.program_id(0) == 0)
        def _():
            dw_ref[...] = jnp.zeros_like(dw_ref)
            dbt_ref[...] = jnp.zeros_like(dbt_ref)
            dg_ref[...] = jnp.zeros_like(dg_ref)
            db_ref[...] = jnp.zeros_like(db_ref)

        gv, dgv = _gelu_and_grad(zv_ref[...])
        xhat, rstd = _layernorm_stats(gv)
        ln_gain = g_ref[...]
        vn = (xhat * ln_gain + b_ref[...]).astype(BF16)
        for g in range(SG_GROUPS):
            cols = slice(g * gd, (g + 1) * gd)
            gu, dgu = _gelu_and_grad(zu_ref[:, cols])
            mixed = jnp.dot(_tril_bf16(w_ref[g], False), vn[:, cols], preferred_element_type=F32) + bt_ref[:, g:g + 1]
            dy = dy_ref[:, cols]
            dz_ref[:, cols] = (dy * mixed * dgu).astype(BF16)
            d_mixed = dy * gu
            d_mixed_b = d_mixed.astype(BF16)
            dvn[:, cols] = jnp.dot(_tril_bf16(wt_ref[g], True), d_mixed_b, preferred_element_type=F32)
            d_w = lax.dot_general(d_mixed_b, vn[:, cols], (((1,), (1,)), ((), ())), preferred_element_type=F32)
            row = lax.broadcasted_iota(jnp.int32, d_w.shape, 0)
            col = lax.broadcasted_iota(jnp.int32, d_w.shape, 1)
            dw_ref[g] += jnp.where(row >= col, d_w, 0.0)
            dbt_ref[:, g:g + 1] += jnp.sum(d_mixed, axis=-1, keepdims=True)
        d_vn = dvn[...]
        dg_ref[...] += jnp.sum(d_vn * xhat, axis=0, keepdims=True)
        db_ref[...] += jnp.sum(d_vn, axis=0, keepdims=True)
        d_xhat = d_vn * ln_gain
        d_gv = rstd * (d_xhat - jnp.mean(d_xhat, axis=-1, keepdims=True)
                       - xhat * jnp.mean(d_xhat * xhat, axis=-1, keepdims=True))
        dz_ref[:, sgw:2 * sgw] = (d_gv * dgv).astype(BF16)

    vec = pl.BlockSpec((1, sgw), lambda i: (0, 0))
    wspec = pl.BlockSpec((SG_GROUPS, CHUNK, CHUNK), lambda i: (0, 0, 0))
    btspec = pl.BlockSpec((CHUNK, SG_GROUPS), lambda i: (0, 0))
    return pl.pallas_call(
        kern, name=name, grid=(t // CHUNK,),
        in_specs=[pl.BlockSpec((CHUNK, sgw), lambda i: (i, 0)), pl.BlockSpec((CHUNK, sgw), lambda i: (i, 1)),
                  pl.BlockSpec((CHUNK, sgw), lambda i: (i, 0)), vec, vec, wspec, wspec, btspec],
        out_specs=[pl.BlockSpec((CHUNK, 2 * sgw), lambda i: (i, 0)), wspec, btspec, vec, vec],
        out_shape=[jax.ShapeDtypeStruct((t, 2 * sgw), BF16), jax.ShapeDtypeStruct((SG_GROUPS, CHUNK, CHUNK), F32),
                   jax.ShapeDtypeStruct((CHUNK, SG_GROUPS), F32), jax.ShapeDtypeStruct((1, sgw), F32),
                   jax.ShapeDtypeStruct((1, sgw), F32)],
        scratch_shapes=[pltpu.VMEM((CHUNK, sgw), F32)],
        compiler_params=_params(1))(proj, proj, d_ya, ln_g, ln_b, sg_w, sg_w_t, sg_b_t)


def _scan_fwd(a_ref, b_ref, o_ref):
    s, c = a_ref.shape

    def body(i, h):
        r0 = pl.multiple_of(i * SUBLANES, SUBLANES)
        a = a_ref[pl.ds(r0, SUBLANES), :]
        b = b_ref[pl.ds(r0, SUBLANES), :]
        rows = []
        for j in range(SUBLANES):
            h = a[j:j + 1, :] * h + b[j:j + 1, :]
            rows.append(h)
        o_ref[pl.ds(r0, SUBLANES), :] = jnp.concatenate(rows, axis=0)
        return h

    lax.fori_loop(0, s // SUBLANES, body, jnp.zeros((1, c), F32))


def _scan_bwd(a_next_ref, d_ref, o_ref):
    s, c = d_ref.shape
    nblk = s // SUBLANES

    def body(i, lam):
        r0 = pl.multiple_of((nblk - 1 - i) * SUBLANES, SUBLANES)
        a = a_next_ref[pl.ds(r0, SUBLANES), :]
        d = d_ref[pl.ds(r0, SUBLANES), :]
        rows = [None] * SUBLANES
        for j in reversed(range(SUBLANES)):
            lam = d[j:j + 1, :] + a[j:j + 1, :] * lam
            rows[j] = lam
        o_ref[pl.ds(r0, SUBLANES), :] = jnp.concatenate(rows, axis=0)
        return lam

    lax.fori_loop(0, nblk, body, jnp.zeros((1, c), F32))


def _lru_gates(xc, wa_ref, ba_ref, wx_ref, bx_ref, lam_ref):
    xcb = xc.astype(BF16)
    ra = _sigmoid(jnp.dot(xcb, wa_ref[...].astype(BF16), preferred_element_type=F32) + ba_ref[...])
    ia = _sigmoid(jnp.dot(xcb, wx_ref[...].astype(BF16), preferred_element_type=F32) + bx_ref[...])
    neg = -lam_ref[...]
    sp = jnp.maximum(neg, 0.0) + jnp.log1p(jnp.exp(-jnp.abs(neg)))
    log_a = -LRU_C * ra * sp
    a = jnp.exp(log_a)
    a2 = jnp.exp(2.0 * log_a)
    sq = jnp.sqrt(-jnp.tanh(log_a) * (a2 + 1.0))
    return ra, ia, sp, a, a2, sq


def _mixer_b_specs(seq, hd, sgw, lw):
    x_col = (2 * sgw) // hd
    y_col = (2 * sgw + lw) // hd
    tile = lambda col: pl.BlockSpec((seq, hd), lambda h, b: (b, col + h))
    vec = pl.BlockSpec((1, hd), lambda h, b: (0, h))
    mat = pl.BlockSpec((None, hd, hd), lambda h, b: (h, 0, 0))
    return tile(x_col), tile(y_col), tile(0), vec, mat


def _mixer_b_fwd(proj, conv_w, conv_b, wa, ba, wx, bx, lam, *, seq, sgw, lw, name):
    t = proj.shape[0]
    hd = lw // LRU_HEADS
    k_taps = conv_w.shape[0]
    x_spec, y_spec, o_spec, vec, mat = _mixer_b_specs(seq, hd, sgw, lw)

    def kern(xr_ref, yr_ref, cw_ref, cb_ref, wa_ref, ba_ref, wx_ref, bx_ref, lam_ref, o_ref, s_a, s_h):
        xc = _causal_conv(xr_ref[...], cw_ref[...], cb_ref[...])
        _, ia, _, a, _, sq = _lru_gates(xc, wa_ref, ba_ref, wx_ref, bx_ref, lam_ref)
        s_a[...] = a
        s_h[...] = sq * (ia * xc)
        _scan_fwd(s_a, s_h, s_h)
        o_ref[...] = (s_h[...] * _gelu(yr_ref[...])).astype(BF16)

    return pl.pallas_call(
        kern, name=name, grid=(LRU_HEADS, t // seq),
        in_specs=[x_spec, y_spec, pl.BlockSpec((k_taps, hd), lambda h, b: (0, h)), vec, mat, vec, mat, vec, vec],
        out_specs=o_spec, out_shape=jax.ShapeDtypeStruct((t, lw), BF16),
        scratch_shapes=[pltpu.VMEM((seq, hd), F32), pltpu.VMEM((seq, hd), F32)],
        compiler_params=_params(2))(proj, proj, conv_w, conv_b, wa, ba, wx, bx, lam)


def _mixer_b_bwd(proj, d_yb, conv_w, conv_b, wa, wa_t, ba, wx, wx_t, bx, lam, *, seq, sgw, lw, name):
    t = proj.shape[0]
    hd = lw // LRU_HEADS
    k_taps = conv_w.shape[0]
    x_spec, y_spec, o_spec, vec, mat = _mixer_b_specs(seq, hd, sgw, lw)
    cw_spec = pl.BlockSpec((k_taps, hd), lambda h, b: (0, h))

    def kern(xr_ref, yr_ref, dyb_ref, cw_ref, cb_ref, wa_ref, wat_ref, ba_ref, wx_ref, wxt_ref, bx_ref, lam_ref,
             dxr_ref, dyr_ref, dcw_ref, dcb_ref, dwa_ref, dba_ref, dwx_ref, dbx_ref, dlam_ref,
             s_xc, s_a, s_h, s_lam, s_dpa, s_dpx):
        @pl.when(pl.program_id(1) == 0)
        def _():
            for ref in (dcw_ref, dcb_ref, dwa_ref, dba_ref, dwx_ref, dbx_ref, dlam_ref):
                ref[...] = jnp.zeros_like(ref)

        s_xc[...] = _causal_conv(xr_ref[...], cw_ref[...], cb_ref[...])
        _, ia, _, a, _, sq = _lru_gates(s_xc[...], wa_ref, ba_ref, wx_ref, bx_ref, lam_ref)
        s_a[...] = a
        s_h[...] = sq * (ia * s_xc[...])
        _scan_fwd(s_a, s_h, s_h)

        gel, dgel = _gelu_and_grad(yr_ref[...])
        dyb = dyb_ref[...]
        dyr_ref[...] = (dyb * s_h[...] * dgel).astype(BF16)
        s_lam[...] = dyb * gel
        s_dpa[...] = _shift_up(s_a[...], 1)
        _scan_bwd(s_dpa, s_lam, s_lam)
        ra, ia, sp, a, a2, sq = _lru_gates(s_xc[...], wa_ref, ba_ref, wx_ref, bx_ref, lam_ref)
        d_gx = s_lam[...]
        d_a = d_gx * _shift_down(s_h[...], 1)
        xc = s_xc[...]
        d_sq = d_gx * (ia * xc)
        d_ia = d_gx * (sq * xc)
        d_log_a = d_a * a - d_sq * (a2 / sq)
        d_ra = d_log_a * (-LRU_C * sp)
        d_sp = jnp.sum(d_log_a * (-LRU_C * ra), axis=0, keepdims=True)
        dlam_ref[...] += d_sp * (-_sigmoid(-lam_ref[...]))
        d_pa = d_ra * (ra * (1.0 - ra))
        d_px = d_ia * (ia * (1.0 - ia))
        s_dpa[...] = d_pa
        s_dpx[...] = d_px
        dba_ref[...] += jnp.sum(d_pa, axis=0, keepdims=True)
        dbx_ref[...] += jnp.sum(d_px, axis=0, keepdims=True)
        xcb = s_xc[...].astype(BF16)
        d_pa_b = s_dpa[...].astype(BF16)
        d_px_b = s_dpx[...].astype(BF16)
        contract_rows = (((0,), (0,)), ((), ()))
        dwa_ref[...] += lax.dot_general(xcb, d_pa_b, contract_rows, preferred_element_type=F32)
        dwx_ref[...] += lax.dot_general(xcb, d_px_b, contract_rows, preferred_element_type=F32)
        d_xc = (s_lam[...] * (sq * ia)
                + jnp.dot(d_pa_b, wat_ref[...].astype(BF16), preferred_element_type=F32)
                + jnp.dot(d_px_b, wxt_ref[...].astype(BF16), preferred_element_type=F32))
        dcb_ref[...] += jnp.sum(d_xc, axis=0, keepdims=True)
        dcw_ref[...] += _causal_conv_bwd_w(d_xc, xr_ref[...], k_taps)
        dxr_ref[...] = _causal_conv_bwd_x(d_xc, cw_ref[...]).astype(BF16)

    tile_shape = jax.ShapeDtypeStruct((t, lw), BF16)
    vec_shape = jax.ShapeDtypeStruct((1, lw), F32)
    mat_shape = jax.ShapeDtypeStruct((LRU_HEADS, hd, hd), F32)
    return pl.pallas_call(
        kern, name=name, grid=(LRU_HEADS, t // seq),
        in_specs=[x_spec, y_spec, o_spec, cw_spec, vec, mat, mat, vec, mat, mat, vec, vec],
        out_specs=[o_spec, o_spec, cw_spec, vec, mat, vec, mat, vec, vec],
        out_shape=[tile_shape, tile_shape, jax.ShapeDtypeStruct((k_taps, lw), F32), vec_shape, mat_shape, vec_shape,
                   mat_shape, vec_shape, vec_shape],
        scratch_shapes=[pltpu.VMEM((seq, hd), F32)] * 6,
        compiler_params=_params(2))(proj, proj, d_yb, conv_w, conv_b, wa, wa_t, ba, wx, wx_t, bx, lam)


FFN_TILE = 256


def _ffn_mid_fwd(up_pre, conv_w, conv_b, *, seq, name):
    t, f2 = up_pre.shape
    f = f2 // 2
    tc = _tile(f, FFN_TILE, LANES)
    nf = f // tc
    k_taps = conv_w.shape[0]

    def kern(pg_ref, pv_ref, wg_ref, wv_ref, bg_ref, bv_ref, o_ref):
        cg = _causal_conv(pg_ref[...], wg_ref[...], bg_ref[...])
        cv = _causal_conv(pv_ref[...], wv_ref[...], bv_ref[...])
        o_ref[...] = (_gelu(cg) * cv).astype(BF16)

    tile = lambda off: pl.BlockSpec((seq, tc), lambda j, b: (b, off + j))
    wspec = lambda off: pl.BlockSpec((k_taps, tc), lambda j, b: (0, off + j))
    bspec = lambda off: pl.BlockSpec((1, tc), lambda j, b: (0, off + j))
    return pl.pallas_call(
        kern, name=name, grid=(nf, t // seq),
        in_specs=[tile(0), tile(nf), wspec(0), wspec(nf), bspec(0), bspec(nf)], out_specs=tile(0),
        out_shape=jax.ShapeDtypeStruct((t, f), BF16),
        compiler_params=_params(2))(up_pre, up_pre, conv_w, conv_w, conv_b, conv_b)


def _ffn_mid_bwd(up_pre, d_act, conv_w, conv_b, *, seq, name):
    t, f2 = up_pre.shape
    f = f2 // 2
    tc = _tile(f, FFN_TILE, LANES)
    nf = f // tc
    k_taps = conv_w.shape[0]

    def kern(pg_ref, pv_ref, da_ref, wg_ref, wv_ref, bg_ref, bv_ref, dpg_ref, dpv_ref, dwg_ref, dwv_ref, dbg_ref, dbv_ref):
        @pl.when(pl.program_id(1) == 0)
        def _():
            for ref in (dwg_ref, dwv_ref, dbg_ref, dbv_ref):
                ref[...] = jnp.zeros_like(ref)

        pg = pg_ref[...]
        pv = pv_ref[...]
        gel, dgel = _gelu_and_grad(_causal_conv(pg, wg_ref[...], bg_ref[...]))
        cv = _causal_conv(pv, wv_ref[...], bv_ref[...])
        d_act_v = da_ref[...]
        d_cg = d_act_v * cv * dgel
        d_cv = d_act_v * gel
        dpg_ref[...] = _causal_conv_bwd_x(d_cg, wg_ref[...]).astype(BF16)
        dpv_ref[...] = _causal_conv_bwd_x(d_cv, wv_ref[...]).astype(BF16)
        dwg_ref[...] += _causal_conv_bwd_w(d_cg, pg, k_taps)
        dwv_ref[...] += _causal_conv_bwd_w(d_cv, pv, k_taps)
        dbg_ref[...] += jnp.sum(d_cg, axis=0, keepdims=True)
        dbv_ref[...] += jnp.sum(d_cv, axis=0, keepdims=True)

    tile = lambda off: pl.BlockSpec((seq, tc), lambda j, b: (b, off + j))
    wspec = lambda off: pl.BlockSpec((k_taps, tc), lambda j, b: (0, off + j))
    bspec = lambda off: pl.BlockSpec((1, tc), lambda j, b: (0, off + j))
    half = jax.ShapeDtypeStruct((t, f), BF16)
    wshape = jax.ShapeDtypeStruct((k_taps, f), F32)
    bshape = jax.ShapeDtypeStruct((1, f), F32)
    d_pg, d_pv, d_wg, d_wv, d_bg, d_bv = pl.pallas_call(
        kern, name=name, grid=(nf, t // seq),
        in_specs=[tile(0), tile(nf), tile(0), wspec(0), wspec(nf), bspec(0), bspec(nf)],
        out_specs=[tile(0), tile(0), wspec(0), wspec(0), bspec(0), bspec(0)],
        out_shape=[half, half, wshape, wshape, bshape, bshape],
        compiler_params=_params(2))(up_pre, up_pre, d_act, conv_w, conv_w, conv_b, conv_b)
    return (jnp.concatenate([d_pg, d_pv], axis=1), jnp.concatenate([d_wg, d_wv], axis=1),
            jnp.concatenate([d_bg, d_bv], axis=1))


ELEM_TILE_BYTES = 1 << 20


def _as_2d(a):
    if a.ndim >= 2 and a.shape[-1] % LANES == 0 and a.size // a.shape[-1] >= SUBLANES:
        return a.reshape(-1, a.shape[-1])
    return a.reshape(-1, LANES)


def _row_tile(rows, cols):
    return _tile(rows, max(16, ELEM_TILE_BYTES // (4 * cols)), 16)


def _cast_bf16(a, *, name):
    v = _as_2d(a)
    rows, cols = v.shape
    tr = _row_tile(rows, cols)

    def kern(x_ref, o_ref):
        o_ref[...] = x_ref[...].astype(BF16)

    spec = pl.BlockSpec((tr, cols), lambda i: (i, 0))
    out = pl.pallas_call(kern, name=name, grid=(rows // tr,), in_specs=[spec], out_specs=spec,
                         out_shape=jax.ShapeDtypeStruct(v.shape, BF16), compiler_params=_params(1))(v)
    return out.reshape(a.shape)


def _add_sibling_part(own, core, got, *, name):
    _, _, rows, cols = own.shape
    tr = _row_tile(rows, cols)

    def kern(core_ref, a_ref, b_ref, o_ref):
        o_ref[...] = (a_ref[...].astype(F32) + b_ref[...].astype(F32)).astype(BF16)

    spec = pl.BlockSpec((None, tr, cols), lambda ch, i, core_ref: (ch, i, 0))
    grid_spec = pltpu.PrefetchScalarGridSpec(
        num_scalar_prefetch=1, grid=(4, rows // tr),
        in_specs=[pl.BlockSpec((None, None, tr, cols), lambda ch, i, core_ref: (ch, core_ref[0], i, 0)), spec],
        out_specs=spec)
    return pl.pallas_call(kern, name=name, grid_spec=grid_spec, out_shape=jax.ShapeDtypeStruct(got.shape, BF16),
                          compiler_params=_params(2))(core, own, got)


def _sum_parts(parts, *, name):
    n_parts, rows, cols = parts.shape
    tr = _row_tile(rows, cols * n_parts)

    def kern(p_ref, o_ref):
        acc = p_ref[0].astype(F32)
        for p in range(1, n_parts):
            acc = acc + p_ref[p].astype(F32)
        o_ref[...] = acc

    return pl.pallas_call(
        kern, name=name, grid=(rows // tr,), in_specs=[pl.BlockSpec((n_parts, tr, cols), lambda i: (0, i, 0))],
        out_specs=pl.BlockSpec((tr, cols), lambda i: (i, 0)), out_shape=jax.ShapeDtypeStruct((rows, cols), F32),
        compiler_params=_params(1))(parts)


def _adamw(w, m, v, grad_parts, *, name):
    shape = w.shape
    w2 = _as_2d(w)
    rows, cols = w2.shape
    n_parts = grad_parts.shape[0]
    parts = grad_parts.reshape(n_parts, rows, cols)
    tr = _row_tile(rows, cols)
    c_m = 1.0 - ADAM_B1 ** ADAM_STEP
    c_v = 1.0 - ADAM_B2 ** ADAM_STEP

    def kern(w_ref, m_ref, v_ref, p_ref, g_ref, d_ref, nm_ref, nv_ref):
        g = p_ref[0].astype(F32)
        for p in range(1, n_parts):
            g = g + p_ref[p].astype(F32)
        new_m = ADAM_B1 * m_ref[...] + (1.0 - ADAM_B1) * g
        new_v = ADAM_B2 * v_ref[...] + (1.0 - ADAM_B2) * (g * g)
        g_ref[...] = g
        nm_ref[...] = new_m
        nv_ref[...] = new_v
        d_ref[...] = -ADAM_LR * ((new_m / c_m) / (jnp.sqrt(new_v / c_v) + ADAM_EPS) + ADAM_WD * w_ref[...])

    spec = pl.BlockSpec((tr, cols), lambda i: (i, 0))
    out = jax.ShapeDtypeStruct((rows, cols), F32)
    res = pl.pallas_call(
        kern, name=name, grid=(rows // tr,),
        in_specs=[spec, spec, spec, pl.BlockSpec((n_parts, tr, cols), lambda i: (0, i, 0))],
        out_specs=[spec] * 4, out_shape=[out] * 4, compiler_params=_params(1))(w2, _as_2d(m), _as_2d(v), parts)
    return [r.reshape(shape) for r in res]


def _place():
    return lax.axis_index("x"), lax.axis_index("y"), lax.axis_index("c")


def _all_gather(shards, *, name):
    n = len(shards)

    def body(*refs):
        ins, outs = refs[:n], refs[n:2 * n]
        send_sems, recv_sems, local_sems = refs[2 * n:]
        x, y, c = _place()
        me, sibling = (x, y, c), (x, y, 1 - c)
        chips = [(1 - x, y), (x, 1 - y), (1 - x, 1 - y)]

        def slot(i, dev):
            return outs[i].at[4 * dev[0] + 2 * dev[1] + dev[2]]

        def copy(i, k, block, to, src=None):
            return pltpu.make_async_remote_copy(
                src_ref=slot(i, block) if src is None else src, dst_ref=slot(i, block),
                send_sem=send_sems.at[i, k], recv_sem=recv_sems.at[i, k], device_id=to, device_id_type=MESH)

        mine = [pltpu.make_async_copy(ins[i], slot(i, me), local_sems.at[i]) for i in range(n)]
        for cp in mine:
            cp.start()
        first = []
        for i in range(n):
            first.append(copy(i, 0, me, sibling, src=ins[i]))
            first += [copy(i, 1 + j, me, (*chip, c), src=ins[i]) for j, chip in enumerate(chips)]
        for cp in first:
            cp.start()
        passed = []
        for j, chip in enumerate(chips):
            for i in range(n):
                copy(i, 1 + j, (*chip, c), me).wait_recv()
                onward = copy(i, 4 + j, (*chip, c), sibling)
                onward.start()
                passed.append(onward)
        for i in range(n):
            copy(i, 0, sibling, me).wait_recv()
            for j, chip in enumerate(chips):
                copy(i, 4 + j, (*chip, 1 - c), me).wait_recv()
        for cp in first + passed:
            cp.wait_send()
        for cp in mine:
            cp.wait()

    return pl.pallas_call(
        body, name=name, in_specs=[ANY] * n, out_specs=[ANY] * n,
        out_shape=[jax.ShapeDtypeStruct((N_DEV,) + s.shape, s.dtype) for s in shards],
        scratch_shapes=[pltpu.SemaphoreType.DMA((n, 7)), pltpu.SemaphoreType.DMA((n, 7)), pltpu.SemaphoreType.DMA((n,))],
    )(*shards)


def _swap_with_sibling(parts, *, name):
    n = len(parts)

    def body(*refs):
        ins, outs = refs[:n], refs[n:2 * n]
        send_sems, recv_sems = refs[2 * n:]
        x, y, c = _place()
        copies = [pltpu.make_async_remote_copy(
            src_ref=ins[i].at[ch, 1 - c], dst_ref=outs[i].at[ch], send_sem=send_sems.at[i, ch],
            recv_sem=recv_sems.at[i, ch], device_id=(x, y, 1 - c), device_id_type=MESH)
            for i in range(n) for ch in range(4)]
        for cp in copies:
            cp.start()
        for cp in copies:
            cp.wait()

    return pl.pallas_call(
        body, name=name, in_specs=[ANY] * n, out_specs=[ANY] * n,
        out_shape=[jax.ShapeDtypeStruct((4,) + p.shape[2:], p.dtype) for p in parts],
        scratch_shapes=[pltpu.SemaphoreType.DMA((n, 4)), pltpu.SemaphoreType.DMA((n, 4))])(*parts)


def _swap_with_chips(parts, *, name):
    n = len(parts)

    def body(*refs):
        ins, outs = refs[:n], refs[n:2 * n]
        send_sems, recv_sems, local_sems = refs[2 * n:]
        x, y, c = _place()
        chips = [(1 - x, y), (x, 1 - y), (1 - x, 1 - y)]
        mine = [pltpu.make_async_copy(ins[i].at[2 * x + y], outs[i].at[0], local_sems.at[i]) for i in range(n)]
        for cp in mine:
            cp.start()
        copies = [pltpu.make_async_remote_copy(
            src_ref=ins[i].at[2 * chip[0] + chip[1]], dst_ref=outs[i].at[1 + j], send_sem=send_sems.at[i, j],
            recv_sem=recv_sems.at[i, j], device_id=(*chip, c), device_id_type=MESH)
            for i in range(n) for j, chip in enumerate(chips)]
        for cp in copies:
            cp.start()
        for cp in copies:
            cp.wait()
        for cp in mine:
            cp.wait()

    return pl.pallas_call(
        body, name=name, in_specs=[ANY] * n, out_specs=[ANY] * n,
        out_shape=[jax.ShapeDtypeStruct(p.shape, p.dtype) for p in parts],
        scratch_shapes=[pltpu.SemaphoreType.DMA((n, 3)), pltpu.SemaphoreType.DMA((n, 3)), pltpu.SemaphoreType.DMA((n,))],
    )(*parts)


def _reduce_scatter(grads):
    x, y, c = _place()
    core = jnp.reshape(c, (1,)).astype(jnp.int32)
    views = [g.reshape(4, 2, -1, g.shape[-1]) for g in grads]
    from_sibling = _swap_with_sibling(views, name="rs_sibling")
    chip_sums = [_add_sibling_part(v, core, s, name=f"rs_add_{i}") for i, (v, s) in enumerate(zip(views, from_sibling))]
    return _swap_with_chips(chip_sums, name="rs_chips")


def _pack(vectors):
    flat = [v.reshape(-1).astype(F32) for v in vectors]
    sizes = [f.shape[0] for f in flat]
    total = sum(sizes)
    padded = -(-total // (SUBLANES * LANES)) * (SUBLANES * LANES)
    if padded > total:
        flat.append(jnp.zeros((padded - total,), F32))
    return jnp.concatenate(flat).reshape(-1, LANES), sizes


def _unpack(packed, sizes, shapes):
    flat = packed.reshape(-1)
    out, off = [], 0
    for size, shape in zip(sizes, shapes):
        out.append(flat[off:off + size].reshape(shape))
        off += size
    return out


def kernel(x, g_mix, w_in, sg_ln_g, sg_ln_b, sg_w, sg_b, lru_conv_w, lru_conv_b, lru_wa, lru_ba, lru_wx, lru_bx, lru_lam, p_sg, p_lru, w_out, g_ffn, w_up, ffn_conv_w, ffn_conv_b, w_down, g_final, loss_target, m_g_mix, m_w_in, m_sg_ln_g, m_sg_ln_b, m_sg_w, m_sg_b, m_lru_conv_w, m_lru_conv_b, m_lru_wa, m_lru_ba, m_lru_wx, m_lru_bx, m_lru_lam, m_p_sg, m_p_lru, m_w_out, m_g_ffn, m_w_up, m_ffn_conv_w, m_ffn_conv_b, m_w_down, m_g_final, v_g_mix, v_w_in, v_sg_ln_g, v_sg_ln_b, v_sg_w, v_sg_b, v_lru_conv_w, v_lru_conv_b, v_lru_wa, v_lru_ba, v_lru_wx, v_lru_bx, v_lru_lam, v_p_sg, v_p_lru, v_w_out, v_g_ffn, v_w_up, v_ffn_conv_w, v_ffn_conv_b, v_w_down, v_g_final):
    weights = dict(g_mix=g_mix, w_in=w_in, sg_ln_g=sg_ln_g, sg_ln_b=sg_ln_b, sg_w=sg_w, sg_b=sg_b, lru_conv_w=lru_conv_w,
                   lru_conv_b=lru_conv_b, lru_wa=lru_wa, lru_ba=lru_ba, lru_wx=lru_wx, lru_bx=lru_bx, lru_lam=lru_lam,
                   p_sg=p_sg, p_lru=p_lru, w_out=w_out, g_ffn=g_ffn, w_up=w_up, ffn_conv_w=ffn_conv_w,
                   ffn_conv_b=ffn_conv_b, w_down=w_down, g_final=g_final)
    m_in = dict(g_mix=m_g_mix, w_in=m_w_in, sg_ln_g=m_sg_ln_g, sg_ln_b=m_sg_ln_b, sg_w=m_sg_w, sg_b=m_sg_b,
                lru_conv_w=m_lru_conv_w, lru_conv_b=m_lru_conv_b, lru_wa=m_lru_wa, lru_ba=m_lru_ba, lru_wx=m_lru_wx,
                lru_bx=m_lru_bx, lru_lam=m_lru_lam, p_sg=m_p_sg, p_lru=m_p_lru, w_out=m_w_out, g_ffn=m_g_ffn,
                w_up=m_w_up, ffn_conv_w=m_ffn_conv_w, ffn_conv_b=m_ffn_conv_b, w_down=m_w_down, g_final=m_g_final)
    v_in = dict(g_mix=v_g_mix, w_in=v_w_in, sg_ln_g=v_sg_ln_g, sg_ln_b=v_sg_ln_b, sg_w=v_sg_w, sg_b=v_sg_b,
                lru_conv_w=v_lru_conv_w, lru_conv_b=v_lru_conv_b, lru_wa=v_lru_wa, lru_ba=v_lru_ba, lru_wx=v_lru_wx,
                lru_bx=v_lru_bx, lru_lam=v_lru_lam, p_sg=v_p_sg, p_lru=v_p_lru, w_out=v_w_out, g_ffn=v_g_ffn,
                w_up=v_w_up, ffn_conv_w=v_ffn_conv_w, ffn_conv_b=v_ffn_conv_b, w_down=v_w_down, g_final=v_g_final)
    order = list(weights)

    n_seq, seq, d = x.shape
    t = n_seq * seq
    sgw = sg_ln_g.shape[-1]
    lw = lru_lam.shape[-1]
    hd = lw // LRU_HEADS
    f2 = ffn_conv_b.shape[-1]
    gate_col = (2 * sgw + 2 * lw) // d
    xi, yi, ci = _place()
    dev = 4 * xi + 2 * yi + ci

    big = ["w_in", "p_sg", "p_lru", "w_out", "w_up", "w_down", "lru_wa", "lru_wx"]
    shards = [_cast_bf16(weights[k][0], name=f"cast_{k}") for k in big]
    taps, tap_sizes = _pack([lru_conv_w[0], ffn_conv_w[0]])
    gathered = _all_gather(shards + [taps], name="gather_weights")
    gw = dict(zip(big, gathered[:-1]))
    w_in_g, p_sg_g, w_up_g = gw["w_in"], gw["p_sg"], gw["w_up"]
    p_lru_g, w_out_g, w_down_g = (gw[k].reshape(1, -1, d) for k in ("p_lru", "w_out", "w_down"))
    wa_g, wx_g = (jnp.swapaxes(gw[k], 0, 1).reshape(LRU_HEADS, hd, hd) for k in ("lru_wa", "lru_wx"))
    wa_t, wx_t = jnp.swapaxes(wa_g, 1, 2), jnp.swapaxes(wx_g, 1, 2)
    tap_parts = [_unpack(gathered[-1][k], tap_sizes, [lru_conv_w.shape[1:], ffn_conv_w.shape[1:]]) for k in range(N_DEV)]
    lru_cw = jnp.concatenate([p[0] for p in tap_parts], axis=1)
    ffn_cw = jnp.concatenate([p[1] for p in tap_parts], axis=1)
    sg_w0 = sg_w[0]
    sg_w_t = jnp.swapaxes(sg_w0, 1, 2)
    sg_b_t = sg_b[0].T

    x2d = x.reshape(t, d)
    h1 = _rmsnorm_fwd(x2d, g_mix, name="norm_mix")
    proj = _mm_nn(h1, w_in_g, out_dtype=F32, name="proj_in")
    y_a = _mixer_a_fwd(proj, sg_ln_g, sg_ln_b, sg_w0, sg_b_t, sgw, name="mixer_a_fwd")
    y_b = _mixer_b_fwd(proj, lru_cw, lru_conv_b, wa_g, lru_ba, wx_g, lru_bx, lru_lam, seq=seq, sgw=sgw, lw=lw,
                       name="mixer_b_fwd")
    m_a = _mm_nn(y_a, p_sg_g, out_dtype=F32, name="proj_sg")
    m_b = _mm_nn(y_b, p_lru_g, out_dtype=F32, name="proj_lru")
    merged = _merge_fwd(m_a, m_b, proj, gate_col, name="merge_fwd")
    x1 = _mm_nn(merged, w_out_g, out_dtype=F32, residual=x2d, name="proj_out")
    h2 = _rmsnorm_fwd(x1, g_ffn, name="norm_ffn")
    up_pre = _mm_nn(h2, w_up_g, out_dtype=F32, name="ffn_up")
    act = _ffn_mid_fwd(up_pre, ffn_cw, ffn_conv_b, seq=seq, name="ffn_mid_fwd")
    x2 = _mm_nn(act, w_down_g, out_dtype=F32, residual=x1, name="ffn_down")
    d_x2, d_x2_b, d_g_final, loss_part = _loss_head(x2, g_final.reshape(1, d), loss_target.reshape(t, d), name="loss_head")
    loss = lax.psum(loss_part[0, 0], ("x", "y", "c"))

    d_w_down = _mm_tn(act, d_x2_b, 1, name="grad_w_down")
    d_act = _mm_nt(d_x2_b, w_down_g, name="bwd_ffn_down")
    d_up_pre, d_ffn_cw, d_ffn_cb = _ffn_mid_bwd(up_pre, d_act, ffn_cw, ffn_conv_b, seq=seq, name="ffn_mid_bwd")
    d_w_up = _mm_tn(h2, d_up_pre, N_DEV, name="grad_w_up")
    d_h2 = _mm_nt(d_up_pre, w_up_g, name="bwd_ffn_up")
    d_x1, d_x1_b, d_g_ffn = _rmsnorm_bwd(x1, g_ffn, d_h2, d_x2, name="norm_ffn_bwd")
    d_w_out = _mm_tn(merged, d_x1_b, 1, name="grad_w_out")
    d_merged = _mm_nt(d_x1_b, w_out_g, name="bwd_proj_out")
    d_m_a, d_m_b, d_gates = _merge_bwd(d_merged, m_a, m_b, proj, gate_col, name="merge_bwd")
    d_p_sg = _mm_tn(y_a, d_m_a, N_DEV, name="grad_p_sg")
    d_p_lru = _mm_tn(y_b, d_m_b, 1, name="grad_p_lru")
    d_y_a = _mm_nt(d_m_a, p_sg_g, name="bwd_proj_sg")
    d_y_b = _mm_nt(d_m_b, p_lru_g, name="bwd_proj_lru")
    d_zuv, d_sg_w, d_sg_b_t, d_ln_g, d_ln_b = _mixer_a_bwd(proj, d_y_a, sg_ln_g, sg_ln_b, sg_w0, sg_w_t, sg_b_t, sgw,
                                                           name="mixer_a_bwd")
    d_xr, d_yr, d_lru_cw, d_lru_cb, d_wa, d_ba, d_wx, d_bx, d_lam = _mixer_b_bwd(
        proj, d_y_b, lru_cw, lru_conv_b, wa_g, wa_t, lru_ba, wx_g, wx_t, lru_bx, lru_lam, seq=seq, sgw=sgw, lw=lw,
        name="mixer_b_bwd")
    d_proj = jnp.concatenate([d_zuv, d_xr, d_yr, d_gates], axis=1)
    d_w_in = _mm_tn(h1, d_proj, N_DEV, name="grad_w_in")
    d_h1 = _mm_nt(d_proj, w_in_g, name="bwd_proj_in")
    grad_x, _, d_g_mix = _rmsnorm_bwd(x2d, g_mix, d_h1, d_x1, name="norm_mix_bwd")

    def by_rows(g):
        return g.reshape(N_DEV, -1, g.shape[-1])

    def by_head_rows(g):
        return jnp.swapaxes(g.reshape(LRU_HEADS, N_DEV, hd // N_DEV, hd), 0, 1)

    d_wa_b = _cast_bf16(by_head_rows(d_wa), name="cast_grad_wa")
    d_wx_b = _cast_bf16(by_head_rows(d_wx), name="cast_grad_wx")
    scattered = _reduce_scatter([d_w_in, d_p_sg, by_rows(d_p_lru), by_rows(d_w_out), d_w_up, by_rows(d_w_down),
                                 d_wa_b, d_wx_b])
    grad_parts = dict(zip(big, scattered))

    small = ["g_mix", "sg_ln_g", "sg_ln_b", "sg_w", "sg_b", "lru_conv_b", "lru_ba", "lru_bx", "lru_lam", "g_ffn",
             "ffn_conv_b", "g_final", "lru_conv_w", "ffn_conv_w"]
    small_parts = dict(g_mix=d_g_mix, sg_ln_g=d_ln_g, sg_ln_b=d_ln_b, sg_w=d_sg_w, sg_b=d_sg_b_t.T, lru_conv_b=d_lru_cb,
                       lru_ba=d_ba, lru_bx=d_bx, lru_lam=d_lam, g_ffn=d_g_ffn, ffn_conv_b=d_ffn_cb, g_final=d_g_final,
                       lru_conv_w=d_lru_cw, ffn_conv_w=d_ffn_cw)
    packed, sizes = _pack([small_parts[k] for k in small])
    (all_small,) = _all_gather([packed], name="gather_small_grads")
    small_sum = _sum_parts(all_small, name="sum_small_grads")
    small_grads = dict(zip(small, _unpack(small_sum, sizes, [small_parts[k].shape for k in small])))
    for k in ("lru_conv_w", "ffn_conv_w"):
        n_loc = weights[k].shape[-1]
        small_grads[k] = lax.dynamic_slice_in_dim(small_grads[k], dev * n_loc, n_loc, axis=1)

    grads, deltas, new_m, new_v = {}, {}, {}, {}
    for k in order:
        w = weights[k]
        if k in grad_parts:
            parts = grad_parts[k].reshape((4,) + w.shape)
        else:
            parts = small_grads[k].reshape((1,) + w.shape)
        grads[k], deltas[k], new_m[k], new_v[k] = _adamw(w, m_in[k], v_in[k], parts, name=f"adamw_{k}")

    return (loss, grad_x.reshape(x.shape), *[grads[k] for k in order], *[deltas[k] for k in order],
            *[new_m[k] for k in order], *[new_v[k] for k in order])
```

```python
import functools
import math
from typing import Callable, NamedTuple

import jax
import jax.numpy as jnp
from jax import lax
from jax.experimental import pallas as pl
from jax.experimental.pallas import tpu as pltpu

F32 = jnp.float32
BF16 = jnp.bfloat16
MESH = pl.DeviceIdType.MESH
ANY = pl.BlockSpec(memory_space=pl.ANY)

N_DEV = 8
EPS = 1e-6
CHUNK = 128
SG_GROUPS = 8
LRU_HEADS = 16
LRU_C = 8.0
ADAM_LR = 0.001
ADAM_B1 = 0.9
ADAM_B2 = 0.999
ADAM_EPS = 1e-08
ADAM_WD = 0.01
ADAM_STEP = 10

V7X_VMEM_LIMIT = 56 * 1024 * 1024
LANES = 128
SUBLANES = 8
MXU = 256

_GELU_C0 = math.sqrt(2.0 / math.pi)
_GELU_C1 = 0.044715


def _params(n_axes):
    return pltpu.CompilerParams(dimension_semantics=("arbitrary",) * n_axes, vmem_limit_bytes=V7X_VMEM_LIMIT)


def _tile(dim, pref, align):
    t = (min(pref, dim) // align) * align
    while t >= align:
        if dim % t == 0:
            return t
        t -= align
    return dim


def _gelu(x):
    return x * (0.5 * (1.0 + jnp.tanh(_GELU_C0 * (x + _GELU_C1 * (x * x * x)))))


def _gelu_and_grad(x):
    t = jnp.tanh(_GELU_C0 * (x + _GELU_C1 * (x * x * x)))
    cdf = 0.5 * (1.0 + t)
    dcdf = 0.5 * (1.0 - t * t) * (_GELU_C0 * (1.0 + 3.0 * _GELU_C1 * (x * x)))
    return x * cdf, cdf + x * dcdf


def _sigmoid(x):
    return 1.0 / (1.0 + jnp.exp(-x))


def _shift_down(x, d):
    if d == 0:
        return x
    row = lax.broadcasted_iota(jnp.int32, x.shape, 0)
    return jnp.where(row >= d, pltpu.roll(x, d, 0), 0.0)


def _shift_up(x, d):
    if d == 0:
        return x
    s = x.shape[0]
    row = lax.broadcasted_iota(jnp.int32, x.shape, 0)
    return jnp.where(row < s - d, pltpu.roll(x, s - d, 0), 0.0)


def _causal_conv(x, w, b):
    k_taps = w.shape[0]
    out = _shift_down(x, k_taps - 1) * w[0:1, :]
    for k in range(1, k_taps):
        out = out + _shift_down(x, k_taps - 1 - k) * w[k:k + 1, :]
    return out + b


def _causal_conv_bwd_x(d_out, w):
    k_taps = w.shape[0]
    d_x = _shift_up(d_out, k_taps - 1) * w[0:1, :]
    for k in range(1, k_taps):
        d_x = d_x + _shift_up(d_out, k_taps - 1 - k) * w[k:k + 1, :]
    return d_x


def _causal_conv_bwd_w(d_out, x, k_taps):
    rows = [jnp.sum(d_out * _shift_down(x, k_taps - 1 - k), axis=0, keepdims=True) for k in range(k_taps)]
    return jnp.concatenate(rows, axis=0)


def _place():
    return lax.axis_index("x"), lax.axis_index("y"), lax.axis_index("c")


def _other_chips(x, y):
    return [(1 - x, y), (x, 1 - y), (1 - x, 1 - y)]


class _Exchange(NamedTuple):
    ins: tuple
    outs: tuple
    in_place: bool
    n_remote: int
    n_local: int
    copies: Callable


def _remote(src, dst, send_sem, recv_sem, to):
    return pltpu.make_async_remote_copy(src_ref=src, dst_ref=dst, send_sem=send_sem, recv_sem=recv_sem, device_id=to,
                                        device_id_type=MESH)


def _gather_first_leg(shards):
    n = len(shards)

    def copies(ins, outs, send_sems, recv_sems, local_sems):
        x, y, c = _place()
        peers = [(x, y, 1 - c)] + [(*chip, c) for chip in _other_chips(x, y)]
        slot = lambda i, dev: outs[i].at[4 * dev[0] + 2 * dev[1] + dev[2]]
        sends = [_remote(ins[i], slot(i, (x, y, c)), send_sems.at[i, k], recv_sems.at[i, k], to)
                 for i in range(n) for k, to in enumerate(peers)]
        receives = [_remote(ins[i], slot(i, frm), send_sems.at[i, k], recv_sems.at[i, k], frm)
                    for i in range(n) for k, frm in enumerate(peers)]
        local = [pltpu.make_async_copy(ins[i], slot(i, (x, y, c)), local_sems.at[i, 0]) for i in range(n)]
        return sends, receives, local

    outs = tuple(jax.ShapeDtypeStruct((N_DEV,) + s.shape, s.dtype) for s in shards)
    return _Exchange(tuple(shards), outs, False, 4, 1, copies)


def _gather_second_leg(gathered):
    n = len(gathered)

    def copies(ins, outs, send_sems, recv_sems):
        x, y, c = _place()
        slot = lambda i, chip, core: outs[i].at[4 * chip[0] + 2 * chip[1] + core]
        sends = [_remote(slot(i, chip, c), slot(i, chip, c), send_sems.at[i, j], recv_sems.at[i, j], (x, y, 1 - c))
                 for i in range(n) for j, chip in enumerate(_other_chips(x, y))]
        receives = [_remote(slot(i, chip, 1 - c), slot(i, chip, 1 - c), send_sems.at[i, j], recv_sems.at[i, j], (x, y, 1 - c))
                    for i in range(n) for j, chip in enumerate(_other_chips(x, y))]
        return sends, receives, []

    outs = tuple(jax.ShapeDtypeStruct(g.shape, g.dtype) for g in gathered)
    return _Exchange(tuple(gathered), outs, True, 3, 0, copies)


def _swap_with_sibling(parts):
    n = len(parts)

    def copies(ins, outs, send_sems, recv_sems):
        x, y, c = _place()
        both = [_remote(ins[i].at[ch, 1 - c], outs[i].at[ch], send_sems.at[i, ch], recv_sems.at[i, ch], (x, y, 1 - c))
                for i in range(n) for ch in range(4)]
        return both, both, []

    outs = tuple(jax.ShapeDtypeStruct((4,) + p.shape[2:], p.dtype) for p in parts)
    return _Exchange(tuple(parts), outs, False, 4, 0, copies)


def _swap_with_chips(parts):
    n = len(parts)

    def copies(ins, outs, send_sems, recv_sems, local_sems):
        x, y, c = _place()
        both = [_remote(ins[i].at[2 * chip[0] + chip[1]], outs[i].at[1 + j], send_sems.at[i, j], recv_sems.at[i, j], (*chip, c))
                for i in range(n) for j, chip in enumerate(_other_chips(x, y))]
        local = [pltpu.make_async_copy(ins[i].at[2 * x + y], outs[i].at[0], local_sems.at[i, 0]) for i in range(n)]
        return both, both, local

    outs = tuple(jax.ShapeDtypeStruct(p.shape, p.dtype) for p in parts)
    return _Exchange(tuple(parts), outs, False, 3, 1, copies)


def _exchange_plumbing(exchanges):
    operands = [a for ex in exchanges for a in ex.ins]
    results = [s for ex in exchanges for s in ex.outs]
    scratch, in_place, at = [], {}, 0
    for ex in exchanges:
        n = len(ex.ins)
        scratch += [pltpu.SemaphoreType.DMA((n, ex.n_remote))] * 2
        if ex.n_local:
            scratch.append(pltpu.SemaphoreType.DMA((n, ex.n_local)))
        if ex.in_place:
            in_place.update({at + i: at + i for i in range(n)})
        at += n

    def copies(in_refs, out_refs, sem_refs):
        sends, receives, local = [], [], []
        at, sem_at = 0, 0
        for ex in exchanges:
            n, n_sem = len(ex.ins), 3 if ex.n_local else 2
            s, r, l = ex.copies(in_refs[at:at + n], out_refs[at:at + n], *sem_refs[sem_at:sem_at + n_sem])
            sends, receives, local = sends + s, receives + r, local + l
            at, sem_at = at + n, sem_at + n_sem
        return sends, receives, local

    return operands, results, scratch, in_place, copies


def _start_all(copies):
    sends, _, local = copies
    for cp in local + sends:
        cp.start()


def _wait_all(copies):
    sends, receives, local = copies
    for cp in receives:
        cp.wait_recv()
    for cp in sends:
        cp.wait_send()
    for cp in local:
        cp.wait()


def _exchange_now(exchanges, *, name):
    operands, results, scratch, in_place, copies = _exchange_plumbing(exchanges)
    n = len(operands)

    def body(*refs):
        made = copies(refs[:n], refs[n:2 * n], refs[2 * n:])
        _start_all(made)
        _wait_all(made)

    out = pl.pallas_call(body, name=name, in_specs=[ANY] * n, out_specs=[ANY] * n, out_shape=results,
                         scratch_shapes=scratch, input_output_aliases=in_place)(*operands)
    return _split(out, exchanges)


def _split(flat, exchanges):
    out, at = [], 0
    for ex in exchanges:
        out.append(list(flat[at:at + len(ex.ins)]))
        at += len(ex.ins)
    return out


def _pallas(kern, *, name, grid, in_specs, out_specs, out_shape, operands, scratch_shapes=(), exchanges=()):
    ex_operands, ex_results, ex_scratch, in_place, copies = _exchange_plumbing(exchanges)
    n_in, n_out, n_scratch, n_ex = len(in_specs), len(out_specs), len(scratch_shapes), len(ex_operands)

    def body(*refs):
        ins, refs = refs[:n_in], refs[n_in:]
        ex_ins, refs = refs[:n_ex], refs[n_ex:]
        outs, refs = refs[:n_out], refs[n_out:]
        ex_outs, refs = refs[:n_ex], refs[n_ex:]
        scratch, sems = refs[:n_scratch], refs[n_scratch:]
        if exchanges:
            first = functools.reduce(jnp.logical_and, [pl.program_id(a) == 0 for a in range(len(grid))])
            last = functools.reduce(jnp.logical_and, [pl.program_id(a) == g - 1 for a, g in enumerate(grid)])

            @pl.when(first)
            def _():
                _start_all(copies(ex_ins, ex_outs, sems))

        kern(*ins, *outs, *scratch)
        if exchanges:
            @pl.when(last)
            def _():
                _wait_all(copies(ex_ins, ex_outs, sems))

    res = pl.pallas_call(
        body, name=name, grid=grid, in_specs=list(in_specs) + [ANY] * n_ex, out_specs=list(out_specs) + [ANY] * n_ex,
        out_shape=list(out_shape) + ex_results, scratch_shapes=list(scratch_shapes) + ex_scratch,
        input_output_aliases={n_in + i: n_out + o for i, o in in_place.items()},
        compiler_params=_params(len(grid)))(*operands, *ex_operands)
    return list(res[:n_out]), _split(res[n_out:], exchanges)


def _accumulate(step, n_steps, acc, value, finish):
    if n_steps == 1:
        finish(value)
        return

    @pl.when(step == 0)
    def _():
        acc[0][...] = value

    @pl.when(step > 0)
    def _():
        acc[0][...] += value

    @pl.when(step == n_steps - 1)
    def _():
        finish(acc[0][...])


def _mm_nn(a, w, *, out_dtype, name, residual=None, exchanges=()):
    m, k = a.shape
    nb, _, n_blk = w.shape
    tm, tn, tk = _tile(m, 512, MXU), _tile(n_blk, 1536, MXU), _tile(k, 4096, MXU)
    per = n_blk // tn
    nk = k // tk

    def kern(*refs):
        a_ref, w_ref = refs[:2]
        r_ref = None if residual is None else refs[2]
        o_ref, acc = refs[2 + (residual is not None)], refs[3 + (residual is not None):]

        def finish(total):
            o_ref[...] = (total if r_ref is None else total + r_ref[...]).astype(o_ref.dtype)

        _accumulate(pl.program_id(2), nk, acc, jnp.dot(a_ref[...], w_ref[...], preferred_element_type=F32), finish)

    tile = pl.BlockSpec((tm, tn), lambda j, i, kk: (i, j))
    in_specs = [pl.BlockSpec((tm, tk), lambda j, i, kk: (i, kk)),
                pl.BlockSpec((None, tk, tn), lambda j, i, kk: (j // per, kk, j % per))]
    operands = [a, w]
    if residual is not None:
        in_specs.append(tile)
        operands.append(residual)
    (out,), carried = _pallas(
        kern, name=name, grid=(nb * per, m // tm, nk), in_specs=in_specs, out_specs=[tile],
        out_shape=[jax.ShapeDtypeStruct((m, nb * n_blk), out_dtype)], operands=operands,
        scratch_shapes=[pltpu.VMEM((tm, tn), F32)] * (nk > 1), exchanges=exchanges)
    return out, carried


def _mm_nt(g, w, *, name, exchanges=()):
    m, n = g.shape
    nb, k, n_blk = w.shape
    tm, tko, tn = _tile(m, 1024, MXU), _tile(k, 1024, MXU), _tile(n_blk, 3072, MXU)
    per = n_blk // tn
    nn = n // tn

    def kern(g_ref, w_ref, o_ref, *acc):
        def finish(total):
            o_ref[...] = total

        part = lax.dot_general(g_ref[...], w_ref[...], (((1,), (1,)), ((), ())), preferred_element_type=F32)
        _accumulate(pl.program_id(2), nn, acc, part, finish)

    (out,), carried = _pallas(
        kern, name=name, grid=(k // tko, m // tm, nn),
        in_specs=[pl.BlockSpec((tm, tn), lambda j, i, jn: (i, jn)),
                  pl.BlockSpec((None, tko, tn), lambda j, i, jn: (jn // per, j, jn % per))],
        out_specs=[pl.BlockSpec((tm, tko), lambda j, i, jn: (i, j))],
        out_shape=[jax.ShapeDtypeStruct((m, k), F32)], operands=[g, w],
        scratch_shapes=[pltpu.VMEM((tm, tko), F32)] * (nn > 1), exchanges=exchanges)
    return out, carried


def _mm_tn(a, g, nb, *, name, exchanges=()):
    m, k = a.shape
    n = g.shape[1]
    n_blk = n // nb
    tko, tn, tm = _tile(k, 512, MXU), _tile(n_blk, 1536, MXU), _tile(m, 4096, MXU)
    per = n_blk // tn
    nm = m // tm

    def kern(a_ref, g_ref, o_ref, *acc):
        def finish(total):
            o_ref[...] = total.astype(o_ref.dtype)

        part = lax.dot_general(a_ref[...], g_ref[...], (((0,), (0,)), ((), ())), preferred_element_type=F32)
        _accumulate(pl.program_id(2), nm, acc, part, finish)

    (out,), carried = _pallas(
        kern, name=name, grid=(nb * per, k // tko, nm),
        in_specs=[pl.BlockSpec((tm, tko), lambda j, i, im: (im, i)),
                  pl.BlockSpec((tm, tn), lambda j, i, im: (im, j))],
        out_specs=[pl.BlockSpec((None, tko, tn), lambda j, i, im: (j // per, i, j % per))],
        out_shape=[jax.ShapeDtypeStruct((nb, k, n_blk), BF16)], operands=[a, g],
        scratch_shapes=[pltpu.VMEM((tko, tn), F32)] * (nm > 1), exchanges=exchanges)
    return out, carried


ROW_TILE = 128


def _rmsnorm_fwd(x, g, *, name):
    t, d = x.shape
    tr = _tile(t, ROW_TILE, SUBLANES)

    def kern(x_ref, g_ref, h_ref):
        xv = x_ref[...]
        r = lax.rsqrt(jnp.mean(xv * xv, axis=-1, keepdims=True) + EPS)
        h_ref[...] = (xv * r * g_ref[...]).astype(BF16)

    return pl.pallas_call(
        kern, name=name, grid=(t // tr,),
        in_specs=[pl.BlockSpec((tr, d), lambda i: (i, 0)), pl.BlockSpec((1, d), lambda i: (0, 0))],
        out_specs=pl.BlockSpec((tr, d), lambda i: (i, 0)),
        out_shape=jax.ShapeDtypeStruct((t, d), BF16), compiler_params=_params(1))(x, g)


def _rmsnorm_bwd(x, g, d_h, d_res, *, name):
    t, d = x.shape
    tr = _tile(t, ROW_TILE, SUBLANES)

    def kern(x_ref, g_ref, dh_ref, dres_ref, dx_ref, dxb_ref, dg_ref):
        xv = x_ref[...]
        r = lax.rsqrt(jnp.mean(xv * xv, axis=-1, keepdims=True) + EPS)
        dh = dh_ref[...]
        gy = dh * g_ref[...]
        dx = dres_ref[...] + r * gy - xv * (r * r * r) * jnp.mean(gy * xv, axis=-1, keepdims=True)
        dx_ref[...] = dx
        dxb_ref[...] = dx.astype(BF16)

        @pl.when(pl.program_id(0) == 0)
        def _():
            dg_ref[...] = jnp.zeros_like(dg_ref)

        dg_ref[...] += jnp.sum(dh * (xv * r), axis=0, keepdims=True)

    row = pl.BlockSpec((tr, d), lambda i: (i, 0))
    vec = pl.BlockSpec((1, d), lambda i: (0, 0))
    return pl.pallas_call(
        kern, name=name, grid=(t // tr,), in_specs=[row, vec, row, row], out_specs=[row, row, vec],
        out_shape=[jax.ShapeDtypeStruct((t, d), F32), jax.ShapeDtypeStruct((t, d), BF16),
                   jax.ShapeDtypeStruct((1, d), F32)], compiler_params=_params(1))(x, g, d_h, d_res)


def _loss_head(x, g, target, *, name):
    t, d = x.shape
    tr = _tile(t, ROW_TILE, SUBLANES)

    def kern(x_ref, g_ref, t_ref, dx_ref, dxb_ref, dg_ref, loss_ref):
        xv = x_ref[...]
        gv = g_ref[...]
        r = lax.rsqrt(jnp.mean(xv * xv, axis=-1, keepdims=True) + EPS)
        diff = xv * r * gv - t_ref[...]
        dy = diff * (1.0 / d)
        gy = dy * gv
        dx = r * gy - xv * (r * r * r) * jnp.mean(gy * xv, axis=-1, keepdims=True)
        dx_ref[...] = dx
        dxb_ref[...] = dx.astype(BF16)

        @pl.when(pl.program_id(0) == 0)
        def _():
            dg_ref[...] = jnp.zeros_like(dg_ref)
            loss_ref[...] = jnp.zeros_like(loss_ref)

        dg_ref[...] += jnp.sum(dy * (xv * r), axis=0, keepdims=True)
        part = 0.5 * jnp.sum(jnp.mean(diff * diff, axis=-1, keepdims=True), axis=0, keepdims=True)
        loss_ref[...] += jnp.broadcast_to(part, loss_ref.shape)

    row = pl.BlockSpec((tr, d), lambda i: (i, 0))
    vec = pl.BlockSpec((1, d), lambda i: (0, 0))
    return pl.pallas_call(
        kern, name=name, grid=(t // tr,), in_specs=[row, vec, row],
        out_specs=[row, row, vec, pl.BlockSpec((1, LANES), lambda i: (0, 0))],
        out_shape=[jax.ShapeDtypeStruct((t, d), F32), jax.ShapeDtypeStruct((t, d), BF16),
                   jax.ShapeDtypeStruct((1, d), F32), jax.ShapeDtypeStruct((1, LANES), F32)],
        compiler_params=_params(1))(x, g, target)


def _merge_fwd(m_a, m_b, proj, gate_col, *, name):
    t, d = m_a.shape
    tr = _tile(t, ROW_TILE, SUBLANES)

    def kern(ma_ref, mb_ref, ga_ref, gb_ref, o_ref):
        o_ref[...] = (_sigmoid(ga_ref[...]) * ma_ref[...] + _sigmoid(gb_ref[...]) * mb_ref[...]).astype(BF16)

    row = pl.BlockSpec((tr, d), lambda i: (i, 0))
    return pl.pallas_call(
        kern, name=name, grid=(t // tr,),
        in_specs=[row, row, pl.BlockSpec((tr, d), lambda i: (i, gate_col)),
                  pl.BlockSpec((tr, d), lambda i: (i, gate_col + 1))],
        out_specs=row, out_shape=jax.ShapeDtypeStruct((t, d), BF16), compiler_params=_params(1))(m_a, m_b, proj, proj)


def _merge_bwd(d_merged, m_a, m_b, proj, gate_col, *, name):
    t, d = m_a.shape
    tr = _tile(t, ROW_TILE, SUBLANES)

    def kern(dm_ref, ma_ref, mb_ref, ga_ref, gb_ref, dma_ref, dmb_ref, dg_ref):
        dm = dm_ref[...]
        sa = _sigmoid(ga_ref[...])
        sb = _sigmoid(gb_ref[...])
        dma_ref[...] = (dm * sa).astype(BF16)
        dmb_ref[...] = (dm * sb).astype(BF16)
        dg_ref[:, 0:d] = (dm * ma_ref[...] * (sa * (1.0 - sa))).astype(BF16)
        dg_ref[:, d:2 * d] = (dm * mb_ref[...] * (sb * (1.0 - sb))).astype(BF16)

    row = pl.BlockSpec((tr, d), lambda i: (i, 0))
    return pl.pallas_call(
        kern, name=name, grid=(t // tr,),
        in_specs=[row, row, row, pl.BlockSpec((tr, d), lambda i: (i, gate_col)),
                  pl.BlockSpec((tr, d), lambda i: (i, gate_col + 1))],
        out_specs=[row, row, pl.BlockSpec((tr, 2 * d), lambda i: (i, 0))],
        out_shape=[jax.ShapeDtypeStruct((t, d), BF16), jax.ShapeDtypeStruct((t, d), BF16),
                   jax.ShapeDtypeStruct((t, 2 * d), BF16)], compiler_params=_params(1))(d_merged, m_a, m_b, proj, proj)


def _tril_bf16(w, transposed):
    row = lax.broadcasted_iota(jnp.int32, w.shape, 0)
    col = lax.broadcasted_iota(jnp.int32, w.shape, 1)
    keep = (row <= col) if transposed else (row >= col)
    return jnp.where(keep, w, 0.0).astype(BF16)


def _layernorm_stats(v):
    mu = jnp.mean(v, axis=-1, keepdims=True)
    vc = v - mu
    rstd = lax.rsqrt(jnp.mean(vc * vc, axis=-1, keepdims=True) + EPS)
    return vc * rstd, rstd


def _mixer_a_fwd(proj, ln_g, ln_b, sg_w, sg_b_t, sgw, *, name):
    t = proj.shape[0]
    gd = sgw // SG_GROUPS

    def kern(zu_ref, zv_ref, g_ref, b_ref, w_ref, bt_ref, o_ref):
        xhat, _ = _layernorm_stats(_gelu(zv_ref[...]))
        vn = (xhat * g_ref[...] + b_ref[...]).astype(BF16)
        for g in range(SG_GROUPS):
            cols = slice(g * gd, (g + 1) * gd)
            mixed = jnp.dot(_tril_bf16(w_ref[g], False), vn[:, cols], preferred_element_type=F32) + bt_ref[:, g:g + 1]
            o_ref[:, cols] = (_gelu(zu_ref[:, cols]) * mixed).astype(BF16)

    vec = pl.BlockSpec((1, sgw), lambda i: (0, 0))
    return pl.pallas_call(
        kern, name=name, grid=(t // CHUNK,),
        in_specs=[pl.BlockSpec((CHUNK, sgw), lambda i: (i, 0)), pl.BlockSpec((CHUNK, sgw), lambda i: (i, 1)), vec, vec,
                  pl.BlockSpec((SG_GROUPS, CHUNK, CHUNK), lambda i: (0, 0, 0)),
                  pl.BlockSpec((CHUNK, SG_GROUPS), lambda i: (0, 0))],
        out_specs=pl.BlockSpec((CHUNK, sgw), lambda i: (i, 0)),
        out_shape=jax.ShapeDtypeStruct((t, sgw), BF16), compiler_params=_params(1))(proj, proj, ln_g, ln_b, sg_w, sg_b_t)


def _mixer_a_bwd(proj, d_ya, ln_g, ln_b, sg_w, sg_w_t, sg_b_t, sgw, *, name):
    t = proj.shape[0]
    gd = sgw // SG_GROUPS

    def kern(zu_ref, zv_ref, dy_ref, g_ref, b_ref, w_ref, wt_ref, bt_ref, dz_ref, dw_ref, dbt_ref, dg_ref, db_ref, dvn):
        @pl.when(pl.program_id(0) == 0)
        def _():
            dw_ref[...] = jnp.zeros_like(dw_ref)
            dbt_ref[...] = jnp.zeros_like(dbt_ref)
            dg_ref[...] = jnp.zeros_like(dg_ref)
            db_ref[...] = jnp.zeros_like(db_ref)

        gv, dgv = _gelu_and_grad(zv_ref[...])
        xhat, rstd = _layernorm_stats(gv)
        ln_gain = g_ref[...]
        vn = (xhat * ln_gain + b_ref[...]).astype(BF16)
        for g in range(SG_GROUPS):
            cols = slice(g * gd, (g + 1) * gd)
            gu, dgu = _gelu_and_grad(zu_ref[:, cols])
            mixed = jnp.dot(_tril_bf16(w_ref[g], False), vn[:, cols], preferred_element_type=F32) + bt_ref[:, g:g + 1]
            dy = dy_ref[:, cols]
            dz_ref[:, cols] = (dy * mixed * dgu).astype(BF16)
            d_mixed = dy * gu
            d_mixed_b = d_mixed.astype(BF16)
            dvn[:, cols] = jnp.dot(_tril_bf16(wt_ref[g], True), d_mixed_b, preferred_element_type=F32)
            d_w = lax.dot_general(d_mixed_b, vn[:, cols], (((1,), (1,)), ((), ())), preferred_element_type=F32)
            row = lax.broadcasted_iota(jnp.int32, d_w.shape, 0)
            col = lax.broadcasted_iota(jnp.int32, d_w.shape, 1)
            dw_ref[g] += jnp.where(row >= col, d_w, 0.0)
            dbt_ref[:, g:g + 1] += jnp.sum(d_mixed, axis=-1, keepdims=True)
        d_vn = dvn[...]
        dg_ref[...] += jnp.sum(d_vn * xhat, axis=0, keepdims=True)
        db_ref[...] += jnp.sum(d_vn, axis=0, keepdims=True)
        d_xhat = d_vn * ln_gain
        d_gv = rstd * (d_xhat - jnp.mean(d_xhat, axis=-1, keepdims=True)
                       - xhat * jnp.mean(d_xhat * xhat, axis=-1, keepdims=True))
        dz_ref[:, sgw:2 * sgw] = (d_gv * dgv).astype(BF16)

    vec = pl.BlockSpec((1, sgw), lambda i: (0, 0))
    wspec = pl.BlockSpec((SG_GROUPS, CHUNK, CHUNK), lambda i: (0, 0, 0))
    btspec = pl.BlockSpec((CHUNK, SG_GROUPS), lambda i: (0, 0))
    return pl.pallas_call(
        kern, name=name, grid=(t // CHUNK,),
        in_specs=[pl.BlockSpec((CHUNK, sgw), lambda i: (i, 0)), pl.BlockSpec((CHUNK, sgw), lambda i: (i, 1)),
                  pl.BlockSpec((CHUNK, sgw), lambda i: (i, 0)), vec, vec, wspec, wspec, btspec],
        out_specs=[pl.BlockSpec((CHUNK, 2 * sgw), lambda i: (i, 0)), wspec, btspec, vec, vec],
        out_shape=[jax.ShapeDtypeStruct((t, 2 * sgw), BF16), jax.ShapeDtypeStruct((SG_GROUPS, CHUNK, CHUNK), F32),
                   jax.ShapeDtypeStruct((CHUNK, SG_GROUPS), F32), jax.ShapeDtypeStruct((1, sgw), F32),
                   jax.ShapeDtypeStruct((1, sgw), F32)],
        scratch_shapes=[pltpu.VMEM((CHUNK, sgw), F32)],
        compiler_params=_params(1))(proj, proj, d_ya, ln_g, ln_b, sg_w, sg_w_t, sg_b_t)


def _scan_fwd(a_ref, b_ref, o_ref):
    s, c = a_ref.shape

    def body(i, h):
        r0 = pl.multiple_of(i * SUBLANES, SUBLANES)
        a = a_ref[pl.ds(r0, SUBLANES), :]
        b = b_ref[pl.ds(r0, SUBLANES), :]
        rows = []
        for j in range(SUBLANES):
            h = a[j:j + 1, :] * h + b[j:j + 1, :]
            rows.append(h)
        o_ref[pl.ds(r0, SUBLANES), :] = jnp.concatenate(rows, axis=0)
        return h

    lax.fori_loop(0, s // SUBLANES, body, jnp.zeros((1, c), F32))


def _scan_bwd(a_next_ref, d_ref, o_ref):
    s, c = d_ref.shape
    nblk = s // SUBLANES

    def body(i, lam):
        r0 = pl.multiple_of((nblk - 1 - i) * SUBLANES, SUBLANES)
        a = a_next_ref[pl.ds(r0, SUBLANES), :]
        d = d_ref[pl.ds(r0, SUBLANES), :]
        rows = [None] * SUBLANES
        for j in reversed(range(SUBLANES)):
            lam = d[j:j + 1, :] + a[j:j + 1, :] * lam
            rows[j] = lam
        o_ref[pl.ds(r0, SUBLANES), :] = jnp.concatenate(rows, axis=0)
        return lam

    lax.fori_loop(0, nblk, body, jnp.zeros((1, c), F32))


def _lru_gates(xc, wa_ref, ba_ref, wx_ref, bx_ref, lam_ref):
    xcb = xc.astype(BF16)
    ra = _sigmoid(jnp.dot(xcb, wa_ref[...].astype(BF16), preferred_element_type=F32) + ba_ref[...])
    ia = _sigmoid(jnp.dot(xcb, wx_ref[...].astype(BF16), preferred_element_type=F32) + bx_ref[...])
    neg = -lam_ref[...]
    sp = jnp.maximum(neg, 0.0) + jnp.log1p(jnp.exp(-jnp.abs(neg)))
    log_a = -LRU_C * ra * sp
    a = jnp.exp(log_a)
    a2 = jnp.exp(2.0 * log_a)
    sq = jnp.sqrt(-jnp.tanh(log_a) * (a2 + 1.0))
    return ra, ia, sp, a, a2, sq


def _mixer_b_specs(seq, hd, sgw, lw):
    x_col = (2 * sgw) // hd
    y_col = (2 * sgw + lw) // hd
    tile = lambda col: pl.BlockSpec((seq, hd), lambda h, b: (b, col + h))
    vec = pl.BlockSpec((1, hd), lambda h, b: (0, h))
    mat = pl.BlockSpec((None, hd, hd), lambda h, b: (h, 0, 0))
    return tile(x_col), tile(y_col), tile(0), vec, mat


def _mixer_b_fwd(proj, conv_w, conv_b, wa, ba, wx, bx, lam, *, seq, sgw, lw, name):
    t = proj.shape[0]
    hd = lw // LRU_HEADS
    k_taps = conv_w.shape[0]
    x_spec, y_spec, o_spec, vec, mat = _mixer_b_specs(seq, hd, sgw, lw)

    def kern(xr_ref, yr_ref, cw_ref, cb_ref, wa_ref, ba_ref, wx_ref, bx_ref, lam_ref, o_ref, s_a, s_h):
        xc = _causal_conv(xr_ref[...], cw_ref[...], cb_ref[...])
        _, ia, _, a, _, sq = _lru_gates(xc, wa_ref, ba_ref, wx_ref, bx_ref, lam_ref)
        s_a[...] = a
        s_h[...] = sq * (ia * xc)
        _scan_fwd(s_a, s_h, s_h)
        o_ref[...] = (s_h[...] * _gelu(yr_ref[...])).astype(BF16)

    return pl.pallas_call(
        kern, name=name, grid=(LRU_HEADS, t // seq),
        in_specs=[x_spec, y_spec, pl.BlockSpec((k_taps, hd), lambda h, b: (0, h)), vec, mat, vec, mat, vec, vec],
        out_specs=o_spec, out_shape=jax.ShapeDtypeStruct((t, lw), BF16),
        scratch_shapes=[pltpu.VMEM((seq, hd), F32), pltpu.VMEM((seq, hd), F32)],
        compiler_params=_params(2))(proj, proj, conv_w, conv_b, wa, ba, wx, bx, lam)


def _mixer_b_bwd(proj, d_yb, conv_w, conv_b, wa, wa_t, ba, wx, wx_t, bx, lam, *, seq, sgw, lw, name, exchanges=()):
    t = proj.shape[0]
    hd = lw // LRU_HEADS
    k_taps = conv_w.shape[0]
    x_spec, y_spec, o_spec, vec, mat = _mixer_b_specs(seq, hd, sgw, lw)
    cw_spec = pl.BlockSpec((k_taps, hd), lambda h, b: (0, h))

    def kern(xr_ref, yr_ref, dyb_ref, cw_ref, cb_ref, wa_ref, wat_ref, ba_ref, wx_ref, wxt_ref, bx_ref, lam_ref,
             dxr_ref, dyr_ref, dcw_ref, dcb_ref, dwa_ref, dba_ref, dwx_ref, dbx_ref, dlam_ref,
             s_xc, s_a, s_h, s_lam, s_dpa, s_dpx):
        @pl.when(pl.program_id(1) == 0)
        def _():
            for ref in (dcw_ref, dcb_ref, dwa_ref, dba_ref, dwx_ref, dbx_ref, dlam_ref):
                ref[...] = jnp.zeros_like(ref)

        s_xc[...] = _causal_conv(xr_ref[...], cw_ref[...], cb_ref[...])
        _, ia, _, a, _, sq = _lru_gates(s_xc[...], wa_ref, ba_ref, wx_ref, bx_ref, lam_ref)
        s_a[...] = a
        s_h[...] = sq * (ia * s_xc[...])
        _scan_fwd(s_a, s_h, s_h)

        gel, dgel = _gelu_and_grad(yr_ref[...])
        dyb = dyb_ref[...]
        dyr_ref[...] = (dyb * s_h[...] * dgel).astype(BF16)
        s_lam[...] = dyb * gel
        s_dpa[...] = _shift_up(s_a[...], 1)
        _scan_bwd(s_dpa, s_lam, s_lam)
        ra, ia, sp, a, a2, sq = _lru_gates(s_xc[...], wa_ref, ba_ref, wx_ref, bx_ref, lam_ref)
        d_gx = s_lam[...]
        d_a = d_gx * _shift_down(s_h[...], 1)
        xc = s_xc[...]
        d_sq = d_gx * (ia * xc)
        d_ia = d_gx * (sq * xc)
        d_log_a = d_a * a - d_sq * (a2 / sq)
        d_ra = d_log_a * (-LRU_C * sp)
        d_sp = jnp.sum(d_log_a * (-LRU_C * ra), axis=0, keepdims=True)
        dlam_ref[...] += d_sp * (-_sigmoid(-lam_ref[...]))
        d_pa = d_ra * (ra * (1.0 - ra))
        d_px = d_ia * (ia * (1.0 - ia))
        s_dpa[...] = d_pa
        s_dpx[...] = d_px
        dba_ref[...] += jnp.sum(d_pa, axis=0, keepdims=True)
        dbx_ref[...] += jnp.sum(d_px, axis=0, keepdims=True)
        xcb = s_xc[...].astype(BF16)
        d_pa_b = s_dpa[...].astype(BF16)
        d_px_b = s_dpx[...].astype(BF16)
        contract_rows = (((0,), (0,)), ((), ()))
        dwa_ref[...] += lax.dot_general(xcb, d_pa_b, contract_rows, preferred_element_type=F32)
        dwx_ref[...] += lax.dot_general(xcb, d_px_b, contract_rows, preferred_element_type=F32)
        d_xc = (s_lam[...] * (sq * ia)
                + jnp.dot(d_pa_b, wat_ref[...].astype(BF16), preferred_element_type=F32)
                + jnp.dot(d_px_b, wxt_ref[...].astype(BF16), preferred_element_type=F32))
        dcb_ref[...] += jnp.sum(d_xc, axis=0, keepdims=True)
        dcw_ref[...] += _causal_conv_bwd_w(d_xc, xr_ref[...], k_taps)
        dxr_ref[...] = _causal_conv_bwd_x(d_xc, cw_ref[...]).astype(BF16)

    tile_shape = jax.ShapeDtypeStruct((t, lw), BF16)
    vec_shape = jax.ShapeDtypeStruct((1, lw), F32)
    mat_shape = jax.ShapeDtypeStruct((LRU_HEADS, hd, hd), F32)
    return _pallas(
        kern, name=name, grid=(LRU_HEADS, t // seq),
        in_specs=[x_spec, y_spec, o_spec, cw_spec, vec, mat, mat, vec, mat, mat, vec, vec],
        out_specs=[o_spec, o_spec, cw_spec, vec, mat, vec, mat, vec, vec],
        out_shape=[tile_shape, tile_shape, jax.ShapeDtypeStruct((k_taps, lw), F32), vec_shape, mat_shape, vec_shape,
                   mat_shape, vec_shape, vec_shape],
        operands=[proj, proj, d_yb, conv_w, conv_b, wa, wa_t, ba, wx, wx_t, bx, lam],
        scratch_shapes=[pltpu.VMEM((seq, hd), F32)] * 6, exchanges=exchanges)


FFN_TILE = 256


def _ffn_mid_fwd(up_pre, conv_w, conv_b, *, seq, name):
    t, f2 = up_pre.shape
    f = f2 // 2
    tc = _tile(f, FFN_TILE, LANES)
    nf = f // tc
    k_taps = conv_w.shape[0]

    def kern(pg_ref, pv_ref, wg_ref, wv_ref, bg_ref, bv_ref, o_ref):
        cg = _causal_conv(pg_ref[...], wg_ref[...], bg_ref[...])
        cv = _causal_conv(pv_ref[...], wv_ref[...], bv_ref[...])
        o_ref[...] = (_gelu(cg) * cv).astype(BF16)

    tile = lambda off: pl.BlockSpec((seq, tc), lambda j, b: (b, off + j))
    wspec = lambda off: pl.BlockSpec((k_taps, tc), lambda j, b: (0, off + j))
    bspec = lambda off: pl.BlockSpec((1, tc), lambda j, b: (0, off + j))
    return pl.pallas_call(
        kern, name=name, grid=(nf, t // seq),
        in_specs=[tile(0), tile(nf), wspec(0), wspec(nf), bspec(0), bspec(nf)], out_specs=tile(0),
        out_shape=jax.ShapeDtypeStruct((t, f), BF16),
        compiler_params=_params(2))(up_pre, up_pre, conv_w, conv_w, conv_b, conv_b)


def _ffn_mid_bwd(up_pre, d_act, conv_w, conv_b, *, seq, name):
    t, f2 = up_pre.shape
    f = f2 // 2
    tc = _tile(f, FFN_TILE, LANES)
    nf = f // tc
    k_taps = conv_w.shape[0]

    def kern(pg_ref, pv_ref, da_ref, wg_ref, wv_ref, bg_ref, bv_ref, dpg_ref, dpv_ref, dwg_ref, dwv_ref, dbg_ref, dbv_ref):
        @pl.when(pl.program_id(1) == 0)
        def _():
            for ref in (dwg_ref, dwv_ref, dbg_ref, dbv_ref):
                ref[...] = jnp.zeros_like(ref)

        pg = pg_ref[...]
        pv = pv_ref[...]
        gel, dgel = _gelu_and_grad(_causal_conv(pg, wg_ref[...], bg_ref[...]))
        cv = _causal_conv(pv, wv_ref[...], bv_ref[...])
        d_act_v = da_ref[...]
        d_cg = d_act_v * cv * dgel
        d_cv = d_act_v * gel
        dpg_ref[...] = _causal_conv_bwd_x(d_cg, wg_ref[...]).astype(BF16)
        dpv_ref[...] = _causal_conv_bwd_x(d_cv, wv_ref[...]).astype(BF16)
        dwg_ref[...] += _causal_conv_bwd_w(d_cg, pg, k_taps)
        dwv_ref[...] += _causal_conv_bwd_w(d_cv, pv, k_taps)
        dbg_ref[...] += jnp.sum(d_cg, axis=0, keepdims=True)
        dbv_ref[...] += jnp.sum(d_cv, axis=0, keepdims=True)

    tile = lambda off: pl.BlockSpec((seq, tc), lambda j, b: (b, off + j))
    wspec = lambda off: pl.BlockSpec((k_taps, tc), lambda j, b: (0, off + j))
    bspec = lambda off: pl.BlockSpec((1, tc), lambda j, b: (0, off + j))
    half = jax.ShapeDtypeStruct((t, f), BF16)
    wshape = jax.ShapeDtypeStruct((k_taps, f), F32)
    bshape = jax.ShapeDtypeStruct((1, f), F32)
    d_pg, d_pv, d_wg, d_wv, d_bg, d_bv = pl.pallas_call(
        kern, name=name, grid=(nf, t // seq),
        in_specs=[tile(0), tile(nf), tile(0), wspec(0), wspec(nf), bspec(0), bspec(nf)],
        out_specs=[tile(0), tile(0), wspec(0), wspec(0), bspec(0), bspec(0)],
        out_shape=[half, half, wshape, wshape, bshape, bshape],
        compiler_params=_params(2))(up_pre, up_pre, d_act, conv_w, conv_w, conv_b, conv_b)
    return (jnp.concatenate([d_pg, d_pv], axis=1), jnp.concatenate([d_wg, d_wv], axis=1),
            jnp.concatenate([d_bg, d_bv], axis=1))


ELEM_TILE_BYTES = 1 << 20


def _as_2d(a):
    if a.ndim >= 2 and a.shape[-1] % LANES == 0 and a.size // a.shape[-1] >= SUBLANES:
        return a.reshape(-1, a.shape[-1])
    return a.reshape(-1, LANES)


def _row_tile(rows, cols):
    return _tile(rows, max(16, ELEM_TILE_BYTES // (4 * cols)), 16)


def _cast_bf16(a, *, name):
    v = _as_2d(a)
    rows, cols = v.shape
    tr = _row_tile(rows, cols)

    def kern(x_ref, o_ref):
        o_ref[...] = x_ref[...].astype(BF16)

    spec = pl.BlockSpec((tr, cols), lambda i: (i, 0))
    out = pl.pallas_call(kern, name=name, grid=(rows // tr,), in_specs=[spec], out_specs=spec,
                         out_shape=jax.ShapeDtypeStruct(v.shape, BF16), compiler_params=_params(1))(v)
    return out.reshape(a.shape)


def _add_sibling_part(own, core, got, *, name):
    _, _, rows, cols = own.shape
    tr = _row_tile(rows, cols)

    def kern(core_ref, a_ref, b_ref, o_ref):
        o_ref[...] = (a_ref[...].astype(F32) + b_ref[...].astype(F32)).astype(BF16)

    spec = pl.BlockSpec((None, tr, cols), lambda ch, i, core_ref: (ch, i, 0))
    grid_spec = pltpu.PrefetchScalarGridSpec(
        num_scalar_prefetch=1, grid=(4, rows // tr),
        in_specs=[pl.BlockSpec((None, None, tr, cols), lambda ch, i, core_ref: (ch, core_ref[0], i, 0)), spec],
        out_specs=spec)
    return pl.pallas_call(kern, name=name, grid_spec=grid_spec, out_shape=jax.ShapeDtypeStruct(got.shape, BF16),
                          compiler_params=_params(2))(core, own, got)


def _sum_parts(parts, *, name):
    n_parts, rows, cols = parts.shape
    tr = _row_tile(rows, cols * n_parts)

    def kern(p_ref, o_ref):
        acc = p_ref[0].astype(F32)
        for p in range(1, n_parts):
            acc = acc + p_ref[p].astype(F32)
        o_ref[...] = acc

    return pl.pallas_call(
        kern, name=name, grid=(rows // tr,), in_specs=[pl.BlockSpec((n_parts, tr, cols), lambda i: (0, i, 0))],
        out_specs=pl.BlockSpec((tr, cols), lambda i: (i, 0)), out_shape=jax.ShapeDtypeStruct((rows, cols), F32),
        compiler_params=_params(1))(parts)


def _adamw(w, m, v, grad_parts, *, name):
    shape = w.shape
    w2 = _as_2d(w)
    rows, cols = w2.shape
    n_parts = grad_parts.shape[0]
    parts = grad_parts.reshape(n_parts, rows, cols)
    tr = _row_tile(rows, cols)
    c_m = 1.0 - ADAM_B1 ** ADAM_STEP
    c_v = 1.0 - ADAM_B2 ** ADAM_STEP

    def kern(w_ref, m_ref, v_ref, p_ref, g_ref, d_ref, nm_ref, nv_ref):
        g = p_ref[0].astype(F32)
        for p in range(1, n_parts):
            g = g + p_ref[p].astype(F32)
        new_m = ADAM_B1 * m_ref[...] + (1.0 - ADAM_B1) * g
        new_v = ADAM_B2 * v_ref[...] + (1.0 - ADAM_B2) * (g * g)
        g_ref[...] = g
        nm_ref[...] = new_m
        nv_ref[...] = new_v
        d_ref[...] = -ADAM_LR * ((new_m / c_m) / (jnp.sqrt(new_v / c_v) + ADAM_EPS) + ADAM_WD * w_ref[...])

    spec = pl.BlockSpec((tr, cols), lambda i: (i, 0))
    out = jax.ShapeDtypeStruct((rows, cols), F32)
    res = pl.pallas_call(
        kern, name=name, grid=(rows // tr,),
        in_specs=[spec, spec, spec, pl.BlockSpec((n_parts, tr, cols), lambda i: (0, i, 0))],
        out_specs=[spec] * 4, out_shape=[out] * 4, compiler_params=_params(1))(w2, _as_2d(m), _as_2d(v), parts)
    return [r.reshape(shape) for r in res]


def _all_gather(shards, *, name):
    n = len(shards)

    def body(*refs):
        ins, outs = refs[:n], refs[n:2 * n]
        send_sems, recv_sems, local_sems = refs[2 * n:]
        x, y, c = _place()
        me, sibling = (x, y, c), (x, y, 1 - c)
        chips = [(1 - x, y), (x, 1 - y), (1 - x, 1 - y)]

        def slot(i, dev):
            return outs[i].at[4 * dev[0] + 2 * dev[1] + dev[2]]

        def copy(i, k, block, to, src=None):
            return pltpu.make_async_remote_copy(
                src_ref=slot(i, block) if src is None else src, dst_ref=slot(i, block),
                send_sem=send_sems.at[i, k], recv_sem=recv_sems.at[i, k], device_id=to, device_id_type=MESH)

        mine = [pltpu.make_async_copy(ins[i], slot(i, me), local_sems.at[i]) for i in range(n)]
        for cp in mine:
            cp.start()
        first = []
        for i in range(n):
            first.append(copy(i, 0, me, sibling, src=ins[i]))
            first += [copy(i, 1 + j, me, (*chip, c), src=ins[i]) for j, chip in enumerate(chips)]
        for cp in first:
            cp.start()
        passed = []
        for j, chip in enumerate(chips):
            for i in range(n):
                copy(i, 1 + j, (*chip, c), me).wait_recv()
                onward = copy(i, 4 + j, (*chip, c), sibling)
                onward.start()
                passed.append(onward)
        for i in range(n):
            copy(i, 0, sibling, me).wait_recv()
            for j, chip in enumerate(chips):
                copy(i, 4 + j, (*chip, 1 - c), me).wait_recv()
        for cp in first + passed:
            cp.wait_send()
        for cp in mine:
            cp.wait()

    return pl.pallas_call(
        body, name=name, in_specs=[ANY] * n, out_specs=[ANY] * n,
        out_shape=[jax.ShapeDtypeStruct((N_DEV,) + s.shape, s.dtype) for s in shards],
        scratch_shapes=[pltpu.SemaphoreType.DMA((n, 7)), pltpu.SemaphoreType.DMA((n, 7)), pltpu.SemaphoreType.DMA((n,))],
    )(*shards)


def _by_chip_and_core(grad):
    return grad.reshape(4, 2, -1, grad.shape[-1])


def _pack(vectors):
    flat = [v.reshape(-1).astype(F32) for v in vectors]
    sizes = [f.shape[0] for f in flat]
    total = sum(sizes)
    padded = -(-total // (SUBLANES * LANES)) * (SUBLANES * LANES)
    if padded > total:
        flat.append(jnp.zeros((padded - total,), F32))
    return jnp.concatenate(flat).reshape(-1, LANES), sizes


def _unpack(packed, sizes, shapes):
    flat = packed.reshape(-1)
    out, off = [], 0
    for size, shape in zip(sizes, shapes):
        out.append(flat[off:off + size].reshape(shape))
        off += size
    return out


def kernel(x, g_mix, w_in, sg_ln_g, sg_ln_b, sg_w, sg_b, lru_conv_w, lru_conv_b, lru_wa, lru_ba, lru_wx, lru_bx, lru_lam, p_sg, p_lru, w_out, g_ffn, w_up, ffn_conv_w, ffn_conv_b, w_down, g_final, loss_target, m_g_mix, m_w_in, m_sg_ln_g, m_sg_ln_b, m_sg_w, m_sg_b, m_lru_conv_w, m_lru_conv_b, m_lru_wa, m_lru_ba, m_lru_wx, m_lru_bx, m_lru_lam, m_p_sg, m_p_lru, m_w_out, m_g_ffn, m_w_up, m_ffn_conv_w, m_ffn_conv_b, m_w_down, m_g_final, v_g_mix, v_w_in, v_sg_ln_g, v_sg_ln_b, v_sg_w, v_sg_b, v_lru_conv_w, v_lru_conv_b, v_lru_wa, v_lru_ba, v_lru_wx, v_lru_bx, v_lru_lam, v_p_sg, v_p_lru, v_w_out, v_g_ffn, v_w_up, v_ffn_conv_w, v_ffn_conv_b, v_w_down, v_g_final):
    weights = dict(g_mix=g_mix, w_in=w_in, sg_ln_g=sg_ln_g, sg_ln_b=sg_ln_b, sg_w=sg_w, sg_b=sg_b, lru_conv_w=lru_conv_w,
                   lru_conv_b=lru_conv_b, lru_wa=lru_wa, lru_ba=lru_ba, lru_wx=lru_wx, lru_bx=lru_bx, lru_lam=lru_lam,
                   p_sg=p_sg, p_lru=p_lru, w_out=w_out, g_ffn=g_ffn, w_up=w_up, ffn_conv_w=ffn_conv_w,
                   ffn_conv_b=ffn_conv_b, w_down=w_down, g_final=g_final)
    m_in = dict(g_mix=m_g_mix, w_in=m_w_in, sg_ln_g=m_sg_ln_g, sg_ln_b=m_sg_ln_b, sg_w=m_sg_w, sg_b=m_sg_b,
                lru_conv_w=m_lru_conv_w, lru_conv_b=m_lru_conv_b, lru_wa=m_lru_wa, lru_ba=m_lru_ba, lru_wx=m_lru_wx,
                lru_bx=m_lru_bx, lru_lam=m_lru_lam, p_sg=m_p_sg, p_lru=m_p_lru, w_out=m_w_out, g_ffn=m_g_ffn,
                w_up=m_w_up, ffn_conv_w=m_ffn_conv_w, ffn_conv_b=m_ffn_conv_b, w_down=m_w_down, g_final=m_g_final)
    v_in = dict(g_mix=v_g_mix, w_in=v_w_in, sg_ln_g=v_sg_ln_g, sg_ln_b=v_sg_ln_b, sg_w=v_sg_w, sg_b=v_sg_b,
                lru_conv_w=v_lru_conv_w, lru_conv_b=v_lru_conv_b, lru_wa=v_lru_wa, lru_ba=v_lru_ba, lru_wx=v_lru_wx,
                lru_bx=v_lru_bx, lru_lam=v_lru_lam, p_sg=v_p_sg, p_lru=v_p_lru, w_out=v_w_out, g_ffn=v_g_ffn,
                w_up=v_w_up, ffn_conv_w=v_ffn_conv_w, ffn_conv_b=v_ffn_conv_b, w_down=v_w_down, g_final=v_g_final)
    order = list(weights)

    n_seq, seq, d = x.shape
    t = n_seq * seq
    sgw = sg_ln_g.shape[-1]
    lw = lru_lam.shape[-1]
    hd = lw // LRU_HEADS
    f2 = ffn_conv_b.shape[-1]
    gate_col = (2 * sgw + 2 * lw) // d
    xi, yi, ci = _place()
    dev = 4 * xi + 2 * yi + ci

    core = jnp.reshape(ci, (1,)).astype(jnp.int32)
    big = ["w_in", "p_sg", "p_lru", "w_out", "w_up", "w_down", "lru_wa", "lru_wx"]
    cast = {k: _cast_bf16(weights[k][0], name=f"cast_{k}") for k in big}
    taps, tap_sizes = _pack([lru_conv_w[0], ffn_conv_w[0]])
    at_once = ["w_in", "lru_wa", "lru_wx", "p_sg"]
    gathered = _all_gather([cast[k] for k in at_once] + [taps], name="gather_first_weights")
    w_in_g, wa_8, wx_8, p_sg_g = gathered[:-1]
    wa_g, wx_g = (jnp.swapaxes(w8, 0, 1).reshape(LRU_HEADS, hd, hd) for w8 in (wa_8, wx_8))
    wa_t, wx_t = jnp.swapaxes(wa_g, 1, 2), jnp.swapaxes(wx_g, 1, 2)
    tap_parts = [_unpack(gathered[-1][k], tap_sizes, [lru_conv_w.shape[1:], ffn_conv_w.shape[1:]]) for k in range(N_DEV)]
    lru_cw = jnp.concatenate([p[0] for p in tap_parts], axis=1)
    ffn_cw = jnp.concatenate([p[1] for p in tap_parts], axis=1)
    sg_w0 = sg_w[0]
    sg_w_t = jnp.swapaxes(sg_w0, 1, 2)
    sg_b_t = sg_b[0].T

    def rows_in_order(g8):
        return g8.reshape(1, -1, g8.shape[-1])

    x2d = x.reshape(t, d)
    h1 = _rmsnorm_fwd(x2d, g_mix, name="norm_mix")
    proj, (legs,) = _mm_nn(h1, w_in_g, out_dtype=F32, name="proj_in",
                           exchanges=[_gather_first_leg([cast["w_up"], cast["p_lru"], cast["w_out"]])])
    y_a = _mixer_a_fwd(proj, sg_ln_g, sg_ln_b, sg_w0, sg_b_t, sgw, name="mixer_a_fwd")
    y_b = _mixer_b_fwd(proj, lru_cw, lru_conv_b, wa_g, lru_ba, wx_g, lru_bx, lru_lam, seq=seq, sgw=sgw, lw=lw,
                       name="mixer_b_fwd")
    m_a, ((w_up_g, p_lru_8, w_out_8),) = _mm_nn(y_a, p_sg_g, out_dtype=F32, name="proj_sg",
                                                 exchanges=[_gather_second_leg(legs)])
    p_lru_g, w_out_g = rows_in_order(p_lru_8), rows_in_order(w_out_8)
    m_b, _ = _mm_nn(y_b, p_lru_g, out_dtype=F32, name="proj_lru")
    merged = _merge_fwd(m_a, m_b, proj, gate_col, name="merge_fwd")
    x1, _ = _mm_nn(merged, w_out_g, out_dtype=F32, residual=x2d, name="proj_out")
    h2 = _rmsnorm_fwd(x1, g_ffn, name="norm_ffn")
    up_pre, (leg,) = _mm_nn(h2, w_up_g, out_dtype=F32, name="ffn_up", exchanges=[_gather_first_leg([cast["w_down"]])])
    act = _ffn_mid_fwd(up_pre, ffn_cw, ffn_conv_b, seq=seq, name="ffn_mid_fwd")
    ((w_down_8,),) = _exchange_now([_gather_second_leg(leg)], name="gather_w_down_second_leg")
    w_down_g = rows_in_order(w_down_8)
    x2, _ = _mm_nn(act, w_down_g, out_dtype=F32, residual=x1, name="ffn_down")
    d_x2, d_x2_b, d_g_final, loss_part = _loss_head(x2, g_final.reshape(1, d), loss_target.reshape(t, d), name="loss_head")
    loss = lax.psum(loss_part[0, 0], ("x", "y", "c"))

    def by_rows(g):
        return g.reshape(N_DEV, -1, g.shape[-1])

    def by_head_rows(g):
        return jnp.swapaxes(g.reshape(LRU_HEADS, N_DEV, hd // N_DEV, hd), 0, 1)

    def chip_sum(view, from_sibling, key):
        return _add_sibling_part(view, core, from_sibling, name=f"chip_sum_{key}")

    d_w_down, _ = _mm_tn(act, d_x2_b, 1, name="grad_w_down")
    v_down = _by_chip_and_core(by_rows(d_w_down))
    d_act, ((s_down,),) = _mm_nt(d_x2_b, w_down_g, name="bwd_ffn_down", exchanges=[_swap_with_sibling([v_down])])
    c_down = chip_sum(v_down, s_down, "w_down")
    d_up_pre, d_ffn_cw, d_ffn_cb = _ffn_mid_bwd(up_pre, d_act, ffn_cw, ffn_conv_b, seq=seq, name="ffn_mid_bwd")
    d_w_up, ((parts_down,),) = _mm_tn(h2, d_up_pre, N_DEV, name="grad_w_up", exchanges=[_swap_with_chips([c_down])])
    v_up = _by_chip_and_core(d_w_up)
    d_h2, ((s_up,),) = _mm_nt(d_up_pre, w_up_g, name="bwd_ffn_up", exchanges=[_swap_with_sibling([v_up])])
    c_up = chip_sum(v_up, s_up, "w_up")
    d_x1, d_x1_b, d_g_ffn = _rmsnorm_bwd(x1, g_ffn, d_h2, d_x2, name="norm_ffn_bwd")
    d_w_out, _ = _mm_tn(merged, d_x1_b, 1, name="grad_w_out")
    v_out = _by_chip_and_core(by_rows(d_w_out))
    d_merged, ((s_out,),) = _mm_nt(d_x1_b, w_out_g, name="bwd_proj_out", exchanges=[_swap_with_sibling([v_out])])
    d_m_a, d_m_b, d_gates = _merge_bwd(d_merged, m_a, m_b, proj, gate_col, name="merge_bwd")
    d_p_sg, _ = _mm_tn(y_a, d_m_a, N_DEV, name="grad_p_sg")
    d_p_lru, _ = _mm_tn(y_b, d_m_b, 1, name="grad_p_lru")
    v_sg, v_lru = _by_chip_and_core(d_p_sg), _by_chip_and_core(by_rows(d_p_lru))
    d_y_a, ((s_sg, s_lru),) = _mm_nt(d_m_a, p_sg_g, name="bwd_proj_sg", exchanges=[_swap_with_sibling([v_sg, v_lru])])
    d_y_b, _ = _mm_nt(d_m_b, p_lru_g, name="bwd_proj_lru")
    c_out, c_sg, c_lru = chip_sum(v_out, s_out, "w_out"), chip_sum(v_sg, s_sg, "p_sg"), chip_sum(v_lru, s_lru, "p_lru")
    d_zuv, d_sg_w, d_sg_b_t, d_ln_g, d_ln_b = _mixer_a_bwd(proj, d_y_a, sg_ln_g, sg_ln_b, sg_w0, sg_w_t, sg_b_t, sgw,
                                                           name="mixer_a_bwd")
    (d_xr, d_yr, d_lru_cw, d_lru_cb, d_wa, d_ba, d_wx, d_bx, d_lam), ((parts_up,),) = _mixer_b_bwd(
        proj, d_y_b, lru_cw, lru_conv_b, wa_g, wa_t, lru_ba, wx_g, wx_t, lru_bx, lru_lam, seq=seq, sgw=sgw, lw=lw,
        name="mixer_b_bwd", exchanges=[_swap_with_chips([c_up])])
    d_proj = jnp.concatenate([d_zuv, d_xr, d_yr, d_gates], axis=1)
    d_w_in, ((parts_out, parts_sg, parts_lru),) = _mm_tn(h1, d_proj, N_DEV, name="grad_w_in",
                                                         exchanges=[_swap_with_chips([c_out, c_sg, c_lru])])
    v_win = _by_chip_and_core(d_w_in)
    v_wa = _by_chip_and_core(_cast_bf16(by_head_rows(d_wa), name="cast_grad_wa"))
    v_wx = _by_chip_and_core(_cast_bf16(by_head_rows(d_wx), name="cast_grad_wx"))
    ((s_in, s_wa, s_wx),) = _exchange_now([_swap_with_sibling([v_win, v_wa, v_wx])], name="swap_sibling_w_in")
    c_in, c_wa, c_wx = chip_sum(v_win, s_in, "w_in"), chip_sum(v_wa, s_wa, "lru_wa"), chip_sum(v_wx, s_wx, "lru_wx")
    d_h1, ((parts_in, parts_wa, parts_wx),) = _mm_nt(d_proj, w_in_g, name="bwd_proj_in",
                                                     exchanges=[_swap_with_chips([c_in, c_wa, c_wx])])
    grad_x, _, d_g_mix = _rmsnorm_bwd(x2d, g_mix, d_h1, d_x1, name="norm_mix_bwd")
    grad_parts = dict(w_in=parts_in, p_sg=parts_sg, p_lru=parts_lru, w_out=parts_out, w_up=parts_up, w_down=parts_down,
                      lru_wa=parts_wa, lru_wx=parts_wx)

    small = ["g_mix", "sg_ln_g", "sg_ln_b", "sg_w", "sg_b", "lru_conv_b", "lru_ba", "lru_bx", "lru_lam", "g_ffn",
             "ffn_conv_b", "g_final", "lru_conv_w", "ffn_conv_w"]
    small_parts = dict(g_mix=d_g_mix, sg_ln_g=d_ln_g, sg_ln_b=d_ln_b, sg_w=d_sg_w, sg_b=d_sg_b_t.T, lru_conv_b=d_lru_cb,
                       lru_ba=d_ba, lru_bx=d_bx, lru_lam=d_lam, g_ffn=d_g_ffn, ffn_conv_b=d_ffn_cb, g_final=d_g_final,
                       lru_conv_w=d_lru_cw, ffn_conv_w=d_ffn_cw)
    packed, sizes = _pack([small_parts[k] for k in small])
    (all_small,) = _all_gather([packed], name="gather_small_grads")
    small_sum = _sum_parts(all_small, name="sum_small_grads")
    small_grads = dict(zip(small, _unpack(small_sum, sizes, [small_parts[k].shape for k in small])))
    for k in ("lru_conv_w", "ffn_conv_w"):
        n_loc = weights[k].shape[-1]
        small_grads[k] = lax.dynamic_slice_in_dim(small_grads[k], dev * n_loc, n_loc, axis=1)

    grads, deltas, new_m, new_v = {}, {}, {}, {}
    for k in order:
        w = weights[k]
        if k in grad_parts:
            parts = grad_parts[k].reshape((4,) + w.shape)
        else:
            parts = small_grads[k].reshape((1,) + w.shape)
        grads[k], deltas[k], new_m[k], new_v[k] = _adamw(w, m_in[k], v_in[k], parts, name=f"adamw_{k}")

    return (loss, grad_x.reshape(x.shape), *[grads[k] for k in order], *[deltas[k] for k in order],
            *[new_m[k] for k in order], *[new_v[k] for k in order])
```

```python
import functools
import math
from typing import Callable, NamedTuple

import jax
import jax.numpy as jnp
from jax import lax
from jax.experimental import pallas as pl
from jax.experimental.pallas import tpu as pltpu

F32 = jnp.float32
BF16 = jnp.bfloat16
MESH = pl.DeviceIdType.MESH
ANY = pl.BlockSpec(memory_space=pl.ANY)

N_DEV = 8
EPS = 1e-6
CHUNK = 128
SG_GROUPS = 8
LRU_HEADS = 16
LRU_C = 8.0
ADAM_LR = 0.001
ADAM_B1 = 0.9
ADAM_B2 = 0.999
ADAM_EPS = 1e-08
ADAM_WD = 0.01
ADAM_STEP = 10

V7X_VMEM_LIMIT = 56 * 1024 * 1024
LANES = 128
SUBLANES = 8
MXU = 256

_GELU_C0 = math.sqrt(2.0 / math.pi)
_GELU_C1 = 0.044715


def _params(n_axes):
    return pltpu.CompilerParams(dimension_semantics=("arbitrary",) * n_axes, vmem_limit_bytes=V7X_VMEM_LIMIT)


def _tile(dim, pref, align):
    t = (min(pref, dim) // align) * align
    while t >= align:
        if dim % t == 0:
            return t
        t -= align
    return dim


def _gelu(x):
    return x * (0.5 * (1.0 + jnp.tanh(_GELU_C0 * (x + _GELU_C1 * (x * x * x)))))


def _gelu_and_grad(x):
    t = jnp.tanh(_GELU_C0 * (x + _GELU_C1 * (x * x * x)))
    cdf = 0.5 * (1.0 + t)
    dcdf = 0.5 * (1.0 - t * t) * (_GELU_C0 * (1.0 + 3.0 * _GELU_C1 * (x * x)))
    return x * cdf, cdf + x * dcdf


def _sigmoid(x):
    return 1.0 / (1.0 + jnp.exp(-x))


def _shift_down(x, d):
    if d == 0:
        return x
    row = lax.broadcasted_iota(jnp.int32, x.shape, 0)
    return jnp.where(row >= d, pltpu.roll(x, d, 0), 0.0)


def _shift_up(x, d):
    if d == 0:
        return x
    s = x.shape[0]
    row = lax.broadcasted_iota(jnp.int32, x.shape, 0)
    return jnp.where(row < s - d, pltpu.roll(x, s - d, 0), 0.0)


def _causal_conv(x, w, b):
    k_taps = w.shape[0]
    out = _shift_down(x, k_taps - 1) * w[0:1, :]
    for k in range(1, k_taps):
        out = out + _shift_down(x, k_taps - 1 - k) * w[k:k + 1, :]
    return out + b


def _causal_conv_bwd_x(d_out, w):
    k_taps = w.shape[0]
    d_x = _shift_up(d_out, k_taps - 1) * w[0:1, :]
    for k in range(1, k_taps):
        d_x = d_x + _shift_up(d_out, k_taps - 1 - k) * w[k:k + 1, :]
    return d_x


def _causal_conv_bwd_w(d_out, x, k_taps):
    rows = [jnp.sum(d_out * _shift_down(x, k_taps - 1 - k), axis=0, keepdims=True) for k in range(k_taps)]
    return jnp.concatenate(rows, axis=0)


def _place():
    return lax.axis_index("x"), lax.axis_index("y"), lax.axis_index("c")


def _other_chips(x, y):
    return [(1 - x, y), (x, 1 - y), (1 - x, 1 - y)]


class _Exchange(NamedTuple):
    ins: tuple
    outs: tuple
    in_place: bool
    n_remote: int
    n_local: int
    copies: Callable


def _remote(src, dst, send_sem, recv_sem, to):
    return pltpu.make_async_remote_copy(src_ref=src, dst_ref=dst, send_sem=send_sem, recv_sem=recv_sem, device_id=to,
                                        device_id_type=MESH)


def _gather_first_leg(shards):
    n = len(shards)

    def copies(ins, outs, send_sems, recv_sems, local_sems):
        x, y, c = _place()
        peers = [(x, y, 1 - c)] + [(*chip, c) for chip in _other_chips(x, y)]
        slot = lambda i, dev: outs[i].at[4 * dev[0] + 2 * dev[1] + dev[2]]
        sends = [_remote(ins[i], slot(i, (x, y, c)), send_sems.at[i, k], recv_sems.at[i, k], to)
                 for i in range(n) for k, to in enumerate(peers)]
        receives = [_remote(ins[i], slot(i, frm), send_sems.at[i, k], recv_sems.at[i, k], frm)
                    for i in range(n) for k, frm in enumerate(peers)]
        local = [pltpu.make_async_copy(ins[i], slot(i, (x, y, c)), local_sems.at[i, 0]) for i in range(n)]
        return sends, receives, local

    outs = tuple(jax.ShapeDtypeStruct((N_DEV,) + s.shape, s.dtype) for s in shards)
    return _Exchange(tuple(shards), outs, False, 4, 1, copies)


def _gather_second_leg(gathered):
    n = len(gathered)

    def copies(ins, outs, send_sems, recv_sems):
        x, y, c = _place()
        slot = lambda i, chip, core: outs[i].at[4 * chip[0] + 2 * chip[1] + core]
        sends = [_remote(slot(i, chip, c), slot(i, chip, c), send_sems.at[i, j], recv_sems.at[i, j], (x, y, 1 - c))
                 for i in range(n) for j, chip in enumerate(_other_chips(x, y))]
        receives = [_remote(slot(i, chip, 1 - c), slot(i, chip, 1 - c), send_sems.at[i, j], recv_sems.at[i, j], (x, y, 1 - c))
                    for i in range(n) for j, chip in enumerate(_other_chips(x, y))]
        return sends, receives, []

    outs = tuple(jax.ShapeDtypeStruct(g.shape, g.dtype) for g in gathered)
    return _Exchange(tuple(gathered), outs, True, 3, 0, copies)


def _swap_with_sibling(parts):
    n = len(parts)

    def copies(ins, outs, send_sems, recv_sems):
        x, y, c = _place()
        both = [_remote(ins[i].at[ch, 1 - c], outs[i].at[ch], send_sems.at[i, ch], recv_sems.at[i, ch], (x, y, 1 - c))
                for i in range(n) for ch in range(4)]
        return both, both, []

    outs = tuple(jax.ShapeDtypeStruct((4,) + p.shape[2:], p.dtype) for p in parts)
    return _Exchange(tuple(parts), outs, False, 4, 0, copies)


def _swap_with_chips(parts):
    n = len(parts)

    def copies(ins, outs, send_sems, recv_sems, local_sems):
        x, y, c = _place()
        both = [_remote(ins[i].at[2 * chip[0] + chip[1]], outs[i].at[1 + j], send_sems.at[i, j], recv_sems.at[i, j], (*chip, c))
                for i in range(n) for j, chip in enumerate(_other_chips(x, y))]
        local = [pltpu.make_async_copy(ins[i].at[2 * x + y], outs[i].at[0], local_sems.at[i, 0]) for i in range(n)]
        return both, both, local

    outs = tuple(jax.ShapeDtypeStruct(p.shape, p.dtype) for p in parts)
    return _Exchange(tuple(parts), outs, False, 3, 1, copies)


def _exchange_plumbing(exchanges):
    operands = [a for ex in exchanges for a in ex.ins]
    results = [s for ex in exchanges for s in ex.outs]
    scratch, in_place, at = [], {}, 0
    for ex in exchanges:
        n = len(ex.ins)
        scratch += [pltpu.SemaphoreType.DMA((n, ex.n_remote))] * 2
        if ex.n_local:
            scratch.append(pltpu.SemaphoreType.DMA((n, ex.n_local)))
        if ex.in_place:
            in_place.update({at + i: at + i for i in range(n)})
        at += n

    def copies(in_refs, out_refs, sem_refs):
        sends, receives, local = [], [], []
        at, sem_at = 0, 0
        for ex in exchanges:
            n, n_sem = len(ex.ins), 3 if ex.n_local else 2
            s, r, l = ex.copies(in_refs[at:at + n], out_refs[at:at + n], *sem_refs[sem_at:sem_at + n_sem])
            sends, receives, local = sends + s, receives + r, local + l
            at, sem_at = at + n, sem_at + n_sem
        return sends, receives, local

    return operands, results, scratch, in_place, copies


def _start_all(copies):
    sends, _, local = copies
    for cp in local + sends:
        cp.start()


def _wait_all(copies):
    sends, receives, local = copies
    for cp in receives:
        cp.wait_recv()
    for cp in sends:
        cp.wait_send()
    for cp in local:
        cp.wait()


def _exchange_now(exchanges, *, name):
    operands, results, scratch, in_place, copies = _exchange_plumbing(exchanges)
    n = len(operands)

    def body(*refs):
        made = copies(refs[:n], refs[n:2 * n], refs[2 * n:])
        _start_all(made)
        _wait_all(made)

    out = pl.pallas_call(body, name=name, in_specs=[ANY] * n, out_specs=[ANY] * n, out_shape=results,
                         scratch_shapes=scratch, input_output_aliases=in_place)(*operands)
    return _split(out, exchanges)


def _split(flat, exchanges):
    out, at = [], 0
    for ex in exchanges:
        out.append(list(flat[at:at + len(ex.ins)]))
        at += len(ex.ins)
    return out


def _pallas(kern, *, name, grid, in_specs, out_specs, out_shape, operands, scratch_shapes=(), exchanges=()):
    ex_operands, ex_results, ex_scratch, in_place, copies = _exchange_plumbing(exchanges)
    n_in, n_out, n_scratch, n_ex = len(in_specs), len(out_specs), len(scratch_shapes), len(ex_operands)

    def body(*refs):
        ins, refs = refs[:n_in], refs[n_in:]
        ex_ins, refs = refs[:n_ex], refs[n_ex:]
        outs, refs = refs[:n_out], refs[n_out:]
        ex_outs, refs = refs[:n_ex], refs[n_ex:]
        scratch, sems = refs[:n_scratch], refs[n_scratch:]
        if exchanges:
            first = functools.reduce(jnp.logical_and, [pl.program_id(a) == 0 for a in range(len(grid))])
            last = functools.reduce(jnp.logical_and, [pl.program_id(a) == g - 1 for a, g in enumerate(grid)])

            @pl.when(first)
            def _():
                _start_all(copies(ex_ins, ex_outs, sems))

        kern(*ins, *outs, *scratch)
        if exchanges:
            @pl.when(last)
            def _():
                _wait_all(copies(ex_ins, ex_outs, sems))

    res = pl.pallas_call(
        body, name=name, grid=grid, in_specs=list(in_specs) + [ANY] * n_ex, out_specs=list(out_specs) + [ANY] * n_ex,
        out_shape=list(out_shape) + ex_results, scratch_shapes=list(scratch_shapes) + ex_scratch,
        input_output_aliases={n_in + i: n_out + o for i, o in in_place.items()},
        compiler_params=_params(len(grid)))(*operands, *ex_operands)
    return list(res[:n_out]), _split(res[n_out:], exchanges)


def _accumulate(step, n_steps, acc, value, finish):
    if n_steps == 1:
        finish(value)
        return

    @pl.when(step == 0)
    def _():
        acc[0][...] = value

    @pl.when(step > 0)
    def _():
        acc[0][...] += value

    @pl.when(step == n_steps - 1)
    def _():
        finish(acc[0][...])


def _mm_nn(a, w, *, out_dtype, name, residual=None, exchanges=()):
    m, k = a.shape
    nb, _, n_blk = w.shape
    tm, tn, tk = _tile(m, 512, MXU), _tile(n_blk, 1536, MXU), _tile(k, 4096, MXU)
    per = n_blk // tn
    nk = k // tk

    def kern(*refs):
        a_ref, w_ref = refs[:2]
        r_ref = None if residual is None else refs[2]
        o_ref, acc = refs[2 + (residual is not None)], refs[3 + (residual is not None):]

        def finish(total):
            o_ref[...] = (total if r_ref is None else total + r_ref[...]).astype(o_ref.dtype)

        _accumulate(pl.program_id(2), nk, acc, jnp.dot(a_ref[...], w_ref[...], preferred_element_type=F32), finish)

    tile = pl.BlockSpec((tm, tn), lambda j, i, kk: (i, j))
    in_specs = [pl.BlockSpec((tm, tk), lambda j, i, kk: (i, kk)),
                pl.BlockSpec((None, tk, tn), lambda j, i, kk: (j // per, kk, j % per))]
    operands = [a, w]
    if residual is not None:
        in_specs.append(tile)
        operands.append(residual)
    (out,), carried = _pallas(
        kern, name=name, grid=(nb * per, m // tm, nk), in_specs=in_specs, out_specs=[tile],
        out_shape=[jax.ShapeDtypeStruct((m, nb * n_blk), out_dtype)], operands=operands,
        scratch_shapes=[pltpu.VMEM((tm, tn), F32)] * (nk > 1), exchanges=exchanges)
    return out, carried


def _mm_nt(g, w, *, name, exchanges=()):
    m, n = g.shape
    nb, k, n_blk = w.shape
    tm, tko, tn = _tile(m, 1024, MXU), _tile(k, 1024, MXU), _tile(n_blk, 3072, MXU)
    per = n_blk // tn
    nn = n // tn

    def kern(g_ref, w_ref, o_ref, *acc):
        def finish(total):
            o_ref[...] = total

        part = lax.dot_general(g_ref[...], w_ref[...], (((1,), (1,)), ((), ())), preferred_element_type=F32)
        _accumulate(pl.program_id(2), nn, acc, part, finish)

    (out,), carried = _pallas(
        kern, name=name, grid=(k // tko, m // tm, nn),
        in_specs=[pl.BlockSpec((tm, tn), lambda j, i, jn: (i, jn)),
                  pl.BlockSpec((None, tko, tn), lambda j, i, jn: (jn // per, j, jn % per))],
        out_specs=[pl.BlockSpec((tm, tko), lambda j, i, jn: (i, j))],
        out_shape=[jax.ShapeDtypeStruct((m, k), F32)], operands=[g, w],
        scratch_shapes=[pltpu.VMEM((tm, tko), F32)] * (nn > 1), exchanges=exchanges)
    return out, carried


def _mm_tn(a, g, nb, *, name, exchanges=()):
    m, k = a.shape
    n = g.shape[1]
    n_blk = n // nb
    tko, tn, tm = _tile(k, 512, MXU), _tile(n_blk, 1536, MXU), _tile(m, 4096, MXU)
    per = n_blk // tn
    nm = m // tm

    def kern(a_ref, g_ref, o_ref, *acc):
        def finish(total):
            o_ref[...] = total.astype(o_ref.dtype)

        part = lax.dot_general(a_ref[...], g_ref[...], (((0,), (0,)), ((), ())), preferred_element_type=F32)
        _accumulate(pl.program_id(2), nm, acc, part, finish)

    (out,), carried = _pallas(
        kern, name=name, grid=(nb * per, k // tko, nm),
        in_specs=[pl.BlockSpec((tm, tko), lambda j, i, im: (im, i)),
                  pl.BlockSpec((tm, tn), lambda j, i, im: (im, j))],
        out_specs=[pl.BlockSpec((None, tko, tn), lambda j, i, im: (j // per, i, j % per))],
        out_shape=[jax.ShapeDtypeStruct((nb, k, n_blk), BF16)], operands=[a, g],
        scratch_shapes=[pltpu.VMEM((tko, tn), F32)] * (nm > 1), exchanges=exchanges)
    return out, carried


ROW_TILE = 128


def _rmsnorm_fwd(x, g, *, name):
    t, d = x.shape
    tr = _tile(t, ROW_TILE, SUBLANES)

    def kern(x_ref, g_ref, h_ref):
        xv = x_ref[...]
        r = lax.rsqrt(jnp.mean(xv * xv, axis=-1, keepdims=True) + EPS)
        h_ref[...] = (xv * r * g_ref[...]).astype(BF16)

    return pl.pallas_call(
        kern, name=name, grid=(t // tr,),
        in_specs=[pl.BlockSpec((tr, d), lambda i: (i, 0)), pl.BlockSpec((1, d), lambda i: (0, 0))],
        out_specs=pl.BlockSpec((tr, d), lambda i: (i, 0)),
        out_shape=jax.ShapeDtypeStruct((t, d), BF16), compiler_params=_params(1))(x, g)


def _rmsnorm_bwd(x, g, d_h, d_res, *, name):
    t, d = x.shape
    tr = _tile(t, ROW_TILE, SUBLANES)

    def kern(x_ref, g_ref, dh_ref, dres_ref, dx_ref, dxb_ref, dg_ref):
        xv = x_ref[...]
        r = lax.rsqrt(jnp.mean(xv * xv, axis=-1, keepdims=True) + EPS)
        dh = dh_ref[...]
        gy = dh * g_ref[...]
        dx = dres_ref[...] + r * gy - xv * (r * r * r) * jnp.mean(gy * xv, axis=-1, keepdims=True)
        dx_ref[...] = dx
        dxb_ref[...] = dx.astype(BF16)

        @pl.when(pl.program_id(0) == 0)
        def _():
            dg_ref[...] = jnp.zeros_like(dg_ref)

        dg_ref[...] += jnp.sum(dh * (xv * r), axis=0, keepdims=True)

    row = pl.BlockSpec((tr, d), lambda i: (i, 0))
    vec = pl.BlockSpec((1, d), lambda i: (0, 0))
    return pl.pallas_call(
        kern, name=name, grid=(t // tr,), in_specs=[row, vec, row, row], out_specs=[row, row, vec],
        out_shape=[jax.ShapeDtypeStruct((t, d), F32), jax.ShapeDtypeStruct((t, d), BF16),
                   jax.ShapeDtypeStruct((1, d), F32)], compiler_params=_params(1))(x, g, d_h, d_res)


def _loss_head(x, g, target, *, name):
    t, d = x.shape
    tr = _tile(t, ROW_TILE, SUBLANES)

    def kern(x_ref, g_ref, t_ref, dx_ref, dxb_ref, dg_ref, loss_ref):
        xv = x_ref[...]
        gv = g_ref[...]
        r = lax.rsqrt(jnp.mean(xv * xv, axis=-1, keepdims=True) + EPS)
        diff = xv * r * gv - t_ref[...]
        dy = diff * (1.0 / d)
        gy = dy * gv
        dx = r * gy - xv * (r * r * r) * jnp.mean(gy * xv, axis=-1, keepdims=True)
        dx_ref[...] = dx
        dxb_ref[...] = dx.astype(BF16)

        @pl.when(pl.program_id(0) == 0)
        def _():
            dg_ref[...] = jnp.zeros_like(dg_ref)
            loss_ref[...] = jnp.zeros_like(loss_ref)

        dg_ref[...] += jnp.sum(dy * (xv * r), axis=0, keepdims=True)
        part = 0.5 * jnp.sum(jnp.mean(diff * diff, axis=-1, keepdims=True), axis=0, keepdims=True)
        loss_ref[...] += jnp.broadcast_to(part, loss_ref.shape)

    row = pl.BlockSpec((tr, d), lambda i: (i, 0))
    vec = pl.BlockSpec((1, d), lambda i: (0, 0))
    return pl.pallas_call(
        kern, name=name, grid=(t // tr,), in_specs=[row, vec, row],
        out_specs=[row, row, vec, pl.BlockSpec((1, LANES), lambda i: (0, 0))],
        out_shape=[jax.ShapeDtypeStruct((t, d), F32), jax.ShapeDtypeStruct((t, d), BF16),
                   jax.ShapeDtypeStruct((1, d), F32), jax.ShapeDtypeStruct((1, LANES), F32)],
        compiler_params=_params(1))(x, g, target)


def _merge_fwd(m_a, m_b, proj, gate_col, *, name):
    t, d = m_a.shape
    tr = _tile(t, ROW_TILE, SUBLANES)

    def kern(ma_ref, mb_ref, ga_ref, gb_ref, o_ref):
        o_ref[...] = (_sigmoid(ga_ref[...]) * ma_ref[...] + _sigmoid(gb_ref[...]) * mb_ref[...]).astype(BF16)

    row = pl.BlockSpec((tr, d), lambda i: (i, 0))
    return pl.pallas_call(
        kern, name=name, grid=(t // tr,),
        in_specs=[row, row, pl.BlockSpec((tr, d), lambda i: (i, gate_col)),
                  pl.BlockSpec((tr, d), lambda i: (i, gate_col + 1))],
        out_specs=row, out_shape=jax.ShapeDtypeStruct((t, d), BF16), compiler_params=_params(1))(m_a, m_b, proj, proj)


def _merge_bwd(d_merged, m_a, m_b, proj, gate_col, *, name):
    t, d = m_a.shape
    tr = _tile(t, ROW_TILE, SUBLANES)

    def kern(dm_ref, ma_ref, mb_ref, ga_ref, gb_ref, dma_ref, dmb_ref, dg_ref):
        dm = dm_ref[...]
        sa = _sigmoid(ga_ref[...])
        sb = _sigmoid(gb_ref[...])
        dma_ref[...] = (dm * sa).astype(BF16)
        dmb_ref[...] = (dm * sb).astype(BF16)
        dg_ref[:, 0:d] = (dm * ma_ref[...] * (sa * (1.0 - sa))).astype(BF16)
        dg_ref[:, d:2 * d] = (dm * mb_ref[...] * (sb * (1.0 - sb))).astype(BF16)

    row = pl.BlockSpec((tr, d), lambda i: (i, 0))
    return pl.pallas_call(
        kern, name=name, grid=(t // tr,),
        in_specs=[row, row, row, pl.BlockSpec((tr, d), lambda i: (i, gate_col)),
                  pl.BlockSpec((tr, d), lambda i: (i, gate_col + 1))],
        out_specs=[row, row, pl.BlockSpec((tr, 2 * d), lambda i: (i, 0))],
        out_shape=[jax.ShapeDtypeStruct((t, d), BF16), jax.ShapeDtypeStruct((t, d), BF16),
                   jax.ShapeDtypeStruct((t, 2 * d), BF16)], compiler_params=_params(1))(d_merged, m_a, m_b, proj, proj)


def _tril_bf16(w, transposed):
    row = lax.broadcasted_iota(jnp.int32, w.shape, 0)
    col = lax.broadcasted_iota(jnp.int32, w.shape, 1)
    keep = (row <= col) if transposed else (row >= col)
    return jnp.where(keep, w, 0.0).astype(BF16)


def _layernorm_stats(v):
    mu = jnp.mean(v, axis=-1, keepdims=True)
    vc = v - mu
    rstd = lax.rsqrt(jnp.mean(vc * vc, axis=-1, keepdims=True) + EPS)
    return vc * rstd, rstd


def _mixer_a_fwd(proj, ln_g, ln_b, sg_w, sg_b_t, sgw, *, name):
    t = proj.shape[0]
    gd = sgw // SG_GROUPS

    def kern(zu_ref, zv_ref, g_ref, b_ref, w_ref, bt_ref, o_ref):
        xhat, _ = _layernorm_stats(_gelu(zv_ref[...]))
        vn = (xhat * g_ref[...] + b_ref[...]).astype(BF16)
        for g in range(SG_GROUPS):
            cols = slice(g * gd, (g + 1) * gd)
            mixed = jnp.dot(_tril_bf16(w_ref[g], False), vn[:, cols], preferred_element_type=F32) + bt_ref[:, g:g + 1]
            o_ref[:, cols] = (_gelu(zu_ref[:, cols]) * mixed).astype(BF16)

    vec = pl.BlockSpec((1, sgw), lambda i: (0, 0))
    return pl.pallas_call(
        kern, name=name, grid=(t // CHUNK,),
        in_specs=[pl.BlockSpec((CHUNK, sgw), lambda i: (i, 0)), pl.BlockSpec((CHUNK, sgw), lambda i: (i, 1)), vec, vec,
                  pl.BlockSpec((SG_GROUPS, CHUNK, CHUNK), lambda i: (0, 0, 0)),
                  pl.BlockSpec((CHUNK, SG_GROUPS), lambda i: (0, 0))],
        out_specs=pl.BlockSpec((CHUNK, sgw), lambda i: (i, 0)),
        out_shape=jax.ShapeDtypeStruct((t, sgw), BF16), compiler_params=_params(1))(proj, proj, ln_g, ln_b, sg_w, sg_b_t)


def _mixer_a_bwd(proj, d_ya, ln_g, ln_b, sg_w, sg_w_t, sg_b_t, sgw, *, name):
    t = proj.shape[0]
    gd = sgw // SG_GROUPS

    def kern(zu_ref, zv_ref, dy_ref, g_ref, b_ref, w_ref, wt_ref, bt_ref, dz_ref, dw_ref, dbt_ref, dg_ref, db_ref, dvn):
        @pl.when(pl.program_id(0) == 0)
        def _():
            dw_ref[...] = jnp.zeros_like(dw_ref)
            dbt_ref[...] = jnp.zeros_like(dbt_ref)
            dg_ref[...] = jnp.zeros_like(dg_ref)
            db_ref[...] = jnp.zeros_like(db_ref)

        gv, dgv = _gelu_and_grad(zv_ref[...])
        xhat, rstd = _layernorm_stats(gv)
        ln_gain = g_ref[...]
        vn = (xhat * ln_gain + b_ref[...]).astype(BF16)
        for g in range(SG_GROUPS):
            cols = slice(g * gd, (g + 1) * gd)
            gu, dgu = _gelu_and_grad(zu_ref[:, cols])
            mixed = jnp.dot(_tril_bf16(w_ref[g], False), vn[:, cols], preferred_element_type=F32) + bt_ref[:, g:g + 1]
            dy = dy_ref[:, cols]
            dz_ref[:, cols] = (dy * mixed * dgu).astype(BF16)
            d_mixed = dy * gu
            d_mixed_b = d_mixed.astype(BF16)
            dvn[:, cols] = jnp.dot(_tril_bf16(wt_ref[g], True), d_mixed_b, preferred_element_type=F32)
            d_w = lax.dot_general(d_mixed_b, vn[:, cols], (((1,), (1,)), ((), ())), preferred_element_type=F32)
            row = lax.broadcasted_iota(jnp.int32, d_w.shape, 0)
            col = lax.broadcasted_iota(jnp.int32, d_w.shape, 1)
            dw_ref[g] += jnp.where(row >= col, d_w, 0.0)
            dbt_ref[:, g:g + 1] += jnp.sum(d_mixed, axis=-1, keepdims=True)
        d_vn = dvn[...]
        dg_ref[...] += jnp.sum(d_vn * xhat, axis=0, keepdims=True)
        db_ref[...] += jnp.sum(d_vn, axis=0, keepdims=True)
        d_xhat = d_vn * ln_gain
        d_gv = rstd * (d_xhat - jnp.mean(d_xhat, axis=-1, keepdims=True)
                       - xhat * jnp.mean(d_xhat * xhat, axis=-1, keepdims=True))
        dz_ref[:, sgw:2 * sgw] = (d_gv * dgv).astype(BF16)

    vec = pl.BlockSpec((1, sgw), lambda i: (0, 0))
    wspec = pl.BlockSpec((SG_GROUPS, CHUNK, CHUNK), lambda i: (0, 0, 0))
    btspec = pl.BlockSpec((CHUNK, SG_GROUPS), lambda i: (0, 0))
    return pl.pallas_call(
        kern, name=name, grid=(t // CHUNK,),
        in_specs=[pl.BlockSpec((CHUNK, sgw), lambda i: (i, 0)), pl.BlockSpec((CHUNK, sgw), lambda i: (i, 1)),
                  pl.BlockSpec((CHUNK, sgw), lambda i: (i, 0)), vec, vec, wspec, wspec, btspec],
        out_specs=[pl.BlockSpec((CHUNK, 2 * sgw), lambda i: (i, 0)), wspec, btspec, vec, vec],
        out_shape=[jax.ShapeDtypeStruct((t, 2 * sgw), BF16), jax.ShapeDtypeStruct((SG_GROUPS, CHUNK, CHUNK), F32),
                   jax.ShapeDtypeStruct((CHUNK, SG_GROUPS), F32), jax.ShapeDtypeStruct((1, sgw), F32),
                   jax.ShapeDtypeStruct((1, sgw), F32)],
        scratch_shapes=[pltpu.VMEM((CHUNK, sgw), F32)],
        compiler_params=_params(1))(proj, proj, d_ya, ln_g, ln_b, sg_w, sg_w_t, sg_b_t)


def _scan_rows(a_ref, h_ref, reverse):
    s, c = a_ref.shape
    nblk = s // SUBLANES
    a, b = a_ref[...], h_ref[...]
    row = jnp.bitwise_and(lax.broadcasted_iota(jnp.int32, (s, c), 0), SUBLANES - 1)
    for d in (1, 2, 4):
        inside = (row < SUBLANES - d) if reverse else (row >= d)
        shift = s - d if reverse else d
        b = a * jnp.where(inside, pltpu.roll(b, shift, 0), 0.0) + b
        a = a * jnp.where(inside, pltpu.roll(a, shift, 0), 1.0)
    a_ref[...] = a
    h_ref[...] = b
    leaving = 0 if reverse else SUBLANES - 1

    def chain(i, carry):
        r0 = pl.multiple_of((nblk - 1 - i if reverse else i) * SUBLANES, SUBLANES)
        h = a_ref[pl.ds(r0, SUBLANES), :] * carry + h_ref[pl.ds(r0, SUBLANES), :]
        h_ref[pl.ds(r0, SUBLANES), :] = h
        return jnp.broadcast_to(h[leaving:leaving + 1, :], (SUBLANES, c))

    lax.fori_loop(0, nblk, chain, jnp.zeros((SUBLANES, c), F32))


def _lru_gates(xc, wa_ref, ba_ref, wx_ref, bx_ref, lam_ref):
    xcb = xc.astype(BF16)
    ra = _sigmoid(jnp.dot(xcb, wa_ref[...].astype(BF16), preferred_element_type=F32) + ba_ref[...])
    ia = _sigmoid(jnp.dot(xcb, wx_ref[...].astype(BF16), preferred_element_type=F32) + bx_ref[...])
    neg = -lam_ref[...]
    sp = jnp.maximum(neg, 0.0) + jnp.log1p(jnp.exp(-jnp.abs(neg)))
    log_a = -LRU_C * ra * sp
    a = jnp.exp(log_a)
    a2 = jnp.exp(2.0 * log_a)
    sq = jnp.sqrt(-jnp.tanh(log_a) * (a2 + 1.0))
    return ra, ia, sp, a, a2, sq


def _mixer_b_specs(seq, hd, sgw, lw):
    x_col = (2 * sgw) // hd
    y_col = (2 * sgw + lw) // hd
    tile = lambda col: pl.BlockSpec((seq, hd), lambda h, b: (b, col + h))
    vec = pl.BlockSpec((1, hd), lambda h, b: (0, h))
    mat = pl.BlockSpec((None, hd, hd), lambda h, b: (h, 0, 0))
    return tile(x_col), tile(y_col), tile(0), vec, mat


def _mixer_b_fwd(proj, conv_w, conv_b, wa, ba, wx, bx, lam, *, seq, sgw, lw, name):
    t = proj.shape[0]
    hd = lw // LRU_HEADS
    k_taps = conv_w.shape[0]
    x_spec, y_spec, o_spec, vec, mat = _mixer_b_specs(seq, hd, sgw, lw)

    def kern(xr_ref, yr_ref, cw_ref, cb_ref, wa_ref, ba_ref, wx_ref, bx_ref, lam_ref, o_ref, s_a, s_h):
        xc = _causal_conv(xr_ref[...], cw_ref[...], cb_ref[...])
        _, ia, _, a, _, sq = _lru_gates(xc, wa_ref, ba_ref, wx_ref, bx_ref, lam_ref)
        s_a[...] = a
        s_h[...] = sq * (ia * xc)
        _scan_rows(s_a, s_h, False)
        o_ref[...] = (s_h[...] * _gelu(yr_ref[...])).astype(BF16)

    return pl.pallas_call(
        kern, name=name, grid=(LRU_HEADS, t // seq),
        in_specs=[x_spec, y_spec, pl.BlockSpec((k_taps, hd), lambda h, b: (0, h)), vec, mat, vec, mat, vec, vec],
        out_specs=o_spec, out_shape=jax.ShapeDtypeStruct((t, lw), BF16),
        scratch_shapes=[pltpu.VMEM((seq, hd), F32), pltpu.VMEM((seq, hd), F32)],
        compiler_params=_params(2))(proj, proj, conv_w, conv_b, wa, ba, wx, bx, lam)


def _mixer_b_bwd(proj, d_yb, conv_w, conv_b, wa, wa_t, ba, wx, wx_t, bx, lam, *, seq, sgw, lw, name, exchanges=()):
    t = proj.shape[0]
    hd = lw // LRU_HEADS
    k_taps = conv_w.shape[0]
    x_spec, y_spec, o_spec, vec, mat = _mixer_b_specs(seq, hd, sgw, lw)
    cw_spec = pl.BlockSpec((k_taps, hd), lambda h, b: (0, h))

    def kern(xr_ref, yr_ref, dyb_ref, cw_ref, cb_ref, wa_ref, wat_ref, ba_ref, wx_ref, wxt_ref, bx_ref, lam_ref,
             dxr_ref, dyr_ref, dcw_ref, dcb_ref, dwa_ref, dba_ref, dwx_ref, dbx_ref, dlam_ref,
             s_xc, s_a, s_h, s_lam, s_dpa, s_dpx):
        @pl.when(pl.program_id(1) == 0)
        def _():
            for ref in (dcw_ref, dcb_ref, dwa_ref, dba_ref, dwx_ref, dbx_ref, dlam_ref):
                ref[...] = jnp.zeros_like(ref)

        s_xc[...] = _causal_conv(xr_ref[...], cw_ref[...], cb_ref[...])
        _, ia, _, a, _, sq = _lru_gates(s_xc[...], wa_ref, ba_ref, wx_ref, bx_ref, lam_ref)
        s_a[...] = a
        s_dpa[...] = _shift_up(a, 1)
        s_h[...] = sq * (ia * s_xc[...])
        _scan_rows(s_a, s_h, False)

        gel, dgel = _gelu_and_grad(yr_ref[...])
        dyb = dyb_ref[...]
        dyr_ref[...] = (dyb * s_h[...] * dgel).astype(BF16)
        s_lam[...] = dyb * gel
        _scan_rows(s_dpa, s_lam, True)
        ra, ia, sp, a, a2, sq = _lru_gates(s_xc[...], wa_ref, ba_ref, wx_ref, bx_ref, lam_ref)
        d_gx = s_lam[...]
        d_a = d_gx * _shift_down(s_h[...], 1)
        xc = s_xc[...]
        d_sq = d_gx * (ia * xc)
        d_ia = d_gx * (sq * xc)
        d_log_a = d_a * a - d_sq * (a2 / sq)
        d_ra = d_log_a * (-LRU_C * sp)
        d_sp = jnp.sum(d_log_a * (-LRU_C * ra), axis=0, keepdims=True)
        dlam_ref[...] += d_sp * (-_sigmoid(-lam_ref[...]))
        d_pa = d_ra * (ra * (1.0 - ra))
        d_px = d_ia * (ia * (1.0 - ia))
        s_dpa[...] = d_pa
        s_dpx[...] = d_px
        dba_ref[...] += jnp.sum(d_pa, axis=0, keepdims=True)
        dbx_ref[...] += jnp.sum(d_px, axis=0, keepdims=True)
        xcb = s_xc[...].astype(BF16)
        d_pa_b = s_dpa[...].astype(BF16)
        d_px_b = s_dpx[...].astype(BF16)
        contract_rows = (((0,), (0,)), ((), ()))
        dwa_ref[...] += lax.dot_general(xcb, d_pa_b, contract_rows, preferred_element_type=F32)
        dwx_ref[...] += lax.dot_general(xcb, d_px_b, contract_rows, preferred_element_type=F32)
        d_xc = (s_lam[...] * (sq * ia)
                + jnp.dot(d_pa_b, wat_ref[...].astype(BF16), preferred_element_type=F32)
                + jnp.dot(d_px_b, wxt_ref[...].astype(BF16), preferred_element_type=F32))
        dcb_ref[...] += jnp.sum(d_xc, axis=0, keepdims=True)
        dcw_ref[...] += _causal_conv_bwd_w(d_xc, xr_ref[...], k_taps)
        dxr_ref[...] = _causal_conv_bwd_x(d_xc, cw_ref[...]).astype(BF16)

    tile_shape = jax.ShapeDtypeStruct((t, lw), BF16)
    vec_shape = jax.ShapeDtypeStruct((1, lw), F32)
    mat_shape = jax.ShapeDtypeStruct((LRU_HEADS, hd, hd), F32)
    return _pallas(
        kern, name=name, grid=(LRU_HEADS, t // seq),
        in_specs=[x_spec, y_spec, o_spec, cw_spec, vec, mat, mat, vec, mat, mat, vec, vec],
        out_specs=[o_spec, o_spec, cw_spec, vec, mat, vec, mat, vec, vec],
        out_shape=[tile_shape, tile_shape, jax.ShapeDtypeStruct((k_taps, lw), F32), vec_shape, mat_shape, vec_shape,
                   mat_shape, vec_shape, vec_shape],
        operands=[proj, proj, d_yb, conv_w, conv_b, wa, wa_t, ba, wx, wx_t, bx, lam],
        scratch_shapes=[pltpu.VMEM((seq, hd), F32)] * 6, exchanges=exchanges)


FFN_TILE = 256


def _ffn_mid_fwd(up_pre, conv_w, conv_b, *, seq, name):
    t, f2 = up_pre.shape
    f = f2 // 2
    tc = _tile(f, FFN_TILE, LANES)
    nf = f // tc
    k_taps = conv_w.shape[0]

    def kern(pg_ref, pv_ref, wg_ref, wv_ref, bg_ref, bv_ref, o_ref):
        cg = _causal_conv(pg_ref[...], wg_ref[...], bg_ref[...])
        cv = _causal_conv(pv_ref[...], wv_ref[...], bv_ref[...])
        o_ref[...] = (_gelu(cg) * cv).astype(BF16)

    tile = lambda off: pl.BlockSpec((seq, tc), lambda j, b: (b, off + j))
    wspec = lambda off: pl.BlockSpec((k_taps, tc), lambda j, b: (0, off + j))
    bspec = lambda off: pl.BlockSpec((1, tc), lambda j, b: (0, off + j))
    return pl.pallas_call(
        kern, name=name, grid=(nf, t // seq),
        in_specs=[tile(0), tile(nf), wspec(0), wspec(nf), bspec(0), bspec(nf)], out_specs=tile(0),
        out_shape=jax.ShapeDtypeStruct((t, f), BF16),
        compiler_params=_params(2))(up_pre, up_pre, conv_w, conv_w, conv_b, conv_b)


def _ffn_mid_bwd(up_pre, d_act, conv_w, conv_b, *, seq, name):
    t, f2 = up_pre.shape
    f = f2 // 2
    tc = _tile(f, FFN_TILE, LANES)
    nf = f // tc
    k_taps = conv_w.shape[0]

    def kern(pg_ref, pv_ref, da_ref, wg_ref, wv_ref, bg_ref, bv_ref, dpg_ref, dpv_ref, dwg_ref, dwv_ref, dbg_ref, dbv_ref):
        @pl.when(pl.program_id(1) == 0)
        def _():
            for ref in (dwg_ref, dwv_ref, dbg_ref, dbv_ref):
                ref[...] = jnp.zeros_like(ref)

        pg = pg_ref[...]
        pv = pv_ref[...]
        gel, dgel = _gelu_and_grad(_causal_conv(pg, wg_ref[...], bg_ref[...]))
        cv = _causal_conv(pv, wv_ref[...], bv_ref[...])
        d_act_v = da_ref[...]
        d_cg = d_act_v * cv * dgel
        d_cv = d_act_v * gel
        dpg_ref[...] = _causal_conv_bwd_x(d_cg, wg_ref[...]).astype(BF16)
        dpv_ref[...] = _causal_conv_bwd_x(d_cv, wv_ref[...]).astype(BF16)
        dwg_ref[...] += _causal_conv_bwd_w(d_cg, pg, k_taps)
        dwv_ref[...] += _causal_conv_bwd_w(d_cv, pv, k_taps)
        dbg_ref[...] += jnp.sum(d_cg, axis=0, keepdims=True)
        dbv_ref[...] += jnp.sum(d_cv, axis=0, keepdims=True)

    tile = lambda off: pl.BlockSpec((seq, tc), lambda j, b: (b, off + j))
    wspec = lambda off: pl.BlockSpec((k_taps, tc), lambda j, b: (0, off + j))
    bspec = lambda off: pl.BlockSpec((1, tc), lambda j, b: (0, off + j))
    half = jax.ShapeDtypeStruct((t, f), BF16)
    wshape = jax.ShapeDtypeStruct((k_taps, f), F32)
    bshape = jax.ShapeDtypeStruct((1, f), F32)
    d_pg, d_pv, d_wg, d_wv, d_bg, d_bv = pl.pallas_call(
        kern, name=name, grid=(nf, t // seq),
        in_specs=[tile(0), tile(nf), tile(0), wspec(0), wspec(nf), bspec(0), bspec(nf)],
        out_specs=[tile(0), tile(0), wspec(0), wspec(0), bspec(0), bspec(0)],
        out_shape=[half, half, wshape, wshape, bshape, bshape],
        compiler_params=_params(2))(up_pre, up_pre, d_act, conv_w, conv_w, conv_b, conv_b)
    return (jnp.concatenate([d_pg, d_pv], axis=1), jnp.concatenate([d_wg, d_wv], axis=1),
            jnp.concatenate([d_bg, d_bv], axis=1))


ELEM_VMEM_BYTES = 24 << 20


def _as_2d(a):
    if a.ndim >= 2 and a.shape[-1] % LANES == 0 and a.size // a.shape[-1] >= SUBLANES:
        return a.reshape(-1, a.shape[-1])
    return a.reshape(-1, LANES)


def _row_tile(rows, bytes_per_row):
    return _tile(rows, max(16, ELEM_VMEM_BYTES // (2 * bytes_per_row)), 16)


def _cast_bf16(a, *, name):
    v = _as_2d(a)
    rows, cols = v.shape
    tr = _row_tile(rows, cols * (4 + 2))

    def kern(x_ref, o_ref):
        o_ref[...] = x_ref[...].astype(BF16)

    spec = pl.BlockSpec((tr, cols), lambda i: (i, 0))
    out = pl.pallas_call(kern, name=name, grid=(rows // tr,), in_specs=[spec], out_specs=spec,
                         out_shape=jax.ShapeDtypeStruct(v.shape, BF16), compiler_params=_params(1))(v)
    return out.reshape(a.shape)


def _add_sibling_part(own, core, got, *, name, chunk=0, n_chunks=1):
    _, _, all_rows, cols = own.shape
    rows = all_rows // n_chunks
    tr = _row_tile(rows, cols * (2 + 2 + 2))
    first = chunk * (rows // tr)

    def kern(core_ref, a_ref, b_ref, o_ref):
        o_ref[...] = (a_ref[...].astype(F32) + b_ref[...].astype(F32)).astype(BF16)

    grid_spec = pltpu.PrefetchScalarGridSpec(
        num_scalar_prefetch=1, grid=(4, rows // tr),
        in_specs=[pl.BlockSpec((None, None, tr, cols), lambda ch, i, core_ref: (ch, core_ref[0], first + i, 0)),
                  pl.BlockSpec((None, tr, cols), lambda ch, i, core_ref: (ch, first + i, 0))],
        out_specs=pl.BlockSpec((None, tr, cols), lambda ch, i, core_ref: (ch, i, 0)))
    return pl.pallas_call(kern, name=name, grid_spec=grid_spec, out_shape=jax.ShapeDtypeStruct((4, rows, cols), BF16),
                          compiler_params=_params(2))(core, own, got)


def _sum_parts(parts, *, name):
    n_parts, rows, cols = parts.shape
    tr = _row_tile(rows, cols * 4 * (n_parts + 1))

    def kern(p_ref, o_ref):
        acc = p_ref[0].astype(F32)
        for p in range(1, n_parts):
            acc = acc + p_ref[p].astype(F32)
        o_ref[...] = acc

    return pl.pallas_call(
        kern, name=name, grid=(rows // tr,), in_specs=[pl.BlockSpec((n_parts, tr, cols), lambda i: (0, i, 0))],
        out_specs=pl.BlockSpec((tr, cols), lambda i: (i, 0)), out_shape=jax.ShapeDtypeStruct((rows, cols), F32),
        compiler_params=_params(1))(parts)


def _adamw(w, m, v, grad_chunks, *, name):
    shape = w.shape
    w2 = _as_2d(w)
    rows, cols = w2.shape
    n_chunks = len(grad_chunks)
    n_parts = grad_chunks[0].shape[0]
    chunks = [c.reshape(n_parts, rows // n_chunks, cols) for c in grad_chunks]
    tr = _row_tile(rows // n_chunks, cols * (3 * 4 + n_chunks * n_parts * chunks[0].dtype.itemsize + 4 * 4))
    per_chunk = rows // n_chunks // tr
    c_m = 1.0 - ADAM_B1 ** ADAM_STEP
    c_v = 1.0 - ADAM_B2 ** ADAM_STEP

    def kern(w_ref, m_ref, v_ref, *refs):
        p_refs, (g_ref, d_ref, nm_ref, nv_ref) = refs[:n_chunks], refs[n_chunks:]
        g = None
        for k, p_ref in enumerate(p_refs):
            total = p_ref[0].astype(F32)
            for p in range(1, n_parts):
                total = total + p_ref[p].astype(F32)
            g = total if g is None else jnp.where(pl.program_id(0) // per_chunk == k, total, g)
        new_m = ADAM_B1 * m_ref[...] + (1.0 - ADAM_B1) * g
        new_v = ADAM_B2 * v_ref[...] + (1.0 - ADAM_B2) * (g * g)
        g_ref[...] = g
        nm_ref[...] = new_m
        nv_ref[...] = new_v
        d_ref[...] = -ADAM_LR * ((new_m / c_m) / (jnp.sqrt(new_v / c_v) + ADAM_EPS) + ADAM_WD * w_ref[...])

    spec = pl.BlockSpec((tr, cols), lambda i: (i, 0))
    out = jax.ShapeDtypeStruct((rows, cols), F32)
    def chunk_spec(k):
        return pl.BlockSpec((n_parts, tr, cols), lambda i: (0, jnp.clip(i - k * per_chunk, 0, per_chunk - 1), 0))

    res = pl.pallas_call(
        kern, name=name, grid=(rows // tr,),
        in_specs=[spec, spec, spec] + [chunk_spec(k) for k in range(n_chunks)],
        out_specs=[spec] * 4, out_shape=[out] * 4, compiler_params=_params(1))(w2, _as_2d(m), _as_2d(v), *chunks)
    return [r.reshape(shape) for r in res]


def _all_gather(shards, *, name):
    n = len(shards)

    def body(*refs):
        ins, outs = refs[:n], refs[n:2 * n]
        send_sems, recv_sems, local_sems = refs[2 * n:]
        x, y, c = _place()
        me, sibling = (x, y, c), (x, y, 1 - c)
        chips = [(1 - x, y), (x, 1 - y), (1 - x, 1 - y)]

        def slot(i, dev):
            return outs[i].at[4 * dev[0] + 2 * dev[1] + dev[2]]

        def copy(i, k, block, to, src=None):
            return pltpu.make_async_remote_copy(
                src_ref=slot(i, block) if src is None else src, dst_ref=slot(i, block),
                send_sem=send_sems.at[i, k], recv_sem=recv_sems.at[i, k], device_id=to, device_id_type=MESH)

        mine = [pltpu.make_async_copy(ins[i], slot(i, me), local_sems.at[i]) for i in range(n)]
        for cp in mine:
            cp.start()
        first = []
        for i in range(n):
            first.append(copy(i, 0, me, sibling, src=ins[i]))
            first += [copy(i, 1 + j, me, (*chip, c), src=ins[i]) for j, chip in enumerate(chips)]
        for cp in first:
            cp.start()
        passed = []
        for j, chip in enumerate(chips):
            for i in range(n):
                copy(i, 1 + j, (*chip, c), me).wait_recv()
                onward = copy(i, 4 + j, (*chip, c), sibling)
                onward.start()
                passed.append(onward)
        for i in range(n):
            copy(i, 0, sibling, me).wait_recv()
            for j, chip in enumerate(chips):
                copy(i, 4 + j, (*chip, 1 - c), me).wait_recv()
        for cp in first + passed:
            cp.wait_send()
        for cp in mine:
            cp.wait()

    return pl.pallas_call(
        body, name=name, in_specs=[ANY] * n, out_specs=[ANY] * n,
        out_shape=[jax.ShapeDtypeStruct((N_DEV,) + s.shape, s.dtype) for s in shards],
        scratch_shapes=[pltpu.SemaphoreType.DMA((n, 7)), pltpu.SemaphoreType.DMA((n, 7)), pltpu.SemaphoreType.DMA((n,))],
    )(*shards)


def _by_chip_and_core(grad):
    return grad.reshape(4, 2, -1, grad.shape[-1])


def _pack(vectors):
    flat = [v.reshape(-1).astype(F32) for v in vectors]
    sizes = [f.shape[0] for f in flat]
    total = sum(sizes)
    padded = -(-total // (SUBLANES * LANES)) * (SUBLANES * LANES)
    if padded > total:
        flat.append(jnp.zeros((padded - total,), F32))
    return jnp.concatenate(flat).reshape(-1, LANES), sizes


def _unpack(packed, sizes, shapes):
    flat = packed.reshape(-1)
    out, off = [], 0
    for size, shape in zip(sizes, shapes):
        out.append(flat[off:off + size].reshape(shape))
        off += size
    return out


def kernel(x, g_mix, w_in, sg_ln_g, sg_ln_b, sg_w, sg_b, lru_conv_w, lru_conv_b, lru_wa, lru_ba, lru_wx, lru_bx, lru_lam, p_sg, p_lru, w_out, g_ffn, w_up, ffn_conv_w, ffn_conv_b, w_down, g_final, loss_target, m_g_mix, m_w_in, m_sg_ln_g, m_sg_ln_b, m_sg_w, m_sg_b, m_lru_conv_w, m_lru_conv_b, m_lru_wa, m_lru_ba, m_lru_wx, m_lru_bx, m_lru_lam, m_p_sg, m_p_lru, m_w_out, m_g_ffn, m_w_up, m_ffn_conv_w, m_ffn_conv_b, m_w_down, m_g_final, v_g_mix, v_w_in, v_sg_ln_g, v_sg_ln_b, v_sg_w, v_sg_b, v_lru_conv_w, v_lru_conv_b, v_lru_wa, v_lru_ba, v_lru_wx, v_lru_bx, v_lru_lam, v_p_sg, v_p_lru, v_w_out, v_g_ffn, v_w_up, v_ffn_conv_w, v_ffn_conv_b, v_w_down, v_g_final):
    weights = dict(g_mix=g_mix, w_in=w_in, sg_ln_g=sg_ln_g, sg_ln_b=sg_ln_b, sg_w=sg_w, sg_b=sg_b, lru_conv_w=lru_conv_w,
                   lru_conv_b=lru_conv_b, lru_wa=lru_wa, lru_ba=lru_ba, lru_wx=lru_wx, lru_bx=lru_bx, lru_lam=lru_lam,
                   p_sg=p_sg, p_lru=p_lru, w_out=w_out, g_ffn=g_ffn, w_up=w_up, ffn_conv_w=ffn_conv_w,
                   ffn_conv_b=ffn_conv_b, w_down=w_down, g_final=g_final)
    m_in = dict(g_mix=m_g_mix, w_in=m_w_in, sg_ln_g=m_sg_ln_g, sg_ln_b=m_sg_ln_b, sg_w=m_sg_w, sg_b=m_sg_b,
                lru_conv_w=m_lru_conv_w, lru_conv_b=m_lru_conv_b, lru_wa=m_lru_wa, lru_ba=m_lru_ba, lru_wx=m_lru_wx,
                lru_bx=m_lru_bx, lru_lam=m_lru_lam, p_sg=m_p_sg, p_lru=m_p_lru, w_out=m_w_out, g_ffn=m_g_ffn,
                w_up=m_w_up, ffn_conv_w=m_ffn_conv_w, ffn_conv_b=m_ffn_conv_b, w_down=m_w_down, g_final=m_g_final)
    v_in = dict(g_mix=v_g_mix, w_in=v_w_in, sg_ln_g=v_sg_ln_g, sg_ln_b=v_sg_ln_b, sg_w=v_sg_w, sg_b=v_sg_b,
                lru_conv_w=v_lru_conv_w, lru_conv_b=v_lru_conv_b, lru_wa=v_lru_wa, lru_ba=v_lru_ba, lru_wx=v_lru_wx,
                lru_bx=v_lru_bx, lru_lam=v_lru_lam, p_sg=v_p_sg, p_lru=v_p_lru, w_out=v_w_out, g_ffn=v_g_ffn,
                w_up=v_w_up, ffn_conv_w=v_ffn_conv_w, ffn_conv_b=v_ffn_conv_b, w_down=v_w_down, g_final=v_g_final)
    order = list(weights)

    n_seq, seq, d = x.shape
    t = n_seq * seq
    sgw = sg_ln_g.shape[-1]
    lw = lru_lam.shape[-1]
    hd = lw // LRU_HEADS
    f2 = ffn_conv_b.shape[-1]
    gate_col = (2 * sgw + 2 * lw) // d
    xi, yi, ci = _place()
    dev = 4 * xi + 2 * yi + ci

    core = jnp.reshape(ci, (1,)).astype(jnp.int32)
    big = ["w_in", "p_sg", "p_lru", "w_out", "w_up", "w_down", "lru_wa", "lru_wx"]
    cast = {k: _cast_bf16(weights[k][0], name=f"cast_{k}") for k in big}
    taps, tap_sizes = _pack([lru_conv_w[0], ffn_conv_w[0]])
    at_once = ["w_in", "lru_wa", "lru_wx", "p_sg"]
    gathered = _all_gather([cast[k] for k in at_once] + [taps], name="gather_first_weights")
    w_in_g, wa_8, wx_8, p_sg_g = gathered[:-1]
    wa_g, wx_g = (jnp.swapaxes(w8, 0, 1).reshape(LRU_HEADS, hd, hd) for w8 in (wa_8, wx_8))
    wa_t, wx_t = jnp.swapaxes(wa_g, 1, 2), jnp.swapaxes(wx_g, 1, 2)
    tap_parts = [_unpack(gathered[-1][k], tap_sizes, [lru_conv_w.shape[1:], ffn_conv_w.shape[1:]]) for k in range(N_DEV)]
    lru_cw = jnp.concatenate([p[0] for p in tap_parts], axis=1)
    ffn_cw = jnp.concatenate([p[1] for p in tap_parts], axis=1)
    sg_w0 = sg_w[0]
    sg_w_t = jnp.swapaxes(sg_w0, 1, 2)
    sg_b_t = sg_b[0].T

    def rows_in_order(g8):
        return g8.reshape(1, -1, g8.shape[-1])

    x2d = x.reshape(t, d)
    h1 = _rmsnorm_fwd(x2d, g_mix, name="norm_mix")
    proj, (legs,) = _mm_nn(h1, w_in_g, out_dtype=F32, name="proj_in",
                           exchanges=[_gather_first_leg([cast["w_up"], cast["p_lru"], cast["w_out"]])])
    y_a = _mixer_a_fwd(proj, sg_ln_g, sg_ln_b, sg_w0, sg_b_t, sgw, name="mixer_a_fwd")
    y_b = _mixer_b_fwd(proj, lru_cw, lru_conv_b, wa_g, lru_ba, wx_g, lru_bx, lru_lam, seq=seq, sgw=sgw, lw=lw,
                       name="mixer_b_fwd")
    m_a, ((w_up_g, p_lru_8, w_out_8),) = _mm_nn(y_a, p_sg_g, out_dtype=F32, name="proj_sg",
                                                 exchanges=[_gather_second_leg(legs)])
    p_lru_g, w_out_g = rows_in_order(p_lru_8), rows_in_order(w_out_8)
    m_b, _ = _mm_nn(y_b, p_lru_g, out_dtype=F32, name="proj_lru")
    merged = _merge_fwd(m_a, m_b, proj, gate_col, name="merge_fwd")
    x1, _ = _mm_nn(merged, w_out_g, out_dtype=F32, residual=x2d, name="proj_out")
    h2 = _rmsnorm_fwd(x1, g_ffn, name="norm_ffn")
    up_pre, (leg,) = _mm_nn(h2, w_up_g, out_dtype=F32, name="ffn_up", exchanges=[_gather_first_leg([cast["w_down"]])])
    act = _ffn_mid_fwd(up_pre, ffn_cw, ffn_conv_b, seq=seq, name="ffn_mid_fwd")
    ((w_down_8,),) = _exchange_now([_gather_second_leg(leg)], name="gather_w_down_second_leg")
    w_down_g = rows_in_order(w_down_8)
    x2, _ = _mm_nn(act, w_down_g, out_dtype=F32, residual=x1, name="ffn_down")
    d_x2, d_x2_b, d_g_final, loss_part = _loss_head(x2, g_final.reshape(1, d), loss_target.reshape(t, d), name="loss_head")
    loss = lax.psum(loss_part[0, 0], ("x", "y", "c"))

    def by_rows(g):
        return g.reshape(N_DEV, -1, g.shape[-1])

    def by_head_rows(g):
        return jnp.swapaxes(g.reshape(LRU_HEADS, N_DEV, hd // N_DEV, hd), 0, 1)

    def chip_sum(view, from_sibling, key, n_chunks=1):
        return [_add_sibling_part(view, core, from_sibling, name=f"chip_sum_{key}_{k}", chunk=k, n_chunks=n_chunks)
                for k in range(n_chunks)]

    d_w_down, _ = _mm_tn(act, d_x2_b, 1, name="grad_w_down")
    v_down = _by_chip_and_core(by_rows(d_w_down))
    d_act, ((s_down,),) = _mm_nt(d_x2_b, w_down_g, name="bwd_ffn_down", exchanges=[_swap_with_sibling([v_down])])
    c_down = chip_sum(v_down, s_down, "w_down")
    d_up_pre, d_ffn_cw, d_ffn_cb = _ffn_mid_bwd(up_pre, d_act, ffn_cw, ffn_conv_b, seq=seq, name="ffn_mid_bwd")
    d_w_up, (parts_down,) = _mm_tn(h2, d_up_pre, N_DEV, name="grad_w_up", exchanges=[_swap_with_chips(c_down)])
    v_up = _by_chip_and_core(d_w_up)
    d_h2, ((s_up,),) = _mm_nt(d_up_pre, w_up_g, name="bwd_ffn_up", exchanges=[_swap_with_sibling([v_up])])
    c_up_top, c_up_bottom = chip_sum(v_up, s_up, "w_up", n_chunks=2)
    d_x1, d_x1_b, d_g_ffn = _rmsnorm_bwd(x1, g_ffn, d_h2, d_x2, name="norm_ffn_bwd")
    d_w_out, _ = _mm_tn(merged, d_x1_b, 1, name="grad_w_out")
    v_out = _by_chip_and_core(by_rows(d_w_out))
    d_merged, ((s_out,),) = _mm_nt(d_x1_b, w_out_g, name="bwd_proj_out", exchanges=[_swap_with_sibling([v_out])])
    c_out = chip_sum(v_out, s_out, "w_out")
    d_m_a, d_m_b, d_gates = _merge_bwd(d_merged, m_a, m_b, proj, gate_col, name="merge_bwd")
    d_p_sg, _ = _mm_tn(y_a, d_m_a, N_DEV, name="grad_p_sg")
    d_p_lru, _ = _mm_tn(y_b, d_m_b, 1, name="grad_p_lru")
    v_sg, v_lru = _by_chip_and_core(d_p_sg), _by_chip_and_core(by_rows(d_p_lru))
    d_y_a, ((s_sg, s_lru),) = _mm_nt(d_m_a, p_sg_g, name="bwd_proj_sg", exchanges=[_swap_with_sibling([v_sg, v_lru])])
    d_y_b, (parts_out,) = _mm_nt(d_m_b, p_lru_g, name="bwd_proj_lru", exchanges=[_swap_with_chips(c_out)])
    c_sg, c_lru = chip_sum(v_sg, s_sg, "p_sg"), chip_sum(v_lru, s_lru, "p_lru")
    d_zuv, d_sg_w, d_sg_b_t, d_ln_g, d_ln_b = _mixer_a_bwd(proj, d_y_a, sg_ln_g, sg_ln_b, sg_w0, sg_w_t, sg_b_t, sgw,
                                                           name="mixer_a_bwd")
    (d_xr, d_yr, d_lru_cw, d_lru_cb, d_wa, d_ba, d_wx, d_bx, d_lam), (parts_up_top,) = _mixer_b_bwd(
        proj, d_y_b, lru_cw, lru_conv_b, wa_g, wa_t, lru_ba, wx_g, wx_t, lru_bx, lru_lam, seq=seq, sgw=sgw, lw=lw,
        name="mixer_b_bwd", exchanges=[_swap_with_chips([c_up_top])])
    d_proj = jnp.concatenate([d_zuv, d_xr, d_yr, d_gates], axis=1)
    d_w_in, ((parts_up_bottom, parts_sg, parts_lru),) = _mm_tn(
        h1, d_proj, N_DEV, name="grad_w_in", exchanges=[_swap_with_chips([c_up_bottom] + c_sg + c_lru)])
    v_win = _by_chip_and_core(d_w_in)
    v_wa = _by_chip_and_core(_cast_bf16(by_head_rows(d_wa), name="cast_grad_wa"))
    v_wx = _by_chip_and_core(_cast_bf16(by_head_rows(d_wx), name="cast_grad_wx"))
    ((s_in, s_wa, s_wx),) = _exchange_now([_swap_with_sibling([v_win, v_wa, v_wx])], name="swap_sibling_w_in")
    c_in, c_wa, c_wx = chip_sum(v_win, s_in, "w_in"), chip_sum(v_wa, s_wa, "lru_wa"), chip_sum(v_wx, s_wx, "lru_wx")
    d_h1, ((parts_in, parts_wa, parts_wx),) = _mm_nt(d_proj, w_in_g, name="bwd_proj_in",
                                                     exchanges=[_swap_with_chips(c_in + c_wa + c_wx)])
    grad_x, _, d_g_mix = _rmsnorm_bwd(x2d, g_mix, d_h1, d_x1, name="norm_mix_bwd")
    grad_parts = dict(w_in=[parts_in], p_sg=[parts_sg], p_lru=[parts_lru], w_out=parts_out, w_up=parts_up_top + [parts_up_bottom],
                      w_down=parts_down, lru_wa=[parts_wa], lru_wx=[parts_wx])

    small = ["g_mix", "sg_ln_g", "sg_ln_b", "sg_w", "sg_b", "lru_conv_b", "lru_ba", "lru_bx", "lru_lam", "g_ffn",
             "ffn_conv_b", "g_final", "lru_conv_w", "ffn_conv_w"]
    small_parts = dict(g_mix=d_g_mix, sg_ln_g=d_ln_g, sg_ln_b=d_ln_b, sg_w=d_sg_w, sg_b=d_sg_b_t.T, lru_conv_b=d_lru_cb,
                       lru_ba=d_ba, lru_bx=d_bx, lru_lam=d_lam, g_ffn=d_g_ffn, ffn_conv_b=d_ffn_cb, g_final=d_g_final,
                       lru_conv_w=d_lru_cw, ffn_conv_w=d_ffn_cw)
    packed, sizes = _pack([small_parts[k] for k in small])
    (all_small,) = _all_gather([packed], name="gather_small_grads")
    small_sum = _sum_parts(all_small, name="sum_small_grads")
    small_grads = dict(zip(small, _unpack(small_sum, sizes, [small_parts[k].shape for k in small])))
    for k in ("lru_conv_w", "ffn_conv_w"):
        n_loc = weights[k].shape[-1]
        small_grads[k] = lax.dynamic_slice_in_dim(small_grads[k], dev * n_loc, n_loc, axis=1)

    grads, deltas, new_m, new_v = {}, {}, {}, {}
    for k in order:
        w = weights[k]
        chunks = grad_parts[k] if k in grad_parts else [small_grads[k].reshape((1,) + w.shape)]
        grads[k], deltas[k], new_m[k], new_v[k] = _adamw(w, m_in[k], v_in[k], chunks, name=f"adamw_{k}")

    return (loss, grad_x.reshape(x.shape), *[grads[k] for k in order], *[deltas[k] for k in order],
            *[new_m[k] for k in order], *[new_v[k] for k in order])
```

```python
import functools
import math
from typing import Callable, NamedTuple

import jax
import jax.numpy as jnp
from jax import lax
from jax.experimental import pallas as pl
from jax.experimental.pallas import tpu as pltpu

F32 = jnp.float32
BF16 = jnp.bfloat16
MESH = pl.DeviceIdType.MESH
ANY = pl.BlockSpec(memory_space=pl.ANY)

N_DEV = 8
EPS = 1e-6
CHUNK = 128
SG_GROUPS = 8
LRU_HEADS = 16
LRU_C = 8.0
ADAM_LR = 0.001
ADAM_B1 = 0.9
ADAM_B2 = 0.999
ADAM_EPS = 1e-08
ADAM_WD = 0.01
ADAM_STEP = 10

V7X_VMEM_LIMIT = 56 * 1024 * 1024
LANES = 128
SUBLANES = 8
MXU = 256

_GELU_C0 = math.sqrt(2.0 / math.pi)
_GELU_C1 = 0.044715


def _params(n_axes):
    return pltpu.CompilerParams(dimension_semantics=("arbitrary",) * n_axes, vmem_limit_bytes=V7X_VMEM_LIMIT)


def _tile(dim, pref, align):
    t = (min(pref, dim) // align) * align
    while t >= align:
        if dim % t == 0:
            return t
        t -= align
    return dim


def _gelu(x):
    return x * (0.5 * (1.0 + jnp.tanh(_GELU_C0 * (x + _GELU_C1 * (x * x * x)))))


def _gelu_and_grad(x):
    t = jnp.tanh(_GELU_C0 * (x + _GELU_C1 * (x * x * x)))
    cdf = 0.5 * (1.0 + t)
    dcdf = 0.5 * (1.0 - t * t) * (_GELU_C0 * (1.0 + 3.0 * _GELU_C1 * (x * x)))
    return x * cdf, cdf + x * dcdf


def _sigmoid(x):
    return 1.0 / (1.0 + jnp.exp(-x))


def _shift_down(x, d):
    if d == 0:
        return x
    row = lax.broadcasted_iota(jnp.int32, x.shape, 0)
    return jnp.where(row >= d, pltpu.roll(x, d, 0), 0.0)


def _shift_up(x, d):
    if d == 0:
        return x
    s = x.shape[0]
    row = lax.broadcasted_iota(jnp.int32, x.shape, 0)
    return jnp.where(row < s - d, pltpu.roll(x, s - d, 0), 0.0)


def _causal_conv(x, w, b):
    k_taps = w.shape[0]
    out = _shift_down(x, k_taps - 1) * w[0:1, :]
    for k in range(1, k_taps):
        out = out + _shift_down(x, k_taps - 1 - k) * w[k:k + 1, :]
    return out + b


def _causal_conv_bwd_x(d_out, w):
    k_taps = w.shape[0]
    d_x = _shift_up(d_out, k_taps - 1) * w[0:1, :]
    for k in range(1, k_taps):
        d_x = d_x + _shift_up(d_out, k_taps - 1 - k) * w[k:k + 1, :]
    return d_x


def _causal_conv_bwd_w(d_out, x, k_taps):
    rows = [jnp.sum(d_out * _shift_down(x, k_taps - 1 - k), axis=0, keepdims=True) for k in range(k_taps)]
    return jnp.concatenate(rows, axis=0)


def _place():
    return lax.axis_index("x"), lax.axis_index("y"), lax.axis_index("c")


def _other_chips(x, y):
    return [(1 - x, y), (x, 1 - y), (1 - x, 1 - y)]


class _Exchange(NamedTuple):
    ins: tuple
    outs: tuple
    in_place: bool
    n_remote: int
    n_local: int
    copies: Callable


def _remote(src, dst, send_sem, recv_sem, to):
    return pltpu.make_async_remote_copy(src_ref=src, dst_ref=dst, send_sem=send_sem, recv_sem=recv_sem, device_id=to,
                                        device_id_type=MESH)


def _gather_first_leg(shards):
    n = len(shards)

    def copies(ins, outs, send_sems, recv_sems, local_sems):
        x, y, c = _place()
        peers = [(x, y, 1 - c)] + [(*chip, c) for chip in _other_chips(x, y)]
        slot = lambda i, dev: outs[i].at[4 * dev[0] + 2 * dev[1] + dev[2]]
        sends = [_remote(ins[i], slot(i, (x, y, c)), send_sems.at[i, k], recv_sems.at[i, k], to)
                 for i in range(n) for k, to in enumerate(peers)]
        receives = [_remote(ins[i], slot(i, frm), send_sems.at[i, k], recv_sems.at[i, k], frm)
                    for i in range(n) for k, frm in enumerate(peers)]
        local = [pltpu.make_async_copy(ins[i], slot(i, (x, y, c)), local_sems.at[i, 0]) for i in range(n)]
        return sends, receives, local

    outs = tuple(jax.ShapeDtypeStruct((N_DEV,) + s.shape, s.dtype) for s in shards)
    return _Exchange(tuple(shards), outs, False, 4, 1, copies)


def _gather_second_leg(gathered):
    n = len(gathered)

    def copies(ins, outs, send_sems, recv_sems):
        x, y, c = _place()
        slot = lambda i, chip, core: outs[i].at[4 * chip[0] + 2 * chip[1] + core]
        sends = [_remote(slot(i, chip, c), slot(i, chip, c), send_sems.at[i, j], recv_sems.at[i, j], (x, y, 1 - c))
                 for i in range(n) for j, chip in enumerate(_other_chips(x, y))]
        receives = [_remote(slot(i, chip, 1 - c), slot(i, chip, 1 - c), send_sems.at[i, j], recv_sems.at[i, j], (x, y, 1 - c))
                    for i in range(n) for j, chip in enumerate(_other_chips(x, y))]
        return sends, receives, []

    outs = tuple(jax.ShapeDtypeStruct(g.shape, g.dtype) for g in gathered)
    return _Exchange(tuple(gathered), outs, True, 3, 0, copies)


def _swap_with_sibling(parts):
    n = len(parts)

    def copies(ins, outs, send_sems, recv_sems):
        x, y, c = _place()
        both = [_remote(ins[i].at[ch, 1 - c], outs[i].at[ch], send_sems.at[i, ch], recv_sems.at[i, ch], (x, y, 1 - c))
                for i in range(n) for ch in range(4)]
        return both, both, []

    outs = tuple(jax.ShapeDtypeStruct((4,) + p.shape[2:], p.dtype) for p in parts)
    return _Exchange(tuple(parts), outs, False, 4, 0, copies)


def _swap_with_chips(parts):
    n = len(parts)

    def copies(ins, outs, send_sems, recv_sems, local_sems):
        x, y, c = _place()
        both = [_remote(ins[i].at[2 * chip[0] + chip[1]], outs[i].at[1 + j], send_sems.at[i, j], recv_sems.at[i, j], (*chip, c))
                for i in range(n) for j, chip in enumerate(_other_chips(x, y))]
        local = [pltpu.make_async_copy(ins[i].at[2 * x + y], outs[i].at[0], local_sems.at[i, 0]) for i in range(n)]
        return both, both, local

    outs = tuple(jax.ShapeDtypeStruct(p.shape, p.dtype) for p in parts)
    return _Exchange(tuple(parts), outs, False, 3, 1, copies)


def _exchange_plumbing(exchanges):
    operands = [a for ex in exchanges for a in ex.ins]
    results = [s for ex in exchanges for s in ex.outs]
    scratch, in_place, at = [], {}, 0
    for ex in exchanges:
        n = len(ex.ins)
        scratch += [pltpu.SemaphoreType.DMA((n, ex.n_remote))] * 2
        if ex.n_local:
            scratch.append(pltpu.SemaphoreType.DMA((n, ex.n_local)))
        if ex.in_place:
            in_place.update({at + i: at + i for i in range(n)})
        at += n

    def copies(in_refs, out_refs, sem_refs):
        sends, receives, local = [], [], []
        at, sem_at = 0, 0
        for ex in exchanges:
            n, n_sem = len(ex.ins), 3 if ex.n_local else 2
            s, r, l = ex.copies(in_refs[at:at + n], out_refs[at:at + n], *sem_refs[sem_at:sem_at + n_sem])
            sends, receives, local = sends + s, receives + r, local + l
            at, sem_at = at + n, sem_at + n_sem
        return sends, receives, local

    return operands, results, scratch, in_place, copies


def _start_all(copies):
    sends, _, local = copies
    for cp in local + sends:
        cp.start()


def _wait_all(copies):
    sends, receives, local = copies
    for cp in receives:
        cp.wait_recv()
    for cp in sends:
        cp.wait_send()
    for cp in local:
        cp.wait()


def _exchange_now(exchanges, *, name):
    operands, results, scratch, in_place, copies = _exchange_plumbing(exchanges)
    n = len(operands)

    def body(*refs):
        made = copies(refs[:n], refs[n:2 * n], refs[2 * n:])
        _start_all(made)
        _wait_all(made)

    out = pl.pallas_call(body, name=name, in_specs=[ANY] * n, out_specs=[ANY] * n, out_shape=results,
                         scratch_shapes=scratch, input_output_aliases=in_place)(*operands)
    return _split(out, exchanges)


def _split(flat, exchanges):
    out, at = [], 0
    for ex in exchanges:
        out.append(list(flat[at:at + len(ex.ins)]))
        at += len(ex.ins)
    return out


HBM = pl.BlockSpec(memory_space=pltpu.HBM)
SEMAPHORES = pl.BlockSpec(memory_space=pltpu.SEMAPHORE)
SPLIT_COPY = pltpu.CompilerParams(has_side_effects=pltpu.SideEffectType.DATAFLOW_SIDE_EFFECTING)


def _first_leg_copies(shard_refs, landing_refs, send_sems, recv_sems):
    x, y, c = _place()
    peers = [(x, y, 1 - c)] + [(*chip, c) for chip in _other_chips(x, y)]
    slot = lambda i, dev: landing_refs[i].at[4 * dev[0] + 2 * dev[1] + dev[2]]
    n = len(shard_refs)
    sends = [_remote(shard_refs[i], slot(i, (x, y, c)), send_sems[i].at[k], recv_sems[i].at[k], to)
             for i in range(n) for k, to in enumerate(peers)]
    receives = [_remote(shard_refs[i], slot(i, frm), send_sems[i].at[k], recv_sems[i].at[k], frm)
                for i in range(n) for k, frm in enumerate(peers)]
    return sends, receives


def _start_first_legs(shards, dev, *, name):
    n = len(shards)
    landings = [lax.dynamic_update_index_in_dim(lax.empty((N_DEV,) + s.shape, s.dtype), s, dev, 0) for s in shards]

    def body(*refs):
        sends, _ = _first_leg_copies(refs[:n], refs[n:2 * n], refs[2 * n:3 * n], refs[3 * n:4 * n])
        for cp in sends:
            cp.start()

    buffers = [pltpu.with_memory_space_constraint(a, pltpu.HBM) for a in list(shards) + landings]
    out = pl.pallas_call(
        body, name=name, in_specs=[HBM] * (2 * n), out_specs=[SEMAPHORES] * (2 * n) + [HBM] * (2 * n),
        out_shape=[pltpu.SemaphoreType.DMA((4,))] * (2 * n) + [pltpu.HBM(a.shape, a.dtype) for a in buffers],
        input_output_aliases={i: 2 * n + i for i in range(2 * n)}, compiler_params=SPLIT_COPY)(*buffers)
    return [(out[i], out[n + i], out[2 * n + i], out[3 * n + i]) for i in range(n)]


def _await_first_legs(in_flight, after, *, name):
    n = len(in_flight)
    send_sems, recv_sems, shards, landings = zip(*in_flight)

    def body(*refs):
        sends, receives = _first_leg_copies(refs[:n], refs[n:2 * n], refs[2 * n:3 * n], refs[3 * n:4 * n])
        for cp in sends:
            cp.wait_send()
        for cp in receives:
            cp.wait_recv()

    out = pl.pallas_call(
        body, name=name, in_specs=[HBM] * (2 * n) + [SEMAPHORES] * (2 * n) + [ANY], out_specs=[HBM] * (2 * n),
        out_shape=[pltpu.HBM(a.shape, a.dtype) for a in shards + landings],
        input_output_aliases={i: i for i in range(2 * n)}, compiler_params=SPLIT_COPY,
    )(*shards, *landings, *send_sems, *recv_sems, after)
    return list(out[n:])


def _pallas(kern, *, name, grid, in_specs, out_specs, out_shape, operands, scratch_shapes=(), exchanges=()):
    ex_operands, ex_results, ex_scratch, in_place, copies = _exchange_plumbing(exchanges)
    n_in, n_out, n_scratch, n_ex = len(in_specs), len(out_specs), len(scratch_shapes), len(ex_operands)

    def body(*refs):
        ins, refs = refs[:n_in], refs[n_in:]
        ex_ins, refs = refs[:n_ex], refs[n_ex:]
        outs, refs = refs[:n_out], refs[n_out:]
        ex_outs, refs = refs[:n_ex], refs[n_ex:]
        scratch, sems = refs[:n_scratch], refs[n_scratch:]
        if exchanges:
            first = functools.reduce(jnp.logical_and, [pl.program_id(a) == 0 for a in range(len(grid))])
            last = functools.reduce(jnp.logical_and, [pl.program_id(a) == g - 1 for a, g in enumerate(grid)])

            @pl.when(first)
            def _():
                _start_all(copies(ex_ins, ex_outs, sems))

        kern(*ins, *outs, *scratch)
        if exchanges:
            @pl.when(last)
            def _():
                _wait_all(copies(ex_ins, ex_outs, sems))

    res = pl.pallas_call(
        body, name=name, grid=grid, in_specs=list(in_specs) + [ANY] * n_ex, out_specs=list(out_specs) + [ANY] * n_ex,
        out_shape=list(out_shape) + ex_results, scratch_shapes=list(scratch_shapes) + ex_scratch,
        input_output_aliases={n_in + i: n_out + o for i, o in in_place.items()},
        compiler_params=_params(len(grid)))(*operands, *ex_operands)
    return list(res[:n_out]), _split(res[n_out:], exchanges)


def _accumulate(step, n_steps, acc, value, finish):
    if n_steps == 1:
        finish(value)
        return

    @pl.when(step == 0)
    def _():
        acc[0][...] = value

    @pl.when(step > 0)
    def _():
        acc[0][...] += value

    @pl.when(step == n_steps - 1)
    def _():
        finish(acc[0][...])


def _mm_nn(a, w, *, out_dtype, name, residual=None, exchanges=()):
    m, k = a.shape
    nb, _, n_blk = w.shape
    tm, tn, tk = _tile(m, 512, MXU), _tile(n_blk, 1536, MXU), _tile(k, 4096, MXU)
    per = n_blk // tn
    nk = k // tk

    def kern(*refs):
        a_ref, w_ref = refs[:2]
        r_ref = None if residual is None else refs[2]
        o_ref, acc = refs[2 + (residual is not None)], refs[3 + (residual is not None):]

        def finish(total):
            o_ref[...] = (total if r_ref is None else total + r_ref[...]).astype(o_ref.dtype)

        _accumulate(pl.program_id(2), nk, acc, jnp.dot(a_ref[...], w_ref[...], preferred_element_type=F32), finish)

    tile = pl.BlockSpec((tm, tn), lambda j, i, kk: (i, j))
    in_specs = [pl.BlockSpec((tm, tk), lambda j, i, kk: (i, kk)),
                pl.BlockSpec((None, tk, tn), lambda j, i, kk: (j // per, kk, j % per))]
    operands = [a, w]
    if residual is not None:
        in_specs.append(tile)
        operands.append(residual)
    (out,), carried = _pallas(
        kern, name=name, grid=(nb * per, m // tm, nk), in_specs=in_specs, out_specs=[tile],
        out_shape=[jax.ShapeDtypeStruct((m, nb * n_blk), out_dtype)], operands=operands,
        scratch_shapes=[pltpu.VMEM((tm, tn), F32)] * (nk > 1), exchanges=exchanges)
    return out, carried


def _mm_nt(g, w, *, name, exchanges=()):
    m, n = g.shape
    nb, k, n_blk = w.shape
    tm, tko, tn = _tile(m, 1024, MXU), _tile(k, 1024, MXU), _tile(n_blk, 3072, MXU)
    per = n_blk // tn
    nn = n // tn

    def kern(g_ref, w_ref, o_ref, *acc):
        def finish(total):
            o_ref[...] = total

        part = lax.dot_general(g_ref[...], w_ref[...], (((1,), (1,)), ((), ())), preferred_element_type=F32)
        _accumulate(pl.program_id(2), nn, acc, part, finish)

    (out,), carried = _pallas(
        kern, name=name, grid=(k // tko, m // tm, nn),
        in_specs=[pl.BlockSpec((tm, tn), lambda j, i, jn: (i, jn)),
                  pl.BlockSpec((None, tko, tn), lambda j, i, jn: (jn // per, j, jn % per))],
        out_specs=[pl.BlockSpec((tm, tko), lambda j, i, jn: (i, j))],
        out_shape=[jax.ShapeDtypeStruct((m, k), F32)], operands=[g, w],
        scratch_shapes=[pltpu.VMEM((tm, tko), F32)] * (nn > 1), exchanges=exchanges)
    return out, carried


def _mm_tn(a, g, nb, *, name, exchanges=()):
    m, k = a.shape
    n = g.shape[1]
    n_blk = n // nb
    tko, tn, tm = _tile(k, 512, MXU), _tile(n_blk, 1536, MXU), _tile(m, 4096, MXU)
    per = n_blk // tn
    nm = m // tm

    def kern(a_ref, g_ref, o_ref, *acc):
        def finish(total):
            o_ref[...] = total.astype(o_ref.dtype)

        part = lax.dot_general(a_ref[...], g_ref[...], (((0,), (0,)), ((), ())), preferred_element_type=F32)
        _accumulate(pl.program_id(2), nm, acc, part, finish)

    (out,), carried = _pallas(
        kern, name=name, grid=(nb * per, k // tko, nm),
        in_specs=[pl.BlockSpec((tm, tko), lambda j, i, im: (im, i)),
                  pl.BlockSpec((tm, tn), lambda j, i, im: (im, j))],
        out_specs=[pl.BlockSpec((None, tko, tn), lambda j, i, im: (j // per, i, j % per))],
        out_shape=[jax.ShapeDtypeStruct((nb, k, n_blk), BF16)], operands=[a, g],
        scratch_shapes=[pltpu.VMEM((tko, tn), F32)] * (nm > 1), exchanges=exchanges)
    return out, carried


ROW_TILE = 128


def _rmsnorm_fwd(x, g, *, name):
    t, d = x.shape
    tr = _tile(t, ROW_TILE, SUBLANES)

    def kern(x_ref, g_ref, h_ref):
        xv = x_ref[...]
        r = lax.rsqrt(jnp.mean(xv * xv, axis=-1, keepdims=True) + EPS)
        h_ref[...] = (xv * r * g_ref[...]).astype(BF16)

    return pl.pallas_call(
        kern, name=name, grid=(t // tr,),
        in_specs=[pl.BlockSpec((tr, d), lambda i: (i, 0)), pl.BlockSpec((1, d), lambda i: (0, 0))],
        out_specs=pl.BlockSpec((tr, d), lambda i: (i, 0)),
        out_shape=jax.ShapeDtypeStruct((t, d), BF16), compiler_params=_params(1))(x, g)


def _rmsnorm_bwd(x, g, d_h, d_res, *, name):
    t, d = x.shape
    tr = _tile(t, ROW_TILE, SUBLANES)

    def kern(x_ref, g_ref, dh_ref, dres_ref, dx_ref, dxb_ref, dg_ref):
        xv = x_ref[...]
        r = lax.rsqrt(jnp.mean(xv * xv, axis=-1, keepdims=True) + EPS)
        dh = dh_ref[...]
        gy = dh * g_ref[...]
        dx = dres_ref[...] + r * gy - xv * (r * r * r) * jnp.mean(gy * xv, axis=-1, keepdims=True)
        dx_ref[...] = dx
        dxb_ref[...] = dx.astype(BF16)

        @pl.when(pl.program_id(0) == 0)
        def _():
            dg_ref[...] = jnp.zeros_like(dg_ref)

        dg_ref[...] += jnp.sum(dh * (xv * r), axis=0, keepdims=True)

    row = pl.BlockSpec((tr, d), lambda i: (i, 0))
    vec = pl.BlockSpec((1, d), lambda i: (0, 0))
    return pl.pallas_call(
        kern, name=name, grid=(t // tr,), in_specs=[row, vec, row, row], out_specs=[row, row, vec],
        out_shape=[jax.ShapeDtypeStruct((t, d), F32), jax.ShapeDtypeStruct((t, d), BF16),
                   jax.ShapeDtypeStruct((1, d), F32)], compiler_params=_params(1))(x, g, d_h, d_res)


def _loss_head(x, g, target, *, name):
    t, d = x.shape
    tr = _tile(t, ROW_TILE, SUBLANES)

    def kern(x_ref, g_ref, t_ref, dx_ref, dxb_ref, dg_ref, loss_ref):
        xv = x_ref[...]
        gv = g_ref[...]
        r = lax.rsqrt(jnp.mean(xv * xv, axis=-1, keepdims=True) + EPS)
        diff = xv * r * gv - t_ref[...]
        dy = diff * (1.0 / d)
        gy = dy * gv
        dx = r * gy - xv * (r * r * r) * jnp.mean(gy * xv, axis=-1, keepdims=True)
        dx_ref[...] = dx
        dxb_ref[...] = dx.astype(BF16)

        @pl.when(pl.program_id(0) == 0)
        def _():
            dg_ref[...] = jnp.zeros_like(dg_ref)
            loss_ref[...] = jnp.zeros_like(loss_ref)

        dg_ref[...] += jnp.sum(dy * (xv * r), axis=0, keepdims=True)
        part = 0.5 * jnp.sum(jnp.mean(diff * diff, axis=-1, keepdims=True), axis=0, keepdims=True)
        loss_ref[...] += jnp.broadcast_to(part, loss_ref.shape)

    row = pl.BlockSpec((tr, d), lambda i: (i, 0))
    vec = pl.BlockSpec((1, d), lambda i: (0, 0))
    return pl.pallas_call(
        kern, name=name, grid=(t // tr,), in_specs=[row, vec, row],
        out_specs=[row, row, vec, pl.BlockSpec((1, LANES), lambda i: (0, 0))],
        out_shape=[jax.ShapeDtypeStruct((t, d), F32), jax.ShapeDtypeStruct((t, d), BF16),
                   jax.ShapeDtypeStruct((1, d), F32), jax.ShapeDtypeStruct((1, LANES), F32)],
        compiler_params=_params(1))(x, g, target)


def _merge_fwd(m_a, m_b, proj, gate_col, *, name):
    t, d = m_a.shape
    tr = _tile(t, ROW_TILE, SUBLANES)

    def kern(ma_ref, mb_ref, ga_ref, gb_ref, o_ref):
        o_ref[...] = (_sigmoid(ga_ref[...]) * ma_ref[...] + _sigmoid(gb_ref[...]) * mb_ref[...]).astype(BF16)

    row = pl.BlockSpec((tr, d), lambda i: (i, 0))
    return pl.pallas_call(
        kern, name=name, grid=(t // tr,),
        in_specs=[row, row, pl.BlockSpec((tr, d), lambda i: (i, gate_col)),
                  pl.BlockSpec((tr, d), lambda i: (i, gate_col + 1))],
        out_specs=row, out_shape=jax.ShapeDtypeStruct((t, d), BF16), compiler_params=_params(1))(m_a, m_b, proj, proj)


def _merge_bwd(d_merged, m_a, m_b, proj, gate_col, *, name):
    t, d = m_a.shape
    tr = _tile(t, ROW_TILE, SUBLANES)

    def kern(dm_ref, ma_ref, mb_ref, ga_ref, gb_ref, dma_ref, dmb_ref, dg_ref):
        dm = dm_ref[...]
        sa = _sigmoid(ga_ref[...])
        sb = _sigmoid(gb_ref[...])
        dma_ref[...] = (dm * sa).astype(BF16)
        dmb_ref[...] = (dm * sb).astype(BF16)
        dg_ref[:, 0:d] = (dm * ma_ref[...] * (sa * (1.0 - sa))).astype(BF16)
        dg_ref[:, d:2 * d] = (dm * mb_ref[...] * (sb * (1.0 - sb))).astype(BF16)

    row = pl.BlockSpec((tr, d), lambda i: (i, 0))
    return pl.pallas_call(
        kern, name=name, grid=(t // tr,),
        in_specs=[row, row, row, pl.BlockSpec((tr, d), lambda i: (i, gate_col)),
                  pl.BlockSpec((tr, d), lambda i: (i, gate_col + 1))],
        out_specs=[row, row, pl.BlockSpec((tr, 2 * d), lambda i: (i, 0))],
        out_shape=[jax.ShapeDtypeStruct((t, d), BF16), jax.ShapeDtypeStruct((t, d), BF16),
                   jax.ShapeDtypeStruct((t, 2 * d), BF16)], compiler_params=_params(1))(d_merged, m_a, m_b, proj, proj)


def _tril_bf16(w, transposed):
    row = lax.broadcasted_iota(jnp.int32, w.shape, 0)
    col = lax.broadcasted_iota(jnp.int32, w.shape, 1)
    keep = (row <= col) if transposed else (row >= col)
    return jnp.where(keep, w, 0.0).astype(BF16)


def _layernorm_stats(v):
    mu = jnp.mean(v, axis=-1, keepdims=True)
    vc = v - mu
    rstd = lax.rsqrt(jnp.mean(vc * vc, axis=-1, keepdims=True) + EPS)
    return vc * rstd, rstd


def _mixer_a_fwd(proj, ln_g, ln_b, sg_w, sg_b_t, sgw, *, name):
    t = proj.shape[0]
    gd = sgw // SG_GROUPS

    def kern(zu_ref, zv_ref, g_ref, b_ref, w_ref, bt_ref, o_ref):
        xhat, _ = _layernorm_stats(_gelu(zv_ref[...]))
        vn = (xhat * g_ref[...] + b_ref[...]).astype(BF16)
        for g in range(SG_GROUPS):
            cols = slice(g * gd, (g + 1) * gd)
            mixed = jnp.dot(_tril_bf16(w_ref[g], False), vn[:, cols], preferred_element_type=F32) + bt_ref[:, g:g + 1]
            o_ref[:, cols] = (_gelu(zu_ref[:, cols]) * mixed).astype(BF16)

    vec = pl.BlockSpec((1, sgw), lambda i: (0, 0))
    return pl.pallas_call(
        kern, name=name, grid=(t // CHUNK,),
        in_specs=[pl.BlockSpec((CHUNK, sgw), lambda i: (i, 0)), pl.BlockSpec((CHUNK, sgw), lambda i: (i, 1)), vec, vec,
                  pl.BlockSpec((SG_GROUPS, CHUNK, CHUNK), lambda i: (0, 0, 0)),
                  pl.BlockSpec((CHUNK, SG_GROUPS), lambda i: (0, 0))],
        out_specs=pl.BlockSpec((CHUNK, sgw), lambda i: (i, 0)),
        out_shape=jax.ShapeDtypeStruct((t, sgw), BF16), compiler_params=_params(1))(proj, proj, ln_g, ln_b, sg_w, sg_b_t)


def _mixer_a_bwd(proj, d_ya, ln_g, ln_b, sg_w, sg_w_t, sg_b_t, sgw, *, name):
    t = proj.shape[0]
    gd = sgw // SG_GROUPS

    def kern(zu_ref, zv_ref, dy_ref, g_ref, b_ref, w_ref, wt_ref, bt_ref, dz_ref, dw_ref, dbt_ref, dg_ref, db_ref, dvn):
        @pl.when(pl.program_id(0) == 0)
        def _():
            dw_ref[...] = jnp.zeros_like(dw_ref)
            dbt_ref[...] = jnp.zeros_like(dbt_ref)
            dg_ref[...] = jnp.zeros_like(dg_ref)
            db_ref[...] = jnp.zeros_like(db_ref)

        gv, dgv = _gelu_and_grad(zv_ref[...])
        xhat, rstd = _layernorm_stats(gv)
        ln_gain = g_ref[...]
        vn = (xhat * ln_gain + b_ref[...]).astype(BF16)
        for g in range(SG_GROUPS):
            cols = slice(g * gd, (g + 1) * gd)
            gu, dgu = _gelu_and_grad(zu_ref[:, cols])
            mixed = jnp.dot(_tril_bf16(w_ref[g], False), vn[:, cols], preferred_element_type=F32) + bt_ref[:, g:g + 1]
            dy = dy_ref[:, cols]
            dz_ref[:, cols] = (dy * mixed * dgu).astype(BF16)
            d_mixed = dy * gu
            d_mixed_b = d_mixed.astype(BF16)
            dvn[:, cols] = jnp.dot(_tril_bf16(wt_ref[g], True), d_mixed_b, preferred_element_type=F32)
            d_w = lax.dot_general(d_mixed_b, vn[:, cols], (((1,), (1,)), ((), ())), preferred_element_type=F32)
            row = lax.broadcasted_iota(jnp.int32, d_w.shape, 0)
            col = lax.broadcasted_iota(jnp.int32, d_w.shape, 1)
            dw_ref[g] += jnp.where(row >= col, d_w, 0.0)
            dbt_ref[:, g:g + 1] += jnp.sum(d_mixed, axis=-1, keepdims=True)
        d_vn = dvn[...]
        dg_ref[...] += jnp.sum(d_vn * xhat, axis=0, keepdims=True)
        db_ref[...] += jnp.sum(d_vn, axis=0, keepdims=True)
        d_xhat = d_vn * ln_gain
        d_gv = rstd * (d_xhat - jnp.mean(d_xhat, axis=-1, keepdims=True)
                       - xhat * jnp.mean(d_xhat * xhat, axis=-1, keepdims=True))
        dz_ref[:, sgw:2 * sgw] = (d_gv * dgv).astype(BF16)

    vec = pl.BlockSpec((1, sgw), lambda i: (0, 0))
    wspec = pl.BlockSpec((SG_GROUPS, CHUNK, CHUNK), lambda i: (0, 0, 0))
    btspec = pl.BlockSpec((CHUNK, SG_GROUPS), lambda i: (0, 0))
    return pl.pallas_call(
        kern, name=name, grid=(t // CHUNK,),
        in_specs=[pl.BlockSpec((CHUNK, sgw), lambda i: (i, 0)), pl.BlockSpec((CHUNK, sgw), lambda i: (i, 1)),
                  pl.BlockSpec((CHUNK, sgw), lambda i: (i, 0)), vec, vec, wspec, wspec, btspec],
        out_specs=[pl.BlockSpec((CHUNK, 2 * sgw), lambda i: (i, 0)), wspec, btspec, vec, vec],
        out_shape=[jax.ShapeDtypeStruct((t, 2 * sgw), BF16), jax.ShapeDtypeStruct((SG_GROUPS, CHUNK, CHUNK), F32),
                   jax.ShapeDtypeStruct((CHUNK, SG_GROUPS), F32), jax.ShapeDtypeStruct((1, sgw), F32),
                   jax.ShapeDtypeStruct((1, sgw), F32)],
        scratch_shapes=[pltpu.VMEM((CHUNK, sgw), F32)],
        compiler_params=_params(1))(proj, proj, d_ya, ln_g, ln_b, sg_w, sg_w_t, sg_b_t)


def _scan_rows(a_ref, h_ref, reverse):
    s, c = a_ref.shape
    nblk = s // SUBLANES
    a, b = a_ref[...], h_ref[...]
    row = jnp.bitwise_and(lax.broadcasted_iota(jnp.int32, (s, c), 0), SUBLANES - 1)
    for d in (1, 2, 4):
        inside = (row < SUBLANES - d) if reverse else (row >= d)
        shift = s - d if reverse else d
        b = a * jnp.where(inside, pltpu.roll(b, shift, 0), 0.0) + b
        a = a * jnp.where(inside, pltpu.roll(a, shift, 0), 1.0)
    a_ref[...] = a
    h_ref[...] = b
    leaving = 0 if reverse else SUBLANES - 1

    def chain(i, carry):
        r0 = pl.multiple_of((nblk - 1 - i if reverse else i) * SUBLANES, SUBLANES)
        h = a_ref[pl.ds(r0, SUBLANES), :] * carry + h_ref[pl.ds(r0, SUBLANES), :]
        h_ref[pl.ds(r0, SUBLANES), :] = h
        return jnp.broadcast_to(h[leaving:leaving + 1, :], (SUBLANES, c))

    lax.fori_loop(0, nblk, chain, jnp.zeros((SUBLANES, c), F32))


def _lru_gates(xc, wa_ref, ba_ref, wx_ref, bx_ref, lam_ref):
    xcb = xc.astype(BF16)
    ra = _sigmoid(jnp.dot(xcb, wa_ref[...].astype(BF16), preferred_element_type=F32) + ba_ref[...])
    ia = _sigmoid(jnp.dot(xcb, wx_ref[...].astype(BF16), preferred_element_type=F32) + bx_ref[...])
    neg = -lam_ref[...]
    sp = jnp.maximum(neg, 0.0) + jnp.log1p(jnp.exp(-jnp.abs(neg)))
    log_a = -LRU_C * ra * sp
    a = jnp.exp(log_a)
    a2 = jnp.exp(2.0 * log_a)
    sq = jnp.sqrt(-jnp.tanh(log_a) * (a2 + 1.0))
    return ra, ia, sp, a, a2, sq


def _mixer_b_specs(seq, hd, sgw, lw):
    x_col = (2 * sgw) // hd
    y_col = (2 * sgw + lw) // hd
    tile = lambda col: pl.BlockSpec((seq, hd), lambda h, b: (b, col + h))
    vec = pl.BlockSpec((1, hd), lambda h, b: (0, h))
    mat = pl.BlockSpec((None, hd, hd), lambda h, b: (h, 0, 0))
    return tile(x_col), tile(y_col), tile(0), vec, mat


def _mixer_b_fwd(proj, conv_w, conv_b, wa, ba, wx, bx, lam, *, seq, sgw, lw, name):
    t = proj.shape[0]
    hd = lw // LRU_HEADS
    k_taps = conv_w.shape[0]
    x_spec, y_spec, o_spec, vec, mat = _mixer_b_specs(seq, hd, sgw, lw)

    def kern(xr_ref, yr_ref, cw_ref, cb_ref, wa_ref, ba_ref, wx_ref, bx_ref, lam_ref, o_ref, s_a, s_h):
        xc = _causal_conv(xr_ref[...], cw_ref[...], cb_ref[...])
        _, ia, _, a, _, sq = _lru_gates(xc, wa_ref, ba_ref, wx_ref, bx_ref, lam_ref)
        s_a[...] = a
        s_h[...] = sq * (ia * xc)
        _scan_rows(s_a, s_h, False)
        o_ref[...] = (s_h[...] * _gelu(yr_ref[...])).astype(BF16)

    return pl.pallas_call(
        kern, name=name, grid=(LRU_HEADS, t // seq),
        in_specs=[x_spec, y_spec, pl.BlockSpec((k_taps, hd), lambda h, b: (0, h)), vec, mat, vec, mat, vec, vec],
        out_specs=o_spec, out_shape=jax.ShapeDtypeStruct((t, lw), BF16),
        scratch_shapes=[pltpu.VMEM((seq, hd), F32), pltpu.VMEM((seq, hd), F32)],
        compiler_params=_params(2))(proj, proj, conv_w, conv_b, wa, ba, wx, bx, lam)


def _mixer_b_bwd(proj, d_yb, conv_w, conv_b, wa, wa_t, ba, wx, wx_t, bx, lam, *, seq, sgw, lw, name, exchanges=()):
    t = proj.shape[0]
    hd = lw // LRU_HEADS
    k_taps = conv_w.shape[0]
    x_spec, y_spec, o_spec, vec, mat = _mixer_b_specs(seq, hd, sgw, lw)
    cw_spec = pl.BlockSpec((k_taps, hd), lambda h, b: (0, h))

    def kern(xr_ref, yr_ref, dyb_ref, cw_ref, cb_ref, wa_ref, wat_ref, ba_ref, wx_ref, wxt_ref, bx_ref, lam_ref,
             dxr_ref, dyr_ref, dcw_ref, dcb_ref, dwa_ref, dba_ref, dwx_ref, dbx_ref, dlam_ref,
             s_xc, s_a, s_h, s_lam, s_dpa, s_dpx):
        @pl.when(pl.program_id(1) == 0)
        def _():
            for ref in (dcw_ref, dcb_ref, dwa_ref, dba_ref, dwx_ref, dbx_ref, dlam_ref):
                ref[...] = jnp.zeros_like(ref)

        s_xc[...] = _causal_conv(xr_ref[...], cw_ref[...], cb_ref[...])
        _, ia, _, a, _, sq = _lru_gates(s_xc[...], wa_ref, ba_ref, wx_ref, bx_ref, lam_ref)
        s_a[...] = a
        s_dpa[...] = _shift_up(a, 1)
        s_h[...] = sq * (ia * s_xc[...])
        _scan_rows(s_a, s_h, False)

        gel, dgel = _gelu_and_grad(yr_ref[...])
        dyb = dyb_ref[...]
        dyr_ref[...] = (dyb * s_h[...] * dgel).astype(BF16)
        s_lam[...] = dyb * gel
        _scan_rows(s_dpa, s_lam, True)
        ra, ia, sp, a, a2, sq = _lru_gates(s_xc[...], wa_ref, ba_ref, wx_ref, bx_ref, lam_ref)
        d_gx = s_lam[...]
        d_a = d_gx * _shift_down(s_h[...], 1)
        xc = s_xc[...]
        d_sq = d_gx * (ia * xc)
        d_ia = d_gx * (sq * xc)
        d_log_a = d_a * a - d_sq * (a2 / sq)
        d_ra = d_log_a * (-LRU_C * sp)
        d_sp = jnp.sum(d_log_a * (-LRU_C * ra), axis=0, keepdims=True)
        dlam_ref[...] += d_sp * (-_sigmoid(-lam_ref[...]))
        d_pa = d_ra * (ra * (1.0 - ra))
        d_px = d_ia * (ia * (1.0 - ia))
        s_dpa[...] = d_pa
        s_dpx[...] = d_px
        dba_ref[...] += jnp.sum(d_pa, axis=0, keepdims=True)
        dbx_ref[...] += jnp.sum(d_px, axis=0, keepdims=True)
        xcb = s_xc[...].astype(BF16)
        d_pa_b = s_dpa[...].astype(BF16)
        d_px_b = s_dpx[...].astype(BF16)
        contract_rows = (((0,), (0,)), ((), ()))
        dwa_ref[...] += lax.dot_general(xcb, d_pa_b, contract_rows, preferred_element_type=F32)
        dwx_ref[...] += lax.dot_general(xcb, d_px_b, contract_rows, preferred_element_type=F32)
        d_xc = (s_lam[...] * (sq * ia)
                + jnp.dot(d_pa_b, wat_ref[...].astype(BF16), preferred_element_type=F32)
                + jnp.dot(d_px_b, wxt_ref[...].astype(BF16), preferred_element_type=F32))
        dcb_ref[...] += jnp.sum(d_xc, axis=0, keepdims=True)
        dcw_ref[...] += _causal_conv_bwd_w(d_xc, xr_ref[...], k_taps)
        dxr_ref[...] = _causal_conv_bwd_x(d_xc, cw_ref[...]).astype(BF16)

    tile_shape = jax.ShapeDtypeStruct((t, lw), BF16)
    vec_shape = jax.ShapeDtypeStruct((1, lw), F32)
    mat_shape = jax.ShapeDtypeStruct((LRU_HEADS, hd, hd), F32)
    return _pallas(
        kern, name=name, grid=(LRU_HEADS, t // seq),
        in_specs=[x_spec, y_spec, o_spec, cw_spec, vec, mat, mat, vec, mat, mat, vec, vec],
        out_specs=[o_spec, o_spec, cw_spec, vec, mat, vec, mat, vec, vec],
        out_shape=[tile_shape, tile_shape, jax.ShapeDtypeStruct((k_taps, lw), F32), vec_shape, mat_shape, vec_shape,
                   mat_shape, vec_shape, vec_shape],
        operands=[proj, proj, d_yb, conv_w, conv_b, wa, wa_t, ba, wx, wx_t, bx, lam],
        scratch_shapes=[pltpu.VMEM((seq, hd), F32)] * 6, exchanges=exchanges)


FFN_TILE = 256


def _ffn_mid_fwd(up_pre, conv_w, conv_b, *, seq, name):
    t, f2 = up_pre.shape
    f = f2 // 2
    tc = _tile(f, FFN_TILE, LANES)
    nf = f // tc
    k_taps = conv_w.shape[0]

    def kern(pg_ref, pv_ref, wg_ref, wv_ref, bg_ref, bv_ref, o_ref):
        cg = _causal_conv(pg_ref[...], wg_ref[...], bg_ref[...])
        cv = _causal_conv(pv_ref[...], wv_ref[...], bv_ref[...])
        o_ref[...] = (_gelu(cg) * cv).astype(BF16)

    tile = lambda off: pl.BlockSpec((seq, tc), lambda j, b: (b, off + j))
    wspec = lambda off: pl.BlockSpec((k_taps, tc), lambda j, b: (0, off + j))
    bspec = lambda off: pl.BlockSpec((1, tc), lambda j, b: (0, off + j))
    return pl.pallas_call(
        kern, name=name, grid=(nf, t // seq),
        in_specs=[tile(0), tile(nf), wspec(0), wspec(nf), bspec(0), bspec(nf)], out_specs=tile(0),
        out_shape=jax.ShapeDtypeStruct((t, f), BF16),
        compiler_params=_params(2))(up_pre, up_pre, conv_w, conv_w, conv_b, conv_b)


def _ffn_mid_bwd(up_pre, d_act, conv_w, conv_b, *, seq, name):
    t, f2 = up_pre.shape
    f = f2 // 2
    tc = _tile(f, FFN_TILE, LANES)
    nf = f // tc
    k_taps = conv_w.shape[0]

    def kern(pg_ref, pv_ref, da_ref, wg_ref, wv_ref, bg_ref, bv_ref, dpg_ref, dpv_ref, dwg_ref, dwv_ref, dbg_ref, dbv_ref):
        @pl.when(pl.program_id(1) == 0)
        def _():
            for ref in (dwg_ref, dwv_ref, dbg_ref, dbv_ref):
                ref[...] = jnp.zeros_like(ref)

        pg = pg_ref[...]
        pv = pv_ref[...]
        gel, dgel = _gelu_and_grad(_causal_conv(pg, wg_ref[...], bg_ref[...]))
        cv = _causal_conv(pv, wv_ref[...], bv_ref[...])
        d_act_v = da_ref[...]
        d_cg = d_act_v * cv * dgel
        d_cv = d_act_v * gel
        dpg_ref[...] = _causal_conv_bwd_x(d_cg, wg_ref[...]).astype(BF16)
        dpv_ref[...] = _causal_conv_bwd_x(d_cv, wv_ref[...]).astype(BF16)
        dwg_ref[...] += _causal_conv_bwd_w(d_cg, pg, k_taps)
        dwv_ref[...] += _causal_conv_bwd_w(d_cv, pv, k_taps)
        dbg_ref[...] += jnp.sum(d_cg, axis=0, keepdims=True)
        dbv_ref[...] += jnp.sum(d_cv, axis=0, keepdims=True)

    tile = lambda off: pl.BlockSpec((seq, tc), lambda j, b: (b, off + j))
    wspec = lambda off: pl.BlockSpec((k_taps, tc), lambda j, b: (0, off + j))
    bspec = lambda off: pl.BlockSpec((1, tc), lambda j, b: (0, off + j))
    half = jax.ShapeDtypeStruct((t, f), BF16)
    wshape = jax.ShapeDtypeStruct((k_taps, f), F32)
    bshape = jax.ShapeDtypeStruct((1, f), F32)
    d_pg, d_pv, d_wg, d_wv, d_bg, d_bv = pl.pallas_call(
        kern, name=name, grid=(nf, t // seq),
        in_specs=[tile(0), tile(nf), tile(0), wspec(0), wspec(nf), bspec(0), bspec(nf)],
        out_specs=[tile(0), tile(0), wspec(0), wspec(0), bspec(0), bspec(0)],
        out_shape=[half, half, wshape, wshape, bshape, bshape],
        compiler_params=_params(2))(up_pre, up_pre, d_act, conv_w, conv_w, conv_b, conv_b)
    return (jnp.concatenate([d_pg, d_pv], axis=1), jnp.concatenate([d_wg, d_wv], axis=1),
            jnp.concatenate([d_bg, d_bv], axis=1))


ELEM_VMEM_BYTES = 24 << 20


def _as_2d(a):
    if a.ndim >= 2 and a.shape[-1] % LANES == 0 and a.size // a.shape[-1] >= SUBLANES:
        return a.reshape(-1, a.shape[-1])
    return a.reshape(-1, LANES)


def _row_tile(rows, bytes_per_row):
    return _tile(rows, max(16, ELEM_VMEM_BYTES // (2 * bytes_per_row)), 16)


def _cast_bf16(a, *, name):
    v = _as_2d(a)
    rows, cols = v.shape
    tr = _row_tile(rows, cols * (4 + 2))

    def kern(x_ref, o_ref):
        o_ref[...] = x_ref[...].astype(BF16)

    spec = pl.BlockSpec((tr, cols), lambda i: (i, 0))
    out = pl.pallas_call(kern, name=name, grid=(rows // tr,), in_specs=[spec], out_specs=spec,
                         out_shape=jax.ShapeDtypeStruct(v.shape, BF16), compiler_params=_params(1))(v)
    return out.reshape(a.shape)


def _add_sibling_part(own, core, got, *, name, chunk=0, n_chunks=1):
    _, _, all_rows, cols = own.shape
    rows = all_rows // n_chunks
    tr = _row_tile(rows, cols * (2 + 2 + 2))
    first = chunk * (rows // tr)

    def kern(core_ref, a_ref, b_ref, o_ref):
        o_ref[...] = (a_ref[...].astype(F32) + b_ref[...].astype(F32)).astype(BF16)

    grid_spec = pltpu.PrefetchScalarGridSpec(
        num_scalar_prefetch=1, grid=(4, rows // tr),
        in_specs=[pl.BlockSpec((None, None, tr, cols), lambda ch, i, core_ref: (ch, core_ref[0], first + i, 0)),
                  pl.BlockSpec((None, tr, cols), lambda ch, i, core_ref: (ch, first + i, 0))],
        out_specs=pl.BlockSpec((None, tr, cols), lambda ch, i, core_ref: (ch, i, 0)))
    return pl.pallas_call(kern, name=name, grid_spec=grid_spec, out_shape=jax.ShapeDtypeStruct((4, rows, cols), BF16),
                          compiler_params=_params(2))(core, own, got)


def _sum_parts(parts, *, name):
    n_parts, rows, cols = parts.shape
    tr = _row_tile(rows, cols * 4 * (n_parts + 1))

    def kern(p_ref, o_ref):
        acc = p_ref[0].astype(F32)
        for p in range(1, n_parts):
            acc = acc + p_ref[p].astype(F32)
        o_ref[...] = acc

    return pl.pallas_call(
        kern, name=name, grid=(rows // tr,), in_specs=[pl.BlockSpec((n_parts, tr, cols), lambda i: (0, i, 0))],
        out_specs=pl.BlockSpec((tr, cols), lambda i: (i, 0)), out_shape=jax.ShapeDtypeStruct((rows, cols), F32),
        compiler_params=_params(1))(parts)


def _adamw(w, m, v, grad_chunks, *, name):
    shape = w.shape
    w2 = _as_2d(w)
    rows, cols = w2.shape
    n_chunks = len(grad_chunks)
    n_parts = grad_chunks[0].shape[0]
    chunks = [c.reshape(n_parts, rows // n_chunks, cols) for c in grad_chunks]
    tr = _row_tile(rows // n_chunks, cols * (3 * 4 + n_chunks * n_parts * chunks[0].dtype.itemsize + 4 * 4))
    per_chunk = rows // n_chunks // tr
    c_m = 1.0 - ADAM_B1 ** ADAM_STEP
    c_v = 1.0 - ADAM_B2 ** ADAM_STEP

    def kern(w_ref, m_ref, v_ref, *refs):
        p_refs, (g_ref, d_ref, nm_ref, nv_ref) = refs[:n_chunks], refs[n_chunks:]
        g = None
        for k, p_ref in enumerate(p_refs):
            total = p_ref[0].astype(F32)
            for p in range(1, n_parts):
                total = total + p_ref[p].astype(F32)
            g = total if g is None else jnp.where(pl.program_id(0) // per_chunk == k, total, g)
        new_m = ADAM_B1 * m_ref[...] + (1.0 - ADAM_B1) * g
        new_v = ADAM_B2 * v_ref[...] + (1.0 - ADAM_B2) * (g * g)
        g_ref[...] = g
        nm_ref[...] = new_m
        nv_ref[...] = new_v
        d_ref[...] = -ADAM_LR * ((new_m / c_m) / (jnp.sqrt(new_v / c_v) + ADAM_EPS) + ADAM_WD * w_ref[...])

    spec = pl.BlockSpec((tr, cols), lambda i: (i, 0))
    out = jax.ShapeDtypeStruct((rows, cols), F32)
    def chunk_spec(k):
        return pl.BlockSpec((n_parts, tr, cols), lambda i: (0, jnp.clip(i - k * per_chunk, 0, per_chunk - 1), 0))

    res = pl.pallas_call(
        kern, name=name, grid=(rows // tr,),
        in_specs=[spec, spec, spec] + [chunk_spec(k) for k in range(n_chunks)],
        out_specs=[spec] * 4, out_shape=[out] * 4, compiler_params=_params(1))(w2, _as_2d(m), _as_2d(v), *chunks)
    return [r.reshape(shape) for r in res]


def _all_gather(shards, *, name):
    n = len(shards)

    def body(*refs):
        ins, outs = refs[:n], refs[n:2 * n]
        send_sems, recv_sems, local_sems = refs[2 * n:]
        x, y, c = _place()
        me, sibling = (x, y, c), (x, y, 1 - c)
        chips = [(1 - x, y), (x, 1 - y), (1 - x, 1 - y)]

        def slot(i, dev):
            return outs[i].at[4 * dev[0] + 2 * dev[1] + dev[2]]

        def copy(i, k, block, to, src=None):
            return pltpu.make_async_remote_copy(
                src_ref=slot(i, block) if src is None else src, dst_ref=slot(i, block),
                send_sem=send_sems.at[i, k], recv_sem=recv_sems.at[i, k], device_id=to, device_id_type=MESH)

        mine = [pltpu.make_async_copy(ins[i], slot(i, me), local_sems.at[i]) for i in range(n)]
        for cp in mine:
            cp.start()
        first = []
        for i in range(n):
            first.append(copy(i, 0, me, sibling, src=ins[i]))
            first += [copy(i, 1 + j, me, (*chip, c), src=ins[i]) for j, chip in enumerate(chips)]
        for cp in first:
            cp.start()
        passed = []
        for j, chip in enumerate(chips):
            for i in range(n):
                copy(i, 1 + j, (*chip, c), me).wait_recv()
                onward = copy(i, 4 + j, (*chip, c), sibling)
                onward.start()
                passed.append(onward)
        for i in range(n):
            copy(i, 0, sibling, me).wait_recv()
            for j, chip in enumerate(chips):
                copy(i, 4 + j, (*chip, 1 - c), me).wait_recv()
        for cp in first + passed:
            cp.wait_send()
        for cp in mine:
            cp.wait()

    return pl.pallas_call(
        body, name=name, in_specs=[ANY] * n, out_specs=[ANY] * n,
        out_shape=[jax.ShapeDtypeStruct((N_DEV,) + s.shape, s.dtype) for s in shards],
        scratch_shapes=[pltpu.SemaphoreType.DMA((n, 7)), pltpu.SemaphoreType.DMA((n, 7)), pltpu.SemaphoreType.DMA((n,))],
    )(*shards)


def _by_chip_and_core(grad):
    return grad.reshape(4, 2, -1, grad.shape[-1])


def _pack(vectors):
    flat = [v.reshape(-1).astype(F32) for v in vectors]
    sizes = [f.shape[0] for f in flat]
    total = sum(sizes)
    padded = -(-total // (SUBLANES * LANES)) * (SUBLANES * LANES)
    if padded > total:
        flat.append(jnp.zeros((padded - total,), F32))
    return jnp.concatenate(flat).reshape(-1, LANES), sizes


def _unpack(packed, sizes, shapes):
    flat = packed.reshape(-1)
    out, off = [], 0
    for size, shape in zip(sizes, shapes):
        out.append(flat[off:off + size].reshape(shape))
        off += size
    return out


def kernel(x, g_mix, w_in, sg_ln_g, sg_ln_b, sg_w, sg_b, lru_conv_w, lru_conv_b, lru_wa, lru_ba, lru_wx, lru_bx, lru_lam, p_sg, p_lru, w_out, g_ffn, w_up, ffn_conv_w, ffn_conv_b, w_down, g_final, loss_target, m_g_mix, m_w_in, m_sg_ln_g, m_sg_ln_b, m_sg_w, m_sg_b, m_lru_conv_w, m_lru_conv_b, m_lru_wa, m_lru_ba, m_lru_wx, m_lru_bx, m_lru_lam, m_p_sg, m_p_lru, m_w_out, m_g_ffn, m_w_up, m_ffn_conv_w, m_ffn_conv_b, m_w_down, m_g_final, v_g_mix, v_w_in, v_sg_ln_g, v_sg_ln_b, v_sg_w, v_sg_b, v_lru_conv_w, v_lru_conv_b, v_lru_wa, v_lru_ba, v_lru_wx, v_lru_bx, v_lru_lam, v_p_sg, v_p_lru, v_w_out, v_g_ffn, v_w_up, v_ffn_conv_w, v_ffn_conv_b, v_w_down, v_g_final):
    weights = dict(g_mix=g_mix, w_in=w_in, sg_ln_g=sg_ln_g, sg_ln_b=sg_ln_b, sg_w=sg_w, sg_b=sg_b, lru_conv_w=lru_conv_w,
                   lru_conv_b=lru_conv_b, lru_wa=lru_wa, lru_ba=lru_ba, lru_wx=lru_wx, lru_bx=lru_bx, lru_lam=lru_lam,
                   p_sg=p_sg, p_lru=p_lru, w_out=w_out, g_ffn=g_ffn, w_up=w_up, ffn_conv_w=ffn_conv_w,
                   ffn_conv_b=ffn_conv_b, w_down=w_down, g_final=g_final)
    m_in = dict(g_mix=m_g_mix, w_in=m_w_in, sg_ln_g=m_sg_ln_g, sg_ln_b=m_sg_ln_b, sg_w=m_sg_w, sg_b=m_sg_b,
                lru_conv_w=m_lru_conv_w, lru_conv_b=m_lru_conv_b, lru_wa=m_lru_wa, lru_ba=m_lru_ba, lru_wx=m_lru_wx,
                lru_bx=m_lru_bx, lru_lam=m_lru_lam, p_sg=m_p_sg, p_lru=m_p_lru, w_out=m_w_out, g_ffn=m_g_ffn,
                w_up=m_w_up, ffn_conv_w=m_ffn_conv_w, ffn_conv_b=m_ffn_conv_b, w_down=m_w_down, g_final=m_g_final)
    v_in = dict(g_mix=v_g_mix, w_in=v_w_in, sg_ln_g=v_sg_ln_g, sg_ln_b=v_sg_ln_b, sg_w=v_sg_w, sg_b=v_sg_b,
                lru_conv_w=v_lru_conv_w, lru_conv_b=v_lru_conv_b, lru_wa=v_lru_wa, lru_ba=v_lru_ba, lru_wx=v_lru_wx,
                lru_bx=v_lru_bx, lru_lam=v_lru_lam, p_sg=v_p_sg, p_lru=v_p_lru, w_out=v_w_out, g_ffn=v_g_ffn,
                w_up=v_w_up, ffn_conv_w=v_ffn_conv_w, ffn_conv_b=v_ffn_conv_b, w_down=v_w_down, g_final=v_g_final)
    order = list(weights)

    n_seq, seq, d = x.shape
    t = n_seq * seq
    sgw = sg_ln_g.shape[-1]
    lw = lru_lam.shape[-1]
    hd = lw // LRU_HEADS
    f2 = ffn_conv_b.shape[-1]
    gate_col = (2 * sgw + 2 * lw) // d
    xi, yi, ci = _place()
    dev = 4 * xi + 2 * yi + ci

    core = jnp.reshape(ci, (1,)).astype(jnp.int32)
    big = ["w_in", "p_sg", "p_lru", "w_out", "w_up", "w_down", "lru_wa", "lru_wx"]
    shards = {k: _cast_bf16(weights[k][0], name=f"cast_{k}") for k in big}
    shards["taps"], tap_sizes = _pack([lru_conv_w[0], ffn_conv_w[0]])
    by_need = ["w_in", "lru_wa", "lru_wx", "taps", "p_sg", "p_lru", "w_out", "w_up", "w_down"]
    in_flight = dict(zip(by_need, _start_first_legs([shards[k] for k in by_need], dev, name="start_weight_gather")))

    def rows_in_order(g8):
        return g8.reshape(1, -1, g8.shape[-1])

    x2d = x.reshape(t, d)
    h1 = _rmsnorm_fwd(x2d, g_mix, name="norm_mix")
    landed = _await_first_legs([in_flight[k] for k in by_need[:5]], h1, name="await_first_weights")
    ((w_in_g, wa_8, wx_8, taps_8, p_sg_g),) = _exchange_now([_gather_second_leg(landed)], name="gather_first_second_leg")
    wa_g, wx_g = (jnp.swapaxes(w8, 0, 1).reshape(LRU_HEADS, hd, hd) for w8 in (wa_8, wx_8))
    wa_t, wx_t = jnp.swapaxes(wa_g, 1, 2), jnp.swapaxes(wx_g, 1, 2)
    tap_parts = [_unpack(taps_8[k], tap_sizes, [lru_conv_w.shape[1:], ffn_conv_w.shape[1:]]) for k in range(N_DEV)]
    lru_cw = jnp.concatenate([p[0] for p in tap_parts], axis=1)
    ffn_cw = jnp.concatenate([p[1] for p in tap_parts], axis=1)
    sg_w0 = sg_w[0]
    sg_w_t = jnp.swapaxes(sg_w0, 1, 2)
    sg_b_t = sg_b[0].T
    proj, _ = _mm_nn(h1, w_in_g, out_dtype=F32, name="proj_in")
    y_a = _mixer_a_fwd(proj, sg_ln_g, sg_ln_b, sg_w0, sg_b_t, sgw, name="mixer_a_fwd")
    y_b = _mixer_b_fwd(proj, lru_cw, lru_conv_b, wa_g, lru_ba, wx_g, lru_bx, lru_lam, seq=seq, sgw=sgw, lw=lw,
                       name="mixer_b_fwd")
    landed = _await_first_legs([in_flight["p_lru"], in_flight["w_out"]], y_b, name="await_p_lru_w_out")
    m_a, ((p_lru_8, w_out_8),) = _mm_nn(y_a, p_sg_g, out_dtype=F32, name="proj_sg", exchanges=[_gather_second_leg(landed)])
    p_lru_g, w_out_g = rows_in_order(p_lru_8), rows_in_order(w_out_8)
    m_b, _ = _mm_nn(y_b, p_lru_g, out_dtype=F32, name="proj_lru")
    merged = _merge_fwd(m_a, m_b, proj, gate_col, name="merge_fwd")
    landed = _await_first_legs([in_flight["w_up"]], merged, name="await_w_up")
    x1, ((w_up_g,),) = _mm_nn(merged, w_out_g, out_dtype=F32, residual=x2d, name="proj_out",
                              exchanges=[_gather_second_leg(landed)])
    h2 = _rmsnorm_fwd(x1, g_ffn, name="norm_ffn")
    up_pre, _ = _mm_nn(h2, w_up_g, out_dtype=F32, name="ffn_up")
    act = _ffn_mid_fwd(up_pre, ffn_cw, ffn_conv_b, seq=seq, name="ffn_mid_fwd")
    landed = _await_first_legs([in_flight["w_down"]], act, name="await_w_down")
    ((w_down_8,),) = _exchange_now([_gather_second_leg(landed)], name="gather_w_down_second_leg")
    w_down_g = rows_in_order(w_down_8)
    x2, _ = _mm_nn(act, w_down_g, out_dtype=F32, residual=x1, name="ffn_down")
    d_x2, d_x2_b, d_g_final, loss_part = _loss_head(x2, g_final.reshape(1, d), loss_target.reshape(t, d), name="loss_head")
    loss = lax.psum(loss_part[0, 0], ("x", "y", "c"))

    def by_rows(g):
        return g.reshape(N_DEV, -1, g.shape[-1])

    def by_head_rows(g):
        return jnp.swapaxes(g.reshape(LRU_HEADS, N_DEV, hd // N_DEV, hd), 0, 1)

    def chip_sum(view, from_sibling, key, n_chunks=1):
        return [_add_sibling_part(view, core, from_sibling, name=f"chip_sum_{key}_{k}", chunk=k, n_chunks=n_chunks)
                for k in range(n_chunks)]

    d_w_down, _ = _mm_tn(act, d_x2_b, 1, name="grad_w_down")
    v_down = _by_chip_and_core(by_rows(d_w_down))
    d_act, ((s_down,),) = _mm_nt(d_x2_b, w_down_g, name="bwd_ffn_down", exchanges=[_swap_with_sibling([v_down])])
    c_down = chip_sum(v_down, s_down, "w_down")
    d_up_pre, d_ffn_cw, d_ffn_cb = _ffn_mid_bwd(up_pre, d_act, ffn_cw, ffn_conv_b, seq=seq, name="ffn_mid_bwd")
    d_w_up, (parts_down,) = _mm_tn(h2, d_up_pre, N_DEV, name="grad_w_up", exchanges=[_swap_with_chips(c_down)])
    v_up = _by_chip_and_core(d_w_up)
    d_h2, ((s_up,),) = _mm_nt(d_up_pre, w_up_g, name="bwd_ffn_up", exchanges=[_swap_with_sibling([v_up])])
    c_up_top, c_up_bottom = chip_sum(v_up, s_up, "w_up", n_chunks=2)
    d_x1, d_x1_b, d_g_ffn = _rmsnorm_bwd(x1, g_ffn, d_h2, d_x2, name="norm_ffn_bwd")
    d_w_out, _ = _mm_tn(merged, d_x1_b, 1, name="grad_w_out")
    v_out = _by_chip_and_core(by_rows(d_w_out))
    d_merged, ((s_out,),) = _mm_nt(d_x1_b, w_out_g, name="bwd_proj_out", exchanges=[_swap_with_sibling([v_out])])
    c_out = chip_sum(v_out, s_out, "w_out")
    d_m_a, d_m_b, d_gates = _merge_bwd(d_merged, m_a, m_b, proj, gate_col, name="merge_bwd")
    d_p_sg, _ = _mm_tn(y_a, d_m_a, N_DEV, name="grad_p_sg")
    d_p_lru, _ = _mm_tn(y_b, d_m_b, 1, name="grad_p_lru")
    v_sg, v_lru = _by_chip_and_core(d_p_sg), _by_chip_and_core(by_rows(d_p_lru))
    d_y_a, ((s_sg, s_lru),) = _mm_nt(d_m_a, p_sg_g, name="bwd_proj_sg", exchanges=[_swap_with_sibling([v_sg, v_lru])])
    d_y_b, (parts_out,) = _mm_nt(d_m_b, p_lru_g, name="bwd_proj_lru", exchanges=[_swap_with_chips(c_out)])
    c_sg, c_lru = chip_sum(v_sg, s_sg, "p_sg"), chip_sum(v_lru, s_lru, "p_lru")
    d_zuv, d_sg_w, d_sg_b_t, d_ln_g, d_ln_b = _mixer_a_bwd(proj, d_y_a, sg_ln_g, sg_ln_b, sg_w0, sg_w_t, sg_b_t, sgw,
                                                           name="mixer_a_bwd")
    (d_xr, d_yr, d_lru_cw, d_lru_cb, d_wa, d_ba, d_wx, d_bx, d_lam), (parts_up_top,) = _mixer_b_bwd(
        proj, d_y_b, lru_cw, lru_conv_b, wa_g, wa_t, lru_ba, wx_g, wx_t, lru_bx, lru_lam, seq=seq, sgw=sgw, lw=lw,
        name="mixer_b_bwd", exchanges=[_swap_with_chips([c_up_top])])
    d_proj = jnp.concatenate([d_zuv, d_xr, d_yr, d_gates], axis=1)
    d_w_in, ((parts_up_bottom, parts_sg, parts_lru),) = _mm_tn(
        h1, d_proj, N_DEV, name="grad_w_in", exchanges=[_swap_with_chips([c_up_bottom] + c_sg + c_lru)])
    v_win = _by_chip_and_core(d_w_in)
    v_wa = _by_chip_and_core(_cast_bf16(by_head_rows(d_wa), name="cast_grad_wa"))
    v_wx = _by_chip_and_core(_cast_bf16(by_head_rows(d_wx), name="cast_grad_wx"))
    ((s_in, s_wa, s_wx),) = _exchange_now([_swap_with_sibling([v_win, v_wa, v_wx])], name="swap_sibling_w_in")
    c_in, c_wa, c_wx = chip_sum(v_win, s_in, "w_in"), chip_sum(v_wa, s_wa, "lru_wa"), chip_sum(v_wx, s_wx, "lru_wx")
    d_h1, ((parts_in, parts_wa, parts_wx),) = _mm_nt(d_proj, w_in_g, name="bwd_proj_in",
                                                     exchanges=[_swap_with_chips(c_in + c_wa + c_wx)])
    grad_x, _, d_g_mix = _rmsnorm_bwd(x2d, g_mix, d_h1, d_x1, name="norm_mix_bwd")
    grad_parts = dict(w_in=[parts_in], p_sg=[parts_sg], p_lru=[parts_lru], w_out=parts_out, w_up=parts_up_top + [parts_up_bottom],
                      w_down=parts_down, lru_wa=[parts_wa], lru_wx=[parts_wx])

    small = ["g_mix", "sg_ln_g", "sg_ln_b", "sg_w", "sg_b", "lru_conv_b", "lru_ba", "lru_bx", "lru_lam", "g_ffn",
             "ffn_conv_b", "g_final", "lru_conv_w", "ffn_conv_w"]
    small_parts = dict(g_mix=d_g_mix, sg_ln_g=d_ln_g, sg_ln_b=d_ln_b, sg_w=d_sg_w, sg_b=d_sg_b_t.T, lru_conv_b=d_lru_cb,
                       lru_ba=d_ba, lru_bx=d_bx, lru_lam=d_lam, g_ffn=d_g_ffn, ffn_conv_b=d_ffn_cb, g_final=d_g_final,
                       lru_conv_w=d_lru_cw, ffn_conv_w=d_ffn_cw)
    packed, sizes = _pack([small_parts[k] for k in small])
    (all_small,) = _all_gather([packed], name="gather_small_grads")
    small_sum = _sum_parts(all_small, name="sum_small_grads")
    small_grads = dict(zip(small, _unpack(small_sum, sizes, [small_parts[k].shape for k in small])))
    for k in ("lru_conv_w", "ffn_conv_w"):
        n_loc = weights[k].shape[-1]
        small_grads[k] = lax.dynamic_slice_in_dim(small_grads[k], dev * n_loc, n_loc, axis=1)

    grads, deltas, new_m, new_v = {}, {}, {}, {}
    for k in order:
        w = weights[k]
        chunks = grad_parts[k] if k in grad_parts else [small_grads[k].reshape((1,) + w.shape)]
        grads[k], deltas[k], new_m[k], new_v[k] = _adamw(w, m_in[k], v_in[k], chunks, name=f"adamw_{k}")

    return (loss, grad_x.reshape(x.shape), *[grads[k] for k in order], *[deltas[k] for k in order],
            *[new_m[k] for k in order], *[new_v[k] for k in order])
```

```python
import functools
import math
from typing import Callable, NamedTuple

import jax
import jax.numpy as jnp
from jax import lax
from jax.experimental import pallas as pl
from jax.experimental.pallas import tpu as pltpu

F32 = jnp.float32
BF16 = jnp.bfloat16
MESH = pl.DeviceIdType.MESH
ANY = pl.BlockSpec(memory_space=pl.ANY)

N_DEV = 8
EPS = 1e-6
CHUNK = 128
SG_GROUPS = 8
LRU_HEADS = 16
LRU_C = 8.0
ADAM_LR = 0.001
ADAM_B1 = 0.9
ADAM_B2 = 0.999
ADAM_EPS = 1e-08
ADAM_WD = 0.01
ADAM_STEP = 10

V7X_VMEM_LIMIT = 56 * 1024 * 1024
LANES = 128
SUBLANES = 8
MXU = 256

_GELU_C0 = math.sqrt(2.0 / math.pi)
_GELU_C1 = 0.044715


def _params(n_axes):
    return pltpu.CompilerParams(dimension_semantics=("arbitrary",) * n_axes, vmem_limit_bytes=V7X_VMEM_LIMIT)


def _tile(dim, pref, align):
    t = (min(pref, dim) // align) * align
    while t >= align:
        if dim % t == 0:
            return t
        t -= align
    return dim


def _gelu(x):
    return x * (0.5 * (1.0 + jnp.tanh(_GELU_C0 * (x + _GELU_C1 * (x * x * x)))))


def _gelu_and_grad(x):
    t = jnp.tanh(_GELU_C0 * (x + _GELU_C1 * (x * x * x)))
    cdf = 0.5 * (1.0 + t)
    dcdf = 0.5 * (1.0 - t * t) * (_GELU_C0 * (1.0 + 3.0 * _GELU_C1 * (x * x)))
    return x * cdf, cdf + x * dcdf


def _sigmoid(x):
    return 1.0 / (1.0 + jnp.exp(-x))


def _shift_down(x, d):
    if d == 0:
        return x
    row = lax.broadcasted_iota(jnp.int32, x.shape, 0)
    return jnp.where(row >= d, pltpu.roll(x, d, 0), 0.0)


def _shift_up(x, d):
    if d == 0:
        return x
    s = x.shape[0]
    row = lax.broadcasted_iota(jnp.int32, x.shape, 0)
    return jnp.where(row < s - d, pltpu.roll(x, s - d, 0), 0.0)


def _causal_conv(x, w, b):
    k_taps = w.shape[0]
    out = _shift_down(x, k_taps - 1) * w[0:1, :]
    for k in range(1, k_taps):
        out = out + _shift_down(x, k_taps - 1 - k) * w[k:k + 1, :]
    return out + b


def _causal_conv_bwd_x(d_out, w):
    k_taps = w.shape[0]
    d_x = _shift_up(d_out, k_taps - 1) * w[0:1, :]
    for k in range(1, k_taps):
        d_x = d_x + _shift_up(d_out, k_taps - 1 - k) * w[k:k + 1, :]
    return d_x


def _causal_conv_bwd_w(d_out, x, k_taps):
    rows = [jnp.sum(d_out * _shift_down(x, k_taps - 1 - k), axis=0, keepdims=True) for k in range(k_taps)]
    return jnp.concatenate(rows, axis=0)


def _place():
    return lax.axis_index("x"), lax.axis_index("y"), lax.axis_index("c")


def _other_chips(x, y):
    return [(1 - x, y), (x, 1 - y), (1 - x, 1 - y)]


class _Exchange(NamedTuple):
    ins: tuple
    outs: tuple
    in_place: bool
    n_remote: int
    n_local: int
    copies: Callable


def _remote(src, dst, send_sem, recv_sem, to):
    return pltpu.make_async_remote_copy(src_ref=src, dst_ref=dst, send_sem=send_sem, recv_sem=recv_sem, device_id=to,
                                        device_id_type=MESH)


def _gather_first_leg(shards, place_own=True):
    n = len(shards)

    def copies(ins, outs, send_sems, recv_sems, local_sems):
        x, y, c = _place()
        peers = [(x, y, 1 - c)] + [(*chip, c) for chip in _other_chips(x, y)]
        slot = lambda i, dev: outs[i].at[4 * dev[0] + 2 * dev[1] + dev[2]]
        sends = [_remote(ins[i], slot(i, (x, y, c)), send_sems[i].at[k], recv_sems[i].at[k], to)
                 for i in range(n) for k, to in enumerate(peers)]
        receives = [_remote(ins[i], slot(i, frm), send_sems[i].at[k], recv_sems[i].at[k], frm)
                    for i in range(n) for k, frm in enumerate(peers)]
        local = [pltpu.make_async_copy(ins[i], slot(i, (x, y, c)), local_sems[i].at[0]) for i in range(n)] if place_own else []
        return sends, receives, local

    outs = tuple(jax.ShapeDtypeStruct((N_DEV,) + s.shape, s.dtype) for s in shards)
    return _Exchange(tuple(shards), outs, False, 4, int(place_own), copies)


def _gather_second_leg(gathered):
    n = len(gathered)

    def copies(ins, outs, send_sems, recv_sems, local_sems):
        x, y, c = _place()
        slot = lambda i, chip, core: outs[i].at[4 * chip[0] + 2 * chip[1] + core]
        sends = [_remote(slot(i, chip, c), slot(i, chip, c), send_sems[i].at[j], recv_sems[i].at[j], (x, y, 1 - c))
                 for i in range(n) for j, chip in enumerate(_other_chips(x, y))]
        receives = [_remote(slot(i, chip, 1 - c), slot(i, chip, 1 - c), send_sems[i].at[j], recv_sems[i].at[j], (x, y, 1 - c))
                    for i in range(n) for j, chip in enumerate(_other_chips(x, y))]
        return sends, receives, []

    outs = tuple(jax.ShapeDtypeStruct(g.shape, g.dtype) for g in gathered)
    return _Exchange(tuple(gathered), outs, True, 3, 0, copies)


def _swap_with_sibling(parts):
    n = len(parts)

    def copies(ins, outs, send_sems, recv_sems, local_sems):
        x, y, c = _place()
        both = [_remote(ins[i].at[ch, 1 - c], outs[i].at[ch], send_sems[i].at[ch], recv_sems[i].at[ch], (x, y, 1 - c))
                for i in range(n) for ch in range(4)]
        return both, both, []

    outs = tuple(jax.ShapeDtypeStruct((4,) + p.shape[2:], p.dtype) for p in parts)
    return _Exchange(tuple(parts), outs, False, 4, 0, copies)


def _swap_with_chips(parts, place_own=True):
    n = len(parts)

    def copies(ins, outs, send_sems, recv_sems, local_sems):
        x, y, c = _place()
        both = [_remote(ins[i].at[2 * chip[0] + chip[1]], outs[i].at[1 + j], send_sems[i].at[j], recv_sems[i].at[j], (*chip, c))
                for i in range(n) for j, chip in enumerate(_other_chips(x, y))]
        local = [pltpu.make_async_copy(ins[i].at[2 * x + y], outs[i].at[0], local_sems[i].at[0]) for i in range(n)] if place_own else []
        return both, both, local

    outs = tuple(jax.ShapeDtypeStruct(p.shape, p.dtype) for p in parts)
    return _Exchange(tuple(parts), outs, False, 3, int(place_own), copies)


def _exchange_plumbing(exchanges):
    operands = [a for ex in exchanges for a in ex.ins]
    results = [s for ex in exchanges for s in ex.outs]
    scratch, in_place, at = [], {}, 0
    for ex in exchanges:
        n = len(ex.ins)
        scratch += [pltpu.SemaphoreType.DMA((n, ex.n_remote))] * 2
        if ex.n_local:
            scratch.append(pltpu.SemaphoreType.DMA((n, ex.n_local)))
        if ex.in_place:
            in_place.update({at + i: at + i for i in range(n)})
        at += n

    def copies(in_refs, out_refs, sem_refs):
        sends, receives, local = [], [], []
        at, sem_at = 0, 0
        for ex in exchanges:
            n, n_sem = len(ex.ins), 3 if ex.n_local else 2
            per_operand = [[sem.at[i] for i in range(n)] for sem in sem_refs[sem_at:sem_at + n_sem]] + [[]] * (3 - n_sem)
            s, r, l = ex.copies(in_refs[at:at + n], out_refs[at:at + n], *per_operand)
            sends, receives, local = sends + s, receives + r, local + l
            at, sem_at = at + n, sem_at + n_sem
        return sends, receives, local

    return operands, results, scratch, in_place, copies


def _start_all(copies):
    sends, _, local = copies
    for cp in local + sends:
        cp.start()


def _wait_all(copies):
    sends, receives, local = copies
    for cp in receives:
        cp.wait_recv()
    for cp in sends:
        cp.wait_send()
    for cp in local:
        cp.wait()


def _exchange_now(exchanges, *, name):
    operands, results, scratch, in_place, copies = _exchange_plumbing(exchanges)
    n = len(operands)

    def body(*refs):
        made = copies(refs[:n], refs[n:2 * n], refs[2 * n:])
        _start_all(made)
        _wait_all(made)

    out = pl.pallas_call(body, name=name, in_specs=[ANY] * n, out_specs=[ANY] * n, out_shape=results,
                         scratch_shapes=scratch, input_output_aliases=in_place)(*operands)
    return _split(out, exchanges)


def _split(flat, exchanges):
    out, at = [], 0
    for ex in exchanges:
        out.append(list(flat[at:at + len(ex.ins)]))
        at += len(ex.ins)
    return out


HBM = pl.BlockSpec(memory_space=pltpu.HBM)
SEMAPHORES = pl.BlockSpec(memory_space=pltpu.SEMAPHORE)
SPLIT_COPY = pltpu.CompilerParams(has_side_effects=pltpu.SideEffectType.DATAFLOW_SIDE_EFFECTING)


def _start_exchange(make, operands, landings, *, name):
    n = len(operands)
    ex = make(operands)

    def body(*refs):
        sends, _, _ = ex.copies(refs[:n], refs[n:2 * n], refs[2 * n:3 * n], refs[3 * n:4 * n], [])
        for cp in sends:
            cp.start()

    buffers = [pltpu.with_memory_space_constraint(a, pltpu.HBM) for a in list(operands) + list(landings)]
    out = pl.pallas_call(
        body, name=name, in_specs=[HBM] * (2 * n), out_specs=[SEMAPHORES] * (2 * n) + [HBM] * (2 * n),
        out_shape=[pltpu.SemaphoreType.DMA((ex.n_remote,))] * (2 * n) + [pltpu.HBM(a.shape, a.dtype) for a in buffers],
        input_output_aliases={i: 2 * n + i for i in range(2 * n)}, compiler_params=SPLIT_COPY)(*buffers)
    return [(out[i], out[n + i], out[2 * n + i], out[3 * n + i]) for i in range(n)]


def _await_exchange(make, in_flight, after, *, name):
    n = len(in_flight)
    send_sems, recv_sems, operands, landings = zip(*in_flight)
    ex = make(operands)

    def body(*refs):
        sends, receives, _ = ex.copies(refs[:n], refs[n:2 * n], refs[2 * n:3 * n], refs[3 * n:4 * n], [])
        for cp in receives:
            cp.wait_recv()
        for cp in sends:
            cp.wait_send()

    out = pl.pallas_call(
        body, name=name, in_specs=[HBM] * (2 * n) + [SEMAPHORES] * (2 * n) + [ANY], out_specs=[HBM] * (2 * n),
        out_shape=[pltpu.HBM(a.shape, a.dtype) for a in operands + landings],
        input_output_aliases={i: i for i in range(2 * n)}, compiler_params=SPLIT_COPY,
    )(*operands, *landings, *send_sems, *recv_sems, after)
    return list(out[n:])


def _own_block_first(blocks, index):
    own = lax.dynamic_index_in_dim(blocks, index, 0, keepdims=True)
    return lax.dynamic_update_index_in_dim(lax.empty(blocks.shape, blocks.dtype), own, 0, 0)


def _own_block_in_place(shard, index):
    return lax.dynamic_update_index_in_dim(lax.empty((N_DEV,) + shard.shape, shard.dtype), shard, index, 0)


def _pallas(kern, *, name, grid, in_specs, out_specs, out_shape, operands, scratch_shapes=(), exchanges=()):
    ex_operands, ex_results, ex_scratch, in_place, copies = _exchange_plumbing(exchanges)
    n_in, n_out, n_scratch, n_ex = len(in_specs), len(out_specs), len(scratch_shapes), len(ex_operands)

    def body(*refs):
        ins, refs = refs[:n_in], refs[n_in:]
        ex_ins, refs = refs[:n_ex], refs[n_ex:]
        outs, refs = refs[:n_out], refs[n_out:]
        ex_outs, refs = refs[:n_ex], refs[n_ex:]
        scratch, sems = refs[:n_scratch], refs[n_scratch:]
        if exchanges:
            first = functools.reduce(jnp.logical_and, [pl.program_id(a) == 0 for a in range(len(grid))])
            last = functools.reduce(jnp.logical_and, [pl.program_id(a) == g - 1 for a, g in enumerate(grid)])

            @pl.when(first)
            def _():
                _start_all(copies(ex_ins, ex_outs, sems))

        kern(*ins, *outs, *scratch)
        if exchanges:
            @pl.when(last)
            def _():
                _wait_all(copies(ex_ins, ex_outs, sems))

    res = pl.pallas_call(
        body, name=name, grid=grid, in_specs=list(in_specs) + [ANY] * n_ex, out_specs=list(out_specs) + [ANY] * n_ex,
        out_shape=list(out_shape) + ex_results, scratch_shapes=list(scratch_shapes) + ex_scratch,
        input_output_aliases={n_in + i: n_out + o for i, o in in_place.items()},
        compiler_params=_params(len(grid)))(*operands, *ex_operands)
    return list(res[:n_out]), _split(res[n_out:], exchanges)


def _accumulate(step, n_steps, acc, value, finish):
    if n_steps == 1:
        finish(value)
        return

    @pl.when(step == 0)
    def _():
        acc[0][...] = value

    @pl.when(step > 0)
    def _():
        acc[0][...] += value

    @pl.when(step == n_steps - 1)
    def _():
        finish(acc[0][...])


def _mm_nn(a, w, *, out_dtype, name, residual=None, exchanges=()):
    m, k = a.shape
    nb, _, n_blk = w.shape
    tm, tn, tk = _tile(m, 512, MXU), _tile(n_blk, 1536, MXU), _tile(k, 4096, MXU)
    per = n_blk // tn
    nk = k // tk

    def kern(*refs):
        a_ref, w_ref = refs[:2]
        r_ref = None if residual is None else refs[2]
        o_ref, acc = refs[2 + (residual is not None)], refs[3 + (residual is not None):]

        def finish(total):
            o_ref[...] = (total if r_ref is None else total + r_ref[...]).astype(o_ref.dtype)

        _accumulate(pl.program_id(2), nk, acc, jnp.dot(a_ref[...], w_ref[...], preferred_element_type=F32), finish)

    tile = pl.BlockSpec((tm, tn), lambda j, i, kk: (i, j))
    in_specs = [pl.BlockSpec((tm, tk), lambda j, i, kk: (i, kk)),
                pl.BlockSpec((None, tk, tn), lambda j, i, kk: (j // per, kk, j % per))]
    operands = [a, w]
    if residual is not None:
        in_specs.append(tile)
        operands.append(residual)
    (out,), carried = _pallas(
        kern, name=name, grid=(nb * per, m // tm, nk), in_specs=in_specs, out_specs=[tile],
        out_shape=[jax.ShapeDtypeStruct((m, nb * n_blk), out_dtype)], operands=operands,
        scratch_shapes=[pltpu.VMEM((tm, tn), F32)] * (nk > 1), exchanges=exchanges)
    return out, carried


def _mm_nt(g, w, *, name, exchanges=()):
    m, n = g.shape
    nb, k, n_blk = w.shape
    tm, tko, tn = _tile(m, 1024, MXU), _tile(k, 1024, MXU), _tile(n_blk, 3072, MXU)
    per = n_blk // tn
    nn = n // tn

    def kern(g_ref, w_ref, o_ref, *acc):
        def finish(total):
            o_ref[...] = total

        part = lax.dot_general(g_ref[...], w_ref[...], (((1,), (1,)), ((), ())), preferred_element_type=F32)
        _accumulate(pl.program_id(2), nn, acc, part, finish)

    (out,), carried = _pallas(
        kern, name=name, grid=(k // tko, m // tm, nn),
        in_specs=[pl.BlockSpec((tm, tn), lambda j, i, jn: (i, jn)),
                  pl.BlockSpec((None, tko, tn), lambda j, i, jn: (jn // per, j, jn % per))],
        out_specs=[pl.BlockSpec((tm, tko), lambda j, i, jn: (i, j))],
        out_shape=[jax.ShapeDtypeStruct((m, k), F32)], operands=[g, w],
        scratch_shapes=[pltpu.VMEM((tm, tko), F32)] * (nn > 1), exchanges=exchanges)
    return out, carried


def _mm_tn(a, g, nb, *, name, exchanges=()):
    m, k = a.shape
    n = g.shape[1]
    n_blk = n // nb
    tko, tn, tm = _tile(k, 512, MXU), _tile(n_blk, 1536, MXU), _tile(m, 4096, MXU)
    per = n_blk // tn
    nm = m // tm

    def kern(a_ref, g_ref, o_ref, *acc):
        def finish(total):
            o_ref[...] = total.astype(o_ref.dtype)

        part = lax.dot_general(a_ref[...], g_ref[...], (((0,), (0,)), ((), ())), preferred_element_type=F32)
        _accumulate(pl.program_id(2), nm, acc, part, finish)

    (out,), carried = _pallas(
        kern, name=name, grid=(nb * per, k // tko, nm),
        in_specs=[pl.BlockSpec((tm, tko), lambda j, i, im: (im, i)),
                  pl.BlockSpec((tm, tn), lambda j, i, im: (im, j))],
        out_specs=[pl.BlockSpec((None, tko, tn), lambda j, i, im: (j // per, i, j % per))],
        out_shape=[jax.ShapeDtypeStruct((nb, k, n_blk), BF16)], operands=[a, g],
        scratch_shapes=[pltpu.VMEM((tko, tn), F32)] * (nm > 1), exchanges=exchanges)
    return out, carried


ROW_TILE = 128


def _rmsnorm_fwd(x, g, *, name):
    t, d = x.shape
    tr = _tile(t, ROW_TILE, SUBLANES)

    def kern(x_ref, g_ref, h_ref):
        xv = x_ref[...]
        r = lax.rsqrt(jnp.mean(xv * xv, axis=-1, keepdims=True) + EPS)
        h_ref[...] = (xv * r * g_ref[...]).astype(BF16)

    return pl.pallas_call(
        kern, name=name, grid=(t // tr,),
        in_specs=[pl.BlockSpec((tr, d), lambda i: (i, 0)), pl.BlockSpec((1, d), lambda i: (0, 0))],
        out_specs=pl.BlockSpec((tr, d), lambda i: (i, 0)),
        out_shape=jax.ShapeDtypeStruct((t, d), BF16), compiler_params=_params(1))(x, g)


def _rmsnorm_bwd(x, g, d_h, d_res, *, name):
    t, d = x.shape
    tr = _tile(t, ROW_TILE, SUBLANES)

    def kern(x_ref, g_ref, dh_ref, dres_ref, dx_ref, dxb_ref, dg_ref):
        xv = x_ref[...]
        r = lax.rsqrt(jnp.mean(xv * xv, axis=-1, keepdims=True) + EPS)
        dh = dh_ref[...]
        gy = dh * g_ref[...]
        dx = dres_ref[...] + r * gy - xv * (r * r * r) * jnp.mean(gy * xv, axis=-1, keepdims=True)
        dx_ref[...] = dx
        dxb_ref[...] = dx.astype(BF16)

        @pl.when(pl.program_id(0) == 0)
        def _():
            dg_ref[...] = jnp.zeros_like(dg_ref)

        dg_ref[...] += jnp.sum(dh * (xv * r), axis=0, keepdims=True)

    row = pl.BlockSpec((tr, d), lambda i: (i, 0))
    vec = pl.BlockSpec((1, d), lambda i: (0, 0))
    return pl.pallas_call(
        kern, name=name, grid=(t // tr,), in_specs=[row, vec, row, row], out_specs=[row, row, vec],
        out_shape=[jax.ShapeDtypeStruct((t, d), F32), jax.ShapeDtypeStruct((t, d), BF16),
                   jax.ShapeDtypeStruct((1, d), F32)], compiler_params=_params(1))(x, g, d_h, d_res)


def _loss_head(x, g, target, *, name):
    t, d = x.shape
    tr = _tile(t, ROW_TILE, SUBLANES)

    def kern(x_ref, g_ref, t_ref, dx_ref, dxb_ref, dg_ref, loss_ref):
        xv = x_ref[...]
        gv = g_ref[...]
        r = lax.rsqrt(jnp.mean(xv * xv, axis=-1, keepdims=True) + EPS)
        diff = xv * r * gv - t_ref[...]
        dy = diff * (1.0 / d)
        gy = dy * gv
        dx = r * gy - xv * (r * r * r) * jnp.mean(gy * xv, axis=-1, keepdims=True)
        dx_ref[...] = dx
        dxb_ref[...] = dx.astype(BF16)

        @pl.when(pl.program_id(0) == 0)
        def _():
            dg_ref[...] = jnp.zeros_like(dg_ref)
            loss_ref[...] = jnp.zeros_like(loss_ref)

        dg_ref[...] += jnp.sum(dy * (xv * r), axis=0, keepdims=True)
        part = 0.5 * jnp.sum(jnp.mean(diff * diff, axis=-1, keepdims=True), axis=0, keepdims=True)
        loss_ref[...] += jnp.broadcast_to(part, loss_ref.shape)

    row = pl.BlockSpec((tr, d), lambda i: (i, 0))
    vec = pl.BlockSpec((1, d), lambda i: (0, 0))
    return pl.pallas_call(
        kern, name=name, grid=(t // tr,), in_specs=[row, vec, row],
        out_specs=[row, row, vec, pl.BlockSpec((1, LANES), lambda i: (0, 0))],
        out_shape=[jax.ShapeDtypeStruct((t, d), F32), jax.ShapeDtypeStruct((t, d), BF16),
                   jax.ShapeDtypeStruct((1, d), F32), jax.ShapeDtypeStruct((1, LANES), F32)],
        compiler_params=_params(1))(x, g, target)


def _merge_fwd(m_a, m_b, proj, gate_col, *, name):
    t, d = m_a.shape
    tr = _tile(t, ROW_TILE, SUBLANES)

    def kern(ma_ref, mb_ref, ga_ref, gb_ref, o_ref):
        o_ref[...] = (_sigmoid(ga_ref[...]) * ma_ref[...] + _sigmoid(gb_ref[...]) * mb_ref[...]).astype(BF16)

    row = pl.BlockSpec((tr, d), lambda i: (i, 0))
    return pl.pallas_call(
        kern, name=name, grid=(t // tr,),
        in_specs=[row, row, pl.BlockSpec((tr, d), lambda i: (i, gate_col)),
                  pl.BlockSpec((tr, d), lambda i: (i, gate_col + 1))],
        out_specs=row, out_shape=jax.ShapeDtypeStruct((t, d), BF16), compiler_params=_params(1))(m_a, m_b, proj, proj)


def _merge_bwd(d_merged, m_a, m_b, proj, gate_col, *, name):
    t, d = m_a.shape
    tr = _tile(t, ROW_TILE, SUBLANES)

    def kern(dm_ref, ma_ref, mb_ref, ga_ref, gb_ref, dma_ref, dmb_ref, dg_ref):
        dm = dm_ref[...]
        sa = _sigmoid(ga_ref[...])
        sb = _sigmoid(gb_ref[...])
        dma_ref[...] = (dm * sa).astype(BF16)
        dmb_ref[...] = (dm * sb).astype(BF16)
        dg_ref[:, 0:d] = (dm * ma_ref[...] * (sa * (1.0 - sa))).astype(BF16)
        dg_ref[:, d:2 * d] = (dm * mb_ref[...] * (sb * (1.0 - sb))).astype(BF16)

    row = pl.BlockSpec((tr, d), lambda i: (i, 0))
    return pl.pallas_call(
        kern, name=name, grid=(t // tr,),
        in_specs=[row, row, row, pl.BlockSpec((tr, d), lambda i: (i, gate_col)),
                  pl.BlockSpec((tr, d), lambda i: (i, gate_col + 1))],
        out_specs=[row, row, pl.BlockSpec((tr, 2 * d), lambda i: (i, 0))],
        out_shape=[jax.ShapeDtypeStruct((t, d), BF16), jax.ShapeDtypeStruct((t, d), BF16),
                   jax.ShapeDtypeStruct((t, 2 * d), BF16)], compiler_params=_params(1))(d_merged, m_a, m_b, proj, proj)


def _tril_bf16(w, transposed):
    row = lax.broadcasted_iota(jnp.int32, w.shape, 0)
    col = lax.broadcasted_iota(jnp.int32, w.shape, 1)
    keep = (row <= col) if transposed else (row >= col)
    return jnp.where(keep, w, 0.0).astype(BF16)


def _layernorm_stats(v):
    mu = jnp.mean(v, axis=-1, keepdims=True)
    vc = v - mu
    rstd = lax.rsqrt(jnp.mean(vc * vc, axis=-1, keepdims=True) + EPS)
    return vc * rstd, rstd


def _mixer_a_fwd(proj, ln_g, ln_b, sg_w, sg_b_t, sgw, *, name):
    t = proj.shape[0]
    gd = sgw // SG_GROUPS

    def kern(zu_ref, zv_ref, g_ref, b_ref, w_ref, bt_ref, o_ref):
        xhat, _ = _layernorm_stats(_gelu(zv_ref[...]))
        vn = (xhat * g_ref[...] + b_ref[...]).astype(BF16)
        for g in range(SG_GROUPS):
            cols = slice(g * gd, (g + 1) * gd)
            mixed = jnp.dot(_tril_bf16(w_ref[g], False), vn[:, cols], preferred_element_type=F32) + bt_ref[:, g:g + 1]
            o_ref[:, cols] = (_gelu(zu_ref[:, cols]) * mixed).astype(BF16)

    vec = pl.BlockSpec((1, sgw), lambda i: (0, 0))
    return pl.pallas_call(
        kern, name=name, grid=(t // CHUNK,),
        in_specs=[pl.BlockSpec((CHUNK, sgw), lambda i: (i, 0)), pl.BlockSpec((CHUNK, sgw), lambda i: (i, 1)), vec, vec,
                  pl.BlockSpec((SG_GROUPS, CHUNK, CHUNK), lambda i: (0, 0, 0)),
                  pl.BlockSpec((CHUNK, SG_GROUPS), lambda i: (0, 0))],
        out_specs=pl.BlockSpec((CHUNK, sgw), lambda i: (i, 0)),
        out_shape=jax.ShapeDtypeStruct((t, sgw), BF16), compiler_params=_params(1))(proj, proj, ln_g, ln_b, sg_w, sg_b_t)


def _mixer_a_bwd(proj, d_ya, ln_g, ln_b, sg_w, sg_w_t, sg_b_t, sgw, *, name):
    t = proj.shape[0]
    gd = sgw // SG_GROUPS

    def kern(zu_ref, zv_ref, dy_ref, g_ref, b_ref, w_ref, wt_ref, bt_ref, dz_ref, dw_ref, dbt_ref, dg_ref, db_ref, dvn):
        @pl.when(pl.program_id(0) == 0)
        def _():
            dw_ref[...] = jnp.zeros_like(dw_ref)
            dbt_ref[...] = jnp.zeros_like(dbt_ref)
            dg_ref[...] = jnp.zeros_like(dg_ref)
            db_ref[...] = jnp.zeros_like(db_ref)

        gv, dgv = _gelu_and_grad(zv_ref[...])
        xhat, rstd = _layernorm_stats(gv)
        ln_gain = g_ref[...]
        vn = (xhat * ln_gain + b_ref[...]).astype(BF16)
        for g in range(SG_GROUPS):
            cols = slice(g * gd, (g + 1) * gd)
            gu, dgu = _gelu_and_grad(zu_ref[:, cols])
            mixed = jnp.dot(_tril_bf16(w_ref[g], False), vn[:, cols], preferred_element_type=F32) + bt_ref[:, g:g + 1]
            dy = dy_ref[:, cols]
            dz_ref[:, cols] = (dy * mixed * dgu).astype(BF16)
            d_mixed = dy * gu
            d_mixed_b = d_mixed.astype(BF16)
            dvn[:, cols] = jnp.dot(_tril_bf16(wt_ref[g], True), d_mixed_b, preferred_element_type=F32)
            d_w = lax.dot_general(d_mixed_b, vn[:, cols], (((1,), (1,)), ((), ())), preferred_element_type=F32)
            row = lax.broadcasted_iota(jnp.int32, d_w.shape, 0)
            col = lax.broadcasted_iota(jnp.int32, d_w.shape, 1)
            dw_ref[g] += jnp.where(row >= col, d_w, 0.0)
            dbt_ref[:, g:g + 1] += jnp.sum(d_mixed, axis=-1, keepdims=True)
        d_vn = dvn[...]
        dg_ref[...] += jnp.sum(d_vn * xhat, axis=0, keepdims=True)
        db_ref[...] += jnp.sum(d_vn, axis=0, keepdims=True)
        d_xhat = d_vn * ln_gain
        d_gv = rstd * (d_xhat - jnp.mean(d_xhat, axis=-1, keepdims=True)
                       - xhat * jnp.mean(d_xhat * xhat, axis=-1, keepdims=True))
        dz_ref[:, sgw:2 * sgw] = (d_gv * dgv).astype(BF16)

    vec = pl.BlockSpec((1, sgw), lambda i: (0, 0))
    wspec = pl.BlockSpec((SG_GROUPS, CHUNK, CHUNK), lambda i: (0, 0, 0))
    btspec = pl.BlockSpec((CHUNK, SG_GROUPS), lambda i: (0, 0))
    return pl.pallas_call(
        kern, name=name, grid=(t // CHUNK,),
        in_specs=[pl.BlockSpec((CHUNK, sgw), lambda i: (i, 0)), pl.BlockSpec((CHUNK, sgw), lambda i: (i, 1)),
                  pl.BlockSpec((CHUNK, sgw), lambda i: (i, 0)), vec, vec, wspec, wspec, btspec],
        out_specs=[pl.BlockSpec((CHUNK, 2 * sgw), lambda i: (i, 0)), wspec, btspec, vec, vec],
        out_shape=[jax.ShapeDtypeStruct((t, 2 * sgw), BF16), jax.ShapeDtypeStruct((SG_GROUPS, CHUNK, CHUNK), F32),
                   jax.ShapeDtypeStruct((CHUNK, SG_GROUPS), F32), jax.ShapeDtypeStruct((1, sgw), F32),
                   jax.ShapeDtypeStruct((1, sgw), F32)],
        scratch_shapes=[pltpu.VMEM((CHUNK, sgw), F32)],
        compiler_params=_params(1))(proj, proj, d_ya, ln_g, ln_b, sg_w, sg_w_t, sg_b_t)


def _scan_rows(a_ref, h_ref, reverse):
    s, c = a_ref.shape
    nblk = s // SUBLANES
    a, b = a_ref[...], h_ref[...]
    row = jnp.bitwise_and(lax.broadcasted_iota(jnp.int32, (s, c), 0), SUBLANES - 1)
    for d in (1, 2, 4):
        inside = (row < SUBLANES - d) if reverse else (row >= d)
        shift = s - d if reverse else d
        b = a * jnp.where(inside, pltpu.roll(b, shift, 0), 0.0) + b
        a = a * jnp.where(inside, pltpu.roll(a, shift, 0), 1.0)
    a_ref[...] = a
    h_ref[...] = b
    leaving = 0 if reverse else SUBLANES - 1

    def chain(i, carry):
        r0 = pl.multiple_of((nblk - 1 - i if reverse else i) * SUBLANES, SUBLANES)
        h = a_ref[pl.ds(r0, SUBLANES), :] * carry + h_ref[pl.ds(r0, SUBLANES), :]
        h_ref[pl.ds(r0, SUBLANES), :] = h
        return jnp.broadcast_to(h[leaving:leaving + 1, :], (SUBLANES, c))

    lax.fori_loop(0, nblk, chain, jnp.zeros((SUBLANES, c), F32))


def _lru_gates(xc, wa_ref, ba_ref, wx_ref, bx_ref, lam_ref):
    xcb = xc.astype(BF16)
    ra = _sigmoid(jnp.dot(xcb, wa_ref[...].astype(BF16), preferred_element_type=F32) + ba_ref[...])
    ia = _sigmoid(jnp.dot(xcb, wx_ref[...].astype(BF16), preferred_element_type=F32) + bx_ref[...])
    neg = -lam_ref[...]
    sp = jnp.maximum(neg, 0.0) + jnp.log1p(jnp.exp(-jnp.abs(neg)))
    log_a = -LRU_C * ra * sp
    a = jnp.exp(log_a)
    a2 = jnp.exp(2.0 * log_a)
    sq = jnp.sqrt(-jnp.tanh(log_a) * (a2 + 1.0))
    return ra, ia, sp, a, a2, sq


def _mixer_b_specs(seq, hd, sgw, lw):
    x_col = (2 * sgw) // hd
    y_col = (2 * sgw + lw) // hd
    tile = lambda col: pl.BlockSpec((seq, hd), lambda h, b: (b, col + h))
    vec = pl.BlockSpec((1, hd), lambda h, b: (0, h))
    mat = pl.BlockSpec((None, hd, hd), lambda h, b: (h, 0, 0))
    return tile(x_col), tile(y_col), tile(0), vec, mat


def _mixer_b_fwd(proj, conv_w, conv_b, wa, ba, wx, bx, lam, *, seq, sgw, lw, name):
    t = proj.shape[0]
    hd = lw // LRU_HEADS
    k_taps = conv_w.shape[0]
    x_spec, y_spec, o_spec, vec, mat = _mixer_b_specs(seq, hd, sgw, lw)

    def kern(xr_ref, yr_ref, cw_ref, cb_ref, wa_ref, ba_ref, wx_ref, bx_ref, lam_ref, o_ref, s_a, s_h):
        xc = _causal_conv(xr_ref[...], cw_ref[...], cb_ref[...])
        _, ia, _, a, _, sq = _lru_gates(xc, wa_ref, ba_ref, wx_ref, bx_ref, lam_ref)
        s_a[...] = a
        s_h[...] = sq * (ia * xc)
        _scan_rows(s_a, s_h, False)
        o_ref[...] = (s_h[...] * _gelu(yr_ref[...])).astype(BF16)

    return pl.pallas_call(
        kern, name=name, grid=(LRU_HEADS, t // seq),
        in_specs=[x_spec, y_spec, pl.BlockSpec((k_taps, hd), lambda h, b: (0, h)), vec, mat, vec, mat, vec, vec],
        out_specs=o_spec, out_shape=jax.ShapeDtypeStruct((t, lw), BF16),
        scratch_shapes=[pltpu.VMEM((seq, hd), F32), pltpu.VMEM((seq, hd), F32)],
        compiler_params=_params(2))(proj, proj, conv_w, conv_b, wa, ba, wx, bx, lam)


def _mixer_b_bwd(proj, d_yb, conv_w, conv_b, wa, wa_t, ba, wx, wx_t, bx, lam, *, seq, sgw, lw, name, exchanges=()):
    t = proj.shape[0]
    hd = lw // LRU_HEADS
    k_taps = conv_w.shape[0]
    x_spec, y_spec, o_spec, vec, mat = _mixer_b_specs(seq, hd, sgw, lw)
    cw_spec = pl.BlockSpec((k_taps, hd), lambda h, b: (0, h))

    def kern(xr_ref, yr_ref, dyb_ref, cw_ref, cb_ref, wa_ref, wat_ref, ba_ref, wx_ref, wxt_ref, bx_ref, lam_ref,
             dxr_ref, dyr_ref, dcw_ref, dcb_ref, dwa_ref, dba_ref, dwx_ref, dbx_ref, dlam_ref,
             s_xc, s_a, s_h, s_lam, s_dpa, s_dpx):
        @pl.when(pl.program_id(1) == 0)
        def _():
            for ref in (dcw_ref, dcb_ref, dwa_ref, dba_ref, dwx_ref, dbx_ref, dlam_ref):
                ref[...] = jnp.zeros_like(ref)

        s_xc[...] = _causal_conv(xr_ref[...], cw_ref[...], cb_ref[...])
        _, ia, _, a, _, sq = _lru_gates(s_xc[...], wa_ref, ba_ref, wx_ref, bx_ref, lam_ref)
        s_a[...] = a
        s_dpa[...] = _shift_up(a, 1)
        s_h[...] = sq * (ia * s_xc[...])
        _scan_rows(s_a, s_h, False)

        gel, dgel = _gelu_and_grad(yr_ref[...])
        dyb = dyb_ref[...]
        dyr_ref[...] = (dyb * s_h[...] * dgel).astype(BF16)
        s_lam[...] = dyb * gel
        _scan_rows(s_dpa, s_lam, True)
        ra, ia, sp, a, a2, sq = _lru_gates(s_xc[...], wa_ref, ba_ref, wx_ref, bx_ref, lam_ref)
        d_gx = s_lam[...]
        d_a = d_gx * _shift_down(s_h[...], 1)
        xc = s_xc[...]
        d_sq = d_gx * (ia * xc)
        d_ia = d_gx * (sq * xc)
        d_log_a = d_a * a - d_sq * (a2 / sq)
        d_ra = d_log_a * (-LRU_C * sp)
        d_sp = jnp.sum(d_log_a * (-LRU_C * ra), axis=0, keepdims=True)
        dlam_ref[...] += d_sp * (-_sigmoid(-lam_ref[...]))
        d_pa = d_ra * (ra * (1.0 - ra))
        d_px = d_ia * (ia * (1.0 - ia))
        s_dpa[...] = d_pa
        s_dpx[...] = d_px
        dba_ref[...] += jnp.sum(d_pa, axis=0, keepdims=True)
        dbx_ref[...] += jnp.sum(d_px, axis=0, keepdims=True)
        xcb = s_xc[...].astype(BF16)
        d_pa_b = s_dpa[...].astype(BF16)
        d_px_b = s_dpx[...].astype(BF16)
        contract_rows = (((0,), (0,)), ((), ()))
        dwa_ref[...] += lax.dot_general(xcb, d_pa_b, contract_rows, preferred_element_type=F32)
        dwx_ref[...] += lax.dot_general(xcb, d_px_b, contract_rows, preferred_element_type=F32)
        d_xc = (s_lam[...] * (sq * ia)
                + jnp.dot(d_pa_b, wat_ref[...].astype(BF16), preferred_element_type=F32)
                + jnp.dot(d_px_b, wxt_ref[...].astype(BF16), preferred_element_type=F32))
        dcb_ref[...] += jnp.sum(d_xc, axis=0, keepdims=True)
        dcw_ref[...] += _causal_conv_bwd_w(d_xc, xr_ref[...], k_taps)
        dxr_ref[...] = _causal_conv_bwd_x(d_xc, cw_ref[...]).astype(BF16)

    tile_shape = jax.ShapeDtypeStruct((t, lw), BF16)
    vec_shape = jax.ShapeDtypeStruct((1, lw), F32)
    mat_shape = jax.ShapeDtypeStruct((LRU_HEADS, hd, hd), F32)
    return _pallas(
        kern, name=name, grid=(LRU_HEADS, t // seq),
        in_specs=[x_spec, y_spec, o_spec, cw_spec, vec, mat, mat, vec, mat, mat, vec, vec],
        out_specs=[o_spec, o_spec, cw_spec, vec, mat, vec, mat, vec, vec],
        out_shape=[tile_shape, tile_shape, jax.ShapeDtypeStruct((k_taps, lw), F32), vec_shape, mat_shape, vec_shape,
                   mat_shape, vec_shape, vec_shape],
        operands=[proj, proj, d_yb, conv_w, conv_b, wa, wa_t, ba, wx, wx_t, bx, lam],
        scratch_shapes=[pltpu.VMEM((seq, hd), F32)] * 6, exchanges=exchanges)


FFN_TILE = 256


def _ffn_mid_fwd(up_pre, conv_w, conv_b, *, seq, name):
    t, f2 = up_pre.shape
    f = f2 // 2
    tc = _tile(f, FFN_TILE, LANES)
    nf = f // tc
    k_taps = conv_w.shape[0]

    def kern(pg_ref, pv_ref, wg_ref, wv_ref, bg_ref, bv_ref, o_ref):
        cg = _causal_conv(pg_ref[...], wg_ref[...], bg_ref[...])
        cv = _causal_conv(pv_ref[...], wv_ref[...], bv_ref[...])
        o_ref[...] = (_gelu(cg) * cv).astype(BF16)

    tile = lambda off: pl.BlockSpec((seq, tc), lambda j, b: (b, off + j))
    wspec = lambda off: pl.BlockSpec((k_taps, tc), lambda j, b: (0, off + j))
    bspec = lambda off: pl.BlockSpec((1, tc), lambda j, b: (0, off + j))
    return pl.pallas_call(
        kern, name=name, grid=(nf, t // seq),
        in_specs=[tile(0), tile(nf), wspec(0), wspec(nf), bspec(0), bspec(nf)], out_specs=tile(0),
        out_shape=jax.ShapeDtypeStruct((t, f), BF16),
        compiler_params=_params(2))(up_pre, up_pre, conv_w, conv_w, conv_b, conv_b)


def _ffn_mid_bwd(up_pre, d_act, conv_w, conv_b, *, seq, name):
    t, f2 = up_pre.shape
    f = f2 // 2
    tc = _tile(f, FFN_TILE, LANES)
    nf = f // tc
    k_taps = conv_w.shape[0]

    def kern(pg_ref, pv_ref, da_ref, wg_ref, wv_ref, bg_ref, bv_ref, dpg_ref, dpv_ref, dwg_ref, dwv_ref, dbg_ref, dbv_ref):
        @pl.when(pl.program_id(1) == 0)
        def _():
            for ref in (dwg_ref, dwv_ref, dbg_ref, dbv_ref):
                ref[...] = jnp.zeros_like(ref)

        pg = pg_ref[...]
        pv = pv_ref[...]
        gel, dgel = _gelu_and_grad(_causal_conv(pg, wg_ref[...], bg_ref[...]))
        cv = _causal_conv(pv, wv_ref[...], bv_ref[...])
        d_act_v = da_ref[...]
        d_cg = d_act_v * cv * dgel
        d_cv = d_act_v * gel
        dpg_ref[...] = _causal_conv_bwd_x(d_cg, wg_ref[...]).astype(BF16)
        dpv_ref[...] = _causal_conv_bwd_x(d_cv, wv_ref[...]).astype(BF16)
        dwg_ref[...] += _causal_conv_bwd_w(d_cg, pg, k_taps)
        dwv_ref[...] += _causal_conv_bwd_w(d_cv, pv, k_taps)
        dbg_ref[...] += jnp.sum(d_cg, axis=0, keepdims=True)
        dbv_ref[...] += jnp.sum(d_cv, axis=0, keepdims=True)

    tile = lambda off: pl.BlockSpec((seq, tc), lambda j, b: (b, off + j))
    wspec = lambda off: pl.BlockSpec((k_taps, tc), lambda j, b: (0, off + j))
    bspec = lambda off: pl.BlockSpec((1, tc), lambda j, b: (0, off + j))
    half = jax.ShapeDtypeStruct((t, f), BF16)
    wshape = jax.ShapeDtypeStruct((k_taps, f), F32)
    bshape = jax.ShapeDtypeStruct((1, f), F32)
    d_pg, d_pv, d_wg, d_wv, d_bg, d_bv = pl.pallas_call(
        kern, name=name, grid=(nf, t // seq),
        in_specs=[tile(0), tile(nf), tile(0), wspec(0), wspec(nf), bspec(0), bspec(nf)],
        out_specs=[tile(0), tile(0), wspec(0), wspec(0), bspec(0), bspec(0)],
        out_shape=[half, half, wshape, wshape, bshape, bshape],
        compiler_params=_params(2))(up_pre, up_pre, d_act, conv_w, conv_w, conv_b, conv_b)
    return (jnp.concatenate([d_pg, d_pv], axis=1), jnp.concatenate([d_wg, d_wv], axis=1),
            jnp.concatenate([d_bg, d_bv], axis=1))


ELEM_VMEM_BYTES = 24 << 20


def _as_2d(a):
    if a.ndim >= 2 and a.shape[-1] % LANES == 0 and a.size // a.shape[-1] >= SUBLANES:
        return a.reshape(-1, a.shape[-1])
    return a.reshape(-1, LANES)


def _row_tile(rows, bytes_per_row):
    return _tile(rows, max(16, ELEM_VMEM_BYTES // (2 * bytes_per_row)), 16)


def _cast_bf16(a, *, name):
    v = _as_2d(a)
    rows, cols = v.shape
    tr = _row_tile(rows, cols * (4 + 2))

    def kern(x_ref, o_ref):
        o_ref[...] = x_ref[...].astype(BF16)

    spec = pl.BlockSpec((tr, cols), lambda i: (i, 0))
    out = pl.pallas_call(kern, name=name, grid=(rows // tr,), in_specs=[spec], out_specs=spec,
                         out_shape=jax.ShapeDtypeStruct(v.shape, BF16), compiler_params=_params(1))(v)
    return out.reshape(a.shape)


def _add_sibling_part(own, core, got, *, name):
    _, _, rows, cols = own.shape
    tr = _row_tile(rows, cols * (2 + 2 + 2))

    def kern(core_ref, a_ref, b_ref, o_ref):
        o_ref[...] = (a_ref[...].astype(F32) + b_ref[...].astype(F32)).astype(BF16)

    spec = pl.BlockSpec((None, tr, cols), lambda ch, i, core_ref: (ch, i, 0))
    grid_spec = pltpu.PrefetchScalarGridSpec(
        num_scalar_prefetch=1, grid=(4, rows // tr),
        in_specs=[pl.BlockSpec((None, None, tr, cols), lambda ch, i, core_ref: (ch, core_ref[0], i, 0)), spec],
        out_specs=spec)
    return pl.pallas_call(kern, name=name, grid_spec=grid_spec, out_shape=jax.ShapeDtypeStruct(got.shape, BF16),
                          compiler_params=_params(2))(core, own, got)


def _sum_parts(parts, *, name):
    n_parts, rows, cols = parts.shape
    tr = _row_tile(rows, cols * 4 * (n_parts + 1))

    def kern(p_ref, o_ref):
        acc = p_ref[0].astype(F32)
        for p in range(1, n_parts):
            acc = acc + p_ref[p].astype(F32)
        o_ref[...] = acc

    return pl.pallas_call(
        kern, name=name, grid=(rows // tr,), in_specs=[pl.BlockSpec((n_parts, tr, cols), lambda i: (0, i, 0))],
        out_specs=pl.BlockSpec((tr, cols), lambda i: (i, 0)), out_shape=jax.ShapeDtypeStruct((rows, cols), F32),
        compiler_params=_params(1))(parts)


def _adamw(w, m, v, grad_parts, *, name):
    shape = w.shape
    w2 = _as_2d(w)
    rows, cols = w2.shape
    n_parts = grad_parts.shape[0]
    parts = grad_parts.reshape(n_parts, rows, cols)
    tr = _row_tile(rows, cols * (3 * 4 + n_parts * parts.dtype.itemsize + 4 * 4))
    c_m = 1.0 - ADAM_B1 ** ADAM_STEP
    c_v = 1.0 - ADAM_B2 ** ADAM_STEP

    def kern(w_ref, m_ref, v_ref, p_ref, g_ref, d_ref, nm_ref, nv_ref):
        g = p_ref[0].astype(F32)
        for p in range(1, n_parts):
            g = g + p_ref[p].astype(F32)
        new_m = ADAM_B1 * m_ref[...] + (1.0 - ADAM_B1) * g
        new_v = ADAM_B2 * v_ref[...] + (1.0 - ADAM_B2) * (g * g)
        g_ref[...] = g
        nm_ref[...] = new_m
        nv_ref[...] = new_v
        d_ref[...] = -ADAM_LR * ((new_m / c_m) / (jnp.sqrt(new_v / c_v) + ADAM_EPS) + ADAM_WD * w_ref[...])

    spec = pl.BlockSpec((tr, cols), lambda i: (i, 0))
    out = jax.ShapeDtypeStruct((rows, cols), F32)
    res = pl.pallas_call(
        kern, name=name, grid=(rows // tr,),
        in_specs=[spec, spec, spec, pl.BlockSpec((n_parts, tr, cols), lambda i: (0, i, 0))],
        out_specs=[spec] * 4, out_shape=[out] * 4, compiler_params=_params(1))(w2, _as_2d(m), _as_2d(v), parts)
    return [r.reshape(shape) for r in res]


def _all_gather(shards, *, name):
    n = len(shards)

    def body(*refs):
        ins, outs = refs[:n], refs[n:2 * n]
        send_sems, recv_sems, local_sems = refs[2 * n:]
        x, y, c = _place()
        me, sibling = (x, y, c), (x, y, 1 - c)
        chips = [(1 - x, y), (x, 1 - y), (1 - x, 1 - y)]

        def slot(i, dev):
            return outs[i].at[4 * dev[0] + 2 * dev[1] + dev[2]]

        def copy(i, k, block, to, src=None):
            return pltpu.make_async_remote_copy(
                src_ref=slot(i, block) if src is None else src, dst_ref=slot(i, block),
                send_sem=send_sems.at[i, k], recv_sem=recv_sems.at[i, k], device_id=to, device_id_type=MESH)

        mine = [pltpu.make_async_copy(ins[i], slot(i, me), local_sems.at[i]) for i in range(n)]
        for cp in mine:
            cp.start()
        first = []
        for i in range(n):
            first.append(copy(i, 0, me, sibling, src=ins[i]))
            first += [copy(i, 1 + j, me, (*chip, c), src=ins[i]) for j, chip in enumerate(chips)]
        for cp in first:
            cp.start()
        passed = []
        for j, chip in enumerate(chips):
            for i in range(n):
                copy(i, 1 + j, (*chip, c), me).wait_recv()
                onward = copy(i, 4 + j, (*chip, c), sibling)
                onward.start()
                passed.append(onward)
        for i in range(n):
            copy(i, 0, sibling, me).wait_recv()
            for j, chip in enumerate(chips):
                copy(i, 4 + j, (*chip, 1 - c), me).wait_recv()
        for cp in first + passed:
            cp.wait_send()
        for cp in mine:
            cp.wait()

    return pl.pallas_call(
        body, name=name, in_specs=[ANY] * n, out_specs=[ANY] * n,
        out_shape=[jax.ShapeDtypeStruct((N_DEV,) + s.shape, s.dtype) for s in shards],
        scratch_shapes=[pltpu.SemaphoreType.DMA((n, 7)), pltpu.SemaphoreType.DMA((n, 7)), pltpu.SemaphoreType.DMA((n,))],
    )(*shards)


def _by_chip_and_core(grad):
    return grad.reshape(4, 2, -1, grad.shape[-1])


def _pack(vectors):
    flat = [v.reshape(-1).astype(F32) for v in vectors]
    sizes = [f.shape[0] for f in flat]
    total = sum(sizes)
    padded = -(-total // (SUBLANES * LANES)) * (SUBLANES * LANES)
    if padded > total:
        flat.append(jnp.zeros((padded - total,), F32))
    return jnp.concatenate(flat).reshape(-1, LANES), sizes


def _unpack(packed, sizes, shapes):
    flat = packed.reshape(-1)
    out, off = [], 0
    for size, shape in zip(sizes, shapes):
        out.append(flat[off:off + size].reshape(shape))
        off += size
    return out


def kernel(x, g_mix, w_in, sg_ln_g, sg_ln_b, sg_w, sg_b, lru_conv_w, lru_conv_b, lru_wa, lru_ba, lru_wx, lru_bx, lru_lam, p_sg, p_lru, w_out, g_ffn, w_up, ffn_conv_w, ffn_conv_b, w_down, g_final, loss_target, m_g_mix, m_w_in, m_sg_ln_g, m_sg_ln_b, m_sg_w, m_sg_b, m_lru_conv_w, m_lru_conv_b, m_lru_wa, m_lru_ba, m_lru_wx, m_lru_bx, m_lru_lam, m_p_sg, m_p_lru, m_w_out, m_g_ffn, m_w_up, m_ffn_conv_w, m_ffn_conv_b, m_w_down, m_g_final, v_g_mix, v_w_in, v_sg_ln_g, v_sg_ln_b, v_sg_w, v_sg_b, v_lru_conv_w, v_lru_conv_b, v_lru_wa, v_lru_ba, v_lru_wx, v_lru_bx, v_lru_lam, v_p_sg, v_p_lru, v_w_out, v_g_ffn, v_w_up, v_ffn_conv_w, v_ffn_conv_b, v_w_down, v_g_final):
    weights = dict(g_mix=g_mix, w_in=w_in, sg_ln_g=sg_ln_g, sg_ln_b=sg_ln_b, sg_w=sg_w, sg_b=sg_b, lru_conv_w=lru_conv_w,
                   lru_conv_b=lru_conv_b, lru_wa=lru_wa, lru_ba=lru_ba, lru_wx=lru_wx, lru_bx=lru_bx, lru_lam=lru_lam,
                   p_sg=p_sg, p_lru=p_lru, w_out=w_out, g_ffn=g_ffn, w_up=w_up, ffn_conv_w=ffn_conv_w,
                   ffn_conv_b=ffn_conv_b, w_down=w_down, g_final=g_final)
    m_in = dict(g_mix=m_g_mix, w_in=m_w_in, sg_ln_g=m_sg_ln_g, sg_ln_b=m_sg_ln_b, sg_w=m_sg_w, sg_b=m_sg_b,
                lru_conv_w=m_lru_conv_w, lru_conv_b=m_lru_conv_b, lru_wa=m_lru_wa, lru_ba=m_lru_ba, lru_wx=m_lru_wx,
                lru_bx=m_lru_bx, lru_lam=m_lru_lam, p_sg=m_p_sg, p_lru=m_p_lru, w_out=m_w_out, g_ffn=m_g_ffn,
                w_up=m_w_up, ffn_conv_w=m_ffn_conv_w, ffn_conv_b=m_ffn_conv_b, w_down=m_w_down, g_final=m_g_final)
    v_in = dict(g_mix=v_g_mix, w_in=v_w_in, sg_ln_g=v_sg_ln_g, sg_ln_b=v_sg_ln_b, sg_w=v_sg_w, sg_b=v_sg_b,
                lru_conv_w=v_lru_conv_w, lru_conv_b=v_lru_conv_b, lru_wa=v_lru_wa, lru_ba=v_lru_ba, lru_wx=v_lru_wx,
                lru_bx=v_lru_bx, lru_lam=v_lru_lam, p_sg=v_p_sg, p_lru=v_p_lru, w_out=v_w_out, g_ffn=v_g_ffn,
                w_up=v_w_up, ffn_conv_w=v_ffn_conv_w, ffn_conv_b=v_ffn_conv_b, w_down=v_w_down, g_final=v_g_final)
    order = list(weights)

    n_seq, seq, d = x.shape
    t = n_seq * seq
    sgw = sg_ln_g.shape[-1]
    lw = lru_lam.shape[-1]
    hd = lw // LRU_HEADS
    f2 = ffn_conv_b.shape[-1]
    gate_col = (2 * sgw + 2 * lw) // d
    xi, yi, ci = _place()
    dev = 4 * xi + 2 * yi + ci

    core = jnp.reshape(ci, (1,)).astype(jnp.int32)
    first_leg = functools.partial(_gather_first_leg, place_own=False)
    chip_swap = functools.partial(_swap_with_chips, place_own=False)
    shards, in_flight = {}, {}

    def start_gather(keys, name):
        started = _start_exchange(first_leg, [shards[k] for k in keys], [_own_block_in_place(shards[k], dev) for k in keys],
                                  name=name)
        in_flight.update(zip(keys, started))

    def landed(keys, after, name):
        return _await_exchange(first_leg, [in_flight[k] for k in keys], after, name=name)

    needed_first = ["w_in", "lru_wa", "lru_wx", "p_sg"]
    shards.update({k: _cast_bf16(weights[k][0], name=f"cast_{k}") for k in needed_first})
    shards["taps"], tap_sizes = _pack([lru_conv_w[0], ffn_conv_w[0]])
    start_gather(needed_first + ["taps"], "start_first_gather")
    needed_later = ["p_lru", "w_out", "w_up", "w_down"]
    shards.update({k: _cast_bf16(weights[k][0], name=f"cast_{k}") for k in needed_later})
    start_gather(needed_later, "start_later_gather")

    def rows_in_order(g8):
        return g8.reshape(1, -1, g8.shape[-1])

    x2d = x.reshape(t, d)
    h1 = _rmsnorm_fwd(x2d, g_mix, name="norm_mix")
    ((w_in_g, wa_8, wx_8, p_sg_g, taps_8),) = _exchange_now(
        [_gather_second_leg(landed(needed_first + ["taps"], h1, "await_first_gather"))], name="second_leg_first_gather")
    wa_g, wx_g = (jnp.swapaxes(w8, 0, 1).reshape(LRU_HEADS, hd, hd) for w8 in (wa_8, wx_8))
    wa_t, wx_t = jnp.swapaxes(wa_g, 1, 2), jnp.swapaxes(wx_g, 1, 2)
    tap_parts = [_unpack(taps_8[k], tap_sizes, [lru_conv_w.shape[1:], ffn_conv_w.shape[1:]]) for k in range(N_DEV)]
    lru_cw = jnp.concatenate([p[0] for p in tap_parts], axis=1)
    ffn_cw = jnp.concatenate([p[1] for p in tap_parts], axis=1)
    sg_w0 = sg_w[0]
    sg_w_t = jnp.swapaxes(sg_w0, 1, 2)
    sg_b_t = sg_b[0].T
    proj, _ = _mm_nn(h1, w_in_g, out_dtype=F32, name="proj_in")
    y_a = _mixer_a_fwd(proj, sg_ln_g, sg_ln_b, sg_w0, sg_b_t, sgw, name="mixer_a_fwd")
    y_b = _mixer_b_fwd(proj, lru_cw, lru_conv_b, wa_g, lru_ba, wx_g, lru_bx, lru_lam, seq=seq, sgw=sgw, lw=lw,
                       name="mixer_b_fwd")
    m_a, ((p_lru_8, w_out_8),) = _mm_nn(
        y_a, p_sg_g, out_dtype=F32, name="proj_sg",
        exchanges=[_gather_second_leg(landed(["p_lru", "w_out"], y_b, "await_p_lru_w_out"))])
    p_lru_g, w_out_g = rows_in_order(p_lru_8), rows_in_order(w_out_8)
    m_b, _ = _mm_nn(y_b, p_lru_g, out_dtype=F32, name="proj_lru")
    merged = _merge_fwd(m_a, m_b, proj, gate_col, name="merge_fwd")
    x1, ((w_up_g,),) = _mm_nn(merged, w_out_g, out_dtype=F32, residual=x2d, name="proj_out",
                              exchanges=[_gather_second_leg(landed(["w_up"], merged, "await_w_up"))])
    h2 = _rmsnorm_fwd(x1, g_ffn, name="norm_ffn")
    up_pre, _ = _mm_nn(h2, w_up_g, out_dtype=F32, name="ffn_up")
    act = _ffn_mid_fwd(up_pre, ffn_cw, ffn_conv_b, seq=seq, name="ffn_mid_fwd")
    ((w_down_8,),) = _exchange_now([_gather_second_leg(landed(["w_down"], act, "await_w_down"))], name="second_leg_w_down")
    w_down_g = rows_in_order(w_down_8)
    x2, _ = _mm_nn(act, w_down_g, out_dtype=F32, residual=x1, name="ffn_down")
    d_x2, d_x2_b, d_g_final, loss_part = _loss_head(x2, g_final.reshape(1, d), loss_target.reshape(t, d), name="loss_head")
    loss = lax.psum(loss_part[0, 0], ("x", "y", "c"))

    def by_rows(g):
        return g.reshape(N_DEV, -1, g.shape[-1])

    def by_head_rows(g):
        return jnp.swapaxes(g.reshape(LRU_HEADS, N_DEV, hd // N_DEV, hd), 0, 1)

    def start_sibling_swap(views, key):
        return _start_exchange(_swap_with_sibling, views, [lax.empty((4,) + v.shape[2:], v.dtype) for v in views],
                               name=f"start_sibling_{key}")

    chip_swaps = {}

    def sum_and_start_chip_swap(keys, views, sibling_swap, after):
        from_sibling = _await_exchange(_swap_with_sibling, sibling_swap, after, name=f"await_sibling_{keys[0]}")
        sums = [_add_sibling_part(v, core, s, name=f"chip_sum_{k}") for k, v, s in zip(keys, views, from_sibling)]
        started = _start_exchange(chip_swap, sums, [_own_block_first(s, 2 * xi + yi) for s in sums],
                                  name=f"start_chips_{keys[0]}")
        chip_swaps.update(zip(keys, started))

    d_w_down, _ = _mm_tn(act, d_x2_b, 1, name="grad_w_down")
    v_down = _by_chip_and_core(by_rows(d_w_down))
    swap = start_sibling_swap([v_down], "w_down")
    d_act, _ = _mm_nt(d_x2_b, w_down_g, name="bwd_ffn_down")
    sum_and_start_chip_swap(["w_down"], [v_down], swap, d_act)
    d_up_pre, d_ffn_cw, d_ffn_cb = _ffn_mid_bwd(up_pre, d_act, ffn_cw, ffn_conv_b, seq=seq, name="ffn_mid_bwd")
    d_w_up, _ = _mm_tn(h2, d_up_pre, N_DEV, name="grad_w_up")
    v_up = _by_chip_and_core(d_w_up)
    swap = start_sibling_swap([v_up], "w_up")
    d_h2, _ = _mm_nt(d_up_pre, w_up_g, name="bwd_ffn_up")
    sum_and_start_chip_swap(["w_up"], [v_up], swap, d_h2)
    d_x1, d_x1_b, d_g_ffn = _rmsnorm_bwd(x1, g_ffn, d_h2, d_x2, name="norm_ffn_bwd")
    d_w_out, _ = _mm_tn(merged, d_x1_b, 1, name="grad_w_out")
    v_out = _by_chip_and_core(by_rows(d_w_out))
    swap = start_sibling_swap([v_out], "w_out")
    d_merged, _ = _mm_nt(d_x1_b, w_out_g, name="bwd_proj_out")
    sum_and_start_chip_swap(["w_out"], [v_out], swap, d_merged)
    d_m_a, d_m_b, d_gates = _merge_bwd(d_merged, m_a, m_b, proj, gate_col, name="merge_bwd")
    d_p_sg, _ = _mm_tn(y_a, d_m_a, N_DEV, name="grad_p_sg")
    d_p_lru, _ = _mm_tn(y_b, d_m_b, 1, name="grad_p_lru")
    v_sg, v_lru = _by_chip_and_core(d_p_sg), _by_chip_and_core(by_rows(d_p_lru))
    swap = start_sibling_swap([v_sg, v_lru], "p_sg")
    d_y_a, _ = _mm_nt(d_m_a, p_sg_g, name="bwd_proj_sg")
    d_y_b, _ = _mm_nt(d_m_b, p_lru_g, name="bwd_proj_lru")
    sum_and_start_chip_swap(["p_sg", "p_lru"], [v_sg, v_lru], swap, d_y_b)
    d_zuv, d_sg_w, d_sg_b_t, d_ln_g, d_ln_b = _mixer_a_bwd(proj, d_y_a, sg_ln_g, sg_ln_b, sg_w0, sg_w_t, sg_b_t, sgw,
                                                           name="mixer_a_bwd")
    (d_xr, d_yr, d_lru_cw, d_lru_cb, d_wa, d_ba, d_wx, d_bx, d_lam), _ = _mixer_b_bwd(
        proj, d_y_b, lru_cw, lru_conv_b, wa_g, wa_t, lru_ba, wx_g, wx_t, lru_bx, lru_lam, seq=seq, sgw=sgw, lw=lw,
        name="mixer_b_bwd")
    d_proj = jnp.concatenate([d_zuv, d_xr, d_yr, d_gates], axis=1)
    d_w_in, _ = _mm_tn(h1, d_proj, N_DEV, name="grad_w_in")
    v_win = _by_chip_and_core(d_w_in)
    v_wa = _by_chip_and_core(_cast_bf16(by_head_rows(d_wa), name="cast_grad_wa"))
    v_wx = _by_chip_and_core(_cast_bf16(by_head_rows(d_wx), name="cast_grad_wx"))
    swap = start_sibling_swap([v_win, v_wa, v_wx], "w_in")
    sum_and_start_chip_swap(["w_in", "lru_wa", "lru_wx"], [v_win, v_wa, v_wx], swap, d_w_in)
    d_h1, _ = _mm_nt(d_proj, w_in_g, name="bwd_proj_in")
    grad_x, _, d_g_mix = _rmsnorm_bwd(x2d, g_mix, d_h1, d_x1, name="norm_mix_bwd")

    small = ["g_mix", "sg_ln_g", "sg_ln_b", "sg_w", "sg_b", "lru_conv_b", "lru_ba", "lru_bx", "lru_lam", "g_ffn",
             "ffn_conv_b", "g_final", "lru_conv_w", "ffn_conv_w"]
    small_parts = dict(g_mix=d_g_mix, sg_ln_g=d_ln_g, sg_ln_b=d_ln_b, sg_w=d_sg_w, sg_b=d_sg_b_t.T, lru_conv_b=d_lru_cb,
                       lru_ba=d_ba, lru_bx=d_bx, lru_lam=d_lam, g_ffn=d_g_ffn, ffn_conv_b=d_ffn_cb, g_final=d_g_final,
                       lru_conv_w=d_lru_cw, ffn_conv_w=d_ffn_cw)
    packed, sizes = _pack([small_parts[k] for k in small])
    (all_small,) = _all_gather([packed], name="gather_small_grads")
    small_sum = _sum_parts(all_small, name="sum_small_grads")
    small_grads = dict(zip(small, _unpack(small_sum, sizes, [small_parts[k].shape for k in small])))
    for k in ("lru_conv_w", "ffn_conv_w"):
        n_loc = weights[k].shape[-1]
        small_grads[k] = lax.dynamic_slice_in_dim(small_grads[k], dev * n_loc, n_loc, axis=1)

    grads, deltas, new_m, new_v = {}, {}, {}, {}

    def update(k, parts):
        grads[k], deltas[k], new_m[k], new_v[k] = _adamw(weights[k], m_in[k], v_in[k], parts, name=f"adamw_{k}")

    def update_when_landed(keys, after):
        parts = _await_exchange(chip_swap, [chip_swaps[k] for k in keys], after, name=f"await_chips_{keys[0]}")
        for k, p in zip(keys, parts):
            update(k, p)

    for keys in (["w_down"], ["w_up"], ["w_out"], ["p_sg", "p_lru"]):
        update_when_landed(keys, grad_x)
    for k in small:
        update(k, small_grads[k][None])
    update_when_landed(["w_in", "lru_wa", "lru_wx"], deltas["w_up"])

    return (loss, grad_x.reshape(x.shape), *[grads[k] for k in order], *[deltas[k] for k in order],
            *[new_m[k] for k in order], *[new_v[k] for k in order])
```

```python
import functools
import math
from typing import Callable, NamedTuple

import jax
import jax.numpy as jnp
from jax import lax
from jax.experimental import pallas as pl
from jax.experimental.pallas import tpu as pltpu

F32 = jnp.float32
BF16 = jnp.bfloat16
MESH = pl.DeviceIdType.MESH
ANY = pl.BlockSpec(memory_space=pl.ANY)

N_DEV = 8
EPS = 1e-6
CHUNK = 128
SG_GROUPS = 8
LRU_HEADS = 16
LRU_C = 8.0
ADAM_LR = 0.001
ADAM_B1 = 0.9
ADAM_B2 = 0.999
ADAM_EPS = 1e-08
ADAM_WD = 0.01
ADAM_STEP = 10

V7X_VMEM_LIMIT = 56 * 1024 * 1024
LANES = 128
SUBLANES = 8
MXU = 256

_GELU_C0 = math.sqrt(2.0 / math.pi)
_GELU_C1 = 0.044715


def _params(n_axes):
    return pltpu.CompilerParams(dimension_semantics=("arbitrary",) * n_axes, vmem_limit_bytes=V7X_VMEM_LIMIT)


def _tile(dim, pref, align):
    t = (min(pref, dim) // align) * align
    while t >= align:
        if dim % t == 0:
            return t
        t -= align
    return dim


def _gelu(x):
    return x * (0.5 * (1.0 + jnp.tanh(_GELU_C0 * (x + _GELU_C1 * (x * x * x)))))


def _gelu_and_grad(x):
    t = jnp.tanh(_GELU_C0 * (x + _GELU_C1 * (x * x * x)))
    cdf = 0.5 * (1.0 + t)
    dcdf = 0.5 * (1.0 - t * t) * (_GELU_C0 * (1.0 + 3.0 * _GELU_C1 * (x * x)))
    return x * cdf, cdf + x * dcdf


def _sigmoid(x):
    return 1.0 / (1.0 + jnp.exp(-x))


def _shift_down(x, d):
    if d == 0:
        return x
    row = lax.broadcasted_iota(jnp.int32, x.shape, 0)
    return jnp.where(row >= d, pltpu.roll(x, d, 0), 0.0)


def _shift_up(x, d):
    if d == 0:
        return x
    s = x.shape[0]
    row = lax.broadcasted_iota(jnp.int32, x.shape, 0)
    return jnp.where(row < s - d, pltpu.roll(x, s - d, 0), 0.0)


def _causal_conv(x, w, b):
    k_taps = w.shape[0]
    out = _shift_down(x, k_taps - 1) * w[0:1, :]
    for k in range(1, k_taps):
        out = out + _shift_down(x, k_taps - 1 - k) * w[k:k + 1, :]
    return out + b


def _causal_conv_bwd_x(d_out, w):
    k_taps = w.shape[0]
    d_x = _shift_up(d_out, k_taps - 1) * w[0:1, :]
    for k in range(1, k_taps):
        d_x = d_x + _shift_up(d_out, k_taps - 1 - k) * w[k:k + 1, :]
    return d_x


def _causal_conv_bwd_w(d_out, x, k_taps):
    rows = [jnp.sum(d_out * _shift_down(x, k_taps - 1 - k), axis=0, keepdims=True) for k in range(k_taps)]
    return jnp.concatenate(rows, axis=0)


def _place():
    return lax.axis_index("x"), lax.axis_index("y"), lax.axis_index("c")


def _other_chips(x, y):
    return [(1 - x, y), (x, 1 - y), (1 - x, 1 - y)]


class _Exchange(NamedTuple):
    ins: tuple
    outs: tuple
    in_place: bool
    n_remote: int
    n_local: int
    copies: Callable


def _remote(src, dst, send_sem, recv_sem, to):
    return pltpu.make_async_remote_copy(src_ref=src, dst_ref=dst, send_sem=send_sem, recv_sem=recv_sem, device_id=to,
                                        device_id_type=MESH)


def _gather_first_leg(shards, place_own=True):
    n = len(shards)

    def copies(ins, outs, send_sems, recv_sems, local_sems):
        x, y, c = _place()
        peers = [(x, y, 1 - c)] + [(*chip, c) for chip in _other_chips(x, y)]
        slot = lambda i, dev: outs[i].at[4 * dev[0] + 2 * dev[1] + dev[2]]
        sends = [_remote(ins[i], slot(i, (x, y, c)), send_sems[i].at[k], recv_sems[i].at[k], to)
                 for i in range(n) for k, to in enumerate(peers)]
        receives = [_remote(ins[i], slot(i, frm), send_sems[i].at[k], recv_sems[i].at[k], frm)
                    for i in range(n) for k, frm in enumerate(peers)]
        local = [pltpu.make_async_copy(ins[i], slot(i, (x, y, c)), local_sems[i].at[0]) for i in range(n)] if place_own else []
        return sends, receives, local

    outs = tuple(jax.ShapeDtypeStruct((N_DEV,) + s.shape, s.dtype) for s in shards)
    return _Exchange(tuple(shards), outs, False, 4, int(place_own), copies)


def _gather_second_leg(gathered):
    n = len(gathered)

    def copies(ins, outs, send_sems, recv_sems, local_sems):
        x, y, c = _place()
        slot = lambda i, chip, core: outs[i].at[4 * chip[0] + 2 * chip[1] + core]
        sends = [_remote(slot(i, chip, c), slot(i, chip, c), send_sems[i].at[j], recv_sems[i].at[j], (x, y, 1 - c))
                 for i in range(n) for j, chip in enumerate(_other_chips(x, y))]
        receives = [_remote(slot(i, chip, 1 - c), slot(i, chip, 1 - c), send_sems[i].at[j], recv_sems[i].at[j], (x, y, 1 - c))
                    for i in range(n) for j, chip in enumerate(_other_chips(x, y))]
        return sends, receives, []

    outs = tuple(jax.ShapeDtypeStruct(g.shape, g.dtype) for g in gathered)
    return _Exchange(tuple(gathered), outs, True, 3, 0, copies)


def _swap_with_sibling(parts):
    n = len(parts)

    def copies(ins, outs, send_sems, recv_sems, local_sems):
        x, y, c = _place()
        both = [_remote(ins[i].at[ch, 1 - c], outs[i].at[ch], send_sems[i].at[ch], recv_sems[i].at[ch], (x, y, 1 - c))
                for i in range(n) for ch in range(4)]
        return both, both, []

    outs = tuple(jax.ShapeDtypeStruct((4,) + p.shape[2:], p.dtype) for p in parts)
    return _Exchange(tuple(parts), outs, False, 4, 0, copies)


def _swap_with_chips(parts, place_own=True):
    n = len(parts)

    def copies(ins, outs, send_sems, recv_sems, local_sems):
        x, y, c = _place()
        both = [_remote(ins[i].at[2 * chip[0] + chip[1]], outs[i].at[1 + j], send_sems[i].at[j], recv_sems[i].at[j], (*chip, c))
                for i in range(n) for j, chip in enumerate(_other_chips(x, y))]
        local = [pltpu.make_async_copy(ins[i].at[2 * x + y], outs[i].at[0], local_sems[i].at[0]) for i in range(n)] if place_own else []
        return both, both, local

    outs = tuple(jax.ShapeDtypeStruct(p.shape, p.dtype) for p in parts)
    return _Exchange(tuple(parts), outs, False, 3, int(place_own), copies)


def _exchange_plumbing(exchanges):
    operands = [a for ex in exchanges for a in ex.ins]
    results = [s for ex in exchanges for s in ex.outs]
    scratch, in_place, at = [], {}, 0
    for ex in exchanges:
        n = len(ex.ins)
        scratch += [pltpu.SemaphoreType.DMA((n, ex.n_remote))] * 2
        if ex.n_local:
            scratch.append(pltpu.SemaphoreType.DMA((n, ex.n_local)))
        if ex.in_place:
            in_place.update({at + i: at + i for i in range(n)})
        at += n

    def copies(in_refs, out_refs, sem_refs):
        sends, receives, local = [], [], []
        at, sem_at = 0, 0
        for ex in exchanges:
            n, n_sem = len(ex.ins), 3 if ex.n_local else 2
            per_operand = [[sem.at[i] for i in range(n)] for sem in sem_refs[sem_at:sem_at + n_sem]] + [[]] * (3 - n_sem)
            s, r, l = ex.copies(in_refs[at:at + n], out_refs[at:at + n], *per_operand)
            sends, receives, local = sends + s, receives + r, local + l
            at, sem_at = at + n, sem_at + n_sem
        return sends, receives, local

    return operands, results, scratch, in_place, copies


def _start_all(copies):
    sends, _, local = copies
    for cp in local + sends:
        cp.start()


def _wait_all(copies):
    sends, receives, local = copies
    for cp in receives:
        cp.wait_recv()
    for cp in sends:
        cp.wait_send()
    for cp in local:
        cp.wait()


def _exchange_now(exchanges, *, name):
    operands, results, scratch, in_place, copies = _exchange_plumbing(exchanges)
    n = len(operands)

    def body(*refs):
        made = copies(refs[:n], refs[n:2 * n], refs[2 * n:])
        _start_all(made)
        _wait_all(made)

    out = pl.pallas_call(body, name=name, in_specs=[ANY] * n, out_specs=[ANY] * n, out_shape=results,
                         scratch_shapes=scratch, input_output_aliases=in_place)(*operands)
    return _split(out, exchanges)


def _split(flat, exchanges):
    out, at = [], 0
    for ex in exchanges:
        out.append(list(flat[at:at + len(ex.ins)]))
        at += len(ex.ins)
    return out


HBM = pl.BlockSpec(memory_space=pltpu.HBM)
SEMAPHORES = pl.BlockSpec(memory_space=pltpu.SEMAPHORE)
SPLIT_COPY = pltpu.CompilerParams(has_side_effects=pltpu.SideEffectType.DATAFLOW_SIDE_EFFECTING)


def _start_exchange(make, operands, landings, before, *, name):
    n = len(operands)
    ex = make(operands)

    def body(*refs):
        sends, _, _ = ex.copies(refs[:n], refs[n:2 * n], refs[2 * n + 1:3 * n + 1], refs[3 * n + 1:4 * n + 1], [])
        for cp in sends:
            cp.start()

    buffers = [pltpu.with_memory_space_constraint(a, pltpu.HBM) for a in list(operands) + list(landings) + [before]]
    out = pl.pallas_call(
        body, name=name, in_specs=[HBM] * (2 * n + 1), out_specs=[SEMAPHORES] * (2 * n) + [HBM] * (2 * n + 1),
        out_shape=[pltpu.SemaphoreType.DMA((ex.n_remote,))] * (2 * n) + [pltpu.HBM(a.shape, a.dtype) for a in buffers],
        input_output_aliases={i: 2 * n + i for i in range(2 * n + 1)}, compiler_params=SPLIT_COPY)(*buffers)
    return [(out[i], out[n + i], out[2 * n + i], out[3 * n + i]) for i in range(n)], out[4 * n]


def _await_exchange(make, in_flight, after, *, name):
    n = len(in_flight)
    send_sems, recv_sems, operands, landings = zip(*in_flight)
    ex = make(operands)

    def body(*refs):
        sends, receives, _ = ex.copies(refs[:n], refs[n:2 * n], refs[2 * n:3 * n], refs[3 * n:4 * n], [])
        for cp in receives:
            cp.wait_recv()
        for cp in sends:
            cp.wait_send()

    out = pl.pallas_call(
        body, name=name, in_specs=[HBM] * (2 * n) + [SEMAPHORES] * (2 * n) + [ANY], out_specs=[HBM] * (2 * n),
        out_shape=[pltpu.HBM(a.shape, a.dtype) for a in operands + landings],
        input_output_aliases={i: i for i in range(2 * n)}, compiler_params=SPLIT_COPY,
    )(*operands, *landings, *send_sems, *recv_sems, after)
    return list(out[n:])


def _own_block_first(blocks, index):
    own = lax.dynamic_index_in_dim(blocks, index, 0, keepdims=True)
    return lax.dynamic_update_index_in_dim(lax.empty(blocks.shape, blocks.dtype), own, 0, 0)


def _own_block_in_place(shard, index):
    return lax.dynamic_update_index_in_dim(lax.empty((N_DEV,) + shard.shape, shard.dtype), shard, index, 0)


def _pallas(kern, *, name, grid, in_specs, out_specs, out_shape, operands, scratch_shapes=(), exchanges=()):
    ex_operands, ex_results, ex_scratch, in_place, copies = _exchange_plumbing(exchanges)
    n_in, n_out, n_scratch, n_ex = len(in_specs), len(out_specs), len(scratch_shapes), len(ex_operands)

    def body(*refs):
        ins, refs = refs[:n_in], refs[n_in:]
        ex_ins, refs = refs[:n_ex], refs[n_ex:]
        outs, refs = refs[:n_out], refs[n_out:]
        ex_outs, refs = refs[:n_ex], refs[n_ex:]
        scratch, sems = refs[:n_scratch], refs[n_scratch:]
        if exchanges:
            first = functools.reduce(jnp.logical_and, [pl.program_id(a) == 0 for a in range(len(grid))])
            last = functools.reduce(jnp.logical_and, [pl.program_id(a) == g - 1 for a, g in enumerate(grid)])

            @pl.when(first)
            def _():
                _start_all(copies(ex_ins, ex_outs, sems))

        kern(*ins, *outs, *scratch)
        if exchanges:
            @pl.when(last)
            def _():
                _wait_all(copies(ex_ins, ex_outs, sems))

    res = pl.pallas_call(
        body, name=name, grid=grid, in_specs=list(in_specs) + [ANY] * n_ex, out_specs=list(out_specs) + [ANY] * n_ex,
        out_shape=list(out_shape) + ex_results, scratch_shapes=list(scratch_shapes) + ex_scratch,
        input_output_aliases={n_in + i: n_out + o for i, o in in_place.items()},
        compiler_params=_params(len(grid)))(*operands, *ex_operands)
    return list(res[:n_out]), _split(res[n_out:], exchanges)


def _accumulate(step, n_steps, acc, value, finish):
    if n_steps == 1:
        finish(value)
        return

    @pl.when(step == 0)
    def _():
        acc[0][...] = value

    @pl.when(step > 0)
    def _():
        acc[0][...] += value

    @pl.when(step == n_steps - 1)
    def _():
        finish(acc[0][...])


def _mm_nn(a, w, *, out_dtype, name, residual=None, exchanges=()):
    m, k = a.shape
    nb, _, n_blk = w.shape
    tm, tn, tk = _tile(m, 512, MXU), _tile(n_blk, 1536, MXU), _tile(k, 4096, MXU)
    per = n_blk // tn
    nk = k // tk

    def kern(*refs):
        a_ref, w_ref = refs[:2]
        r_ref = None if residual is None else refs[2]
        o_ref, acc = refs[2 + (residual is not None)], refs[3 + (residual is not None):]

        def finish(total):
            o_ref[...] = (total if r_ref is None else total + r_ref[...]).astype(o_ref.dtype)

        _accumulate(pl.program_id(2), nk, acc, jnp.dot(a_ref[...], w_ref[...], preferred_element_type=F32), finish)

    tile = pl.BlockSpec((tm, tn), lambda j, i, kk: (i, j))
    in_specs = [pl.BlockSpec((tm, tk), lambda j, i, kk: (i, kk)),
                pl.BlockSpec((None, tk, tn), lambda j, i, kk: (j // per, kk, j % per))]
    operands = [a, w]
    if residual is not None:
        in_specs.append(tile)
        operands.append(residual)
    (out,), carried = _pallas(
        kern, name=name, grid=(nb * per, m // tm, nk), in_specs=in_specs, out_specs=[tile],
        out_shape=[jax.ShapeDtypeStruct((m, nb * n_blk), out_dtype)], operands=operands,
        scratch_shapes=[pltpu.VMEM((tm, tn), F32)] * (nk > 1), exchanges=exchanges)
    return out, carried


def _mm_nt(g, w, *, name, exchanges=()):
    m, n = g.shape
    nb, k, n_blk = w.shape
    tm, tko, tn = _tile(m, 1024, MXU), _tile(k, 1024, MXU), _tile(n_blk, 3072, MXU)
    per = n_blk // tn
    nn = n // tn

    def kern(g_ref, w_ref, o_ref, *acc):
        def finish(total):
            o_ref[...] = total

        part = lax.dot_general(g_ref[...], w_ref[...], (((1,), (1,)), ((), ())), preferred_element_type=F32)
        _accumulate(pl.program_id(2), nn, acc, part, finish)

    (out,), carried = _pallas(
        kern, name=name, grid=(k // tko, m // tm, nn),
        in_specs=[pl.BlockSpec((tm, tn), lambda j, i, jn: (i, jn)),
                  pl.BlockSpec((None, tko, tn), lambda j, i, jn: (jn // per, j, jn % per))],
        out_specs=[pl.BlockSpec((tm, tko), lambda j, i, jn: (i, j))],
        out_shape=[jax.ShapeDtypeStruct((m, k), F32)], operands=[g, w],
        scratch_shapes=[pltpu.VMEM((tm, tko), F32)] * (nn > 1), exchanges=exchanges)
    return out, carried


def _mm_tn(a, g, nb, *, name, exchanges=()):
    m, k = a.shape
    n = g.shape[1]
    n_blk = n // nb
    tko, tn, tm = _tile(k, 512, MXU), _tile(n_blk, 1536, MXU), _tile(m, 4096, MXU)
    per = n_blk // tn
    nm = m // tm

    def kern(a_ref, g_ref, o_ref, *acc):
        def finish(total):
            o_ref[...] = total.astype(o_ref.dtype)

        part = lax.dot_general(a_ref[...], g_ref[...], (((0,), (0,)), ((), ())), preferred_element_type=F32)
        _accumulate(pl.program_id(2), nm, acc, part, finish)

    (out,), carried = _pallas(
        kern, name=name, grid=(nb * per, k // tko, nm),
        in_specs=[pl.BlockSpec((tm, tko), lambda j, i, im: (im, i)),
                  pl.BlockSpec((tm, tn), lambda j, i, im: (im, j))],
        out_specs=[pl.BlockSpec((None, tko, tn), lambda j, i, im: (j // per, i, j % per))],
        out_shape=[jax.ShapeDtypeStruct((nb, k, n_blk), BF16)], operands=[a, g],
        scratch_shapes=[pltpu.VMEM((tko, tn), F32)] * (nm > 1), exchanges=exchanges)
    return out, carried


ROW_TILE = 128


def _rmsnorm_fwd(x, g, *, name):
    t, d = x.shape
    tr = _tile(t, ROW_TILE, SUBLANES)

    def kern(x_ref, g_ref, h_ref):
        xv = x_ref[...]
        r = lax.rsqrt(jnp.mean(xv * xv, axis=-1, keepdims=True) + EPS)
        h_ref[...] = (xv * r * g_ref[...]).astype(BF16)

    return pl.pallas_call(
        kern, name=name, grid=(t // tr,),
        in_specs=[pl.BlockSpec((tr, d), lambda i: (i, 0)), pl.BlockSpec((1, d), lambda i: (0, 0))],
        out_specs=pl.BlockSpec((tr, d), lambda i: (i, 0)),
        out_shape=jax.ShapeDtypeStruct((t, d), BF16), compiler_params=_params(1))(x, g)


def _rmsnorm_bwd(x, g, d_h, d_res, *, name):
    t, d = x.shape
    tr = _tile(t, ROW_TILE, SUBLANES)

    def kern(x_ref, g_ref, dh_ref, dres_ref, dx_ref, dxb_ref, dg_ref):
        xv = x_ref[...]
        r = lax.rsqrt(jnp.mean(xv * xv, axis=-1, keepdims=True) + EPS)
        dh = dh_ref[...]
        gy = dh * g_ref[...]
        dx = dres_ref[...] + r * gy - xv * (r * r * r) * jnp.mean(gy * xv, axis=-1, keepdims=True)
        dx_ref[...] = dx
        dxb_ref[...] = dx.astype(BF16)

        @pl.when(pl.program_id(0) == 0)
        def _():
            dg_ref[...] = jnp.zeros_like(dg_ref)

        dg_ref[...] += jnp.sum(dh * (xv * r), axis=0, keepdims=True)

    row = pl.BlockSpec((tr, d), lambda i: (i, 0))
    vec = pl.BlockSpec((1, d), lambda i: (0, 0))
    return pl.pallas_call(
        kern, name=name, grid=(t // tr,), in_specs=[row, vec, row, row], out_specs=[row, row, vec],
        out_shape=[jax.ShapeDtypeStruct((t, d), F32), jax.ShapeDtypeStruct((t, d), BF16),
                   jax.ShapeDtypeStruct((1, d), F32)], compiler_params=_params(1))(x, g, d_h, d_res)


def _loss_head(x, g, target, *, name):
    t, d = x.shape
    tr = _tile(t, ROW_TILE, SUBLANES)

    def kern(x_ref, g_ref, t_ref, dx_ref, dxb_ref, dg_ref, loss_ref):
        xv = x_ref[...]
        gv = g_ref[...]
        r = lax.rsqrt(jnp.mean(xv * xv, axis=-1, keepdims=True) + EPS)
        diff = xv * r * gv - t_ref[...]
        dy = diff * (1.0 / d)
        gy = dy * gv
        dx = r * gy - xv * (r * r * r) * jnp.mean(gy * xv, axis=-1, keepdims=True)
        dx_ref[...] = dx
        dxb_ref[...] = dx.astype(BF16)

        @pl.when(pl.program_id(0) == 0)
        def _():
            dg_ref[...] = jnp.zeros_like(dg_ref)
            loss_ref[...] = jnp.zeros_like(loss_ref)

        dg_ref[...] += jnp.sum(dy * (xv * r), axis=0, keepdims=True)
        part = 0.5 * jnp.sum(jnp.mean(diff * diff, axis=-1, keepdims=True), axis=0, keepdims=True)
        loss_ref[...] += jnp.broadcast_to(part, loss_ref.shape)

    row = pl.BlockSpec((tr, d), lambda i: (i, 0))
    vec = pl.BlockSpec((1, d), lambda i: (0, 0))
    return pl.pallas_call(
        kern, name=name, grid=(t // tr,), in_specs=[row, vec, row],
        out_specs=[row, row, vec, pl.BlockSpec((1, LANES), lambda i: (0, 0))],
        out_shape=[jax.ShapeDtypeStruct((t, d), F32), jax.ShapeDtypeStruct((t, d), BF16),
                   jax.ShapeDtypeStruct((1, d), F32), jax.ShapeDtypeStruct((1, LANES), F32)],
        compiler_params=_params(1))(x, g, target)


def _merge_fwd(m_a, m_b, proj, gate_col, *, name):
    t, d = m_a.shape
    tr = _tile(t, ROW_TILE, SUBLANES)

    def kern(ma_ref, mb_ref, ga_ref, gb_ref, o_ref):
        o_ref[...] = (_sigmoid(ga_ref[...]) * ma_ref[...] + _sigmoid(gb_ref[...]) * mb_ref[...]).astype(BF16)

    row = pl.BlockSpec((tr, d), lambda i: (i, 0))
    return pl.pallas_call(
        kern, name=name, grid=(t // tr,),
        in_specs=[row, row, pl.BlockSpec((tr, d), lambda i: (i, gate_col)),
                  pl.BlockSpec((tr, d), lambda i: (i, gate_col + 1))],
        out_specs=row, out_shape=jax.ShapeDtypeStruct((t, d), BF16), compiler_params=_params(1))(m_a, m_b, proj, proj)


def _merge_bwd(d_merged, m_a, m_b, proj, gate_col, *, name):
    t, d = m_a.shape
    tr = _tile(t, ROW_TILE, SUBLANES)

    def kern(dm_ref, ma_ref, mb_ref, ga_ref, gb_ref, dma_ref, dmb_ref, dg_ref):
        dm = dm_ref[...]
        sa = _sigmoid(ga_ref[...])
        sb = _sigmoid(gb_ref[...])
        dma_ref[...] = (dm * sa).astype(BF16)
        dmb_ref[...] = (dm * sb).astype(BF16)
        dg_ref[:, 0:d] = (dm * ma_ref[...] * (sa * (1.0 - sa))).astype(BF16)
        dg_ref[:, d:2 * d] = (dm * mb_ref[...] * (sb * (1.0 - sb))).astype(BF16)

    row = pl.BlockSpec((tr, d), lambda i: (i, 0))
    return pl.pallas_call(
        kern, name=name, grid=(t // tr,),
        in_specs=[row, row, row, pl.BlockSpec((tr, d), lambda i: (i, gate_col)),
                  pl.BlockSpec((tr, d), lambda i: (i, gate_col + 1))],
        out_specs=[row, row, pl.BlockSpec((tr, 2 * d), lambda i: (i, 0))],
        out_shape=[jax.ShapeDtypeStruct((t, d), BF16), jax.ShapeDtypeStruct((t, d), BF16),
                   jax.ShapeDtypeStruct((t, 2 * d), BF16)], compiler_params=_params(1))(d_merged, m_a, m_b, proj, proj)


def _tril_bf16(w, transposed):
    row = lax.broadcasted_iota(jnp.int32, w.shape, 0)
    col = lax.broadcasted_iota(jnp.int32, w.shape, 1)
    keep = (row <= col) if transposed else (row >= col)
    return jnp.where(keep, w, 0.0).astype(BF16)


def _layernorm_stats(v):
    mu = jnp.mean(v, axis=-1, keepdims=True)
    vc = v - mu
    rstd = lax.rsqrt(jnp.mean(vc * vc, axis=-1, keepdims=True) + EPS)
    return vc * rstd, rstd


def _mixer_a_fwd(proj, ln_g, ln_b, sg_w, sg_b_t, sgw, *, name):
    t = proj.shape[0]
    gd = sgw // SG_GROUPS

    def kern(zu_ref, zv_ref, g_ref, b_ref, w_ref, bt_ref, o_ref):
        xhat, _ = _layernorm_stats(_gelu(zv_ref[...]))
        vn = (xhat * g_ref[...] + b_ref[...]).astype(BF16)
        for g in range(SG_GROUPS):
            cols = slice(g * gd, (g + 1) * gd)
            mixed = jnp.dot(_tril_bf16(w_ref[g], False), vn[:, cols], preferred_element_type=F32) + bt_ref[:, g:g + 1]
            o_ref[:, cols] = (_gelu(zu_ref[:, cols]) * mixed).astype(BF16)

    vec = pl.BlockSpec((1, sgw), lambda i: (0, 0))
    return pl.pallas_call(
        kern, name=name, grid=(t // CHUNK,),
        in_specs=[pl.BlockSpec((CHUNK, sgw), lambda i: (i, 0)), pl.BlockSpec((CHUNK, sgw), lambda i: (i, 1)), vec, vec,
                  pl.BlockSpec((SG_GROUPS, CHUNK, CHUNK), lambda i: (0, 0, 0)),
                  pl.BlockSpec((CHUNK, SG_GROUPS), lambda i: (0, 0))],
        out_specs=pl.BlockSpec((CHUNK, sgw), lambda i: (i, 0)),
        out_shape=jax.ShapeDtypeStruct((t, sgw), BF16), compiler_params=_params(1))(proj, proj, ln_g, ln_b, sg_w, sg_b_t)


def _mixer_a_bwd(proj, d_ya, ln_g, ln_b, sg_w, sg_w_t, sg_b_t, sgw, *, name):
    t = proj.shape[0]
    gd = sgw // SG_GROUPS

    def kern(zu_ref, zv_ref, dy_ref, g_ref, b_ref, w_ref, wt_ref, bt_ref, dz_ref, dw_ref, dbt_ref, dg_ref, db_ref, dvn):
        @pl.when(pl.program_id(0) == 0)
        def _():
            dw_ref[...] = jnp.zeros_like(dw_ref)
            dbt_ref[...] = jnp.zeros_like(dbt_ref)
            dg_ref[...] = jnp.zeros_like(dg_ref)
            db_ref[...] = jnp.zeros_like(db_ref)

        gv, dgv = _gelu_and_grad(zv_ref[...])
        xhat, rstd = _layernorm_stats(gv)
        ln_gain = g_ref[...]
        vn = (xhat * ln_gain + b_ref[...]).astype(BF16)
        for g in range(SG_GROUPS):
            cols = slice(g * gd, (g + 1) * gd)
            gu, dgu = _gelu_and_grad(zu_ref[:, cols])
            mixed = jnp.dot(_tril_bf16(w_ref[g], False), vn[:, cols], preferred_element_type=F32) + bt_ref[:, g:g + 1]
            dy = dy_ref[:, cols]
            dz_ref[:, cols] = (dy * mixed * dgu).astype(BF16)
            d_mixed = dy * gu
            d_mixed_b = d_mixed.astype(BF16)
            dvn[:, cols] = jnp.dot(_tril_bf16(wt_ref[g], True), d_mixed_b, preferred_element_type=F32)
            d_w = lax.dot_general(d_mixed_b, vn[:, cols], (((1,), (1,)), ((), ())), preferred_element_type=F32)
            row = lax.broadcasted_iota(jnp.int32, d_w.shape, 0)
            col = lax.broadcasted_iota(jnp.int32, d_w.shape, 1)
            dw_ref[g] += jnp.where(row >= col, d_w, 0.0)
            dbt_ref[:, g:g + 1] += jnp.sum(d_mixed, axis=-1, keepdims=True)
        d_vn = dvn[...]
        dg_ref[...] += jnp.sum(d_vn * xhat, axis=0, keepdims=True)
        db_ref[...] += jnp.sum(d_vn, axis=0, keepdims=True)
        d_xhat = d_vn * ln_gain
        d_gv = rstd * (d_xhat - jnp.mean(d_xhat, axis=-1, keepdims=True)
                       - xhat * jnp.mean(d_xhat * xhat, axis=-1, keepdims=True))
        dz_ref[:, sgw:2 * sgw] = (d_gv * dgv).astype(BF16)

    vec = pl.BlockSpec((1, sgw), lambda i: (0, 0))
    wspec = pl.BlockSpec((SG_GROUPS, CHUNK, CHUNK), lambda i: (0, 0, 0))
    btspec = pl.BlockSpec((CHUNK, SG_GROUPS), lambda i: (0, 0))
    return pl.pallas_call(
        kern, name=name, grid=(t // CHUNK,),
        in_specs=[pl.BlockSpec((CHUNK, sgw), lambda i: (i, 0)), pl.BlockSpec((CHUNK, sgw), lambda i: (i, 1)),
                  pl.BlockSpec((CHUNK, sgw), lambda i: (i, 0)), vec, vec, wspec, wspec, btspec],
        out_specs=[pl.BlockSpec((CHUNK, 2 * sgw), lambda i: (i, 0)), wspec, btspec, vec, vec],
        out_shape=[jax.ShapeDtypeStruct((t, 2 * sgw), BF16), jax.ShapeDtypeStruct((SG_GROUPS, CHUNK, CHUNK), F32),
                   jax.ShapeDtypeStruct((CHUNK, SG_GROUPS), F32), jax.ShapeDtypeStruct((1, sgw), F32),
                   jax.ShapeDtypeStruct((1, sgw), F32)],
        scratch_shapes=[pltpu.VMEM((CHUNK, sgw), F32)],
        compiler_params=_params(1))(proj, proj, d_ya, ln_g, ln_b, sg_w, sg_w_t, sg_b_t)


def _scan_rows(a_ref, h_ref, reverse):
    s, c = a_ref.shape
    nblk = s // SUBLANES
    a, b = a_ref[...], h_ref[...]
    row = jnp.bitwise_and(lax.broadcasted_iota(jnp.int32, (s, c), 0), SUBLANES - 1)
    for d in (1, 2, 4):
        inside = (row < SUBLANES - d) if reverse else (row >= d)
        shift = s - d if reverse else d
        b = a * jnp.where(inside, pltpu.roll(b, shift, 0), 0.0) + b
        a = a * jnp.where(inside, pltpu.roll(a, shift, 0), 1.0)
    a_ref[...] = a
    h_ref[...] = b
    leaving = 0 if reverse else SUBLANES - 1

    def chain(i, carry):
        r0 = pl.multiple_of((nblk - 1 - i if reverse else i) * SUBLANES, SUBLANES)
        h = a_ref[pl.ds(r0, SUBLANES), :] * carry + h_ref[pl.ds(r0, SUBLANES), :]
        h_ref[pl.ds(r0, SUBLANES), :] = h
        return jnp.broadcast_to(h[leaving:leaving + 1, :], (SUBLANES, c))

    lax.fori_loop(0, nblk, chain, jnp.zeros((SUBLANES, c), F32))


def _lru_gates(xc, wa_ref, ba_ref, wx_ref, bx_ref, lam_ref):
    xcb = xc.astype(BF16)
    ra = _sigmoid(jnp.dot(xcb, wa_ref[...].astype(BF16), preferred_element_type=F32) + ba_ref[...])
    ia = _sigmoid(jnp.dot(xcb, wx_ref[...].astype(BF16), preferred_element_type=F32) + bx_ref[...])
    neg = -lam_ref[...]
    sp = jnp.maximum(neg, 0.0) + jnp.log1p(jnp.exp(-jnp.abs(neg)))
    log_a = -LRU_C * ra * sp
    a = jnp.exp(log_a)
    a2 = jnp.exp(2.0 * log_a)
    sq = jnp.sqrt(-jnp.tanh(log_a) * (a2 + 1.0))
    return ra, ia, sp, a, a2, sq


def _mixer_b_specs(seq, hd, sgw, lw):
    x_col = (2 * sgw) // hd
    y_col = (2 * sgw + lw) // hd
    tile = lambda col: pl.BlockSpec((seq, hd), lambda h, b: (b, col + h))
    vec = pl.BlockSpec((1, hd), lambda h, b: (0, h))
    mat = pl.BlockSpec((None, hd, hd), lambda h, b: (h, 0, 0))
    return tile(x_col), tile(y_col), tile(0), vec, mat


def _mixer_b_fwd(proj, conv_w, conv_b, wa, ba, wx, bx, lam, *, seq, sgw, lw, name):
    t = proj.shape[0]
    hd = lw // LRU_HEADS
    k_taps = conv_w.shape[0]
    x_spec, y_spec, o_spec, vec, mat = _mixer_b_specs(seq, hd, sgw, lw)

    def kern(xr_ref, yr_ref, cw_ref, cb_ref, wa_ref, ba_ref, wx_ref, bx_ref, lam_ref, o_ref, s_a, s_h):
        xc = _causal_conv(xr_ref[...], cw_ref[...], cb_ref[...])
        _, ia, _, a, _, sq = _lru_gates(xc, wa_ref, ba_ref, wx_ref, bx_ref, lam_ref)
        s_a[...] = a
        s_h[...] = sq * (ia * xc)
        _scan_rows(s_a, s_h, False)
        o_ref[...] = (s_h[...] * _gelu(yr_ref[...])).astype(BF16)

    return pl.pallas_call(
        kern, name=name, grid=(LRU_HEADS, t // seq),
        in_specs=[x_spec, y_spec, pl.BlockSpec((k_taps, hd), lambda h, b: (0, h)), vec, mat, vec, mat, vec, vec],
        out_specs=o_spec, out_shape=jax.ShapeDtypeStruct((t, lw), BF16),
        scratch_shapes=[pltpu.VMEM((seq, hd), F32), pltpu.VMEM((seq, hd), F32)],
        compiler_params=_params(2))(proj, proj, conv_w, conv_b, wa, ba, wx, bx, lam)


def _mixer_b_bwd(proj, d_yb, conv_w, conv_b, wa, wa_t, ba, wx, wx_t, bx, lam, *, seq, sgw, lw, name, exchanges=()):
    t = proj.shape[0]
    hd = lw // LRU_HEADS
    k_taps = conv_w.shape[0]
    x_spec, y_spec, o_spec, vec, mat = _mixer_b_specs(seq, hd, sgw, lw)
    cw_spec = pl.BlockSpec((k_taps, hd), lambda h, b: (0, h))

    def kern(xr_ref, yr_ref, dyb_ref, cw_ref, cb_ref, wa_ref, wat_ref, ba_ref, wx_ref, wxt_ref, bx_ref, lam_ref,
             dxr_ref, dyr_ref, dcw_ref, dcb_ref, dwa_ref, dba_ref, dwx_ref, dbx_ref, dlam_ref,
             s_xc, s_a, s_h, s_lam, s_dpa, s_dpx):
        @pl.when(pl.program_id(1) == 0)
        def _():
            for ref in (dcw_ref, dcb_ref, dwa_ref, dba_ref, dwx_ref, dbx_ref, dlam_ref):
                ref[...] = jnp.zeros_like(ref)

        s_xc[...] = _causal_conv(xr_ref[...], cw_ref[...], cb_ref[...])
        _, ia, _, a, _, sq = _lru_gates(s_xc[...], wa_ref, ba_ref, wx_ref, bx_ref, lam_ref)
        s_a[...] = a
        s_dpa[...] = _shift_up(a, 1)
        s_h[...] = sq * (ia * s_xc[...])
        _scan_rows(s_a, s_h, False)

        gel, dgel = _gelu_and_grad(yr_ref[...])
        dyb = dyb_ref[...]
        dyr_ref[...] = (dyb * s_h[...] * dgel).astype(BF16)
        s_lam[...] = dyb * gel
        _scan_rows(s_dpa, s_lam, True)
        ra, ia, sp, a, a2, sq = _lru_gates(s_xc[...], wa_ref, ba_ref, wx_ref, bx_ref, lam_ref)
        d_gx = s_lam[...]
        d_a = d_gx * _shift_down(s_h[...], 1)
        xc = s_xc[...]
        d_sq = d_gx * (ia * xc)
        d_ia = d_gx * (sq * xc)
        d_log_a = d_a * a - d_sq * (a2 / sq)
        d_ra = d_log_a * (-LRU_C * sp)
        d_sp = jnp.sum(d_log_a * (-LRU_C * ra), axis=0, keepdims=True)
        dlam_ref[...] += d_sp * (-_sigmoid(-lam_ref[...]))
        d_pa = d_ra * (ra * (1.0 - ra))
        d_px = d_ia * (ia * (1.0 - ia))
        s_dpa[...] = d_pa
        s_dpx[...] = d_px
        dba_ref[...] += jnp.sum(d_pa, axis=0, keepdims=True)
        dbx_ref[...] += jnp.sum(d_px, axis=0, keepdims=True)
        xcb = s_xc[...].astype(BF16)
        d_pa_b = s_dpa[...].astype(BF16)
        d_px_b = s_dpx[...].astype(BF16)
        contract_rows = (((0,), (0,)), ((), ()))
        dwa_ref[...] += lax.dot_general(xcb, d_pa_b, contract_rows, preferred_element_type=F32)
        dwx_ref[...] += lax.dot_general(xcb, d_px_b, contract_rows, preferred_element_type=F32)
        d_xc = (s_lam[...] * (sq * ia)
                + jnp.dot(d_pa_b, wat_ref[...].astype(BF16), preferred_element_type=F32)
                + jnp.dot(d_px_b, wxt_ref[...].astype(BF16), preferred_element_type=F32))
        dcb_ref[...] += jnp.sum(d_xc, axis=0, keepdims=True)
        dcw_ref[...] += _causal_conv_bwd_w(d_xc, xr_ref[...], k_taps)
        dxr_ref[...] = _causal_conv_bwd_x(d_xc, cw_ref[...]).astype(BF16)

    tile_shape = jax.ShapeDtypeStruct((t, lw), BF16)
    vec_shape = jax.ShapeDtypeStruct((1, lw), F32)
    mat_shape = jax.ShapeDtypeStruct((LRU_HEADS, hd, hd), F32)
    return _pallas(
        kern, name=name, grid=(LRU_HEADS, t // seq),
        in_specs=[x_spec, y_spec, o_spec, cw_spec, vec, mat, mat, vec, mat, mat, vec, vec],
        out_specs=[o_spec, o_spec, cw_spec, vec, mat, vec, mat, vec, vec],
        out_shape=[tile_shape, tile_shape, jax.ShapeDtypeStruct((k_taps, lw), F32), vec_shape, mat_shape, vec_shape,
                   mat_shape, vec_shape, vec_shape],
        operands=[proj, proj, d_yb, conv_w, conv_b, wa, wa_t, ba, wx, wx_t, bx, lam],
        scratch_shapes=[pltpu.VMEM((seq, hd), F32)] * 6, exchanges=exchanges)


FFN_TILE = 256


def _ffn_mid_fwd(up_pre, conv_w, conv_b, *, seq, name):
    t, f2 = up_pre.shape
    f = f2 // 2
    tc = _tile(f, FFN_TILE, LANES)
    nf = f // tc
    k_taps = conv_w.shape[0]

    def kern(pg_ref, pv_ref, wg_ref, wv_ref, bg_ref, bv_ref, o_ref):
        cg = _causal_conv(pg_ref[...], wg_ref[...], bg_ref[...])
        cv = _causal_conv(pv_ref[...], wv_ref[...], bv_ref[...])
        o_ref[...] = (_gelu(cg) * cv).astype(BF16)

    tile = lambda off: pl.BlockSpec((seq, tc), lambda j, b: (b, off + j))
    wspec = lambda off: pl.BlockSpec((k_taps, tc), lambda j, b: (0, off + j))
    bspec = lambda off: pl.BlockSpec((1, tc), lambda j, b: (0, off + j))
    return pl.pallas_call(
        kern, name=name, grid=(nf, t // seq),
        in_specs=[tile(0), tile(nf), wspec(0), wspec(nf), bspec(0), bspec(nf)], out_specs=tile(0),
        out_shape=jax.ShapeDtypeStruct((t, f), BF16),
        compiler_params=_params(2))(up_pre, up_pre, conv_w, conv_w, conv_b, conv_b)


def _ffn_mid_bwd(up_pre, d_act, conv_w, conv_b, *, seq, name):
    t, f2 = up_pre.shape
    f = f2 // 2
    tc = _tile(f, FFN_TILE, LANES)
    nf = f // tc
    k_taps = conv_w.shape[0]

    def kern(pg_ref, pv_ref, da_ref, wg_ref, wv_ref, bg_ref, bv_ref, dpg_ref, dpv_ref, dwg_ref, dwv_ref, dbg_ref, dbv_ref):
        @pl.when(pl.program_id(1) == 0)
        def _():
            for ref in (dwg_ref, dwv_ref, dbg_ref, dbv_ref):
                ref[...] = jnp.zeros_like(ref)

        pg = pg_ref[...]
        pv = pv_ref[...]
        gel, dgel = _gelu_and_grad(_causal_conv(pg, wg_ref[...], bg_ref[...]))
        cv = _causal_conv(pv, wv_ref[...], bv_ref[...])
        d_act_v = da_ref[...]
        d_cg = d_act_v * cv * dgel
        d_cv = d_act_v * gel
        dpg_ref[...] = _causal_conv_bwd_x(d_cg, wg_ref[...]).astype(BF16)
        dpv_ref[...] = _causal_conv_bwd_x(d_cv, wv_ref[...]).astype(BF16)
        dwg_ref[...] += _causal_conv_bwd_w(d_cg, pg, k_taps)
        dwv_ref[...] += _causal_conv_bwd_w(d_cv, pv, k_taps)
        dbg_ref[...] += jnp.sum(d_cg, axis=0, keepdims=True)
        dbv_ref[...] += jnp.sum(d_cv, axis=0, keepdims=True)

    tile = lambda off: pl.BlockSpec((seq, tc), lambda j, b: (b, off + j))
    wspec = lambda off: pl.BlockSpec((k_taps, tc), lambda j, b: (0, off + j))
    bspec = lambda off: pl.BlockSpec((1, tc), lambda j, b: (0, off + j))
    half = jax.ShapeDtypeStruct((t, f), BF16)
    wshape = jax.ShapeDtypeStruct((k_taps, f), F32)
    bshape = jax.ShapeDtypeStruct((1, f), F32)
    d_pg, d_pv, d_wg, d_wv, d_bg, d_bv = pl.pallas_call(
        kern, name=name, grid=(nf, t // seq),
        in_specs=[tile(0), tile(nf), tile(0), wspec(0), wspec(nf), bspec(0), bspec(nf)],
        out_specs=[tile(0), tile(0), wspec(0), wspec(0), bspec(0), bspec(0)],
        out_shape=[half, half, wshape, wshape, bshape, bshape],
        compiler_params=_params(2))(up_pre, up_pre, d_act, conv_w, conv_w, conv_b, conv_b)
    return (jnp.concatenate([d_pg, d_pv], axis=1), jnp.concatenate([d_wg, d_wv], axis=1),
            jnp.concatenate([d_bg, d_bv], axis=1))


ELEM_VMEM_BYTES = 24 << 20


def _as_2d(a):
    if a.ndim >= 2 and a.shape[-1] % LANES == 0 and a.size // a.shape[-1] >= SUBLANES:
        return a.reshape(-1, a.shape[-1])
    return a.reshape(-1, LANES)


def _row_tile(rows, bytes_per_row):
    return _tile(rows, max(16, ELEM_VMEM_BYTES // (2 * bytes_per_row)), 16)


def _cast_bf16(a, *, name):
    v = _as_2d(a)
    rows, cols = v.shape
    tr = _row_tile(rows, cols * (4 + 2))

    def kern(x_ref, o_ref):
        o_ref[...] = x_ref[...].astype(BF16)

    spec = pl.BlockSpec((tr, cols), lambda i: (i, 0))
    out = pl.pallas_call(kern, name=name, grid=(rows // tr,), in_specs=[spec], out_specs=spec,
                         out_shape=jax.ShapeDtypeStruct(v.shape, BF16), compiler_params=_params(1))(v)
    return out.reshape(a.shape)


def _add_sibling_part(own, core, got, *, name):
    _, _, rows, cols = own.shape
    tr = _row_tile(rows, cols * (2 + 2 + 2))

    def kern(core_ref, a_ref, b_ref, o_ref):
        o_ref[...] = (a_ref[...].astype(F32) + b_ref[...].astype(F32)).astype(BF16)

    spec = pl.BlockSpec((None, tr, cols), lambda ch, i, core_ref: (ch, i, 0))
    grid_spec = pltpu.PrefetchScalarGridSpec(
        num_scalar_prefetch=1, grid=(4, rows // tr),
        in_specs=[pl.BlockSpec((None, None, tr, cols), lambda ch, i, core_ref: (ch, core_ref[0], i, 0)), spec],
        out_specs=spec)
    return pl.pallas_call(kern, name=name, grid_spec=grid_spec, out_shape=jax.ShapeDtypeStruct(got.shape, BF16),
                          compiler_params=_params(2))(core, own, got)


def _sum_parts(parts, *, name):
    n_parts, rows, cols = parts.shape
    tr = _row_tile(rows, cols * 4 * (n_parts + 1))

    def kern(p_ref, o_ref):
        acc = p_ref[0].astype(F32)
        for p in range(1, n_parts):
            acc = acc + p_ref[p].astype(F32)
        o_ref[...] = acc

    return pl.pallas_call(
        kern, name=name, grid=(rows // tr,), in_specs=[pl.BlockSpec((n_parts, tr, cols), lambda i: (0, i, 0))],
        out_specs=pl.BlockSpec((tr, cols), lambda i: (i, 0)), out_shape=jax.ShapeDtypeStruct((rows, cols), F32),
        compiler_params=_params(1))(parts)


def _adamw(w, m, v, grad_parts, *, name):
    shape = w.shape
    w2 = _as_2d(w)
    rows, cols = w2.shape
    n_parts = grad_parts.shape[0]
    parts = grad_parts.reshape(n_parts, rows, cols)
    tr = _row_tile(rows, cols * (3 * 4 + n_parts * parts.dtype.itemsize + 4 * 4))
    c_m = 1.0 - ADAM_B1 ** ADAM_STEP
    c_v = 1.0 - ADAM_B2 ** ADAM_STEP

    def kern(w_ref, m_ref, v_ref, p_ref, g_ref, d_ref, nm_ref, nv_ref):
        g = p_ref[0].astype(F32)
        for p in range(1, n_parts):
            g = g + p_ref[p].astype(F32)
        new_m = ADAM_B1 * m_ref[...] + (1.0 - ADAM_B1) * g
        new_v = ADAM_B2 * v_ref[...] + (1.0 - ADAM_B2) * (g * g)
        g_ref[...] = g
        nm_ref[...] = new_m
        nv_ref[...] = new_v
        d_ref[...] = -ADAM_LR * ((new_m / c_m) / (jnp.sqrt(new_v / c_v) + ADAM_EPS) + ADAM_WD * w_ref[...])

    spec = pl.BlockSpec((tr, cols), lambda i: (i, 0))
    out = jax.ShapeDtypeStruct((rows, cols), F32)
    res = pl.pallas_call(
        kern, name=name, grid=(rows // tr,),
        in_specs=[spec, spec, spec, pl.BlockSpec((n_parts, tr, cols), lambda i: (0, i, 0))],
        out_specs=[spec] * 4, out_shape=[out] * 4, compiler_params=_params(1))(w2, _as_2d(m), _as_2d(v), parts)
    return [r.reshape(shape) for r in res]


def _all_gather(shards, *, name):
    n = len(shards)

    def body(*refs):
        ins, outs = refs[:n], refs[n:2 * n]
        send_sems, recv_sems, local_sems = refs[2 * n:]
        x, y, c = _place()
        me, sibling = (x, y, c), (x, y, 1 - c)
        chips = [(1 - x, y), (x, 1 - y), (1 - x, 1 - y)]

        def slot(i, dev):
            return outs[i].at[4 * dev[0] + 2 * dev[1] + dev[2]]

        def copy(i, k, block, to, src=None):
            return pltpu.make_async_remote_copy(
                src_ref=slot(i, block) if src is None else src, dst_ref=slot(i, block),
                send_sem=send_sems.at[i, k], recv_sem=recv_sems.at[i, k], device_id=to, device_id_type=MESH)

        mine = [pltpu.make_async_copy(ins[i], slot(i, me), local_sems.at[i]) for i in range(n)]
        for cp in mine:
            cp.start()
        first = []
        for i in range(n):
            first.append(copy(i, 0, me, sibling, src=ins[i]))
            first += [copy(i, 1 + j, me, (*chip, c), src=ins[i]) for j, chip in enumerate(chips)]
        for cp in first:
            cp.start()
        passed = []
        for j, chip in enumerate(chips):
            for i in range(n):
                copy(i, 1 + j, (*chip, c), me).wait_recv()
                onward = copy(i, 4 + j, (*chip, c), sibling)
                onward.start()
                passed.append(onward)
        for i in range(n):
            copy(i, 0, sibling, me).wait_recv()
            for j, chip in enumerate(chips):
                copy(i, 4 + j, (*chip, 1 - c), me).wait_recv()
        for cp in first + passed:
            cp.wait_send()
        for cp in mine:
            cp.wait()

    return pl.pallas_call(
        body, name=name, in_specs=[ANY] * n, out_specs=[ANY] * n,
        out_shape=[jax.ShapeDtypeStruct((N_DEV,) + s.shape, s.dtype) for s in shards],
        scratch_shapes=[pltpu.SemaphoreType.DMA((n, 7)), pltpu.SemaphoreType.DMA((n, 7)), pltpu.SemaphoreType.DMA((n,))],
    )(*shards)


def _by_chip_and_core(grad):
    return grad.reshape(4, 2, -1, grad.shape[-1])


def _pack(vectors):
    flat = [v.reshape(-1).astype(F32) for v in vectors]
    sizes = [f.shape[0] for f in flat]
    total = sum(sizes)
    padded = -(-total // (SUBLANES * LANES)) * (SUBLANES * LANES)
    if padded > total:
        flat.append(jnp.zeros((padded - total,), F32))
    return jnp.concatenate(flat).reshape(-1, LANES), sizes


def _unpack(packed, sizes, shapes):
    flat = packed.reshape(-1)
    out, off = [], 0
    for size, shape in zip(sizes, shapes):
        out.append(flat[off:off + size].reshape(shape))
        off += size
    return out


def kernel(x, g_mix, w_in, sg_ln_g, sg_ln_b, sg_w, sg_b, lru_conv_w, lru_conv_b, lru_wa, lru_ba, lru_wx, lru_bx, lru_lam, p_sg, p_lru, w_out, g_ffn, w_up, ffn_conv_w, ffn_conv_b, w_down, g_final, loss_target, m_g_mix, m_w_in, m_sg_ln_g, m_sg_ln_b, m_sg_w, m_sg_b, m_lru_conv_w, m_lru_conv_b, m_lru_wa, m_lru_ba, m_lru_wx, m_lru_bx, m_lru_lam, m_p_sg, m_p_lru, m_w_out, m_g_ffn, m_w_up, m_ffn_conv_w, m_ffn_conv_b, m_w_down, m_g_final, v_g_mix, v_w_in, v_sg_ln_g, v_sg_ln_b, v_sg_w, v_sg_b, v_lru_conv_w, v_lru_conv_b, v_lru_wa, v_lru_ba, v_lru_wx, v_lru_bx, v_lru_lam, v_p_sg, v_p_lru, v_w_out, v_g_ffn, v_w_up, v_ffn_conv_w, v_ffn_conv_b, v_w_down, v_g_final):
    weights = dict(g_mix=g_mix, w_in=w_in, sg_ln_g=sg_ln_g, sg_ln_b=sg_ln_b, sg_w=sg_w, sg_b=sg_b, lru_conv_w=lru_conv_w,
                   lru_conv_b=lru_conv_b, lru_wa=lru_wa, lru_ba=lru_ba, lru_wx=lru_wx, lru_bx=lru_bx, lru_lam=lru_lam,
                   p_sg=p_sg, p_lru=p_lru, w_out=w_out, g_ffn=g_ffn, w_up=w_up, ffn_conv_w=ffn_conv_w,
                   ffn_conv_b=ffn_conv_b, w_down=w_down, g_final=g_final)
    m_in = dict(g_mix=m_g_mix, w_in=m_w_in, sg_ln_g=m_sg_ln_g, sg_ln_b=m_sg_ln_b, sg_w=m_sg_w, sg_b=m_sg_b,
                lru_conv_w=m_lru_conv_w, lru_conv_b=m_lru_conv_b, lru_wa=m_lru_wa, lru_ba=m_lru_ba, lru_wx=m_lru_wx,
                lru_bx=m_lru_bx, lru_lam=m_lru_lam, p_sg=m_p_sg, p_lru=m_p_lru, w_out=m_w_out, g_ffn=m_g_ffn,
                w_up=m_w_up, ffn_conv_w=m_ffn_conv_w, ffn_conv_b=m_ffn_conv_b, w_down=m_w_down, g_final=m_g_final)
    v_in = dict(g_mix=v_g_mix, w_in=v_w_in, sg_ln_g=v_sg_ln_g, sg_ln_b=v_sg_ln_b, sg_w=v_sg_w, sg_b=v_sg_b,
                lru_conv_w=v_lru_conv_w, lru_conv_b=v_lru_conv_b, lru_wa=v_lru_wa, lru_ba=v_lru_ba, lru_wx=v_lru_wx,
                lru_bx=v_lru_bx, lru_lam=v_lru_lam, p_sg=v_p_sg, p_lru=v_p_lru, w_out=v_w_out, g_ffn=v_g_ffn,
                w_up=v_w_up, ffn_conv_w=v_ffn_conv_w, ffn_conv_b=v_ffn_conv_b, w_down=v_w_down, g_final=v_g_final)
    order = list(weights)

    n_seq, seq, d = x.shape
    t = n_seq * seq
    sgw = sg_ln_g.shape[-1]
    lw = lru_lam.shape[-1]
    hd = lw // LRU_HEADS
    f2 = ffn_conv_b.shape[-1]
    gate_col = (2 * sgw + 2 * lw) // d
    xi, yi, ci = _place()
    dev = 4 * xi + 2 * yi + ci

    core = jnp.reshape(ci, (1,)).astype(jnp.int32)
    first_leg = functools.partial(_gather_first_leg, place_own=False)
    chip_swap = functools.partial(_swap_with_chips, place_own=False)
    shards, in_flight = {}, {}

    def landed(keys, after, name):
        return _await_exchange(first_leg, [in_flight[k] for k in keys], after, name=name)

    needed_first = ["w_in", "lru_wa", "lru_wx", "p_sg", "taps"]
    by_need = needed_first + ["p_lru", "w_out", "w_up", "w_down"]
    shards.update({k: _cast_bf16(weights[k][0], name=f"cast_{k}") for k in by_need if k != "taps"})
    shards["taps"], tap_sizes = _pack([lru_conv_w[0], ffn_conv_w[0]])
    started, g_mix_after_start = _start_exchange(
        first_leg, [shards[k] for k in by_need], [_own_block_in_place(shards[k], dev) for k in by_need], g_mix,
        name="start_weight_gather")
    in_flight.update(zip(by_need, started))

    def rows_in_order(g8):
        return g8.reshape(1, -1, g8.shape[-1])

    x2d = x.reshape(t, d)
    h1 = _rmsnorm_fwd(x2d, g_mix_after_start, name="norm_mix")
    ((w_in_g, wa_8, wx_8, p_sg_g, taps_8),) = _exchange_now(
        [_gather_second_leg(landed(needed_first, h1, "await_first_gather"))], name="second_leg_first_gather")
    wa_g, wx_g = (jnp.swapaxes(w8, 0, 1).reshape(LRU_HEADS, hd, hd) for w8 in (wa_8, wx_8))
    wa_t, wx_t = jnp.swapaxes(wa_g, 1, 2), jnp.swapaxes(wx_g, 1, 2)
    tap_parts = [_unpack(taps_8[k], tap_sizes, [lru_conv_w.shape[1:], ffn_conv_w.shape[1:]]) for k in range(N_DEV)]
    lru_cw = jnp.concatenate([p[0] for p in tap_parts], axis=1)
    ffn_cw = jnp.concatenate([p[1] for p in tap_parts], axis=1)
    sg_w0 = sg_w[0]
    sg_w_t = jnp.swapaxes(sg_w0, 1, 2)
    sg_b_t = sg_b[0].T
    proj, _ = _mm_nn(h1, w_in_g, out_dtype=F32, name="proj_in")
    y_a = _mixer_a_fwd(proj, sg_ln_g, sg_ln_b, sg_w0, sg_b_t, sgw, name="mixer_a_fwd")
    y_b = _mixer_b_fwd(proj, lru_cw, lru_conv_b, wa_g, lru_ba, wx_g, lru_bx, lru_lam, seq=seq, sgw=sgw, lw=lw,
                       name="mixer_b_fwd")
    m_a, ((p_lru_8, w_out_8),) = _mm_nn(
        y_a, p_sg_g, out_dtype=F32, name="proj_sg",
        exchanges=[_gather_second_leg(landed(["p_lru", "w_out"], y_b, "await_p_lru_w_out"))])
    p_lru_g, w_out_g = rows_in_order(p_lru_8), rows_in_order(w_out_8)
    m_b, _ = _mm_nn(y_b, p_lru_g, out_dtype=F32, name="proj_lru")
    merged = _merge_fwd(m_a, m_b, proj, gate_col, name="merge_fwd")
    x1, ((w_up_g,),) = _mm_nn(merged, w_out_g, out_dtype=F32, residual=x2d, name="proj_out",
                              exchanges=[_gather_second_leg(landed(["w_up"], merged, "await_w_up"))])
    h2 = _rmsnorm_fwd(x1, g_ffn, name="norm_ffn")
    up_pre, _ = _mm_nn(h2, w_up_g, out_dtype=F32, name="ffn_up")
    act = _ffn_mid_fwd(up_pre, ffn_cw, ffn_conv_b, seq=seq, name="ffn_mid_fwd")
    ((w_down_8,),) = _exchange_now([_gather_second_leg(landed(["w_down"], act, "await_w_down"))], name="second_leg_w_down")
    w_down_g = rows_in_order(w_down_8)
    x2, _ = _mm_nn(act, w_down_g, out_dtype=F32, residual=x1, name="ffn_down")
    d_x2, d_x2_b, d_g_final, loss_part = _loss_head(x2, g_final.reshape(1, d), loss_target.reshape(t, d), name="loss_head")
    loss = lax.psum(loss_part[0, 0], ("x", "y", "c"))

    def by_rows(g):
        return g.reshape(N_DEV, -1, g.shape[-1])

    def by_head_rows(g):
        return jnp.swapaxes(g.reshape(LRU_HEADS, N_DEV, hd // N_DEV, hd), 0, 1)

    def start_sibling_swap(views, key, before):
        return _start_exchange(_swap_with_sibling, views, [lax.empty((4,) + v.shape[2:], v.dtype) for v in views], before,
                               name=f"start_sibling_{key}")

    chip_swaps = {}

    def sum_and_start_chip_swap(keys, views, sibling_swap, done):
        from_sibling = _await_exchange(_swap_with_sibling, sibling_swap, done, name=f"await_sibling_{keys[0]}")
        sums = [_add_sibling_part(v, core, s, name=f"chip_sum_{k}") for k, v, s in zip(keys, views, from_sibling)]
        started, done = _start_exchange(chip_swap, sums, [_own_block_first(s, 2 * xi + yi) for s in sums], done,
                                        name=f"start_chips_{keys[0]}")
        chip_swaps.update(zip(keys, started))
        return done

    d_w_down, _ = _mm_tn(act, d_x2_b, 1, name="grad_w_down")
    v_down = _by_chip_and_core(by_rows(d_w_down))
    swap, d_x2_b = start_sibling_swap([v_down], "w_down", d_x2_b)
    d_act, _ = _mm_nt(d_x2_b, w_down_g, name="bwd_ffn_down")
    d_act = sum_and_start_chip_swap(["w_down"], [v_down], swap, d_act)
    d_up_pre, d_ffn_cw, d_ffn_cb = _ffn_mid_bwd(up_pre, d_act, ffn_cw, ffn_conv_b, seq=seq, name="ffn_mid_bwd")
    d_w_up, _ = _mm_tn(h2, d_up_pre, N_DEV, name="grad_w_up")
    v_up = _by_chip_and_core(d_w_up)
    swap, d_up_pre = start_sibling_swap([v_up], "w_up", d_up_pre)
    d_h2, _ = _mm_nt(d_up_pre, w_up_g, name="bwd_ffn_up")
    d_h2 = sum_and_start_chip_swap(["w_up"], [v_up], swap, d_h2)
    d_x1, d_x1_b, d_g_ffn = _rmsnorm_bwd(x1, g_ffn, d_h2, d_x2, name="norm_ffn_bwd")
    d_w_out, _ = _mm_tn(merged, d_x1_b, 1, name="grad_w_out")
    v_out = _by_chip_and_core(by_rows(d_w_out))
    swap, d_x1_b = start_sibling_swap([v_out], "w_out", d_x1_b)
    d_merged, _ = _mm_nt(d_x1_b, w_out_g, name="bwd_proj_out")
    d_merged = sum_and_start_chip_swap(["w_out"], [v_out], swap, d_merged)
    d_m_a, d_m_b, d_gates = _merge_bwd(d_merged, m_a, m_b, proj, gate_col, name="merge_bwd")
    d_p_sg, _ = _mm_tn(y_a, d_m_a, N_DEV, name="grad_p_sg")
    d_p_lru, _ = _mm_tn(y_b, d_m_b, 1, name="grad_p_lru")
    v_sg, v_lru = _by_chip_and_core(d_p_sg), _by_chip_and_core(by_rows(d_p_lru))
    swap, d_m_a = start_sibling_swap([v_sg, v_lru], "p_sg", d_m_a)
    d_y_a, _ = _mm_nt(d_m_a, p_sg_g, name="bwd_proj_sg")
    d_y_b, _ = _mm_nt(d_m_b, p_lru_g, name="bwd_proj_lru")
    d_y_b = sum_and_start_chip_swap(["p_sg", "p_lru"], [v_sg, v_lru], swap, d_y_b)
    d_zuv, d_sg_w, d_sg_b_t, d_ln_g, d_ln_b = _mixer_a_bwd(proj, d_y_a, sg_ln_g, sg_ln_b, sg_w0, sg_w_t, sg_b_t, sgw,
                                                           name="mixer_a_bwd")
    (d_xr, d_yr, d_lru_cw, d_lru_cb, d_wa, d_ba, d_wx, d_bx, d_lam), _ = _mixer_b_bwd(
        proj, d_y_b, lru_cw, lru_conv_b, wa_g, wa_t, lru_ba, wx_g, wx_t, lru_bx, lru_lam, seq=seq, sgw=sgw, lw=lw,
        name="mixer_b_bwd")
    d_proj = jnp.concatenate([d_zuv, d_xr, d_yr, d_gates], axis=1)
    d_w_in, _ = _mm_tn(h1, d_proj, N_DEV, name="grad_w_in")
    v_win = _by_chip_and_core(d_w_in)
    v_wa = _by_chip_and_core(_cast_bf16(by_head_rows(d_wa), name="cast_grad_wa"))
    v_wx = _by_chip_and_core(_cast_bf16(by_head_rows(d_wx), name="cast_grad_wx"))
    swap, d_proj = start_sibling_swap([v_win, v_wa, v_wx], "w_in", d_proj)
    d_proj = sum_and_start_chip_swap(["w_in", "lru_wa", "lru_wx"], [v_win, v_wa, v_wx], swap, d_proj)
    d_h1, _ = _mm_nt(d_proj, w_in_g, name="bwd_proj_in")
    grad_x, _, d_g_mix = _rmsnorm_bwd(x2d, g_mix, d_h1, d_x1, name="norm_mix_bwd")

    small = ["g_mix", "sg_ln_g", "sg_ln_b", "sg_w", "sg_b", "lru_conv_b", "lru_ba", "lru_bx", "lru_lam", "g_ffn",
             "ffn_conv_b", "g_final", "lru_conv_w", "ffn_conv_w"]
    small_parts = dict(g_mix=d_g_mix, sg_ln_g=d_ln_g, sg_ln_b=d_ln_b, sg_w=d_sg_w, sg_b=d_sg_b_t.T, lru_conv_b=d_lru_cb,
                       lru_ba=d_ba, lru_bx=d_bx, lru_lam=d_lam, g_ffn=d_g_ffn, ffn_conv_b=d_ffn_cb, g_final=d_g_final,
                       lru_conv_w=d_lru_cw, ffn_conv_w=d_ffn_cw)
    packed, sizes = _pack([small_parts[k] for k in small])
    (all_small,) = _all_gather([packed], name="gather_small_grads")
    small_sum = _sum_parts(all_small, name="sum_small_grads")
    small_grads = dict(zip(small, _unpack(small_sum, sizes, [small_parts[k].shape for k in small])))
    for k in ("lru_conv_w", "ffn_conv_w"):
        n_loc = weights[k].shape[-1]
        small_grads[k] = lax.dynamic_slice_in_dim(small_grads[k], dev * n_loc, n_loc, axis=1)

    grads, deltas, new_m, new_v = {}, {}, {}, {}

    def update(k, parts):
        grads[k], deltas[k], new_m[k], new_v[k] = _adamw(weights[k], m_in[k], v_in[k], parts, name=f"adamw_{k}")

    def update_when_landed(keys, after):
        parts = _await_exchange(chip_swap, [chip_swaps[k] for k in keys], after, name=f"await_chips_{keys[0]}")
        for k, p in zip(keys, parts):
            update(k, p)

    for keys in (["w_down"], ["w_up"], ["w_out"], ["p_sg", "p_lru"]):
        update_when_landed(keys, grad_x)
    for k in small:
        update(k, small_grads[k][None])
    update_when_landed(["w_in", "lru_wa", "lru_wx"], deltas["w_up"])

    return (loss, grad_x.reshape(x.shape), *[grads[k] for k in order], *[deltas[k] for k in order],
            *[new_m[k] for k in order], *[new_v[k] for k in order])
```

```python
import functools
import math
from typing import Callable, NamedTuple

import jax
import jax.numpy as jnp
from jax import lax
from jax.experimental import pallas as pl
from jax.experimental.pallas import tpu as pltpu

F32 = jnp.float32
BF16 = jnp.bfloat16
MESH = pl.DeviceIdType.MESH
ANY = pl.BlockSpec(memory_space=pl.ANY)

N_DEV = 8
EPS = 1e-6
CHUNK = 128
SG_GROUPS = 8
LRU_HEADS = 16
LRU_C = 8.0
ADAM_LR = 0.001
ADAM_B1 = 0.9
ADAM_B2 = 0.999
ADAM_EPS = 1e-08
ADAM_WD = 0.01
ADAM_STEP = 10

V7X_VMEM_LIMIT = 56 * 1024 * 1024
LANES = 128
SUBLANES = 8
MXU = 256

_GELU_C0 = math.sqrt(2.0 / math.pi)
_GELU_C1 = 0.044715


def _params(n_axes):
    return pltpu.CompilerParams(dimension_semantics=("arbitrary",) * n_axes, vmem_limit_bytes=V7X_VMEM_LIMIT)


def _tile(dim, pref, align):
    t = (min(pref, dim) // align) * align
    while t >= align:
        if dim % t == 0:
            return t
        t -= align
    return dim


def _gelu(x):
    return x * (0.5 * (1.0 + jnp.tanh(_GELU_C0 * (x + _GELU_C1 * (x * x * x)))))


def _gelu_and_grad(x):
    t = jnp.tanh(_GELU_C0 * (x + _GELU_C1 * (x * x * x)))
    cdf = 0.5 * (1.0 + t)
    dcdf = 0.5 * (1.0 - t * t) * (_GELU_C0 * (1.0 + 3.0 * _GELU_C1 * (x * x)))
    return x * cdf, cdf + x * dcdf


def _sigmoid(x):
    return 1.0 / (1.0 + jnp.exp(-x))


def _shift_down(x, d):
    if d == 0:
        return x
    row = lax.broadcasted_iota(jnp.int32, x.shape, 0)
    return jnp.where(row >= d, pltpu.roll(x, d, 0), 0.0)


def _shift_up(x, d):
    if d == 0:
        return x
    s = x.shape[0]
    row = lax.broadcasted_iota(jnp.int32, x.shape, 0)
    return jnp.where(row < s - d, pltpu.roll(x, s - d, 0), 0.0)


def _causal_conv(x, w, b):
    k_taps = w.shape[0]
    out = _shift_down(x, k_taps - 1) * w[0:1, :]
    for k in range(1, k_taps):
        out = out + _shift_down(x, k_taps - 1 - k) * w[k:k + 1, :]
    return out + b


def _causal_conv_bwd_x(d_out, w):
    k_taps = w.shape[0]
    d_x = _shift_up(d_out, k_taps - 1) * w[0:1, :]
    for k in range(1, k_taps):
        d_x = d_x + _shift_up(d_out, k_taps - 1 - k) * w[k:k + 1, :]
    return d_x


def _causal_conv_bwd_w(d_out, x, k_taps):
    rows = [jnp.sum(d_out * _shift_down(x, k_taps - 1 - k), axis=0, keepdims=True) for k in range(k_taps)]
    return jnp.concatenate(rows, axis=0)


def _place():
    return lax.axis_index("x"), lax.axis_index("y"), lax.axis_index("c")


def _other_chips(x, y):
    return [(1 - x, y), (x, 1 - y), (1 - x, 1 - y)]


class _Exchange(NamedTuple):
    ins: tuple
    outs: tuple
    in_place: bool
    n_remote: int
    n_local: int
    copies: Callable


def _remote(src, dst, send_sem, recv_sem, to):
    return pltpu.make_async_remote_copy(src_ref=src, dst_ref=dst, send_sem=send_sem, recv_sem=recv_sem, device_id=to,
                                        device_id_type=MESH)


def _gather_first_leg(shards, place_own=True):
    n = len(shards)

    def copies(ins, outs, send_sems, recv_sems, local_sems):
        x, y, c = _place()
        peers = [(x, y, 1 - c)] + [(*chip, c) for chip in _other_chips(x, y)]
        slot = lambda i, dev: outs[i].at[4 * dev[0] + 2 * dev[1] + dev[2]]
        sends = [_remote(ins[i], slot(i, (x, y, c)), send_sems[i].at[k], recv_sems[i].at[k], to)
                 for i in range(n) for k, to in enumerate(peers)]
        receives = [_remote(ins[i], slot(i, frm), send_sems[i].at[k], recv_sems[i].at[k], frm)
                    for i in range(n) for k, frm in enumerate(peers)]
        local = [pltpu.make_async_copy(ins[i], slot(i, (x, y, c)), local_sems[i].at[0]) for i in range(n)] if place_own else []
        return sends, receives, local

    outs = tuple(jax.ShapeDtypeStruct((N_DEV,) + s.shape, s.dtype) for s in shards)
    return _Exchange(tuple(shards), outs, False, 4, int(place_own), copies)


def _gather_second_leg(gathered):
    n = len(gathered)

    def copies(ins, outs, send_sems, recv_sems, local_sems):
        x, y, c = _place()
        slot = lambda i, chip, core: outs[i].at[4 * chip[0] + 2 * chip[1] + core]
        sends = [_remote(slot(i, chip, c), slot(i, chip, c), send_sems[i].at[j], recv_sems[i].at[j], (x, y, 1 - c))
                 for i in range(n) for j, chip in enumerate(_other_chips(x, y))]
        receives = [_remote(slot(i, chip, 1 - c), slot(i, chip, 1 - c), send_sems[i].at[j], recv_sems[i].at[j], (x, y, 1 - c))
                    for i in range(n) for j, chip in enumerate(_other_chips(x, y))]
        return sends, receives, []

    outs = tuple(jax.ShapeDtypeStruct(g.shape, g.dtype) for g in gathered)
    return _Exchange(tuple(gathered), outs, True, 3, 0, copies)


def _swap_with_sibling(parts):
    n = len(parts)

    def copies(ins, outs, send_sems, recv_sems, local_sems):
        x, y, c = _place()
        both = [_remote(ins[i].at[ch, 1 - c], outs[i].at[ch], send_sems[i].at[ch], recv_sems[i].at[ch], (x, y, 1 - c))
                for i in range(n) for ch in range(4)]
        return both, both, []

    outs = tuple(jax.ShapeDtypeStruct((4,) + p.shape[2:], p.dtype) for p in parts)
    return _Exchange(tuple(parts), outs, False, 4, 0, copies)


def _swap_with_chips(parts, place_own=True):
    n = len(parts)

    def copies(ins, outs, send_sems, recv_sems, local_sems):
        x, y, c = _place()
        both = [_remote(ins[i].at[2 * chip[0] + chip[1]], outs[i].at[1 + j], send_sems[i].at[j], recv_sems[i].at[j], (*chip, c))
                for i in range(n) for j, chip in enumerate(_other_chips(x, y))]
        local = [pltpu.make_async_copy(ins[i].at[2 * x + y], outs[i].at[0], local_sems[i].at[0]) for i in range(n)] if place_own else []
        return both, both, local

    outs = tuple(jax.ShapeDtypeStruct(p.shape, p.dtype) for p in parts)
    return _Exchange(tuple(parts), outs, False, 3, int(place_own), copies)


def _exchange_plumbing(exchanges):
    operands = [a for ex in exchanges for a in ex.ins]
    results = [s for ex in exchanges for s in ex.outs]
    scratch, in_place, at = [], {}, 0
    for ex in exchanges:
        n = len(ex.ins)
        scratch += [pltpu.SemaphoreType.DMA((n, ex.n_remote))] * 2
        if ex.n_local:
            scratch.append(pltpu.SemaphoreType.DMA((n, ex.n_local)))
        if ex.in_place:
            in_place.update({at + i: at + i for i in range(n)})
        at += n

    def copies(in_refs, out_refs, sem_refs):
        sends, receives, local = [], [], []
        at, sem_at = 0, 0
        for ex in exchanges:
            n, n_sem = len(ex.ins), 3 if ex.n_local else 2
            per_operand = [[sem.at[i] for i in range(n)] for sem in sem_refs[sem_at:sem_at + n_sem]] + [[]] * (3 - n_sem)
            s, r, l = ex.copies(in_refs[at:at + n], out_refs[at:at + n], *per_operand)
            sends, receives, local = sends + s, receives + r, local + l
            at, sem_at = at + n, sem_at + n_sem
        return sends, receives, local

    return operands, results, scratch, in_place, copies


def _start_all(copies):
    sends, _, local = copies
    for cp in local + sends:
        cp.start()


def _wait_all(copies):
    sends, receives, local = copies
    for cp in receives:
        cp.wait_recv()
    for cp in sends:
        cp.wait_send()
    for cp in local:
        cp.wait()


def _exchange_now(exchanges, *, name):
    operands, results, scratch, in_place, copies = _exchange_plumbing(exchanges)
    n = len(operands)

    def body(*refs):
        made = copies(refs[:n], refs[n:2 * n], refs[2 * n:])
        _start_all(made)
        _wait_all(made)

    out = pl.pallas_call(body, name=name, in_specs=[ANY] * n, out_specs=[ANY] * n, out_shape=results,
                         scratch_shapes=scratch, input_output_aliases=in_place)(*operands)
    return _split(out, exchanges)


def _split(flat, exchanges):
    out, at = [], 0
    for ex in exchanges:
        out.append(list(flat[at:at + len(ex.ins)]))
        at += len(ex.ins)
    return out


HBM = pl.BlockSpec(memory_space=pltpu.HBM)
SEMAPHORES = pl.BlockSpec(memory_space=pltpu.SEMAPHORE)
SPLIT_COPY = pltpu.CompilerParams(has_side_effects=pltpu.SideEffectType.DATAFLOW_SIDE_EFFECTING)


def _start_exchange(make, operands, landings, before, *, name):
    n = len(operands)
    ex = make(operands)

    def body(*refs):
        sends, _, _ = ex.copies(refs[:n], refs[n:2 * n], refs[2 * n + 1:3 * n + 1], refs[3 * n + 1:4 * n + 1], [])
        for cp in sends:
            cp.start()

    buffers = [pltpu.with_memory_space_constraint(a, pltpu.HBM) for a in list(operands) + list(landings) + [before]]
    out = pl.pallas_call(
        body, name=name, in_specs=[HBM] * (2 * n + 1), out_specs=[SEMAPHORES] * (2 * n) + [HBM] * (2 * n + 1),
        out_shape=[pltpu.SemaphoreType.DMA((ex.n_remote,))] * (2 * n) + [pltpu.HBM(a.shape, a.dtype) for a in buffers],
        input_output_aliases={i: 2 * n + i for i in range(2 * n + 1)}, compiler_params=SPLIT_COPY)(*buffers)
    return [(out[i], out[n + i], out[2 * n + i], out[3 * n + i]) for i in range(n)], out[4 * n]


def _await_exchange(make, in_flight, after, *, name):
    n = len(in_flight)
    send_sems, recv_sems, operands, landings = zip(*in_flight)
    ex = make(operands)

    def body(*refs):
        sends, receives, _ = ex.copies(refs[:n], refs[n:2 * n], refs[2 * n:3 * n], refs[3 * n:4 * n], [])
        for cp in receives:
            cp.wait_recv()
        for cp in sends:
            cp.wait_send()

    out = pl.pallas_call(
        body, name=name, in_specs=[HBM] * (2 * n) + [SEMAPHORES] * (2 * n) + [ANY], out_specs=[HBM] * (2 * n),
        out_shape=[pltpu.HBM(a.shape, a.dtype) for a in operands + landings],
        input_output_aliases={i: i for i in range(2 * n)}, compiler_params=SPLIT_COPY,
    )(*operands, *landings, *send_sems, *recv_sems, after)
    return list(out[:n]), list(out[n:])


def _own_block_first(blocks, index):
    own = lax.dynamic_index_in_dim(blocks, index, 0, keepdims=True)
    return lax.dynamic_update_index_in_dim(lax.empty(blocks.shape, blocks.dtype), own, 0, 0)


def _own_block_in_place(shard, index):
    return lax.dynamic_update_index_in_dim(lax.empty((N_DEV,) + shard.shape, shard.dtype), shard, index, 0)


def _pallas(kern, *, name, grid, in_specs, out_specs, out_shape, operands, scratch_shapes=(), exchanges=(), after=None):
    ex_operands, ex_results, ex_scratch, in_place, copies = _exchange_plumbing(exchanges)
    if after is not None:
        ex_operands = [after] + ex_operands
        in_place = {i + 1: o for i, o in in_place.items()}
    n_in, n_out, n_scratch, n_ex = len(in_specs), len(out_specs), len(scratch_shapes), len(ex_results)
    n_unread = len(ex_operands) - n_ex

    def body(*refs):
        ins, refs = refs[:n_in], refs[n_in + n_unread:]
        ex_ins, refs = refs[:n_ex], refs[n_ex:]
        outs, refs = refs[:n_out], refs[n_out:]
        ex_outs, refs = refs[:n_ex], refs[n_ex:]
        scratch, sems = refs[:n_scratch], refs[n_scratch:]
        if exchanges:
            first = functools.reduce(jnp.logical_and, [pl.program_id(a) == 0 for a in range(len(grid))])
            last = functools.reduce(jnp.logical_and, [pl.program_id(a) == g - 1 for a, g in enumerate(grid)])

            @pl.when(first)
            def _():
                _start_all(copies(ex_ins, ex_outs, sems))

        kern(*ins, *outs, *scratch)
        if exchanges:
            @pl.when(last)
            def _():
                _wait_all(copies(ex_ins, ex_outs, sems))

    res = pl.pallas_call(
        body, name=name, grid=grid, in_specs=list(in_specs) + [ANY] * len(ex_operands), out_specs=list(out_specs) + [ANY] * n_ex,
        out_shape=list(out_shape) + ex_results, scratch_shapes=list(scratch_shapes) + ex_scratch,
        input_output_aliases={n_in + i: n_out + o for i, o in in_place.items()},
        compiler_params=_params(len(grid)))(*operands, *ex_operands)
    return list(res[:n_out]), _split(res[n_out:], exchanges)


def _accumulate(step, n_steps, acc, value, finish):
    if n_steps == 1:
        finish(value)
        return

    @pl.when(step == 0)
    def _():
        acc[0][...] = value

    @pl.when(step > 0)
    def _():
        acc[0][...] += value

    @pl.when(step == n_steps - 1)
    def _():
        finish(acc[0][...])


def _mm_nn(a, w, *, out_dtype, name, residual=None, exchanges=()):
    m, k = a.shape
    nb, _, n_blk = w.shape
    tm, tn, tk = _tile(m, 512, MXU), _tile(n_blk, 1536, MXU), _tile(k, 4096, MXU)
    per = n_blk // tn
    nk = k // tk

    def kern(*refs):
        a_ref, w_ref = refs[:2]
        r_ref = None if residual is None else refs[2]
        o_ref, acc = refs[2 + (residual is not None)], refs[3 + (residual is not None):]

        def finish(total):
            o_ref[...] = (total if r_ref is None else total + r_ref[...]).astype(o_ref.dtype)

        _accumulate(pl.program_id(2), nk, acc, jnp.dot(a_ref[...], w_ref[...], preferred_element_type=F32), finish)

    tile = pl.BlockSpec((tm, tn), lambda j, i, kk: (i, j))
    in_specs = [pl.BlockSpec((tm, tk), lambda j, i, kk: (i, kk)),
                pl.BlockSpec((None, tk, tn), lambda j, i, kk: (j // per, kk, j % per))]
    operands = [a, w]
    if residual is not None:
        in_specs.append(tile)
        operands.append(residual)
    (out,), carried = _pallas(
        kern, name=name, grid=(nb * per, m // tm, nk), in_specs=in_specs, out_specs=[tile],
        out_shape=[jax.ShapeDtypeStruct((m, nb * n_blk), out_dtype)], operands=operands,
        scratch_shapes=[pltpu.VMEM((tm, tn), F32)] * (nk > 1), exchanges=exchanges)
    return out, carried


def _mm_nt(g, w, *, name, exchanges=(), after=None):
    m, n = g.shape
    nb, k, n_blk = w.shape
    tm, tko, tn = _tile(m, 1024, MXU), _tile(k, 1024, MXU), _tile(n_blk, 3072, MXU)
    per = n_blk // tn
    nn = n // tn

    def kern(g_ref, w_ref, o_ref, *acc):
        def finish(total):
            o_ref[...] = total

        part = lax.dot_general(g_ref[...], w_ref[...], (((1,), (1,)), ((), ())), preferred_element_type=F32)
        _accumulate(pl.program_id(2), nn, acc, part, finish)

    (out,), carried = _pallas(
        kern, name=name, grid=(k // tko, m // tm, nn),
        in_specs=[pl.BlockSpec((tm, tn), lambda j, i, jn: (i, jn)),
                  pl.BlockSpec((None, tko, tn), lambda j, i, jn: (jn // per, j, jn % per))],
        out_specs=[pl.BlockSpec((tm, tko), lambda j, i, jn: (i, j))],
        out_shape=[jax.ShapeDtypeStruct((m, k), F32)], operands=[g, w],
        scratch_shapes=[pltpu.VMEM((tm, tko), F32)] * (nn > 1), exchanges=exchanges, after=after)
    return out, carried


def _mm_tn(a, g, nb, *, name, exchanges=()):
    m, k = a.shape
    n = g.shape[1]
    n_blk = n // nb
    tko, tn, tm = _tile(k, 512, MXU), _tile(n_blk, 1536, MXU), _tile(m, 4096, MXU)
    per = n_blk // tn
    nm = m // tm

    def kern(a_ref, g_ref, o_ref, *acc):
        def finish(total):
            o_ref[...] = total.astype(o_ref.dtype)

        part = lax.dot_general(a_ref[...], g_ref[...], (((0,), (0,)), ((), ())), preferred_element_type=F32)
        _accumulate(pl.program_id(2), nm, acc, part, finish)

    (out,), carried = _pallas(
        kern, name=name, grid=(nb * per, k // tko, nm),
        in_specs=[pl.BlockSpec((tm, tko), lambda j, i, im: (im, i)),
                  pl.BlockSpec((tm, tn), lambda j, i, im: (im, j))],
        out_specs=[pl.BlockSpec((None, tko, tn), lambda j, i, im: (j // per, i, j % per))],
        out_shape=[jax.ShapeDtypeStruct((nb, k, n_blk), BF16)], operands=[a, g],
        scratch_shapes=[pltpu.VMEM((tko, tn), F32)] * (nm > 1), exchanges=exchanges)
    return out, carried


ROW_TILE = 128


def _rmsnorm_fwd(x, g, *, name):
    t, d = x.shape
    tr = _tile(t, ROW_TILE, SUBLANES)

    def kern(x_ref, g_ref, h_ref):
        xv = x_ref[...]
        r = lax.rsqrt(jnp.mean(xv * xv, axis=-1, keepdims=True) + EPS)
        h_ref[...] = (xv * r * g_ref[...]).astype(BF16)

    return pl.pallas_call(
        kern, name=name, grid=(t // tr,),
        in_specs=[pl.BlockSpec((tr, d), lambda i: (i, 0)), pl.BlockSpec((1, d), lambda i: (0, 0))],
        out_specs=pl.BlockSpec((tr, d), lambda i: (i, 0)),
        out_shape=jax.ShapeDtypeStruct((t, d), BF16), compiler_params=_params(1))(x, g)


def _rmsnorm_bwd(x, g, d_h, d_res, *, name):
    t, d = x.shape
    tr = _tile(t, ROW_TILE, SUBLANES)

    def kern(x_ref, g_ref, dh_ref, dres_ref, dx_ref, dxb_ref, dg_ref):
        xv = x_ref[...]
        r = lax.rsqrt(jnp.mean(xv * xv, axis=-1, keepdims=True) + EPS)
        dh = dh_ref[...]
        gy = dh * g_ref[...]
        dx = dres_ref[...] + r * gy - xv * (r * r * r) * jnp.mean(gy * xv, axis=-1, keepdims=True)
        dx_ref[...] = dx
        dxb_ref[...] = dx.astype(BF16)

        @pl.when(pl.program_id(0) == 0)
        def _():
            dg_ref[...] = jnp.zeros_like(dg_ref)

        dg_ref[...] += jnp.sum(dh * (xv * r), axis=0, keepdims=True)

    row = pl.BlockSpec((tr, d), lambda i: (i, 0))
    vec = pl.BlockSpec((1, d), lambda i: (0, 0))
    return pl.pallas_call(
        kern, name=name, grid=(t // tr,), in_specs=[row, vec, row, row], out_specs=[row, row, vec],
        out_shape=[jax.ShapeDtypeStruct((t, d), F32), jax.ShapeDtypeStruct((t, d), BF16),
                   jax.ShapeDtypeStruct((1, d), F32)], compiler_params=_params(1))(x, g, d_h, d_res)


def _loss_head(x, g, target, *, name):
    t, d = x.shape
    tr = _tile(t, ROW_TILE, SUBLANES)

    def kern(x_ref, g_ref, t_ref, dx_ref, dxb_ref, dg_ref, loss_ref):
        xv = x_ref[...]
        gv = g_ref[...]
        r = lax.rsqrt(jnp.mean(xv * xv, axis=-1, keepdims=True) + EPS)
        diff = xv * r * gv - t_ref[...]
        dy = diff * (1.0 / d)
        gy = dy * gv
        dx = r * gy - xv * (r * r * r) * jnp.mean(gy * xv, axis=-1, keepdims=True)
        dx_ref[...] = dx
        dxb_ref[...] = dx.astype(BF16)

        @pl.when(pl.program_id(0) == 0)
        def _():
            dg_ref[...] = jnp.zeros_like(dg_ref)
            loss_ref[...] = jnp.zeros_like(loss_ref)

        dg_ref[...] += jnp.sum(dy * (xv * r), axis=0, keepdims=True)
        part = 0.5 * jnp.sum(jnp.mean(diff * diff, axis=-1, keepdims=True), axis=0, keepdims=True)
        loss_ref[...] += jnp.broadcast_to(part, loss_ref.shape)

    row = pl.BlockSpec((tr, d), lambda i: (i, 0))
    vec = pl.BlockSpec((1, d), lambda i: (0, 0))
    return pl.pallas_call(
        kern, name=name, grid=(t // tr,), in_specs=[row, vec, row],
        out_specs=[row, row, vec, pl.BlockSpec((1, LANES), lambda i: (0, 0))],
        out_shape=[jax.ShapeDtypeStruct((t, d), F32), jax.ShapeDtypeStruct((t, d), BF16),
                   jax.ShapeDtypeStruct((1, d), F32), jax.ShapeDtypeStruct((1, LANES), F32)],
        compiler_params=_params(1))(x, g, target)


def _merge_fwd(m_a, m_b, proj, gate_col, *, name):
    t, d = m_a.shape
    tr = _tile(t, ROW_TILE, SUBLANES)

    def kern(ma_ref, mb_ref, ga_ref, gb_ref, o_ref):
        o_ref[...] = (_sigmoid(ga_ref[...]) * ma_ref[...] + _sigmoid(gb_ref[...]) * mb_ref[...]).astype(BF16)

    row = pl.BlockSpec((tr, d), lambda i: (i, 0))
    return pl.pallas_call(
        kern, name=name, grid=(t // tr,),
        in_specs=[row, row, pl.BlockSpec((tr, d), lambda i: (i, gate_col)),
                  pl.BlockSpec((tr, d), lambda i: (i, gate_col + 1))],
        out_specs=row, out_shape=jax.ShapeDtypeStruct((t, d), BF16), compiler_params=_params(1))(m_a, m_b, proj, proj)


def _merge_bwd(d_merged, m_a, m_b, proj, gate_col, *, name, after=None):
    t, d = m_a.shape
    tr = _tile(t, ROW_TILE, SUBLANES)

    def kern(dm_ref, ma_ref, mb_ref, ga_ref, gb_ref, dma_ref, dmb_ref, dg_ref):
        dm = dm_ref[...]
        sa = _sigmoid(ga_ref[...])
        sb = _sigmoid(gb_ref[...])
        dma_ref[...] = (dm * sa).astype(BF16)
        dmb_ref[...] = (dm * sb).astype(BF16)
        dg_ref[:, 0:d] = (dm * ma_ref[...] * (sa * (1.0 - sa))).astype(BF16)
        dg_ref[:, d:2 * d] = (dm * mb_ref[...] * (sb * (1.0 - sb))).astype(BF16)

    row = pl.BlockSpec((tr, d), lambda i: (i, 0))
    res, _ = _pallas(
        kern, name=name, grid=(t // tr,),
        in_specs=[row, row, row, pl.BlockSpec((tr, d), lambda i: (i, gate_col)),
                  pl.BlockSpec((tr, d), lambda i: (i, gate_col + 1))],
        out_specs=[row, row, pl.BlockSpec((tr, 2 * d), lambda i: (i, 0))],
        out_shape=[jax.ShapeDtypeStruct((t, d), BF16), jax.ShapeDtypeStruct((t, d), BF16),
                   jax.ShapeDtypeStruct((t, 2 * d), BF16)], operands=[d_merged, m_a, m_b, proj, proj], after=after)
    return res


def _tril_bf16(w, transposed):
    row = lax.broadcasted_iota(jnp.int32, w.shape, 0)
    col = lax.broadcasted_iota(jnp.int32, w.shape, 1)
    keep = (row <= col) if transposed else (row >= col)
    return jnp.where(keep, w, 0.0).astype(BF16)


def _layernorm_stats(v):
    mu = jnp.mean(v, axis=-1, keepdims=True)
    vc = v - mu
    rstd = lax.rsqrt(jnp.mean(vc * vc, axis=-1, keepdims=True) + EPS)
    return vc * rstd, rstd


def _mixer_a_fwd(proj, ln_g, ln_b, sg_w, sg_b_t, sgw, *, name):
    t = proj.shape[0]
    gd = sgw // SG_GROUPS

    def kern(zu_ref, zv_ref, g_ref, b_ref, w_ref, bt_ref, o_ref):
        xhat, _ = _layernorm_stats(_gelu(zv_ref[...]))
        vn = (xhat * g_ref[...] + b_ref[...]).astype(BF16)
        for g in range(SG_GROUPS):
            cols = slice(g * gd, (g + 1) * gd)
            mixed = jnp.dot(_tril_bf16(w_ref[g], False), vn[:, cols], preferred_element_type=F32) + bt_ref[:, g:g + 1]
            o_ref[:, cols] = (_gelu(zu_ref[:, cols]) * mixed).astype(BF16)

    vec = pl.BlockSpec((1, sgw), lambda i: (0, 0))
    return pl.pallas_call(
        kern, name=name, grid=(t // CHUNK,),
        in_specs=[pl.BlockSpec((CHUNK, sgw), lambda i: (i, 0)), pl.BlockSpec((CHUNK, sgw), lambda i: (i, 1)), vec, vec,
                  pl.BlockSpec((SG_GROUPS, CHUNK, CHUNK), lambda i: (0, 0, 0)),
                  pl.BlockSpec((CHUNK, SG_GROUPS), lambda i: (0, 0))],
        out_specs=pl.BlockSpec((CHUNK, sgw), lambda i: (i, 0)),
        out_shape=jax.ShapeDtypeStruct((t, sgw), BF16), compiler_params=_params(1))(proj, proj, ln_g, ln_b, sg_w, sg_b_t)


def _mixer_a_bwd(proj, d_ya, ln_g, ln_b, sg_w, sg_w_t, sg_b_t, sgw, *, name):
    t = proj.shape[0]
    gd = sgw // SG_GROUPS

    def kern(zu_ref, zv_ref, dy_ref, g_ref, b_ref, w_ref, wt_ref, bt_ref, dz_ref, dw_ref, dbt_ref, dg_ref, db_ref, dvn):
        @pl.when(pl.program_id(0) == 0)
        def _():
            dw_ref[...] = jnp.zeros_like(dw_ref)
            dbt_ref[...] = jnp.zeros_like(dbt_ref)
            dg_ref[...] = jnp.zeros_like(dg_ref)
            db_ref[...] = jnp.zeros_like(db_ref)

        gv, dgv = _gelu_and_grad(zv_ref[...])
        xhat, rstd = _layernorm_stats(gv)
        ln_gain = g_ref[...]
        vn = (xhat * ln_gain + b_ref[...]).astype(BF16)
        for g in range(SG_GROUPS):
            cols = slice(g * gd, (g + 1) * gd)
            gu, dgu = _gelu_and_grad(zu_ref[:, cols])
            mixed = jnp.dot(_tril_bf16(w_ref[g], False), vn[:, cols], preferred_element_type=F32) + bt_ref[:, g:g + 1]
            dy = dy_ref[:, cols]
            dz_ref[:, cols] = (dy * mixed * dgu).astype(BF16)
            d_mixed = dy * gu
            d_mixed_b = d_mixed.astype(BF16)
            dvn[:, cols] = jnp.dot(_tril_bf16(wt_ref[g], True), d_mixed_b, preferred_element_type=F32)
            d_w = lax.dot_general(d_mixed_b, vn[:, cols], (((1,), (1,)), ((), ())), preferred_element_type=F32)
            row = lax.broadcasted_iota(jnp.int32, d_w.shape, 0)
            col = lax.broadcasted_iota(jnp.int32, d_w.shape, 1)
            dw_ref[g] += jnp.where(row >= col, d_w, 0.0)
            dbt_ref[:, g:g + 1] += jnp.sum(d_mixed, axis=-1, keepdims=True)
        d_vn = dvn[...]
        dg_ref[...] += jnp.sum(d_vn * xhat, axis=0, keepdims=True)
        db_ref[...] += jnp.sum(d_vn, axis=0, keepdims=True)
        d_xhat = d_vn * ln_gain
        d_gv = rstd * (d_xhat - jnp.mean(d_xhat, axis=-1, keepdims=True)
                       - xhat * jnp.mean(d_xhat * xhat, axis=-1, keepdims=True))
        dz_ref[:, sgw:2 * sgw] = (d_gv * dgv).astype(BF16)

    vec = pl.BlockSpec((1, sgw), lambda i: (0, 0))
    wspec = pl.BlockSpec((SG_GROUPS, CHUNK, CHUNK), lambda i: (0, 0, 0))
    btspec = pl.BlockSpec((CHUNK, SG_GROUPS), lambda i: (0, 0))
    return pl.pallas_call(
        kern, name=name, grid=(t // CHUNK,),
        in_specs=[pl.BlockSpec((CHUNK, sgw), lambda i: (i, 0)), pl.BlockSpec((CHUNK, sgw), lambda i: (i, 1)),
                  pl.BlockSpec((CHUNK, sgw), lambda i: (i, 0)), vec, vec, wspec, wspec, btspec],
        out_specs=[pl.BlockSpec((CHUNK, 2 * sgw), lambda i: (i, 0)), wspec, btspec, vec, vec],
        out_shape=[jax.ShapeDtypeStruct((t, 2 * sgw), BF16), jax.ShapeDtypeStruct((SG_GROUPS, CHUNK, CHUNK), F32),
                   jax.ShapeDtypeStruct((CHUNK, SG_GROUPS), F32), jax.ShapeDtypeStruct((1, sgw), F32),
                   jax.ShapeDtypeStruct((1, sgw), F32)],
        scratch_shapes=[pltpu.VMEM((CHUNK, sgw), F32)],
        compiler_params=_params(1))(proj, proj, d_ya, ln_g, ln_b, sg_w, sg_w_t, sg_b_t)


def _scan_rows(a_ref, h_ref, reverse):
    s, c = a_ref.shape
    nblk = s // SUBLANES
    a, b = a_ref[...], h_ref[...]
    row = jnp.bitwise_and(lax.broadcasted_iota(jnp.int32, (s, c), 0), SUBLANES - 1)
    for d in (1, 2, 4):
        inside = (row < SUBLANES - d) if reverse else (row >= d)
        shift = s - d if reverse else d
        b = a * jnp.where(inside, pltpu.roll(b, shift, 0), 0.0) + b
        a = a * jnp.where(inside, pltpu.roll(a, shift, 0), 1.0)
    a_ref[...] = a
    h_ref[...] = b
    leaving = 0 if reverse else SUBLANES - 1

    def chain(i, carry):
        r0 = pl.multiple_of((nblk - 1 - i if reverse else i) * SUBLANES, SUBLANES)
        h = a_ref[pl.ds(r0, SUBLANES), :] * carry + h_ref[pl.ds(r0, SUBLANES), :]
        h_ref[pl.ds(r0, SUBLANES), :] = h
        return jnp.broadcast_to(h[leaving:leaving + 1, :], (SUBLANES, c))

    lax.fori_loop(0, nblk, chain, jnp.zeros((SUBLANES, c), F32))


def _lru_gates(xc, wa_ref, ba_ref, wx_ref, bx_ref, lam_ref):
    xcb = xc.astype(BF16)
    ra = _sigmoid(jnp.dot(xcb, wa_ref[...].astype(BF16), preferred_element_type=F32) + ba_ref[...])
    ia = _sigmoid(jnp.dot(xcb, wx_ref[...].astype(BF16), preferred_element_type=F32) + bx_ref[...])
    neg = -lam_ref[...]
    sp = jnp.maximum(neg, 0.0) + jnp.log1p(jnp.exp(-jnp.abs(neg)))
    log_a = -LRU_C * ra * sp
    a = jnp.exp(log_a)
    a2 = jnp.exp(2.0 * log_a)
    sq = jnp.sqrt(-jnp.tanh(log_a) * (a2 + 1.0))
    return ra, ia, sp, a, a2, sq


def _mixer_b_specs(seq, hd, sgw, lw):
    x_col = (2 * sgw) // hd
    y_col = (2 * sgw + lw) // hd
    tile = lambda col: pl.BlockSpec((seq, hd), lambda h, b: (b, col + h))
    vec = pl.BlockSpec((1, hd), lambda h, b: (0, h))
    mat = pl.BlockSpec((None, hd, hd), lambda h, b: (h, 0, 0))
    return tile(x_col), tile(y_col), tile(0), vec, mat


def _mixer_b_fwd(proj, conv_w, conv_b, wa, ba, wx, bx, lam, *, seq, sgw, lw, name):
    t = proj.shape[0]
    hd = lw // LRU_HEADS
    k_taps = conv_w.shape[0]
    x_spec, y_spec, o_spec, vec, mat = _mixer_b_specs(seq, hd, sgw, lw)

    def kern(xr_ref, yr_ref, cw_ref, cb_ref, wa_ref, ba_ref, wx_ref, bx_ref, lam_ref, o_ref, s_a, s_h):
        xc = _causal_conv(xr_ref[...], cw_ref[...], cb_ref[...])
        _, ia, _, a, _, sq = _lru_gates(xc, wa_ref, ba_ref, wx_ref, bx_ref, lam_ref)
        s_a[...] = a
        s_h[...] = sq * (ia * xc)
        _scan_rows(s_a, s_h, False)
        o_ref[...] = (s_h[...] * _gelu(yr_ref[...])).astype(BF16)

    return pl.pallas_call(
        kern, name=name, grid=(LRU_HEADS, t // seq),
        in_specs=[x_spec, y_spec, pl.BlockSpec((k_taps, hd), lambda h, b: (0, h)), vec, mat, vec, mat, vec, vec],
        out_specs=o_spec, out_shape=jax.ShapeDtypeStruct((t, lw), BF16),
        scratch_shapes=[pltpu.VMEM((seq, hd), F32), pltpu.VMEM((seq, hd), F32)],
        compiler_params=_params(2))(proj, proj, conv_w, conv_b, wa, ba, wx, bx, lam)


def _mixer_b_bwd(proj, d_yb, conv_w, conv_b, wa, wa_t, ba, wx, wx_t, bx, lam, *, seq, sgw, lw, name, exchanges=()):
    t = proj.shape[0]
    hd = lw // LRU_HEADS
    k_taps = conv_w.shape[0]
    x_spec, y_spec, o_spec, vec, mat = _mixer_b_specs(seq, hd, sgw, lw)
    cw_spec = pl.BlockSpec((k_taps, hd), lambda h, b: (0, h))

    def kern(xr_ref, yr_ref, dyb_ref, cw_ref, cb_ref, wa_ref, wat_ref, ba_ref, wx_ref, wxt_ref, bx_ref, lam_ref,
             dxr_ref, dyr_ref, dcw_ref, dcb_ref, dwa_ref, dba_ref, dwx_ref, dbx_ref, dlam_ref,
             s_xc, s_a, s_h, s_lam, s_dpa, s_dpx):
        @pl.when(pl.program_id(1) == 0)
        def _():
            for ref in (dcw_ref, dcb_ref, dwa_ref, dba_ref, dwx_ref, dbx_ref, dlam_ref):
                ref[...] = jnp.zeros_like(ref)

        s_xc[...] = _causal_conv(xr_ref[...], cw_ref[...], cb_ref[...])
        _, ia, _, a, _, sq = _lru_gates(s_xc[...], wa_ref, ba_ref, wx_ref, bx_ref, lam_ref)
        s_a[...] = a
        s_dpa[...] = _shift_up(a, 1)
        s_h[...] = sq * (ia * s_xc[...])
        _scan_rows(s_a, s_h, False)

        gel, dgel = _gelu_and_grad(yr_ref[...])
        dyb = dyb_ref[...]
        dyr_ref[...] = (dyb * s_h[...] * dgel).astype(BF16)
        s_lam[...] = dyb * gel
        _scan_rows(s_dpa, s_lam, True)
        ra, ia, sp, a, a2, sq = _lru_gates(s_xc[...], wa_ref, ba_ref, wx_ref, bx_ref, lam_ref)
        d_gx = s_lam[...]
        d_a = d_gx * _shift_down(s_h[...], 1)
        xc = s_xc[...]
        d_sq = d_gx * (ia * xc)
        d_ia = d_gx * (sq * xc)
        d_log_a = d_a * a - d_sq * (a2 / sq)
        d_ra = d_log_a * (-LRU_C * sp)
        d_sp = jnp.sum(d_log_a * (-LRU_C * ra), axis=0, keepdims=True)
        dlam_ref[...] += d_sp * (-_sigmoid(-lam_ref[...]))
        d_pa = d_ra * (ra * (1.0 - ra))
        d_px = d_ia * (ia * (1.0 - ia))
        s_dpa[...] = d_pa
        s_dpx[...] = d_px
        dba_ref[...] += jnp.sum(d_pa, axis=0, keepdims=True)
        dbx_ref[...] += jnp.sum(d_px, axis=0, keepdims=True)
        xcb = s_xc[...].astype(BF16)
        d_pa_b = s_dpa[...].astype(BF16)
        d_px_b = s_dpx[...].astype(BF16)
        contract_rows = (((0,), (0,)), ((), ()))
        dwa_ref[...] += lax.dot_general(xcb, d_pa_b, contract_rows, preferred_element_type=F32)
        dwx_ref[...] += lax.dot_general(xcb, d_px_b, contract_rows, preferred_element_type=F32)
        d_xc = (s_lam[...] * (sq * ia)
                + jnp.dot(d_pa_b, wat_ref[...].astype(BF16), preferred_element_type=F32)
                + jnp.dot(d_px_b, wxt_ref[...].astype(BF16), preferred_element_type=F32))
        dcb_ref[...] += jnp.sum(d_xc, axis=0, keepdims=True)
        dcw_ref[...] += _causal_conv_bwd_w(d_xc, xr_ref[...], k_taps)
        dxr_ref[...] = _causal_conv_bwd_x(d_xc, cw_ref[...]).astype(BF16)

    tile_shape = jax.ShapeDtypeStruct((t, lw), BF16)
    vec_shape = jax.ShapeDtypeStruct((1, lw), F32)
    mat_shape = jax.ShapeDtypeStruct((LRU_HEADS, hd, hd), F32)
    return _pallas(
        kern, name=name, grid=(LRU_HEADS, t // seq),
        in_specs=[x_spec, y_spec, o_spec, cw_spec, vec, mat, mat, vec, mat, mat, vec, vec],
        out_specs=[o_spec, o_spec, cw_spec, vec, mat, vec, mat, vec, vec],
        out_shape=[tile_shape, tile_shape, jax.ShapeDtypeStruct((k_taps, lw), F32), vec_shape, mat_shape, vec_shape,
                   mat_shape, vec_shape, vec_shape],
        operands=[proj, proj, d_yb, conv_w, conv_b, wa, wa_t, ba, wx, wx_t, bx, lam],
        scratch_shapes=[pltpu.VMEM((seq, hd), F32)] * 6, exchanges=exchanges)


FFN_TILE = 256


def _ffn_mid_fwd(up_pre, conv_w, conv_b, *, seq, name):
    t, f2 = up_pre.shape
    f = f2 // 2
    tc = _tile(f, FFN_TILE, LANES)
    nf = f // tc
    k_taps = conv_w.shape[0]

    def kern(pg_ref, pv_ref, wg_ref, wv_ref, bg_ref, bv_ref, o_ref):
        cg = _causal_conv(pg_ref[...], wg_ref[...], bg_ref[...])
        cv = _causal_conv(pv_ref[...], wv_ref[...], bv_ref[...])
        o_ref[...] = (_gelu(cg) * cv).astype(BF16)

    tile = lambda off: pl.BlockSpec((seq, tc), lambda j, b: (b, off + j))
    wspec = lambda off: pl.BlockSpec((k_taps, tc), lambda j, b: (0, off + j))
    bspec = lambda off: pl.BlockSpec((1, tc), lambda j, b: (0, off + j))
    return pl.pallas_call(
        kern, name=name, grid=(nf, t // seq),
        in_specs=[tile(0), tile(nf), wspec(0), wspec(nf), bspec(0), bspec(nf)], out_specs=tile(0),
        out_shape=jax.ShapeDtypeStruct((t, f), BF16),
        compiler_params=_params(2))(up_pre, up_pre, conv_w, conv_w, conv_b, conv_b)


def _ffn_mid_bwd(up_pre, d_act, conv_w, conv_b, *, seq, name):
    t, f2 = up_pre.shape
    f = f2 // 2
    tc = _tile(f, FFN_TILE, LANES)
    nf = f // tc
    k_taps = conv_w.shape[0]

    def kern(pg_ref, pv_ref, da_ref, wg_ref, wv_ref, bg_ref, bv_ref, dpg_ref, dpv_ref, dwg_ref, dwv_ref, dbg_ref, dbv_ref):
        @pl.when(pl.program_id(1) == 0)
        def _():
            for ref in (dwg_ref, dwv_ref, dbg_ref, dbv_ref):
                ref[...] = jnp.zeros_like(ref)

        pg = pg_ref[...]
        pv = pv_ref[...]
        gel, dgel = _gelu_and_grad(_causal_conv(pg, wg_ref[...], bg_ref[...]))
        cv = _causal_conv(pv, wv_ref[...], bv_ref[...])
        d_act_v = da_ref[...]
        d_cg = d_act_v * cv * dgel
        d_cv = d_act_v * gel
        dpg_ref[...] = _causal_conv_bwd_x(d_cg, wg_ref[...]).astype(BF16)
        dpv_ref[...] = _causal_conv_bwd_x(d_cv, wv_ref[...]).astype(BF16)
        dwg_ref[...] += _causal_conv_bwd_w(d_cg, pg, k_taps)
        dwv_ref[...] += _causal_conv_bwd_w(d_cv, pv, k_taps)
        dbg_ref[...] += jnp.sum(d_cg, axis=0, keepdims=True)
        dbv_ref[...] += jnp.sum(d_cv, axis=0, keepdims=True)

    tile = lambda off: pl.BlockSpec((seq, tc), lambda j, b: (b, off + j))
    wspec = lambda off: pl.BlockSpec((k_taps, tc), lambda j, b: (0, off + j))
    bspec = lambda off: pl.BlockSpec((1, tc), lambda j, b: (0, off + j))
    half = jax.ShapeDtypeStruct((t, f), BF16)
    wshape = jax.ShapeDtypeStruct((k_taps, f), F32)
    bshape = jax.ShapeDtypeStruct((1, f), F32)
    d_pg, d_pv, d_wg, d_wv, d_bg, d_bv = pl.pallas_call(
        kern, name=name, grid=(nf, t // seq),
        in_specs=[tile(0), tile(nf), tile(0), wspec(0), wspec(nf), bspec(0), bspec(nf)],
        out_specs=[tile(0), tile(0), wspec(0), wspec(0), bspec(0), bspec(0)],
        out_shape=[half, half, wshape, wshape, bshape, bshape],
        compiler_params=_params(2))(up_pre, up_pre, d_act, conv_w, conv_w, conv_b, conv_b)
    return (jnp.concatenate([d_pg, d_pv], axis=1), jnp.concatenate([d_wg, d_wv], axis=1),
            jnp.concatenate([d_bg, d_bv], axis=1))


ELEM_VMEM_BYTES = 24 << 20


def _as_2d(a):
    if a.ndim >= 2 and a.shape[-1] % LANES == 0 and a.size // a.shape[-1] >= SUBLANES:
        return a.reshape(-1, a.shape[-1])
    return a.reshape(-1, LANES)


def _row_tile(rows, bytes_per_row):
    return _tile(rows, max(16, ELEM_VMEM_BYTES // (2 * bytes_per_row)), 16)


def _cast_bf16(a, *, name):
    v = _as_2d(a)
    rows, cols = v.shape
    tr = _row_tile(rows, cols * (4 + 2))

    def kern(x_ref, o_ref):
        o_ref[...] = x_ref[...].astype(BF16)

    spec = pl.BlockSpec((tr, cols), lambda i: (i, 0))
    out = pl.pallas_call(kern, name=name, grid=(rows // tr,), in_specs=[spec], out_specs=spec,
                         out_shape=jax.ShapeDtypeStruct(v.shape, BF16), compiler_params=_params(1))(v)
    return out.reshape(a.shape)


def _add_sibling_part(own, core, got, *, name):
    _, _, rows, cols = own.shape
    tr = _row_tile(rows, cols * (2 + 2 + 2))

    def kern(core_ref, a_ref, b_ref, o_ref):
        o_ref[...] = (a_ref[...].astype(F32) + b_ref[...].astype(F32)).astype(BF16)

    spec = pl.BlockSpec((None, tr, cols), lambda ch, i, core_ref: (ch, i, 0))
    grid_spec = pltpu.PrefetchScalarGridSpec(
        num_scalar_prefetch=1, grid=(4, rows // tr),
        in_specs=[pl.BlockSpec((None, None, tr, cols), lambda ch, i, core_ref: (ch, core_ref[0], i, 0)), spec],
        out_specs=spec)
    return pl.pallas_call(kern, name=name, grid_spec=grid_spec, out_shape=jax.ShapeDtypeStruct(got.shape, BF16),
                          compiler_params=_params(2))(core, own, got)


def _sum_parts(parts, *, name):
    n_parts, rows, cols = parts.shape
    tr = _row_tile(rows, cols * 4 * (n_parts + 1))

    def kern(p_ref, o_ref):
        acc = p_ref[0].astype(F32)
        for p in range(1, n_parts):
            acc = acc + p_ref[p].astype(F32)
        o_ref[...] = acc

    return pl.pallas_call(
        kern, name=name, grid=(rows // tr,), in_specs=[pl.BlockSpec((n_parts, tr, cols), lambda i: (0, i, 0))],
        out_specs=pl.BlockSpec((tr, cols), lambda i: (i, 0)), out_shape=jax.ShapeDtypeStruct((rows, cols), F32),
        compiler_params=_params(1))(parts)


def _adamw(w, m, v, grad_parts, *, name):
    shape = w.shape
    w2 = _as_2d(w)
    rows, cols = w2.shape
    n_parts = grad_parts.shape[0]
    parts = grad_parts.reshape(n_parts, rows, cols)
    tr = _row_tile(rows, cols * (3 * 4 + n_parts * parts.dtype.itemsize + 4 * 4))
    c_m = 1.0 - ADAM_B1 ** ADAM_STEP
    c_v = 1.0 - ADAM_B2 ** ADAM_STEP

    def kern(w_ref, m_ref, v_ref, p_ref, g_ref, d_ref, nm_ref, nv_ref):
        g = p_ref[0].astype(F32)
        for p in range(1, n_parts):
            g = g + p_ref[p].astype(F32)
        new_m = ADAM_B1 * m_ref[...] + (1.0 - ADAM_B1) * g
        new_v = ADAM_B2 * v_ref[...] + (1.0 - ADAM_B2) * (g * g)
        g_ref[...] = g
        nm_ref[...] = new_m
        nv_ref[...] = new_v
        d_ref[...] = -ADAM_LR * ((new_m / c_m) / (jnp.sqrt(new_v / c_v) + ADAM_EPS) + ADAM_WD * w_ref[...])

    spec = pl.BlockSpec((tr, cols), lambda i: (i, 0))
    out = jax.ShapeDtypeStruct((rows, cols), F32)
    res = pl.pallas_call(
        kern, name=name, grid=(rows // tr,),
        in_specs=[spec, spec, spec, pl.BlockSpec((n_parts, tr, cols), lambda i: (0, i, 0))],
        out_specs=[spec] * 4, out_shape=[out] * 4, compiler_params=_params(1))(w2, _as_2d(m), _as_2d(v), parts)
    return [r.reshape(shape) for r in res]


def _all_gather(shards, *, name):
    n = len(shards)

    def body(*refs):
        ins, outs = refs[:n], refs[n:2 * n]
        send_sems, recv_sems, local_sems = refs[2 * n:]
        x, y, c = _place()
        me, sibling = (x, y, c), (x, y, 1 - c)
        chips = [(1 - x, y), (x, 1 - y), (1 - x, 1 - y)]

        def slot(i, dev):
            return outs[i].at[4 * dev[0] + 2 * dev[1] + dev[2]]

        def copy(i, k, block, to, src=None):
            return pltpu.make_async_remote_copy(
                src_ref=slot(i, block) if src is None else src, dst_ref=slot(i, block),
                send_sem=send_sems.at[i, k], recv_sem=recv_sems.at[i, k], device_id=to, device_id_type=MESH)

        mine = [pltpu.make_async_copy(ins[i], slot(i, me), local_sems.at[i]) for i in range(n)]
        for cp in mine:
            cp.start()
        first = []
        for i in range(n):
            first.append(copy(i, 0, me, sibling, src=ins[i]))
            first += [copy(i, 1 + j, me, (*chip, c), src=ins[i]) for j, chip in enumerate(chips)]
        for cp in first:
            cp.start()
        passed = []
        for j, chip in enumerate(chips):
            for i in range(n):
                copy(i, 1 + j, (*chip, c), me).wait_recv()
                onward = copy(i, 4 + j, (*chip, c), sibling)
                onward.start()
                passed.append(onward)
        for i in range(n):
            copy(i, 0, sibling, me).wait_recv()
            for j, chip in enumerate(chips):
                copy(i, 4 + j, (*chip, 1 - c), me).wait_recv()
        for cp in first + passed:
            cp.wait_send()
        for cp in mine:
            cp.wait()

    return pl.pallas_call(
        body, name=name, in_specs=[ANY] * n, out_specs=[ANY] * n,
        out_shape=[jax.ShapeDtypeStruct((N_DEV,) + s.shape, s.dtype) for s in shards],
        scratch_shapes=[pltpu.SemaphoreType.DMA((n, 7)), pltpu.SemaphoreType.DMA((n, 7)), pltpu.SemaphoreType.DMA((n,))],
    )(*shards)


def _by_chip_and_core(grad):
    return grad.reshape(4, 2, -1, grad.shape[-1])


def _pack(vectors):
    flat = [v.reshape(-1).astype(F32) for v in vectors]
    sizes = [f.shape[0] for f in flat]
    total = sum(sizes)
    padded = -(-total // (SUBLANES * LANES)) * (SUBLANES * LANES)
    if padded > total:
        flat.append(jnp.zeros((padded - total,), F32))
    return jnp.concatenate(flat).reshape(-1, LANES), sizes


def _unpack(packed, sizes, shapes):
    flat = packed.reshape(-1)
    out, off = [], 0
    for size, shape in zip(sizes, shapes):
        out.append(flat[off:off + size].reshape(shape))
        off += size
    return out


def kernel(x, g_mix, w_in, sg_ln_g, sg_ln_b, sg_w, sg_b, lru_conv_w, lru_conv_b, lru_wa, lru_ba, lru_wx, lru_bx, lru_lam, p_sg, p_lru, w_out, g_ffn, w_up, ffn_conv_w, ffn_conv_b, w_down, g_final, loss_target, m_g_mix, m_w_in, m_sg_ln_g, m_sg_ln_b, m_sg_w, m_sg_b, m_lru_conv_w, m_lru_conv_b, m_lru_wa, m_lru_ba, m_lru_wx, m_lru_bx, m_lru_lam, m_p_sg, m_p_lru, m_w_out, m_g_ffn, m_w_up, m_ffn_conv_w, m_ffn_conv_b, m_w_down, m_g_final, v_g_mix, v_w_in, v_sg_ln_g, v_sg_ln_b, v_sg_w, v_sg_b, v_lru_conv_w, v_lru_conv_b, v_lru_wa, v_lru_ba, v_lru_wx, v_lru_bx, v_lru_lam, v_p_sg, v_p_lru, v_w_out, v_g_ffn, v_w_up, v_ffn_conv_w, v_ffn_conv_b, v_w_down, v_g_final):
    weights = dict(g_mix=g_mix, w_in=w_in, sg_ln_g=sg_ln_g, sg_ln_b=sg_ln_b, sg_w=sg_w, sg_b=sg_b, lru_conv_w=lru_conv_w,
                   lru_conv_b=lru_conv_b, lru_wa=lru_wa, lru_ba=lru_ba, lru_wx=lru_wx, lru_bx=lru_bx, lru_lam=lru_lam,
                   p_sg=p_sg, p_lru=p_lru, w_out=w_out, g_ffn=g_ffn, w_up=w_up, ffn_conv_w=ffn_conv_w,
                   ffn_conv_b=ffn_conv_b, w_down=w_down, g_final=g_final)
    m_in = dict(g_mix=m_g_mix, w_in=m_w_in, sg_ln_g=m_sg_ln_g, sg_ln_b=m_sg_ln_b, sg_w=m_sg_w, sg_b=m_sg_b,
                lru_conv_w=m_lru_conv_w, lru_conv_b=m_lru_conv_b, lru_wa=m_lru_wa, lru_ba=m_lru_ba, lru_wx=m_lru_wx,
                lru_bx=m_lru_bx, lru_lam=m_lru_lam, p_sg=m_p_sg, p_lru=m_p_lru, w_out=m_w_out, g_ffn=m_g_ffn,
                w_up=m_w_up, ffn_conv_w=m_ffn_conv_w, ffn_conv_b=m_ffn_conv_b, w_down=m_w_down, g_final=m_g_final)
    v_in = dict(g_mix=v_g_mix, w_in=v_w_in, sg_ln_g=v_sg_ln_g, sg_ln_b=v_sg_ln_b, sg_w=v_sg_w, sg_b=v_sg_b,
                lru_conv_w=v_lru_conv_w, lru_conv_b=v_lru_conv_b, lru_wa=v_lru_wa, lru_ba=v_lru_ba, lru_wx=v_lru_wx,
                lru_bx=v_lru_bx, lru_lam=v_lru_lam, p_sg=v_p_sg, p_lru=v_p_lru, w_out=v_w_out, g_ffn=v_g_ffn,
                w_up=v_w_up, ffn_conv_w=v_ffn_conv_w, ffn_conv_b=v_ffn_conv_b, w_down=v_w_down, g_final=v_g_final)
    order = list(weights)

    n_seq, seq, d = x.shape
    t = n_seq * seq
    sgw = sg_ln_g.shape[-1]
    lw = lru_lam.shape[-1]
    hd = lw // LRU_HEADS
    f2 = ffn_conv_b.shape[-1]
    gate_col = (2 * sgw + 2 * lw) // d
    xi, yi, ci = _place()
    dev = 4 * xi + 2 * yi + ci

    core = jnp.reshape(ci, (1,)).astype(jnp.int32)
    first_leg = functools.partial(_gather_first_leg, place_own=False)
    chip_swap = functools.partial(_swap_with_chips, place_own=False)
    shards, in_flight = {}, {}

    def landed(keys, after, name):
        return _await_exchange(first_leg, [in_flight[k] for k in keys], after, name=name)[1]

    needed_first = ["w_in", "lru_wa", "lru_wx", "p_sg", "taps"]
    by_need = needed_first + ["p_lru", "w_out", "w_up", "w_down"]
    shards.update({k: _cast_bf16(weights[k][0], name=f"cast_{k}") for k in by_need if k != "taps"})
    shards["taps"], tap_sizes = _pack([lru_conv_w[0], ffn_conv_w[0]])
    started, g_mix_after_start = _start_exchange(
        first_leg, [shards[k] for k in by_need], [_own_block_in_place(shards[k], dev) for k in by_need], g_mix,
        name="start_weight_gather")
    in_flight.update(zip(by_need, started))

    def rows_in_order(g8):
        return g8.reshape(1, -1, g8.shape[-1])

    x2d = x.reshape(t, d)
    h1 = _rmsnorm_fwd(x2d, g_mix_after_start, name="norm_mix")
    ((w_in_g, wa_8, wx_8, p_sg_g, taps_8),) = _exchange_now(
        [_gather_second_leg(landed(needed_first, h1, "await_first_gather"))], name="second_leg_first_gather")
    wa_g, wx_g = (jnp.swapaxes(w8, 0, 1).reshape(LRU_HEADS, hd, hd) for w8 in (wa_8, wx_8))
    wa_t, wx_t = jnp.swapaxes(wa_g, 1, 2), jnp.swapaxes(wx_g, 1, 2)
    tap_parts = [_unpack(taps_8[k], tap_sizes, [lru_conv_w.shape[1:], ffn_conv_w.shape[1:]]) for k in range(N_DEV)]
    lru_cw = jnp.concatenate([p[0] for p in tap_parts], axis=1)
    ffn_cw = jnp.concatenate([p[1] for p in tap_parts], axis=1)
    sg_w0 = sg_w[0]
    sg_w_t = jnp.swapaxes(sg_w0, 1, 2)
    sg_b_t = sg_b[0].T
    proj, _ = _mm_nn(h1, w_in_g, out_dtype=F32, name="proj_in")
    y_a = _mixer_a_fwd(proj, sg_ln_g, sg_ln_b, sg_w0, sg_b_t, sgw, name="mixer_a_fwd")
    y_b = _mixer_b_fwd(proj, lru_cw, lru_conv_b, wa_g, lru_ba, wx_g, lru_bx, lru_lam, seq=seq, sgw=sgw, lw=lw,
                       name="mixer_b_fwd")
    m_a, ((p_lru_8, w_out_8),) = _mm_nn(
        y_a, p_sg_g, out_dtype=F32, name="proj_sg",
        exchanges=[_gather_second_leg(landed(["p_lru", "w_out"], y_b, "await_p_lru_w_out"))])
    p_lru_g, w_out_g = rows_in_order(p_lru_8), rows_in_order(w_out_8)
    m_b, _ = _mm_nn(y_b, p_lru_g, out_dtype=F32, name="proj_lru")
    merged = _merge_fwd(m_a, m_b, proj, gate_col, name="merge_fwd")
    x1, ((w_up_g,),) = _mm_nn(merged, w_out_g, out_dtype=F32, residual=x2d, name="proj_out",
                              exchanges=[_gather_second_leg(landed(["w_up"], merged, "await_w_up"))])
    h2 = _rmsnorm_fwd(x1, g_ffn, name="norm_ffn")
    up_pre, _ = _mm_nn(h2, w_up_g, out_dtype=F32, name="ffn_up")
    act = _ffn_mid_fwd(up_pre, ffn_cw, ffn_conv_b, seq=seq, name="ffn_mid_fwd")
    ((w_down_8,),) = _exchange_now([_gather_second_leg(landed(["w_down"], act, "await_w_down"))], name="second_leg_w_down")
    w_down_g = rows_in_order(w_down_8)
    x2, _ = _mm_nn(act, w_down_g, out_dtype=F32, residual=x1, name="ffn_down")
    d_x2, d_x2_b, d_g_final, loss_part = _loss_head(x2, g_final.reshape(1, d), loss_target.reshape(t, d), name="loss_head")
    loss = lax.psum(loss_part[0, 0], ("x", "y", "c"))

    def by_rows(g):
        return g.reshape(N_DEV, -1, g.shape[-1])

    def by_head_rows(g):
        return jnp.swapaxes(g.reshape(LRU_HEADS, N_DEV, hd // N_DEV, hd), 0, 1)

    def token():
        return jnp.zeros((SUBLANES, LANES), F32)

    def start_sibling_swap(views, key):
        return _start_exchange(_swap_with_sibling, views, [lax.empty((4,) + v.shape[2:], v.dtype) for v in views], token(),
                               name=f"start_sibling_{key}")

    chip_swaps = {}

    def sum_and_start_chip_swap(keys, sibling_swap, done, ride=None):
        views, from_sibling = _await_exchange(_swap_with_sibling, sibling_swap, done, name=f"await_sibling_{keys[0]}")
        sums = [_add_sibling_part(v, core, s, name=f"chip_sum_{k}") for k, v, s in zip(keys, views, from_sibling)]
        started, ride = _start_exchange(chip_swap, sums, [_own_block_first(s, 2 * xi + yi) for s in sums],
                                        token() if ride is None else ride, name=f"start_chips_{keys[0]}")
        chip_swaps.update(zip(keys, started))
        return ride

    d_w_down, _ = _mm_tn(act, d_x2_b, 1, name="grad_w_down")
    v_down = _by_chip_and_core(by_rows(d_w_down))
    swap, started = start_sibling_swap([v_down], "w_down")
    d_act, _ = _mm_nt(d_x2_b, w_down_g, name="bwd_ffn_down", after=started)
    ffn_cb = sum_and_start_chip_swap(["w_down"], swap, d_act, ride=ffn_conv_b)
    d_up_pre, d_ffn_cw, d_ffn_cb = _ffn_mid_bwd(up_pre, d_act, ffn_cw, ffn_cb, seq=seq, name="ffn_mid_bwd")
    d_w_up, _ = _mm_tn(h2, d_up_pre, N_DEV, name="grad_w_up")
    v_up = _by_chip_and_core(d_w_up)
    swap, started = start_sibling_swap([v_up], "w_up")
    d_h2, _ = _mm_nt(d_up_pre, w_up_g, name="bwd_ffn_up", after=started)
    g_ffn_then = sum_and_start_chip_swap(["w_up"], swap, d_h2, ride=g_ffn)
    d_x1, d_x1_b, d_g_ffn = _rmsnorm_bwd(x1, g_ffn_then, d_h2, d_x2, name="norm_ffn_bwd")
    d_w_out, _ = _mm_tn(merged, d_x1_b, 1, name="grad_w_out")
    v_out = _by_chip_and_core(by_rows(d_w_out))
    swap, started = start_sibling_swap([v_out], "w_out")
    d_merged, _ = _mm_nt(d_x1_b, w_out_g, name="bwd_proj_out", after=started)
    started = sum_and_start_chip_swap(["w_out"], swap, d_merged)
    d_m_a, d_m_b, d_gates = _merge_bwd(d_merged, m_a, m_b, proj, gate_col, name="merge_bwd", after=started)
    d_p_sg, _ = _mm_tn(y_a, d_m_a, N_DEV, name="grad_p_sg")
    d_p_lru, _ = _mm_tn(y_b, d_m_b, 1, name="grad_p_lru")
    v_sg, v_lru = _by_chip_and_core(d_p_sg), _by_chip_and_core(by_rows(d_p_lru))
    swap, started = start_sibling_swap([v_sg, v_lru], "p_sg")
    d_y_a, _ = _mm_nt(d_m_a, p_sg_g, name="bwd_proj_sg", after=started)
    d_y_b, _ = _mm_nt(d_m_b, p_lru_g, name="bwd_proj_lru")
    lru_cb = sum_and_start_chip_swap(["p_sg", "p_lru"], swap, d_y_b, ride=lru_conv_b)
    d_zuv, d_sg_w, d_sg_b_t, d_ln_g, d_ln_b = _mixer_a_bwd(proj, d_y_a, sg_ln_g, sg_ln_b, sg_w0, sg_w_t, sg_b_t, sgw,
                                                           name="mixer_a_bwd")
    (d_xr, d_yr, d_lru_cw, d_lru_cb, d_wa, d_ba, d_wx, d_bx, d_lam), _ = _mixer_b_bwd(
        proj, d_y_b, lru_cw, lru_cb, wa_g, wa_t, lru_ba, wx_g, wx_t, lru_bx, lru_lam, seq=seq, sgw=sgw, lw=lw,
        name="mixer_b_bwd")
    d_proj = jnp.concatenate([d_zuv, d_xr, d_yr, d_gates], axis=1)
    d_w_in, _ = _mm_tn(h1, d_proj, N_DEV, name="grad_w_in")
    v_win = _by_chip_and_core(d_w_in)
    v_wa = _by_chip_and_core(_cast_bf16(by_head_rows(d_wa), name="cast_grad_wa"))
    v_wx = _by_chip_and_core(_cast_bf16(by_head_rows(d_wx), name="cast_grad_wx"))
    swap, started = start_sibling_swap([v_win, v_wa, v_wx], "w_in")
    started = sum_and_start_chip_swap(["w_in", "lru_wa", "lru_wx"], swap, started)
    d_h1, _ = _mm_nt(d_proj, w_in_g, name="bwd_proj_in", after=started)
    grad_x, _, d_g_mix = _rmsnorm_bwd(x2d, g_mix, d_h1, d_x1, name="norm_mix_bwd")

    small = ["g_mix", "sg_ln_g", "sg_ln_b", "sg_w", "sg_b", "lru_conv_b", "lru_ba", "lru_bx", "lru_lam", "g_ffn",
             "ffn_conv_b", "g_final", "lru_conv_w", "ffn_conv_w"]
    small_parts = dict(g_mix=d_g_mix, sg_ln_g=d_ln_g, sg_ln_b=d_ln_b, sg_w=d_sg_w, sg_b=d_sg_b_t.T, lru_conv_b=d_lru_cb,
                       lru_ba=d_ba, lru_bx=d_bx, lru_lam=d_lam, g_ffn=d_g_ffn, ffn_conv_b=d_ffn_cb, g_final=d_g_final,
                       lru_conv_w=d_lru_cw, ffn_conv_w=d_ffn_cw)
    packed, sizes = _pack([small_parts[k] for k in small])
    (all_small,) = _all_gather([packed], name="gather_small_grads")
    small_sum = _sum_parts(all_small, name="sum_small_grads")
    small_grads = dict(zip(small, _unpack(small_sum, sizes, [small_parts[k].shape for k in small])))
    for k in ("lru_conv_w", "ffn_conv_w"):
        n_loc = weights[k].shape[-1]
        small_grads[k] = lax.dynamic_slice_in_dim(small_grads[k], dev * n_loc, n_loc, axis=1)

    grads, deltas, new_m, new_v = {}, {}, {}, {}

    def update(k, parts):
        grads[k], deltas[k], new_m[k], new_v[k] = _adamw(weights[k], m_in[k], v_in[k], parts, name=f"adamw_{k}")

    def update_when_landed(keys, after):
        _, parts = _await_exchange(chip_swap, [chip_swaps[k] for k in keys], after, name=f"await_chips_{keys[0]}")
        for k, p in zip(keys, parts):
            update(k, p)

    for keys in (["w_down"], ["w_up"], ["w_out"], ["p_sg", "p_lru"]):
        update_when_landed(keys, grad_x)
    for k in small:
        update(k, small_grads[k][None])
    update_when_landed(["w_in", "lru_wa", "lru_wx"], deltas["w_up"])

    return (loss, grad_x.reshape(x.shape), *[grads[k] for k in order], *[deltas[k] for k in order],
            *[new_m[k] for k in order], *[new_v[k] for k in order])
```

```python
import functools
import math
from typing import Callable, NamedTuple

import jax
import jax.numpy as jnp
from jax import lax
from jax.experimental import pallas as pl
from jax.experimental.pallas import tpu as pltpu

F32 = jnp.float32
BF16 = jnp.bfloat16
MESH = pl.DeviceIdType.MESH
ANY = pl.BlockSpec(memory_space=pl.ANY)

N_DEV = 8
EPS = 1e-6
CHUNK = 128
SG_GROUPS = 8
LRU_HEADS = 16
LRU_C = 8.0
ADAM_LR = 0.001
ADAM_B1 = 0.9
ADAM_B2 = 0.999
ADAM_EPS = 1e-08
ADAM_WD = 0.01
ADAM_STEP = 10

V7X_VMEM_LIMIT = 56 * 1024 * 1024
LANES = 128
SUBLANES = 8
MXU = 256

_GELU_C0 = math.sqrt(2.0 / math.pi)
_GELU_C1 = 0.044715


def _params(n_axes):
    return pltpu.CompilerParams(dimension_semantics=("arbitrary",) * n_axes, vmem_limit_bytes=V7X_VMEM_LIMIT)


def _tile(dim, pref, align):
    t = (min(pref, dim) // align) * align
    while t >= align:
        if dim % t == 0:
            return t
        t -= align
    return dim


def _gelu(x):
    return x * (0.5 * (1.0 + jnp.tanh(_GELU_C0 * (x + _GELU_C1 * (x * x * x)))))


def _gelu_and_grad(x):
    t = jnp.tanh(_GELU_C0 * (x + _GELU_C1 * (x * x * x)))
    cdf = 0.5 * (1.0 + t)
    dcdf = 0.5 * (1.0 - t * t) * (_GELU_C0 * (1.0 + 3.0 * _GELU_C1 * (x * x)))
    return x * cdf, cdf + x * dcdf


def _sigmoid(x):
    return 1.0 / (1.0 + jnp.exp(-x))


def _shift_down(x, d):
    if d == 0:
        return x
    row = lax.broadcasted_iota(jnp.int32, x.shape, 0)
    return jnp.where(row >= d, pltpu.roll(x, d, 0), 0.0)


def _shift_up(x, d):
    if d == 0:
        return x
    s = x.shape[0]
    row = lax.broadcasted_iota(jnp.int32, x.shape, 0)
    return jnp.where(row < s - d, pltpu.roll(x, s - d, 0), 0.0)


def _causal_conv(x, w, b):
    k_taps = w.shape[0]
    out = _shift_down(x, k_taps - 1) * w[0:1, :]
    for k in range(1, k_taps):
        out = out + _shift_down(x, k_taps - 1 - k) * w[k:k + 1, :]
    return out + b


def _causal_conv_bwd_x(d_out, w):
    k_taps = w.shape[0]
    d_x = _shift_up(d_out, k_taps - 1) * w[0:1, :]
    for k in range(1, k_taps):
        d_x = d_x + _shift_up(d_out, k_taps - 1 - k) * w[k:k + 1, :]
    return d_x


def _causal_conv_bwd_w(d_out, x, k_taps):
    rows = [jnp.sum(d_out * _shift_down(x, k_taps - 1 - k), axis=0, keepdims=True) for k in range(k_taps)]
    return jnp.concatenate(rows, axis=0)


def _place():
    return lax.axis_index("x"), lax.axis_index("y"), lax.axis_index("c")


def _other_chips(x, y):
    return [(1 - x, y), (x, 1 - y), (1 - x, 1 - y)]


class _Exchange(NamedTuple):
    ins: tuple
    outs: tuple
    in_place: bool
    n_remote: int
    n_local: int
    copies: Callable


def _remote(src, dst, send_sem, recv_sem, to):
    return pltpu.make_async_remote_copy(src_ref=src, dst_ref=dst, send_sem=send_sem, recv_sem=recv_sem, device_id=to,
                                        device_id_type=MESH)


def _gather_first_leg(shards, place_own=True):
    n = len(shards)

    def copies(ins, outs, send_sems, recv_sems, local_sems):
        x, y, c = _place()
        peers = [(x, y, 1 - c)] + [(*chip, c) for chip in _other_chips(x, y)]
        slot = lambda i, dev: outs[i].at[4 * dev[0] + 2 * dev[1] + dev[2]]
        sends = [_remote(ins[i], slot(i, (x, y, c)), send_sems[i].at[k], recv_sems[i].at[k], to)
                 for i in range(n) for k, to in enumerate(peers)]
        receives = [_remote(ins[i], slot(i, frm), send_sems[i].at[k], recv_sems[i].at[k], frm)
                    for i in range(n) for k, frm in enumerate(peers)]
        local = [pltpu.make_async_copy(ins[i], slot(i, (x, y, c)), local_sems[i].at[0]) for i in range(n)] if place_own else []
        return sends, receives, local

    outs = tuple(jax.ShapeDtypeStruct((N_DEV,) + s.shape, s.dtype) for s in shards)
    return _Exchange(tuple(shards), outs, False, 4, int(place_own), copies)


def _gather_second_leg(gathered):
    n = len(gathered)

    def copies(ins, outs, send_sems, recv_sems, local_sems):
        x, y, c = _place()
        slot = lambda i, chip, core: outs[i].at[4 * chip[0] + 2 * chip[1] + core]
        sends = [_remote(slot(i, chip, c), slot(i, chip, c), send_sems[i].at[j], recv_sems[i].at[j], (x, y, 1 - c))
                 for i in range(n) for j, chip in enumerate(_other_chips(x, y))]
        receives = [_remote(slot(i, chip, 1 - c), slot(i, chip, 1 - c), send_sems[i].at[j], recv_sems[i].at[j], (x, y, 1 - c))
                    for i in range(n) for j, chip in enumerate(_other_chips(x, y))]
        return sends, receives, []

    outs = tuple(jax.ShapeDtypeStruct(g.shape, g.dtype) for g in gathered)
    return _Exchange(tuple(gathered), outs, True, 3, 0, copies)


def _swap_with_sibling(parts):
    n = len(parts)

    def copies(ins, outs, send_sems, recv_sems, local_sems):
        x, y, c = _place()
        both = [_remote(ins[i].at[ch, 1 - c], outs[i].at[ch], send_sems[i].at[ch], recv_sems[i].at[ch], (x, y, 1 - c))
                for i in range(n) for ch in range(4)]
        return both, both, []

    outs = tuple(jax.ShapeDtypeStruct((4,) + p.shape[2:], p.dtype) for p in parts)
    return _Exchange(tuple(parts), outs, False, 4, 0, copies)


def _swap_with_chips(parts, place_own=True):
    n = len(parts)

    def copies(ins, outs, send_sems, recv_sems, local_sems):
        x, y, c = _place()
        both = [_remote(ins[i].at[2 * chip[0] + chip[1]], outs[i].at[1 + j], send_sems[i].at[j], recv_sems[i].at[j], (*chip, c))
                for i in range(n) for j, chip in enumerate(_other_chips(x, y))]
        local = [pltpu.make_async_copy(ins[i].at[2 * x + y], outs[i].at[0], local_sems[i].at[0]) for i in range(n)] if place_own else []
        return both, both, local

    outs = tuple(jax.ShapeDtypeStruct(p.shape, p.dtype) for p in parts)
    return _Exchange(tuple(parts), outs, False, 3, int(place_own), copies)


def _exchange_plumbing(exchanges):
    operands = [a for ex in exchanges for a in ex.ins]
    results = [s for ex in exchanges for s in ex.outs]
    scratch, in_place, at = [], {}, 0
    for ex in exchanges:
        n = len(ex.ins)
        scratch += [pltpu.SemaphoreType.DMA((n, ex.n_remote))] * 2
        if ex.n_local:
            scratch.append(pltpu.SemaphoreType.DMA((n, ex.n_local)))
        if ex.in_place:
            in_place.update({at + i: at + i for i in range(n)})
        at += n

    def copies(in_refs, out_refs, sem_refs):
        sends, receives, local = [], [], []
        at, sem_at = 0, 0
        for ex in exchanges:
            n, n_sem = len(ex.ins), 3 if ex.n_local else 2
            per_operand = [[sem.at[i] for i in range(n)] for sem in sem_refs[sem_at:sem_at + n_sem]] + [[]] * (3 - n_sem)
            s, r, l = ex.copies(in_refs[at:at + n], out_refs[at:at + n], *per_operand)
            sends, receives, local = sends + s, receives + r, local + l
            at, sem_at = at + n, sem_at + n_sem
        return sends, receives, local

    return operands, results, scratch, in_place, copies


def _start_all(copies):
    sends, _, local = copies
    for cp in local + sends:
        cp.start()


def _wait_all(copies):
    sends, receives, local = copies
    for cp in receives:
        cp.wait_recv()
    for cp in sends:
        cp.wait_send()
    for cp in local:
        cp.wait()


def _exchange_now(exchanges, *, name):
    operands, results, scratch, in_place, copies = _exchange_plumbing(exchanges)
    n = len(operands)

    def body(*refs):
        made = copies(refs[:n], refs[n:2 * n], refs[2 * n:])
        _start_all(made)
        _wait_all(made)

    out = pl.pallas_call(body, name=name, in_specs=[ANY] * n, out_specs=[ANY] * n, out_shape=results,
                         scratch_shapes=scratch, input_output_aliases=in_place)(*operands)
    return _split(out, exchanges)


def _split(flat, exchanges):
    out, at = [], 0
    for ex in exchanges:
        out.append(list(flat[at:at + len(ex.ins)]))
        at += len(ex.ins)
    return out


HBM = pl.BlockSpec(memory_space=pltpu.HBM)
SEMAPHORES = pl.BlockSpec(memory_space=pltpu.SEMAPHORE)
SPLIT_COPY = pltpu.CompilerParams(has_side_effects=pltpu.SideEffectType.DATAFLOW_SIDE_EFFECTING)


def _start_exchange(make, operands, landings, before, *, name):
    n = len(operands)
    ex = make(operands)

    def body(*refs):
        sends, _, _ = ex.copies(refs[:n], refs[n:2 * n], refs[2 * n + 1:3 * n + 1], refs[3 * n + 1:4 * n + 1], [])
        for cp in sends:
            cp.start()

    buffers = [pltpu.with_memory_space_constraint(a, pltpu.HBM) for a in list(operands) + list(landings) + [before]]
    out = pl.pallas_call(
        body, name=name, in_specs=[HBM] * (2 * n + 1), out_specs=[SEMAPHORES] * (2 * n) + [HBM] * (2 * n + 1),
        out_shape=[pltpu.SemaphoreType.DMA((ex.n_remote,))] * (2 * n) + [pltpu.HBM(a.shape, a.dtype) for a in buffers],
        input_output_aliases={i: 2 * n + i for i in range(2 * n + 1)}, compiler_params=SPLIT_COPY)(*buffers)
    return [(out[i], out[n + i], out[2 * n + i], out[3 * n + i]) for i in range(n)], out[4 * n]


def _await_exchange(make, in_flight, after, *, name):
    n = len(in_flight)
    send_sems, recv_sems, operands, landings = zip(*in_flight)
    ex = make(operands)

    def body(*refs):
        sends, receives, _ = ex.copies(refs[:n], refs[n:2 * n], refs[2 * n:3 * n], refs[3 * n:4 * n], [])
        for cp in receives:
            cp.wait_recv()
        for cp in sends:
            cp.wait_send()

    out = pl.pallas_call(
        body, name=name, in_specs=[HBM] * (2 * n) + [SEMAPHORES] * (2 * n) + [ANY], out_specs=[HBM] * (2 * n),
        out_shape=[pltpu.HBM(a.shape, a.dtype) for a in operands + landings],
        input_output_aliases={i: i for i in range(2 * n)}, compiler_params=SPLIT_COPY,
    )(*operands, *landings, *send_sems, *recv_sems, after)
    return list(out[:n]), list(out[n:])


def _own_block_first(blocks, index):
    own = lax.dynamic_index_in_dim(blocks, index, 0, keepdims=True)
    return lax.dynamic_update_index_in_dim(lax.empty(blocks.shape, blocks.dtype), own, 0, 0)


def _own_block_in_place(shard, index):
    return lax.dynamic_update_index_in_dim(lax.empty((N_DEV,) + shard.shape, shard.dtype), shard, index, 0)


def _pallas(kern, *, name, grid, in_specs, out_specs, out_shape, operands, scratch_shapes=(), exchanges=(), after=None):
    ex_operands, ex_results, ex_scratch, in_place, copies = _exchange_plumbing(exchanges)
    if after is not None:
        ex_operands = [after] + ex_operands
        in_place = {i + 1: o for i, o in in_place.items()}
    n_in, n_out, n_scratch, n_ex = len(in_specs), len(out_specs), len(scratch_shapes), len(ex_results)
    n_unread = len(ex_operands) - n_ex

    def body(*refs):
        ins, refs = refs[:n_in], refs[n_in + n_unread:]
        ex_ins, refs = refs[:n_ex], refs[n_ex:]
        outs, refs = refs[:n_out], refs[n_out:]
        ex_outs, refs = refs[:n_ex], refs[n_ex:]
        scratch, sems = refs[:n_scratch], refs[n_scratch:]
        if exchanges:
            first = functools.reduce(jnp.logical_and, [pl.program_id(a) == 0 for a in range(len(grid))])
            last = functools.reduce(jnp.logical_and, [pl.program_id(a) == g - 1 for a, g in enumerate(grid)])

            @pl.when(first)
            def _():
                _start_all(copies(ex_ins, ex_outs, sems))

        kern(*ins, *outs, *scratch)
        if exchanges:
            @pl.when(last)
            def _():
                _wait_all(copies(ex_ins, ex_outs, sems))

    res = pl.pallas_call(
        body, name=name, grid=grid, in_specs=list(in_specs) + [ANY] * len(ex_operands), out_specs=list(out_specs) + [ANY] * n_ex,
        out_shape=list(out_shape) + ex_results, scratch_shapes=list(scratch_shapes) + ex_scratch,
        input_output_aliases={n_in + i: n_out + o for i, o in in_place.items()},
        compiler_params=_params(len(grid)))(*operands, *ex_operands)
    return list(res[:n_out]), _split(res[n_out:], exchanges)


def _accumulate(step, n_steps, acc, value, finish):
    if n_steps == 1:
        finish(value)
        return

    @pl.when(step == 0)
    def _():
        acc[0][...] = value

    @pl.when(step > 0)
    def _():
        acc[0][...] += value

    @pl.when(step == n_steps - 1)
    def _():
        finish(acc[0][...])


def _mm_nn(a, w, *, out_dtype, name, residual=None, exchanges=(), a_part=0):
    m = a.shape[0]
    nb, k, n_blk = w.shape
    tm, tn, tk = _tile(m, 512, MXU), _tile(n_blk, 1536, MXU), _tile(k, 4096, MXU)
    per = n_blk // tn
    nk = k // tk

    def kern(*refs):
        a_ref, w_ref = refs[:2]
        r_ref = None if residual is None else refs[2]
        o_ref, acc = refs[2 + (residual is not None)], refs[3 + (residual is not None):]

        def finish(total):
            o_ref[...] = (total if r_ref is None else total + r_ref[...]).astype(o_ref.dtype)

        _accumulate(pl.program_id(2), nk, acc, jnp.dot(a_ref[...], w_ref[...], preferred_element_type=F32), finish)

    tile = pl.BlockSpec((tm, tn), lambda j, i, kk: (i, j))
    in_specs = [pl.BlockSpec((tm, tk), lambda j, i, kk: (i, a_part * nk + kk)),
                pl.BlockSpec((None, tk, tn), lambda j, i, kk: (j // per, kk, j % per))]
    operands = [a, w]
    if residual is not None:
        in_specs.append(tile)
        operands.append(residual)
    (out,), carried = _pallas(
        kern, name=name, grid=(nb * per, m // tm, nk), in_specs=in_specs, out_specs=[tile],
        out_shape=[jax.ShapeDtypeStruct((m, nb * n_blk), out_dtype)], operands=operands,
        scratch_shapes=[pltpu.VMEM((tm, tn), F32)] * (nk > 1), exchanges=exchanges)
    return out, carried


def _mm_nt(g, w, *, name, exchanges=(), after=None):
    m, n = g.shape
    nb, k, n_blk = w.shape
    tm, tko, tn = _tile(m, 1024, MXU), _tile(k, 1024, MXU), _tile(n_blk, 3072, MXU)
    per = n_blk // tn
    nn = n // tn

    def kern(g_ref, w_ref, o_ref, *acc):
        def finish(total):
            o_ref[...] = total

        part = lax.dot_general(g_ref[...], w_ref[...], (((1,), (1,)), ((), ())), preferred_element_type=F32)
        _accumulate(pl.program_id(2), nn, acc, part, finish)

    (out,), carried = _pallas(
        kern, name=name, grid=(k // tko, m // tm, nn),
        in_specs=[pl.BlockSpec((tm, tn), lambda j, i, jn: (i, jn)),
                  pl.BlockSpec((None, tko, tn), lambda j, i, jn: (jn // per, j, jn % per))],
        out_specs=[pl.BlockSpec((tm, tko), lambda j, i, jn: (i, j))],
        out_shape=[jax.ShapeDtypeStruct((m, k), F32)], operands=[g, w],
        scratch_shapes=[pltpu.VMEM((tm, tko), F32)] * (nn > 1), exchanges=exchanges, after=after)
    return out, carried


def _mm_tn(a, g, nb, *, name, exchanges=()):
    m, k = a.shape
    n = g.shape[1]
    n_blk = n // nb
    tko, tn, tm = _tile(k, 512, MXU), _tile(n_blk, 1536, MXU), _tile(m, 4096, MXU)
    per = n_blk // tn
    nm = m // tm

    def kern(a_ref, g_ref, o_ref, *acc):
        def finish(total):
            o_ref[...] = total.astype(o_ref.dtype)

        part = lax.dot_general(a_ref[...], g_ref[...], (((0,), (0,)), ((), ())), preferred_element_type=F32)
        _accumulate(pl.program_id(2), nm, acc, part, finish)

    (out,), carried = _pallas(
        kern, name=name, grid=(nb * per, k // tko, nm),
        in_specs=[pl.BlockSpec((tm, tko), lambda j, i, im: (im, i)),
                  pl.BlockSpec((tm, tn), lambda j, i, im: (im, j))],
        out_specs=[pl.BlockSpec((None, tko, tn), lambda j, i, im: (j // per, i, j % per))],
        out_shape=[jax.ShapeDtypeStruct((nb, k, n_blk), BF16)], operands=[a, g],
        scratch_shapes=[pltpu.VMEM((tko, tn), F32)] * (nm > 1), exchanges=exchanges)
    return out, carried


ROW_TILE = 128


def _rmsnorm_fwd(x, g, *, name):
    t, d = x.shape
    tr = _tile(t, ROW_TILE, SUBLANES)

    def kern(x_ref, g_ref, h_ref):
        xv = x_ref[...]
        r = lax.rsqrt(jnp.mean(xv * xv, axis=-1, keepdims=True) + EPS)
        h_ref[...] = (xv * r * g_ref[...]).astype(BF16)

    return pl.pallas_call(
        kern, name=name, grid=(t // tr,),
        in_specs=[pl.BlockSpec((tr, d), lambda i: (i, 0)), pl.BlockSpec((1, d), lambda i: (0, 0))],
        out_specs=pl.BlockSpec((tr, d), lambda i: (i, 0)),
        out_shape=jax.ShapeDtypeStruct((t, d), BF16), compiler_params=_params(1))(x, g)


def _rmsnorm_bwd(x, g, d_h_parts, d_res, *, name):
    t, d = x.shape
    tr = _tile(t, ROW_TILE, SUBLANES)
    n_parts = len(d_h_parts)

    def kern(x_ref, g_ref, *refs):
        dh_refs, (dres_ref, dx_ref, dxb_ref, dg_ref) = refs[:n_parts], refs[n_parts:]
        xv = x_ref[...]
        r = lax.rsqrt(jnp.mean(xv * xv, axis=-1, keepdims=True) + EPS)
        dh = jnp.concatenate([ref[...] for ref in dh_refs], axis=1)
        gy = dh * g_ref[...]
        dx = dres_ref[...] + r * gy - xv * (r * r * r) * jnp.mean(gy * xv, axis=-1, keepdims=True)
        dx_ref[...] = dx
        dxb_ref[...] = dx.astype(BF16)

        @pl.when(pl.program_id(0) == 0)
        def _():
            dg_ref[...] = jnp.zeros_like(dg_ref)

        dg_ref[...] += jnp.sum(dh * (xv * r), axis=0, keepdims=True)

    row = pl.BlockSpec((tr, d), lambda i: (i, 0))
    vec = pl.BlockSpec((1, d), lambda i: (0, 0))
    part = pl.BlockSpec((tr, d // n_parts), lambda i: (i, 0))
    return pl.pallas_call(
        kern, name=name, grid=(t // tr,), in_specs=[row, vec] + [part] * n_parts + [row], out_specs=[row, row, vec],
        out_shape=[jax.ShapeDtypeStruct((t, d), F32), jax.ShapeDtypeStruct((t, d), BF16),
                   jax.ShapeDtypeStruct((1, d), F32)], compiler_params=_params(1))(x, g, *d_h_parts, d_res)


def _loss_head(x, g, target, *, name):
    t, d = x.shape
    tr = _tile(t, ROW_TILE, SUBLANES)

    def kern(x_ref, g_ref, t_ref, dx_ref, dxb_ref, dg_ref, loss_ref):
        xv = x_ref[...]
        gv = g_ref[...]
        r = lax.rsqrt(jnp.mean(xv * xv, axis=-1, keepdims=True) + EPS)
        diff = xv * r * gv - t_ref[...]
        dy = diff * (1.0 / d)
        gy = dy * gv
        dx = r * gy - xv * (r * r * r) * jnp.mean(gy * xv, axis=-1, keepdims=True)
        dx_ref[...] = dx
        dxb_ref[...] = dx.astype(BF16)

        @pl.when(pl.program_id(0) == 0)
        def _():
            dg_ref[...] = jnp.zeros_like(dg_ref)
            loss_ref[...] = jnp.zeros_like(loss_ref)

        dg_ref[...] += jnp.sum(dy * (xv * r), axis=0, keepdims=True)
        part = 0.5 * jnp.sum(jnp.mean(diff * diff, axis=-1, keepdims=True), axis=0, keepdims=True)
        loss_ref[...] += jnp.broadcast_to(part, loss_ref.shape)

    row = pl.BlockSpec((tr, d), lambda i: (i, 0))
    vec = pl.BlockSpec((1, d), lambda i: (0, 0))
    return pl.pallas_call(
        kern, name=name, grid=(t // tr,), in_specs=[row, vec, row],
        out_specs=[row, row, vec, pl.BlockSpec((1, LANES), lambda i: (0, 0))],
        out_shape=[jax.ShapeDtypeStruct((t, d), F32), jax.ShapeDtypeStruct((t, d), BF16),
                   jax.ShapeDtypeStruct((1, d), F32), jax.ShapeDtypeStruct((1, LANES), F32)],
        compiler_params=_params(1))(x, g, target)


def _merge_fwd(m_a, m_b, proj, gate_col, *, name):
    t, d = m_a.shape
    tr = _tile(t, ROW_TILE, SUBLANES)

    def kern(ma_ref, mb_ref, ga_ref, gb_ref, o_ref):
        o_ref[...] = (_sigmoid(ga_ref[...]) * ma_ref[...] + _sigmoid(gb_ref[...]) * mb_ref[...]).astype(BF16)

    row = pl.BlockSpec((tr, d), lambda i: (i, 0))
    return pl.pallas_call(
        kern, name=name, grid=(t // tr,),
        in_specs=[row, row, pl.BlockSpec((tr, d), lambda i: (i, gate_col)),
                  pl.BlockSpec((tr, d), lambda i: (i, gate_col + 1))],
        out_specs=row, out_shape=jax.ShapeDtypeStruct((t, d), BF16), compiler_params=_params(1))(m_a, m_b, proj, proj)


def _merge_bwd(d_merged, m_a, m_b, proj, gate_col, *, name, after=None):
    t, d = m_a.shape
    tr = _tile(t, ROW_TILE, SUBLANES)

    def kern(dm_ref, ma_ref, mb_ref, ga_ref, gb_ref, dma_ref, dmb_ref, dg_ref):
        dm = dm_ref[...]
        sa = _sigmoid(ga_ref[...])
        sb = _sigmoid(gb_ref[...])
        dma_ref[...] = (dm * sa).astype(BF16)
        dmb_ref[...] = (dm * sb).astype(BF16)
        dg_ref[:, 0:d] = (dm * ma_ref[...] * (sa * (1.0 - sa))).astype(BF16)
        dg_ref[:, d:2 * d] = (dm * mb_ref[...] * (sb * (1.0 - sb))).astype(BF16)

    row = pl.BlockSpec((tr, d), lambda i: (i, 0))
    res, _ = _pallas(
        kern, name=name, grid=(t // tr,),
        in_specs=[row, row, row, pl.BlockSpec((tr, d), lambda i: (i, gate_col)),
                  pl.BlockSpec((tr, d), lambda i: (i, gate_col + 1))],
        out_specs=[row, row, pl.BlockSpec((tr, 2 * d), lambda i: (i, 0))],
        out_shape=[jax.ShapeDtypeStruct((t, d), BF16), jax.ShapeDtypeStruct((t, d), BF16),
                   jax.ShapeDtypeStruct((t, 2 * d), BF16)], operands=[d_merged, m_a, m_b, proj, proj], after=after)
    return res


def _tril_bf16(w, transposed):
    row = lax.broadcasted_iota(jnp.int32, w.shape, 0)
    col = lax.broadcasted_iota(jnp.int32, w.shape, 1)
    keep = (row <= col) if transposed else (row >= col)
    return jnp.where(keep, w, 0.0).astype(BF16)


def _layernorm_stats(v):
    mu = jnp.mean(v, axis=-1, keepdims=True)
    vc = v - mu
    rstd = lax.rsqrt(jnp.mean(vc * vc, axis=-1, keepdims=True) + EPS)
    return vc * rstd, rstd


def _mixer_a_fwd(proj, ln_g, ln_b, sg_w, sg_b_t, sgw, *, name):
    t = proj.shape[0]
    gd = sgw // SG_GROUPS

    def kern(zu_ref, zv_ref, g_ref, b_ref, w_ref, bt_ref, o_ref):
        xhat, _ = _layernorm_stats(_gelu(zv_ref[...]))
        vn = (xhat * g_ref[...] + b_ref[...]).astype(BF16)
        for g in range(SG_GROUPS):
            cols = slice(g * gd, (g + 1) * gd)
            mixed = jnp.dot(_tril_bf16(w_ref[g], False), vn[:, cols], preferred_element_type=F32) + bt_ref[:, g:g + 1]
            o_ref[:, cols] = (_gelu(zu_ref[:, cols]) * mixed).astype(BF16)

    vec = pl.BlockSpec((1, sgw), lambda i: (0, 0))
    return pl.pallas_call(
        kern, name=name, grid=(t // CHUNK,),
        in_specs=[pl.BlockSpec((CHUNK, sgw), lambda i: (i, 0)), pl.BlockSpec((CHUNK, sgw), lambda i: (i, 1)), vec, vec,
                  pl.BlockSpec((SG_GROUPS, CHUNK, CHUNK), lambda i: (0, 0, 0)),
                  pl.BlockSpec((CHUNK, SG_GROUPS), lambda i: (0, 0))],
        out_specs=pl.BlockSpec((CHUNK, sgw), lambda i: (i, 0)),
        out_shape=jax.ShapeDtypeStruct((t, sgw), BF16), compiler_params=_params(1))(proj, proj, ln_g, ln_b, sg_w, sg_b_t)


def _mixer_a_bwd(proj, d_ya, ln_g, ln_b, sg_w, sg_w_t, sg_b_t, sgw, *, name):
    t = proj.shape[0]
    gd = sgw // SG_GROUPS

    def kern(zu_ref, zv_ref, dy_ref, g_ref, b_ref, w_ref, wt_ref, bt_ref, dz_ref, dw_ref, dbt_ref, dg_ref, db_ref, dvn):
        @pl.when(pl.program_id(0) == 0)
        def _():
            dw_ref[...] = jnp.zeros_like(dw_ref)
            dbt_ref[...] = jnp.zeros_like(dbt_ref)
            dg_ref[...] = jnp.zeros_like(dg_ref)
            db_ref[...] = jnp.zeros_like(db_ref)

        gv, dgv = _gelu_and_grad(zv_ref[...])
        xhat, rstd = _layernorm_stats(gv)
        ln_gain = g_ref[...]
        vn = (xhat * ln_gain + b_ref[...]).astype(BF16)
        for g in range(SG_GROUPS):
            cols = slice(g * gd, (g + 1) * gd)
            gu, dgu = _gelu_and_grad(zu_ref[:, cols])
            mixed = jnp.dot(_tril_bf16(w_ref[g], False), vn[:, cols], preferred_element_type=F32) + bt_ref[:, g:g + 1]
            dy = dy_ref[:, cols]
            dz_ref[:, cols] = (dy * mixed * dgu).astype(BF16)
            d_mixed = dy * gu
            d_mixed_b = d_mixed.astype(BF16)
            dvn[:, cols] = jnp.dot(_tril_bf16(wt_ref[g], True), d_mixed_b, preferred_element_type=F32)
            d_w = lax.dot_general(d_mixed_b, vn[:, cols], (((1,), (1,)), ((), ())), preferred_element_type=F32)
            row = lax.broadcasted_iota(jnp.int32, d_w.shape, 0)
            col = lax.broadcasted_iota(jnp.int32, d_w.shape, 1)
            dw_ref[g] += jnp.where(row >= col, d_w, 0.0)
            dbt_ref[:, g:g + 1] += jnp.sum(d_mixed, axis=-1, keepdims=True)
        d_vn = dvn[...]
        dg_ref[...] += jnp.sum(d_vn * xhat, axis=0, keepdims=True)
        db_ref[...] += jnp.sum(d_vn, axis=0, keepdims=True)
        d_xhat = d_vn * ln_gain
        d_gv = rstd * (d_xhat - jnp.mean(d_xhat, axis=-1, keepdims=True)
                       - xhat * jnp.mean(d_xhat * xhat, axis=-1, keepdims=True))
        dz_ref[:, sgw:2 * sgw] = (d_gv * dgv).astype(BF16)

    vec = pl.BlockSpec((1, sgw), lambda i: (0, 0))
    wspec = pl.BlockSpec((SG_GROUPS, CHUNK, CHUNK), lambda i: (0, 0, 0))
    btspec = pl.BlockSpec((CHUNK, SG_GROUPS), lambda i: (0, 0))
    return pl.pallas_call(
        kern, name=name, grid=(t // CHUNK,),
        in_specs=[pl.BlockSpec((CHUNK, sgw), lambda i: (i, 0)), pl.BlockSpec((CHUNK, sgw), lambda i: (i, 1)),
                  pl.BlockSpec((CHUNK, sgw), lambda i: (i, 0)), vec, vec, wspec, wspec, btspec],
        out_specs=[pl.BlockSpec((CHUNK, 2 * sgw), lambda i: (i, 0)), wspec, btspec, vec, vec],
        out_shape=[jax.ShapeDtypeStruct((t, 2 * sgw), BF16), jax.ShapeDtypeStruct((SG_GROUPS, CHUNK, CHUNK), F32),
                   jax.ShapeDtypeStruct((CHUNK, SG_GROUPS), F32), jax.ShapeDtypeStruct((1, sgw), F32),
                   jax.ShapeDtypeStruct((1, sgw), F32)],
        scratch_shapes=[pltpu.VMEM((CHUNK, sgw), F32)],
        compiler_params=_params(1))(proj, proj, d_ya, ln_g, ln_b, sg_w, sg_w_t, sg_b_t)


def _scan_rows(a_ref, h_ref, reverse):
    s, c = a_ref.shape
    nblk = s // SUBLANES
    a, b = a_ref[...], h_ref[...]
    row = jnp.bitwise_and(lax.broadcasted_iota(jnp.int32, (s, c), 0), SUBLANES - 1)
    for d in (1, 2, 4):
        inside = (row < SUBLANES - d) if reverse else (row >= d)
        shift = s - d if reverse else d
        b = a * jnp.where(inside, pltpu.roll(b, shift, 0), 0.0) + b
        a = a * jnp.where(inside, pltpu.roll(a, shift, 0), 1.0)
    a_ref[...] = a
    h_ref[...] = b
    leaving = 0 if reverse else SUBLANES - 1

    def chain(i, carry):
        r0 = pl.multiple_of((nblk - 1 - i if reverse else i) * SUBLANES, SUBLANES)
        h = a_ref[pl.ds(r0, SUBLANES), :] * carry + h_ref[pl.ds(r0, SUBLANES), :]
        h_ref[pl.ds(r0, SUBLANES), :] = h
        return jnp.broadcast_to(h[leaving:leaving + 1, :], (SUBLANES, c))

    lax.fori_loop(0, nblk, chain, jnp.zeros((SUBLANES, c), F32))


def _lru_gates(xc, wa_ref, ba_ref, wx_ref, bx_ref, lam_ref):
    xcb = xc.astype(BF16)
    ra = _sigmoid(jnp.dot(xcb, wa_ref[...].astype(BF16), preferred_element_type=F32) + ba_ref[...])
    ia = _sigmoid(jnp.dot(xcb, wx_ref[...].astype(BF16), preferred_element_type=F32) + bx_ref[...])
    neg = -lam_ref[...]
    sp = jnp.maximum(neg, 0.0) + jnp.log1p(jnp.exp(-jnp.abs(neg)))
    log_a = -LRU_C * ra * sp
    a = jnp.exp(log_a)
    a2 = jnp.exp(2.0 * log_a)
    sq = jnp.sqrt(-jnp.tanh(log_a) * (a2 + 1.0))
    return ra, ia, sp, a, a2, sq


def _mixer_b_specs(seq, hd, sgw, lw):
    x_col = (2 * sgw) // hd
    y_col = (2 * sgw + lw) // hd
    tile = lambda col: pl.BlockSpec((seq, hd), lambda h, b: (b, col + h))
    vec = pl.BlockSpec((1, hd), lambda h, b: (0, h))
    mat = pl.BlockSpec((None, hd, hd), lambda h, b: (h, 0, 0))
    return tile(x_col), tile(y_col), tile(0), vec, mat


def _mixer_b_fwd(proj, conv_w, conv_b, wa, ba, wx, bx, lam, *, seq, sgw, lw, name):
    t = proj.shape[0]
    hd = lw // LRU_HEADS
    k_taps = conv_w.shape[0]
    x_spec, y_spec, o_spec, vec, mat = _mixer_b_specs(seq, hd, sgw, lw)

    def kern(xr_ref, yr_ref, cw_ref, cb_ref, wa_ref, ba_ref, wx_ref, bx_ref, lam_ref, o_ref, s_a, s_h):
        xc = _causal_conv(xr_ref[...], cw_ref[...], cb_ref[...])
        _, ia, _, a, _, sq = _lru_gates(xc, wa_ref, ba_ref, wx_ref, bx_ref, lam_ref)
        s_a[...] = a
        s_h[...] = sq * (ia * xc)
        _scan_rows(s_a, s_h, False)
        o_ref[...] = (s_h[...] * _gelu(yr_ref[...])).astype(BF16)

    return pl.pallas_call(
        kern, name=name, grid=(LRU_HEADS, t // seq),
        in_specs=[x_spec, y_spec, pl.BlockSpec((k_taps, hd), lambda h, b: (0, h)), vec, mat, vec, mat, vec, vec],
        out_specs=o_spec, out_shape=jax.ShapeDtypeStruct((t, lw), BF16),
        scratch_shapes=[pltpu.VMEM((seq, hd), F32), pltpu.VMEM((seq, hd), F32)],
        compiler_params=_params(2))(proj, proj, conv_w, conv_b, wa, ba, wx, bx, lam)


def _mixer_b_bwd(proj, d_yb, conv_w, conv_b, wa, wa_t, ba, wx, wx_t, bx, lam, *, seq, sgw, lw, name, exchanges=()):
    t = proj.shape[0]
    hd = lw // LRU_HEADS
    k_taps = conv_w.shape[0]
    x_spec, y_spec, o_spec, vec, mat = _mixer_b_specs(seq, hd, sgw, lw)
    cw_spec = pl.BlockSpec((k_taps, hd), lambda h, b: (0, h))

    def kern(xr_ref, yr_ref, dyb_ref, cw_ref, cb_ref, wa_ref, wat_ref, ba_ref, wx_ref, wxt_ref, bx_ref, lam_ref,
             dxr_ref, dyr_ref, dcw_ref, dcb_ref, dwa_ref, dba_ref, dwx_ref, dbx_ref, dlam_ref,
             s_xc, s_a, s_h, s_lam, s_dpa, s_dpx):
        @pl.when(pl.program_id(1) == 0)
        def _():
            for ref in (dcw_ref, dcb_ref, dwa_ref, dba_ref, dwx_ref, dbx_ref, dlam_ref):
                ref[...] = jnp.zeros_like(ref)

        s_xc[...] = _causal_conv(xr_ref[...], cw_ref[...], cb_ref[...])
        _, ia, _, a, _, sq = _lru_gates(s_xc[...], wa_ref, ba_ref, wx_ref, bx_ref, lam_ref)
        s_a[...] = a
        s_dpa[...] = _shift_up(a, 1)
        s_h[...] = sq * (ia * s_xc[...])
        _scan_rows(s_a, s_h, False)

        gel, dgel = _gelu_and_grad(yr_ref[...])
        dyb = dyb_ref[...]
        dyr_ref[...] = (dyb * s_h[...] * dgel).astype(BF16)
        s_lam[...] = dyb * gel
        _scan_rows(s_dpa, s_lam, True)
        ra, ia, sp, a, a2, sq = _lru_gates(s_xc[...], wa_ref, ba_ref, wx_ref, bx_ref, lam_ref)
        d_gx = s_lam[...]
        d_a = d_gx * _shift_down(s_h[...], 1)
        xc = s_xc[...]
        d_sq = d_gx * (ia * xc)
        d_ia = d_gx * (sq * xc)
        d_log_a = d_a * a - d_sq * (a2 / sq)
        d_ra = d_log_a * (-LRU_C * sp)
        d_sp = jnp.sum(d_log_a * (-LRU_C * ra), axis=0, keepdims=True)
        dlam_ref[...] += d_sp * (-_sigmoid(-lam_ref[...]))
        d_pa = d_ra * (ra * (1.0 - ra))
        d_px = d_ia * (ia * (1.0 - ia))
        s_dpa[...] = d_pa
        s_dpx[...] = d_px
        dba_ref[...] += jnp.sum(d_pa, axis=0, keepdims=True)
        dbx_ref[...] += jnp.sum(d_px, axis=0, keepdims=True)
        xcb = s_xc[...].astype(BF16)
        d_pa_b = s_dpa[...].astype(BF16)
        d_px_b = s_dpx[...].astype(BF16)
        contract_rows = (((0,), (0,)), ((), ()))
        dwa_ref[...] += lax.dot_general(xcb, d_pa_b, contract_rows, preferred_element_type=F32)
        dwx_ref[...] += lax.dot_general(xcb, d_px_b, contract_rows, preferred_element_type=F32)
        d_xc = (s_lam[...] * (sq * ia)
                + jnp.dot(d_pa_b, wat_ref[...].astype(BF16), preferred_element_type=F32)
                + jnp.dot(d_px_b, wxt_ref[...].astype(BF16), preferred_element_type=F32))
        dcb_ref[...] += jnp.sum(d_xc, axis=0, keepdims=True)
        dcw_ref[...] += _causal_conv_bwd_w(d_xc, xr_ref[...], k_taps)
        dxr_ref[...] = _causal_conv_bwd_x(d_xc, cw_ref[...]).astype(BF16)

    tile_shape = jax.ShapeDtypeStruct((t, lw), BF16)
    vec_shape = jax.ShapeDtypeStruct((1, lw), F32)
    mat_shape = jax.ShapeDtypeStruct((LRU_HEADS, hd, hd), F32)
    return _pallas(
        kern, name=name, grid=(LRU_HEADS, t // seq),
        in_specs=[x_spec, y_spec, o_spec, cw_spec, vec, mat, mat, vec, mat, mat, vec, vec],
        out_specs=[o_spec, o_spec, cw_spec, vec, mat, vec, mat, vec, vec],
        out_shape=[tile_shape, tile_shape, jax.ShapeDtypeStruct((k_taps, lw), F32), vec_shape, mat_shape, vec_shape,
                   mat_shape, vec_shape, vec_shape],
        operands=[proj, proj, d_yb, conv_w, conv_b, wa, wa_t, ba, wx, wx_t, bx, lam],
        scratch_shapes=[pltpu.VMEM((seq, hd), F32)] * 6, exchanges=exchanges)


FFN_TILE = 256


def _ffn_mid_fwd(up_pre, conv_w, conv_b, *, seq, name):
    t, f2 = up_pre.shape
    f = f2 // 2
    tc = _tile(f, FFN_TILE, LANES)
    nf = f // tc
    k_taps = conv_w.shape[0]

    def kern(pg_ref, pv_ref, wg_ref, wv_ref, bg_ref, bv_ref, o_ref):
        cg = _causal_conv(pg_ref[...], wg_ref[...], bg_ref[...])
        cv = _causal_conv(pv_ref[...], wv_ref[...], bv_ref[...])
        o_ref[...] = (_gelu(cg) * cv).astype(BF16)

    tile = lambda off: pl.BlockSpec((seq, tc), lambda j, b: (b, off + j))
    wspec = lambda off: pl.BlockSpec((k_taps, tc), lambda j, b: (0, off + j))
    bspec = lambda off: pl.BlockSpec((1, tc), lambda j, b: (0, off + j))
    return pl.pallas_call(
        kern, name=name, grid=(nf, t // seq),
        in_specs=[tile(0), tile(nf), wspec(0), wspec(nf), bspec(0), bspec(nf)], out_specs=tile(0),
        out_shape=jax.ShapeDtypeStruct((t, f), BF16),
        compiler_params=_params(2))(up_pre, up_pre, conv_w, conv_w, conv_b, conv_b)


def _ffn_mid_bwd(up_pre, d_act, conv_w, conv_b, *, seq, name):
    t, f2 = up_pre.shape
    f = f2 // 2
    tc = _tile(f, FFN_TILE, LANES)
    nf = f // tc
    k_taps = conv_w.shape[0]

    def kern(pg_ref, pv_ref, da_ref, wg_ref, wv_ref, bg_ref, bv_ref, dpg_ref, dpv_ref, dwg_ref, dwv_ref, dbg_ref, dbv_ref):
        @pl.when(pl.program_id(1) == 0)
        def _():
            for ref in (dwg_ref, dwv_ref, dbg_ref, dbv_ref):
                ref[...] = jnp.zeros_like(ref)

        pg = pg_ref[...]
        pv = pv_ref[...]
        gel, dgel = _gelu_and_grad(_causal_conv(pg, wg_ref[...], bg_ref[...]))
        cv = _causal_conv(pv, wv_ref[...], bv_ref[...])
        d_act_v = da_ref[...]
        d_cg = d_act_v * cv * dgel
        d_cv = d_act_v * gel
        dpg_ref[...] = _causal_conv_bwd_x(d_cg, wg_ref[...]).astype(BF16)
        dpv_ref[...] = _causal_conv_bwd_x(d_cv, wv_ref[...]).astype(BF16)
        dwg_ref[...] += _causal_conv_bwd_w(d_cg, pg, k_taps)
        dwv_ref[...] += _causal_conv_bwd_w(d_cv, pv, k_taps)
        dbg_ref[...] += jnp.sum(d_cg, axis=0, keepdims=True)
        dbv_ref[...] += jnp.sum(d_cv, axis=0, keepdims=True)

    tile = lambda off: pl.BlockSpec((seq, tc), lambda j, b: (b, off + j))
    wspec = lambda off: pl.BlockSpec((k_taps, tc), lambda j, b: (0, off + j))
    bspec = lambda off: pl.BlockSpec((1, tc), lambda j, b: (0, off + j))
    half = jax.ShapeDtypeStruct((t, f), BF16)
    wshape = jax.ShapeDtypeStruct((k_taps, f), F32)
    bshape = jax.ShapeDtypeStruct((1, f), F32)
    d_pg, d_pv, d_wg, d_wv, d_bg, d_bv = pl.pallas_call(
        kern, name=name, grid=(nf, t // seq),
        in_specs=[tile(0), tile(nf), tile(0), wspec(0), wspec(nf), bspec(0), bspec(nf)],
        out_specs=[tile(0), tile(0), wspec(0), wspec(0), bspec(0), bspec(0)],
        out_shape=[half, half, wshape, wshape, bshape, bshape],
        compiler_params=_params(2))(up_pre, up_pre, d_act, conv_w, conv_w, conv_b, conv_b)
    return (jnp.concatenate([d_pg, d_pv], axis=1), jnp.concatenate([d_wg, d_wv], axis=1),
            jnp.concatenate([d_bg, d_bv], axis=1))


ELEM_VMEM_BYTES = 24 << 20


def _as_2d(a):
    if a.ndim >= 2 and a.shape[-1] % LANES == 0 and a.size // a.shape[-1] >= SUBLANES:
        return a.reshape(-1, a.shape[-1])
    return a.reshape(-1, LANES)


def _row_tile(rows, bytes_per_row):
    return _tile(rows, max(16, ELEM_VMEM_BYTES // (2 * bytes_per_row)), 16)


def _cast_bf16(a, *, name, part=0, n_parts=1):
    v = _as_2d(a)
    rows, cols = v.shape[0] // n_parts, v.shape[1]
    tr = _row_tile(rows, cols * (4 + 2))
    first = part * (rows // tr)

    def kern(x_ref, o_ref):
        o_ref[...] = x_ref[...].astype(BF16)

    out = pl.pallas_call(kern, name=name, grid=(rows // tr,), in_specs=[pl.BlockSpec((tr, cols), lambda i: (first + i, 0))],
                         out_specs=pl.BlockSpec((tr, cols), lambda i: (i, 0)),
                         out_shape=jax.ShapeDtypeStruct((rows, cols), BF16), compiler_params=_params(1))(v)
    return out.reshape(a.shape) if n_parts == 1 else out


def _add_sibling_part(own, core, got, *, name):
    _, _, rows, cols = own.shape
    tr = _row_tile(rows, cols * (2 + 2 + 2))

    def kern(core_ref, a_ref, b_ref, o_ref):
        o_ref[...] = (a_ref[...].astype(F32) + b_ref[...].astype(F32)).astype(BF16)

    spec = pl.BlockSpec((None, tr, cols), lambda ch, i, core_ref: (ch, i, 0))
    grid_spec = pltpu.PrefetchScalarGridSpec(
        num_scalar_prefetch=1, grid=(4, rows // tr),
        in_specs=[pl.BlockSpec((None, None, tr, cols), lambda ch, i, core_ref: (ch, core_ref[0], i, 0)), spec],
        out_specs=spec)
    return pl.pallas_call(kern, name=name, grid_spec=grid_spec, out_shape=jax.ShapeDtypeStruct(got.shape, BF16),
                          compiler_params=_params(2))(core, own, got)


def _sum_parts(parts, *, name):
    n_parts, rows, cols = parts.shape
    tr = _row_tile(rows, cols * 4 * (n_parts + 1))

    def kern(p_ref, o_ref):
        acc = p_ref[0].astype(F32)
        for p in range(1, n_parts):
            acc = acc + p_ref[p].astype(F32)
        o_ref[...] = acc

    return pl.pallas_call(
        kern, name=name, grid=(rows // tr,), in_specs=[pl.BlockSpec((n_parts, tr, cols), lambda i: (0, i, 0))],
        out_specs=pl.BlockSpec((tr, cols), lambda i: (i, 0)), out_shape=jax.ShapeDtypeStruct((rows, cols), F32),
        compiler_params=_params(1))(parts)


def _adamw(w, m, v, grad_parts, *, name):
    shape = w.shape
    w2 = _as_2d(w)
    rows, cols = w2.shape
    n_parts = grad_parts.shape[0]
    parts = grad_parts.reshape(n_parts, rows, cols)
    tr = _row_tile(rows, cols * (3 * 4 + n_parts * parts.dtype.itemsize + 4 * 4))
    c_m = 1.0 - ADAM_B1 ** ADAM_STEP
    c_v = 1.0 - ADAM_B2 ** ADAM_STEP

    def kern(w_ref, m_ref, v_ref, p_ref, g_ref, d_ref, nm_ref, nv_ref):
        g = p_ref[0].astype(F32)
        for p in range(1, n_parts):
            g = g + p_ref[p].astype(F32)
        new_m = ADAM_B1 * m_ref[...] + (1.0 - ADAM_B1) * g
        new_v = ADAM_B2 * v_ref[...] + (1.0 - ADAM_B2) * (g * g)
        g_ref[...] = g
        nm_ref[...] = new_m
        nv_ref[...] = new_v
        d_ref[...] = -ADAM_LR * ((new_m / c_m) / (jnp.sqrt(new_v / c_v) + ADAM_EPS) + ADAM_WD * w_ref[...])

    spec = pl.BlockSpec((tr, cols), lambda i: (i, 0))
    out = jax.ShapeDtypeStruct((rows, cols), F32)
    res = pl.pallas_call(
        kern, name=name, grid=(rows // tr,),
        in_specs=[spec, spec, spec, pl.BlockSpec((n_parts, tr, cols), lambda i: (0, i, 0))],
        out_specs=[spec] * 4, out_shape=[out] * 4, compiler_params=_params(1))(w2, _as_2d(m), _as_2d(v), parts)
    return [r.reshape(shape) for r in res]


def _all_gather(shards, *, name):
    n = len(shards)

    def body(*refs):
        ins, outs = refs[:n], refs[n:2 * n]
        send_sems, recv_sems, local_sems = refs[2 * n:]
        x, y, c = _place()
        me, sibling = (x, y, c), (x, y, 1 - c)
        chips = [(1 - x, y), (x, 1 - y), (1 - x, 1 - y)]

        def slot(i, dev):
            return outs[i].at[4 * dev[0] + 2 * dev[1] + dev[2]]

        def copy(i, k, block, to, src=None):
            return pltpu.make_async_remote_copy(
                src_ref=slot(i, block) if src is None else src, dst_ref=slot(i, block),
                send_sem=send_sems.at[i, k], recv_sem=recv_sems.at[i, k], device_id=to, device_id_type=MESH)

        mine = [pltpu.make_async_copy(ins[i], slot(i, me), local_sems.at[i]) for i in range(n)]
        for cp in mine:
            cp.start()
        first = []
        for i in range(n):
            first.append(copy(i, 0, me, sibling, src=ins[i]))
            first += [copy(i, 1 + j, me, (*chip, c), src=ins[i]) for j, chip in enumerate(chips)]
        for cp in first:
            cp.start()
        passed = []
        for j, chip in enumerate(chips):
            for i in range(n):
                copy(i, 1 + j, (*chip, c), me).wait_recv()
                onward = copy(i, 4 + j, (*chip, c), sibling)
                onward.start()
                passed.append(onward)
        for i in range(n):
            copy(i, 0, sibling, me).wait_recv()
            for j, chip in enumerate(chips):
                copy(i, 4 + j, (*chip, 1 - c), me).wait_recv()
        for cp in first + passed:
            cp.wait_send()
        for cp in mine:
            cp.wait()

    return pl.pallas_call(
        body, name=name, in_specs=[ANY] * n, out_specs=[ANY] * n,
        out_shape=[jax.ShapeDtypeStruct((N_DEV,) + s.shape, s.dtype) for s in shards],
        scratch_shapes=[pltpu.SemaphoreType.DMA((n, 7)), pltpu.SemaphoreType.DMA((n, 7)), pltpu.SemaphoreType.DMA((n,))],
    )(*shards)


def _by_chip_and_core(grad):
    return grad.reshape(4, 2, -1, grad.shape[-1])


def _pack(vectors):
    flat = [v.reshape(-1).astype(F32) for v in vectors]
    sizes = [f.shape[0] for f in flat]
    total = sum(sizes)
    padded = -(-total // (SUBLANES * LANES)) * (SUBLANES * LANES)
    if padded > total:
        flat.append(jnp.zeros((padded - total,), F32))
    return jnp.concatenate(flat).reshape(-1, LANES), sizes


def _unpack(packed, sizes, shapes):
    flat = packed.reshape(-1)
    out, off = [], 0
    for size, shape in zip(sizes, shapes):
        out.append(flat[off:off + size].reshape(shape))
        off += size
    return out


def kernel(x, g_mix, w_in, sg_ln_g, sg_ln_b, sg_w, sg_b, lru_conv_w, lru_conv_b, lru_wa, lru_ba, lru_wx, lru_bx, lru_lam, p_sg, p_lru, w_out, g_ffn, w_up, ffn_conv_w, ffn_conv_b, w_down, g_final, loss_target, m_g_mix, m_w_in, m_sg_ln_g, m_sg_ln_b, m_sg_w, m_sg_b, m_lru_conv_w, m_lru_conv_b, m_lru_wa, m_lru_ba, m_lru_wx, m_lru_bx, m_lru_lam, m_p_sg, m_p_lru, m_w_out, m_g_ffn, m_w_up, m_ffn_conv_w, m_ffn_conv_b, m_w_down, m_g_final, v_g_mix, v_w_in, v_sg_ln_g, v_sg_ln_b, v_sg_w, v_sg_b, v_lru_conv_w, v_lru_conv_b, v_lru_wa, v_lru_ba, v_lru_wx, v_lru_bx, v_lru_lam, v_p_sg, v_p_lru, v_w_out, v_g_ffn, v_w_up, v_ffn_conv_w, v_ffn_conv_b, v_w_down, v_g_final):
    weights = dict(g_mix=g_mix, w_in=w_in, sg_ln_g=sg_ln_g, sg_ln_b=sg_ln_b, sg_w=sg_w, sg_b=sg_b, lru_conv_w=lru_conv_w,
                   lru_conv_b=lru_conv_b, lru_wa=lru_wa, lru_ba=lru_ba, lru_wx=lru_wx, lru_bx=lru_bx, lru_lam=lru_lam,
                   p_sg=p_sg, p_lru=p_lru, w_out=w_out, g_ffn=g_ffn, w_up=w_up, ffn_conv_w=ffn_conv_w,
                   ffn_conv_b=ffn_conv_b, w_down=w_down, g_final=g_final)
    m_in = dict(g_mix=m_g_mix, w_in=m_w_in, sg_ln_g=m_sg_ln_g, sg_ln_b=m_sg_ln_b, sg_w=m_sg_w, sg_b=m_sg_b,
                lru_conv_w=m_lru_conv_w, lru_conv_b=m_lru_conv_b, lru_wa=m_lru_wa, lru_ba=m_lru_ba, lru_wx=m_lru_wx,
                lru_bx=m_lru_bx, lru_lam=m_lru_lam, p_sg=m_p_sg, p_lru=m_p_lru, w_out=m_w_out, g_ffn=m_g_ffn,
                w_up=m_w_up, ffn_conv_w=m_ffn_conv_w, ffn_conv_b=m_ffn_conv_b, w_down=m_w_down, g_final=m_g_final)
    v_in = dict(g_mix=v_g_mix, w_in=v_w_in, sg_ln_g=v_sg_ln_g, sg_ln_b=v_sg_ln_b, sg_w=v_sg_w, sg_b=v_sg_b,
                lru_conv_w=v_lru_conv_w, lru_conv_b=v_lru_conv_b, lru_wa=v_lru_wa, lru_ba=v_lru_ba, lru_wx=v_lru_wx,
                lru_bx=v_lru_bx, lru_lam=v_lru_lam, p_sg=v_p_sg, p_lru=v_p_lru, w_out=v_w_out, g_ffn=v_g_ffn,
                w_up=v_w_up, ffn_conv_w=v_ffn_conv_w, ffn_conv_b=v_ffn_conv_b, w_down=v_w_down, g_final=v_g_final)
    order = list(weights)

    n_seq, seq, d = x.shape
    t = n_seq * seq
    sgw = sg_ln_g.shape[-1]
    lw = lru_lam.shape[-1]
    hd = lw // LRU_HEADS
    f2 = ffn_conv_b.shape[-1]
    gate_col = (2 * sgw + 2 * lw) // d
    xi, yi, ci = _place()
    dev = 4 * xi + 2 * yi + ci

    core = jnp.reshape(ci, (1,)).astype(jnp.int32)
    first_leg = functools.partial(_gather_first_leg, place_own=False)
    chip_swap = functools.partial(_swap_with_chips, place_own=False)
    shards, in_flight = {}, {}

    def landed(keys, after, name):
        return _await_exchange(first_leg, [in_flight[k] for k in keys], after, name=name)[1]

    needed_next = ["w_in_bottom", "lru_wa", "lru_wx", "p_sg", "taps"]
    by_need = ["w_in_top"] + needed_next + ["p_lru", "w_out", "w_up", "w_down"]
    shards["w_in_top"] = _cast_bf16(w_in[0], name="cast_w_in_top", part=0, n_parts=2)
    shards["w_in_bottom"] = _cast_bf16(w_in[0], name="cast_w_in_bottom", part=1, n_parts=2)
    shards.update({k: _cast_bf16(weights[k][0], name=f"cast_{k}") for k in by_need if k in weights})
    shards["taps"], tap_sizes = _pack([lru_conv_w[0], ffn_conv_w[0]])
    started, g_mix_after_start = _start_exchange(
        first_leg, [shards[k] for k in by_need], [_own_block_in_place(shards[k], dev) for k in by_need], g_mix,
        name="start_weight_gather")
    in_flight.update(zip(by_need, started))

    def rows_in_order(g8):
        return g8.reshape(1, -1, g8.shape[-1])

    x2d = x.reshape(t, d)
    h1 = _rmsnorm_fwd(x2d, g_mix_after_start, name="norm_mix")
    ((w_in_top_g,),) = _exchange_now([_gather_second_leg(landed(["w_in_top"], h1, "await_w_in_top"))],
                                     name="second_leg_w_in_top")
    proj_top, _ = _mm_nn(h1, w_in_top_g, out_dtype=F32, name="proj_in_top")
    ((w_in_bottom_g, wa_8, wx_8, p_sg_g, taps_8),) = _exchange_now(
        [_gather_second_leg(landed(needed_next, proj_top, "await_w_in_bottom"))], name="second_leg_w_in_bottom")
    proj, _ = _mm_nn(h1, w_in_bottom_g, out_dtype=F32, residual=proj_top, a_part=1, name="proj_in")
    wa_g, wx_g = (jnp.swapaxes(w8, 0, 1).reshape(LRU_HEADS, hd, hd) for w8 in (wa_8, wx_8))
    wa_t, wx_t = jnp.swapaxes(wa_g, 1, 2), jnp.swapaxes(wx_g, 1, 2)
    tap_parts = [_unpack(taps_8[k], tap_sizes, [lru_conv_w.shape[1:], ffn_conv_w.shape[1:]]) for k in range(N_DEV)]
    lru_cw = jnp.concatenate([p[0] for p in tap_parts], axis=1)
    ffn_cw = jnp.concatenate([p[1] for p in tap_parts], axis=1)
    sg_w0 = sg_w[0]
    sg_w_t = jnp.swapaxes(sg_w0, 1, 2)
    sg_b_t = sg_b[0].T
    y_a = _mixer_a_fwd(proj, sg_ln_g, sg_ln_b, sg_w0, sg_b_t, sgw, name="mixer_a_fwd")
    y_b = _mixer_b_fwd(proj, lru_cw, lru_conv_b, wa_g, lru_ba, wx_g, lru_bx, lru_lam, seq=seq, sgw=sgw, lw=lw,
                       name="mixer_b_fwd")
    m_a, ((p_lru_8, w_out_8),) = _mm_nn(
        y_a, p_sg_g, out_dtype=F32, name="proj_sg",
        exchanges=[_gather_second_leg(landed(["p_lru", "w_out"], y_b, "await_p_lru_w_out"))])
    p_lru_g, w_out_g = rows_in_order(p_lru_8), rows_in_order(w_out_8)
    m_b, _ = _mm_nn(y_b, p_lru_g, out_dtype=F32, name="proj_lru")
    merged = _merge_fwd(m_a, m_b, proj, gate_col, name="merge_fwd")
    x1, ((w_up_g,),) = _mm_nn(merged, w_out_g, out_dtype=F32, residual=x2d, name="proj_out",
                              exchanges=[_gather_second_leg(landed(["w_up"], merged, "await_w_up"))])
    h2 = _rmsnorm_fwd(x1, g_ffn, name="norm_ffn")
    up_pre, _ = _mm_nn(h2, w_up_g, out_dtype=F32, name="ffn_up")
    act = _ffn_mid_fwd(up_pre, ffn_cw, ffn_conv_b, seq=seq, name="ffn_mid_fwd")
    ((w_down_8,),) = _exchange_now([_gather_second_leg(landed(["w_down"], act, "await_w_down"))], name="second_leg_w_down")
    w_down_g = rows_in_order(w_down_8)
    x2, _ = _mm_nn(act, w_down_g, out_dtype=F32, residual=x1, name="ffn_down")
    d_x2, d_x2_b, d_g_final, loss_part = _loss_head(x2, g_final.reshape(1, d), loss_target.reshape(t, d), name="loss_head")
    loss = lax.psum(loss_part[0, 0], ("x", "y", "c"))

    def by_rows(g):
        return g.reshape(N_DEV, -1, g.shape[-1])

    def by_head_rows(g):
        return jnp.swapaxes(g.reshape(LRU_HEADS, N_DEV, hd // N_DEV, hd), 0, 1)

    def token():
        return jnp.zeros((SUBLANES, LANES), F32)

    def start_sibling_swap(views, key):
        return _start_exchange(_swap_with_sibling, views, [lax.empty((4,) + v.shape[2:], v.dtype) for v in views], token(),
                               name=f"start_sibling_{key}")

    chip_swaps = {}

    def sum_and_start_chip_swap(keys, sibling_swap, done, ride=None):
        views, from_sibling = _await_exchange(_swap_with_sibling, sibling_swap, done, name=f"await_sibling_{keys[0]}")
        sums = [_add_sibling_part(v, core, s, name=f"chip_sum_{k}") for k, v, s in zip(keys, views, from_sibling)]
        started, ride = _start_exchange(chip_swap, sums, [_own_block_first(s, 2 * xi + yi) for s in sums],
                                        token() if ride is None else ride, name=f"start_chips_{keys[0]}")
        chip_swaps.update(zip(keys, started))
        return ride

    d_w_down, _ = _mm_tn(act, d_x2_b, 1, name="grad_w_down")
    v_down = _by_chip_and_core(by_rows(d_w_down))
    swap, started = start_sibling_swap([v_down], "w_down")
    d_act, _ = _mm_nt(d_x2_b, w_down_g, name="bwd_ffn_down", after=started)
    ffn_cb = sum_and_start_chip_swap(["w_down"], swap, d_act, ride=ffn_conv_b)
    d_up_pre, d_ffn_cw, d_ffn_cb = _ffn_mid_bwd(up_pre, d_act, ffn_cw, ffn_cb, seq=seq, name="ffn_mid_bwd")
    d_w_up, _ = _mm_tn(h2, d_up_pre, N_DEV, name="grad_w_up")
    v_up = _by_chip_and_core(d_w_up)
    swap, started = start_sibling_swap([v_up], "w_up")
    d_h2, _ = _mm_nt(d_up_pre, w_up_g, name="bwd_ffn_up", after=started)
    g_ffn_then = sum_and_start_chip_swap(["w_up"], swap, d_h2, ride=g_ffn)
    d_x1, d_x1_b, d_g_ffn = _rmsnorm_bwd(x1, g_ffn_then, [d_h2], d_x2, name="norm_ffn_bwd")
    d_w_out, _ = _mm_tn(merged, d_x1_b, 1, name="grad_w_out")
    v_out = _by_chip_and_core(by_rows(d_w_out))
    swap, started = start_sibling_swap([v_out], "w_out")
    d_merged, _ = _mm_nt(d_x1_b, w_out_g, name="bwd_proj_out", after=started)
    started = sum_and_start_chip_swap(["w_out"], swap, d_merged)
    d_m_a, d_m_b, d_gates = _merge_bwd(d_merged, m_a, m_b, proj, gate_col, name="merge_bwd", after=started)
    d_p_sg, _ = _mm_tn(y_a, d_m_a, N_DEV, name="grad_p_sg")
    d_p_lru, _ = _mm_tn(y_b, d_m_b, 1, name="grad_p_lru")
    v_sg, v_lru = _by_chip_and_core(d_p_sg), _by_chip_and_core(by_rows(d_p_lru))
    swap, started = start_sibling_swap([v_sg, v_lru], "p_sg")
    d_y_a, _ = _mm_nt(d_m_a, p_sg_g, name="bwd_proj_sg", after=started)
    d_y_b, _ = _mm_nt(d_m_b, p_lru_g, name="bwd_proj_lru")
    lru_cb = sum_and_start_chip_swap(["p_sg", "p_lru"], swap, d_y_b, ride=lru_conv_b)
    d_zuv, d_sg_w, d_sg_b_t, d_ln_g, d_ln_b = _mixer_a_bwd(proj, d_y_a, sg_ln_g, sg_ln_b, sg_w0, sg_w_t, sg_b_t, sgw,
                                                           name="mixer_a_bwd")
    (d_xr, d_yr, d_lru_cw, d_lru_cb, d_wa, d_ba, d_wx, d_bx, d_lam), _ = _mixer_b_bwd(
        proj, d_y_b, lru_cw, lru_cb, wa_g, wa_t, lru_ba, wx_g, wx_t, lru_bx, lru_lam, seq=seq, sgw=sgw, lw=lw,
        name="mixer_b_bwd")
    d_proj = jnp.concatenate([d_zuv, d_xr, d_yr, d_gates], axis=1)
    d_w_in, _ = _mm_tn(h1, d_proj, N_DEV, name="grad_w_in")
    v_win = _by_chip_and_core(d_w_in)
    v_wa = _by_chip_and_core(_cast_bf16(by_head_rows(d_wa), name="cast_grad_wa"))
    v_wx = _by_chip_and_core(_cast_bf16(by_head_rows(d_wx), name="cast_grad_wx"))
    swap, started = start_sibling_swap([v_win, v_wa, v_wx], "w_in")
    started = sum_and_start_chip_swap(["w_in", "lru_wa", "lru_wx"], swap, started)
    d_h1_left, _ = _mm_nt(d_proj, w_in_top_g, name="bwd_proj_in_top", after=started)
    d_h1_right, _ = _mm_nt(d_proj, w_in_bottom_g, name="bwd_proj_in_bottom")
    grad_x, _, d_g_mix = _rmsnorm_bwd(x2d, g_mix, [d_h1_left, d_h1_right], d_x1, name="norm_mix_bwd")

    small = ["g_mix", "sg_ln_g", "sg_ln_b", "sg_w", "sg_b", "lru_conv_b", "lru_ba", "lru_bx", "lru_lam", "g_ffn",
             "ffn_conv_b", "g_final", "lru_conv_w", "ffn_conv_w"]
    small_parts = dict(g_mix=d_g_mix, sg_ln_g=d_ln_g, sg_ln_b=d_ln_b, sg_w=d_sg_w, sg_b=d_sg_b_t.T, lru_conv_b=d_lru_cb,
                       lru_ba=d_ba, lru_bx=d_bx, lru_lam=d_lam, g_ffn=d_g_ffn, ffn_conv_b=d_ffn_cb, g_final=d_g_final,
                       lru_conv_w=d_lru_cw, ffn_conv_w=d_ffn_cw)
    packed, sizes = _pack([small_parts[k] for k in small])
    (all_small,) = _all_gather([packed], name="gather_small_grads")
    small_sum = _sum_parts(all_small, name="sum_small_grads")
    small_grads = dict(zip(small, _unpack(small_sum, sizes, [small_parts[k].shape for k in small])))
    for k in ("lru_conv_w", "ffn_conv_w"):
        n_loc = weights[k].shape[-1]
        small_grads[k] = lax.dynamic_slice_in_dim(small_grads[k], dev * n_loc, n_loc, axis=1)

    grads, deltas, new_m, new_v = {}, {}, {}, {}

    def update(k, parts):
        grads[k], deltas[k], new_m[k], new_v[k] = _adamw(weights[k], m_in[k], v_in[k], parts, name=f"adamw_{k}")

    def update_when_landed(keys, after):
        _, parts = _await_exchange(chip_swap, [chip_swaps[k] for k in keys], after, name=f"await_chips_{keys[0]}")
        for k, p in zip(keys, parts):
            update(k, p)

    for keys in (["w_down"], ["w_up"], ["w_out"], ["p_sg", "p_lru"]):
        update_when_landed(keys, grad_x)
    for k in small:
        update(k, small_grads[k][None])
    update_when_landed(["w_in", "lru_wa", "lru_wx"], deltas["w_up"])

    return (loss, grad_x.reshape(x.shape), *[grads[k] for k in order], *[deltas[k] for k in order],
            *[new_m[k] for k in order], *[new_v[k] for k in order])
```

```python
import functools
import math
from typing import Callable, NamedTuple

import jax
import jax.numpy as jnp
from jax import lax
from jax.experimental import pallas as pl
from jax.experimental.pallas import tpu as pltpu

F32 = jnp.float32
BF16 = jnp.bfloat16
MESH = pl.DeviceIdType.MESH
ANY = pl.BlockSpec(memory_space=pl.ANY)

N_DEV = 8
EPS = 1e-6
CHUNK = 128
SG_GROUPS = 8
LRU_HEADS = 16
LRU_C = 8.0
ADAM_LR = 0.001
ADAM_B1 = 0.9
ADAM_B2 = 0.999
ADAM_EPS = 1e-08
ADAM_WD = 0.01
ADAM_STEP = 10

V7X_VMEM_LIMIT = 56 * 1024 * 1024
LANES = 128
SUBLANES = 8
MXU = 256

_GELU_C0 = math.sqrt(2.0 / math.pi)
_GELU_C1 = 0.044715


def _params(n_axes):
    return pltpu.CompilerParams(dimension_semantics=("arbitrary",) * n_axes, vmem_limit_bytes=V7X_VMEM_LIMIT)


def _tile(dim, pref, align):
    t = (min(pref, dim) // align) * align
    while t >= align:
        if dim % t == 0:
            return t
        t -= align
    return dim


def _gelu(x):
    return x * (0.5 * (1.0 + jnp.tanh(_GELU_C0 * (x + _GELU_C1 * (x * x * x)))))


def _gelu_and_grad(x):
    t = jnp.tanh(_GELU_C0 * (x + _GELU_C1 * (x * x * x)))
    cdf = 0.5 * (1.0 + t)
    dcdf = 0.5 * (1.0 - t * t) * (_GELU_C0 * (1.0 + 3.0 * _GELU_C1 * (x * x)))
    return x * cdf, cdf + x * dcdf


def _sigmoid(x):
    return 1.0 / (1.0 + jnp.exp(-x))


def _shift_down(x, d):
    if d == 0:
        return x
    row = lax.broadcasted_iota(jnp.int32, x.shape, 0)
    return jnp.where(row >= d, pltpu.roll(x, d, 0), 0.0)


def _shift_up(x, d):
    if d == 0:
        return x
    s = x.shape[0]
    row = lax.broadcasted_iota(jnp.int32, x.shape, 0)
    return jnp.where(row < s - d, pltpu.roll(x, s - d, 0), 0.0)


def _causal_conv(x, w, b):
    k_taps = w.shape[0]
    out = _shift_down(x, k_taps - 1) * w[0:1, :]
    for k in range(1, k_taps):
        out = out + _shift_down(x, k_taps - 1 - k) * w[k:k + 1, :]
    return out + b


def _causal_conv_bwd_x(d_out, w):
    k_taps = w.shape[0]
    d_x = _shift_up(d_out, k_taps - 1) * w[0:1, :]
    for k in range(1, k_taps):
        d_x = d_x + _shift_up(d_out, k_taps - 1 - k) * w[k:k + 1, :]
    return d_x


def _causal_conv_bwd_w(d_out, x, k_taps):
    rows = [jnp.sum(d_out * _shift_down(x, k_taps - 1 - k), axis=0, keepdims=True) for k in range(k_taps)]
    return jnp.concatenate(rows, axis=0)


def _place():
    return lax.axis_index("x"), lax.axis_index("y"), lax.axis_index("c")


def _other_chips(x, y):
    return [(1 - x, y), (x, 1 - y), (1 - x, 1 - y)]


class _Exchange(NamedTuple):
    ins: tuple
    outs: tuple
    in_place: bool
    n_remote: int
    n_local: int
    copies: Callable


def _remote(src, dst, send_sem, recv_sem, to):
    return pltpu.make_async_remote_copy(src_ref=src, dst_ref=dst, send_sem=send_sem, recv_sem=recv_sem, device_id=to,
                                        device_id_type=MESH)


def _gather_first_leg(shards, place_own=True):
    n = len(shards)

    def copies(ins, outs, send_sems, recv_sems, local_sems):
        x, y, c = _place()
        peers = [(x, y, 1 - c)] + [(*chip, c) for chip in _other_chips(x, y)]
        slot = lambda i, dev: outs[i].at[4 * dev[0] + 2 * dev[1] + dev[2]]
        sends = [_remote(ins[i], slot(i, (x, y, c)), send_sems[i].at[k], recv_sems[i].at[k], to)
                 for i in range(n) for k, to in enumerate(peers)]
        receives = [_remote(ins[i], slot(i, frm), send_sems[i].at[k], recv_sems[i].at[k], frm)
                    for i in range(n) for k, frm in enumerate(peers)]
        local = [pltpu.make_async_copy(ins[i], slot(i, (x, y, c)), local_sems[i].at[0]) for i in range(n)] if place_own else []
        return sends, receives, local

    outs = tuple(jax.ShapeDtypeStruct((N_DEV,) + s.shape, s.dtype) for s in shards)
    return _Exchange(tuple(shards), outs, False, 4, int(place_own), copies)


def _gather_second_leg(gathered):
    n = len(gathered)

    def copies(ins, outs, send_sems, recv_sems, local_sems):
        x, y, c = _place()
        slot = lambda i, chip, core: outs[i].at[4 * chip[0] + 2 * chip[1] + core]
        sends = [_remote(slot(i, chip, c), slot(i, chip, c), send_sems[i].at[j], recv_sems[i].at[j], (x, y, 1 - c))
                 for i in range(n) for j, chip in enumerate(_other_chips(x, y))]
        receives = [_remote(slot(i, chip, 1 - c), slot(i, chip, 1 - c), send_sems[i].at[j], recv_sems[i].at[j], (x, y, 1 - c))
                    for i in range(n) for j, chip in enumerate(_other_chips(x, y))]
        return sends, receives, []

    outs = tuple(jax.ShapeDtypeStruct(g.shape, g.dtype) for g in gathered)
    return _Exchange(tuple(gathered), outs, True, 3, 0, copies)


def _swap_with_sibling(parts):
    n = len(parts)

    def copies(ins, outs, send_sems, recv_sems, local_sems):
        x, y, c = _place()
        both = [_remote(ins[i].at[ch, 1 - c], outs[i].at[ch], send_sems[i].at[ch], recv_sems[i].at[ch], (x, y, 1 - c))
                for i in range(n) for ch in range(4)]
        return both, both, []

    outs = tuple(jax.ShapeDtypeStruct((4,) + p.shape[2:], p.dtype) for p in parts)
    return _Exchange(tuple(parts), outs, False, 4, 0, copies)


def _swap_with_chips(parts, place_own=True):
    n = len(parts)

    def copies(ins, outs, send_sems, recv_sems, local_sems):
        x, y, c = _place()
        both = [_remote(ins[i].at[2 * chip[0] + chip[1]], outs[i].at[1 + j], send_sems[i].at[j], recv_sems[i].at[j], (*chip, c))
                for i in range(n) for j, chip in enumerate(_other_chips(x, y))]
        local = [pltpu.make_async_copy(ins[i].at[2 * x + y], outs[i].at[0], local_sems[i].at[0]) for i in range(n)] if place_own else []
        return both, both, local

    outs = tuple(jax.ShapeDtypeStruct(p.shape, p.dtype) for p in parts)
    return _Exchange(tuple(parts), outs, False, 3, int(place_own), copies)


def _exchange_plumbing(exchanges):
    operands = [a for ex in exchanges for a in ex.ins]
    results = [s for ex in exchanges for s in ex.outs]
    scratch, in_place, at = [], {}, 0
    for ex in exchanges:
        n = len(ex.ins)
        scratch += [pltpu.SemaphoreType.DMA((n, ex.n_remote))] * 2
        if ex.n_local:
            scratch.append(pltpu.SemaphoreType.DMA((n, ex.n_local)))
        if ex.in_place:
            in_place.update({at + i: at + i for i in range(n)})
        at += n

    def copies(in_refs, out_refs, sem_refs):
        sends, receives, local = [], [], []
        at, sem_at = 0, 0
        for ex in exchanges:
            n, n_sem = len(ex.ins), 3 if ex.n_local else 2
            per_operand = [[sem.at[i] for i in range(n)] for sem in sem_refs[sem_at:sem_at + n_sem]] + [[]] * (3 - n_sem)
            s, r, l = ex.copies(in_refs[at:at + n], out_refs[at:at + n], *per_operand)
            sends, receives, local = sends + s, receives + r, local + l
            at, sem_at = at + n, sem_at + n_sem
        return sends, receives, local

    return operands, results, scratch, in_place, copies


def _start_all(copies):
    sends, _, local = copies
    for cp in local + sends:
        cp.start()


def _wait_all(copies):
    sends, receives, local = copies
    for cp in receives:
        cp.wait_recv()
    for cp in sends:
        cp.wait_send()
    for cp in local:
        cp.wait()


def _exchange_now(exchanges, *, name):
    operands, results, scratch, in_place, copies = _exchange_plumbing(exchanges)
    n = len(operands)

    def body(*refs):
        made = copies(refs[:n], refs[n:2 * n], refs[2 * n:])
        _start_all(made)
        _wait_all(made)

    out = pl.pallas_call(body, name=name, in_specs=[ANY] * n, out_specs=[ANY] * n, out_shape=results,
                         scratch_shapes=scratch, input_output_aliases=in_place)(*operands)
    return _split(out, exchanges)


def _split(flat, exchanges):
    out, at = [], 0
    for ex in exchanges:
        out.append(list(flat[at:at + len(ex.ins)]))
        at += len(ex.ins)
    return out


HBM = pl.BlockSpec(memory_space=pltpu.HBM)
SEMAPHORES = pl.BlockSpec(memory_space=pltpu.SEMAPHORE)
SPLIT_COPY = pltpu.CompilerParams(has_side_effects=pltpu.SideEffectType.DATAFLOW_SIDE_EFFECTING)


def _start_exchange(make, operands, landings, before, *, name):
    n = len(operands)
    ex = make(operands)

    def body(*refs):
        sends, _, _ = ex.copies(refs[:n], refs[n:2 * n], refs[2 * n + 1:3 * n + 1], refs[3 * n + 1:4 * n + 1], [])
        for cp in sends:
            cp.start()

    buffers = [pltpu.with_memory_space_constraint(a, pltpu.HBM) for a in list(operands) + list(landings) + [before]]
    out = pl.pallas_call(
        body, name=name, in_specs=[HBM] * (2 * n + 1), out_specs=[SEMAPHORES] * (2 * n) + [HBM] * (2 * n + 1),
        out_shape=[pltpu.SemaphoreType.DMA((ex.n_remote,))] * (2 * n) + [pltpu.HBM(a.shape, a.dtype) for a in buffers],
        input_output_aliases={i: 2 * n + i for i in range(2 * n + 1)}, compiler_params=SPLIT_COPY)(*buffers)
    return [(out[i], out[n + i], out[2 * n + i], out[3 * n + i]) for i in range(n)], out[4 * n]


def _await_exchange(make, in_flight, after, *, name):
    n = len(in_flight)
    send_sems, recv_sems, operands, landings = zip(*in_flight)
    ex = make(operands)

    def body(*refs):
        sends, receives, _ = ex.copies(refs[:n], refs[n:2 * n], refs[2 * n:3 * n], refs[3 * n:4 * n], [])
        for cp in receives:
            cp.wait_recv()
        for cp in sends:
            cp.wait_send()

    out = pl.pallas_call(
        body, name=name, in_specs=[HBM] * (2 * n) + [SEMAPHORES] * (2 * n) + [ANY], out_specs=[HBM] * (2 * n),
        out_shape=[pltpu.HBM(a.shape, a.dtype) for a in operands + landings],
        input_output_aliases={i: i for i in range(2 * n)}, compiler_params=SPLIT_COPY,
    )(*operands, *landings, *send_sems, *recv_sems, after)
    return list(out[:n]), list(out[n:])


def _own_block_first(blocks, index):
    own = lax.dynamic_index_in_dim(blocks, index, 0, keepdims=True)
    return lax.dynamic_update_index_in_dim(lax.empty(blocks.shape, blocks.dtype), own, 0, 0)


def _own_block_in_place(shard, index):
    return lax.dynamic_update_index_in_dim(lax.empty((N_DEV,) + shard.shape, shard.dtype), shard, index, 0)


def _pallas(kern, *, name, grid, in_specs, out_specs, out_shape, operands, scratch_shapes=(), exchanges=(), after=None):
    ex_operands, ex_results, ex_scratch, in_place, copies = _exchange_plumbing(exchanges)
    if after is not None:
        ex_operands = [after] + ex_operands
        in_place = {i + 1: o for i, o in in_place.items()}
    n_in, n_out, n_scratch, n_ex = len(in_specs), len(out_specs), len(scratch_shapes), len(ex_results)
    n_unread = len(ex_operands) - n_ex

    def body(*refs):
        ins, refs = refs[:n_in], refs[n_in + n_unread:]
        ex_ins, refs = refs[:n_ex], refs[n_ex:]
        outs, refs = refs[:n_out], refs[n_out:]
        ex_outs, refs = refs[:n_ex], refs[n_ex:]
        scratch, sems = refs[:n_scratch], refs[n_scratch:]
        if exchanges:
            first = functools.reduce(jnp.logical_and, [pl.program_id(a) == 0 for a in range(len(grid))])
            last = functools.reduce(jnp.logical_and, [pl.program_id(a) == g - 1 for a, g in enumerate(grid)])

            @pl.when(first)
            def _():
                _start_all(copies(ex_ins, ex_outs, sems))

        kern(*ins, *outs, *scratch)
        if exchanges:
            @pl.when(last)
            def _():
                _wait_all(copies(ex_ins, ex_outs, sems))

    res = pl.pallas_call(
        body, name=name, grid=grid, in_specs=list(in_specs) + [ANY] * len(ex_operands), out_specs=list(out_specs) + [ANY] * n_ex,
        out_shape=list(out_shape) + ex_results, scratch_shapes=list(scratch_shapes) + ex_scratch,
        input_output_aliases={n_in + i: n_out + o for i, o in in_place.items()},
        compiler_params=_params(len(grid)))(*operands, *ex_operands)
    return list(res[:n_out]), _split(res[n_out:], exchanges)


def _accumulate(step, n_steps, acc, value, finish):
    if n_steps == 1:
        finish(value)
        return

    @pl.when(step == 0)
    def _():
        acc[0][...] = value

    @pl.when(step > 0)
    def _():
        acc[0][...] += value

    @pl.when(step == n_steps - 1)
    def _():
        finish(acc[0][...])


def _mm_nn(a, w, *, out_dtype, name, residual=None, exchanges=(), a_part=0):
    m = a.shape[0]
    nb, k, n_blk = w.shape
    tm, tn, tk = _tile(m, 512, MXU), _tile(n_blk, 1536, MXU), _tile(k, 4096, MXU)
    per = n_blk // tn
    nk = k // tk

    def kern(*refs):
        a_ref, w_ref = refs[:2]
        r_ref = None if residual is None else refs[2]
        o_ref, acc = refs[2 + (residual is not None)], refs[3 + (residual is not None):]

        def finish(total):
            o_ref[...] = (total if r_ref is None else total + r_ref[...]).astype(o_ref.dtype)

        _accumulate(pl.program_id(2), nk, acc, jnp.dot(a_ref[...], w_ref[...], preferred_element_type=F32), finish)

    tile = pl.BlockSpec((tm, tn), lambda j, i, kk: (i, j))
    in_specs = [pl.BlockSpec((tm, tk), lambda j, i, kk: (i, a_part * nk + kk)),
                pl.BlockSpec((None, tk, tn), lambda j, i, kk: (j // per, kk, j % per))]
    operands = [a, w]
    if residual is not None:
        in_specs.append(tile)
        operands.append(residual)
    (out,), carried = _pallas(
        kern, name=name, grid=(nb * per, m // tm, nk), in_specs=in_specs, out_specs=[tile],
        out_shape=[jax.ShapeDtypeStruct((m, nb * n_blk), out_dtype)], operands=operands,
        scratch_shapes=[pltpu.VMEM((tm, tn), F32)] * (nk > 1), exchanges=exchanges)
    return out, carried


def _mm_nt(g, w, *, name, exchanges=(), after=None):
    m, n = g.shape
    nb, k, n_blk = w.shape
    tm, tko, tn = _tile(m, 1024, MXU), _tile(k, 1024, MXU), _tile(n_blk, 3072, MXU)
    per = n_blk // tn
    nn = n // tn

    def kern(g_ref, w_ref, o_ref, *acc):
        def finish(total):
            o_ref[...] = total

        part = lax.dot_general(g_ref[...], w_ref[...], (((1,), (1,)), ((), ())), preferred_element_type=F32)
        _accumulate(pl.program_id(2), nn, acc, part, finish)

    (out,), carried = _pallas(
        kern, name=name, grid=(k // tko, m // tm, nn),
        in_specs=[pl.BlockSpec((tm, tn), lambda j, i, jn: (i, jn)),
                  pl.BlockSpec((None, tko, tn), lambda j, i, jn: (jn // per, j, jn % per))],
        out_specs=[pl.BlockSpec((tm, tko), lambda j, i, jn: (i, j))],
        out_shape=[jax.ShapeDtypeStruct((m, k), F32)], operands=[g, w],
        scratch_shapes=[pltpu.VMEM((tm, tko), F32)] * (nn > 1), exchanges=exchanges, after=after)
    return out, carried


def _mm_tn(a, g, nb, *, name, exchanges=(), after=None, part=0, n_parts=1):
    m, k = a.shape[0], a.shape[1] // n_parts
    n = g.shape[1]
    n_blk = n // nb
    tko, tn, tm = _tile(k, 512, MXU), _tile(n_blk, 1536, MXU), _tile(m, 4096, MXU)
    per = n_blk // tn
    nm = m // tm

    def kern(a_ref, g_ref, o_ref, *acc):
        def finish(total):
            o_ref[...] = total.astype(o_ref.dtype)

        part = lax.dot_general(a_ref[...], g_ref[...], (((0,), (0,)), ((), ())), preferred_element_type=F32)
        _accumulate(pl.program_id(2), nm, acc, part, finish)

    (out,), carried = _pallas(
        kern, name=name, grid=(nb * per, k // tko, nm),
        in_specs=[pl.BlockSpec((tm, tko), lambda j, i, im: (im, part * (k // tko) + i)),
                  pl.BlockSpec((tm, tn), lambda j, i, im: (im, j))],
        out_specs=[pl.BlockSpec((None, tko, tn), lambda j, i, im: (j // per, i, j % per))],
        out_shape=[jax.ShapeDtypeStruct((nb, k, n_blk), BF16)], operands=[a, g],
        scratch_shapes=[pltpu.VMEM((tko, tn), F32)] * (nm > 1), exchanges=exchanges, after=after)
    return out, carried


ROW_TILE = 128


def _rmsnorm_fwd(x, g, *, name):
    t, d = x.shape
    tr = _tile(t, ROW_TILE, SUBLANES)

    def kern(x_ref, g_ref, h_ref):
        xv = x_ref[...]
        r = lax.rsqrt(jnp.mean(xv * xv, axis=-1, keepdims=True) + EPS)
        h_ref[...] = (xv * r * g_ref[...]).astype(BF16)

    return pl.pallas_call(
        kern, name=name, grid=(t // tr,),
        in_specs=[pl.BlockSpec((tr, d), lambda i: (i, 0)), pl.BlockSpec((1, d), lambda i: (0, 0))],
        out_specs=pl.BlockSpec((tr, d), lambda i: (i, 0)),
        out_shape=jax.ShapeDtypeStruct((t, d), BF16), compiler_params=_params(1))(x, g)


def _rmsnorm_bwd(x, g, d_h_parts, d_res, *, name):
    t, d = x.shape
    tr = _tile(t, ROW_TILE, SUBLANES)
    n_parts = len(d_h_parts)

    def kern(x_ref, g_ref, *refs):
        dh_refs, (dres_ref, dx_ref, dxb_ref, dg_ref) = refs[:n_parts], refs[n_parts:]
        xv = x_ref[...]
        r = lax.rsqrt(jnp.mean(xv * xv, axis=-1, keepdims=True) + EPS)
        dh = jnp.concatenate([ref[...] for ref in dh_refs], axis=1)
        gy = dh * g_ref[...]
        dx = dres_ref[...] + r * gy - xv * (r * r * r) * jnp.mean(gy * xv, axis=-1, keepdims=True)
        dx_ref[...] = dx
        dxb_ref[...] = dx.astype(BF16)

        @pl.when(pl.program_id(0) == 0)
        def _():
            dg_ref[...] = jnp.zeros_like(dg_ref)

        dg_ref[...] += jnp.sum(dh * (xv * r), axis=0, keepdims=True)

    row = pl.BlockSpec((tr, d), lambda i: (i, 0))
    vec = pl.BlockSpec((1, d), lambda i: (0, 0))
    part = pl.BlockSpec((tr, d // n_parts), lambda i: (i, 0))
    return pl.pallas_call(
        kern, name=name, grid=(t // tr,), in_specs=[row, vec] + [part] * n_parts + [row], out_specs=[row, row, vec],
        out_shape=[jax.ShapeDtypeStruct((t, d), F32), jax.ShapeDtypeStruct((t, d), BF16),
                   jax.ShapeDtypeStruct((1, d), F32)], compiler_params=_params(1))(x, g, *d_h_parts, d_res)


def _loss_head(x, g, target, *, name):
    t, d = x.shape
    tr = _tile(t, ROW_TILE, SUBLANES)

    def kern(x_ref, g_ref, t_ref, dx_ref, dxb_ref, dg_ref, loss_ref):
        xv = x_ref[...]
        gv = g_ref[...]
        r = lax.rsqrt(jnp.mean(xv * xv, axis=-1, keepdims=True) + EPS)
        diff = xv * r * gv - t_ref[...]
        dy = diff * (1.0 / d)
        gy = dy * gv
        dx = r * gy - xv * (r * r * r) * jnp.mean(gy * xv, axis=-1, keepdims=True)
        dx_ref[...] = dx
        dxb_ref[...] = dx.astype(BF16)

        @pl.when(pl.program_id(0) == 0)
        def _():
            dg_ref[...] = jnp.zeros_like(dg_ref)
            loss_ref[...] = jnp.zeros_like(loss_ref)

        dg_ref[...] += jnp.sum(dy * (xv * r), axis=0, keepdims=True)
        part = 0.5 * jnp.sum(jnp.mean(diff * diff, axis=-1, keepdims=True), axis=0, keepdims=True)
        loss_ref[...] += jnp.broadcast_to(part, loss_ref.shape)

    row = pl.BlockSpec((tr, d), lambda i: (i, 0))
    vec = pl.BlockSpec((1, d), lambda i: (0, 0))
    return pl.pallas_call(
        kern, name=name, grid=(t // tr,), in_specs=[row, vec, row],
        out_specs=[row, row, vec, pl.BlockSpec((1, LANES), lambda i: (0, 0))],
        out_shape=[jax.ShapeDtypeStruct((t, d), F32), jax.ShapeDtypeStruct((t, d), BF16),
                   jax.ShapeDtypeStruct((1, d), F32), jax.ShapeDtypeStruct((1, LANES), F32)],
        compiler_params=_params(1))(x, g, target)


def _merge_fwd(m_a, m_b, proj, gate_col, *, name):
    t, d = m_a.shape
    tr = _tile(t, ROW_TILE, SUBLANES)

    def kern(ma_ref, mb_ref, ga_ref, gb_ref, o_ref):
        o_ref[...] = (_sigmoid(ga_ref[...]) * ma_ref[...] + _sigmoid(gb_ref[...]) * mb_ref[...]).astype(BF16)

    row = pl.BlockSpec((tr, d), lambda i: (i, 0))
    return pl.pallas_call(
        kern, name=name, grid=(t // tr,),
        in_specs=[row, row, pl.BlockSpec((tr, d), lambda i: (i, gate_col)),
                  pl.BlockSpec((tr, d), lambda i: (i, gate_col + 1))],
        out_specs=row, out_shape=jax.ShapeDtypeStruct((t, d), BF16), compiler_params=_params(1))(m_a, m_b, proj, proj)


def _merge_bwd(d_merged, m_a, m_b, proj, gate_col, *, name, after=None):
    t, d = m_a.shape
    tr = _tile(t, ROW_TILE, SUBLANES)

    def kern(dm_ref, ma_ref, mb_ref, ga_ref, gb_ref, dma_ref, dmb_ref, dg_ref):
        dm = dm_ref[...]
        sa = _sigmoid(ga_ref[...])
        sb = _sigmoid(gb_ref[...])
        dma_ref[...] = (dm * sa).astype(BF16)
        dmb_ref[...] = (dm * sb).astype(BF16)
        dg_ref[:, 0:d] = (dm * ma_ref[...] * (sa * (1.0 - sa))).astype(BF16)
        dg_ref[:, d:2 * d] = (dm * mb_ref[...] * (sb * (1.0 - sb))).astype(BF16)

    row = pl.BlockSpec((tr, d), lambda i: (i, 0))
    res, _ = _pallas(
        kern, name=name, grid=(t // tr,),
        in_specs=[row, row, row, pl.BlockSpec((tr, d), lambda i: (i, gate_col)),
                  pl.BlockSpec((tr, d), lambda i: (i, gate_col + 1))],
        out_specs=[row, row, pl.BlockSpec((tr, 2 * d), lambda i: (i, 0))],
        out_shape=[jax.ShapeDtypeStruct((t, d), BF16), jax.ShapeDtypeStruct((t, d), BF16),
                   jax.ShapeDtypeStruct((t, 2 * d), BF16)], operands=[d_merged, m_a, m_b, proj, proj], after=after)
    return res


def _tril_bf16(w, transposed):
    row = lax.broadcasted_iota(jnp.int32, w.shape, 0)
    col = lax.broadcasted_iota(jnp.int32, w.shape, 1)
    keep = (row <= col) if transposed else (row >= col)
    return jnp.where(keep, w, 0.0).astype(BF16)


def _layernorm_stats(v):
    mu = jnp.mean(v, axis=-1, keepdims=True)
    vc = v - mu
    rstd = lax.rsqrt(jnp.mean(vc * vc, axis=-1, keepdims=True) + EPS)
    return vc * rstd, rstd


def _mixer_a_fwd(proj, ln_g, ln_b, sg_w, sg_b_t, sgw, *, name):
    t = proj.shape[0]
    gd = sgw // SG_GROUPS

    def kern(zu_ref, zv_ref, g_ref, b_ref, w_ref, bt_ref, o_ref):
        xhat, _ = _layernorm_stats(_gelu(zv_ref[...]))
        vn = (xhat * g_ref[...] + b_ref[...]).astype(BF16)
        for g in range(SG_GROUPS):
            cols = slice(g * gd, (g + 1) * gd)
            mixed = jnp.dot(_tril_bf16(w_ref[g], False), vn[:, cols], preferred_element_type=F32) + bt_ref[:, g:g + 1]
            o_ref[:, cols] = (_gelu(zu_ref[:, cols]) * mixed).astype(BF16)

    vec = pl.BlockSpec((1, sgw), lambda i: (0, 0))
    return pl.pallas_call(
        kern, name=name, grid=(t // CHUNK,),
        in_specs=[pl.BlockSpec((CHUNK, sgw), lambda i: (i, 0)), pl.BlockSpec((CHUNK, sgw), lambda i: (i, 1)), vec, vec,
                  pl.BlockSpec((SG_GROUPS, CHUNK, CHUNK), lambda i: (0, 0, 0)),
                  pl.BlockSpec((CHUNK, SG_GROUPS), lambda i: (0, 0))],
        out_specs=pl.BlockSpec((CHUNK, sgw), lambda i: (i, 0)),
        out_shape=jax.ShapeDtypeStruct((t, sgw), BF16), compiler_params=_params(1))(proj, proj, ln_g, ln_b, sg_w, sg_b_t)


def _mixer_a_bwd(proj, d_ya, ln_g, ln_b, sg_w, sg_w_t, sg_b_t, sgw, *, name):
    t = proj.shape[0]
    gd = sgw // SG_GROUPS

    def kern(zu_ref, zv_ref, dy_ref, g_ref, b_ref, w_ref, wt_ref, bt_ref, dz_ref, dw_ref, dbt_ref, dg_ref, db_ref, dvn):
        @pl.when(pl.program_id(0) == 0)
        def _():
            dw_ref[...] = jnp.zeros_like(dw_ref)
            dbt_ref[...] = jnp.zeros_like(dbt_ref)
            dg_ref[...] = jnp.zeros_like(dg_ref)
            db_ref[...] = jnp.zeros_like(db_ref)

        gv, dgv = _gelu_and_grad(zv_ref[...])
        xhat, rstd = _layernorm_stats(gv)
        ln_gain = g_ref[...]
        vn = (xhat * ln_gain + b_ref[...]).astype(BF16)
        for g in range(SG_GROUPS):
            cols = slice(g * gd, (g + 1) * gd)
            gu, dgu = _gelu_and_grad(zu_ref[:, cols])
            mixed = jnp.dot(_tril_bf16(w_ref[g], False), vn[:, cols], preferred_element_type=F32) + bt_ref[:, g:g + 1]
            dy = dy_ref[:, cols]
            dz_ref[:, cols] = (dy * mixed * dgu).astype(BF16)
            d_mixed = dy * gu
            d_mixed_b = d_mixed.astype(BF16)
            dvn[:, cols] = jnp.dot(_tril_bf16(wt_ref[g], True), d_mixed_b, preferred_element_type=F32)
            d_w = lax.dot_general(d_mixed_b, vn[:, cols], (((1,), (1,)), ((), ())), preferred_element_type=F32)
            row = lax.broadcasted_iota(jnp.int32, d_w.shape, 0)
            col = lax.broadcasted_iota(jnp.int32, d_w.shape, 1)
            dw_ref[g] += jnp.where(row >= col, d_w, 0.0)
            dbt_ref[:, g:g + 1] += jnp.sum(d_mixed, axis=-1, keepdims=True)
        d_vn = dvn[...]
        dg_ref[...] += jnp.sum(d_vn * xhat, axis=0, keepdims=True)
        db_ref[...] += jnp.sum(d_vn, axis=0, keepdims=True)
        d_xhat = d_vn * ln_gain
        d_gv = rstd * (d_xhat - jnp.mean(d_xhat, axis=-1, keepdims=True)
                       - xhat * jnp.mean(d_xhat * xhat, axis=-1, keepdims=True))
        dz_ref[:, sgw:2 * sgw] = (d_gv * dgv).astype(BF16)

    vec = pl.BlockSpec((1, sgw), lambda i: (0, 0))
    wspec = pl.BlockSpec((SG_GROUPS, CHUNK, CHUNK), lambda i: (0, 0, 0))
    btspec = pl.BlockSpec((CHUNK, SG_GROUPS), lambda i: (0, 0))
    return pl.pallas_call(
        kern, name=name, grid=(t // CHUNK,),
        in_specs=[pl.BlockSpec((CHUNK, sgw), lambda i: (i, 0)), pl.BlockSpec((CHUNK, sgw), lambda i: (i, 1)),
                  pl.BlockSpec((CHUNK, sgw), lambda i: (i, 0)), vec, vec, wspec, wspec, btspec],
        out_specs=[pl.BlockSpec((CHUNK, 2 * sgw), lambda i: (i, 0)), wspec, btspec, vec, vec],
        out_shape=[jax.ShapeDtypeStruct((t, 2 * sgw), BF16), jax.ShapeDtypeStruct((SG_GROUPS, CHUNK, CHUNK), F32),
                   jax.ShapeDtypeStruct((CHUNK, SG_GROUPS), F32), jax.ShapeDtypeStruct((1, sgw), F32),
                   jax.ShapeDtypeStruct((1, sgw), F32)],
        scratch_shapes=[pltpu.VMEM((CHUNK, sgw), F32)],
        compiler_params=_params(1))(proj, proj, d_ya, ln_g, ln_b, sg_w, sg_w_t, sg_b_t)


def _scan_rows(a_ref, h_ref, reverse):
    s, c = a_ref.shape
    nblk = s // SUBLANES
    a, b = a_ref[...], h_ref[...]
    row = jnp.bitwise_and(lax.broadcasted_iota(jnp.int32, (s, c), 0), SUBLANES - 1)
    for d in (1, 2, 4):
        inside = (row < SUBLANES - d) if reverse else (row >= d)
        shift = s - d if reverse else d
        b = a * jnp.where(inside, pltpu.roll(b, shift, 0), 0.0) + b
        a = a * jnp.where(inside, pltpu.roll(a, shift, 0), 1.0)
    a_ref[...] = a
    h_ref[...] = b
    leaving = 0 if reverse else SUBLANES - 1

    def chain(i, carry):
        r0 = pl.multiple_of((nblk - 1 - i if reverse else i) * SUBLANES, SUBLANES)
        h = a_ref[pl.ds(r0, SUBLANES), :] * carry + h_ref[pl.ds(r0, SUBLANES), :]
        h_ref[pl.ds(r0, SUBLANES), :] = h
        return jnp.broadcast_to(h[leaving:leaving + 1, :], (SUBLANES, c))

    lax.fori_loop(0, nblk, chain, jnp.zeros((SUBLANES, c), F32))


def _lru_gates(xc, wa_ref, ba_ref, wx_ref, bx_ref, lam_ref):
    xcb = xc.astype(BF16)
    ra = _sigmoid(jnp.dot(xcb, wa_ref[...].astype(BF16), preferred_element_type=F32) + ba_ref[...])
    ia = _sigmoid(jnp.dot(xcb, wx_ref[...].astype(BF16), preferred_element_type=F32) + bx_ref[...])
    neg = -lam_ref[...]
    sp = jnp.maximum(neg, 0.0) + jnp.log1p(jnp.exp(-jnp.abs(neg)))
    log_a = -LRU_C * ra * sp
    a = jnp.exp(log_a)
    a2 = jnp.exp(2.0 * log_a)
    sq = jnp.sqrt(-jnp.tanh(log_a) * (a2 + 1.0))
    return ra, ia, sp, a, a2, sq


def _mixer_b_specs(seq, hd, sgw, lw):
    x_col = (2 * sgw) // hd
    y_col = (2 * sgw + lw) // hd
    tile = lambda col: pl.BlockSpec((seq, hd), lambda h, b: (b, col + h))
    vec = pl.BlockSpec((1, hd), lambda h, b: (0, h))
    mat = pl.BlockSpec((None, hd, hd), lambda h, b: (h, 0, 0))
    return tile(x_col), tile(y_col), tile(0), vec, mat


def _mixer_b_fwd(proj, conv_w, conv_b, wa, ba, wx, bx, lam, *, seq, sgw, lw, name):
    t = proj.shape[0]
    hd = lw // LRU_HEADS
    k_taps = conv_w.shape[0]
    x_spec, y_spec, o_spec, vec, mat = _mixer_b_specs(seq, hd, sgw, lw)

    def kern(xr_ref, yr_ref, cw_ref, cb_ref, wa_ref, ba_ref, wx_ref, bx_ref, lam_ref, o_ref, s_a, s_h):
        xc = _causal_conv(xr_ref[...], cw_ref[...], cb_ref[...])
        _, ia, _, a, _, sq = _lru_gates(xc, wa_ref, ba_ref, wx_ref, bx_ref, lam_ref)
        s_a[...] = a
        s_h[...] = sq * (ia * xc)
        _scan_rows(s_a, s_h, False)
        o_ref[...] = (s_h[...] * _gelu(yr_ref[...])).astype(BF16)

    return pl.pallas_call(
        kern, name=name, grid=(LRU_HEADS, t // seq),
        in_specs=[x_spec, y_spec, pl.BlockSpec((k_taps, hd), lambda h, b: (0, h)), vec, mat, vec, mat, vec, vec],
        out_specs=o_spec, out_shape=jax.ShapeDtypeStruct((t, lw), BF16),
        scratch_shapes=[pltpu.VMEM((seq, hd), F32), pltpu.VMEM((seq, hd), F32)],
        compiler_params=_params(2))(proj, proj, conv_w, conv_b, wa, ba, wx, bx, lam)


def _mixer_b_bwd(proj, d_yb, conv_w, conv_b, wa, wa_t, ba, wx, wx_t, bx, lam, *, seq, sgw, lw, name, exchanges=()):
    t = proj.shape[0]
    hd = lw // LRU_HEADS
    k_taps = conv_w.shape[0]
    x_spec, y_spec, o_spec, vec, mat = _mixer_b_specs(seq, hd, sgw, lw)
    cw_spec = pl.BlockSpec((k_taps, hd), lambda h, b: (0, h))

    def kern(xr_ref, yr_ref, dyb_ref, cw_ref, cb_ref, wa_ref, wat_ref, ba_ref, wx_ref, wxt_ref, bx_ref, lam_ref,
             dxr_ref, dyr_ref, dcw_ref, dcb_ref, dwa_ref, dba_ref, dwx_ref, dbx_ref, dlam_ref,
             s_xc, s_a, s_h, s_lam, s_dpa, s_dpx):
        @pl.when(pl.program_id(1) == 0)
        def _():
            for ref in (dcw_ref, dcb_ref, dwa_ref, dba_ref, dwx_ref, dbx_ref, dlam_ref):
                ref[...] = jnp.zeros_like(ref)

        s_xc[...] = _causal_conv(xr_ref[...], cw_ref[...], cb_ref[...])
        _, ia, _, a, _, sq = _lru_gates(s_xc[...], wa_ref, ba_ref, wx_ref, bx_ref, lam_ref)
        s_a[...] = a
        s_dpa[...] = _shift_up(a, 1)
        s_h[...] = sq * (ia * s_xc[...])
        _scan_rows(s_a, s_h, False)

        gel, dgel = _gelu_and_grad(yr_ref[...])
        dyb = dyb_ref[...]
        dyr_ref[...] = (dyb * s_h[...] * dgel).astype(BF16)
        s_lam[...] = dyb * gel
        _scan_rows(s_dpa, s_lam, True)
        ra, ia, sp, a, a2, sq = _lru_gates(s_xc[...], wa_ref, ba_ref, wx_ref, bx_ref, lam_ref)
        d_gx = s_lam[...]
        d_a = d_gx * _shift_down(s_h[...], 1)
        xc = s_xc[...]
        d_sq = d_gx * (ia * xc)
        d_ia = d_gx * (sq * xc)
        d_log_a = d_a * a - d_sq * (a2 / sq)
        d_ra = d_log_a * (-LRU_C * sp)
        d_sp = jnp.sum(d_log_a * (-LRU_C * ra), axis=0, keepdims=True)
        dlam_ref[...] += d_sp * (-_sigmoid(-lam_ref[...]))
        d_pa = d_ra * (ra * (1.0 - ra))
        d_px = d_ia * (ia * (1.0 - ia))
        s_dpa[...] = d_pa
        s_dpx[...] = d_px
        dba_ref[...] += jnp.sum(d_pa, axis=0, keepdims=True)
        dbx_ref[...] += jnp.sum(d_px, axis=0, keepdims=True)
        xcb = s_xc[...].astype(BF16)
        d_pa_b = s_dpa[...].astype(BF16)
        d_px_b = s_dpx[...].astype(BF16)
        contract_rows = (((0,), (0,)), ((), ()))
        dwa_ref[...] += lax.dot_general(xcb, d_pa_b, contract_rows, preferred_element_type=F32)
        dwx_ref[...] += lax.dot_general(xcb, d_px_b, contract_rows, preferred_element_type=F32)
        d_xc = (s_lam[...] * (sq * ia)
                + jnp.dot(d_pa_b, wat_ref[...].astype(BF16), preferred_element_type=F32)
                + jnp.dot(d_px_b, wxt_ref[...].astype(BF16), preferred_element_type=F32))
        dcb_ref[...] += jnp.sum(d_xc, axis=0, keepdims=True)
        dcw_ref[...] += _causal_conv_bwd_w(d_xc, xr_ref[...], k_taps)
        dxr_ref[...] = _causal_conv_bwd_x(d_xc, cw_ref[...]).astype(BF16)

    tile_shape = jax.ShapeDtypeStruct((t, lw), BF16)
    vec_shape = jax.ShapeDtypeStruct((1, lw), F32)
    mat_shape = jax.ShapeDtypeStruct((LRU_HEADS, hd, hd), F32)
    return _pallas(
        kern, name=name, grid=(LRU_HEADS, t // seq),
        in_specs=[x_spec, y_spec, o_spec, cw_spec, vec, mat, mat, vec, mat, mat, vec, vec],
        out_specs=[o_spec, o_spec, cw_spec, vec, mat, vec, mat, vec, vec],
        out_shape=[tile_shape, tile_shape, jax.ShapeDtypeStruct((k_taps, lw), F32), vec_shape, mat_shape, vec_shape,
                   mat_shape, vec_shape, vec_shape],
        operands=[proj, proj, d_yb, conv_w, conv_b, wa, wa_t, ba, wx, wx_t, bx, lam],
        scratch_shapes=[pltpu.VMEM((seq, hd), F32)] * 6, exchanges=exchanges)


FFN_TILE = 256


def _ffn_mid_fwd(up_pre, conv_w, conv_b, *, seq, name):
    t, f2 = up_pre.shape
    f = f2 // 2
    tc = _tile(f, FFN_TILE, LANES)
    nf = f // tc
    k_taps = conv_w.shape[0]

    def kern(pg_ref, pv_ref, wg_ref, wv_ref, bg_ref, bv_ref, o_ref):
        cg = _causal_conv(pg_ref[...], wg_ref[...], bg_ref[...])
        cv = _causal_conv(pv_ref[...], wv_ref[...], bv_ref[...])
        o_ref[...] = (_gelu(cg) * cv).astype(BF16)

    tile = lambda off: pl.BlockSpec((seq, tc), lambda j, b: (b, off + j))
    wspec = lambda off: pl.BlockSpec((k_taps, tc), lambda j, b: (0, off + j))
    bspec = lambda off: pl.BlockSpec((1, tc), lambda j, b: (0, off + j))
    return pl.pallas_call(
        kern, name=name, grid=(nf, t // seq),
        in_specs=[tile(0), tile(nf), wspec(0), wspec(nf), bspec(0), bspec(nf)], out_specs=tile(0),
        out_shape=jax.ShapeDtypeStruct((t, f), BF16),
        compiler_params=_params(2))(up_pre, up_pre, conv_w, conv_w, conv_b, conv_b)


def _ffn_mid_bwd(up_pre, d_act, conv_w, conv_b, *, seq, name):
    t, f2 = up_pre.shape
    f = f2 // 2
    tc = _tile(f, FFN_TILE, LANES)
    nf = f // tc
    k_taps = conv_w.shape[0]

    def kern(pg_ref, pv_ref, da_ref, wg_ref, wv_ref, bg_ref, bv_ref, dpg_ref, dpv_ref, dwg_ref, dwv_ref, dbg_ref, dbv_ref):
        @pl.when(pl.program_id(1) == 0)
        def _():
            for ref in (dwg_ref, dwv_ref, dbg_ref, dbv_ref):
                ref[...] = jnp.zeros_like(ref)

        pg = pg_ref[...]
        pv = pv_ref[...]
        gel, dgel = _gelu_and_grad(_causal_conv(pg, wg_ref[...], bg_ref[...]))
        cv = _causal_conv(pv, wv_ref[...], bv_ref[...])
        d_act_v = da_ref[...]
        d_cg = d_act_v * cv * dgel
        d_cv = d_act_v * gel
        dpg_ref[...] = _causal_conv_bwd_x(d_cg, wg_ref[...]).astype(BF16)
        dpv_ref[...] = _causal_conv_bwd_x(d_cv, wv_ref[...]).astype(BF16)
        dwg_ref[...] += _causal_conv_bwd_w(d_cg, pg, k_taps)
        dwv_ref[...] += _causal_conv_bwd_w(d_cv, pv, k_taps)
        dbg_ref[...] += jnp.sum(d_cg, axis=0, keepdims=True)
        dbv_ref[...] += jnp.sum(d_cv, axis=0, keepdims=True)

    tile = lambda off: pl.BlockSpec((seq, tc), lambda j, b: (b, off + j))
    wspec = lambda off: pl.BlockSpec((k_taps, tc), lambda j, b: (0, off + j))
    bspec = lambda off: pl.BlockSpec((1, tc), lambda j, b: (0, off + j))
    half = jax.ShapeDtypeStruct((t, f), BF16)
    wshape = jax.ShapeDtypeStruct((k_taps, f), F32)
    bshape = jax.ShapeDtypeStruct((1, f), F32)
    d_pg, d_pv, d_wg, d_wv, d_bg, d_bv = pl.pallas_call(
        kern, name=name, grid=(nf, t // seq),
        in_specs=[tile(0), tile(nf), tile(0), wspec(0), wspec(nf), bspec(0), bspec(nf)],
        out_specs=[tile(0), tile(0), wspec(0), wspec(0), bspec(0), bspec(0)],
        out_shape=[half, half, wshape, wshape, bshape, bshape],
        compiler_params=_params(2))(up_pre, up_pre, d_act, conv_w, conv_w, conv_b, conv_b)
    return (jnp.concatenate([d_pg, d_pv], axis=1), jnp.concatenate([d_wg, d_wv], axis=1),
            jnp.concatenate([d_bg, d_bv], axis=1))


ELEM_VMEM_BYTES = 24 << 20


def _as_2d(a):
    if a.ndim >= 2 and a.shape[-1] % LANES == 0 and a.size // a.shape[-1] >= SUBLANES:
        return a.reshape(-1, a.shape[-1])
    return a.reshape(-1, LANES)


def _row_tile(rows, bytes_per_row):
    return _tile(rows, max(16, ELEM_VMEM_BYTES // (2 * bytes_per_row)), 16)


def _cast_bf16(a, *, name, part=0, n_parts=1):
    v = _as_2d(a)
    rows, cols = v.shape[0] // n_parts, v.shape[1]
    tr = _row_tile(rows, cols * (4 + 2))
    first = part * (rows // tr)

    def kern(x_ref, o_ref):
        o_ref[...] = x_ref[...].astype(BF16)

    out = pl.pallas_call(kern, name=name, grid=(rows // tr,), in_specs=[pl.BlockSpec((tr, cols), lambda i: (first + i, 0))],
                         out_specs=pl.BlockSpec((tr, cols), lambda i: (i, 0)),
                         out_shape=jax.ShapeDtypeStruct((rows, cols), BF16), compiler_params=_params(1))(v)
    return out.reshape(a.shape) if n_parts == 1 else out


def _add_sibling_part(own, core, got, *, name):
    _, _, rows, cols = own.shape
    tr = _row_tile(rows, cols * (2 + 2 + 2))

    def kern(core_ref, a_ref, b_ref, o_ref):
        o_ref[...] = (a_ref[...].astype(F32) + b_ref[...].astype(F32)).astype(BF16)

    spec = pl.BlockSpec((None, tr, cols), lambda ch, i, core_ref: (ch, i, 0))
    grid_spec = pltpu.PrefetchScalarGridSpec(
        num_scalar_prefetch=1, grid=(4, rows // tr),
        in_specs=[pl.BlockSpec((None, None, tr, cols), lambda ch, i, core_ref: (ch, core_ref[0], i, 0)), spec],
        out_specs=spec)
    return pl.pallas_call(kern, name=name, grid_spec=grid_spec, out_shape=jax.ShapeDtypeStruct(got.shape, BF16),
                          compiler_params=_params(2))(core, own, got)


def _sum_parts(parts, *, name):
    n_parts, rows, cols = parts.shape
    tr = _row_tile(rows, cols * 4 * (n_parts + 1))

    def kern(p_ref, o_ref):
        acc = p_ref[0].astype(F32)
        for p in range(1, n_parts):
            acc = acc + p_ref[p].astype(F32)
        o_ref[...] = acc

    return pl.pallas_call(
        kern, name=name, grid=(rows // tr,), in_specs=[pl.BlockSpec((n_parts, tr, cols), lambda i: (0, i, 0))],
        out_specs=pl.BlockSpec((tr, cols), lambda i: (i, 0)), out_shape=jax.ShapeDtypeStruct((rows, cols), F32),
        compiler_params=_params(1))(parts)


def _adamw(w, m, v, grad_chunks, *, name):
    shape = w.shape
    w2 = _as_2d(w)
    rows, cols = w2.shape
    n_chunks = len(grad_chunks)
    n_parts = grad_chunks[0].shape[0]
    chunks = [c.reshape(n_parts, rows // n_chunks, cols) for c in grad_chunks]
    tr = _row_tile(rows // n_chunks, cols * (3 * 4 + n_chunks * n_parts * chunks[0].dtype.itemsize + 4 * 4))
    per_chunk = rows // n_chunks // tr
    c_m = 1.0 - ADAM_B1 ** ADAM_STEP
    c_v = 1.0 - ADAM_B2 ** ADAM_STEP

    def kern(w_ref, m_ref, v_ref, *refs):
        p_refs, (g_ref, d_ref, nm_ref, nv_ref) = refs[:n_chunks], refs[n_chunks:]
        g = None
        for k, p_ref in enumerate(p_refs):
            total = p_ref[0].astype(F32)
            for p in range(1, n_parts):
                total = total + p_ref[p].astype(F32)
            g = total if g is None else jnp.where(pl.program_id(0) // per_chunk == k, total, g)
        new_m = ADAM_B1 * m_ref[...] + (1.0 - ADAM_B1) * g
        new_v = ADAM_B2 * v_ref[...] + (1.0 - ADAM_B2) * (g * g)
        g_ref[...] = g
        nm_ref[...] = new_m
        nv_ref[...] = new_v
        d_ref[...] = -ADAM_LR * ((new_m / c_m) / (jnp.sqrt(new_v / c_v) + ADAM_EPS) + ADAM_WD * w_ref[...])

    spec = pl.BlockSpec((tr, cols), lambda i: (i, 0))
    out = jax.ShapeDtypeStruct((rows, cols), F32)
    def chunk_spec(k):
        return pl.BlockSpec((n_parts, tr, cols), lambda i: (0, jnp.clip(i - k * per_chunk, 0, per_chunk - 1), 0))

    res = pl.pallas_call(
        kern, name=name, grid=(rows // tr,),
        in_specs=[spec, spec, spec] + [chunk_spec(k) for k in range(n_chunks)],
        out_specs=[spec] * 4, out_shape=[out] * 4, compiler_params=_params(1))(w2, _as_2d(m), _as_2d(v), *chunks)
    return [r.reshape(shape) for r in res]


def _all_gather(shards, *, name):
    n = len(shards)

    def body(*refs):
        ins, outs = refs[:n], refs[n:2 * n]
        send_sems, recv_sems, local_sems = refs[2 * n:]
        x, y, c = _place()
        me, sibling = (x, y, c), (x, y, 1 - c)
        chips = [(1 - x, y), (x, 1 - y), (1 - x, 1 - y)]

        def slot(i, dev):
            return outs[i].at[4 * dev[0] + 2 * dev[1] + dev[2]]

        def copy(i, k, block, to, src=None):
            return pltpu.make_async_remote_copy(
                src_ref=slot(i, block) if src is None else src, dst_ref=slot(i, block),
                send_sem=send_sems.at[i, k], recv_sem=recv_sems.at[i, k], device_id=to, device_id_type=MESH)

        mine = [pltpu.make_async_copy(ins[i], slot(i, me), local_sems.at[i]) for i in range(n)]
        for cp in mine:
            cp.start()
        first = []
        for i in range(n):
            first.append(copy(i, 0, me, sibling, src=ins[i]))
            first += [copy(i, 1 + j, me, (*chip, c), src=ins[i]) for j, chip in enumerate(chips)]
        for cp in first:
            cp.start()
        passed = []
        for j, chip in enumerate(chips):
            for i in range(n):
                copy(i, 1 + j, (*chip, c), me).wait_recv()
                onward = copy(i, 4 + j, (*chip, c), sibling)
                onward.start()
                passed.append(onward)
        for i in range(n):
            copy(i, 0, sibling, me).wait_recv()
            for j, chip in enumerate(chips):
                copy(i, 4 + j, (*chip, 1 - c), me).wait_recv()
        for cp in first + passed:
            cp.wait_send()
        for cp in mine:
            cp.wait()

    return pl.pallas_call(
        body, name=name, in_specs=[ANY] * n, out_specs=[ANY] * n,
        out_shape=[jax.ShapeDtypeStruct((N_DEV,) + s.shape, s.dtype) for s in shards],
        scratch_shapes=[pltpu.SemaphoreType.DMA((n, 7)), pltpu.SemaphoreType.DMA((n, 7)), pltpu.SemaphoreType.DMA((n,))],
    )(*shards)


def _by_chip_and_core(grad):
    return grad.reshape(4, 2, -1, grad.shape[-1])


def _pack(vectors):
    flat = [v.reshape(-1).astype(F32) for v in vectors]
    sizes = [f.shape[0] for f in flat]
    total = sum(sizes)
    padded = -(-total // (SUBLANES * LANES)) * (SUBLANES * LANES)
    if padded > total:
        flat.append(jnp.zeros((padded - total,), F32))
    return jnp.concatenate(flat).reshape(-1, LANES), sizes


def _unpack(packed, sizes, shapes):
    flat = packed.reshape(-1)
    out, off = [], 0
    for size, shape in zip(sizes, shapes):
        out.append(flat[off:off + size].reshape(shape))
        off += size
    return out


def kernel(x, g_mix, w_in, sg_ln_g, sg_ln_b, sg_w, sg_b, lru_conv_w, lru_conv_b, lru_wa, lru_ba, lru_wx, lru_bx, lru_lam, p_sg, p_lru, w_out, g_ffn, w_up, ffn_conv_w, ffn_conv_b, w_down, g_final, loss_target, m_g_mix, m_w_in, m_sg_ln_g, m_sg_ln_b, m_sg_w, m_sg_b, m_lru_conv_w, m_lru_conv_b, m_lru_wa, m_lru_ba, m_lru_wx, m_lru_bx, m_lru_lam, m_p_sg, m_p_lru, m_w_out, m_g_ffn, m_w_up, m_ffn_conv_w, m_ffn_conv_b, m_w_down, m_g_final, v_g_mix, v_w_in, v_sg_ln_g, v_sg_ln_b, v_sg_w, v_sg_b, v_lru_conv_w, v_lru_conv_b, v_lru_wa, v_lru_ba, v_lru_wx, v_lru_bx, v_lru_lam, v_p_sg, v_p_lru, v_w_out, v_g_ffn, v_w_up, v_ffn_conv_w, v_ffn_conv_b, v_w_down, v_g_final):
    weights = dict(g_mix=g_mix, w_in=w_in, sg_ln_g=sg_ln_g, sg_ln_b=sg_ln_b, sg_w=sg_w, sg_b=sg_b, lru_conv_w=lru_conv_w,
                   lru_conv_b=lru_conv_b, lru_wa=lru_wa, lru_ba=lru_ba, lru_wx=lru_wx, lru_bx=lru_bx, lru_lam=lru_lam,
                   p_sg=p_sg, p_lru=p_lru, w_out=w_out, g_ffn=g_ffn, w_up=w_up, ffn_conv_w=ffn_conv_w,
                   ffn_conv_b=ffn_conv_b, w_down=w_down, g_final=g_final)
    m_in = dict(g_mix=m_g_mix, w_in=m_w_in, sg_ln_g=m_sg_ln_g, sg_ln_b=m_sg_ln_b, sg_w=m_sg_w, sg_b=m_sg_b,
                lru_conv_w=m_lru_conv_w, lru_conv_b=m_lru_conv_b, lru_wa=m_lru_wa, lru_ba=m_lru_ba, lru_wx=m_lru_wx,
                lru_bx=m_lru_bx, lru_lam=m_lru_lam, p_sg=m_p_sg, p_lru=m_p_lru, w_out=m_w_out, g_ffn=m_g_ffn,
                w_up=m_w_up, ffn_conv_w=m_ffn_conv_w, ffn_conv_b=m_ffn_conv_b, w_down=m_w_down, g_final=m_g_final)
    v_in = dict(g_mix=v_g_mix, w_in=v_w_in, sg_ln_g=v_sg_ln_g, sg_ln_b=v_sg_ln_b, sg_w=v_sg_w, sg_b=v_sg_b,
                lru_conv_w=v_lru_conv_w, lru_conv_b=v_lru_conv_b, lru_wa=v_lru_wa, lru_ba=v_lru_ba, lru_wx=v_lru_wx,
                lru_bx=v_lru_bx, lru_lam=v_lru_lam, p_sg=v_p_sg, p_lru=v_p_lru, w_out=v_w_out, g_ffn=v_g_ffn,
                w_up=v_w_up, ffn_conv_w=v_ffn_conv_w, ffn_conv_b=v_ffn_conv_b, w_down=v_w_down, g_final=v_g_final)
    order = list(weights)

    n_seq, seq, d = x.shape
    t = n_seq * seq
    sgw = sg_ln_g.shape[-1]
    lw = lru_lam.shape[-1]
    hd = lw // LRU_HEADS
    f2 = ffn_conv_b.shape[-1]
    gate_col = (2 * sgw + 2 * lw) // d
    xi, yi, ci = _place()
    dev = 4 * xi + 2 * yi + ci

    core = jnp.reshape(ci, (1,)).astype(jnp.int32)
    first_leg = functools.partial(_gather_first_leg, place_own=False)
    chip_swap = functools.partial(_swap_with_chips, place_own=False)
    shards, in_flight = {}, {}

    def landed(keys, after, name):
        return _await_exchange(first_leg, [in_flight[k] for k in keys], after, name=name)[1]

    needed_next = ["w_in_bottom", "lru_wa", "lru_wx", "p_sg", "taps"]
    by_need = ["w_in_top"] + needed_next + ["p_lru", "w_out", "w_up", "w_down"]
    shards["w_in_top"] = _cast_bf16(w_in[0], name="cast_w_in_top", part=0, n_parts=2)
    shards["w_in_bottom"] = _cast_bf16(w_in[0], name="cast_w_in_bottom", part=1, n_parts=2)
    shards.update({k: _cast_bf16(weights[k][0], name=f"cast_{k}") for k in by_need if k in weights})
    shards["taps"], tap_sizes = _pack([lru_conv_w[0], ffn_conv_w[0]])
    started, g_mix_after_start = _start_exchange(
        first_leg, [shards[k] for k in by_need], [_own_block_in_place(shards[k], dev) for k in by_need], g_mix,
        name="start_weight_gather")
    in_flight.update(zip(by_need, started))

    def rows_in_order(g8):
        return g8.reshape(1, -1, g8.shape[-1])

    x2d = x.reshape(t, d)
    h1 = _rmsnorm_fwd(x2d, g_mix_after_start, name="norm_mix")
    ((w_in_top_g,),) = _exchange_now([_gather_second_leg(landed(["w_in_top"], h1, "await_w_in_top"))],
                                     name="second_leg_w_in_top")
    proj_top, _ = _mm_nn(h1, w_in_top_g, out_dtype=F32, name="proj_in_top")
    ((w_in_bottom_g, wa_8, wx_8, p_sg_g, taps_8),) = _exchange_now(
        [_gather_second_leg(landed(needed_next, proj_top, "await_w_in_bottom"))], name="second_leg_w_in_bottom")
    proj, _ = _mm_nn(h1, w_in_bottom_g, out_dtype=F32, residual=proj_top, a_part=1, name="proj_in")
    wa_g, wx_g = (jnp.swapaxes(w8, 0, 1).reshape(LRU_HEADS, hd, hd) for w8 in (wa_8, wx_8))
    wa_t, wx_t = jnp.swapaxes(wa_g, 1, 2), jnp.swapaxes(wx_g, 1, 2)
    tap_parts = [_unpack(taps_8[k], tap_sizes, [lru_conv_w.shape[1:], ffn_conv_w.shape[1:]]) for k in range(N_DEV)]
    lru_cw = jnp.concatenate([p[0] for p in tap_parts], axis=1)
    ffn_cw = jnp.concatenate([p[1] for p in tap_parts], axis=1)
    sg_w0 = sg_w[0]
    sg_w_t = jnp.swapaxes(sg_w0, 1, 2)
    sg_b_t = sg_b[0].T
    y_a = _mixer_a_fwd(proj, sg_ln_g, sg_ln_b, sg_w0, sg_b_t, sgw, name="mixer_a_fwd")
    y_b = _mixer_b_fwd(proj, lru_cw, lru_conv_b, wa_g, lru_ba, wx_g, lru_bx, lru_lam, seq=seq, sgw=sgw, lw=lw,
                       name="mixer_b_fwd")
    m_a, ((p_lru_8, w_out_8),) = _mm_nn(
        y_a, p_sg_g, out_dtype=F32, name="proj_sg",
        exchanges=[_gather_second_leg(landed(["p_lru", "w_out"], y_b, "await_p_lru_w_out"))])
    p_lru_g, w_out_g = rows_in_order(p_lru_8), rows_in_order(w_out_8)
    m_b, _ = _mm_nn(y_b, p_lru_g, out_dtype=F32, name="proj_lru")
    merged = _merge_fwd(m_a, m_b, proj, gate_col, name="merge_fwd")
    x1, ((w_up_g,),) = _mm_nn(merged, w_out_g, out_dtype=F32, residual=x2d, name="proj_out",
                              exchanges=[_gather_second_leg(landed(["w_up"], merged, "await_w_up"))])
    h2 = _rmsnorm_fwd(x1, g_ffn, name="norm_ffn")
    up_pre, _ = _mm_nn(h2, w_up_g, out_dtype=F32, name="ffn_up")
    act = _ffn_mid_fwd(up_pre, ffn_cw, ffn_conv_b, seq=seq, name="ffn_mid_fwd")
    ((w_down_8,),) = _exchange_now([_gather_second_leg(landed(["w_down"], act, "await_w_down"))], name="second_leg_w_down")
    w_down_g = rows_in_order(w_down_8)
    x2, _ = _mm_nn(act, w_down_g, out_dtype=F32, residual=x1, name="ffn_down")
    d_x2, d_x2_b, d_g_final, loss_part = _loss_head(x2, g_final.reshape(1, d), loss_target.reshape(t, d), name="loss_head")
    loss = lax.psum(loss_part[0, 0], ("x", "y", "c"))

    def by_rows(g):
        return g.reshape(N_DEV, -1, g.shape[-1])

    def by_head_rows(g):
        return jnp.swapaxes(g.reshape(LRU_HEADS, N_DEV, hd // N_DEV, hd), 0, 1)

    def token():
        return jnp.zeros((SUBLANES, LANES), F32)

    def start_sibling_swap(views, key):
        return _start_exchange(_swap_with_sibling, views, [lax.empty((4,) + v.shape[2:], v.dtype) for v in views], token(),
                               name=f"start_sibling_{key}")

    chip_swaps = {}

    def sum_and_start_chip_swap(keys, sibling_swap, done, ride=None):
        views, from_sibling = _await_exchange(_swap_with_sibling, sibling_swap, done, name=f"await_sibling_{keys[0]}")
        sums = [_add_sibling_part(v, core, s, name=f"chip_sum_{k}") for k, v, s in zip(keys, views, from_sibling)]
        started, ride = _start_exchange(chip_swap, sums, [_own_block_first(s, 2 * xi + yi) for s in sums],
                                        token() if ride is None else ride, name=f"start_chips_{keys[0]}")
        chip_swaps.update(zip(keys, started))
        return ride

    d_w_down, _ = _mm_tn(act, d_x2_b, 1, name="grad_w_down")
    v_down = _by_chip_and_core(by_rows(d_w_down))
    swap, started = start_sibling_swap([v_down], "w_down")
    d_act, _ = _mm_nt(d_x2_b, w_down_g, name="bwd_ffn_down", after=started)
    ffn_cb = sum_and_start_chip_swap(["w_down"], swap, d_act, ride=ffn_conv_b)
    d_up_pre, d_ffn_cw, d_ffn_cb = _ffn_mid_bwd(up_pre, d_act, ffn_cw, ffn_cb, seq=seq, name="ffn_mid_bwd")
    d_w_up, _ = _mm_tn(h2, d_up_pre, N_DEV, name="grad_w_up")
    v_up = _by_chip_and_core(d_w_up)
    swap, started = start_sibling_swap([v_up], "w_up")
    d_h2, _ = _mm_nt(d_up_pre, w_up_g, name="bwd_ffn_up", after=started)
    g_ffn_then = sum_and_start_chip_swap(["w_up"], swap, d_h2, ride=g_ffn)
    d_x1, d_x1_b, d_g_ffn = _rmsnorm_bwd(x1, g_ffn_then, [d_h2], d_x2, name="norm_ffn_bwd")
    d_w_out, _ = _mm_tn(merged, d_x1_b, 1, name="grad_w_out")
    v_out = _by_chip_and_core(by_rows(d_w_out))
    swap, started = start_sibling_swap([v_out], "w_out")
    d_merged, _ = _mm_nt(d_x1_b, w_out_g, name="bwd_proj_out", after=started)
    started = sum_and_start_chip_swap(["w_out"], swap, d_merged)
    d_m_a, d_m_b, d_gates = _merge_bwd(d_merged, m_a, m_b, proj, gate_col, name="merge_bwd", after=started)
    d_p_sg, _ = _mm_tn(y_a, d_m_a, N_DEV, name="grad_p_sg")
    d_p_lru, _ = _mm_tn(y_b, d_m_b, 1, name="grad_p_lru")
    v_sg, v_lru = _by_chip_and_core(d_p_sg), _by_chip_and_core(by_rows(d_p_lru))
    swap, started = start_sibling_swap([v_sg, v_lru], "p_sg")
    d_y_a, _ = _mm_nt(d_m_a, p_sg_g, name="bwd_proj_sg", after=started)
    d_y_b, _ = _mm_nt(d_m_b, p_lru_g, name="bwd_proj_lru")
    lru_cb = sum_and_start_chip_swap(["p_sg", "p_lru"], swap, d_y_b, ride=lru_conv_b)
    d_zuv, d_sg_w, d_sg_b_t, d_ln_g, d_ln_b = _mixer_a_bwd(proj, d_y_a, sg_ln_g, sg_ln_b, sg_w0, sg_w_t, sg_b_t, sgw,
                                                           name="mixer_a_bwd")
    (d_xr, d_yr, d_lru_cw, d_lru_cb, d_wa, d_ba, d_wx, d_bx, d_lam), _ = _mixer_b_bwd(
        proj, d_y_b, lru_cw, lru_cb, wa_g, wa_t, lru_ba, wx_g, wx_t, lru_bx, lru_lam, seq=seq, sgw=sgw, lw=lw,
        name="mixer_b_bwd")
    d_proj = jnp.concatenate([d_zuv, d_xr, d_yr, d_gates], axis=1)
    v_wa = _by_chip_and_core(_cast_bf16(by_head_rows(d_wa), name="cast_grad_wa"))
    v_wx = _by_chip_and_core(_cast_bf16(by_head_rows(d_wx), name="cast_grad_wx"))
    d_w_in_top, _ = _mm_tn(h1, d_proj, N_DEV, name="grad_w_in_top", part=0, n_parts=2)
    swap_top, started = start_sibling_swap([_by_chip_and_core(d_w_in_top)], "w_in_top")
    d_w_in_bottom, _ = _mm_tn(h1, d_proj, N_DEV, name="grad_w_in_bottom", part=1, n_parts=2, after=started)
    swap_bottom, started = start_sibling_swap([_by_chip_and_core(d_w_in_bottom), v_wa, v_wx], "w_in_bottom")
    started = sum_and_start_chip_swap(["w_in_top"], swap_top, started)
    d_h1_left, _ = _mm_nt(d_proj, w_in_top_g, name="bwd_proj_in_top", after=started)
    d_h1_right, _ = _mm_nt(d_proj, w_in_bottom_g, name="bwd_proj_in_bottom")
    g_mix_then = sum_and_start_chip_swap(["w_in_bottom", "lru_wa", "lru_wx"], swap_bottom, d_h1_right, ride=g_mix)
    grad_x, _, d_g_mix = _rmsnorm_bwd(x2d, g_mix_then, [d_h1_left, d_h1_right], d_x1, name="norm_mix_bwd")

    small = ["g_mix", "sg_ln_g", "sg_ln_b", "sg_w", "sg_b", "lru_conv_b", "lru_ba", "lru_bx", "lru_lam", "g_ffn",
             "ffn_conv_b", "g_final", "lru_conv_w", "ffn_conv_w"]
    small_parts = dict(g_mix=d_g_mix, sg_ln_g=d_ln_g, sg_ln_b=d_ln_b, sg_w=d_sg_w, sg_b=d_sg_b_t.T, lru_conv_b=d_lru_cb,
                       lru_ba=d_ba, lru_bx=d_bx, lru_lam=d_lam, g_ffn=d_g_ffn, ffn_conv_b=d_ffn_cb, g_final=d_g_final,
                       lru_conv_w=d_lru_cw, ffn_conv_w=d_ffn_cw)
    packed, sizes = _pack([small_parts[k] for k in small])
    (all_small,) = _all_gather([packed], name="gather_small_grads")
    small_sum = _sum_parts(all_small, name="sum_small_grads")
    small_grads = dict(zip(small, _unpack(small_sum, sizes, [small_parts[k].shape for k in small])))
    for k in ("lru_conv_w", "ffn_conv_w"):
        n_loc = weights[k].shape[-1]
        small_grads[k] = lax.dynamic_slice_in_dim(small_grads[k], dev * n_loc, n_loc, axis=1)

    grads, deltas, new_m, new_v = {}, {}, {}, {}

    def update(k, chunks):
        grads[k], deltas[k], new_m[k], new_v[k] = _adamw(weights[k], m_in[k], v_in[k], chunks, name=f"adamw_{k}")

    def chip_sums_landed(keys, after):
        _, parts = _await_exchange(chip_swap, [chip_swaps[k] for k in keys], after, name=f"await_chips_{keys[0]}")
        return dict(zip(keys, parts))

    for keys in (["w_down"], ["w_up"], ["w_out"], ["p_sg", "p_lru"]):
        for k, parts in chip_sums_landed(keys, grad_x).items():
            update(k, [parts])
    for k in small:
        update(k, [small_grads[k][None]])
    last = chip_sums_landed(["w_in_top", "w_in_bottom", "lru_wa", "lru_wx"], deltas["w_up"])
    update("w_in", [last["w_in_top"], last["w_in_bottom"]])
    update("lru_wa", [last["lru_wa"]])
    update("lru_wx", [last["lru_wx"]])

    return (loss, grad_x.reshape(x.shape), *[grads[k] for k in order], *[deltas[k] for k in order],
            *[new_m[k] for k in order], *[new_v[k] for k in order])
```

```python
import functools
import math
from typing import Callable, NamedTuple

import jax
import jax.numpy as jnp
from jax import lax
from jax.experimental import pallas as pl
from jax.experimental.pallas import tpu as pltpu

F32 = jnp.float32
BF16 = jnp.bfloat16
MESH = pl.DeviceIdType.MESH
ANY = pl.BlockSpec(memory_space=pl.ANY)

N_DEV = 8
EPS = 1e-6
CHUNK = 128
SG_GROUPS = 8
LRU_HEADS = 16
LRU_C = 8.0
ADAM_LR = 0.001
ADAM_B1 = 0.9
ADAM_B2 = 0.999
ADAM_EPS = 1e-08
ADAM_WD = 0.01
ADAM_STEP = 10

V7X_VMEM_LIMIT = 56 * 1024 * 1024
LANES = 128
SUBLANES = 8
MXU = 256

_GELU_C0 = math.sqrt(2.0 / math.pi)
_GELU_C1 = 0.044715


def _params(n_axes):
    return pltpu.CompilerParams(dimension_semantics=("arbitrary",) * n_axes, vmem_limit_bytes=V7X_VMEM_LIMIT)


def _tile(dim, pref, align):
    t = (min(pref, dim) // align) * align
    while t >= align:
        if dim % t == 0:
            return t
        t -= align
    return dim


def _gelu(x):
    return x * (0.5 * (1.0 + jnp.tanh(_GELU_C0 * (x + _GELU_C1 * (x * x * x)))))


def _gelu_and_grad(x):
    t = jnp.tanh(_GELU_C0 * (x + _GELU_C1 * (x * x * x)))
    cdf = 0.5 * (1.0 + t)
    dcdf = 0.5 * (1.0 - t * t) * (_GELU_C0 * (1.0 + 3.0 * _GELU_C1 * (x * x)))
    return x * cdf, cdf + x * dcdf


def _sigmoid(x):
    return 1.0 / (1.0 + jnp.exp(-x))


def _shift_down(x, d):
    if d == 0:
        return x
    row = lax.broadcasted_iota(jnp.int32, x.shape, 0)
    return jnp.where(row >= d, pltpu.roll(x, d, 0), 0.0)


def _shift_up(x, d):
    if d == 0:
        return x
    s = x.shape[0]
    row = lax.broadcasted_iota(jnp.int32, x.shape, 0)
    return jnp.where(row < s - d, pltpu.roll(x, s - d, 0), 0.0)


def _causal_conv(x, w, b):
    k_taps = w.shape[0]
    out = _shift_down(x, k_taps - 1) * w[0:1, :]
    for k in range(1, k_taps):
        out = out + _shift_down(x, k_taps - 1 - k) * w[k:k + 1, :]
    return out + b


def _causal_conv_bwd_x(d_out, w):
    k_taps = w.shape[0]
    d_x = _shift_up(d_out, k_taps - 1) * w[0:1, :]
    for k in range(1, k_taps):
        d_x = d_x + _shift_up(d_out, k_taps - 1 - k) * w[k:k + 1, :]
    return d_x


def _causal_conv_bwd_w(d_out, x, k_taps):
    rows = [jnp.sum(d_out * _shift_down(x, k_taps - 1 - k), axis=0, keepdims=True) for k in range(k_taps)]
    return jnp.concatenate(rows, axis=0)


def _place():
    return lax.axis_index("x"), lax.axis_index("y"), lax.axis_index("c")


def _other_chips(x, y):
    return [(1 - x, y), (x, 1 - y), (1 - x, 1 - y)]


class _Exchange(NamedTuple):
    ins: tuple
    outs: tuple
    in_place: bool
    n_remote: int
    n_local: int
    copies: Callable


def _remote(src, dst, send_sem, recv_sem, to):
    return pltpu.make_async_remote_copy(src_ref=src, dst_ref=dst, send_sem=send_sem, recv_sem=recv_sem, device_id=to,
                                        device_id_type=MESH)


def _gather_first_leg(shards, place_own=True):
    n = len(shards)

    def copies(ins, outs, send_sems, recv_sems, local_sems):
        x, y, c = _place()
        peers = [(x, y, 1 - c)] + [(*chip, c) for chip in _other_chips(x, y)]
        slot = lambda i, dev: outs[i].at[4 * dev[0] + 2 * dev[1] + dev[2]]
        sends = [_remote(ins[i], slot(i, (x, y, c)), send_sems[i].at[k], recv_sems[i].at[k], to)
                 for i in range(n) for k, to in enumerate(peers)]
        receives = [_remote(ins[i], slot(i, frm), send_sems[i].at[k], recv_sems[i].at[k], frm)
                    for i in range(n) for k, frm in enumerate(peers)]
        local = [pltpu.make_async_copy(ins[i], slot(i, (x, y, c)), local_sems[i].at[0]) for i in range(n)] if place_own else []
        return sends, receives, local

    outs = tuple(jax.ShapeDtypeStruct((N_DEV,) + s.shape, s.dtype) for s in shards)
    return _Exchange(tuple(shards), outs, False, 4, int(place_own), copies)


def _gather_second_leg(gathered):
    n = len(gathered)

    def copies(ins, outs, send_sems, recv_sems, local_sems):
        x, y, c = _place()
        slot = lambda i, chip, core: outs[i].at[4 * chip[0] + 2 * chip[1] + core]
        sends = [_remote(slot(i, chip, c), slot(i, chip, c), send_sems[i].at[j], recv_sems[i].at[j], (x, y, 1 - c))
                 for i in range(n) for j, chip in enumerate(_other_chips(x, y))]
        receives = [_remote(slot(i, chip, 1 - c), slot(i, chip, 1 - c), send_sems[i].at[j], recv_sems[i].at[j], (x, y, 1 - c))
                    for i in range(n) for j, chip in enumerate(_other_chips(x, y))]
        return sends, receives, []

    outs = tuple(jax.ShapeDtypeStruct(g.shape, g.dtype) for g in gathered)
    return _Exchange(tuple(gathered), outs, True, 3, 0, copies)


def _swap_with_sibling(parts):
    n = len(parts)

    def copies(ins, outs, send_sems, recv_sems, local_sems):
        x, y, c = _place()
        both = [_remote(ins[i].at[ch, 1 - c], outs[i].at[ch], send_sems[i].at[ch], recv_sems[i].at[ch], (x, y, 1 - c))
                for i in range(n) for ch in range(4)]
        return both, both, []

    outs = tuple(jax.ShapeDtypeStruct((4,) + p.shape[2:], p.dtype) for p in parts)
    return _Exchange(tuple(parts), outs, False, 4, 0, copies)


def _swap_with_chips(parts, place_own=True):
    n = len(parts)

    def copies(ins, outs, send_sems, recv_sems, local_sems):
        x, y, c = _place()
        both = [_remote(ins[i].at[2 * chip[0] + chip[1]], outs[i].at[1 + j], send_sems[i].at[j], recv_sems[i].at[j], (*chip, c))
                for i in range(n) for j, chip in enumerate(_other_chips(x, y))]
        local = [pltpu.make_async_copy(ins[i].at[2 * x + y], outs[i].at[0], local_sems[i].at[0]) for i in range(n)] if place_own else []
        return both, both, local

    outs = tuple(jax.ShapeDtypeStruct(p.shape, p.dtype) for p in parts)
    return _Exchange(tuple(parts), outs, False, 3, int(place_own), copies)


def _exchange_plumbing(exchanges):
    operands = [a for ex in exchanges for a in ex.ins]
    results = [s for ex in exchanges for s in ex.outs]
    scratch, in_place, at = [], {}, 0
    for ex in exchanges:
        n = len(ex.ins)
        scratch += [pltpu.SemaphoreType.DMA((n, ex.n_remote))] * 2
        if ex.n_local:
            scratch.append(pltpu.SemaphoreType.DMA((n, ex.n_local)))
        if ex.in_place:
            in_place.update({at + i: at + i for i in range(n)})
        at += n

    def copies(in_refs, out_refs, sem_refs):
        sends, receives, local = [], [], []
        at, sem_at = 0, 0
        for ex in exchanges:
            n, n_sem = len(ex.ins), 3 if ex.n_local else 2
            per_operand = [[sem.at[i] for i in range(n)] for sem in sem_refs[sem_at:sem_at + n_sem]] + [[]] * (3 - n_sem)
            s, r, l = ex.copies(in_refs[at:at + n], out_refs[at:at + n], *per_operand)
            sends, receives, local = sends + s, receives + r, local + l
            at, sem_at = at + n, sem_at + n_sem
        return sends, receives, local

    return operands, results, scratch, in_place, copies


def _start_all(copies):
    sends, _, local = copies
    for cp in local + sends:
        cp.start()


def _wait_all(copies):
    sends, receives, local = copies
    for cp in receives:
        cp.wait_recv()
    for cp in sends:
        cp.wait_send()
    for cp in local:
        cp.wait()


def _exchange_now(exchanges, *, name):
    operands, results, scratch, in_place, copies = _exchange_plumbing(exchanges)
    n = len(operands)

    def body(*refs):
        made = copies(refs[:n], refs[n:2 * n], refs[2 * n:])
        _start_all(made)
        _wait_all(made)

    out = pl.pallas_call(body, name=name, in_specs=[ANY] * n, out_specs=[ANY] * n, out_shape=results,
                         scratch_shapes=scratch, input_output_aliases=in_place)(*operands)
    return _split(out, exchanges)


def _split(flat, exchanges):
    out, at = [], 0
    for ex in exchanges:
        out.append(list(flat[at:at + len(ex.ins)]))
        at += len(ex.ins)
    return out


HBM = pl.BlockSpec(memory_space=pltpu.HBM)
SEMAPHORES = pl.BlockSpec(memory_space=pltpu.SEMAPHORE)
SPLIT_COPY = pltpu.CompilerParams(has_side_effects=pltpu.SideEffectType.DATAFLOW_SIDE_EFFECTING)


def _start_exchange(make, operands, landings, before, *, name):
    n = len(operands)
    ex = make(operands)

    def body(*refs):
        sends, _, _ = ex.copies(refs[:n], refs[n:2 * n], refs[2 * n + 1:3 * n + 1], refs[3 * n + 1:4 * n + 1], [])
        for cp in sends:
            cp.start()

    buffers = [pltpu.with_memory_space_constraint(a, pltpu.HBM) for a in list(operands) + list(landings) + [before]]
    out = pl.pallas_call(
        body, name=name, in_specs=[HBM] * (2 * n + 1), out_specs=[SEMAPHORES] * (2 * n) + [HBM] * (2 * n + 1),
        out_shape=[pltpu.SemaphoreType.DMA((ex.n_remote,))] * (2 * n) + [pltpu.HBM(a.shape, a.dtype) for a in buffers],
        input_output_aliases={i: 2 * n + i for i in range(2 * n + 1)}, compiler_params=SPLIT_COPY)(*buffers)
    return [(out[i], out[n + i], out[2 * n + i], out[3 * n + i]) for i in range(n)], out[4 * n]


def _await_exchange(make, in_flight, after, *, name):
    n = len(in_flight)
    send_sems, recv_sems, operands, landings = zip(*in_flight)
    ex = make(operands)

    def body(*refs):
        sends, receives, _ = ex.copies(refs[:n], refs[n:2 * n], refs[2 * n:3 * n], refs[3 * n:4 * n], [])
        for cp in receives:
            cp.wait_recv()
        for cp in sends:
            cp.wait_send()

    out = pl.pallas_call(
        body, name=name, in_specs=[HBM] * (2 * n) + [SEMAPHORES] * (2 * n) + [ANY], out_specs=[HBM] * (2 * n),
        out_shape=[pltpu.HBM(a.shape, a.dtype) for a in operands + landings],
        input_output_aliases={i: i for i in range(2 * n)}, compiler_params=SPLIT_COPY,
    )(*operands, *landings, *send_sems, *recv_sems, after)
    return list(out[:n]), list(out[n:])


def _own_block_first(blocks, index):
    own = lax.dynamic_index_in_dim(blocks, index, 0, keepdims=True)
    return lax.dynamic_update_index_in_dim(lax.empty(blocks.shape, blocks.dtype), own, 0, 0)


def _own_block_in_place(shard, index):
    return lax.dynamic_update_index_in_dim(lax.empty((N_DEV,) + shard.shape, shard.dtype), shard, index, 0)


def _pallas(kern, *, name, grid, in_specs, out_specs, out_shape, operands, scratch_shapes=(), exchanges=(), after=None):
    ex_operands, ex_results, ex_scratch, in_place, copies = _exchange_plumbing(exchanges)
    if after is not None:
        ex_operands = [after] + ex_operands
        in_place = {i + 1: o for i, o in in_place.items()}
    n_in, n_out, n_scratch, n_ex = len(in_specs), len(out_specs), len(scratch_shapes), len(ex_results)
    n_unread = len(ex_operands) - n_ex

    def body(*refs):
        ins, refs = refs[:n_in], refs[n_in + n_unread:]
        ex_ins, refs = refs[:n_ex], refs[n_ex:]
        outs, refs = refs[:n_out], refs[n_out:]
        ex_outs, refs = refs[:n_ex], refs[n_ex:]
        scratch, sems = refs[:n_scratch], refs[n_scratch:]
        if exchanges:
            first = functools.reduce(jnp.logical_and, [pl.program_id(a) == 0 for a in range(len(grid))])
            last = functools.reduce(jnp.logical_and, [pl.program_id(a) == g - 1 for a, g in enumerate(grid)])

            @pl.when(first)
            def _():
                _start_all(copies(ex_ins, ex_outs, sems))

        kern(*ins, *outs, *scratch)
        if exchanges:
            @pl.when(last)
            def _():
                _wait_all(copies(ex_ins, ex_outs, sems))

    res = pl.pallas_call(
        body, name=name, grid=grid, in_specs=list(in_specs) + [ANY] * len(ex_operands), out_specs=list(out_specs) + [ANY] * n_ex,
        out_shape=list(out_shape) + ex_results, scratch_shapes=list(scratch_shapes) + ex_scratch,
        input_output_aliases={n_in + i: n_out + o for i, o in in_place.items()},
        compiler_params=_params(len(grid)))(*operands, *ex_operands)
    return list(res[:n_out]), _split(res[n_out:], exchanges)


def _accumulate(step, n_steps, acc, value, finish):
    if n_steps == 1:
        finish(value)
        return

    @pl.when(step == 0)
    def _():
        acc[0][...] = value

    @pl.when(step > 0)
    def _():
        acc[0][...] += value

    @pl.when(step == n_steps - 1)
    def _():
        finish(acc[0][...])


def _mm_nn(a, w, *, out_dtype, name, residual=None, exchanges=(), a_part=0):
    m = a.shape[0]
    nb, k, n_blk = w.shape
    tm, tn, tk = _tile(m, 512, MXU), _tile(n_blk, 1536, MXU), _tile(k, 4096, MXU)
    per = n_blk // tn
    nk = k // tk

    def kern(*refs):
        a_ref, w_ref = refs[:2]
        r_ref = None if residual is None else refs[2]
        o_ref, acc = refs[2 + (residual is not None)], refs[3 + (residual is not None):]

        def finish(total):
            o_ref[...] = (total if r_ref is None else total + r_ref[...]).astype(o_ref.dtype)

        _accumulate(pl.program_id(2), nk, acc, jnp.dot(a_ref[...], w_ref[...], preferred_element_type=F32), finish)

    tile = pl.BlockSpec((tm, tn), lambda j, i, kk: (i, j))
    in_specs = [pl.BlockSpec((tm, tk), lambda j, i, kk: (i, a_part * nk + kk)),
                pl.BlockSpec((None, tk, tn), lambda j, i, kk: (j // per, kk, j % per))]
    operands = [a, w]
    if residual is not None:
        in_specs.append(tile)
        operands.append(residual)
    (out,), carried = _pallas(
        kern, name=name, grid=(nb * per, m // tm, nk), in_specs=in_specs, out_specs=[tile],
        out_shape=[jax.ShapeDtypeStruct((m, nb * n_blk), out_dtype)], operands=operands,
        scratch_shapes=[pltpu.VMEM((tm, tn), F32)] * (nk > 1), exchanges=exchanges)
    return out, carried


def _mm_nt(g, w, *, name, exchanges=(), after=None):
    m, n = g.shape
    nb, k, n_blk = w.shape
    tm, tko, tn = _tile(m, 1024, MXU), _tile(k, 1024, MXU), _tile(n_blk, 3072, MXU)
    per = n_blk // tn
    nn = n // tn

    def kern(g_ref, w_ref, o_ref, *acc):
        def finish(total):
            o_ref[...] = total

        part = lax.dot_general(g_ref[...], w_ref[...], (((1,), (1,)), ((), ())), preferred_element_type=F32)
        _accumulate(pl.program_id(2), nn, acc, part, finish)

    (out,), carried = _pallas(
        kern, name=name, grid=(k // tko, m // tm, nn),
        in_specs=[pl.BlockSpec((tm, tn), lambda j, i, jn: (i, jn)),
                  pl.BlockSpec((None, tko, tn), lambda j, i, jn: (jn // per, j, jn % per))],
        out_specs=[pl.BlockSpec((tm, tko), lambda j, i, jn: (i, j))],
        out_shape=[jax.ShapeDtypeStruct((m, k), F32)], operands=[g, w],
        scratch_shapes=[pltpu.VMEM((tm, tko), F32)] * (nn > 1), exchanges=exchanges, after=after)
    return out, carried


def _mm_tn(a, g, nb, *, name, exchanges=(), after=None, part=0, n_parts=1):
    m, k = a.shape[0], a.shape[1] // n_parts
    n = g.shape[1]
    n_blk = n // nb
    tko, tn, tm = _tile(k, 512, MXU), _tile(n_blk, 1536, MXU), _tile(m, 4096, MXU)
    per = n_blk // tn
    nm = m // tm

    def kern(a_ref, g_ref, o_ref, *acc):
        def finish(total):
            o_ref[...] = total.astype(o_ref.dtype)

        part = lax.dot_general(a_ref[...], g_ref[...], (((0,), (0,)), ((), ())), preferred_element_type=F32)
        _accumulate(pl.program_id(2), nm, acc, part, finish)

    (out,), carried = _pallas(
        kern, name=name, grid=(nb * per, k // tko, nm),
        in_specs=[pl.BlockSpec((tm, tko), lambda j, i, im: (im, part * (k // tko) + i)),
                  pl.BlockSpec((tm, tn), lambda j, i, im: (im, j))],
        out_specs=[pl.BlockSpec((None, tko, tn), lambda j, i, im: (j // per, i, j % per))],
        out_shape=[jax.ShapeDtypeStruct((nb, k, n_blk), BF16)], operands=[a, g],
        scratch_shapes=[pltpu.VMEM((tko, tn), F32)] * (nm > 1), exchanges=exchanges, after=after)
    return out, carried


ROW_TILE = 128


def _rmsnorm_fwd(x, g, *, name):
    t, d = x.shape
    tr = _tile(t, ROW_TILE, SUBLANES)

    def kern(x_ref, g_ref, h_ref):
        xv = x_ref[...]
        r = lax.rsqrt(jnp.mean(xv * xv, axis=-1, keepdims=True) + EPS)
        h_ref[...] = (xv * r * g_ref[...]).astype(BF16)

    return pl.pallas_call(
        kern, name=name, grid=(t // tr,),
        in_specs=[pl.BlockSpec((tr, d), lambda i: (i, 0)), pl.BlockSpec((1, d), lambda i: (0, 0))],
        out_specs=pl.BlockSpec((tr, d), lambda i: (i, 0)),
        out_shape=jax.ShapeDtypeStruct((t, d), BF16), compiler_params=_params(1))(x, g)


def _rmsnorm_bwd(x, g, d_h_parts, d_res, *, name):
    t, d = x.shape
    tr = _tile(t, ROW_TILE, SUBLANES)
    n_parts = len(d_h_parts)

    def kern(x_ref, g_ref, *refs):
        dh_refs, (dres_ref, dx_ref, dxb_ref, dg_ref) = refs[:n_parts], refs[n_parts:]
        xv = x_ref[...]
        r = lax.rsqrt(jnp.mean(xv * xv, axis=-1, keepdims=True) + EPS)
        dh = jnp.concatenate([ref[...] for ref in dh_refs], axis=1)
        gy = dh * g_ref[...]
        dx = dres_ref[...] + r * gy - xv * (r * r * r) * jnp.mean(gy * xv, axis=-1, keepdims=True)
        dx_ref[...] = dx
        dxb_ref[...] = dx.astype(BF16)

        @pl.when(pl.program_id(0) == 0)
        def _():
            dg_ref[...] = jnp.zeros_like(dg_ref)

        dg_ref[...] += jnp.sum(dh * (xv * r), axis=0, keepdims=True)

    row = pl.BlockSpec((tr, d), lambda i: (i, 0))
    vec = pl.BlockSpec((1, d), lambda i: (0, 0))
    part = pl.BlockSpec((tr, d // n_parts), lambda i: (i, 0))
    return pl.pallas_call(
        kern, name=name, grid=(t // tr,), in_specs=[row, vec] + [part] * n_parts + [row], out_specs=[row, row, vec],
        out_shape=[jax.ShapeDtypeStruct((t, d), F32), jax.ShapeDtypeStruct((t, d), BF16),
                   jax.ShapeDtypeStruct((1, d), F32)], compiler_params=_params(1))(x, g, *d_h_parts, d_res)


def _loss_head(x, g, target, *, name):
    t, d = x.shape
    tr = _tile(t, ROW_TILE, SUBLANES)

    def kern(x_ref, g_ref, t_ref, dx_ref, dxb_ref, dg_ref, loss_ref):
        xv = x_ref[...]
        gv = g_ref[...]
        r = lax.rsqrt(jnp.mean(xv * xv, axis=-1, keepdims=True) + EPS)
        diff = xv * r * gv - t_ref[...]
        dy = diff * (1.0 / d)
        gy = dy * gv
        dx = r * gy - xv * (r * r * r) * jnp.mean(gy * xv, axis=-1, keepdims=True)
        dx_ref[...] = dx
        dxb_ref[...] = dx.astype(BF16)

        @pl.when(pl.program_id(0) == 0)
        def _():
            dg_ref[...] = jnp.zeros_like(dg_ref)
            loss_ref[...] = jnp.zeros_like(loss_ref)

        dg_ref[...] += jnp.sum(dy * (xv * r), axis=0, keepdims=True)
        part = 0.5 * jnp.sum(jnp.mean(diff * diff, axis=-1, keepdims=True), axis=0, keepdims=True)
        loss_ref[...] += jnp.broadcast_to(part, loss_ref.shape)

    row = pl.BlockSpec((tr, d), lambda i: (i, 0))
    vec = pl.BlockSpec((1, d), lambda i: (0, 0))
    return pl.pallas_call(
        kern, name=name, grid=(t // tr,), in_specs=[row, vec, row],
        out_specs=[row, row, vec, pl.BlockSpec((1, LANES), lambda i: (0, 0))],
        out_shape=[jax.ShapeDtypeStruct((t, d), F32), jax.ShapeDtypeStruct((t, d), BF16),
                   jax.ShapeDtypeStruct((1, d), F32), jax.ShapeDtypeStruct((1, LANES), F32)],
        compiler_params=_params(1))(x, g, target)


def _merge_fwd(m_a, m_b, proj, gate_col, *, name):
    t, d = m_a.shape
    tr = _tile(t, ROW_TILE, SUBLANES)

    def kern(ma_ref, mb_ref, ga_ref, gb_ref, o_ref):
        o_ref[...] = (_sigmoid(ga_ref[...]) * ma_ref[...] + _sigmoid(gb_ref[...]) * mb_ref[...]).astype(BF16)

    row = pl.BlockSpec((tr, d), lambda i: (i, 0))
    return pl.pallas_call(
        kern, name=name, grid=(t // tr,),
        in_specs=[row, row, pl.BlockSpec((tr, d), lambda i: (i, gate_col)),
                  pl.BlockSpec((tr, d), lambda i: (i, gate_col + 1))],
        out_specs=row, out_shape=jax.ShapeDtypeStruct((t, d), BF16), compiler_params=_params(1))(m_a, m_b, proj, proj)


def _merge_bwd(d_merged, m_a, m_b, proj, gate_col, *, name, after=None):
    t, d = m_a.shape
    tr = _tile(t, ROW_TILE, SUBLANES)

    def kern(dm_ref, ma_ref, mb_ref, ga_ref, gb_ref, dma_ref, dmb_ref, dg_ref):
        dm = dm_ref[...]
        sa = _sigmoid(ga_ref[...])
        sb = _sigmoid(gb_ref[...])
        dma_ref[...] = (dm * sa).astype(BF16)
        dmb_ref[...] = (dm * sb).astype(BF16)
        dg_ref[:, 0:d] = (dm * ma_ref[...] * (sa * (1.0 - sa))).astype(BF16)
        dg_ref[:, d:2 * d] = (dm * mb_ref[...] * (sb * (1.0 - sb))).astype(BF16)

    row = pl.BlockSpec((tr, d), lambda i: (i, 0))
    res, _ = _pallas(
        kern, name=name, grid=(t // tr,),
        in_specs=[row, row, row, pl.BlockSpec((tr, d), lambda i: (i, gate_col)),
                  pl.BlockSpec((tr, d), lambda i: (i, gate_col + 1))],
        out_specs=[row, row, pl.BlockSpec((tr, 2 * d), lambda i: (i, 0))],
        out_shape=[jax.ShapeDtypeStruct((t, d), BF16), jax.ShapeDtypeStruct((t, d), BF16),
                   jax.ShapeDtypeStruct((t, 2 * d), BF16)], operands=[d_merged, m_a, m_b, proj, proj], after=after)
    return res


def _tril_bf16(w, transposed):
    row = lax.broadcasted_iota(jnp.int32, w.shape, 0)
    col = lax.broadcasted_iota(jnp.int32, w.shape, 1)
    keep = (row <= col) if transposed else (row >= col)
    return jnp.where(keep, w, 0.0).astype(BF16)


def _layernorm_stats(v):
    mu = jnp.mean(v, axis=-1, keepdims=True)
    vc = v - mu
    rstd = lax.rsqrt(jnp.mean(vc * vc, axis=-1, keepdims=True) + EPS)
    return vc * rstd, rstd


def _mixer_a_fwd(proj, ln_g, ln_b, sg_w, sg_b_t, sgw, *, name):
    t = proj.shape[0]
    gd = sgw // SG_GROUPS

    def kern(zu_ref, zv_ref, g_ref, b_ref, w_ref, bt_ref, o_ref):
        xhat, _ = _layernorm_stats(_gelu(zv_ref[...]))
        vn = (xhat * g_ref[...] + b_ref[...]).astype(BF16)
        for g in range(SG_GROUPS):
            cols = slice(g * gd, (g + 1) * gd)
            mixed = jnp.dot(_tril_bf16(w_ref[g], False), vn[:, cols], preferred_element_type=F32) + bt_ref[:, g:g + 1]
            o_ref[:, cols] = (_gelu(zu_ref[:, cols]) * mixed).astype(BF16)

    vec = pl.BlockSpec((1, sgw), lambda i: (0, 0))
    return pl.pallas_call(
        kern, name=name, grid=(t // CHUNK,),
        in_specs=[pl.BlockSpec((CHUNK, sgw), lambda i: (i, 0)), pl.BlockSpec((CHUNK, sgw), lambda i: (i, 1)), vec, vec,
                  pl.BlockSpec((SG_GROUPS, CHUNK, CHUNK), lambda i: (0, 0, 0)),
                  pl.BlockSpec((CHUNK, SG_GROUPS), lambda i: (0, 0))],
        out_specs=pl.BlockSpec((CHUNK, sgw), lambda i: (i, 0)),
        out_shape=jax.ShapeDtypeStruct((t, sgw), BF16), compiler_params=_params(1))(proj, proj, ln_g, ln_b, sg_w, sg_b_t)


def _mixer_a_bwd(proj, d_ya, ln_g, ln_b, sg_w, sg_w_t, sg_b_t, sgw, *, name):
    t = proj.shape[0]
    gd = sgw // SG_GROUPS

    def kern(zu_ref, zv_ref, dy_ref, g_ref, b_ref, w_ref, wt_ref, bt_ref, dz_ref, dw_ref, dbt_ref, dg_ref, db_ref, dvn):
        @pl.when(pl.program_id(0) == 0)
        def _():
            dw_ref[...] = jnp.zeros_like(dw_ref)
            dbt_ref[...] = jnp.zeros_like(dbt_ref)
            dg_ref[...] = jnp.zeros_like(dg_ref)
            db_ref[...] = jnp.zeros_like(db_ref)

        gv, dgv = _gelu_and_grad(zv_ref[...])
        xhat, rstd = _layernorm_stats(gv)
        ln_gain = g_ref[...]
        vn = (xhat * ln_gain + b_ref[...]).astype(BF16)
        for g in range(SG_GROUPS):
            cols = slice(g * gd, (g + 1) * gd)
            gu, dgu = _gelu_and_grad(zu_ref[:, cols])
            mixed = jnp.dot(_tril_bf16(w_ref[g], False), vn[:, cols], preferred_element_type=F32) + bt_ref[:, g:g + 1]
            dy = dy_ref[:, cols]
            dz_ref[:, cols] = (dy * mixed * dgu).astype(BF16)
            d_mixed = dy * gu
            d_mixed_b = d_mixed.astype(BF16)
            dvn[:, cols] = jnp.dot(_tril_bf16(wt_ref[g], True), d_mixed_b, preferred_element_type=F32)
            d_w = lax.dot_general(d_mixed_b, vn[:, cols], (((1,), (1,)), ((), ())), preferred_element_type=F32)
            row = lax.broadcasted_iota(jnp.int32, d_w.shape, 0)
            col = lax.broadcasted_iota(jnp.int32, d_w.shape, 1)
            dw_ref[g] += jnp.where(row >= col, d_w, 0.0)
            dbt_ref[:, g:g + 1] += jnp.sum(d_mixed, axis=-1, keepdims=True)
        d_vn = dvn[...]
        dg_ref[...] += jnp.sum(d_vn * xhat, axis=0, keepdims=True)
        db_ref[...] += jnp.sum(d_vn, axis=0, keepdims=True)
        d_xhat = d_vn * ln_gain
        d_gv = rstd * (d_xhat - jnp.mean(d_xhat, axis=-1, keepdims=True)
                       - xhat * jnp.mean(d_xhat * xhat, axis=-1, keepdims=True))
        dz_ref[:, sgw:2 * sgw] = (d_gv * dgv).astype(BF16)

    vec = pl.BlockSpec((1, sgw), lambda i: (0, 0))
    wspec = pl.BlockSpec((SG_GROUPS, CHUNK, CHUNK), lambda i: (0, 0, 0))
    btspec = pl.BlockSpec((CHUNK, SG_GROUPS), lambda i: (0, 0))
    return pl.pallas_call(
        kern, name=name, grid=(t // CHUNK,),
        in_specs=[pl.BlockSpec((CHUNK, sgw), lambda i: (i, 0)), pl.BlockSpec((CHUNK, sgw), lambda i: (i, 1)),
                  pl.BlockSpec((CHUNK, sgw), lambda i: (i, 0)), vec, vec, wspec, wspec, btspec],
        out_specs=[pl.BlockSpec((CHUNK, 2 * sgw), lambda i: (i, 0)), wspec, btspec, vec, vec],
        out_shape=[jax.ShapeDtypeStruct((t, 2 * sgw), BF16), jax.ShapeDtypeStruct((SG_GROUPS, CHUNK, CHUNK), F32),
                   jax.ShapeDtypeStruct((CHUNK, SG_GROUPS), F32), jax.ShapeDtypeStruct((1, sgw), F32),
                   jax.ShapeDtypeStruct((1, sgw), F32)],
        scratch_shapes=[pltpu.VMEM((CHUNK, sgw), F32)],
        compiler_params=_params(1))(proj, proj, d_ya, ln_g, ln_b, sg_w, sg_w_t, sg_b_t)


def _scan_rows(a_ref, h_ref, reverse):
    s, c = a_ref.shape
    nblk = s // SUBLANES
    a, b = a_ref[...], h_ref[...]
    row = jnp.bitwise_and(lax.broadcasted_iota(jnp.int32, (s, c), 0), SUBLANES - 1)
    for d in (1, 2, 4):
        inside = (row < SUBLANES - d) if reverse else (row >= d)
        shift = s - d if reverse else d
        b = a * jnp.where(inside, pltpu.roll(b, shift, 0), 0.0) + b
        a = a * jnp.where(inside, pltpu.roll(a, shift, 0), 1.0)
    a_ref[...] = a
    h_ref[...] = b
    leaving = 0 if reverse else SUBLANES - 1

    def chain(i, carry):
        r0 = pl.multiple_of((nblk - 1 - i if reverse else i) * SUBLANES, SUBLANES)
        h = a_ref[pl.ds(r0, SUBLANES), :] * carry + h_ref[pl.ds(r0, SUBLANES), :]
        h_ref[pl.ds(r0, SUBLANES), :] = h
        return jnp.broadcast_to(h[leaving:leaving + 1, :], (SUBLANES, c))

    lax.fori_loop(0, nblk, chain, jnp.zeros((SUBLANES, c), F32))


def _lru_gates(xc, wa_ref, ba_ref, wx_ref, bx_ref, lam_ref):
    xcb = xc.astype(BF16)
    ra = _sigmoid(jnp.dot(xcb, wa_ref[...].astype(BF16), preferred_element_type=F32) + ba_ref[...])
    ia = _sigmoid(jnp.dot(xcb, wx_ref[...].astype(BF16), preferred_element_type=F32) + bx_ref[...])
    neg = -lam_ref[...]
    sp = jnp.maximum(neg, 0.0) + jnp.log1p(jnp.exp(-jnp.abs(neg)))
    log_a = -LRU_C * ra * sp
    a = jnp.exp(log_a)
    a2 = jnp.exp(2.0 * log_a)
    sq = jnp.sqrt(-jnp.tanh(log_a) * (a2 + 1.0))
    return ra, ia, sp, a, a2, sq


def _mixer_b_specs(seq, hd, sgw, lw):
    x_col = (2 * sgw) // hd
    y_col = (2 * sgw + lw) // hd
    tile = lambda col: pl.BlockSpec((seq, hd), lambda h, b: (b, col + h))
    vec = pl.BlockSpec((1, hd), lambda h, b: (0, h))
    mat = pl.BlockSpec((None, hd, hd), lambda h, b: (h, 0, 0))
    return tile(x_col), tile(y_col), tile(0), vec, mat


def _mixer_b_fwd(proj, conv_w, conv_b, wa, ba, wx, bx, lam, *, seq, sgw, lw, name):
    t = proj.shape[0]
    hd = lw // LRU_HEADS
    k_taps = conv_w.shape[0]
    x_spec, y_spec, o_spec, vec, mat = _mixer_b_specs(seq, hd, sgw, lw)

    def kern(xr_ref, yr_ref, cw_ref, cb_ref, wa_ref, ba_ref, wx_ref, bx_ref, lam_ref, o_ref, s_a, s_h):
        xc = _causal_conv(xr_ref[...], cw_ref[...], cb_ref[...])
        _, ia, _, a, _, sq = _lru_gates(xc, wa_ref, ba_ref, wx_ref, bx_ref, lam_ref)
        s_a[...] = a
        s_h[...] = sq * (ia * xc)
        _scan_rows(s_a, s_h, False)
        o_ref[...] = (s_h[...] * _gelu(yr_ref[...])).astype(BF16)

    return pl.pallas_call(
        kern, name=name, grid=(LRU_HEADS, t // seq),
        in_specs=[x_spec, y_spec, pl.BlockSpec((k_taps, hd), lambda h, b: (0, h)), vec, mat, vec, mat, vec, vec],
        out_specs=o_spec, out_shape=jax.ShapeDtypeStruct((t, lw), BF16),
        scratch_shapes=[pltpu.VMEM((seq, hd), F32), pltpu.VMEM((seq, hd), F32)],
        compiler_params=_params(2))(proj, proj, conv_w, conv_b, wa, ba, wx, bx, lam)


def _mixer_b_bwd(proj, d_yb, conv_w, conv_b, wa, wa_t, ba, wx, wx_t, bx, lam, *, seq, sgw, lw, name, exchanges=()):
    t = proj.shape[0]
    hd = lw // LRU_HEADS
    k_taps = conv_w.shape[0]
    x_spec, y_spec, o_spec, vec, mat = _mixer_b_specs(seq, hd, sgw, lw)
    cw_spec = pl.BlockSpec((k_taps, hd), lambda h, b: (0, h))

    def kern(xr_ref, yr_ref, dyb_ref, cw_ref, cb_ref, wa_ref, wat_ref, ba_ref, wx_ref, wxt_ref, bx_ref, lam_ref,
             dxr_ref, dyr_ref, dcw_ref, dcb_ref, dwa_ref, dba_ref, dwx_ref, dbx_ref, dlam_ref,
             s_xc, s_a, s_h, s_lam, s_dpa, s_dpx):
        @pl.when(pl.program_id(1) == 0)
        def _():
            for ref in (dcw_ref, dcb_ref, dwa_ref, dba_ref, dwx_ref, dbx_ref, dlam_ref):
                ref[...] = jnp.zeros_like(ref)

        s_xc[...] = _causal_conv(xr_ref[...], cw_ref[...], cb_ref[...])
        _, ia, _, a, _, sq = _lru_gates(s_xc[...], wa_ref, ba_ref, wx_ref, bx_ref, lam_ref)
        s_a[...] = a
        s_dpa[...] = _shift_up(a, 1)
        s_h[...] = sq * (ia * s_xc[...])
        _scan_rows(s_a, s_h, False)

        gel, dgel = _gelu_and_grad(yr_ref[...])
        dyb = dyb_ref[...]
        dyr_ref[...] = (dyb * s_h[...] * dgel).astype(BF16)
        s_lam[...] = dyb * gel
        _scan_rows(s_dpa, s_lam, True)
        ra, ia, sp, a, a2, sq = _lru_gates(s_xc[...], wa_ref, ba_ref, wx_ref, bx_ref, lam_ref)
        d_gx = s_lam[...]
        d_a = d_gx * _shift_down(s_h[...], 1)
        xc = s_xc[...]
        d_sq = d_gx * (ia * xc)
        d_ia = d_gx * (sq * xc)
        d_log_a = d_a * a - d_sq * (a2 / sq)
        d_ra = d_log_a * (-LRU_C * sp)
        d_sp = jnp.sum(d_log_a * (-LRU_C * ra), axis=0, keepdims=True)
        dlam_ref[...] += d_sp * (-_sigmoid(-lam_ref[...]))
        d_pa = d_ra * (ra * (1.0 - ra))
        d_px = d_ia * (ia * (1.0 - ia))
        s_dpa[...] = d_pa
        s_dpx[...] = d_px
        dba_ref[...] += jnp.sum(d_pa, axis=0, keepdims=True)
        dbx_ref[...] += jnp.sum(d_px, axis=0, keepdims=True)
        xcb = s_xc[...].astype(BF16)
        d_pa_b = s_dpa[...].astype(BF16)
        d_px_b = s_dpx[...].astype(BF16)
        contract_rows = (((0,), (0,)), ((), ()))
        dwa_ref[...] += lax.dot_general(xcb, d_pa_b, contract_rows, preferred_element_type=F32)
        dwx_ref[...] += lax.dot_general(xcb, d_px_b, contract_rows, preferred_element_type=F32)
        d_xc = (s_lam[...] * (sq * ia)
                + jnp.dot(d_pa_b, wat_ref[...].astype(BF16), preferred_element_type=F32)
                + jnp.dot(d_px_b, wxt_ref[...].astype(BF16), preferred_element_type=F32))
        dcb_ref[...] += jnp.sum(d_xc, axis=0, keepdims=True)
        dcw_ref[...] += _causal_conv_bwd_w(d_xc, xr_ref[...], k_taps)
        dxr_ref[...] = _causal_conv_bwd_x(d_xc, cw_ref[...]).astype(BF16)

    tile_shape = jax.ShapeDtypeStruct((t, lw), BF16)
    vec_shape = jax.ShapeDtypeStruct((1, lw), F32)
    mat_shape = jax.ShapeDtypeStruct((LRU_HEADS, hd, hd), F32)
    return _pallas(
        kern, name=name, grid=(LRU_HEADS, t // seq),
        in_specs=[x_spec, y_spec, o_spec, cw_spec, vec, mat, mat, vec, mat, mat, vec, vec],
        out_specs=[o_spec, o_spec, cw_spec, vec, mat, vec, mat, vec, vec],
        out_shape=[tile_shape, tile_shape, jax.ShapeDtypeStruct((k_taps, lw), F32), vec_shape, mat_shape, vec_shape,
                   mat_shape, vec_shape, vec_shape],
        operands=[proj, proj, d_yb, conv_w, conv_b, wa, wa_t, ba, wx, wx_t, bx, lam],
        scratch_shapes=[pltpu.VMEM((seq, hd), F32)] * 6, exchanges=exchanges)


FFN_TILE = 256


def _ffn_mid_fwd(up_pre, conv_w, conv_b, *, seq, name):
    t, f2 = up_pre.shape
    f = f2 // 2
    tc = _tile(f, FFN_TILE, LANES)
    nf = f // tc
    k_taps = conv_w.shape[0]

    def kern(pg_ref, pv_ref, wg_ref, wv_ref, bg_ref, bv_ref, o_ref):
        cg = _causal_conv(pg_ref[...], wg_ref[...], bg_ref[...])
        cv = _causal_conv(pv_ref[...], wv_ref[...], bv_ref[...])
        o_ref[...] = (_gelu(cg) * cv).astype(BF16)

    tile = lambda off: pl.BlockSpec((seq, tc), lambda j, b: (b, off + j))
    wspec = lambda off: pl.BlockSpec((k_taps, tc), lambda j, b: (0, off + j))
    bspec = lambda off: pl.BlockSpec((1, tc), lambda j, b: (0, off + j))
    return pl.pallas_call(
        kern, name=name, grid=(nf, t // seq),
        in_specs=[tile(0), tile(nf), wspec(0), wspec(nf), bspec(0), bspec(nf)], out_specs=tile(0),
        out_shape=jax.ShapeDtypeStruct((t, f), BF16),
        compiler_params=_params(2))(up_pre, up_pre, conv_w, conv_w, conv_b, conv_b)


def _ffn_mid_bwd(up_pre, d_act, conv_w, conv_b, *, seq, name):
    t, f2 = up_pre.shape
    f = f2 // 2
    tc = _tile(f, FFN_TILE, LANES)
    nf = f // tc
    k_taps = conv_w.shape[0]

    def kern(pg_ref, pv_ref, da_ref, wg_ref, wv_ref, bg_ref, bv_ref, dpg_ref, dpv_ref, dwg_ref, dwv_ref, dbg_ref, dbv_ref):
        @pl.when(pl.program_id(1) == 0)
        def _():
            for ref in (dwg_ref, dwv_ref, dbg_ref, dbv_ref):
                ref[...] = jnp.zeros_like(ref)

        pg = pg_ref[...]
        pv = pv_ref[...]
        gel, dgel = _gelu_and_grad(_causal_conv(pg, wg_ref[...], bg_ref[...]))
        cv = _causal_conv(pv, wv_ref[...], bv_ref[...])
        d_act_v = da_ref[...]
        d_cg = d_act_v * cv * dgel
        d_cv = d_act_v * gel
        dpg_ref[...] = _causal_conv_bwd_x(d_cg, wg_ref[...]).astype(BF16)
        dpv_ref[...] = _causal_conv_bwd_x(d_cv, wv_ref[...]).astype(BF16)
        dwg_ref[...] += _causal_conv_bwd_w(d_cg, pg, k_taps)
        dwv_ref[...] += _causal_conv_bwd_w(d_cv, pv, k_taps)
        dbg_ref[...] += jnp.sum(d_cg, axis=0, keepdims=True)
        dbv_ref[...] += jnp.sum(d_cv, axis=0, keepdims=True)

    tile = lambda off: pl.BlockSpec((seq, tc), lambda j, b: (b, off + j))
    wspec = lambda off: pl.BlockSpec((k_taps, tc), lambda j, b: (0, off + j))
    bspec = lambda off: pl.BlockSpec((1, tc), lambda j, b: (0, off + j))
    half = jax.ShapeDtypeStruct((t, f), BF16)
    wshape = jax.ShapeDtypeStruct((k_taps, f), F32)
    bshape = jax.ShapeDtypeStruct((1, f), F32)
    d_pg, d_pv, d_wg, d_wv, d_bg, d_bv = pl.pallas_call(
        kern, name=name, grid=(nf, t // seq),
        in_specs=[tile(0), tile(nf), tile(0), wspec(0), wspec(nf), bspec(0), bspec(nf)],
        out_specs=[tile(0), tile(0), wspec(0), wspec(0), bspec(0), bspec(0)],
        out_shape=[half, half, wshape, wshape, bshape, bshape],
        compiler_params=_params(2))(up_pre, up_pre, d_act, conv_w, conv_w, conv_b, conv_b)
    return (jnp.concatenate([d_pg, d_pv], axis=1), jnp.concatenate([d_wg, d_wv], axis=1),
            jnp.concatenate([d_bg, d_bv], axis=1))


ELEM_VMEM_BYTES = 24 << 20


def _as_2d(a):
    if a.ndim >= 2 and a.shape[-1] % LANES == 0 and a.size // a.shape[-1] >= SUBLANES:
        return a.reshape(-1, a.shape[-1])
    return a.reshape(-1, LANES)


def _row_tile(rows, bytes_per_row):
    return _tile(rows, max(16, ELEM_VMEM_BYTES // (2 * bytes_per_row)), 16)


def _cast_bf16(a, *, name, part=0, n_parts=1, after=None):
    v = _as_2d(a)
    rows, cols = v.shape[0] // n_parts, v.shape[1]
    tr = _row_tile(rows, cols * (4 + 2))
    first = part * (rows // tr)

    def kern(x_ref, o_ref):
        o_ref[...] = x_ref[...].astype(BF16)

    (out,), _ = _pallas(kern, name=name, grid=(rows // tr,), in_specs=[pl.BlockSpec((tr, cols), lambda i: (first + i, 0))],
                        out_specs=[pl.BlockSpec((tr, cols), lambda i: (i, 0))],
                        out_shape=[jax.ShapeDtypeStruct((rows, cols), BF16)], operands=[v], after=after)
    return out.reshape(a.shape) if n_parts == 1 else out


def _add_sibling_part(own, core, got, *, name):
    _, _, rows, cols = own.shape
    tr = _row_tile(rows, cols * (2 + 2 + 2))

    def kern(core_ref, a_ref, b_ref, o_ref):
        o_ref[...] = (a_ref[...].astype(F32) + b_ref[...].astype(F32)).astype(BF16)

    spec = pl.BlockSpec((None, tr, cols), lambda ch, i, core_ref: (ch, i, 0))
    grid_spec = pltpu.PrefetchScalarGridSpec(
        num_scalar_prefetch=1, grid=(4, rows // tr),
        in_specs=[pl.BlockSpec((None, None, tr, cols), lambda ch, i, core_ref: (ch, core_ref[0], i, 0)), spec],
        out_specs=spec)
    return pl.pallas_call(kern, name=name, grid_spec=grid_spec, out_shape=jax.ShapeDtypeStruct(got.shape, BF16),
                          compiler_params=_params(2))(core, own, got)


def _sum_parts(parts, *, name):
    n_parts, rows, cols = parts.shape
    tr = _row_tile(rows, cols * 4 * (n_parts + 1))

    def kern(p_ref, o_ref):
        acc = p_ref[0].astype(F32)
        for p in range(1, n_parts):
            acc = acc + p_ref[p].astype(F32)
        o_ref[...] = acc

    return pl.pallas_call(
        kern, name=name, grid=(rows // tr,), in_specs=[pl.BlockSpec((n_parts, tr, cols), lambda i: (0, i, 0))],
        out_specs=pl.BlockSpec((tr, cols), lambda i: (i, 0)), out_shape=jax.ShapeDtypeStruct((rows, cols), F32),
        compiler_params=_params(1))(parts)


def _adamw(w, m, v, grad_chunks, *, name):
    shape = w.shape
    w2 = _as_2d(w)
    rows, cols = w2.shape
    n_chunks = len(grad_chunks)
    n_parts = grad_chunks[0].shape[0]
    chunks = [c.reshape(n_parts, rows // n_chunks, cols) for c in grad_chunks]
    tr = _row_tile(rows // n_chunks, cols * (3 * 4 + n_chunks * n_parts * chunks[0].dtype.itemsize + 4 * 4))
    per_chunk = rows // n_chunks // tr
    c_m = 1.0 - ADAM_B1 ** ADAM_STEP
    c_v = 1.0 - ADAM_B2 ** ADAM_STEP

    def kern(w_ref, m_ref, v_ref, *refs):
        p_refs, (g_ref, d_ref, nm_ref, nv_ref) = refs[:n_chunks], refs[n_chunks:]
        g = None
        for k, p_ref in enumerate(p_refs):
            total = p_ref[0].astype(F32)
            for p in range(1, n_parts):
                total = total + p_ref[p].astype(F32)
            g = total if g is None else jnp.where(pl.program_id(0) // per_chunk == k, total, g)
        new_m = ADAM_B1 * m_ref[...] + (1.0 - ADAM_B1) * g
        new_v = ADAM_B2 * v_ref[...] + (1.0 - ADAM_B2) * (g * g)
        g_ref[...] = g
        nm_ref[...] = new_m
        nv_ref[...] = new_v
        d_ref[...] = -ADAM_LR * ((new_m / c_m) / (jnp.sqrt(new_v / c_v) + ADAM_EPS) + ADAM_WD * w_ref[...])

    spec = pl.BlockSpec((tr, cols), lambda i: (i, 0))
    out = jax.ShapeDtypeStruct((rows, cols), F32)
    def chunk_spec(k):
        return pl.BlockSpec((n_parts, tr, cols), lambda i: (0, jnp.clip(i - k * per_chunk, 0, per_chunk - 1), 0))

    res = pl.pallas_call(
        kern, name=name, grid=(rows // tr,),
        in_specs=[spec, spec, spec] + [chunk_spec(k) for k in range(n_chunks)],
        out_specs=[spec] * 4, out_shape=[out] * 4, compiler_params=_params(1))(w2, _as_2d(m), _as_2d(v), *chunks)
    return [r.reshape(shape) for r in res]


def _all_gather(shards, *, name):
    n = len(shards)

    def body(*refs):
        ins, outs = refs[:n], refs[n:2 * n]
        send_sems, recv_sems, local_sems = refs[2 * n:]
        x, y, c = _place()
        me, sibling = (x, y, c), (x, y, 1 - c)
        chips = [(1 - x, y), (x, 1 - y), (1 - x, 1 - y)]

        def slot(i, dev):
            return outs[i].at[4 * dev[0] + 2 * dev[1] + dev[2]]

        def copy(i, k, block, to, src=None):
            return pltpu.make_async_remote_copy(
                src_ref=slot(i, block) if src is None else src, dst_ref=slot(i, block),
                send_sem=send_sems.at[i, k], recv_sem=recv_sems.at[i, k], device_id=to, device_id_type=MESH)

        mine = [pltpu.make_async_copy(ins[i], slot(i, me), local_sems.at[i]) for i in range(n)]
        for cp in mine:
            cp.start()
        first = []
        for i in range(n):
            first.append(copy(i, 0, me, sibling, src=ins[i]))
            first += [copy(i, 1 + j, me, (*chip, c), src=ins[i]) for j, chip in enumerate(chips)]
        for cp in first:
            cp.start()
        passed = []
        for j, chip in enumerate(chips):
            for i in range(n):
                copy(i, 1 + j, (*chip, c), me).wait_recv()
                onward = copy(i, 4 + j, (*chip, c), sibling)
                onward.start()
                passed.append(onward)
        for i in range(n):
            copy(i, 0, sibling, me).wait_recv()
            for j, chip in enumerate(chips):
                copy(i, 4 + j, (*chip, 1 - c), me).wait_recv()
        for cp in first + passed:
            cp.wait_send()
        for cp in mine:
            cp.wait()

    return pl.pallas_call(
        body, name=name, in_specs=[ANY] * n, out_specs=[ANY] * n,
        out_shape=[jax.ShapeDtypeStruct((N_DEV,) + s.shape, s.dtype) for s in shards],
        scratch_shapes=[pltpu.SemaphoreType.DMA((n, 7)), pltpu.SemaphoreType.DMA((n, 7)), pltpu.SemaphoreType.DMA((n,))],
    )(*shards)


def _by_chip_and_core(grad):
    return grad.reshape(4, 2, -1, grad.shape[-1])


def _pack(vectors):
    flat = [v.reshape(-1).astype(F32) for v in vectors]
    sizes = [f.shape[0] for f in flat]
    total = sum(sizes)
    padded = -(-total // (SUBLANES * LANES)) * (SUBLANES * LANES)
    if padded > total:
        flat.append(jnp.zeros((padded - total,), F32))
    return jnp.concatenate(flat).reshape(-1, LANES), sizes


def _unpack(packed, sizes, shapes):
    flat = packed.reshape(-1)
    out, off = [], 0
    for size, shape in zip(sizes, shapes):
        out.append(flat[off:off + size].reshape(shape))
        off += size
    return out


def kernel(x, g_mix, w_in, sg_ln_g, sg_ln_b, sg_w, sg_b, lru_conv_w, lru_conv_b, lru_wa, lru_ba, lru_wx, lru_bx, lru_lam, p_sg, p_lru, w_out, g_ffn, w_up, ffn_conv_w, ffn_conv_b, w_down, g_final, loss_target, m_g_mix, m_w_in, m_sg_ln_g, m_sg_ln_b, m_sg_w, m_sg_b, m_lru_conv_w, m_lru_conv_b, m_lru_wa, m_lru_ba, m_lru_wx, m_lru_bx, m_lru_lam, m_p_sg, m_p_lru, m_w_out, m_g_ffn, m_w_up, m_ffn_conv_w, m_ffn_conv_b, m_w_down, m_g_final, v_g_mix, v_w_in, v_sg_ln_g, v_sg_ln_b, v_sg_w, v_sg_b, v_lru_conv_w, v_lru_conv_b, v_lru_wa, v_lru_ba, v_lru_wx, v_lru_bx, v_lru_lam, v_p_sg, v_p_lru, v_w_out, v_g_ffn, v_w_up, v_ffn_conv_w, v_ffn_conv_b, v_w_down, v_g_final):
    weights = dict(g_mix=g_mix, w_in=w_in, sg_ln_g=sg_ln_g, sg_ln_b=sg_ln_b, sg_w=sg_w, sg_b=sg_b, lru_conv_w=lru_conv_w,
                   lru_conv_b=lru_conv_b, lru_wa=lru_wa, lru_ba=lru_ba, lru_wx=lru_wx, lru_bx=lru_bx, lru_lam=lru_lam,
                   p_sg=p_sg, p_lru=p_lru, w_out=w_out, g_ffn=g_ffn, w_up=w_up, ffn_conv_w=ffn_conv_w,
                   ffn_conv_b=ffn_conv_b, w_down=w_down, g_final=g_final)
    m_in = dict(g_mix=m_g_mix, w_in=m_w_in, sg_ln_g=m_sg_ln_g, sg_ln_b=m_sg_ln_b, sg_w=m_sg_w, sg_b=m_sg_b,
                lru_conv_w=m_lru_conv_w, lru_conv_b=m_lru_conv_b, lru_wa=m_lru_wa, lru_ba=m_lru_ba, lru_wx=m_lru_wx,
                lru_bx=m_lru_bx, lru_lam=m_lru_lam, p_sg=m_p_sg, p_lru=m_p_lru, w_out=m_w_out, g_ffn=m_g_ffn,
                w_up=m_w_up, ffn_conv_w=m_ffn_conv_w, ffn_conv_b=m_ffn_conv_b, w_down=m_w_down, g_final=m_g_final)
    v_in = dict(g_mix=v_g_mix, w_in=v_w_in, sg_ln_g=v_sg_ln_g, sg_ln_b=v_sg_ln_b, sg_w=v_sg_w, sg_b=v_sg_b,
                lru_conv_w=v_lru_conv_w, lru_conv_b=v_lru_conv_b, lru_wa=v_lru_wa, lru_ba=v_lru_ba, lru_wx=v_lru_wx,
                lru_bx=v_lru_bx, lru_lam=v_lru_lam, p_sg=v_p_sg, p_lru=v_p_lru, w_out=v_w_out, g_ffn=v_g_ffn,
                w_up=v_w_up, ffn_conv_w=v_ffn_conv_w, ffn_conv_b=v_ffn_conv_b, w_down=v_w_down, g_final=v_g_final)
    order = list(weights)

    n_seq, seq, d = x.shape
    t = n_seq * seq
    sgw = sg_ln_g.shape[-1]
    lw = lru_lam.shape[-1]
    hd = lw // LRU_HEADS
    f2 = ffn_conv_b.shape[-1]
    gate_col = (2 * sgw + 2 * lw) // d
    xi, yi, ci = _place()
    dev = 4 * xi + 2 * yi + ci

    core = jnp.reshape(ci, (1,)).astype(jnp.int32)
    first_leg = functools.partial(_gather_first_leg, place_own=False)
    chip_swap = functools.partial(_swap_with_chips, place_own=False)
    shards, in_flight = {}, {}

    def landed(keys, after, name):
        return _await_exchange(first_leg, [in_flight[k] for k in keys], after, name=name)[1]

    needed_next = ["w_in_bottom", "lru_wa", "lru_wx", "p_sg", "taps"]
    by_need = ["w_in_top"] + needed_next + ["p_lru", "w_out", "w_up", "w_down"]
    def token():
        return jnp.zeros((SUBLANES, LANES), F32)

    def start_gather(keys, before, name):
        started, before = _start_exchange(first_leg, [shards[k] for k in keys],
                                          [_own_block_in_place(shards[k], dev) for k in keys], before, name=name)
        in_flight.update(zip(keys, started))
        return before

    shards["w_in_top"] = _cast_bf16(w_in[0], name="cast_w_in_top", part=0, n_parts=2)
    top_started = start_gather(["w_in_top"], token(), "start_gather_w_in_top")
    shards["w_in_bottom"] = _cast_bf16(w_in[0], name="cast_w_in_bottom", part=1, n_parts=2, after=top_started)
    shards.update({k: _cast_bf16(weights[k][0], name=f"cast_{k}", after=top_started) for k in by_need if k in weights})
    shards["taps"], tap_sizes = _pack([lru_conv_w[0], ffn_conv_w[0]])
    g_mix_after_start = start_gather(by_need[1:], g_mix, "start_gather_rest")

    def rows_in_order(g8):
        return g8.reshape(1, -1, g8.shape[-1])

    x2d = x.reshape(t, d)
    h1 = _rmsnorm_fwd(x2d, g_mix_after_start, name="norm_mix")
    ((w_in_top_g,),) = _exchange_now([_gather_second_leg(landed(["w_in_top"], h1, "await_w_in_top"))],
                                     name="second_leg_w_in_top")
    proj_top, _ = _mm_nn(h1, w_in_top_g, out_dtype=F32, name="proj_in_top")
    ((w_in_bottom_g, wa_8, wx_8, p_sg_g, taps_8),) = _exchange_now(
        [_gather_second_leg(landed(needed_next, proj_top, "await_w_in_bottom"))], name="second_leg_w_in_bottom")
    proj, _ = _mm_nn(h1, w_in_bottom_g, out_dtype=F32, residual=proj_top, a_part=1, name="proj_in")
    wa_g, wx_g = (jnp.swapaxes(w8, 0, 1).reshape(LRU_HEADS, hd, hd) for w8 in (wa_8, wx_8))
    wa_t, wx_t = jnp.swapaxes(wa_g, 1, 2), jnp.swapaxes(wx_g, 1, 2)
    tap_parts = [_unpack(taps_8[k], tap_sizes, [lru_conv_w.shape[1:], ffn_conv_w.shape[1:]]) for k in range(N_DEV)]
    lru_cw = jnp.concatenate([p[0] for p in tap_parts], axis=1)
    ffn_cw = jnp.concatenate([p[1] for p in tap_parts], axis=1)
    sg_w0 = sg_w[0]
    sg_w_t = jnp.swapaxes(sg_w0, 1, 2)
    sg_b_t = sg_b[0].T
    y_a = _mixer_a_fwd(proj, sg_ln_g, sg_ln_b, sg_w0, sg_b_t, sgw, name="mixer_a_fwd")
    y_b = _mixer_b_fwd(proj, lru_cw, lru_conv_b, wa_g, lru_ba, wx_g, lru_bx, lru_lam, seq=seq, sgw=sgw, lw=lw,
                       name="mixer_b_fwd")
    m_a, ((p_lru_8, w_out_8),) = _mm_nn(
        y_a, p_sg_g, out_dtype=F32, name="proj_sg",
        exchanges=[_gather_second_leg(landed(["p_lru", "w_out"], y_b, "await_p_lru_w_out"))])
    p_lru_g, w_out_g = rows_in_order(p_lru_8), rows_in_order(w_out_8)
    m_b, _ = _mm_nn(y_b, p_lru_g, out_dtype=F32, name="proj_lru")
    merged = _merge_fwd(m_a, m_b, proj, gate_col, name="merge_fwd")
    x1, ((w_up_g,),) = _mm_nn(merged, w_out_g, out_dtype=F32, residual=x2d, name="proj_out",
                              exchanges=[_gather_second_leg(landed(["w_up"], merged, "await_w_up"))])
    h2 = _rmsnorm_fwd(x1, g_ffn, name="norm_ffn")
    up_pre, _ = _mm_nn(h2, w_up_g, out_dtype=F32, name="ffn_up")
    act = _ffn_mid_fwd(up_pre, ffn_cw, ffn_conv_b, seq=seq, name="ffn_mid_fwd")
    ((w_down_8,),) = _exchange_now([_gather_second_leg(landed(["w_down"], act, "await_w_down"))], name="second_leg_w_down")
    w_down_g = rows_in_order(w_down_8)
    x2, _ = _mm_nn(act, w_down_g, out_dtype=F32, residual=x1, name="ffn_down")
    d_x2, d_x2_b, d_g_final, loss_part = _loss_head(x2, g_final.reshape(1, d), loss_target.reshape(t, d), name="loss_head")
    loss = lax.psum(loss_part[0, 0], ("x", "y", "c"))

    def by_rows(g):
        return g.reshape(N_DEV, -1, g.shape[-1])

    def by_head_rows(g):
        return jnp.swapaxes(g.reshape(LRU_HEADS, N_DEV, hd // N_DEV, hd), 0, 1)

    def start_sibling_swap(views, key):
        return _start_exchange(_swap_with_sibling, views, [lax.empty((4,) + v.shape[2:], v.dtype) for v in views], token(),
                               name=f"start_sibling_{key}")

    chip_swaps = {}

    def sum_and_start_chip_swap(keys, sibling_swap, done, ride=None):
        views, from_sibling = _await_exchange(_swap_with_sibling, sibling_swap, done, name=f"await_sibling_{keys[0]}")
        sums = [_add_sibling_part(v, core, s, name=f"chip_sum_{k}") for k, v, s in zip(keys, views, from_sibling)]
        started, ride = _start_exchange(chip_swap, sums, [_own_block_first(s, 2 * xi + yi) for s in sums],
                                        token() if ride is None else ride, name=f"start_chips_{keys[0]}")
        chip_swaps.update(zip(keys, started))
        return ride

    d_w_down, _ = _mm_tn(act, d_x2_b, 1, name="grad_w_down")
    v_down = _by_chip_and_core(by_rows(d_w_down))
    swap, started = start_sibling_swap([v_down], "w_down")
    d_act, _ = _mm_nt(d_x2_b, w_down_g, name="bwd_ffn_down", after=started)
    ffn_cb = sum_and_start_chip_swap(["w_down"], swap, d_act, ride=ffn_conv_b)
    d_up_pre, d_ffn_cw, d_ffn_cb = _ffn_mid_bwd(up_pre, d_act, ffn_cw, ffn_cb, seq=seq, name="ffn_mid_bwd")
    d_w_up, _ = _mm_tn(h2, d_up_pre, N_DEV, name="grad_w_up")
    v_up = _by_chip_and_core(d_w_up)
    swap, started = start_sibling_swap([v_up], "w_up")
    d_h2, _ = _mm_nt(d_up_pre, w_up_g, name="bwd_ffn_up", after=started)
    g_ffn_then = sum_and_start_chip_swap(["w_up"], swap, d_h2, ride=g_ffn)
    d_x1, d_x1_b, d_g_ffn = _rmsnorm_bwd(x1, g_ffn_then, [d_h2], d_x2, name="norm_ffn_bwd")
    d_w_out, _ = _mm_tn(merged, d_x1_b, 1, name="grad_w_out")
    v_out = _by_chip_and_core(by_rows(d_w_out))
    swap, started = start_sibling_swap([v_out], "w_out")
    d_merged, _ = _mm_nt(d_x1_b, w_out_g, name="bwd_proj_out", after=started)
    started = sum_and_start_chip_swap(["w_out"], swap, d_merged)
    d_m_a, d_m_b, d_gates = _merge_bwd(d_merged, m_a, m_b, proj, gate_col, name="merge_bwd", after=started)
    d_p_sg, _ = _mm_tn(y_a, d_m_a, N_DEV, name="grad_p_sg")
    d_p_lru, _ = _mm_tn(y_b, d_m_b, 1, name="grad_p_lru")
    v_sg, v_lru = _by_chip_and_core(d_p_sg), _by_chip_and_core(by_rows(d_p_lru))
    swap, started = start_sibling_swap([v_sg, v_lru], "p_sg")
    d_y_a, _ = _mm_nt(d_m_a, p_sg_g, name="bwd_proj_sg", after=started)
    d_y_b, _ = _mm_nt(d_m_b, p_lru_g, name="bwd_proj_lru")
    lru_cb = sum_and_start_chip_swap(["p_sg", "p_lru"], swap, d_y_b, ride=lru_conv_b)
    d_zuv, d_sg_w, d_sg_b_t, d_ln_g, d_ln_b = _mixer_a_bwd(proj, d_y_a, sg_ln_g, sg_ln_b, sg_w0, sg_w_t, sg_b_t, sgw,
                                                           name="mixer_a_bwd")
    (d_xr, d_yr, d_lru_cw, d_lru_cb, d_wa, d_ba, d_wx, d_bx, d_lam), _ = _mixer_b_bwd(
        proj, d_y_b, lru_cw, lru_cb, wa_g, wa_t, lru_ba, wx_g, wx_t, lru_bx, lru_lam, seq=seq, sgw=sgw, lw=lw,
        name="mixer_b_bwd")
    d_proj = jnp.concatenate([d_zuv, d_xr, d_yr, d_gates], axis=1)
    v_wa = _by_chip_and_core(_cast_bf16(by_head_rows(d_wa), name="cast_grad_wa"))
    v_wx = _by_chip_and_core(_cast_bf16(by_head_rows(d_wx), name="cast_grad_wx"))
    d_w_in_top, _ = _mm_tn(h1, d_proj, N_DEV, name="grad_w_in_top", part=0, n_parts=2)
    swap_top, started = start_sibling_swap([_by_chip_and_core(d_w_in_top)], "w_in_top")
    d_w_in_bottom, _ = _mm_tn(h1, d_proj, N_DEV, name="grad_w_in_bottom", part=1, n_parts=2, after=started)
    swap_bottom, started = start_sibling_swap([_by_chip_and_core(d_w_in_bottom), v_wa, v_wx], "w_in_bottom")
    started = sum_and_start_chip_swap(["w_in_top"], swap_top, started)
    d_h1_left, _ = _mm_nt(d_proj, w_in_top_g, name="bwd_proj_in_top", after=started)
    d_h1_right, _ = _mm_nt(d_proj, w_in_bottom_g, name="bwd_proj_in_bottom")
    g_mix_then = sum_and_start_chip_swap(["w_in_bottom", "lru_wa", "lru_wx"], swap_bottom, d_h1_right, ride=g_mix)
    grad_x, _, d_g_mix = _rmsnorm_bwd(x2d, g_mix_then, [d_h1_left, d_h1_right], d_x1, name="norm_mix_bwd")

    small = ["g_mix", "sg_ln_g", "sg_ln_b", "sg_w", "sg_b", "lru_conv_b", "lru_ba", "lru_bx", "lru_lam", "g_ffn",
             "ffn_conv_b", "g_final", "lru_conv_w", "ffn_conv_w"]
    small_parts = dict(g_mix=d_g_mix, sg_ln_g=d_ln_g, sg_ln_b=d_ln_b, sg_w=d_sg_w, sg_b=d_sg_b_t.T, lru_conv_b=d_lru_cb,
                       lru_ba=d_ba, lru_bx=d_bx, lru_lam=d_lam, g_ffn=d_g_ffn, ffn_conv_b=d_ffn_cb, g_final=d_g_final,
                       lru_conv_w=d_lru_cw, ffn_conv_w=d_ffn_cw)
    packed, sizes = _pack([small_parts[k] for k in small])
    (all_small,) = _all_gather([packed], name="gather_small_grads")
    small_sum = _sum_parts(all_small, name="sum_small_grads")
    small_grads = dict(zip(small, _unpack(small_sum, sizes, [small_parts[k].shape for k in small])))
    for k in ("lru_conv_w", "ffn_conv_w"):
        n_loc = weights[k].shape[-1]
        small_grads[k] = lax.dynamic_slice_in_dim(small_grads[k], dev * n_loc, n_loc, axis=1)

    grads, deltas, new_m, new_v = {}, {}, {}, {}

    def update(k, chunks):
        grads[k], deltas[k], new_m[k], new_v[k] = _adamw(weights[k], m_in[k], v_in[k], chunks, name=f"adamw_{k}")

    def chip_sums_landed(keys, after):
        _, parts = _await_exchange(chip_swap, [chip_swaps[k] for k in keys], after, name=f"await_chips_{keys[0]}")
        return dict(zip(keys, parts))

    for keys in (["w_down"], ["w_up"], ["w_out"], ["p_sg", "p_lru"]):
        for k, parts in chip_sums_landed(keys, grad_x).items():
            update(k, [parts])
    for k in small:
        update(k, [small_grads[k][None]])
    last = chip_sums_landed(["w_in_top", "w_in_bottom", "lru_wa", "lru_wx"], deltas["w_up"])
    update("w_in", [last["w_in_top"], last["w_in_bottom"]])
    update("lru_wa", [last["lru_wa"]])
    update("lru_wx", [last["lru_wx"]])

    return (loss, grad_x.reshape(x.shape), *[grads[k] for k in order], *[deltas[k] for k in order],
            *[new_m[k] for k in order], *[new_v[k] for k in order])
```

```python
import functools
import math
from typing import Callable, NamedTuple

import jax
import jax.numpy as jnp
from jax import lax
from jax.experimental import pallas as pl
from jax.experimental.pallas import tpu as pltpu

F32 = jnp.float32
BF16 = jnp.bfloat16
MESH = pl.DeviceIdType.MESH
ANY = pl.BlockSpec(memory_space=pl.ANY)

N_DEV = 8
EPS = 1e-6
CHUNK = 128
SG_GROUPS = 8
LRU_HEADS = 16
LRU_C = 8.0
ADAM_LR = 0.001
ADAM_B1 = 0.9
ADAM_B2 = 0.999
ADAM_EPS = 1e-08
ADAM_WD = 0.01
ADAM_STEP = 10

V7X_VMEM_LIMIT = 56 * 1024 * 1024
LANES = 128
SUBLANES = 8
MXU = 256

_GELU_C0 = math.sqrt(2.0 / math.pi)
_GELU_C1 = 0.044715


def _params(n_axes):
    return pltpu.CompilerParams(dimension_semantics=("arbitrary",) * n_axes, vmem_limit_bytes=V7X_VMEM_LIMIT)


def _tile(dim, pref, align):
    t = (min(pref, dim) // align) * align
    while t >= align:
        if dim % t == 0:
            return t
        t -= align
    return dim


def _gelu(x):
    return x * (0.5 * (1.0 + jnp.tanh(_GELU_C0 * (x + _GELU_C1 * (x * x * x)))))


def _gelu_and_grad(x):
    t = jnp.tanh(_GELU_C0 * (x + _GELU_C1 * (x * x * x)))
    cdf = 0.5 * (1.0 + t)
    dcdf = 0.5 * (1.0 - t * t) * (_GELU_C0 * (1.0 + 3.0 * _GELU_C1 * (x * x)))
    return x * cdf, cdf + x * dcdf


def _sigmoid(x):
    return 1.0 / (1.0 + jnp.exp(-x))


def _shift_down(x, d):
    if d == 0:
        return x
    row = lax.broadcasted_iota(jnp.int32, x.shape, 0)
    return jnp.where(row >= d, pltpu.roll(x, d, 0), 0.0)


def _shift_up(x, d):
    if d == 0:
        return x
    s = x.shape[0]
    row = lax.broadcasted_iota(jnp.int32, x.shape, 0)
    return jnp.where(row < s - d, pltpu.roll(x, s - d, 0), 0.0)


def _causal_conv(x, w, b):
    k_taps = w.shape[0]
    out = _shift_down(x, k_taps - 1) * w[0:1, :]
    for k in range(1, k_taps):
        out = out + _shift_down(x, k_taps - 1 - k) * w[k:k + 1, :]
    return out + b


def _causal_conv_bwd_x(d_out, w):
    k_taps = w.shape[0]
    d_x = _shift_up(d_out, k_taps - 1) * w[0:1, :]
    for k in range(1, k_taps):
        d_x = d_x + _shift_up(d_out, k_taps - 1 - k) * w[k:k + 1, :]
    return d_x


def _causal_conv_bwd_w(d_out, x, k_taps):
    rows = [jnp.sum(d_out * _shift_down(x, k_taps - 1 - k), axis=0, keepdims=True) for k in range(k_taps)]
    return jnp.concatenate(rows, axis=0)


def _place():
    return lax.axis_index("x"), lax.axis_index("y"), lax.axis_index("c")


def _other_chips(x, y):
    return [(1 - x, y), (x, 1 - y), (1 - x, 1 - y)]


class _Exchange(NamedTuple):
    ins: tuple
    outs: tuple
    in_place: bool
    n_remote: int
    n_local: int
    copies: Callable


def _remote(src, dst, send_sem, recv_sem, to):
    return pltpu.make_async_remote_copy(src_ref=src, dst_ref=dst, send_sem=send_sem, recv_sem=recv_sem, device_id=to,
                                        device_id_type=MESH)


def _gather_first_leg(shards, place_own=True):
    n = len(shards)

    def copies(ins, outs, send_sems, recv_sems, local_sems):
        x, y, c = _place()
        peers = [(x, y, 1 - c)] + [(*chip, c) for chip in _other_chips(x, y)]
        slot = lambda i, dev: outs[i].at[4 * dev[0] + 2 * dev[1] + dev[2]]
        sends = [_remote(ins[i], slot(i, (x, y, c)), send_sems[i].at[k], recv_sems[i].at[k], to)
                 for i in range(n) for k, to in enumerate(peers)]
        receives = [_remote(ins[i], slot(i, frm), send_sems[i].at[k], recv_sems[i].at[k], frm)
                    for i in range(n) for k, frm in enumerate(peers)]
        local = [pltpu.make_async_copy(ins[i], slot(i, (x, y, c)), local_sems[i].at[0]) for i in range(n)] if place_own else []
        return sends, receives, local

    outs = tuple(jax.ShapeDtypeStruct((N_DEV,) + s.shape, s.dtype) for s in shards)
    return _Exchange(tuple(shards), outs, False, 4, int(place_own), copies)


def _gather_second_leg(gathered):
    n = len(gathered)

    def copies(ins, outs, send_sems, recv_sems, local_sems):
        x, y, c = _place()
        slot = lambda i, chip, core: outs[i].at[4 * chip[0] + 2 * chip[1] + core]
        sends = [_remote(slot(i, chip, c), slot(i, chip, c), send_sems[i].at[j], recv_sems[i].at[j], (x, y, 1 - c))
                 for i in range(n) for j, chip in enumerate(_other_chips(x, y))]
        receives = [_remote(slot(i, chip, 1 - c), slot(i, chip, 1 - c), send_sems[i].at[j], recv_sems[i].at[j], (x, y, 1 - c))
                    for i in range(n) for j, chip in enumerate(_other_chips(x, y))]
        return sends, receives, []

    outs = tuple(jax.ShapeDtypeStruct(g.shape, g.dtype) for g in gathered)
    return _Exchange(tuple(gathered), outs, True, 3, 0, copies)


def _swap_with_sibling(parts):
    n = len(parts)

    def copies(ins, outs, send_sems, recv_sems, local_sems):
        x, y, c = _place()
        both = [_remote(ins[i].at[ch, 1 - c], outs[i].at[ch], send_sems[i].at[ch], recv_sems[i].at[ch], (x, y, 1 - c))
                for i in range(n) for ch in range(4)]
        return both, both, []

    outs = tuple(jax.ShapeDtypeStruct((4,) + p.shape[2:], p.dtype) for p in parts)
    return _Exchange(tuple(parts), outs, False, 4, 0, copies)


def _swap_with_chips(parts, place_own=True):
    n = len(parts)

    def copies(ins, outs, send_sems, recv_sems, local_sems):
        x, y, c = _place()
        both = [_remote(ins[i].at[2 * chip[0] + chip[1]], outs[i].at[1 + j], send_sems[i].at[j], recv_sems[i].at[j], (*chip, c))
                for i in range(n) for j, chip in enumerate(_other_chips(x, y))]
        local = [pltpu.make_async_copy(ins[i].at[2 * x + y], outs[i].at[0], local_sems[i].at[0]) for i in range(n)] if place_own else []
        return both, both, local

    outs = tuple(jax.ShapeDtypeStruct(p.shape, p.dtype) for p in parts)
    return _Exchange(tuple(parts), outs, False, 3, int(place_own), copies)


def _exchange_plumbing(exchanges):
    operands = [a for ex in exchanges for a in ex.ins]
    results = [s for ex in exchanges for s in ex.outs]
    scratch, in_place, at = [], {}, 0
    for ex in exchanges:
        n = len(ex.ins)
        scratch += [pltpu.SemaphoreType.DMA((n, ex.n_remote))] * 2
        if ex.n_local:
            scratch.append(pltpu.SemaphoreType.DMA((n, ex.n_local)))
        if ex.in_place:
            in_place.update({at + i: at + i for i in range(n)})
        at += n

    def copies(in_refs, out_refs, sem_refs):
        sends, receives, local = [], [], []
        at, sem_at = 0, 0
        for ex in exchanges:
            n, n_sem = len(ex.ins), 3 if ex.n_local else 2
            per_operand = [[sem.at[i] for i in range(n)] for sem in sem_refs[sem_at:sem_at + n_sem]] + [[]] * (3 - n_sem)
            s, r, l = ex.copies(in_refs[at:at + n], out_refs[at:at + n], *per_operand)
            sends, receives, local = sends + s, receives + r, local + l
            at, sem_at = at + n, sem_at + n_sem
        return sends, receives, local

    return operands, results, scratch, in_place, copies


def _start_all(copies):
    sends, _, local = copies
    for cp in local + sends:
        cp.start()


def _wait_all(copies):
    sends, receives, local = copies
    for cp in receives:
        cp.wait_recv()
    for cp in sends:
        cp.wait_send()
    for cp in local:
        cp.wait()


def _exchange_now(exchanges, *, name):
    operands, results, scratch, in_place, copies = _exchange_plumbing(exchanges)
    n = len(operands)

    def body(*refs):
        made = copies(refs[:n], refs[n:2 * n], refs[2 * n:])
        _start_all(made)
        _wait_all(made)

    out = pl.pallas_call(body, name=name, in_specs=[ANY] * n, out_specs=[ANY] * n, out_shape=results,
                         scratch_shapes=scratch, input_output_aliases=in_place)(*operands)
    return _split(out, exchanges)


def _split(flat, exchanges):
    out, at = [], 0
    for ex in exchanges:
        out.append(list(flat[at:at + len(ex.ins)]))
        at += len(ex.ins)
    return out


HBM = pl.BlockSpec(memory_space=pltpu.HBM)
SEMAPHORES = pl.BlockSpec(memory_space=pltpu.SEMAPHORE)
SPLIT_COPY = pltpu.CompilerParams(has_side_effects=pltpu.SideEffectType.DATAFLOW_SIDE_EFFECTING)


def _start_exchange(make, operands, landings, before, *, name):
    n = len(operands)
    ex = make(operands)

    def body(*refs):
        sends, _, _ = ex.copies(refs[:n], refs[n:2 * n], refs[2 * n + 1:3 * n + 1], refs[3 * n + 1:4 * n + 1], [])
        for cp in sends:
            cp.start()

    buffers = [pltpu.with_memory_space_constraint(a, pltpu.HBM) for a in list(operands) + list(landings) + [before]]
    out = pl.pallas_call(
        body, name=name, in_specs=[HBM] * (2 * n + 1), out_specs=[SEMAPHORES] * (2 * n) + [HBM] * (2 * n + 1),
        out_shape=[pltpu.SemaphoreType.DMA((ex.n_remote,))] * (2 * n) + [pltpu.HBM(a.shape, a.dtype) for a in buffers],
        input_output_aliases={i: 2 * n + i for i in range(2 * n + 1)}, compiler_params=SPLIT_COPY)(*buffers)
    return [(out[i], out[n + i], out[2 * n + i], out[3 * n + i]) for i in range(n)], out[4 * n]


def _await_exchange(make, in_flight, after, *, name):
    n = len(in_flight)
    send_sems, recv_sems, operands, landings = zip(*in_flight)
    ex = make(operands)

    def body(*refs):
        sends, receives, _ = ex.copies(refs[:n], refs[n:2 * n], refs[2 * n:3 * n], refs[3 * n:4 * n], [])
        for cp in receives:
            cp.wait_recv()
        for cp in sends:
            cp.wait_send()

    out = pl.pallas_call(
        body, name=name, in_specs=[HBM] * (2 * n) + [SEMAPHORES] * (2 * n) + [ANY], out_specs=[HBM] * (2 * n),
        out_shape=[pltpu.HBM(a.shape, a.dtype) for a in operands + landings],
        input_output_aliases={i: i for i in range(2 * n)}, compiler_params=SPLIT_COPY,
    )(*operands, *landings, *send_sems, *recv_sems, after)
    return list(out[:n]), list(out[n:])


def _own_block_first(blocks, index):
    own = lax.dynamic_index_in_dim(blocks, index, 0, keepdims=True)
    return lax.dynamic_update_index_in_dim(lax.empty(blocks.shape, blocks.dtype), own, 0, 0)


def _own_block_in_place(shard, index):
    return lax.dynamic_update_index_in_dim(lax.empty((N_DEV,) + shard.shape, shard.dtype), shard, index, 0)


def _pallas(kern, *, name, grid, in_specs, out_specs, out_shape, operands, scratch_shapes=(), exchanges=(), after=None):
    ex_operands, ex_results, ex_scratch, in_place, copies = _exchange_plumbing(exchanges)
    if after is not None:
        ex_operands = [after] + ex_operands
        in_place = {i + 1: o for i, o in in_place.items()}
    n_in, n_out, n_scratch, n_ex = len(in_specs), len(out_specs), len(scratch_shapes), len(ex_results)
    n_unread = len(ex_operands) - n_ex

    def body(*refs):
        ins, refs = refs[:n_in], refs[n_in + n_unread:]
        ex_ins, refs = refs[:n_ex], refs[n_ex:]
        outs, refs = refs[:n_out], refs[n_out:]
        ex_outs, refs = refs[:n_ex], refs[n_ex:]
        scratch, sems = refs[:n_scratch], refs[n_scratch:]
        if exchanges:
            first = functools.reduce(jnp.logical_and, [pl.program_id(a) == 0 for a in range(len(grid))])
            last = functools.reduce(jnp.logical_and, [pl.program_id(a) == g - 1 for a, g in enumerate(grid)])

            @pl.when(first)
            def _():
                _start_all(copies(ex_ins, ex_outs, sems))

        kern(*ins, *outs, *scratch)
        if exchanges:
            @pl.when(last)
            def _():
                _wait_all(copies(ex_ins, ex_outs, sems))

    res = pl.pallas_call(
        body, name=name, grid=grid, in_specs=list(in_specs) + [ANY] * len(ex_operands), out_specs=list(out_specs) + [ANY] * n_ex,
        out_shape=list(out_shape) + ex_results, scratch_shapes=list(scratch_shapes) + ex_scratch,
        input_output_aliases={n_in + i: n_out + o for i, o in in_place.items()},
        compiler_params=_params(len(grid)))(*operands, *ex_operands)
    return list(res[:n_out]), _split(res[n_out:], exchanges)


def _accumulate(step, n_steps, acc, value, finish):
    if n_steps == 1:
        finish(value)
        return

    @pl.when(step == 0)
    def _():
        acc[0][...] = value

    @pl.when(step > 0)
    def _():
        acc[0][...] += value

    @pl.when(step == n_steps - 1)
    def _():
        finish(acc[0][...])


def _mm_nn(a, w, *, out_dtype, name, residual=None, exchanges=(), a_part=0):
    m = a.shape[0]
    nb, k, n_blk = w.shape
    tm, tn, tk = _tile(m, 512, MXU), _tile(n_blk, 1536, MXU), _tile(k, 4096, MXU)
    per = n_blk // tn
    nk = k // tk

    def kern(*refs):
        a_ref, w_ref = refs[:2]
        r_ref = None if residual is None else refs[2]
        o_ref, acc = refs[2 + (residual is not None)], refs[3 + (residual is not None):]

        def finish(total):
            o_ref[...] = (total if r_ref is None else total + r_ref[...]).astype(o_ref.dtype)

        _accumulate(pl.program_id(2), nk, acc, jnp.dot(a_ref[...], w_ref[...], preferred_element_type=F32), finish)

    tile = pl.BlockSpec((tm, tn), lambda j, i, kk: (i, j))
    in_specs = [pl.BlockSpec((tm, tk), lambda j, i, kk: (i, a_part * nk + kk)),
                pl.BlockSpec((None, tk, tn), lambda j, i, kk: (j // per, kk, j % per))]
    operands = [a, w]
    if residual is not None:
        in_specs.append(tile)
        operands.append(residual)
    (out,), carried = _pallas(
        kern, name=name, grid=(nb * per, m // tm, nk), in_specs=in_specs, out_specs=[tile],
        out_shape=[jax.ShapeDtypeStruct((m, nb * n_blk), out_dtype)], operands=operands,
        scratch_shapes=[pltpu.VMEM((tm, tn), F32)] * (nk > 1), exchanges=exchanges)
    return out, carried


def _mm_nt(g, w, *, name, exchanges=(), after=None):
    m, n = g.shape
    nb, k, n_blk = w.shape
    tm, tko, tn = _tile(m, 1024, MXU), _tile(k, 1024, MXU), _tile(n_blk, 3072, MXU)
    per = n_blk // tn
    nn = n // tn

    def kern(g_ref, w_ref, o_ref, *acc):
        def finish(total):
            o_ref[...] = total

        part = lax.dot_general(g_ref[...], w_ref[...], (((1,), (1,)), ((), ())), preferred_element_type=F32)
        _accumulate(pl.program_id(2), nn, acc, part, finish)

    (out,), carried = _pallas(
        kern, name=name, grid=(k // tko, m // tm, nn),
        in_specs=[pl.BlockSpec((tm, tn), lambda j, i, jn: (i, jn)),
                  pl.BlockSpec((None, tko, tn), lambda j, i, jn: (jn // per, j, jn % per))],
        out_specs=[pl.BlockSpec((tm, tko), lambda j, i, jn: (i, j))],
        out_shape=[jax.ShapeDtypeStruct((m, k), F32)], operands=[g, w],
        scratch_shapes=[pltpu.VMEM((tm, tko), F32)] * (nn > 1), exchanges=exchanges, after=after)
    return out, carried


def _mm_tn(a, g, nb, *, name, exchanges=(), after=None, part=0, n_parts=1):
    m, k = a.shape[0], a.shape[1] // n_parts
    n = g.shape[1]
    n_blk = n // nb
    tko, tn, tm = _tile(k, 512, MXU), _tile(n_blk, 1536, MXU), _tile(m, 4096, MXU)
    per = n_blk // tn
    nm = m // tm

    def kern(a_ref, g_ref, o_ref, *acc):
        def finish(total):
            o_ref[...] = total.astype(o_ref.dtype)

        part = lax.dot_general(a_ref[...], g_ref[...], (((0,), (0,)), ((), ())), preferred_element_type=F32)
        _accumulate(pl.program_id(2), nm, acc, part, finish)

    (out,), carried = _pallas(
        kern, name=name, grid=(nb * per, k // tko, nm),
        in_specs=[pl.BlockSpec((tm, tko), lambda j, i, im: (im, part * (k // tko) + i)),
                  pl.BlockSpec((tm, tn), lambda j, i, im: (im, j))],
        out_specs=[pl.BlockSpec((None, tko, tn), lambda j, i, im: (j // per, i, j % per))],
        out_shape=[jax.ShapeDtypeStruct((nb, k, n_blk), BF16)], operands=[a, g],
        scratch_shapes=[pltpu.VMEM((tko, tn), F32)] * (nm > 1), exchanges=exchanges, after=after)
    return out, carried


ROW_TILE = 128


def _rmsnorm_fwd(x, g, *, name):
    t, d = x.shape
    tr = _tile(t, ROW_TILE, SUBLANES)

    def kern(x_ref, g_ref, h_ref):
        xv = x_ref[...]
        r = lax.rsqrt(jnp.mean(xv * xv, axis=-1, keepdims=True) + EPS)
        h_ref[...] = (xv * r * g_ref[...]).astype(BF16)

    return pl.pallas_call(
        kern, name=name, grid=(t // tr,),
        in_specs=[pl.BlockSpec((tr, d), lambda i: (i, 0)), pl.BlockSpec((1, d), lambda i: (0, 0))],
        out_specs=pl.BlockSpec((tr, d), lambda i: (i, 0)),
        out_shape=jax.ShapeDtypeStruct((t, d), BF16), compiler_params=_params(1))(x, g)


def _rmsnorm_bwd(x, g, d_h_parts, d_res, *, name):
    t, d = x.shape
    tr = _tile(t, ROW_TILE, SUBLANES)
    n_parts = len(d_h_parts)

    def kern(x_ref, g_ref, *refs):
        dh_refs, (dres_ref, dx_ref, dxb_ref, dg_ref) = refs[:n_parts], refs[n_parts:]
        xv = x_ref[...]
        r = lax.rsqrt(jnp.mean(xv * xv, axis=-1, keepdims=True) + EPS)
        dh = jnp.concatenate([ref[...] for ref in dh_refs], axis=1)
        gy = dh * g_ref[...]
        dx = dres_ref[...] + r * gy - xv * (r * r * r) * jnp.mean(gy * xv, axis=-1, keepdims=True)
        dx_ref[...] = dx
        dxb_ref[...] = dx.astype(BF16)

        @pl.when(pl.program_id(0) == 0)
        def _():
            dg_ref[...] = jnp.zeros_like(dg_ref)

        dg_ref[...] += jnp.sum(dh * (xv * r), axis=0, keepdims=True)

    row = pl.BlockSpec((tr, d), lambda i: (i, 0))
    vec = pl.BlockSpec((1, d), lambda i: (0, 0))
    part = pl.BlockSpec((tr, d // n_parts), lambda i: (i, 0))
    return pl.pallas_call(
        kern, name=name, grid=(t // tr,), in_specs=[row, vec] + [part] * n_parts + [row], out_specs=[row, row, vec],
        out_shape=[jax.ShapeDtypeStruct((t, d), F32), jax.ShapeDtypeStruct((t, d), BF16),
                   jax.ShapeDtypeStruct((1, d), F32)], compiler_params=_params(1))(x, g, *d_h_parts, d_res)


def _loss_head(x, g, target, *, name):
    t, d = x.shape
    tr = _tile(t, ROW_TILE, SUBLANES)

    def kern(x_ref, g_ref, t_ref, dx_ref, dxb_ref, dg_ref, loss_ref):
        xv = x_ref[...]
        gv = g_ref[...]
        r = lax.rsqrt(jnp.mean(xv * xv, axis=-1, keepdims=True) + EPS)
        diff = xv * r * gv - t_ref[...]
        dy = diff * (1.0 / d)
        gy = dy * gv
        dx = r * gy - xv * (r * r * r) * jnp.mean(gy * xv, axis=-1, keepdims=True)
        dx_ref[...] = dx
        dxb_ref[...] = dx.astype(BF16)

        @pl.when(pl.program_id(0) == 0)
        def _():
            dg_ref[...] = jnp.zeros_like(dg_ref)
            loss_ref[...] = jnp.zeros_like(loss_ref)

        dg_ref[...] += jnp.sum(dy * (xv * r), axis=0, keepdims=True)
        part = 0.5 * jnp.sum(jnp.mean(diff * diff, axis=-1, keepdims=True), axis=0, keepdims=True)
        loss_ref[...] += jnp.broadcast_to(part, loss_ref.shape)

    row = pl.BlockSpec((tr, d), lambda i: (i, 0))
    vec = pl.BlockSpec((1, d), lambda i: (0, 0))
    return pl.pallas_call(
        kern, name=name, grid=(t // tr,), in_specs=[row, vec, row],
        out_specs=[row, row, vec, pl.BlockSpec((1, LANES), lambda i: (0, 0))],
        out_shape=[jax.ShapeDtypeStruct((t, d), F32), jax.ShapeDtypeStruct((t, d), BF16),
                   jax.ShapeDtypeStruct((1, d), F32), jax.ShapeDtypeStruct((1, LANES), F32)],
        compiler_params=_params(1))(x, g, target)


def _merge_fwd(m_a, m_b, proj, gate_col, *, name):
    t, d = m_a.shape
    tr = _tile(t, ROW_TILE, SUBLANES)

    def kern(ma_ref, mb_ref, ga_ref, gb_ref, o_ref):
        o_ref[...] = (_sigmoid(ga_ref[...]) * ma_ref[...] + _sigmoid(gb_ref[...]) * mb_ref[...]).astype(BF16)

    row = pl.BlockSpec((tr, d), lambda i: (i, 0))
    return pl.pallas_call(
        kern, name=name, grid=(t // tr,),
        in_specs=[row, row, pl.BlockSpec((tr, d), lambda i: (i, gate_col)),
                  pl.BlockSpec((tr, d), lambda i: (i, gate_col + 1))],
        out_specs=row, out_shape=jax.ShapeDtypeStruct((t, d), BF16), compiler_params=_params(1))(m_a, m_b, proj, proj)


def _merge_bwd(d_merged, m_a, m_b, proj, gate_col, *, name, after=None):
    t, d = m_a.shape
    tr = _tile(t, ROW_TILE, SUBLANES)

    def kern(dm_ref, ma_ref, mb_ref, ga_ref, gb_ref, dma_ref, dmb_ref, dg_ref):
        dm = dm_ref[...]
        sa = _sigmoid(ga_ref[...])
        sb = _sigmoid(gb_ref[...])
        dma_ref[...] = (dm * sa).astype(BF16)
        dmb_ref[...] = (dm * sb).astype(BF16)
        dg_ref[:, 0:d] = (dm * ma_ref[...] * (sa * (1.0 - sa))).astype(BF16)
        dg_ref[:, d:2 * d] = (dm * mb_ref[...] * (sb * (1.0 - sb))).astype(BF16)

    row = pl.BlockSpec((tr, d), lambda i: (i, 0))
    res, _ = _pallas(
        kern, name=name, grid=(t // tr,),
        in_specs=[row, row, row, pl.BlockSpec((tr, d), lambda i: (i, gate_col)),
                  pl.BlockSpec((tr, d), lambda i: (i, gate_col + 1))],
        out_specs=[row, row, pl.BlockSpec((tr, 2 * d), lambda i: (i, 0))],
        out_shape=[jax.ShapeDtypeStruct((t, d), BF16), jax.ShapeDtypeStruct((t, d), BF16),
                   jax.ShapeDtypeStruct((t, 2 * d), BF16)], operands=[d_merged, m_a, m_b, proj, proj], after=after)
    return res


def _tril_bf16(w, transposed):
    row = lax.broadcasted_iota(jnp.int32, w.shape, 0)
    col = lax.broadcasted_iota(jnp.int32, w.shape, 1)
    keep = (row <= col) if transposed else (row >= col)
    return jnp.where(keep, w, 0.0).astype(BF16)


def _layernorm_stats(v):
    mu = jnp.mean(v, axis=-1, keepdims=True)
    vc = v - mu
    rstd = lax.rsqrt(jnp.mean(vc * vc, axis=-1, keepdims=True) + EPS)
    return vc * rstd, rstd


def _mixer_a_fwd(proj, ln_g, ln_b, sg_w, sg_b_t, sgw, *, name):
    t = proj.shape[0]
    gd = sgw // SG_GROUPS

    def kern(zu_ref, zv_ref, g_ref, b_ref, w_ref, bt_ref, o_ref):
        xhat, _ = _layernorm_stats(_gelu(zv_ref[...]))
        vn = (xhat * g_ref[...] + b_ref[...]).astype(BF16)
        for g in range(SG_GROUPS):
            cols = slice(g * gd, (g + 1) * gd)
            mixed = jnp.dot(_tril_bf16(w_ref[g], False), vn[:, cols], preferred_element_type=F32) + bt_ref[:, g:g + 1]
            o_ref[:, cols] = (_gelu(zu_ref[:, cols]) * mixed).astype(BF16)

    vec = pl.BlockSpec((1, sgw), lambda i: (0, 0))
    return pl.pallas_call(
        kern, name=name, grid=(t // CHUNK,),
        in_specs=[pl.BlockSpec((CHUNK, sgw), lambda i: (i, 0)), pl.BlockSpec((CHUNK, sgw), lambda i: (i, 1)), vec, vec,
                  pl.BlockSpec((SG_GROUPS, CHUNK, CHUNK), lambda i: (0, 0, 0)),
                  pl.BlockSpec((CHUNK, SG_GROUPS), lambda i: (0, 0))],
        out_specs=pl.BlockSpec((CHUNK, sgw), lambda i: (i, 0)),
        out_shape=jax.ShapeDtypeStruct((t, sgw), BF16), compiler_params=_params(1))(proj, proj, ln_g, ln_b, sg_w, sg_b_t)


def _mixer_a_bwd(proj, d_ya, ln_g, ln_b, sg_w, sg_w_t, sg_b_t, sgw, *, name):
    t = proj.shape[0]
    gd = sgw // SG_GROUPS

    def kern(zu_ref, zv_ref, dy_ref, g_ref, b_ref, w_ref, wt_ref, bt_ref, dz_ref, dw_ref, dbt_ref, dg_ref, db_ref, dvn):
        @pl.when(pl.program_id(0) == 0)
        def _():
            dw_ref[...] = jnp.zeros_like(dw_ref)
            dbt_ref[...] = jnp.zeros_like(dbt_ref)
            dg_ref[...] = jnp.zeros_like(dg_ref)
            db_ref[...] = jnp.zeros_like(db_ref)

        gv, dgv = _gelu_and_grad(zv_ref[...])
        xhat, rstd = _layernorm_stats(gv)
        ln_gain = g_ref[...]
        vn = (xhat * ln_gain + b_ref[...]).astype(BF16)
        for g in range(SG_GROUPS):
            cols = slice(g * gd, (g + 1) * gd)
            gu, dgu = _gelu_and_grad(zu_ref[:, cols])
            mixed = jnp.dot(_tril_bf16(w_ref[g], False), vn[:, cols], preferred_element_type=F32) + bt_ref[:, g:g + 1]
            dy = dy_ref[:, cols]
            dz_ref[:, cols] = (dy * mixed * dgu).astype(BF16)
            d_mixed = dy * gu
            d_mixed_b = d_mixed.astype(BF16)
            dvn[:, cols] = jnp.dot(_tril_bf16(wt_ref[g], True), d_mixed_b, preferred_element_type=F32)
            d_w = lax.dot_general(d_mixed_b, vn[:, cols], (((1,), (1,)), ((), ())), preferred_element_type=F32)
            row = lax.broadcasted_iota(jnp.int32, d_w.shape, 0)
            col = lax.broadcasted_iota(jnp.int32, d_w.shape, 1)
            dw_ref[g] += jnp.where(row >= col, d_w, 0.0)
            dbt_ref[:, g:g + 1] += jnp.sum(d_mixed, axis=-1, keepdims=True)
        d_vn = dvn[...]
        dg_ref[...] += jnp.sum(d_vn * xhat, axis=0, keepdims=True)
        db_ref[...] += jnp.sum(d_vn, axis=0, keepdims=True)
        d_xhat = d_vn * ln_gain
        d_gv = rstd * (d_xhat - jnp.mean(d_xhat, axis=-1, keepdims=True)
                       - xhat * jnp.mean(d_xhat * xhat, axis=-1, keepdims=True))
        dz_ref[:, sgw:2 * sgw] = (d_gv * dgv).astype(BF16)

    vec = pl.BlockSpec((1, sgw), lambda i: (0, 0))
    wspec = pl.BlockSpec((SG_GROUPS, CHUNK, CHUNK), lambda i: (0, 0, 0))
    btspec = pl.BlockSpec((CHUNK, SG_GROUPS), lambda i: (0, 0))
    return pl.pallas_call(
        kern, name=name, grid=(t // CHUNK,),
        in_specs=[pl.BlockSpec((CHUNK, sgw), lambda i: (i, 0)), pl.BlockSpec((CHUNK, sgw), lambda i: (i, 1)),
                  pl.BlockSpec((CHUNK, sgw), lambda i: (i, 0)), vec, vec, wspec, wspec, btspec],
        out_specs=[pl.BlockSpec((CHUNK, 2 * sgw), lambda i: (i, 0)), wspec, btspec, vec, vec],
        out_shape=[jax.ShapeDtypeStruct((t, 2 * sgw), BF16), jax.ShapeDtypeStruct((SG_GROUPS, CHUNK, CHUNK), F32),
                   jax.ShapeDtypeStruct((CHUNK, SG_GROUPS), F32), jax.ShapeDtypeStruct((1, sgw), F32),
                   jax.ShapeDtypeStruct((1, sgw), F32)],
        scratch_shapes=[pltpu.VMEM((CHUNK, sgw), F32)],
        compiler_params=_params(1))(proj, proj, d_ya, ln_g, ln_b, sg_w, sg_w_t, sg_b_t)


def _scan_rows(a_ref, h_ref, reverse):
    s, c = a_ref.shape
    nblk = s // SUBLANES
    a, b = a_ref[...], h_ref[...]
    row = jnp.bitwise_and(lax.broadcasted_iota(jnp.int32, (s, c), 0), SUBLANES - 1)
    for d in (1, 2, 4):
        inside = (row < SUBLANES - d) if reverse else (row >= d)
        shift = s - d if reverse else d
        b = a * jnp.where(inside, pltpu.roll(b, shift, 0), 0.0) + b
        a = a * jnp.where(inside, pltpu.roll(a, shift, 0), 1.0)
    a_ref[...] = a
    h_ref[...] = b
    leaving = 0 if reverse else SUBLANES - 1

    def chain(i, carry):
        r0 = pl.multiple_of((nblk - 1 - i if reverse else i) * SUBLANES, SUBLANES)
        h = a_ref[pl.ds(r0, SUBLANES), :] * carry + h_ref[pl.ds(r0, SUBLANES), :]
        h_ref[pl.ds(r0, SUBLANES), :] = h
        return jnp.broadcast_to(h[leaving:leaving + 1, :], (SUBLANES, c))

    lax.fori_loop(0, nblk, chain, jnp.zeros((SUBLANES, c), F32))


def _lru_gates(xc, wa_ref, ba_ref, wx_ref, bx_ref, lam_ref):
    xcb = xc.astype(BF16)
    ra = _sigmoid(jnp.dot(xcb, wa_ref[...].astype(BF16), preferred_element_type=F32) + ba_ref[...])
    ia = _sigmoid(jnp.dot(xcb, wx_ref[...].astype(BF16), preferred_element_type=F32) + bx_ref[...])
    neg = -lam_ref[...]
    sp = jnp.maximum(neg, 0.0) + jnp.log1p(jnp.exp(-jnp.abs(neg)))
    log_a = -LRU_C * ra * sp
    a = jnp.exp(log_a)
    a2 = jnp.exp(2.0 * log_a)
    sq = jnp.sqrt(-jnp.tanh(log_a) * (a2 + 1.0))
    return ra, ia, sp, a, a2, sq


def _mixer_b_specs(seq, hd, sgw, lw):
    x_col = (2 * sgw) // hd
    y_col = (2 * sgw + lw) // hd
    tile = lambda col: pl.BlockSpec((seq, hd), lambda h, b: (b, col + h))
    vec = pl.BlockSpec((1, hd), lambda h, b: (0, h))
    mat = pl.BlockSpec((None, hd, hd), lambda h, b: (h, 0, 0))
    return tile(x_col), tile(y_col), tile(0), vec, mat


def _mixer_b_fwd(proj, conv_w, conv_b, wa, ba, wx, bx, lam, *, seq, sgw, lw, name):
    t = proj.shape[0]
    hd = lw // LRU_HEADS
    k_taps = conv_w.shape[0]
    x_spec, y_spec, o_spec, vec, mat = _mixer_b_specs(seq, hd, sgw, lw)

    def kern(xr_ref, yr_ref, cw_ref, cb_ref, wa_ref, ba_ref, wx_ref, bx_ref, lam_ref, o_ref, s_a, s_h):
        xc = _causal_conv(xr_ref[...], cw_ref[...], cb_ref[...])
        _, ia, _, a, _, sq = _lru_gates(xc, wa_ref, ba_ref, wx_ref, bx_ref, lam_ref)
        s_a[...] = a
        s_h[...] = sq * (ia * xc)
        _scan_rows(s_a, s_h, False)
        o_ref[...] = (s_h[...] * _gelu(yr_ref[...])).astype(BF16)

    return pl.pallas_call(
        kern, name=name, grid=(LRU_HEADS, t // seq),
        in_specs=[x_spec, y_spec, pl.BlockSpec((k_taps, hd), lambda h, b: (0, h)), vec, mat, vec, mat, vec, vec],
        out_specs=o_spec, out_shape=jax.ShapeDtypeStruct((t, lw), BF16),
        scratch_shapes=[pltpu.VMEM((seq, hd), F32), pltpu.VMEM((seq, hd), F32)],
        compiler_params=_params(2))(proj, proj, conv_w, conv_b, wa, ba, wx, bx, lam)


def _mixer_b_bwd(proj, d_yb, conv_w, conv_b, wa, wa_t, ba, wx, wx_t, bx, lam, *, seq, sgw, lw, name, exchanges=()):
    t = proj.shape[0]
    hd = lw // LRU_HEADS
    k_taps = conv_w.shape[0]
    x_spec, y_spec, o_spec, vec, mat = _mixer_b_specs(seq, hd, sgw, lw)
    cw_spec = pl.BlockSpec((k_taps, hd), lambda h, b: (0, h))

    def kern(xr_ref, yr_ref, dyb_ref, cw_ref, cb_ref, wa_ref, wat_ref, ba_ref, wx_ref, wxt_ref, bx_ref, lam_ref,
             dxr_ref, dyr_ref, dcw_ref, dcb_ref, dwa_ref, dba_ref, dwx_ref, dbx_ref, dlam_ref,
             s_xc, s_a, s_h, s_lam, s_dpa, s_dpx):
        @pl.when(pl.program_id(1) == 0)
        def _():
            for ref in (dcw_ref, dcb_ref, dwa_ref, dba_ref, dwx_ref, dbx_ref, dlam_ref):
                ref[...] = jnp.zeros_like(ref)

        s_xc[...] = _causal_conv(xr_ref[...], cw_ref[...], cb_ref[...])
        _, ia, _, a, _, sq = _lru_gates(s_xc[...], wa_ref, ba_ref, wx_ref, bx_ref, lam_ref)
        s_a[...] = a
        s_dpa[...] = _shift_up(a, 1)
        s_h[...] = sq * (ia * s_xc[...])
        _scan_rows(s_a, s_h, False)

        gel, dgel = _gelu_and_grad(yr_ref[...])
        dyb = dyb_ref[...]
        dyr_ref[...] = (dyb * s_h[...] * dgel).astype(BF16)
        s_lam[...] = dyb * gel
        _scan_rows(s_dpa, s_lam, True)
        ra, ia, sp, a, a2, sq = _lru_gates(s_xc[...], wa_ref, ba_ref, wx_ref, bx_ref, lam_ref)
        d_gx = s_lam[...]
        d_a = d_gx * _shift_down(s_h[...], 1)
        xc = s_xc[...]
        d_sq = d_gx * (ia * xc)
        d_ia = d_gx * (sq * xc)
        d_log_a = d_a * a - d_sq * (a2 / sq)
        d_ra = d_log_a * (-LRU_C * sp)
        d_sp = jnp.sum(d_log_a * (-LRU_C * ra), axis=0, keepdims=True)
        dlam_ref[...] += d_sp * (-_sigmoid(-lam_ref[...]))
        d_pa = d_ra * (ra * (1.0 - ra))
        d_px = d_ia * (ia * (1.0 - ia))
        s_dpa[...] = d_pa
        s_dpx[...] = d_px
        dba_ref[...] += jnp.sum(d_pa, axis=0, keepdims=True)
        dbx_ref[...] += jnp.sum(d_px, axis=0, keepdims=True)
        xcb = s_xc[...].astype(BF16)
        d_pa_b = s_dpa[...].astype(BF16)
        d_px_b = s_dpx[...].astype(BF16)
        contract_rows = (((0,), (0,)), ((), ()))
        dwa_ref[...] += lax.dot_general(xcb, d_pa_b, contract_rows, preferred_element_type=F32)
        dwx_ref[...] += lax.dot_general(xcb, d_px_b, contract_rows, preferred_element_type=F32)
        d_xc = (s_lam[...] * (sq * ia)
                + jnp.dot(d_pa_b, wat_ref[...].astype(BF16), preferred_element_type=F32)
                + jnp.dot(d_px_b, wxt_ref[...].astype(BF16), preferred_element_type=F32))
        dcb_ref[...] += jnp.sum(d_xc, axis=0, keepdims=True)
        dcw_ref[...] += _causal_conv_bwd_w(d_xc, xr_ref[...], k_taps)
        dxr_ref[...] = _causal_conv_bwd_x(d_xc, cw_ref[...]).astype(BF16)

    tile_shape = jax.ShapeDtypeStruct((t, lw), BF16)
    vec_shape = jax.ShapeDtypeStruct((1, lw), F32)
    mat_shape = jax.ShapeDtypeStruct((LRU_HEADS, hd, hd), F32)
    return _pallas(
        kern, name=name, grid=(LRU_HEADS, t // seq),
        in_specs=[x_spec, y_spec, o_spec, cw_spec, vec, mat, mat, vec, mat, mat, vec, vec],
        out_specs=[o_spec, o_spec, cw_spec, vec, mat, vec, mat, vec, vec],
        out_shape=[tile_shape, tile_shape, jax.ShapeDtypeStruct((k_taps, lw), F32), vec_shape, mat_shape, vec_shape,
                   mat_shape, vec_shape, vec_shape],
        operands=[proj, proj, d_yb, conv_w, conv_b, wa, wa_t, ba, wx, wx_t, bx, lam],
        scratch_shapes=[pltpu.VMEM((seq, hd), F32)] * 6, exchanges=exchanges)


FFN_TILE = 256


def _ffn_mid_fwd(up_pre, conv_w, conv_b, *, seq, name, exchanges=()):
    t, f2 = up_pre.shape
    f = f2 // 2
    tc = _tile(f, FFN_TILE, LANES)
    nf = f // tc
    k_taps = conv_w.shape[0]

    def kern(pg_ref, pv_ref, wg_ref, wv_ref, bg_ref, bv_ref, o_ref):
        cg = _causal_conv(pg_ref[...], wg_ref[...], bg_ref[...])
        cv = _causal_conv(pv_ref[...], wv_ref[...], bv_ref[...])
        o_ref[...] = (_gelu(cg) * cv).astype(BF16)

    tile = lambda off: pl.BlockSpec((seq, tc), lambda j, b: (b, off + j))
    wspec = lambda off: pl.BlockSpec((k_taps, tc), lambda j, b: (0, off + j))
    bspec = lambda off: pl.BlockSpec((1, tc), lambda j, b: (0, off + j))
    (act,), carried = _pallas(
        kern, name=name, grid=(nf, t // seq),
        in_specs=[tile(0), tile(nf), wspec(0), wspec(nf), bspec(0), bspec(nf)], out_specs=[tile(0)],
        out_shape=[jax.ShapeDtypeStruct((t, f), BF16)],
        operands=[up_pre, up_pre, conv_w, conv_w, conv_b, conv_b], exchanges=exchanges)
    return act, carried


def _ffn_mid_bwd(up_pre, d_act, conv_w, conv_b, *, seq, name):
    t, f2 = up_pre.shape
    f = f2 // 2
    tc = _tile(f, FFN_TILE, LANES)
    nf = f // tc
    k_taps = conv_w.shape[0]

    def kern(pg_ref, pv_ref, da_ref, wg_ref, wv_ref, bg_ref, bv_ref, dpg_ref, dpv_ref, dwg_ref, dwv_ref, dbg_ref, dbv_ref):
        @pl.when(pl.program_id(1) == 0)
        def _():
            for ref in (dwg_ref, dwv_ref, dbg_ref, dbv_ref):
                ref[...] = jnp.zeros_like(ref)

        pg = pg_ref[...]
        pv = pv_ref[...]
        gel, dgel = _gelu_and_grad(_causal_conv(pg, wg_ref[...], bg_ref[...]))
        cv = _causal_conv(pv, wv_ref[...], bv_ref[...])
        d_act_v = da_ref[...]
        d_cg = d_act_v * cv * dgel
        d_cv = d_act_v * gel
        dpg_ref[...] = _causal_conv_bwd_x(d_cg, wg_ref[...]).astype(BF16)
        dpv_ref[...] = _causal_conv_bwd_x(d_cv, wv_ref[...]).astype(BF16)
        dwg_ref[...] += _causal_conv_bwd_w(d_cg, pg, k_taps)
        dwv_ref[...] += _causal_conv_bwd_w(d_cv, pv, k_taps)
        dbg_ref[...] += jnp.sum(d_cg, axis=0, keepdims=True)
        dbv_ref[...] += jnp.sum(d_cv, axis=0, keepdims=True)

    tile = lambda off: pl.BlockSpec((seq, tc), lambda j, b: (b, off + j))
    wspec = lambda off: pl.BlockSpec((k_taps, tc), lambda j, b: (0, off + j))
    bspec = lambda off: pl.BlockSpec((1, tc), lambda j, b: (0, off + j))
    half = jax.ShapeDtypeStruct((t, f), BF16)
    wshape = jax.ShapeDtypeStruct((k_taps, f), F32)
    bshape = jax.ShapeDtypeStruct((1, f), F32)
    d_pg, d_pv, d_wg, d_wv, d_bg, d_bv = pl.pallas_call(
        kern, name=name, grid=(nf, t // seq),
        in_specs=[tile(0), tile(nf), tile(0), wspec(0), wspec(nf), bspec(0), bspec(nf)],
        out_specs=[tile(0), tile(0), wspec(0), wspec(0), bspec(0), bspec(0)],
        out_shape=[half, half, wshape, wshape, bshape, bshape],
        compiler_params=_params(2))(up_pre, up_pre, d_act, conv_w, conv_w, conv_b, conv_b)
    return (jnp.concatenate([d_pg, d_pv], axis=1), jnp.concatenate([d_wg, d_wv], axis=1),
            jnp.concatenate([d_bg, d_bv], axis=1))


ELEM_VMEM_BYTES = 24 << 20


def _as_2d(a):
    if a.ndim >= 2 and a.shape[-1] % LANES == 0 and a.size // a.shape[-1] >= SUBLANES:
        return a.reshape(-1, a.shape[-1])
    return a.reshape(-1, LANES)


def _row_tile(rows, bytes_per_row):
    return _tile(rows, max(16, ELEM_VMEM_BYTES // (2 * bytes_per_row)), 16)


def _cast_bf16(a, *, name, part=0, n_parts=1, after=None):
    v = _as_2d(a)
    rows, cols = v.shape[0] // n_parts, v.shape[1]
    tr = _row_tile(rows, cols * (4 + 2))
    first = part * (rows // tr)

    def kern(x_ref, o_ref):
        o_ref[...] = x_ref[...].astype(BF16)

    (out,), _ = _pallas(kern, name=name, grid=(rows // tr,), in_specs=[pl.BlockSpec((tr, cols), lambda i: (first + i, 0))],
                        out_specs=[pl.BlockSpec((tr, cols), lambda i: (i, 0))],
                        out_shape=[jax.ShapeDtypeStruct((rows, cols), BF16)], operands=[v], after=after)
    return out.reshape(a.shape) if n_parts == 1 else out


def _add_sibling_part(own, core, got, *, name):
    _, _, rows, cols = own.shape
    tr = _row_tile(rows, cols * (2 + 2 + 2))

    def kern(core_ref, a_ref, b_ref, o_ref):
        o_ref[...] = (a_ref[...].astype(F32) + b_ref[...].astype(F32)).astype(BF16)

    spec = pl.BlockSpec((None, tr, cols), lambda ch, i, core_ref: (ch, i, 0))
    grid_spec = pltpu.PrefetchScalarGridSpec(
        num_scalar_prefetch=1, grid=(4, rows // tr),
        in_specs=[pl.BlockSpec((None, None, tr, cols), lambda ch, i, core_ref: (ch, core_ref[0], i, 0)), spec],
        out_specs=spec)
    return pl.pallas_call(kern, name=name, grid_spec=grid_spec, out_shape=jax.ShapeDtypeStruct(got.shape, BF16),
                          compiler_params=_params(2))(core, own, got)


def _sum_parts(parts, *, name):
    n_parts, rows, cols = parts.shape
    tr = _row_tile(rows, cols * 4 * (n_parts + 1))

    def kern(p_ref, o_ref):
        acc = p_ref[0].astype(F32)
        for p in range(1, n_parts):
            acc = acc + p_ref[p].astype(F32)
        o_ref[...] = acc

    return pl.pallas_call(
        kern, name=name, grid=(rows // tr,), in_specs=[pl.BlockSpec((n_parts, tr, cols), lambda i: (0, i, 0))],
        out_specs=pl.BlockSpec((tr, cols), lambda i: (i, 0)), out_shape=jax.ShapeDtypeStruct((rows, cols), F32),
        compiler_params=_params(1))(parts)


def _adamw(w, m, v, grad_chunks, *, name):
    shape = w.shape
    w2 = _as_2d(w)
    rows, cols = w2.shape
    n_chunks = len(grad_chunks)
    n_parts = grad_chunks[0].shape[0]
    chunks = [c.reshape(n_parts, rows // n_chunks, cols) for c in grad_chunks]
    tr = _row_tile(rows // n_chunks, cols * (3 * 4 + n_chunks * n_parts * chunks[0].dtype.itemsize + 4 * 4))
    per_chunk = rows // n_chunks // tr
    c_m = 1.0 - ADAM_B1 ** ADAM_STEP
    c_v = 1.0 - ADAM_B2 ** ADAM_STEP

    def kern(w_ref, m_ref, v_ref, *refs):
        p_refs, (g_ref, d_ref, nm_ref, nv_ref) = refs[:n_chunks], refs[n_chunks:]
        g = None
        for k, p_ref in enumerate(p_refs):
            total = p_ref[0].astype(F32)
            for p in range(1, n_parts):
                total = total + p_ref[p].astype(F32)
            g = total if g is None else jnp.where(pl.program_id(0) // per_chunk == k, total, g)
        new_m = ADAM_B1 * m_ref[...] + (1.0 - ADAM_B1) * g
        new_v = ADAM_B2 * v_ref[...] + (1.0 - ADAM_B2) * (g * g)
        g_ref[...] = g
        nm_ref[...] = new_m
        nv_ref[...] = new_v
        d_ref[...] = -ADAM_LR * ((new_m / c_m) / (jnp.sqrt(new_v / c_v) + ADAM_EPS) + ADAM_WD * w_ref[...])

    spec = pl.BlockSpec((tr, cols), lambda i: (i, 0))
    out = jax.ShapeDtypeStruct((rows, cols), F32)
    def chunk_spec(k):
        return pl.BlockSpec((n_parts, tr, cols), lambda i: (0, jnp.clip(i - k * per_chunk, 0, per_chunk - 1), 0))

    res = pl.pallas_call(
        kern, name=name, grid=(rows // tr,),
        in_specs=[spec, spec, spec] + [chunk_spec(k) for k in range(n_chunks)],
        out_specs=[spec] * 4, out_shape=[out] * 4, compiler_params=_params(1))(w2, _as_2d(m), _as_2d(v), *chunks)
    return [r.reshape(shape) for r in res]


def _all_gather(shards, *, name):
    n = len(shards)

    def body(*refs):
        ins, outs = refs[:n], refs[n:2 * n]
        send_sems, recv_sems, local_sems = refs[2 * n:]
        x, y, c = _place()
        me, sibling = (x, y, c), (x, y, 1 - c)
        chips = [(1 - x, y), (x, 1 - y), (1 - x, 1 - y)]

        def slot(i, dev):
            return outs[i].at[4 * dev[0] + 2 * dev[1] + dev[2]]

        def copy(i, k, block, to, src=None):
            return pltpu.make_async_remote_copy(
                src_ref=slot(i, block) if src is None else src, dst_ref=slot(i, block),
                send_sem=send_sems.at[i, k], recv_sem=recv_sems.at[i, k], device_id=to, device_id_type=MESH)

        mine = [pltpu.make_async_copy(ins[i], slot(i, me), local_sems.at[i]) for i in range(n)]
        for cp in mine:
            cp.start()
        first = []
        for i in range(n):
            first.append(copy(i, 0, me, sibling, src=ins[i]))
            first += [copy(i, 1 + j, me, (*chip, c), src=ins[i]) for j, chip in enumerate(chips)]
        for cp in first:
            cp.start()
        passed = []
        for j, chip in enumerate(chips):
            for i in range(n):
                copy(i, 1 + j, (*chip, c), me).wait_recv()
                onward = copy(i, 4 + j, (*chip, c), sibling)
                onward.start()
                passed.append(onward)
        for i in range(n):
            copy(i, 0, sibling, me).wait_recv()
            for j, chip in enumerate(chips):
                copy(i, 4 + j, (*chip, 1 - c), me).wait_recv()
        for cp in first + passed:
            cp.wait_send()
        for cp in mine:
            cp.wait()

    return pl.pallas_call(
        body, name=name, in_specs=[ANY] * n, out_specs=[ANY] * n,
        out_shape=[jax.ShapeDtypeStruct((N_DEV,) + s.shape, s.dtype) for s in shards],
        scratch_shapes=[pltpu.SemaphoreType.DMA((n, 7)), pltpu.SemaphoreType.DMA((n, 7)), pltpu.SemaphoreType.DMA((n,))],
    )(*shards)


def _by_chip_and_core(grad):
    return grad.reshape(4, 2, -1, grad.shape[-1])


def _pack(vectors):
    flat = [v.reshape(-1).astype(F32) for v in vectors]
    sizes = [f.shape[0] for f in flat]
    total = sum(sizes)
    padded = -(-total // (SUBLANES * LANES)) * (SUBLANES * LANES)
    if padded > total:
        flat.append(jnp.zeros((padded - total,), F32))
    return jnp.concatenate(flat).reshape(-1, LANES), sizes


def _unpack(packed, sizes, shapes):
    flat = packed.reshape(-1)
    out, off = [], 0
    for size, shape in zip(sizes, shapes):
        out.append(flat[off:off + size].reshape(shape))
        off += size
    return out


def kernel(x, g_mix, w_in, sg_ln_g, sg_ln_b, sg_w, sg_b, lru_conv_w, lru_conv_b, lru_wa, lru_ba, lru_wx, lru_bx, lru_lam, p_sg, p_lru, w_out, g_ffn, w_up, ffn_conv_w, ffn_conv_b, w_down, g_final, loss_target, m_g_mix, m_w_in, m_sg_ln_g, m_sg_ln_b, m_sg_w, m_sg_b, m_lru_conv_w, m_lru_conv_b, m_lru_wa, m_lru_ba, m_lru_wx, m_lru_bx, m_lru_lam, m_p_sg, m_p_lru, m_w_out, m_g_ffn, m_w_up, m_ffn_conv_w, m_ffn_conv_b, m_w_down, m_g_final, v_g_mix, v_w_in, v_sg_ln_g, v_sg_ln_b, v_sg_w, v_sg_b, v_lru_conv_w, v_lru_conv_b, v_lru_wa, v_lru_ba, v_lru_wx, v_lru_bx, v_lru_lam, v_p_sg, v_p_lru, v_w_out, v_g_ffn, v_w_up, v_ffn_conv_w, v_ffn_conv_b, v_w_down, v_g_final):
    weights = dict(g_mix=g_mix, w_in=w_in, sg_ln_g=sg_ln_g, sg_ln_b=sg_ln_b, sg_w=sg_w, sg_b=sg_b, lru_conv_w=lru_conv_w,
                   lru_conv_b=lru_conv_b, lru_wa=lru_wa, lru_ba=lru_ba, lru_wx=lru_wx, lru_bx=lru_bx, lru_lam=lru_lam,
                   p_sg=p_sg, p_lru=p_lru, w_out=w_out, g_ffn=g_ffn, w_up=w_up, ffn_conv_w=ffn_conv_w,
                   ffn_conv_b=ffn_conv_b, w_down=w_down, g_final=g_final)
    m_in = dict(g_mix=m_g_mix, w_in=m_w_in, sg_ln_g=m_sg_ln_g, sg_ln_b=m_sg_ln_b, sg_w=m_sg_w, sg_b=m_sg_b,
                lru_conv_w=m_lru_conv_w, lru_conv_b=m_lru_conv_b, lru_wa=m_lru_wa, lru_ba=m_lru_ba, lru_wx=m_lru_wx,
                lru_bx=m_lru_bx, lru_lam=m_lru_lam, p_sg=m_p_sg, p_lru=m_p_lru, w_out=m_w_out, g_ffn=m_g_ffn,
                w_up=m_w_up, ffn_conv_w=m_ffn_conv_w, ffn_conv_b=m_ffn_conv_b, w_down=m_w_down, g_final=m_g_final)
    v_in = dict(g_mix=v_g_mix, w_in=v_w_in, sg_ln_g=v_sg_ln_g, sg_ln_b=v_sg_ln_b, sg_w=v_sg_w, sg_b=v_sg_b,
                lru_conv_w=v_lru_conv_w, lru_conv_b=v_lru_conv_b, lru_wa=v_lru_wa, lru_ba=v_lru_ba, lru_wx=v_lru_wx,
                lru_bx=v_lru_bx, lru_lam=v_lru_lam, p_sg=v_p_sg, p_lru=v_p_lru, w_out=v_w_out, g_ffn=v_g_ffn,
                w_up=v_w_up, ffn_conv_w=v_ffn_conv_w, ffn_conv_b=v_ffn_conv_b, w_down=v_w_down, g_final=v_g_final)
    order = list(weights)

    n_seq, seq, d = x.shape
    t = n_seq * seq
    sgw = sg_ln_g.shape[-1]
    lw = lru_lam.shape[-1]
    hd = lw // LRU_HEADS
    f2 = ffn_conv_b.shape[-1]
    gate_col = (2 * sgw + 2 * lw) // d
    xi, yi, ci = _place()
    dev = 4 * xi + 2 * yi + ci

    core = jnp.reshape(ci, (1,)).astype(jnp.int32)
    first_leg = functools.partial(_gather_first_leg, place_own=False)
    chip_swap = functools.partial(_swap_with_chips, place_own=False)
    shards, in_flight = {}, {}

    def landed(keys, after, name):
        return _await_exchange(first_leg, [in_flight[k] for k in keys], after, name=name)[1]

    needed_next = ["w_in_bottom", "lru_wa", "lru_wx", "p_sg", "taps"]
    by_need = ["w_in_top"] + needed_next + ["p_lru", "w_out", "w_up", "w_down"]
    def token():
        return jnp.zeros((SUBLANES, LANES), F32)

    def start_gather(keys, before, name):
        started, before = _start_exchange(first_leg, [shards[k] for k in keys],
                                          [_own_block_in_place(shards[k], dev) for k in keys], before, name=name)
        in_flight.update(zip(keys, started))
        return before

    shards["w_in_top"] = _cast_bf16(w_in[0], name="cast_w_in_top", part=0, n_parts=2)
    top_started = start_gather(["w_in_top"], token(), "start_gather_w_in_top")
    shards["w_in_bottom"] = _cast_bf16(w_in[0], name="cast_w_in_bottom", part=1, n_parts=2, after=top_started)
    shards.update({k: _cast_bf16(weights[k][0], name=f"cast_{k}", after=top_started) for k in by_need if k in weights})
    shards["taps"], tap_sizes = _pack([lru_conv_w[0], ffn_conv_w[0]])
    g_mix_after_start = start_gather(by_need[1:], g_mix, "start_gather_rest")

    def rows_in_order(g8):
        return g8.reshape(1, -1, g8.shape[-1])

    x2d = x.reshape(t, d)
    h1 = _rmsnorm_fwd(x2d, g_mix_after_start, name="norm_mix")
    ((w_in_top_g,),) = _exchange_now([_gather_second_leg(landed(["w_in_top"], h1, "await_w_in_top"))],
                                     name="second_leg_w_in_top")
    proj_top, _ = _mm_nn(h1, w_in_top_g, out_dtype=F32, name="proj_in_top")
    ((w_in_bottom_g, wa_8, wx_8, p_sg_g, taps_8),) = _exchange_now(
        [_gather_second_leg(landed(needed_next, proj_top, "await_w_in_bottom"))], name="second_leg_w_in_bottom")
    proj, _ = _mm_nn(h1, w_in_bottom_g, out_dtype=F32, residual=proj_top, a_part=1, name="proj_in")
    wa_g, wx_g = (jnp.swapaxes(w8, 0, 1).reshape(LRU_HEADS, hd, hd) for w8 in (wa_8, wx_8))
    wa_t, wx_t = jnp.swapaxes(wa_g, 1, 2), jnp.swapaxes(wx_g, 1, 2)
    tap_parts = [_unpack(taps_8[k], tap_sizes, [lru_conv_w.shape[1:], ffn_conv_w.shape[1:]]) for k in range(N_DEV)]
    lru_cw = jnp.concatenate([p[0] for p in tap_parts], axis=1)
    ffn_cw = jnp.concatenate([p[1] for p in tap_parts], axis=1)
    sg_w0 = sg_w[0]
    sg_w_t = jnp.swapaxes(sg_w0, 1, 2)
    sg_b_t = sg_b[0].T
    y_a = _mixer_a_fwd(proj, sg_ln_g, sg_ln_b, sg_w0, sg_b_t, sgw, name="mixer_a_fwd")
    y_b = _mixer_b_fwd(proj, lru_cw, lru_conv_b, wa_g, lru_ba, wx_g, lru_bx, lru_lam, seq=seq, sgw=sgw, lw=lw,
                       name="mixer_b_fwd")
    m_a, ((p_lru_8, w_out_8),) = _mm_nn(
        y_a, p_sg_g, out_dtype=F32, name="proj_sg",
        exchanges=[_gather_second_leg(landed(["p_lru", "w_out"], y_b, "await_p_lru_w_out"))])
    p_lru_g, w_out_g = rows_in_order(p_lru_8), rows_in_order(w_out_8)
    m_b, _ = _mm_nn(y_b, p_lru_g, out_dtype=F32, name="proj_lru")
    merged = _merge_fwd(m_a, m_b, proj, gate_col, name="merge_fwd")
    x1, ((w_up_g,),) = _mm_nn(merged, w_out_g, out_dtype=F32, residual=x2d, name="proj_out",
                              exchanges=[_gather_second_leg(landed(["w_up"], merged, "await_w_up"))])
    h2 = _rmsnorm_fwd(x1, g_ffn, name="norm_ffn")
    up_pre, _ = _mm_nn(h2, w_up_g, out_dtype=F32, name="ffn_up")
    act, ((w_down_8,),) = _ffn_mid_fwd(up_pre, ffn_cw, ffn_conv_b, seq=seq, name="ffn_mid_fwd",
                                       exchanges=[_gather_second_leg(landed(["w_down"], up_pre, "await_w_down"))])
    w_down_g = rows_in_order(w_down_8)
    x2, _ = _mm_nn(act, w_down_g, out_dtype=F32, residual=x1, name="ffn_down")
    d_x2, d_x2_b, d_g_final, loss_part = _loss_head(x2, g_final.reshape(1, d), loss_target.reshape(t, d), name="loss_head")
    loss = lax.psum(loss_part[0, 0], ("x", "y", "c"))

    def by_rows(g):
        return g.reshape(N_DEV, -1, g.shape[-1])

    def by_head_rows(g):
        return jnp.swapaxes(g.reshape(LRU_HEADS, N_DEV, hd // N_DEV, hd), 0, 1)

    def start_sibling_swap(views, key):
        return _start_exchange(_swap_with_sibling, views, [lax.empty((4,) + v.shape[2:], v.dtype) for v in views], token(),
                               name=f"start_sibling_{key}")

    chip_swaps = {}

    def sum_and_start_chip_swap(keys, sibling_swap, done, ride=None):
        views, from_sibling = _await_exchange(_swap_with_sibling, sibling_swap, done, name=f"await_sibling_{keys[0]}")
        sums = [_add_sibling_part(v, core, s, name=f"chip_sum_{k}") for k, v, s in zip(keys, views, from_sibling)]
        started, ride = _start_exchange(chip_swap, sums, [_own_block_first(s, 2 * xi + yi) for s in sums],
                                        token() if ride is None else ride, name=f"start_chips_{keys[0]}")
        chip_swaps.update(zip(keys, started))
        return ride

    d_w_down, _ = _mm_tn(act, d_x2_b, 1, name="grad_w_down")
    v_down = _by_chip_and_core(by_rows(d_w_down))
    swap, started = start_sibling_swap([v_down], "w_down")
    d_act, _ = _mm_nt(d_x2_b, w_down_g, name="bwd_ffn_down", after=started)
    ffn_cb = sum_and_start_chip_swap(["w_down"], swap, d_act, ride=ffn_conv_b)
    d_up_pre, d_ffn_cw, d_ffn_cb = _ffn_mid_bwd(up_pre, d_act, ffn_cw, ffn_cb, seq=seq, name="ffn_mid_bwd")
    d_w_up, _ = _mm_tn(h2, d_up_pre, N_DEV, name="grad_w_up")
    v_up = _by_chip_and_core(d_w_up)
    swap, started = start_sibling_swap([v_up], "w_up")
    d_h2, _ = _mm_nt(d_up_pre, w_up_g, name="bwd_ffn_up", after=started)
    g_ffn_then = sum_and_start_chip_swap(["w_up"], swap, d_h2, ride=g_ffn)
    d_x1, d_x1_b, d_g_ffn = _rmsnorm_bwd(x1, g_ffn_then, [d_h2], d_x2, name="norm_ffn_bwd")
    d_w_out, _ = _mm_tn(merged, d_x1_b, 1, name="grad_w_out")
    v_out = _by_chip_and_core(by_rows(d_w_out))
    swap, started = start_sibling_swap([v_out], "w_out")
    d_merged, _ = _mm_nt(d_x1_b, w_out_g, name="bwd_proj_out", after=started)
    started = sum_and_start_chip_swap(["w_out"], swap, d_merged)
    d_m_a, d_m_b, d_gates = _merge_bwd(d_merged, m_a, m_b, proj, gate_col, name="merge_bwd", after=started)
    d_p_sg, _ = _mm_tn(y_a, d_m_a, N_DEV, name="grad_p_sg")
    d_p_lru, _ = _mm_tn(y_b, d_m_b, 1, name="grad_p_lru")
    v_sg, v_lru = _by_chip_and_core(d_p_sg), _by_chip_and_core(by_rows(d_p_lru))
    swap, started = start_sibling_swap([v_sg, v_lru], "p_sg")
    d_y_a, _ = _mm_nt(d_m_a, p_sg_g, name="bwd_proj_sg", after=started)
    d_y_b, _ = _mm_nt(d_m_b, p_lru_g, name="bwd_proj_lru")
    lru_cb = sum_and_start_chip_swap(["p_sg", "p_lru"], swap, d_y_a, ride=lru_conv_b)
    d_zuv, d_sg_w, d_sg_b_t, d_ln_g, d_ln_b = _mixer_a_bwd(proj, d_y_a, sg_ln_g, sg_ln_b, sg_w0, sg_w_t, sg_b_t, sgw,
                                                           name="mixer_a_bwd")
    (d_xr, d_yr, d_lru_cw, d_lru_cb, d_wa, d_ba, d_wx, d_bx, d_lam), _ = _mixer_b_bwd(
        proj, d_y_b, lru_cw, lru_cb, wa_g, wa_t, lru_ba, wx_g, wx_t, lru_bx, lru_lam, seq=seq, sgw=sgw, lw=lw,
        name="mixer_b_bwd")
    d_proj = jnp.concatenate([d_zuv, d_xr, d_yr, d_gates], axis=1)
    v_wa = _by_chip_and_core(_cast_bf16(by_head_rows(d_wa), name="cast_grad_wa"))
    v_wx = _by_chip_and_core(_cast_bf16(by_head_rows(d_wx), name="cast_grad_wx"))
    d_w_in_top, _ = _mm_tn(h1, d_proj, N_DEV, name="grad_w_in_top", part=0, n_parts=2)
    swap_top, started = start_sibling_swap([_by_chip_and_core(d_w_in_top)], "w_in_top")
    d_w_in_bottom, _ = _mm_tn(h1, d_proj, N_DEV, name="grad_w_in_bottom", part=1, n_parts=2, after=started)
    swap_bottom, started = start_sibling_swap([_by_chip_and_core(d_w_in_bottom), v_wa, v_wx], "w_in_bottom")
    started = sum_and_start_chip_swap(["w_in_top"], swap_top, started)
    d_h1_left, _ = _mm_nt(d_proj, w_in_top_g, name="bwd_proj_in_top", after=started)
    d_h1_right, _ = _mm_nt(d_proj, w_in_bottom_g, name="bwd_proj_in_bottom")
    g_mix_then = sum_and_start_chip_swap(["w_in_bottom", "lru_wa", "lru_wx"], swap_bottom, d_h1_right, ride=g_mix)
    grad_x, _, d_g_mix = _rmsnorm_bwd(x2d, g_mix_then, [d_h1_left, d_h1_right], d_x1, name="norm_mix_bwd")

    small = ["g_mix", "sg_ln_g", "sg_ln_b", "sg_w", "sg_b", "lru_conv_b", "lru_ba", "lru_bx", "lru_lam", "g_ffn",
             "ffn_conv_b", "g_final", "lru_conv_w", "ffn_conv_w"]
    small_parts = dict(g_mix=d_g_mix, sg_ln_g=d_ln_g, sg_ln_b=d_ln_b, sg_w=d_sg_w, sg_b=d_sg_b_t.T, lru_conv_b=d_lru_cb,
                       lru_ba=d_ba, lru_bx=d_bx, lru_lam=d_lam, g_ffn=d_g_ffn, ffn_conv_b=d_ffn_cb, g_final=d_g_final,
                       lru_conv_w=d_lru_cw, ffn_conv_w=d_ffn_cw)
    packed, sizes = _pack([small_parts[k] for k in small])
    (all_small,) = _all_gather([packed], name="gather_small_grads")
    small_sum = _sum_parts(all_small, name="sum_small_grads")
    small_grads = dict(zip(small, _unpack(small_sum, sizes, [small_parts[k].shape for k in small])))
    for k in ("lru_conv_w", "ffn_conv_w"):
        n_loc = weights[k].shape[-1]
        small_grads[k] = lax.dynamic_slice_in_dim(small_grads[k], dev * n_loc, n_loc, axis=1)

    grads, deltas, new_m, new_v = {}, {}, {}, {}

    def update(k, chunks):
        grads[k], deltas[k], new_m[k], new_v[k] = _adamw(weights[k], m_in[k], v_in[k], chunks, name=f"adamw_{k}")

    def chip_sums_landed(keys, after):
        _, parts = _await_exchange(chip_swap, [chip_swaps[k] for k in keys], after, name=f"await_chips_{keys[0]}")
        return dict(zip(keys, parts))

    for keys in (["w_down"], ["w_up"], ["w_out"], ["p_sg", "p_lru"]):
        for k, parts in chip_sums_landed(keys, grad_x).items():
            update(k, [parts])
    for k in small:
        update(k, [small_grads[k][None]])
    last = chip_sums_landed(["w_in_top", "w_in_bottom", "lru_wa", "lru_wx"], deltas["w_up"])
    update("w_in", [last["w_in_top"], last["w_in_bottom"]])
    update("lru_wa", [last["lru_wa"]])
    update("lru_wx", [last["lru_wx"]])

    return (loss, grad_x.reshape(x.shape), *[grads[k] for k in order], *[deltas[k] for k in order],
            *[new_m[k] for k in order], *[new_v[k] for k in order])
```

```python
import functools
import math
from typing import Callable, NamedTuple

import jax
import jax.numpy as jnp
from jax import lax
from jax.experimental import pallas as pl
from jax.experimental.pallas import tpu as pltpu

F32 = jnp.float32
BF16 = jnp.bfloat16
MESH = pl.DeviceIdType.MESH
ANY = pl.BlockSpec(memory_space=pl.ANY)

N_DEV = 8
EPS = 1e-6
CHUNK = 128
SG_GROUPS = 8
LRU_HEADS = 16
LRU_C = 8.0
ADAM_LR = 0.001
ADAM_B1 = 0.9
ADAM_B2 = 0.999
ADAM_EPS = 1e-08
ADAM_WD = 0.01
ADAM_STEP = 10

V7X_VMEM_LIMIT = 56 * 1024 * 1024
LANES = 128
SUBLANES = 8
MXU = 256

_GELU_C0 = math.sqrt(2.0 / math.pi)
_GELU_C1 = 0.044715


def _params(n_axes):
    return pltpu.CompilerParams(dimension_semantics=("arbitrary",) * n_axes, vmem_limit_bytes=V7X_VMEM_LIMIT)


def _tile(dim, pref, align):
    t = (min(pref, dim) // align) * align
    while t >= align:
        if dim % t == 0:
            return t
        t -= align
    return dim


def _gelu(x):
    return x * (0.5 * (1.0 + jnp.tanh(_GELU_C0 * (x + _GELU_C1 * (x * x * x)))))


def _gelu_and_grad(x):
    t = jnp.tanh(_GELU_C0 * (x + _GELU_C1 * (x * x * x)))
    cdf = 0.5 * (1.0 + t)
    dcdf = 0.5 * (1.0 - t * t) * (_GELU_C0 * (1.0 + 3.0 * _GELU_C1 * (x * x)))
    return x * cdf, cdf + x * dcdf


def _sigmoid(x):
    return 1.0 / (1.0 + jnp.exp(-x))


def _shift_down(x, d):
    if d == 0:
        return x
    row = lax.broadcasted_iota(jnp.int32, x.shape, 0)
    return jnp.where(row >= d, pltpu.roll(x, d, 0), 0.0)


def _shift_up(x, d):
    if d == 0:
        return x
    s = x.shape[0]
    row = lax.broadcasted_iota(jnp.int32, x.shape, 0)
    return jnp.where(row < s - d, pltpu.roll(x, s - d, 0), 0.0)


def _causal_conv(x, w, b):
    k_taps = w.shape[0]
    out = _shift_down(x, k_taps - 1) * w[0:1, :]
    for k in range(1, k_taps):
        out = out + _shift_down(x, k_taps - 1 - k) * w[k:k + 1, :]
    return out + b


def _causal_conv_bwd_x(d_out, w):
    k_taps = w.shape[0]
    d_x = _shift_up(d_out, k_taps - 1) * w[0:1, :]
    for k in range(1, k_taps):
        d_x = d_x + _shift_up(d_out, k_taps - 1 - k) * w[k:k + 1, :]
    return d_x


def _causal_conv_bwd_w(d_out, x, k_taps):
    rows = [jnp.sum(d_out * _shift_down(x, k_taps - 1 - k), axis=0, keepdims=True) for k in range(k_taps)]
    return jnp.concatenate(rows, axis=0)


def _place():
    return lax.axis_index("x"), lax.axis_index("y"), lax.axis_index("c")


def _other_chips(x, y):
    return [(1 - x, y), (x, 1 - y), (1 - x, 1 - y)]


class _Exchange(NamedTuple):
    ins: tuple
    outs: tuple
    in_place: bool
    n_remote: int
    n_local: int
    copies: Callable


def _remote(src, dst, send_sem, recv_sem, to):
    return pltpu.make_async_remote_copy(src_ref=src, dst_ref=dst, send_sem=send_sem, recv_sem=recv_sem, device_id=to,
                                        device_id_type=MESH)


def _gather_first_leg(shards, place_own=True):
    n = len(shards)

    def copies(ins, outs, send_sems, recv_sems, local_sems):
        x, y, c = _place()
        peers = [(x, y, 1 - c)] + [(*chip, c) for chip in _other_chips(x, y)]
        slot = lambda i, dev: outs[i].at[4 * dev[0] + 2 * dev[1] + dev[2]]
        sends = [_remote(ins[i], slot(i, (x, y, c)), send_sems[i].at[k], recv_sems[i].at[k], to)
                 for i in range(n) for k, to in enumerate(peers)]
        receives = [_remote(ins[i], slot(i, frm), send_sems[i].at[k], recv_sems[i].at[k], frm)
                    for i in range(n) for k, frm in enumerate(peers)]
        local = [pltpu.make_async_copy(ins[i], slot(i, (x, y, c)), local_sems[i].at[0]) for i in range(n)] if place_own else []
        return sends, receives, local

    outs = tuple(jax.ShapeDtypeStruct((N_DEV,) + s.shape, s.dtype) for s in shards)
    return _Exchange(tuple(shards), outs, False, 4, int(place_own), copies)


def _gather_second_leg(gathered):
    n = len(gathered)

    def copies(ins, outs, send_sems, recv_sems, local_sems):
        x, y, c = _place()
        slot = lambda i, chip, core: outs[i].at[4 * chip[0] + 2 * chip[1] + core]
        sends = [_remote(slot(i, chip, c), slot(i, chip, c), send_sems[i].at[j], recv_sems[i].at[j], (x, y, 1 - c))
                 for i in range(n) for j, chip in enumerate(_other_chips(x, y))]
        receives = [_remote(slot(i, chip, 1 - c), slot(i, chip, 1 - c), send_sems[i].at[j], recv_sems[i].at[j], (x, y, 1 - c))
                    for i in range(n) for j, chip in enumerate(_other_chips(x, y))]
        return sends, receives, []

    outs = tuple(jax.ShapeDtypeStruct(g.shape, g.dtype) for g in gathered)
    return _Exchange(tuple(gathered), outs, True, 3, 0, copies)


def _swap_with_sibling(parts):
    n = len(parts)

    def copies(ins, outs, send_sems, recv_sems, local_sems):
        x, y, c = _place()
        both = [_remote(ins[i].at[ch, 1 - c], outs[i].at[ch], send_sems[i].at[ch], recv_sems[i].at[ch], (x, y, 1 - c))
                for i in range(n) for ch in range(4)]
        return both, both, []

    outs = tuple(jax.ShapeDtypeStruct((4,) + p.shape[2:], p.dtype) for p in parts)
    return _Exchange(tuple(parts), outs, False, 4, 0, copies)


def _swap_with_chips(parts, place_own=True):
    n = len(parts)

    def copies(ins, outs, send_sems, recv_sems, local_sems):
        x, y, c = _place()
        both = [_remote(ins[i].at[2 * chip[0] + chip[1]], outs[i].at[1 + j], send_sems[i].at[j], recv_sems[i].at[j], (*chip, c))
                for i in range(n) for j, chip in enumerate(_other_chips(x, y))]
        local = [pltpu.make_async_copy(ins[i].at[2 * x + y], outs[i].at[0], local_sems[i].at[0]) for i in range(n)] if place_own else []
        return both, both, local

    outs = tuple(jax.ShapeDtypeStruct(p.shape, p.dtype) for p in parts)
    return _Exchange(tuple(parts), outs, False, 3, int(place_own), copies)


def _exchange_plumbing(exchanges):
    operands = [a for ex in exchanges for a in ex.ins]
    results = [s for ex in exchanges for s in ex.outs]
    scratch, in_place, at = [], {}, 0
    for ex in exchanges:
        n = len(ex.ins)
        scratch += [pltpu.SemaphoreType.DMA((n, ex.n_remote))] * 2
        if ex.n_local:
            scratch.append(pltpu.SemaphoreType.DMA((n, ex.n_local)))
        if ex.in_place:
            in_place.update({at + i: at + i for i in range(n)})
        at += n

    def copies(in_refs, out_refs, sem_refs):
        sends, receives, local = [], [], []
        at, sem_at = 0, 0
        for ex in exchanges:
            n, n_sem = len(ex.ins), 3 if ex.n_local else 2
            per_operand = [[sem.at[i] for i in range(n)] for sem in sem_refs[sem_at:sem_at + n_sem]] + [[]] * (3 - n_sem)
            s, r, l = ex.copies(in_refs[at:at + n], out_refs[at:at + n], *per_operand)
            sends, receives, local = sends + s, receives + r, local + l
            at, sem_at = at + n, sem_at + n_sem
        return sends, receives, local

    return operands, results, scratch, in_place, copies


def _start_all(copies):
    sends, _, local = copies
    for cp in local + sends:
        cp.start()


def _wait_all(copies):
    sends, receives, local = copies
    for cp in receives:
        cp.wait_recv()
    for cp in sends:
        cp.wait_send()
    for cp in local:
        cp.wait()


def _exchange_now(exchanges, *, name):
    operands, results, scratch, in_place, copies = _exchange_plumbing(exchanges)
    n = len(operands)

    def body(*refs):
        made = copies(refs[:n], refs[n:2 * n], refs[2 * n:])
        _start_all(made)
        _wait_all(made)

    out = pl.pallas_call(body, name=name, in_specs=[ANY] * n, out_specs=[ANY] * n, out_shape=results,
                         scratch_shapes=scratch, input_output_aliases=in_place)(*operands)
    return _split(out, exchanges)


def _split(flat, exchanges):
    out, at = [], 0
    for ex in exchanges:
        out.append(list(flat[at:at + len(ex.ins)]))
        at += len(ex.ins)
    return out


HBM = pl.BlockSpec(memory_space=pltpu.HBM)
SEMAPHORES = pl.BlockSpec(memory_space=pltpu.SEMAPHORE)
SPLIT_COPY = pltpu.CompilerParams(has_side_effects=pltpu.SideEffectType.DATAFLOW_SIDE_EFFECTING)


def _start_exchange(make, operands, landings, before, *, name):
    n = len(operands)
    ex = make(operands)

    def body(*refs):
        sends, _, _ = ex.copies(refs[:n], refs[n:2 * n], refs[2 * n + 1:3 * n + 1], refs[3 * n + 1:4 * n + 1], [])
        for cp in sends:
            cp.start()

    buffers = [pltpu.with_memory_space_constraint(a, pltpu.HBM) for a in list(operands) + list(landings) + [before]]
    out = pl.pallas_call(
        body, name=name, in_specs=[HBM] * (2 * n + 1), out_specs=[SEMAPHORES] * (2 * n) + [HBM] * (2 * n + 1),
        out_shape=[pltpu.SemaphoreType.DMA((ex.n_remote,))] * (2 * n) + [pltpu.HBM(a.shape, a.dtype) for a in buffers],
        input_output_aliases={i: 2 * n + i for i in range(2 * n + 1)}, compiler_params=SPLIT_COPY)(*buffers)
    return [(out[i], out[n + i], out[2 * n + i], out[3 * n + i]) for i in range(n)], out[4 * n]


def _await_exchange(make, in_flight, after, *, name):
    n = len(in_flight)
    send_sems, recv_sems, operands, landings = zip(*in_flight)
    ex = make(operands)

    def body(*refs):
        sends, receives, _ = ex.copies(refs[:n], refs[n:2 * n], refs[2 * n:3 * n], refs[3 * n:4 * n], [])
        for cp in receives:
            cp.wait_recv()
        for cp in sends:
            cp.wait_send()

    out = pl.pallas_call(
        body, name=name, in_specs=[HBM] * (2 * n) + [SEMAPHORES] * (2 * n) + [ANY], out_specs=[HBM] * (2 * n),
        out_shape=[pltpu.HBM(a.shape, a.dtype) for a in operands + landings],
        input_output_aliases={i: i for i in range(2 * n)}, compiler_params=SPLIT_COPY,
    )(*operands, *landings, *send_sems, *recv_sems, after)
    return list(out[:n]), list(out[n:])


def _own_block_first(blocks, index):
    own = lax.dynamic_index_in_dim(blocks, index, 0, keepdims=True)
    return lax.dynamic_update_index_in_dim(lax.empty(blocks.shape, blocks.dtype), own, 0, 0)


def _own_block_in_place(shard, index):
    return lax.dynamic_update_index_in_dim(lax.empty((N_DEV,) + shard.shape, shard.dtype), shard, index, 0)


def _pallas(kern, *, name, grid, in_specs, out_specs, out_shape, operands, scratch_shapes=(), exchanges=(), after=None):
    ex_operands, ex_results, ex_scratch, in_place, copies = _exchange_plumbing(exchanges)
    if after is not None:
        ex_operands = [after] + ex_operands
        in_place = {i + 1: o for i, o in in_place.items()}
    n_in, n_out, n_scratch, n_ex = len(in_specs), len(out_specs), len(scratch_shapes), len(ex_results)
    n_unread = len(ex_operands) - n_ex

    def body(*refs):
        ins, refs = refs[:n_in], refs[n_in + n_unread:]
        ex_ins, refs = refs[:n_ex], refs[n_ex:]
        outs, refs = refs[:n_out], refs[n_out:]
        ex_outs, refs = refs[:n_ex], refs[n_ex:]
        scratch, sems = refs[:n_scratch], refs[n_scratch:]
        if exchanges:
            first = functools.reduce(jnp.logical_and, [pl.program_id(a) == 0 for a in range(len(grid))])
            last = functools.reduce(jnp.logical_and, [pl.program_id(a) == g - 1 for a, g in enumerate(grid)])

            @pl.when(first)
            def _():
                _start_all(copies(ex_ins, ex_outs, sems))

        kern(*ins, *outs, *scratch)
        if exchanges:
            @pl.when(last)
            def _():
                _wait_all(copies(ex_ins, ex_outs, sems))

    res = pl.pallas_call(
        body, name=name, grid=grid, in_specs=list(in_specs) + [ANY] * len(ex_operands), out_specs=list(out_specs) + [ANY] * n_ex,
        out_shape=list(out_shape) + ex_results, scratch_shapes=list(scratch_shapes) + ex_scratch,
        input_output_aliases={n_in + i: n_out + o for i, o in in_place.items()},
        compiler_params=_params(len(grid)))(*operands, *ex_operands)
    return list(res[:n_out]), _split(res[n_out:], exchanges)


def _accumulate(step, n_steps, acc, value, finish):
    if n_steps == 1:
        finish(value)
        return

    @pl.when(step == 0)
    def _():
        acc[0][...] = value

    @pl.when(step > 0)
    def _():
        acc[0][...] += value

    @pl.when(step == n_steps - 1)
    def _():
        finish(acc[0][...])


def _mm_nn(a, w, *, out_dtype, name, residual=None, exchanges=(), a_part=0):
    m = a.shape[0]
    nb, k, n_blk = w.shape
    tn = _tile(n_blk, 1536, MXU)
    tm = _tile(m, 1024 if tn <= 1024 else 512, MXU)
    tk = _tile(k, 2048 if (tm > 512 and k > 4096) else 4096, MXU)
    per = n_blk // tn
    nk = k // tk

    def kern(*refs):
        a_ref, w_ref = refs[:2]
        r_ref = None if residual is None else refs[2]
        o_ref, acc = refs[2 + (residual is not None)], refs[3 + (residual is not None):]

        def finish(total):
            o_ref[...] = (total if r_ref is None else total + r_ref[...]).astype(o_ref.dtype)

        _accumulate(pl.program_id(2), nk, acc, jnp.dot(a_ref[...], w_ref[...], preferred_element_type=F32), finish)

    tile = pl.BlockSpec((tm, tn), lambda j, i, kk: (i, j))
    in_specs = [pl.BlockSpec((tm, tk), lambda j, i, kk: (i, a_part * nk + kk)),
                pl.BlockSpec((None, tk, tn), lambda j, i, kk: (j // per, kk, j % per))]
    operands = [a, w]
    if residual is not None:
        in_specs.append(tile)
        operands.append(residual)
    (out,), carried = _pallas(
        kern, name=name, grid=(nb * per, m // tm, nk), in_specs=in_specs, out_specs=[tile],
        out_shape=[jax.ShapeDtypeStruct((m, nb * n_blk), out_dtype)], operands=operands,
        scratch_shapes=[pltpu.VMEM((tm, tn), F32)] * (nk > 1), exchanges=exchanges)
    return out, carried


def _mm_nt(g, w, *, name, exchanges=(), after=None):
    m, n = g.shape
    nb, k, n_blk = w.shape
    tm, tko, tn = _tile(m, 1024, MXU), _tile(k, 1024, MXU), _tile(n_blk, 3072, MXU)
    per = n_blk // tn
    nn = n // tn

    def kern(g_ref, w_ref, o_ref, *acc):
        def finish(total):
            o_ref[...] = total

        part = lax.dot_general(g_ref[...], w_ref[...], (((1,), (1,)), ((), ())), preferred_element_type=F32)
        _accumulate(pl.program_id(2), nn, acc, part, finish)

    (out,), carried = _pallas(
        kern, name=name, grid=(k // tko, m // tm, nn),
        in_specs=[pl.BlockSpec((tm, tn), lambda j, i, jn: (i, jn)),
                  pl.BlockSpec((None, tko, tn), lambda j, i, jn: (jn // per, j, jn % per))],
        out_specs=[pl.BlockSpec((tm, tko), lambda j, i, jn: (i, j))],
        out_shape=[jax.ShapeDtypeStruct((m, k), F32)], operands=[g, w],
        scratch_shapes=[pltpu.VMEM((tm, tko), F32)] * (nn > 1), exchanges=exchanges, after=after)
    return out, carried


def _mm_tn(a, g, nb, *, name, exchanges=(), after=None, part=0, n_parts=1):
    m, k = a.shape[0], a.shape[1] // n_parts
    n = g.shape[1]
    n_blk = n // nb
    tko, tn, tm = _tile(k, 512, MXU), _tile(n_blk, 1536, MXU), _tile(m, 4096, MXU)
    per = n_blk // tn
    nm = m // tm

    def kern(a_ref, g_ref, o_ref, *acc):
        def finish(total):
            o_ref[...] = total.astype(o_ref.dtype)

        part = lax.dot_general(a_ref[...], g_ref[...], (((0,), (0,)), ((), ())), preferred_element_type=F32)
        _accumulate(pl.program_id(2), nm, acc, part, finish)

    (out,), carried = _pallas(
        kern, name=name, grid=(nb * per, k // tko, nm),
        in_specs=[pl.BlockSpec((tm, tko), lambda j, i, im: (im, part * (k // tko) + i)),
                  pl.BlockSpec((tm, tn), lambda j, i, im: (im, j))],
        out_specs=[pl.BlockSpec((None, tko, tn), lambda j, i, im: (j // per, i, j % per))],
        out_shape=[jax.ShapeDtypeStruct((nb, k, n_blk), BF16)], operands=[a, g],
        scratch_shapes=[pltpu.VMEM((tko, tn), F32)] * (nm > 1), exchanges=exchanges, after=after)
    return out, carried


ROW_TILE = 128


def _rmsnorm_fwd(x, g, *, name):
    t, d = x.shape
    tr = _tile(t, ROW_TILE, SUBLANES)

    def kern(x_ref, g_ref, h_ref):
        xv = x_ref[...]
        r = lax.rsqrt(jnp.mean(xv * xv, axis=-1, keepdims=True) + EPS)
        h_ref[...] = (xv * r * g_ref[...]).astype(BF16)

    return pl.pallas_call(
        kern, name=name, grid=(t // tr,),
        in_specs=[pl.BlockSpec((tr, d), lambda i: (i, 0)), pl.BlockSpec((1, d), lambda i: (0, 0))],
        out_specs=pl.BlockSpec((tr, d), lambda i: (i, 0)),
        out_shape=jax.ShapeDtypeStruct((t, d), BF16), compiler_params=_params(1))(x, g)


def _rmsnorm_bwd(x, g, d_h_parts, d_res, *, name):
    t, d = x.shape
    tr = _tile(t, ROW_TILE, SUBLANES)
    n_parts = len(d_h_parts)

    def kern(x_ref, g_ref, *refs):
        dh_refs, (dres_ref, dx_ref, dxb_ref, dg_ref) = refs[:n_parts], refs[n_parts:]
        xv = x_ref[...]
        r = lax.rsqrt(jnp.mean(xv * xv, axis=-1, keepdims=True) + EPS)
        dh = jnp.concatenate([ref[...] for ref in dh_refs], axis=1)
        gy = dh * g_ref[...]
        dx = dres_ref[...] + r * gy - xv * (r * r * r) * jnp.mean(gy * xv, axis=-1, keepdims=True)
        dx_ref[...] = dx
        dxb_ref[...] = dx.astype(BF16)

        @pl.when(pl.program_id(0) == 0)
        def _():
            dg_ref[...] = jnp.zeros_like(dg_ref)

        dg_ref[...] += jnp.sum(dh * (xv * r), axis=0, keepdims=True)

    row = pl.BlockSpec((tr, d), lambda i: (i, 0))
    vec = pl.BlockSpec((1, d), lambda i: (0, 0))
    part = pl.BlockSpec((tr, d // n_parts), lambda i: (i, 0))
    return pl.pallas_call(
        kern, name=name, grid=(t // tr,), in_specs=[row, vec] + [part] * n_parts + [row], out_specs=[row, row, vec],
        out_shape=[jax.ShapeDtypeStruct((t, d), F32), jax.ShapeDtypeStruct((t, d), BF16),
                   jax.ShapeDtypeStruct((1, d), F32)], compiler_params=_params(1))(x, g, *d_h_parts, d_res)


def _loss_head(x, g, target, *, name):
    t, d = x.shape
    tr = _tile(t, ROW_TILE, SUBLANES)

    def kern(x_ref, g_ref, t_ref, dx_ref, dxb_ref, dg_ref, loss_ref):
        xv = x_ref[...]
        gv = g_ref[...]
        r = lax.rsqrt(jnp.mean(xv * xv, axis=-1, keepdims=True) + EPS)
        diff = xv * r * gv - t_ref[...]
        dy = diff * (1.0 / d)
        gy = dy * gv
        dx = r * gy - xv * (r * r * r) * jnp.mean(gy * xv, axis=-1, keepdims=True)
        dx_ref[...] = dx
        dxb_ref[...] = dx.astype(BF16)

        @pl.when(pl.program_id(0) == 0)
        def _():
            dg_ref[...] = jnp.zeros_like(dg_ref)
            loss_ref[...] = jnp.zeros_like(loss_ref)

        dg_ref[...] += jnp.sum(dy * (xv * r), axis=0, keepdims=True)
        part = 0.5 * jnp.sum(jnp.mean(diff * diff, axis=-1, keepdims=True), axis=0, keepdims=True)
        loss_ref[...] += jnp.broadcast_to(part, loss_ref.shape)

    row = pl.BlockSpec((tr, d), lambda i: (i, 0))
    vec = pl.BlockSpec((1, d), lambda i: (0, 0))
    return pl.pallas_call(
        kern, name=name, grid=(t // tr,), in_specs=[row, vec, row],
        out_specs=[row, row, vec, pl.BlockSpec((1, LANES), lambda i: (0, 0))],
        out_shape=[jax.ShapeDtypeStruct((t, d), F32), jax.ShapeDtypeStruct((t, d), BF16),
                   jax.ShapeDtypeStruct((1, d), F32), jax.ShapeDtypeStruct((1, LANES), F32)],
        compiler_params=_params(1))(x, g, target)


def _merge_fwd(m_a, m_b, proj, gate_col, *, name):
    t, d = m_a.shape
    tr = _tile(t, ROW_TILE, SUBLANES)

    def kern(ma_ref, mb_ref, ga_ref, gb_ref, o_ref):
        o_ref[...] = (_sigmoid(ga_ref[...]) * ma_ref[...] + _sigmoid(gb_ref[...]) * mb_ref[...]).astype(BF16)

    row = pl.BlockSpec((tr, d), lambda i: (i, 0))
    return pl.pallas_call(
        kern, name=name, grid=(t // tr,),
        in_specs=[row, row, pl.BlockSpec((tr, d), lambda i: (i, gate_col)),
                  pl.BlockSpec((tr, d), lambda i: (i, gate_col + 1))],
        out_specs=row, out_shape=jax.ShapeDtypeStruct((t, d), BF16), compiler_params=_params(1))(m_a, m_b, proj, proj)


def _merge_bwd(d_merged, m_a, m_b, proj, gate_col, *, name, after=None):
    t, d = m_a.shape
    tr = _tile(t, ROW_TILE, SUBLANES)

    def kern(dm_ref, ma_ref, mb_ref, ga_ref, gb_ref, dma_ref, dmb_ref, dg_ref):
        dm = dm_ref[...]
        sa = _sigmoid(ga_ref[...])
        sb = _sigmoid(gb_ref[...])
        dma_ref[...] = (dm * sa).astype(BF16)
        dmb_ref[...] = (dm * sb).astype(BF16)
        dg_ref[:, 0:d] = (dm * ma_ref[...] * (sa * (1.0 - sa))).astype(BF16)
        dg_ref[:, d:2 * d] = (dm * mb_ref[...] * (sb * (1.0 - sb))).astype(BF16)

    row = pl.BlockSpec((tr, d), lambda i: (i, 0))
    res, _ = _pallas(
        kern, name=name, grid=(t // tr,),
        in_specs=[row, row, row, pl.BlockSpec((tr, d), lambda i: (i, gate_col)),
                  pl.BlockSpec((tr, d), lambda i: (i, gate_col + 1))],
        out_specs=[row, row, pl.BlockSpec((tr, 2 * d), lambda i: (i, 0))],
        out_shape=[jax.ShapeDtypeStruct((t, d), BF16), jax.ShapeDtypeStruct((t, d), BF16),
                   jax.ShapeDtypeStruct((t, 2 * d), BF16)], operands=[d_merged, m_a, m_b, proj, proj], after=after)
    return res


def _tril_bf16(w, transposed):
    row = lax.broadcasted_iota(jnp.int32, w.shape, 0)
    col = lax.broadcasted_iota(jnp.int32, w.shape, 1)
    keep = (row <= col) if transposed else (row >= col)
    return jnp.where(keep, w, 0.0).astype(BF16)


def _layernorm_stats(v):
    mu = jnp.mean(v, axis=-1, keepdims=True)
    vc = v - mu
    rstd = lax.rsqrt(jnp.mean(vc * vc, axis=-1, keepdims=True) + EPS)
    return vc * rstd, rstd


def _mixer_a_fwd(proj, ln_g, ln_b, sg_w, sg_b_t, sgw, *, name):
    t = proj.shape[0]
    gd = sgw // SG_GROUPS

    def kern(zu_ref, zv_ref, g_ref, b_ref, w_ref, bt_ref, o_ref):
        xhat, _ = _layernorm_stats(_gelu(zv_ref[...]))
        vn = (xhat * g_ref[...] + b_ref[...]).astype(BF16)
        for g in range(SG_GROUPS):
            cols = slice(g * gd, (g + 1) * gd)
            mixed = jnp.dot(_tril_bf16(w_ref[g], False), vn[:, cols], preferred_element_type=F32) + bt_ref[:, g:g + 1]
            o_ref[:, cols] = (_gelu(zu_ref[:, cols]) * mixed).astype(BF16)

    vec = pl.BlockSpec((1, sgw), lambda i: (0, 0))
    return pl.pallas_call(
        kern, name=name, grid=(t // CHUNK,),
        in_specs=[pl.BlockSpec((CHUNK, sgw), lambda i: (i, 0)), pl.BlockSpec((CHUNK, sgw), lambda i: (i, 1)), vec, vec,
                  pl.BlockSpec((SG_GROUPS, CHUNK, CHUNK), lambda i: (0, 0, 0)),
                  pl.BlockSpec((CHUNK, SG_GROUPS), lambda i: (0, 0))],
        out_specs=pl.BlockSpec((CHUNK, sgw), lambda i: (i, 0)),
        out_shape=jax.ShapeDtypeStruct((t, sgw), BF16), compiler_params=_params(1))(proj, proj, ln_g, ln_b, sg_w, sg_b_t)


def _mixer_a_bwd(proj, d_ya, ln_g, ln_b, sg_w, sg_w_t, sg_b_t, sgw, *, name):
    t = proj.shape[0]
    gd = sgw // SG_GROUPS

    def kern(zu_ref, zv_ref, dy_ref, g_ref, b_ref, w_ref, wt_ref, bt_ref, dz_ref, dw_ref, dbt_ref, dg_ref, db_ref, dvn):
        @pl.when(pl.program_id(0) == 0)
        def _():
            dw_ref[...] = jnp.zeros_like(dw_ref)
            dbt_ref[...] = jnp.zeros_like(dbt_ref)
            dg_ref[...] = jnp.zeros_like(dg_ref)
            db_ref[...] = jnp.zeros_like(db_ref)

        gv, dgv = _gelu_and_grad(zv_ref[...])
        xhat, rstd = _layernorm_stats(gv)
        ln_gain = g_ref[...]
        vn = (xhat * ln_gain + b_ref[...]).astype(BF16)
        for g in range(SG_GROUPS):
            cols = slice(g * gd, (g + 1) * gd)
            gu, dgu = _gelu_and_grad(zu_ref[:, cols])
            mixed = jnp.dot(_tril_bf16(w_ref[g], False), vn[:, cols], preferred_element_type=F32) + bt_ref[:, g:g + 1]
            dy = dy_ref[:, cols]
            dz_ref[:, cols] = (dy * mixed * dgu).astype(BF16)
            d_mixed = dy * gu
            d_mixed_b = d_mixed.astype(BF16)
            dvn[:, cols] = jnp.dot(_tril_bf16(wt_ref[g], True), d_mixed_b, preferred_element_type=F32)
            d_w = lax.dot_general(d_mixed_b, vn[:, cols], (((1,), (1,)), ((), ())), preferred_element_type=F32)
            row = lax.broadcasted_iota(jnp.int32, d_w.shape, 0)
            col = lax.broadcasted_iota(jnp.int32, d_w.shape, 1)
            dw_ref[g] += jnp.where(row >= col, d_w, 0.0)
            dbt_ref[:, g:g + 1] += jnp.sum(d_mixed, axis=-1, keepdims=True)
        d_vn = dvn[...]
        dg_ref[...] += jnp.sum(d_vn * xhat, axis=0, keepdims=True)
        db_ref[...] += jnp.sum(d_vn, axis=0, keepdims=True)
        d_xhat = d_vn * ln_gain
        d_gv = rstd * (d_xhat - jnp.mean(d_xhat, axis=-1, keepdims=True)
                       - xhat * jnp.mean(d_xhat * xhat, axis=-1, keepdims=True))
        dz_ref[:, sgw:2 * sgw] = (d_gv * dgv).astype(BF16)

    vec = pl.BlockSpec((1, sgw), lambda i: (0, 0))
    wspec = pl.BlockSpec((SG_GROUPS, CHUNK, CHUNK), lambda i: (0, 0, 0))
    btspec = pl.BlockSpec((CHUNK, SG_GROUPS), lambda i: (0, 0))
    return pl.pallas_call(
        kern, name=name, grid=(t // CHUNK,),
        in_specs=[pl.BlockSpec((CHUNK, sgw), lambda i: (i, 0)), pl.BlockSpec((CHUNK, sgw), lambda i: (i, 1)),
                  pl.BlockSpec((CHUNK, sgw), lambda i: (i, 0)), vec, vec, wspec, wspec, btspec],
        out_specs=[pl.BlockSpec((CHUNK, 2 * sgw), lambda i: (i, 0)), wspec, btspec, vec, vec],
        out_shape=[jax.ShapeDtypeStruct((t, 2 * sgw), BF16), jax.ShapeDtypeStruct((SG_GROUPS, CHUNK, CHUNK), F32),
                   jax.ShapeDtypeStruct((CHUNK, SG_GROUPS), F32), jax.ShapeDtypeStruct((1, sgw), F32),
                   jax.ShapeDtypeStruct((1, sgw), F32)],
        scratch_shapes=[pltpu.VMEM((CHUNK, sgw), F32)],
        compiler_params=_params(1))(proj, proj, d_ya, ln_g, ln_b, sg_w, sg_w_t, sg_b_t)


def _scan_rows(a_ref, h_ref, reverse):
    s, c = a_ref.shape
    nblk = s // SUBLANES
    a, b = a_ref[...], h_ref[...]
    row = jnp.bitwise_and(lax.broadcasted_iota(jnp.int32, (s, c), 0), SUBLANES - 1)
    for d in (1, 2, 4):
        inside = (row < SUBLANES - d) if reverse else (row >= d)
        shift = s - d if reverse else d
        b = a * jnp.where(inside, pltpu.roll(b, shift, 0), 0.0) + b
        a = a * jnp.where(inside, pltpu.roll(a, shift, 0), 1.0)
    a_ref[...] = a
    h_ref[...] = b
    leaving = 0 if reverse else SUBLANES - 1

    def chain(i, carry):
        r0 = pl.multiple_of((nblk - 1 - i if reverse else i) * SUBLANES, SUBLANES)
        h = a_ref[pl.ds(r0, SUBLANES), :] * carry + h_ref[pl.ds(r0, SUBLANES), :]
        h_ref[pl.ds(r0, SUBLANES), :] = h
        return jnp.broadcast_to(h[leaving:leaving + 1, :], (SUBLANES, c))

    lax.fori_loop(0, nblk, chain, jnp.zeros((SUBLANES, c), F32))


def _lru_gates(xc, wa_ref, ba_ref, wx_ref, bx_ref, lam_ref):
    xcb = xc.astype(BF16)
    ra = _sigmoid(jnp.dot(xcb, wa_ref[...].astype(BF16), preferred_element_type=F32) + ba_ref[...])
    ia = _sigmoid(jnp.dot(xcb, wx_ref[...].astype(BF16), preferred_element_type=F32) + bx_ref[...])
    neg = -lam_ref[...]
    sp = jnp.maximum(neg, 0.0) + jnp.log1p(jnp.exp(-jnp.abs(neg)))
    log_a = -LRU_C * ra * sp
    a = jnp.exp(log_a)
    a2 = jnp.exp(2.0 * log_a)
    sq = jnp.sqrt(-jnp.tanh(log_a) * (a2 + 1.0))
    return ra, ia, sp, a, a2, sq


def _mixer_b_specs(seq, hd, sgw, lw):
    x_col = (2 * sgw) // hd
    y_col = (2 * sgw + lw) // hd
    tile = lambda col: pl.BlockSpec((seq, hd), lambda h, b: (b, col + h))
    vec = pl.BlockSpec((1, hd), lambda h, b: (0, h))
    mat = pl.BlockSpec((None, hd, hd), lambda h, b: (h, 0, 0))
    return tile(x_col), tile(y_col), tile(0), vec, mat


def _mixer_b_fwd(proj, conv_w, conv_b, wa, ba, wx, bx, lam, *, seq, sgw, lw, name):
    t = proj.shape[0]
    hd = lw // LRU_HEADS
    k_taps = conv_w.shape[0]
    x_spec, y_spec, o_spec, vec, mat = _mixer_b_specs(seq, hd, sgw, lw)

    def kern(xr_ref, yr_ref, cw_ref, cb_ref, wa_ref, ba_ref, wx_ref, bx_ref, lam_ref, o_ref, s_a, s_h):
        xc = _causal_conv(xr_ref[...], cw_ref[...], cb_ref[...])
        _, ia, _, a, _, sq = _lru_gates(xc, wa_ref, ba_ref, wx_ref, bx_ref, lam_ref)
        s_a[...] = a
        s_h[...] = sq * (ia * xc)
        _scan_rows(s_a, s_h, False)
        o_ref[...] = (s_h[...] * _gelu(yr_ref[...])).astype(BF16)

    return pl.pallas_call(
        kern, name=name, grid=(LRU_HEADS, t // seq),
        in_specs=[x_spec, y_spec, pl.BlockSpec((k_taps, hd), lambda h, b: (0, h)), vec, mat, vec, mat, vec, vec],
        out_specs=o_spec, out_shape=jax.ShapeDtypeStruct((t, lw), BF16),
        scratch_shapes=[pltpu.VMEM((seq, hd), F32), pltpu.VMEM((seq, hd), F32)],
        compiler_params=_params(2))(proj, proj, conv_w, conv_b, wa, ba, wx, bx, lam)


def _mixer_b_bwd(proj, d_yb, conv_w, conv_b, wa, wa_t, ba, wx, wx_t, bx, lam, *, seq, sgw, lw, name, exchanges=()):
    t = proj.shape[0]
    hd = lw // LRU_HEADS
    k_taps = conv_w.shape[0]
    x_spec, y_spec, o_spec, vec, mat = _mixer_b_specs(seq, hd, sgw, lw)
    cw_spec = pl.BlockSpec((k_taps, hd), lambda h, b: (0, h))

    def kern(xr_ref, yr_ref, dyb_ref, cw_ref, cb_ref, wa_ref, wat_ref, ba_ref, wx_ref, wxt_ref, bx_ref, lam_ref,
             dxr_ref, dyr_ref, dcw_ref, dcb_ref, dwa_ref, dba_ref, dwx_ref, dbx_ref, dlam_ref,
             s_xc, s_a, s_h, s_lam, s_dpa, s_dpx):
        @pl.when(pl.program_id(1) == 0)
        def _():
            for ref in (dcw_ref, dcb_ref, dwa_ref, dba_ref, dwx_ref, dbx_ref, dlam_ref):
                ref[...] = jnp.zeros_like(ref)

        s_xc[...] = _causal_conv(xr_ref[...], cw_ref[...], cb_ref[...])
        _, ia, _, a, _, sq = _lru_gates(s_xc[...], wa_ref, ba_ref, wx_ref, bx_ref, lam_ref)
        s_a[...] = a
        s_dpa[...] = _shift_up(a, 1)
        s_h[...] = sq * (ia * s_xc[...])
        _scan_rows(s_a, s_h, False)

        gel, dgel = _gelu_and_grad(yr_ref[...])
        dyb = dyb_ref[...]
        dyr_ref[...] = (dyb * s_h[...] * dgel).astype(BF16)
        s_lam[...] = dyb * gel
        _scan_rows(s_dpa, s_lam, True)
        ra, ia, sp, a, a2, sq = _lru_gates(s_xc[...], wa_ref, ba_ref, wx_ref, bx_ref, lam_ref)
        d_gx = s_lam[...]
        d_a = d_gx * _shift_down(s_h[...], 1)
        xc = s_xc[...]
        d_sq = d_gx * (ia * xc)
        d_ia = d_gx * (sq * xc)
        d_log_a = d_a * a - d_sq * (a2 / sq)
        d_ra = d_log_a * (-LRU_C * sp)
        d_sp = jnp.sum(d_log_a * (-LRU_C * ra), axis=0, keepdims=True)
        dlam_ref[...] += d_sp * (-_sigmoid(-lam_ref[...]))
        d_pa = d_ra * (ra * (1.0 - ra))
        d_px = d_ia * (ia * (1.0 - ia))
        s_dpa[...] = d_pa
        s_dpx[...] = d_px
        dba_ref[...] += jnp.sum(d_pa, axis=0, keepdims=True)
        dbx_ref[...] += jnp.sum(d_px, axis=0, keepdims=True)
        xcb = s_xc[...].astype(BF16)
        d_pa_b = s_dpa[...].astype(BF16)
        d_px_b = s_dpx[...].astype(BF16)
        contract_rows = (((0,), (0,)), ((), ()))
        dwa_ref[...] += lax.dot_general(xcb, d_pa_b, contract_rows, preferred_element_type=F32)
        dwx_ref[...] += lax.dot_general(xcb, d_px_b, contract_rows, preferred_element_type=F32)
        d_xc = (s_lam[...] * (sq * ia)
                + jnp.dot(d_pa_b, wat_ref[...].astype(BF16), preferred_element_type=F32)
                + jnp.dot(d_px_b, wxt_ref[...].astype(BF16), preferred_element_type=F32))
        dcb_ref[...] += jnp.sum(d_xc, axis=0, keepdims=True)
        dcw_ref[...] += _causal_conv_bwd_w(d_xc, xr_ref[...], k_taps)
        dxr_ref[...] = _causal_conv_bwd_x(d_xc, cw_ref[...]).astype(BF16)

    tile_shape = jax.ShapeDtypeStruct((t, lw), BF16)
    vec_shape = jax.ShapeDtypeStruct((1, lw), F32)
    mat_shape = jax.ShapeDtypeStruct((LRU_HEADS, hd, hd), F32)
    return _pallas(
        kern, name=name, grid=(LRU_HEADS, t // seq),
        in_specs=[x_spec, y_spec, o_spec, cw_spec, vec, mat, mat, vec, mat, mat, vec, vec],
        out_specs=[o_spec, o_spec, cw_spec, vec, mat, vec, mat, vec, vec],
        out_shape=[tile_shape, tile_shape, jax.ShapeDtypeStruct((k_taps, lw), F32), vec_shape, mat_shape, vec_shape,
                   mat_shape, vec_shape, vec_shape],
        operands=[proj, proj, d_yb, conv_w, conv_b, wa, wa_t, ba, wx, wx_t, bx, lam],
        scratch_shapes=[pltpu.VMEM((seq, hd), F32)] * 6, exchanges=exchanges)


FFN_TILE = 256


def _ffn_mid_fwd(up_pre, conv_w, conv_b, *, seq, name, exchanges=()):
    t, f2 = up_pre.shape
    f = f2 // 2
    tc = _tile(f, FFN_TILE, LANES)
    nf = f // tc
    k_taps = conv_w.shape[0]

    def kern(pg_ref, pv_ref, wg_ref, wv_ref, bg_ref, bv_ref, o_ref):
        cg = _causal_conv(pg_ref[...], wg_ref[...], bg_ref[...])
        cv = _causal_conv(pv_ref[...], wv_ref[...], bv_ref[...])
        o_ref[...] = (_gelu(cg) * cv).astype(BF16)

    tile = lambda off: pl.BlockSpec((seq, tc), lambda j, b: (b, off + j))
    wspec = lambda off: pl.BlockSpec((k_taps, tc), lambda j, b: (0, off + j))
    bspec = lambda off: pl.BlockSpec((1, tc), lambda j, b: (0, off + j))
    (act,), carried = _pallas(
        kern, name=name, grid=(nf, t // seq),
        in_specs=[tile(0), tile(nf), wspec(0), wspec(nf), bspec(0), bspec(nf)], out_specs=[tile(0)],
        out_shape=[jax.ShapeDtypeStruct((t, f), BF16)],
        operands=[up_pre, up_pre, conv_w, conv_w, conv_b, conv_b], exchanges=exchanges)
    return act, carried


def _ffn_mid_bwd(up_pre, d_act, conv_w, conv_b, *, seq, name):
    t, f2 = up_pre.shape
    f = f2 // 2
    tc = _tile(f, FFN_TILE, LANES)
    nf = f // tc
    k_taps = conv_w.shape[0]

    def kern(pg_ref, pv_ref, da_ref, wg_ref, wv_ref, bg_ref, bv_ref, dpg_ref, dpv_ref, dwg_ref, dwv_ref, dbg_ref, dbv_ref):
        @pl.when(pl.program_id(1) == 0)
        def _():
            for ref in (dwg_ref, dwv_ref, dbg_ref, dbv_ref):
                ref[...] = jnp.zeros_like(ref)

        pg = pg_ref[...]
        pv = pv_ref[...]
        gel, dgel = _gelu_and_grad(_causal_conv(pg, wg_ref[...], bg_ref[...]))
        cv = _causal_conv(pv, wv_ref[...], bv_ref[...])
        d_act_v = da_ref[...]
        d_cg = d_act_v * cv * dgel
        d_cv = d_act_v * gel
        dpg_ref[...] = _causal_conv_bwd_x(d_cg, wg_ref[...]).astype(BF16)
        dpv_ref[...] = _causal_conv_bwd_x(d_cv, wv_ref[...]).astype(BF16)
        dwg_ref[...] += _causal_conv_bwd_w(d_cg, pg, k_taps)
        dwv_ref[...] += _causal_conv_bwd_w(d_cv, pv, k_taps)
        dbg_ref[...] += jnp.sum(d_cg, axis=0, keepdims=True)
        dbv_ref[...] += jnp.sum(d_cv, axis=0, keepdims=True)

    tile = lambda off: pl.BlockSpec((seq, tc), lambda j, b: (b, off + j))
    wspec = lambda off: pl.BlockSpec((k_taps, tc), lambda j, b: (0, off + j))
    bspec = lambda off: pl.BlockSpec((1, tc), lambda j, b: (0, off + j))
    half = jax.ShapeDtypeStruct((t, f), BF16)
    wshape = jax.ShapeDtypeStruct((k_taps, f), F32)
    bshape = jax.ShapeDtypeStruct((1, f), F32)
    d_pg, d_pv, d_wg, d_wv, d_bg, d_bv = pl.pallas_call(
        kern, name=name, grid=(nf, t // seq),
        in_specs=[tile(0), tile(nf), tile(0), wspec(0), wspec(nf), bspec(0), bspec(nf)],
        out_specs=[tile(0), tile(0), wspec(0), wspec(0), bspec(0), bspec(0)],
        out_shape=[half, half, wshape, wshape, bshape, bshape],
        compiler_params=_params(2))(up_pre, up_pre, d_act, conv_w, conv_w, conv_b, conv_b)
    return (jnp.concatenate([d_pg, d_pv], axis=1), jnp.concatenate([d_wg, d_wv], axis=1),
            jnp.concatenate([d_bg, d_bv], axis=1))


ELEM_VMEM_BYTES = 24 << 20


def _as_2d(a):
    if a.ndim >= 2 and a.shape[-1] % LANES == 0 and a.size // a.shape[-1] >= SUBLANES:
        return a.reshape(-1, a.shape[-1])
    return a.reshape(-1, LANES)


def _row_tile(rows, bytes_per_row):
    return _tile(rows, max(16, ELEM_VMEM_BYTES // (2 * bytes_per_row)), 16)


def _cast_bf16(a, *, name, part=0, n_parts=1, after=None):
    v = _as_2d(a)
    rows, cols = v.shape[0] // n_parts, v.shape[1]
    tr = _row_tile(rows, cols * (4 + 2))
    first = part * (rows // tr)

    def kern(x_ref, o_ref):
        o_ref[...] = x_ref[...].astype(BF16)

    (out,), _ = _pallas(kern, name=name, grid=(rows // tr,), in_specs=[pl.BlockSpec((tr, cols), lambda i: (first + i, 0))],
                        out_specs=[pl.BlockSpec((tr, cols), lambda i: (i, 0))],
                        out_shape=[jax.ShapeDtypeStruct((rows, cols), BF16)], operands=[v], after=after)
    return out.reshape(a.shape) if n_parts == 1 else out


def _add_sibling_part(own, core, got, *, name):
    _, _, rows, cols = own.shape
    tr = _row_tile(rows, cols * (2 + 2 + 2))

    def kern(core_ref, a_ref, b_ref, o_ref):
        o_ref[...] = (a_ref[...].astype(F32) + b_ref[...].astype(F32)).astype(BF16)

    spec = pl.BlockSpec((None, tr, cols), lambda ch, i, core_ref: (ch, i, 0))
    grid_spec = pltpu.PrefetchScalarGridSpec(
        num_scalar_prefetch=1, grid=(4, rows // tr),
        in_specs=[pl.BlockSpec((None, None, tr, cols), lambda ch, i, core_ref: (ch, core_ref[0], i, 0)), spec],
        out_specs=spec)
    return pl.pallas_call(kern, name=name, grid_spec=grid_spec, out_shape=jax.ShapeDtypeStruct(got.shape, BF16),
                          compiler_params=_params(2))(core, own, got)


def _sum_parts(parts, *, name):
    n_parts, rows, cols = parts.shape
    tr = _row_tile(rows, cols * 4 * (n_parts + 1))

    def kern(p_ref, o_ref):
        acc = p_ref[0].astype(F32)
        for p in range(1, n_parts):
            acc = acc + p_ref[p].astype(F32)
        o_ref[...] = acc

    return pl.pallas_call(
        kern, name=name, grid=(rows // tr,), in_specs=[pl.BlockSpec((n_parts, tr, cols), lambda i: (0, i, 0))],
        out_specs=pl.BlockSpec((tr, cols), lambda i: (i, 0)), out_shape=jax.ShapeDtypeStruct((rows, cols), F32),
        compiler_params=_params(1))(parts)


def _adamw(w, m, v, grad_chunks, *, name):
    shape = w.shape
    w2 = _as_2d(w)
    rows, cols = w2.shape
    n_chunks = len(grad_chunks)
    n_parts = grad_chunks[0].shape[0]
    chunks = [c.reshape(n_parts, rows // n_chunks, cols) for c in grad_chunks]
    tr = _row_tile(rows // n_chunks, cols * (3 * 4 + n_chunks * n_parts * chunks[0].dtype.itemsize + 4 * 4))
    per_chunk = rows // n_chunks // tr
    c_m = 1.0 - ADAM_B1 ** ADAM_STEP
    c_v = 1.0 - ADAM_B2 ** ADAM_STEP

    def kern(w_ref, m_ref, v_ref, *refs):
        p_refs, (g_ref, d_ref, nm_ref, nv_ref) = refs[:n_chunks], refs[n_chunks:]
        g = None
        for k, p_ref in enumerate(p_refs):
            total = p_ref[0].astype(F32)
            for p in range(1, n_parts):
                total = total + p_ref[p].astype(F32)
            g = total if g is None else jnp.where(pl.program_id(0) // per_chunk == k, total, g)
        new_m = ADAM_B1 * m_ref[...] + (1.0 - ADAM_B1) * g
        new_v = ADAM_B2 * v_ref[...] + (1.0 - ADAM_B2) * (g * g)
        g_ref[...] = g
        nm_ref[...] = new_m
        nv_ref[...] = new_v
        d_ref[...] = -ADAM_LR * ((new_m / c_m) / (jnp.sqrt(new_v / c_v) + ADAM_EPS) + ADAM_WD * w_ref[...])

    spec = pl.BlockSpec((tr, cols), lambda i: (i, 0))
    out = jax.ShapeDtypeStruct((rows, cols), F32)
    def chunk_spec(k):
        return pl.BlockSpec((n_parts, tr, cols), lambda i: (0, jnp.clip(i - k * per_chunk, 0, per_chunk - 1), 0))

    res = pl.pallas_call(
        kern, name=name, grid=(rows // tr,),
        in_specs=[spec, spec, spec] + [chunk_spec(k) for k in range(n_chunks)],
        out_specs=[spec] * 4, out_shape=[out] * 4, compiler_params=_params(1))(w2, _as_2d(m), _as_2d(v), *chunks)
    return [r.reshape(shape) for r in res]


def _all_gather(shards, *, name):
    n = len(shards)

    def body(*refs):
        ins, outs = refs[:n], refs[n:2 * n]
        send_sems, recv_sems, local_sems = refs[2 * n:]
        x, y, c = _place()
        me, sibling = (x, y, c), (x, y, 1 - c)
        chips = [(1 - x, y), (x, 1 - y), (1 - x, 1 - y)]

        def slot(i, dev):
            return outs[i].at[4 * dev[0] + 2 * dev[1] + dev[2]]

        def copy(i, k, block, to, src=None):
            return pltpu.make_async_remote_copy(
                src_ref=slot(i, block) if src is None else src, dst_ref=slot(i, block),
                send_sem=send_sems.at[i, k], recv_sem=recv_sems.at[i, k], device_id=to, device_id_type=MESH)

        mine = [pltpu.make_async_copy(ins[i], slot(i, me), local_sems.at[i]) for i in range(n)]
        for cp in mine:
            cp.start()
        first = []
        for i in range(n):
            first.append(copy(i, 0, me, sibling, src=ins[i]))
            first += [copy(i, 1 + j, me, (*chip, c), src=ins[i]) for j, chip in enumerate(chips)]
        for cp in first:
            cp.start()
        passed = []
        for j, chip in enumerate(chips):
            for i in range(n):
                copy(i, 1 + j, (*chip, c), me).wait_recv()
                onward = copy(i, 4 + j, (*chip, c), sibling)
                onward.start()
                passed.append(onward)
        for i in range(n):
            copy(i, 0, sibling, me).wait_recv()
            for j, chip in enumerate(chips):
                copy(i, 4 + j, (*chip, 1 - c), me).wait_recv()
        for cp in first + passed:
            cp.wait_send()
        for cp in mine:
            cp.wait()

    return pl.pallas_call(
        body, name=name, in_specs=[ANY] * n, out_specs=[ANY] * n,
        out_shape=[jax.ShapeDtypeStruct((N_DEV,) + s.shape, s.dtype) for s in shards],
        scratch_shapes=[pltpu.SemaphoreType.DMA((n, 7)), pltpu.SemaphoreType.DMA((n, 7)), pltpu.SemaphoreType.DMA((n,))],
    )(*shards)


def _by_chip_and_core(grad):
    return grad.reshape(4, 2, -1, grad.shape[-1])


def _pack(vectors):
    flat = [v.reshape(-1).astype(F32) for v in vectors]
    sizes = [f.shape[0] for f in flat]
    total = sum(sizes)
    padded = -(-total // (SUBLANES * LANES)) * (SUBLANES * LANES)
    if padded > total:
        flat.append(jnp.zeros((padded - total,), F32))
    return jnp.concatenate(flat).reshape(-1, LANES), sizes


def _unpack(packed, sizes, shapes):
    flat = packed.reshape(-1)
    out, off = [], 0
    for size, shape in zip(sizes, shapes):
        out.append(flat[off:off + size].reshape(shape))
        off += size
    return out


def kernel(x, g_mix, w_in, sg_ln_g, sg_ln_b, sg_w, sg_b, lru_conv_w, lru_conv_b, lru_wa, lru_ba, lru_wx, lru_bx, lru_lam, p_sg, p_lru, w_out, g_ffn, w_up, ffn_conv_w, ffn_conv_b, w_down, g_final, loss_target, m_g_mix, m_w_in, m_sg_ln_g, m_sg_ln_b, m_sg_w, m_sg_b, m_lru_conv_w, m_lru_conv_b, m_lru_wa, m_lru_ba, m_lru_wx, m_lru_bx, m_lru_lam, m_p_sg, m_p_lru, m_w_out, m_g_ffn, m_w_up, m_ffn_conv_w, m_ffn_conv_b, m_w_down, m_g_final, v_g_mix, v_w_in, v_sg_ln_g, v_sg_ln_b, v_sg_w, v_sg_b, v_lru_conv_w, v_lru_conv_b, v_lru_wa, v_lru_ba, v_lru_wx, v_lru_bx, v_lru_lam, v_p_sg, v_p_lru, v_w_out, v_g_ffn, v_w_up, v_ffn_conv_w, v_ffn_conv_b, v_w_down, v_g_final):
    weights = dict(g_mix=g_mix, w_in=w_in, sg_ln_g=sg_ln_g, sg_ln_b=sg_ln_b, sg_w=sg_w, sg_b=sg_b, lru_conv_w=lru_conv_w,
                   lru_conv_b=lru_conv_b, lru_wa=lru_wa, lru_ba=lru_ba, lru_wx=lru_wx, lru_bx=lru_bx, lru_lam=lru_lam,
                   p_sg=p_sg, p_lru=p_lru, w_out=w_out, g_ffn=g_ffn, w_up=w_up, ffn_conv_w=ffn_conv_w,
                   ffn_conv_b=ffn_conv_b, w_down=w_down, g_final=g_final)
    m_in = dict(g_mix=m_g_mix, w_in=m_w_in, sg_ln_g=m_sg_ln_g, sg_ln_b=m_sg_ln_b, sg_w=m_sg_w, sg_b=m_sg_b,
                lru_conv_w=m_lru_conv_w, lru_conv_b=m_lru_conv_b, lru_wa=m_lru_wa, lru_ba=m_lru_ba, lru_wx=m_lru_wx,
                lru_bx=m_lru_bx, lru_lam=m_lru_lam, p_sg=m_p_sg, p_lru=m_p_lru, w_out=m_w_out, g_ffn=m_g_ffn,
                w_up=m_w_up, ffn_conv_w=m_ffn_conv_w, ffn_conv_b=m_ffn_conv_b, w_down=m_w_down, g_final=m_g_final)
    v_in = dict(g_mix=v_g_mix, w_in=v_w_in, sg_ln_g=v_sg_ln_g, sg_ln_b=v_sg_ln_b, sg_w=v_sg_w, sg_b=v_sg_b,
                lru_conv_w=v_lru_conv_w, lru_conv_b=v_lru_conv_b, lru_wa=v_lru_wa, lru_ba=v_lru_ba, lru_wx=v_lru_wx,
                lru_bx=v_lru_bx, lru_lam=v_lru_lam, p_sg=v_p_sg, p_lru=v_p_lru, w_out=v_w_out, g_ffn=v_g_ffn,
                w_up=v_w_up, ffn_conv_w=v_ffn_conv_w, ffn_conv_b=v_ffn_conv_b, w_down=v_w_down, g_final=v_g_final)
    order = list(weights)

    n_seq, seq, d = x.shape
    t = n_seq * seq
    sgw = sg_ln_g.shape[-1]
    lw = lru_lam.shape[-1]
    hd = lw // LRU_HEADS
    f2 = ffn_conv_b.shape[-1]
    gate_col = (2 * sgw + 2 * lw) // d
    xi, yi, ci = _place()
    dev = 4 * xi + 2 * yi + ci

    core = jnp.reshape(ci, (1,)).astype(jnp.int32)
    first_leg = functools.partial(_gather_first_leg, place_own=False)
    chip_swap = functools.partial(_swap_with_chips, place_own=False)
    shards, in_flight = {}, {}

    def landed(keys, after, name):
        return _await_exchange(first_leg, [in_flight[k] for k in keys], after, name=name)[1]

    needed_next = ["w_in_bottom", "lru_wa", "lru_wx", "p_sg", "taps"]
    by_need = ["w_in_top"] + needed_next + ["p_lru", "w_out", "w_up", "w_down"]
    def token():
        return jnp.zeros((SUBLANES, LANES), F32)

    def start_gather(keys, before, name):
        started, before = _start_exchange(first_leg, [shards[k] for k in keys],
                                          [_own_block_in_place(shards[k], dev) for k in keys], before, name=name)
        in_flight.update(zip(keys, started))
        return before

    shards["w_in_top"] = _cast_bf16(w_in[0], name="cast_w_in_top", part=0, n_parts=2)
    top_started = start_gather(["w_in_top"], token(), "start_gather_w_in_top")
    shards["w_in_bottom"] = _cast_bf16(w_in[0], name="cast_w_in_bottom", part=1, n_parts=2, after=top_started)
    shards.update({k: _cast_bf16(weights[k][0], name=f"cast_{k}", after=top_started) for k in by_need if k in weights})
    shards["taps"], tap_sizes = _pack([lru_conv_w[0], ffn_conv_w[0]])
    g_mix_after_start = start_gather(by_need[1:], g_mix, "start_gather_rest")

    def rows_in_order(g8):
        return g8.reshape(1, -1, g8.shape[-1])

    x2d = x.reshape(t, d)
    h1 = _rmsnorm_fwd(x2d, g_mix_after_start, name="norm_mix")
    ((w_in_top_g,),) = _exchange_now([_gather_second_leg(landed(["w_in_top"], h1, "await_w_in_top"))],
                                     name="second_leg_w_in_top")
    proj_top, _ = _mm_nn(h1, w_in_top_g, out_dtype=F32, name="proj_in_top")
    ((w_in_bottom_g, wa_8, wx_8, p_sg_g, taps_8),) = _exchange_now(
        [_gather_second_leg(landed(needed_next, proj_top, "await_w_in_bottom"))], name="second_leg_w_in_bottom")
    proj, _ = _mm_nn(h1, w_in_bottom_g, out_dtype=F32, residual=proj_top, a_part=1, name="proj_in")
    wa_g, wx_g = (jnp.swapaxes(w8, 0, 1).reshape(LRU_HEADS, hd, hd) for w8 in (wa_8, wx_8))
    wa_t, wx_t = jnp.swapaxes(wa_g, 1, 2), jnp.swapaxes(wx_g, 1, 2)
    tap_parts = [_unpack(taps_8[k], tap_sizes, [lru_conv_w.shape[1:], ffn_conv_w.shape[1:]]) for k in range(N_DEV)]
    lru_cw = jnp.concatenate([p[0] for p in tap_parts], axis=1)
    ffn_cw = jnp.concatenate([p[1] for p in tap_parts], axis=1)
    sg_w0 = sg_w[0]
    sg_w_t = jnp.swapaxes(sg_w0, 1, 2)
    sg_b_t = sg_b[0].T
    y_a = _mixer_a_fwd(proj, sg_ln_g, sg_ln_b, sg_w0, sg_b_t, sgw, name="mixer_a_fwd")
    y_b = _mixer_b_fwd(proj, lru_cw, lru_conv_b, wa_g, lru_ba, wx_g, lru_bx, lru_lam, seq=seq, sgw=sgw, lw=lw,
                       name="mixer_b_fwd")
    m_a, ((p_lru_8, w_out_8),) = _mm_nn(
        y_a, p_sg_g, out_dtype=F32, name="proj_sg",
        exchanges=[_gather_second_leg(landed(["p_lru", "w_out"], y_b, "await_p_lru_w_out"))])
    p_lru_g, w_out_g = rows_in_order(p_lru_8), rows_in_order(w_out_8)
    m_b, _ = _mm_nn(y_b, p_lru_g, out_dtype=F32, name="proj_lru")
    merged = _merge_fwd(m_a, m_b, proj, gate_col, name="merge_fwd")
    x1, ((w_up_g,),) = _mm_nn(merged, w_out_g, out_dtype=F32, residual=x2d, name="proj_out",
                              exchanges=[_gather_second_leg(landed(["w_up"], merged, "await_w_up"))])
    h2 = _rmsnorm_fwd(x1, g_ffn, name="norm_ffn")
    up_pre, _ = _mm_nn(h2, w_up_g, out_dtype=F32, name="ffn_up")
    act, ((w_down_8,),) = _ffn_mid_fwd(up_pre, ffn_cw, ffn_conv_b, seq=seq, name="ffn_mid_fwd",
                                       exchanges=[_gather_second_leg(landed(["w_down"], up_pre, "await_w_down"))])
    w_down_g = rows_in_order(w_down_8)
    x2, _ = _mm_nn(act, w_down_g, out_dtype=F32, residual=x1, name="ffn_down")
    d_x2, d_x2_b, d_g_final, loss_part = _loss_head(x2, g_final.reshape(1, d), loss_target.reshape(t, d), name="loss_head")
    loss = lax.psum(loss_part[0, 0], ("x", "y", "c"))

    def by_rows(g):
        return g.reshape(N_DEV, -1, g.shape[-1])

    def by_head_rows(g):
        return jnp.swapaxes(g.reshape(LRU_HEADS, N_DEV, hd // N_DEV, hd), 0, 1)

    def start_sibling_swap(views, key):
        return _start_exchange(_swap_with_sibling, views, [lax.empty((4,) + v.shape[2:], v.dtype) for v in views], token(),
                               name=f"start_sibling_{key}")

    chip_swaps = {}

    def sum_and_start_chip_swap(keys, sibling_swap, done, ride=None):
        views, from_sibling = _await_exchange(_swap_with_sibling, sibling_swap, done, name=f"await_sibling_{keys[0]}")
        sums = [_add_sibling_part(v, core, s, name=f"chip_sum_{k}") for k, v, s in zip(keys, views, from_sibling)]
        started, ride = _start_exchange(chip_swap, sums, [_own_block_first(s, 2 * xi + yi) for s in sums],
                                        token() if ride is None else ride, name=f"start_chips_{keys[0]}")
        chip_swaps.update(zip(keys, started))
        return ride

    d_w_down, _ = _mm_tn(act, d_x2_b, 1, name="grad_w_down")
    v_down = _by_chip_and_core(by_rows(d_w_down))
    swap, started = start_sibling_swap([v_down], "w_down")
    d_act, _ = _mm_nt(d_x2_b, w_down_g, name="bwd_ffn_down", after=started)
    ffn_cb = sum_and_start_chip_swap(["w_down"], swap, d_act, ride=ffn_conv_b)
    d_up_pre, d_ffn_cw, d_ffn_cb = _ffn_mid_bwd(up_pre, d_act, ffn_cw, ffn_cb, seq=seq, name="ffn_mid_bwd")
    d_w_up, _ = _mm_tn(h2, d_up_pre, N_DEV, name="grad_w_up")
    v_up = _by_chip_and_core(d_w_up)
    swap, started = start_sibling_swap([v_up], "w_up")
    d_h2, _ = _mm_nt(d_up_pre, w_up_g, name="bwd_ffn_up", after=started)
    g_ffn_then = sum_and_start_chip_swap(["w_up"], swap, d_h2, ride=g_ffn)
    d_x1, d_x1_b, d_g_ffn = _rmsnorm_bwd(x1, g_ffn_then, [d_h2], d_x2, name="norm_ffn_bwd")
    d_w_out, _ = _mm_tn(merged, d_x1_b, 1, name="grad_w_out")
    v_out = _by_chip_and_core(by_rows(d_w_out))
    swap, started = start_sibling_swap([v_out], "w_out")
    d_merged, _ = _mm_nt(d_x1_b, w_out_g, name="bwd_proj_out", after=started)
    started = sum_and_start_chip_swap(["w_out"], swap, d_merged)
    d_m_a, d_m_b, d_gates = _merge_bwd(d_merged, m_a, m_b, proj, gate_col, name="merge_bwd", after=started)
    d_p_sg, _ = _mm_tn(y_a, d_m_a, N_DEV, name="grad_p_sg")
    d_p_lru, _ = _mm_tn(y_b, d_m_b, 1, name="grad_p_lru")
    v_sg, v_lru = _by_chip_and_core(d_p_sg), _by_chip_and_core(by_rows(d_p_lru))
    swap, started = start_sibling_swap([v_sg, v_lru], "p_sg")
    d_y_a, _ = _mm_nt(d_m_a, p_sg_g, name="bwd_proj_sg", after=started)
    d_y_b, _ = _mm_nt(d_m_b, p_lru_g, name="bwd_proj_lru")
    lru_cb = sum_and_start_chip_swap(["p_sg", "p_lru"], swap, d_y_a, ride=lru_conv_b)
    d_zuv, d_sg_w, d_sg_b_t, d_ln_g, d_ln_b = _mixer_a_bwd(proj, d_y_a, sg_ln_g, sg_ln_b, sg_w0, sg_w_t, sg_b_t, sgw,
                                                           name="mixer_a_bwd")
    (d_xr, d_yr, d_lru_cw, d_lru_cb, d_wa, d_ba, d_wx, d_bx, d_lam), _ = _mixer_b_bwd(
        proj, d_y_b, lru_cw, lru_cb, wa_g, wa_t, lru_ba, wx_g, wx_t, lru_bx, lru_lam, seq=seq, sgw=sgw, lw=lw,
        name="mixer_b_bwd")
    d_proj = jnp.concatenate([d_zuv, d_xr, d_yr, d_gates], axis=1)
    v_wa = _by_chip_and_core(_cast_bf16(by_head_rows(d_wa), name="cast_grad_wa"))
    v_wx = _by_chip_and_core(_cast_bf16(by_head_rows(d_wx), name="cast_grad_wx"))
    d_w_in_top, _ = _mm_tn(h1, d_proj, N_DEV, name="grad_w_in_top", part=0, n_parts=2)
    swap_top, started = start_sibling_swap([_by_chip_and_core(d_w_in_top)], "w_in_top")
    d_w_in_bottom, _ = _mm_tn(h1, d_proj, N_DEV, name="grad_w_in_bottom", part=1, n_parts=2, after=started)
    swap_bottom, started = start_sibling_swap([_by_chip_and_core(d_w_in_bottom), v_wa, v_wx], "w_in_bottom")
    started = sum_and_start_chip_swap(["w_in_top"], swap_top, started)
    d_h1_left, _ = _mm_nt(d_proj, w_in_top_g, name="bwd_proj_in_top", after=started)
    d_h1_right, _ = _mm_nt(d_proj, w_in_bottom_g, name="bwd_proj_in_bottom")
    g_mix_then = sum_and_start_chip_swap(["w_in_bottom", "lru_wa", "lru_wx"], swap_bottom, d_h1_right, ride=g_mix)
    grad_x, _, d_g_mix = _rmsnorm_bwd(x2d, g_mix_then, [d_h1_left, d_h1_right], d_x1, name="norm_mix_bwd")

    small = ["g_mix", "sg_ln_g", "sg_ln_b", "sg_w", "sg_b", "lru_conv_b", "lru_ba", "lru_bx", "lru_lam", "g_ffn",
             "ffn_conv_b", "g_final", "lru_conv_w", "ffn_conv_w"]
    small_parts = dict(g_mix=d_g_mix, sg_ln_g=d_ln_g, sg_ln_b=d_ln_b, sg_w=d_sg_w, sg_b=d_sg_b_t.T, lru_conv_b=d_lru_cb,
                       lru_ba=d_ba, lru_bx=d_bx, lru_lam=d_lam, g_ffn=d_g_ffn, ffn_conv_b=d_ffn_cb, g_final=d_g_final,
                       lru_conv_w=d_lru_cw, ffn_conv_w=d_ffn_cw)
    packed, sizes = _pack([small_parts[k] for k in small])
    (all_small,) = _all_gather([packed], name="gather_small_grads")
    small_sum = _sum_parts(all_small, name="sum_small_grads")
    small_grads = dict(zip(small, _unpack(small_sum, sizes, [small_parts[k].shape for k in small])))
    for k in ("lru_conv_w", "ffn_conv_w"):
        n_loc = weights[k].shape[-1]
        small_grads[k] = lax.dynamic_slice_in_dim(small_grads[k], dev * n_loc, n_loc, axis=1)

    grads, deltas, new_m, new_v = {}, {}, {}, {}

    def update(k, chunks):
        grads[k], deltas[k], new_m[k], new_v[k] = _adamw(weights[k], m_in[k], v_in[k], chunks, name=f"adamw_{k}")

    def chip_sums_landed(keys, after):
        _, parts = _await_exchange(chip_swap, [chip_swaps[k] for k in keys], after, name=f"await_chips_{keys[0]}")
        return dict(zip(keys, parts))

    for keys in (["w_down"], ["w_up"], ["w_out"], ["p_sg", "p_lru"]):
        for k, parts in chip_sums_landed(keys, grad_x).items():
            update(k, [parts])
    for k in small:
        update(k, [small_grads[k][None]])
    last = chip_sums_landed(["w_in_top", "w_in_bottom", "lru_wa", "lru_wx"], deltas["w_up"])
    update("w_in", [last["w_in_top"], last["w_in_bottom"]])
    update("lru_wa", [last["lru_wa"]])
    update("lru_wx", [last["lru_wx"]])

    return (loss, grad_x.reshape(x.shape), *[grads[k] for k in order], *[deltas[k] for k in order],
            *[new_m[k] for k in order], *[new_v[k] for k in order])
```

```python
import functools
import math
from typing import Callable, NamedTuple

import jax
import jax.numpy as jnp
from jax import lax
from jax.experimental import pallas as pl
from jax.experimental.pallas import tpu as pltpu

F32 = jnp.float32
BF16 = jnp.bfloat16
MESH = pl.DeviceIdType.MESH
ANY = pl.BlockSpec(memory_space=pl.ANY)

N_DEV = 8
EPS = 1e-6
CHUNK = 128
SG_GROUPS = 8
LRU_HEADS = 16
LRU_C = 8.0
ADAM_LR = 0.001
ADAM_B1 = 0.9
ADAM_B2 = 0.999
ADAM_EPS = 1e-08
ADAM_WD = 0.01
ADAM_STEP = 10

V7X_VMEM_LIMIT = 56 * 1024 * 1024
LANES = 128
SUBLANES = 8
MXU = 256

_GELU_C0 = math.sqrt(2.0 / math.pi)
_GELU_C1 = 0.044715


def _params(n_axes):
    return pltpu.CompilerParams(dimension_semantics=("arbitrary",) * n_axes, vmem_limit_bytes=V7X_VMEM_LIMIT)


def _tile(dim, pref, align):
    t = (min(pref, dim) // align) * align
    while t >= align:
        if dim % t == 0:
            return t
        t -= align
    return dim


def _gelu(x):
    return x * (0.5 * (1.0 + jnp.tanh(_GELU_C0 * (x + _GELU_C1 * (x * x * x)))))


def _gelu_and_grad(x):
    t = jnp.tanh(_GELU_C0 * (x + _GELU_C1 * (x * x * x)))
    cdf = 0.5 * (1.0 + t)
    dcdf = 0.5 * (1.0 - t * t) * (_GELU_C0 * (1.0 + 3.0 * _GELU_C1 * (x * x)))
    return x * cdf, cdf + x * dcdf


def _sigmoid(x):
    return 1.0 / (1.0 + jnp.exp(-x))


def _shift_down(x, d):
    if d == 0:
        return x
    row = lax.broadcasted_iota(jnp.int32, x.shape, 0)
    return jnp.where(row >= d, pltpu.roll(x, d, 0), 0.0)


def _shift_up(x, d):
    if d == 0:
        return x
    s = x.shape[0]
    row = lax.broadcasted_iota(jnp.int32, x.shape, 0)
    return jnp.where(row < s - d, pltpu.roll(x, s - d, 0), 0.0)


def _causal_conv(x, w, b):
    k_taps = w.shape[0]
    out = _shift_down(x, k_taps - 1) * w[0:1, :]
    for k in range(1, k_taps):
        out = out + _shift_down(x, k_taps - 1 - k) * w[k:k + 1, :]
    return out + b


def _causal_conv_bwd_x(d_out, w):
    k_taps = w.shape[0]
    d_x = _shift_up(d_out, k_taps - 1) * w[0:1, :]
    for k in range(1, k_taps):
        d_x = d_x + _shift_up(d_out, k_taps - 1 - k) * w[k:k + 1, :]
    return d_x


def _causal_conv_bwd_w(d_out, x, k_taps):
    rows = [jnp.sum(d_out * _shift_down(x, k_taps - 1 - k), axis=0, keepdims=True) for k in range(k_taps)]
    return jnp.concatenate(rows, axis=0)


def _place():
    return lax.axis_index("x"), lax.axis_index("y"), lax.axis_index("c")


def _other_chips(x, y):
    return [(1 - x, y), (x, 1 - y), (1 - x, 1 - y)]


class _Exchange(NamedTuple):
    ins: tuple
    outs: tuple
    in_place: bool
    n_remote: int
    n_local: int
    copies: Callable


def _remote(src, dst, send_sem, recv_sem, to):
    return pltpu.make_async_remote_copy(src_ref=src, dst_ref=dst, send_sem=send_sem, recv_sem=recv_sem, device_id=to,
                                        device_id_type=MESH)


def _gather_first_leg(shards, place_own=True):
    n = len(shards)

    def copies(ins, outs, send_sems, recv_sems, local_sems):
        x, y, c = _place()
        peers = [(x, y, 1 - c)] + [(*chip, c) for chip in _other_chips(x, y)]
        slot = lambda i, dev: outs[i].at[4 * dev[0] + 2 * dev[1] + dev[2]]
        sends = [_remote(ins[i], slot(i, (x, y, c)), send_sems[i].at[k], recv_sems[i].at[k], to)
                 for i in range(n) for k, to in enumerate(peers)]
        receives = [_remote(ins[i], slot(i, frm), send_sems[i].at[k], recv_sems[i].at[k], frm)
                    for i in range(n) for k, frm in enumerate(peers)]
        local = [pltpu.make_async_copy(ins[i], slot(i, (x, y, c)), local_sems[i].at[0]) for i in range(n)] if place_own else []
        return sends, receives, local

    outs = tuple(jax.ShapeDtypeStruct((N_DEV,) + s.shape, s.dtype) for s in shards)
    return _Exchange(tuple(shards), outs, False, 4, int(place_own), copies)


def _gather_second_leg(gathered):
    n = len(gathered)

    def copies(ins, outs, send_sems, recv_sems, local_sems):
        x, y, c = _place()
        slot = lambda i, chip, core: outs[i].at[4 * chip[0] + 2 * chip[1] + core]
        sends = [_remote(slot(i, chip, c), slot(i, chip, c), send_sems[i].at[j], recv_sems[i].at[j], (x, y, 1 - c))
                 for i in range(n) for j, chip in enumerate(_other_chips(x, y))]
        receives = [_remote(slot(i, chip, 1 - c), slot(i, chip, 1 - c), send_sems[i].at[j], recv_sems[i].at[j], (x, y, 1 - c))
                    for i in range(n) for j, chip in enumerate(_other_chips(x, y))]
        return sends, receives, []

    outs = tuple(jax.ShapeDtypeStruct(g.shape, g.dtype) for g in gathered)
    return _Exchange(tuple(gathered), outs, True, 3, 0, copies)


def _swap_with_sibling(parts):
    n = len(parts)

    def copies(ins, outs, send_sems, recv_sems, local_sems):
        x, y, c = _place()
        both = [_remote(ins[i].at[ch, 1 - c], outs[i].at[ch], send_sems[i].at[ch], recv_sems[i].at[ch], (x, y, 1 - c))
                for i in range(n) for ch in range(4)]
        return both, both, []

    outs = tuple(jax.ShapeDtypeStruct((4,) + p.shape[2:], p.dtype) for p in parts)
    return _Exchange(tuple(parts), outs, False, 4, 0, copies)


def _swap_with_chips(parts, place_own=True):
    n = len(parts)

    def copies(ins, outs, send_sems, recv_sems, local_sems):
        x, y, c = _place()
        both = [_remote(ins[i].at[2 * chip[0] + chip[1]], outs[i].at[1 + j], send_sems[i].at[j], recv_sems[i].at[j], (*chip, c))
                for i in range(n) for j, chip in enumerate(_other_chips(x, y))]
        local = [pltpu.make_async_copy(ins[i].at[2 * x + y], outs[i].at[0], local_sems[i].at[0]) for i in range(n)] if place_own else []
        return both, both, local

    outs = tuple(jax.ShapeDtypeStruct(p.shape, p.dtype) for p in parts)
    return _Exchange(tuple(parts), outs, False, 3, int(place_own), copies)


def _exchange_plumbing(exchanges):
    operands = [a for ex in exchanges for a in ex.ins]
    results = [s for ex in exchanges for s in ex.outs]
    scratch, in_place, at = [], {}, 0
    for ex in exchanges:
        n = len(ex.ins)
        scratch += [pltpu.SemaphoreType.DMA((n, ex.n_remote))] * 2
        if ex.n_local:
            scratch.append(pltpu.SemaphoreType.DMA((n, ex.n_local)))
        if ex.in_place:
            in_place.update({at + i: at + i for i in range(n)})
        at += n

    def copies(in_refs, out_refs, sem_refs):
        sends, receives, local = [], [], []
        at, sem_at = 0, 0
        for ex in exchanges:
            n, n_sem = len(ex.ins), 3 if ex.n_local else 2
            per_operand = [[sem.at[i] for i in range(n)] for sem in sem_refs[sem_at:sem_at + n_sem]] + [[]] * (3 - n_sem)
            s, r, l = ex.copies(in_refs[at:at + n], out_refs[at:at + n], *per_operand)
            sends, receives, local = sends + s, receives + r, local + l
            at, sem_at = at + n, sem_at + n_sem
        return sends, receives, local

    return operands, results, scratch, in_place, copies


def _start_all(copies):
    sends, _, local = copies
    for cp in local + sends:
        cp.start()


def _wait_all(copies):
    sends, receives, local = copies
    for cp in receives:
        cp.wait_recv()
    for cp in sends:
        cp.wait_send()
    for cp in local:
        cp.wait()


def _exchange_now(exchanges, *, name):
    operands, results, scratch, in_place, copies = _exchange_plumbing(exchanges)
    n = len(operands)

    def body(*refs):
        made = copies(refs[:n], refs[n:2 * n], refs[2 * n:])
        _start_all(made)
        _wait_all(made)

    out = pl.pallas_call(body, name=name, in_specs=[ANY] * n, out_specs=[ANY] * n, out_shape=results,
                         scratch_shapes=scratch, input_output_aliases=in_place)(*operands)
    return _split(out, exchanges)


def _split(flat, exchanges):
    out, at = [], 0
    for ex in exchanges:
        out.append(list(flat[at:at + len(ex.ins)]))
        at += len(ex.ins)
    return out


HBM = pl.BlockSpec(memory_space=pltpu.HBM)
SEMAPHORES = pl.BlockSpec(memory_space=pltpu.SEMAPHORE)
SPLIT_COPY = pltpu.CompilerParams(has_side_effects=pltpu.SideEffectType.DATAFLOW_SIDE_EFFECTING)


def _start_exchange(make, operands, landings, before, *, name):
    n = len(operands)
    ex = make(operands)

    def body(*refs):
        sends, _, _ = ex.copies(refs[:n], refs[n:2 * n], refs[2 * n + 1:3 * n + 1], refs[3 * n + 1:4 * n + 1], [])
        for cp in sends:
            cp.start()

    buffers = [pltpu.with_memory_space_constraint(a, pltpu.HBM) for a in list(operands) + list(landings) + [before]]
    out = pl.pallas_call(
        body, name=name, in_specs=[HBM] * (2 * n + 1), out_specs=[SEMAPHORES] * (2 * n) + [HBM] * (2 * n + 1),
        out_shape=[pltpu.SemaphoreType.DMA((ex.n_remote,))] * (2 * n) + [pltpu.HBM(a.shape, a.dtype) for a in buffers],
        input_output_aliases={i: 2 * n + i for i in range(2 * n + 1)}, compiler_params=SPLIT_COPY)(*buffers)
    return [(out[i], out[n + i], out[2 * n + i], out[3 * n + i]) for i in range(n)], out[4 * n]


def _await_exchange(make, in_flight, after, *, name):
    n = len(in_flight)
    send_sems, recv_sems, operands, landings = zip(*in_flight)
    ex = make(operands)

    def body(*refs):
        sends, receives, _ = ex.copies(refs[:n], refs[n:2 * n], refs[2 * n:3 * n], refs[3 * n:4 * n], [])
        for cp in receives:
            cp.wait_recv()
        for cp in sends:
            cp.wait_send()

    out = pl.pallas_call(
        body, name=name, in_specs=[HBM] * (2 * n) + [SEMAPHORES] * (2 * n) + [ANY], out_specs=[HBM] * (2 * n),
        out_shape=[pltpu.HBM(a.shape, a.dtype) for a in operands + landings],
        input_output_aliases={i: i for i in range(2 * n)}, compiler_params=SPLIT_COPY,
    )(*operands, *landings, *send_sems, *recv_sems, after)
    return list(out[:n]), list(out[n:])


def _own_block_first(blocks, index):
    own = lax.dynamic_index_in_dim(blocks, index, 0, keepdims=True)
    return lax.dynamic_update_index_in_dim(lax.empty(blocks.shape, blocks.dtype), own, 0, 0)


def _own_block_in_place(shard, index):
    return lax.dynamic_update_index_in_dim(lax.empty((N_DEV,) + shard.shape, shard.dtype), shard, index, 0)


def _pallas(kern, *, name, grid, in_specs, out_specs, out_shape, operands, scratch_shapes=(), exchanges=(), after=None):
    ex_operands, ex_results, ex_scratch, in_place, copies = _exchange_plumbing(exchanges)
    if after is not None:
        ex_operands = [after] + ex_operands
        in_place = {i + 1: o for i, o in in_place.items()}
    n_in, n_out, n_scratch, n_ex = len(in_specs), len(out_specs), len(scratch_shapes), len(ex_results)
    n_unread = len(ex_operands) - n_ex

    def body(*refs):
        ins, refs = refs[:n_in], refs[n_in + n_unread:]
        ex_ins, refs = refs[:n_ex], refs[n_ex:]
        outs, refs = refs[:n_out], refs[n_out:]
        ex_outs, refs = refs[:n_ex], refs[n_ex:]
        scratch, sems = refs[:n_scratch], refs[n_scratch:]
        if exchanges:
            first = functools.reduce(jnp.logical_and, [pl.program_id(a) == 0 for a in range(len(grid))])
            last = functools.reduce(jnp.logical_and, [pl.program_id(a) == g - 1 for a, g in enumerate(grid)])

            @pl.when(first)
            def _():
                _start_all(copies(ex_ins, ex_outs, sems))

        kern(*ins, *outs, *scratch)
        if exchanges:
            @pl.when(last)
            def _():
                _wait_all(copies(ex_ins, ex_outs, sems))

    res = pl.pallas_call(
        body, name=name, grid=grid, in_specs=list(in_specs) + [ANY] * len(ex_operands), out_specs=list(out_specs) + [ANY] * n_ex,
        out_shape=list(out_shape) + ex_results, scratch_shapes=list(scratch_shapes) + ex_scratch,
        input_output_aliases={n_in + i: n_out + o for i, o in in_place.items()},
        compiler_params=_params(len(grid)))(*operands, *ex_operands)
    return list(res[:n_out]), _split(res[n_out:], exchanges)


def _accumulate(step, n_steps, acc, value, finish):
    if n_steps == 1:
        finish(value)
        return

    @pl.when(step == 0)
    def _():
        acc[0][...] = value

    @pl.when(step > 0)
    def _():
        acc[0][...] += value

    @pl.when(step == n_steps - 1)
    def _():
        finish(acc[0][...])


def _mm_nn(a, w, *, out_dtype, name, residual=None, exchanges=(), a_part=0):
    m = a.shape[0]
    nb, k, n_blk = w.shape
    tn = _tile(n_blk, 1536, MXU)
    tm = _tile(m, 1024 if tn <= 1024 else 512, MXU)
    tk = _tile(k, 2048 if (tm > 512 and k > 4096) else 4096, MXU)
    per = n_blk // tn
    nk = k // tk

    def kern(*refs):
        a_ref, w_ref = refs[:2]
        r_ref = None if residual is None else refs[2]
        o_ref, acc = refs[2 + (residual is not None)], refs[3 + (residual is not None):]

        def finish(total):
            o_ref[...] = (total if r_ref is None else total + r_ref[...]).astype(o_ref.dtype)

        _accumulate(pl.program_id(2), nk, acc, jnp.dot(a_ref[...], w_ref[...], preferred_element_type=F32), finish)

    tile = pl.BlockSpec((tm, tn), lambda j, i, kk: (i, j))
    in_specs = [pl.BlockSpec((tm, tk), lambda j, i, kk: (i, a_part * nk + kk)),
                pl.BlockSpec((None, tk, tn), lambda j, i, kk: (j // per, kk, j % per))]
    operands = [a, w]
    if residual is not None:
        in_specs.append(tile)
        operands.append(residual)
    (out,), carried = _pallas(
        kern, name=name, grid=(nb * per, m // tm, nk), in_specs=in_specs, out_specs=[tile],
        out_shape=[jax.ShapeDtypeStruct((m, nb * n_blk), out_dtype)], operands=operands,
        scratch_shapes=[pltpu.VMEM((tm, tn), F32)] * (nk > 1), exchanges=exchanges)
    return out, carried


def _mm_nt(g, w, *, name, exchanges=(), after=None):
    m, n = g.shape
    nb, k, n_blk = w.shape
    tm, tko, tn = _tile(m, 1024, MXU), _tile(k, 1024, MXU), _tile(n_blk, 3072, MXU)
    per = n_blk // tn
    nn = n // tn

    def kern(g_ref, w_ref, o_ref, *acc):
        def finish(total):
            o_ref[...] = total

        part = lax.dot_general(g_ref[...], w_ref[...], (((1,), (1,)), ((), ())), preferred_element_type=F32)
        _accumulate(pl.program_id(2), nn, acc, part, finish)

    (out,), carried = _pallas(
        kern, name=name, grid=(k // tko, m // tm, nn),
        in_specs=[pl.BlockSpec((tm, tn), lambda j, i, jn: (i, jn)),
                  pl.BlockSpec((None, tko, tn), lambda j, i, jn: (jn // per, j, jn % per))],
        out_specs=[pl.BlockSpec((tm, tko), lambda j, i, jn: (i, j))],
        out_shape=[jax.ShapeDtypeStruct((m, k), F32)], operands=[g, w],
        scratch_shapes=[pltpu.VMEM((tm, tko), F32)] * (nn > 1), exchanges=exchanges, after=after)
    return out, carried


def _mm_tn(a, g, nb, *, name, exchanges=(), after=None, part=0, n_parts=1):
    m, k = a.shape[0], a.shape[1] // n_parts
    n = g.shape[1]
    n_blk = n // nb
    tko, tn, tm = _tile(k, 512, MXU), _tile(n_blk, 1536, MXU), _tile(m, 4096, MXU)
    per = n_blk // tn
    nm = m // tm

    def kern(a_ref, g_ref, o_ref, *acc):
        def finish(total):
            o_ref[...] = total.astype(o_ref.dtype)

        part = lax.dot_general(a_ref[...], g_ref[...], (((0,), (0,)), ((), ())), preferred_element_type=F32)
        _accumulate(pl.program_id(2), nm, acc, part, finish)

    (out,), carried = _pallas(
        kern, name=name, grid=(nb * per, k // tko, nm),
        in_specs=[pl.BlockSpec((tm, tko), lambda j, i, im: (im, part * (k // tko) + i)),
                  pl.BlockSpec((tm, tn), lambda j, i, im: (im, j))],
        out_specs=[pl.BlockSpec((None, tko, tn), lambda j, i, im: (j // per, i, j % per))],
        out_shape=[jax.ShapeDtypeStruct((nb, k, n_blk), BF16)], operands=[a, g],
        scratch_shapes=[pltpu.VMEM((tko, tn), F32)] * (nm > 1), exchanges=exchanges, after=after)
    return out, carried


ROW_TILE = 128


def _rmsnorm_fwd(x, g, *, name):
    t, d = x.shape
    tr = _tile(t, ROW_TILE, SUBLANES)

    def kern(x_ref, g_ref, h_ref):
        xv = x_ref[...]
        r = lax.rsqrt(jnp.mean(xv * xv, axis=-1, keepdims=True) + EPS)
        h_ref[...] = (xv * r * g_ref[...]).astype(BF16)

    return pl.pallas_call(
        kern, name=name, grid=(t // tr,),
        in_specs=[pl.BlockSpec((tr, d), lambda i: (i, 0)), pl.BlockSpec((1, d), lambda i: (0, 0))],
        out_specs=pl.BlockSpec((tr, d), lambda i: (i, 0)),
        out_shape=jax.ShapeDtypeStruct((t, d), BF16), compiler_params=_params(1))(x, g)


def _rmsnorm_bwd(x, g, d_h_parts, d_res, *, name):
    t, d = x.shape
    tr = _tile(t, ROW_TILE, SUBLANES)
    n_parts = len(d_h_parts)

    def kern(x_ref, g_ref, *refs):
        dh_refs, (dres_ref, dx_ref, dxb_ref, dg_ref) = refs[:n_parts], refs[n_parts:]
        xv = x_ref[...]
        r = lax.rsqrt(jnp.mean(xv * xv, axis=-1, keepdims=True) + EPS)
        dh = jnp.concatenate([ref[...] for ref in dh_refs], axis=1)
        gy = dh * g_ref[...]
        dx = dres_ref[...] + r * gy - xv * (r * r * r) * jnp.mean(gy * xv, axis=-1, keepdims=True)
        dx_ref[...] = dx
        dxb_ref[...] = dx.astype(BF16)

        @pl.when(pl.program_id(0) == 0)
        def _():
            dg_ref[...] = jnp.zeros_like(dg_ref)

        dg_ref[...] += jnp.sum(dh * (xv * r), axis=0, keepdims=True)

    row = pl.BlockSpec((tr, d), lambda i: (i, 0))
    vec = pl.BlockSpec((1, d), lambda i: (0, 0))
    part = pl.BlockSpec((tr, d // n_parts), lambda i: (i, 0))
    return pl.pallas_call(
        kern, name=name, grid=(t // tr,), in_specs=[row, vec] + [part] * n_parts + [row], out_specs=[row, row, vec],
        out_shape=[jax.ShapeDtypeStruct((t, d), F32), jax.ShapeDtypeStruct((t, d), BF16),
                   jax.ShapeDtypeStruct((1, d), F32)], compiler_params=_params(1))(x, g, *d_h_parts, d_res)


def _loss_head(x, g, target, *, name):
    t, d = x.shape
    tr = _tile(t, ROW_TILE, SUBLANES)

    def kern(x_ref, g_ref, t_ref, dx_ref, dxb_ref, dg_ref, loss_ref):
        xv = x_ref[...]
        gv = g_ref[...]
        r = lax.rsqrt(jnp.mean(xv * xv, axis=-1, keepdims=True) + EPS)
        diff = xv * r * gv - t_ref[...]
        dy = diff * (1.0 / d)
        gy = dy * gv
        dx = r * gy - xv * (r * r * r) * jnp.mean(gy * xv, axis=-1, keepdims=True)
        dx_ref[...] = dx
        dxb_ref[...] = dx.astype(BF16)

        @pl.when(pl.program_id(0) == 0)
        def _():
            dg_ref[...] = jnp.zeros_like(dg_ref)
            loss_ref[...] = jnp.zeros_like(loss_ref)

        dg_ref[...] += jnp.sum(dy * (xv * r), axis=0, keepdims=True)
        part = 0.5 * jnp.sum(jnp.mean(diff * diff, axis=-1, keepdims=True), axis=0, keepdims=True)
        loss_ref[...] += jnp.broadcast_to(part, loss_ref.shape)

    row = pl.BlockSpec((tr, d), lambda i: (i, 0))
    vec = pl.BlockSpec((1, d), lambda i: (0, 0))
    return pl.pallas_call(
        kern, name=name, grid=(t // tr,), in_specs=[row, vec, row],
        out_specs=[row, row, vec, pl.BlockSpec((1, LANES), lambda i: (0, 0))],
        out_shape=[jax.ShapeDtypeStruct((t, d), F32), jax.ShapeDtypeStruct((t, d), BF16),
                   jax.ShapeDtypeStruct((1, d), F32), jax.ShapeDtypeStruct((1, LANES), F32)],
        compiler_params=_params(1))(x, g, target)


def _merge_fwd(m_a, m_b, proj, gate_col, *, name):
    t, d = m_a.shape
    tr = _tile(t, ROW_TILE, SUBLANES)

    def kern(ma_ref, mb_ref, ga_ref, gb_ref, o_ref):
        o_ref[...] = (_sigmoid(ga_ref[...]) * ma_ref[...] + _sigmoid(gb_ref[...]) * mb_ref[...]).astype(BF16)

    row = pl.BlockSpec((tr, d), lambda i: (i, 0))
    return pl.pallas_call(
        kern, name=name, grid=(t // tr,),
        in_specs=[row, row, pl.BlockSpec((tr, d), lambda i: (i, gate_col)),
                  pl.BlockSpec((tr, d), lambda i: (i, gate_col + 1))],
        out_specs=row, out_shape=jax.ShapeDtypeStruct((t, d), BF16), compiler_params=_params(1))(m_a, m_b, proj, proj)


def _merge_bwd(d_merged, m_a, m_b, proj, gate_col, *, name, after=None):
    t, d = m_a.shape
    tr = _tile(t, ROW_TILE, SUBLANES)

    def kern(dm_ref, ma_ref, mb_ref, ga_ref, gb_ref, dma_ref, dmb_ref, dg_ref):
        dm = dm_ref[...]
        sa = _sigmoid(ga_ref[...])
        sb = _sigmoid(gb_ref[...])
        dma_ref[...] = (dm * sa).astype(BF16)
        dmb_ref[...] = (dm * sb).astype(BF16)
        dg_ref[:, 0:d] = (dm * ma_ref[...] * (sa * (1.0 - sa))).astype(BF16)
        dg_ref[:, d:2 * d] = (dm * mb_ref[...] * (sb * (1.0 - sb))).astype(BF16)

    row = pl.BlockSpec((tr, d), lambda i: (i, 0))
    res, _ = _pallas(
        kern, name=name, grid=(t // tr,),
        in_specs=[row, row, row, pl.BlockSpec((tr, d), lambda i: (i, gate_col)),
                  pl.BlockSpec((tr, d), lambda i: (i, gate_col + 1))],
        out_specs=[row, row, pl.BlockSpec((tr, 2 * d), lambda i: (i, 0))],
        out_shape=[jax.ShapeDtypeStruct((t, d), BF16), jax.ShapeDtypeStruct((t, d), BF16),
                   jax.ShapeDtypeStruct((t, 2 * d), BF16)], operands=[d_merged, m_a, m_b, proj, proj], after=after)
    return res


def _tril_bf16(w, transposed):
    row = lax.broadcasted_iota(jnp.int32, w.shape, 0)
    col = lax.broadcasted_iota(jnp.int32, w.shape, 1)
    keep = (row <= col) if transposed else (row >= col)
    return jnp.where(keep, w, 0.0).astype(BF16)


def _layernorm_stats(v):
    mu = jnp.mean(v, axis=-1, keepdims=True)
    vc = v - mu
    rstd = lax.rsqrt(jnp.mean(vc * vc, axis=-1, keepdims=True) + EPS)
    return vc * rstd, rstd


def _mixer_a_fwd(proj, ln_g, ln_b, sg_w, sg_b_t, sgw, *, name):
    t = proj.shape[0]
    gd = sgw // SG_GROUPS

    def kern(zu_ref, zv_ref, g_ref, b_ref, w_ref, bt_ref, o_ref):
        xhat, _ = _layernorm_stats(_gelu(zv_ref[...]))
        vn = (xhat * g_ref[...] + b_ref[...]).astype(BF16)
        for g in range(SG_GROUPS):
            cols = slice(g * gd, (g + 1) * gd)
            mixed = jnp.dot(_tril_bf16(w_ref[g], False), vn[:, cols], preferred_element_type=F32) + bt_ref[:, g:g + 1]
            o_ref[:, cols] = (_gelu(zu_ref[:, cols]) * mixed).astype(BF16)

    vec = pl.BlockSpec((1, sgw), lambda i: (0, 0))
    return pl.pallas_call(
        kern, name=name, grid=(t // CHUNK,),
        in_specs=[pl.BlockSpec((CHUNK, sgw), lambda i: (i, 0)), pl.BlockSpec((CHUNK, sgw), lambda i: (i, 1)), vec, vec,
                  pl.BlockSpec((SG_GROUPS, CHUNK, CHUNK), lambda i: (0, 0, 0)),
                  pl.BlockSpec((CHUNK, SG_GROUPS), lambda i: (0, 0))],
        out_specs=pl.BlockSpec((CHUNK, sgw), lambda i: (i, 0)),
        out_shape=jax.ShapeDtypeStruct((t, sgw), BF16), compiler_params=_params(1))(proj, proj, ln_g, ln_b, sg_w, sg_b_t)


def _mixer_a_bwd(proj, d_ya, ln_g, ln_b, sg_w, sg_w_t, sg_b_t, sgw, *, name):
    t = proj.shape[0]
    gd = sgw // SG_GROUPS

    def kern(zu_ref, zv_ref, dy_ref, g_ref, b_ref, w_ref, wt_ref, bt_ref, dz_ref, dw_ref, dbt_ref, dg_ref, db_ref, dvn):
        @pl.when(pl.program_id(0) == 0)
        def _():
            dw_ref[...] = jnp.zeros_like(dw_ref)
            dbt_ref[...] = jnp.zeros_like(dbt_ref)
            dg_ref[...] = jnp.zeros_like(dg_ref)
            db_ref[...] = jnp.zeros_like(db_ref)

        gv, dgv = _gelu_and_grad(zv_ref[...])
        xhat, rstd = _layernorm_stats(gv)
        ln_gain = g_ref[...]
        vn = (xhat * ln_gain + b_ref[...]).astype(BF16)
        for g in range(SG_GROUPS):
            cols = slice(g * gd, (g + 1) * gd)
            gu, dgu = _gelu_and_grad(zu_ref[:, cols])
            mixed = jnp.dot(_tril_bf16(w_ref[g], False), vn[:, cols], preferred_element_type=F32) + bt_ref[:, g:g + 1]
            dy = dy_ref[:, cols]
            dz_ref[:, cols] = (dy * mixed * dgu).astype(BF16)
            d_mixed = dy * gu
            d_mixed_b = d_mixed.astype(BF16)
            dvn[:, cols] = jnp.dot(_tril_bf16(wt_ref[g], True), d_mixed_b, preferred_element_type=F32)
            d_w = lax.dot_general(d_mixed_b, vn[:, cols], (((1,), (1,)), ((), ())), preferred_element_type=F32)
            row = lax.broadcasted_iota(jnp.int32, d_w.shape, 0)
            col = lax.broadcasted_iota(jnp.int32, d_w.shape, 1)
            dw_ref[g] += jnp.where(row >= col, d_w, 0.0)
            dbt_ref[:, g:g + 1] += jnp.sum(d_mixed, axis=-1, keepdims=True)
        d_vn = dvn[...]
        dg_ref[...] += jnp.sum(d_vn * xhat, axis=0, keepdims=True)
        db_ref[...] += jnp.sum(d_vn, axis=0, keepdims=True)
        d_xhat = d_vn * ln_gain
        d_gv = rstd * (d_xhat - jnp.mean(d_xhat, axis=-1, keepdims=True)
                       - xhat * jnp.mean(d_xhat * xhat, axis=-1, keepdims=True))
        dz_ref[:, sgw:2 * sgw] = (d_gv * dgv).astype(BF16)

    vec = pl.BlockSpec((1, sgw), lambda i: (0, 0))
    wspec = pl.BlockSpec((SG_GROUPS, CHUNK, CHUNK), lambda i: (0, 0, 0))
    btspec = pl.BlockSpec((CHUNK, SG_GROUPS), lambda i: (0, 0))
    return pl.pallas_call(
        kern, name=name, grid=(t // CHUNK,),
        in_specs=[pl.BlockSpec((CHUNK, sgw), lambda i: (i, 0)), pl.BlockSpec((CHUNK, sgw), lambda i: (i, 1)),
                  pl.BlockSpec((CHUNK, sgw), lambda i: (i, 0)), vec, vec, wspec, wspec, btspec],
        out_specs=[pl.BlockSpec((CHUNK, 2 * sgw), lambda i: (i, 0)), wspec, btspec, vec, vec],
        out_shape=[jax.ShapeDtypeStruct((t, 2 * sgw), BF16), jax.ShapeDtypeStruct((SG_GROUPS, CHUNK, CHUNK), F32),
                   jax.ShapeDtypeStruct((CHUNK, SG_GROUPS), F32), jax.ShapeDtypeStruct((1, sgw), F32),
                   jax.ShapeDtypeStruct((1, sgw), F32)],
        scratch_shapes=[pltpu.VMEM((CHUNK, sgw), F32)],
        compiler_params=_params(1))(proj, proj, d_ya, ln_g, ln_b, sg_w, sg_w_t, sg_b_t)


def _scan_rows(a_ref, h_ref, reverse):
    s, c = a_ref.shape
    nblk = s // SUBLANES
    a, b = a_ref[...], h_ref[...]
    row = jnp.bitwise_and(lax.broadcasted_iota(jnp.int32, (s, c), 0), SUBLANES - 1)
    for d in (1, 2, 4):
        inside = (row < SUBLANES - d) if reverse else (row >= d)
        shift = s - d if reverse else d
        b = a * jnp.where(inside, pltpu.roll(b, shift, 0), 0.0) + b
        a = a * jnp.where(inside, pltpu.roll(a, shift, 0), 1.0)
    a_ref[...] = a
    h_ref[...] = b
    leaving = 0 if reverse else SUBLANES - 1

    def chain(i, carry):
        r0 = pl.multiple_of((nblk - 1 - i if reverse else i) * SUBLANES, SUBLANES)
        h = a_ref[pl.ds(r0, SUBLANES), :] * carry + h_ref[pl.ds(r0, SUBLANES), :]
        h_ref[pl.ds(r0, SUBLANES), :] = h
        return jnp.broadcast_to(h[leaving:leaving + 1, :], (SUBLANES, c))

    lax.fori_loop(0, nblk, chain, jnp.zeros((SUBLANES, c), F32), unroll=8)


def _lru_gates(xc, wa_ref, ba_ref, wx_ref, bx_ref, lam_ref):
    xcb = xc.astype(BF16)
    ra = _sigmoid(jnp.dot(xcb, wa_ref[...].astype(BF16), preferred_element_type=F32) + ba_ref[...])
    ia = _sigmoid(jnp.dot(xcb, wx_ref[...].astype(BF16), preferred_element_type=F32) + bx_ref[...])
    neg = -lam_ref[...]
    sp = jnp.maximum(neg, 0.0) + jnp.log1p(jnp.exp(-jnp.abs(neg)))
    log_a = -LRU_C * ra * sp
    a = jnp.exp(log_a)
    a2 = jnp.exp(2.0 * log_a)
    sq = jnp.sqrt(-jnp.tanh(log_a) * (a2 + 1.0))
    return ra, ia, sp, a, a2, sq


def _mixer_b_specs(seq, hd, sgw, lw):
    x_col = (2 * sgw) // hd
    y_col = (2 * sgw + lw) // hd
    tile = lambda col: pl.BlockSpec((seq, hd), lambda h, b: (b, col + h))
    vec = pl.BlockSpec((1, hd), lambda h, b: (0, h))
    mat = pl.BlockSpec((None, hd, hd), lambda h, b: (h, 0, 0))
    return tile(x_col), tile(y_col), tile(0), vec, mat


def _mixer_b_fwd(proj, conv_w, conv_b, wa, ba, wx, bx, lam, *, seq, sgw, lw, name):
    t = proj.shape[0]
    hd = lw // LRU_HEADS
    k_taps = conv_w.shape[0]
    x_spec, y_spec, o_spec, vec, mat = _mixer_b_specs(seq, hd, sgw, lw)

    def kern(xr_ref, yr_ref, cw_ref, cb_ref, wa_ref, ba_ref, wx_ref, bx_ref, lam_ref, o_ref, s_a, s_h):
        xc = _causal_conv(xr_ref[...], cw_ref[...], cb_ref[...])
        _, ia, _, a, _, sq = _lru_gates(xc, wa_ref, ba_ref, wx_ref, bx_ref, lam_ref)
        s_a[...] = a
        s_h[...] = sq * (ia * xc)
        _scan_rows(s_a, s_h, False)
        o_ref[...] = (s_h[...] * _gelu(yr_ref[...])).astype(BF16)

    return pl.pallas_call(
        kern, name=name, grid=(LRU_HEADS, t // seq),
        in_specs=[x_spec, y_spec, pl.BlockSpec((k_taps, hd), lambda h, b: (0, h)), vec, mat, vec, mat, vec, vec],
        out_specs=o_spec, out_shape=jax.ShapeDtypeStruct((t, lw), BF16),
        scratch_shapes=[pltpu.VMEM((seq, hd), F32), pltpu.VMEM((seq, hd), F32)],
        compiler_params=_params(2))(proj, proj, conv_w, conv_b, wa, ba, wx, bx, lam)


def _mixer_b_bwd(proj, d_yb, conv_w, conv_b, wa, wa_t, ba, wx, wx_t, bx, lam, *, seq, sgw, lw, name, exchanges=()):
    t = proj.shape[0]
    hd = lw // LRU_HEADS
    k_taps = conv_w.shape[0]
    x_spec, y_spec, o_spec, vec, mat = _mixer_b_specs(seq, hd, sgw, lw)
    cw_spec = pl.BlockSpec((k_taps, hd), lambda h, b: (0, h))

    def kern(xr_ref, yr_ref, dyb_ref, cw_ref, cb_ref, wa_ref, wat_ref, ba_ref, wx_ref, wxt_ref, bx_ref, lam_ref,
             dxr_ref, dyr_ref, dcw_ref, dcb_ref, dwa_ref, dba_ref, dwx_ref, dbx_ref, dlam_ref,
             s_xc, s_a, s_h, s_lam, s_dpa, s_dpx):
        @pl.when(pl.program_id(1) == 0)
        def _():
            for ref in (dcw_ref, dcb_ref, dwa_ref, dba_ref, dwx_ref, dbx_ref, dlam_ref):
                ref[...] = jnp.zeros_like(ref)

        s_xc[...] = _causal_conv(xr_ref[...], cw_ref[...], cb_ref[...])
        _, ia, _, a, _, sq = _lru_gates(s_xc[...], wa_ref, ba_ref, wx_ref, bx_ref, lam_ref)
        s_a[...] = a
        s_dpa[...] = _shift_up(a, 1)
        s_h[...] = sq * (ia * s_xc[...])
        _scan_rows(s_a, s_h, False)

        gel, dgel = _gelu_and_grad(yr_ref[...])
        dyb = dyb_ref[...]
        dyr_ref[...] = (dyb * s_h[...] * dgel).astype(BF16)
        s_lam[...] = dyb * gel
        _scan_rows(s_dpa, s_lam, True)
        ra, ia, sp, a, a2, sq = _lru_gates(s_xc[...], wa_ref, ba_ref, wx_ref, bx_ref, lam_ref)
        d_gx = s_lam[...]
        d_a = d_gx * _shift_down(s_h[...], 1)
        xc = s_xc[...]
        d_sq = d_gx * (ia * xc)
        d_ia = d_gx * (sq * xc)
        d_log_a = d_a * a - d_sq * (a2 / sq)
        d_ra = d_log_a * (-LRU_C * sp)
        d_sp = jnp.sum(d_log_a * (-LRU_C * ra), axis=0, keepdims=True)
        dlam_ref[...] += d_sp * (-_sigmoid(-lam_ref[...]))
        d_pa = d_ra * (ra * (1.0 - ra))
        d_px = d_ia * (ia * (1.0 - ia))
        s_dpa[...] = d_pa
        s_dpx[...] = d_px
        dba_ref[...] += jnp.sum(d_pa, axis=0, keepdims=True)
        dbx_ref[...] += jnp.sum(d_px, axis=0, keepdims=True)
        xcb = s_xc[...].astype(BF16)
        d_pa_b = s_dpa[...].astype(BF16)
        d_px_b = s_dpx[...].astype(BF16)
        contract_rows = (((0,), (0,)), ((), ()))
        dwa_ref[...] += lax.dot_general(xcb, d_pa_b, contract_rows, preferred_element_type=F32)
        dwx_ref[...] += lax.dot_general(xcb, d_px_b, contract_rows, preferred_element_type=F32)
        d_xc = (s_lam[...] * (sq * ia)
                + jnp.dot(d_pa_b, wat_ref[...].astype(BF16), preferred_element_type=F32)
                + jnp.dot(d_px_b, wxt_ref[...].astype(BF16), preferred_element_type=F32))
        dcb_ref[...] += jnp.sum(d_xc, axis=0, keepdims=True)
        dcw_ref[...] += _causal_conv_bwd_w(d_xc, xr_ref[...], k_taps)
        dxr_ref[...] = _causal_conv_bwd_x(d_xc, cw_ref[...]).astype(BF16)

    tile_shape = jax.ShapeDtypeStruct((t, lw), BF16)
    vec_shape = jax.ShapeDtypeStruct((1, lw), F32)
    mat_shape = jax.ShapeDtypeStruct((LRU_HEADS, hd, hd), F32)
    return _pallas(
        kern, name=name, grid=(LRU_HEADS, t // seq),
        in_specs=[x_spec, y_spec, o_spec, cw_spec, vec, mat, mat, vec, mat, mat, vec, vec],
        out_specs=[o_spec, o_spec, cw_spec, vec, mat, vec, mat, vec, vec],
        out_shape=[tile_shape, tile_shape, jax.ShapeDtypeStruct((k_taps, lw), F32), vec_shape, mat_shape, vec_shape,
                   mat_shape, vec_shape, vec_shape],
        operands=[proj, proj, d_yb, conv_w, conv_b, wa, wa_t, ba, wx, wx_t, bx, lam],
        scratch_shapes=[pltpu.VMEM((seq, hd), F32)] * 6, exchanges=exchanges)


FFN_TILE = 256


def _ffn_mid_fwd(up_pre, conv_w, conv_b, *, seq, name, exchanges=()):
    t, f2 = up_pre.shape
    f = f2 // 2
    tc = _tile(f, FFN_TILE, LANES)
    nf = f // tc
    k_taps = conv_w.shape[0]

    def kern(pg_ref, pv_ref, wg_ref, wv_ref, bg_ref, bv_ref, o_ref):
        cg = _causal_conv(pg_ref[...], wg_ref[...], bg_ref[...])
        cv = _causal_conv(pv_ref[...], wv_ref[...], bv_ref[...])
        o_ref[...] = (_gelu(cg) * cv).astype(BF16)

    tile = lambda off: pl.BlockSpec((seq, tc), lambda j, b: (b, off + j))
    wspec = lambda off: pl.BlockSpec((k_taps, tc), lambda j, b: (0, off + j))
    bspec = lambda off: pl.BlockSpec((1, tc), lambda j, b: (0, off + j))
    (act,), carried = _pallas(
        kern, name=name, grid=(nf, t // seq),
        in_specs=[tile(0), tile(nf), wspec(0), wspec(nf), bspec(0), bspec(nf)], out_specs=[tile(0)],
        out_shape=[jax.ShapeDtypeStruct((t, f), BF16)],
        operands=[up_pre, up_pre, conv_w, conv_w, conv_b, conv_b], exchanges=exchanges)
    return act, carried


def _ffn_mid_bwd(up_pre, d_act, conv_w, conv_b, *, seq, name):
    t, f2 = up_pre.shape
    f = f2 // 2
    tc = _tile(f, FFN_TILE, LANES)
    nf = f // tc
    k_taps = conv_w.shape[0]

    def kern(pg_ref, pv_ref, da_ref, wg_ref, wv_ref, bg_ref, bv_ref, dpg_ref, dpv_ref, dwg_ref, dwv_ref, dbg_ref, dbv_ref):
        @pl.when(pl.program_id(1) == 0)
        def _():
            for ref in (dwg_ref, dwv_ref, dbg_ref, dbv_ref):
                ref[...] = jnp.zeros_like(ref)

        pg = pg_ref[...]
        pv = pv_ref[...]
        gel, dgel = _gelu_and_grad(_causal_conv(pg, wg_ref[...], bg_ref[...]))
        cv = _causal_conv(pv, wv_ref[...], bv_ref[...])
        d_act_v = da_ref[...]
        d_cg = d_act_v * cv * dgel
        d_cv = d_act_v * gel
        dpg_ref[...] = _causal_conv_bwd_x(d_cg, wg_ref[...]).astype(BF16)
        dpv_ref[...] = _causal_conv_bwd_x(d_cv, wv_ref[...]).astype(BF16)
        dwg_ref[...] += _causal_conv_bwd_w(d_cg, pg, k_taps)
        dwv_ref[...] += _causal_conv_bwd_w(d_cv, pv, k_taps)
        dbg_ref[...] += jnp.sum(d_cg, axis=0, keepdims=True)
        dbv_ref[...] += jnp.sum(d_cv, axis=0, keepdims=True)

    tile = lambda off: pl.BlockSpec((seq, tc), lambda j, b: (b, off + j))
    wspec = lambda off: pl.BlockSpec((k_taps, tc), lambda j, b: (0, off + j))
    bspec = lambda off: pl.BlockSpec((1, tc), lambda j, b: (0, off + j))
    half = jax.ShapeDtypeStruct((t, f), BF16)
    wshape = jax.ShapeDtypeStruct((k_taps, f), F32)
    bshape = jax.ShapeDtypeStruct((1, f), F32)
    d_pg, d_pv, d_wg, d_wv, d_bg, d_bv = pl.pallas_call(
        kern, name=name, grid=(nf, t // seq),
        in_specs=[tile(0), tile(nf), tile(0), wspec(0), wspec(nf), bspec(0), bspec(nf)],
        out_specs=[tile(0), tile(0), wspec(0), wspec(0), bspec(0), bspec(0)],
        out_shape=[half, half, wshape, wshape, bshape, bshape],
        compiler_params=_params(2))(up_pre, up_pre, d_act, conv_w, conv_w, conv_b, conv_b)
    return (jnp.concatenate([d_pg, d_pv], axis=1), jnp.concatenate([d_wg, d_wv], axis=1),
            jnp.concatenate([d_bg, d_bv], axis=1))


ELEM_VMEM_BYTES = 24 << 20


def _as_2d(a):
    if a.ndim >= 2 and a.shape[-1] % LANES == 0 and a.size // a.shape[-1] >= SUBLANES:
        return a.reshape(-1, a.shape[-1])
    return a.reshape(-1, LANES)


def _row_tile(rows, bytes_per_row):
    return _tile(rows, max(16, ELEM_VMEM_BYTES // (2 * bytes_per_row)), 16)


def _cast_bf16(a, *, name, part=0, n_parts=1, after=None):
    v = _as_2d(a)
    rows, cols = v.shape[0] // n_parts, v.shape[1]
    tr = _row_tile(rows, cols * (4 + 2))
    first = part * (rows // tr)

    def kern(x_ref, o_ref):
        o_ref[...] = x_ref[...].astype(BF16)

    (out,), _ = _pallas(kern, name=name, grid=(rows // tr,), in_specs=[pl.BlockSpec((tr, cols), lambda i: (first + i, 0))],
                        out_specs=[pl.BlockSpec((tr, cols), lambda i: (i, 0))],
                        out_shape=[jax.ShapeDtypeStruct((rows, cols), BF16)], operands=[v], after=after)
    return out.reshape(a.shape) if n_parts == 1 else out


def _add_sibling_part(own, core, got, *, name):
    _, _, rows, cols = own.shape
    tr = _row_tile(rows, cols * (2 + 2 + 2))

    def kern(core_ref, a_ref, b_ref, o_ref):
        o_ref[...] = (a_ref[...].astype(F32) + b_ref[...].astype(F32)).astype(BF16)

    spec = pl.BlockSpec((None, tr, cols), lambda ch, i, core_ref: (ch, i, 0))
    grid_spec = pltpu.PrefetchScalarGridSpec(
        num_scalar_prefetch=1, grid=(4, rows // tr),
        in_specs=[pl.BlockSpec((None, None, tr, cols), lambda ch, i, core_ref: (ch, core_ref[0], i, 0)), spec],
        out_specs=spec)
    return pl.pallas_call(kern, name=name, grid_spec=grid_spec, out_shape=jax.ShapeDtypeStruct(got.shape, BF16),
                          compiler_params=_params(2))(core, own, got)


def _sum_parts(parts, *, name):
    n_parts, rows, cols = parts.shape
    tr = _row_tile(rows, cols * 4 * (n_parts + 1))

    def kern(p_ref, o_ref):
        acc = p_ref[0].astype(F32)
        for p in range(1, n_parts):
            acc = acc + p_ref[p].astype(F32)
        o_ref[...] = acc

    return pl.pallas_call(
        kern, name=name, grid=(rows // tr,), in_specs=[pl.BlockSpec((n_parts, tr, cols), lambda i: (0, i, 0))],
        out_specs=pl.BlockSpec((tr, cols), lambda i: (i, 0)), out_shape=jax.ShapeDtypeStruct((rows, cols), F32),
        compiler_params=_params(1))(parts)


def _adamw(w, m, v, grad_chunks, *, name):
    shape = w.shape
    w2 = _as_2d(w)
    rows, cols = w2.shape
    n_chunks = len(grad_chunks)
    n_parts = grad_chunks[0].shape[0]
    chunks = [c.reshape(n_parts, rows // n_chunks, cols) for c in grad_chunks]
    tr = _row_tile(rows // n_chunks, cols * (3 * 4 + n_chunks * n_parts * chunks[0].dtype.itemsize + 4 * 4))
    per_chunk = rows // n_chunks // tr
    c_m = 1.0 - ADAM_B1 ** ADAM_STEP
    c_v = 1.0 - ADAM_B2 ** ADAM_STEP

    def kern(w_ref, m_ref, v_ref, *refs):
        p_refs, (g_ref, d_ref, nm_ref, nv_ref) = refs[:n_chunks], refs[n_chunks:]
        g = None
        for k, p_ref in enumerate(p_refs):
            total = p_ref[0].astype(F32)
            for p in range(1, n_parts):
                total = total + p_ref[p].astype(F32)
            g = total if g is None else jnp.where(pl.program_id(0) // per_chunk == k, total, g)
        new_m = ADAM_B1 * m_ref[...] + (1.0 - ADAM_B1) * g
        new_v = ADAM_B2 * v_ref[...] + (1.0 - ADAM_B2) * (g * g)
        g_ref[...] = g
        nm_ref[...] = new_m
        nv_ref[...] = new_v
        d_ref[...] = -ADAM_LR * ((new_m / c_m) / (jnp.sqrt(new_v / c_v) + ADAM_EPS) + ADAM_WD * w_ref[...])

    spec = pl.BlockSpec((tr, cols), lambda i: (i, 0))
    out = jax.ShapeDtypeStruct((rows, cols), F32)
    def chunk_spec(k):
        return pl.BlockSpec((n_parts, tr, cols), lambda i: (0, jnp.clip(i - k * per_chunk, 0, per_chunk - 1), 0))

    res = pl.pallas_call(
        kern, name=name, grid=(rows // tr,),
        in_specs=[spec, spec, spec] + [chunk_spec(k) for k in range(n_chunks)],
        out_specs=[spec] * 4, out_shape=[out] * 4, compiler_params=_params(1))(w2, _as_2d(m), _as_2d(v), *chunks)
    return [r.reshape(shape) for r in res]


def _all_gather(shards, *, name):
    n = len(shards)

    def body(*refs):
        ins, outs = refs[:n], refs[n:2 * n]
        send_sems, recv_sems, local_sems = refs[2 * n:]
        x, y, c = _place()
        me, sibling = (x, y, c), (x, y, 1 - c)
        chips = [(1 - x, y), (x, 1 - y), (1 - x, 1 - y)]

        def slot(i, dev):
            return outs[i].at[4 * dev[0] + 2 * dev[1] + dev[2]]

        def copy(i, k, block, to, src=None):
            return pltpu.make_async_remote_copy(
                src_ref=slot(i, block) if src is None else src, dst_ref=slot(i, block),
                send_sem=send_sems.at[i, k], recv_sem=recv_sems.at[i, k], device_id=to, device_id_type=MESH)

        mine = [pltpu.make_async_copy(ins[i], slot(i, me), local_sems.at[i]) for i in range(n)]
        for cp in mine:
            cp.start()
        first = []
        for i in range(n):
            first.append(copy(i, 0, me, sibling, src=ins[i]))
            first += [copy(i, 1 + j, me, (*chip, c), src=ins[i]) for j, chip in enumerate(chips)]
        for cp in first:
            cp.start()
        passed = []
        for j, chip in enumerate(chips):
            for i in range(n):
                copy(i, 1 + j, (*chip, c), me).wait_recv()
                onward = copy(i, 4 + j, (*chip, c), sibling)
                onward.start()
                passed.append(onward)
        for i in range(n):
            copy(i, 0, sibling, me).wait_recv()
            for j, chip in enumerate(chips):
                copy(i, 4 + j, (*chip, 1 - c), me).wait_recv()
        for cp in first + passed:
            cp.wait_send()
        for cp in mine:
            cp.wait()

    return pl.pallas_call(
        body, name=name, in_specs=[ANY] * n, out_specs=[ANY] * n,
        out_shape=[jax.ShapeDtypeStruct((N_DEV,) + s.shape, s.dtype) for s in shards],
        scratch_shapes=[pltpu.SemaphoreType.DMA((n, 7)), pltpu.SemaphoreType.DMA((n, 7)), pltpu.SemaphoreType.DMA((n,))],
    )(*shards)


def _by_chip_and_core(grad):
    return grad.reshape(4, 2, -1, grad.shape[-1])


def _pack(vectors):
    flat = [v.reshape(-1).astype(F32) for v in vectors]
    sizes = [f.shape[0] for f in flat]
    total = sum(sizes)
    padded = -(-total // (SUBLANES * LANES)) * (SUBLANES * LANES)
    if padded > total:
        flat.append(jnp.zeros((padded - total,), F32))
    return jnp.concatenate(flat).reshape(-1, LANES), sizes


def _unpack(packed, sizes, shapes):
    flat = packed.reshape(-1)
    out, off = [], 0
    for size, shape in zip(sizes, shapes):
        out.append(flat[off:off + size].reshape(shape))
        off += size
    return out


def kernel(x, g_mix, w_in, sg_ln_g, sg_ln_b, sg_w, sg_b, lru_conv_w, lru_conv_b, lru_wa, lru_ba, lru_wx, lru_bx, lru_lam, p_sg, p_lru, w_out, g_ffn, w_up, ffn_conv_w, ffn_conv_b, w_down, g_final, loss_target, m_g_mix, m_w_in, m_sg_ln_g, m_sg_ln_b, m_sg_w, m_sg_b, m_lru_conv_w, m_lru_conv_b, m_lru_wa, m_lru_ba, m_lru_wx, m_lru_bx, m_lru_lam, m_p_sg, m_p_lru, m_w_out, m_g_ffn, m_w_up, m_ffn_conv_w, m_ffn_conv_b, m_w_down, m_g_final, v_g_mix, v_w_in, v_sg_ln_g, v_sg_ln_b, v_sg_w, v_sg_b, v_lru_conv_w, v_lru_conv_b, v_lru_wa, v_lru_ba, v_lru_wx, v_lru_bx, v_lru_lam, v_p_sg, v_p_lru, v_w_out, v_g_ffn, v_w_up, v_ffn_conv_w, v_ffn_conv_b, v_w_down, v_g_final):
    weights = dict(g_mix=g_mix, w_in=w_in, sg_ln_g=sg_ln_g, sg_ln_b=sg_ln_b, sg_w=sg_w, sg_b=sg_b, lru_conv_w=lru_conv_w,
                   lru_conv_b=lru_conv_b, lru_wa=lru_wa, lru_ba=lru_ba, lru_wx=lru_wx, lru_bx=lru_bx, lru_lam=lru_lam,
                   p_sg=p_sg, p_lru=p_lru, w_out=w_out, g_ffn=g_ffn, w_up=w_up, ffn_conv_w=ffn_conv_w,
                   ffn_conv_b=ffn_conv_b, w_down=w_down, g_final=g_final)
    m_in = dict(g_mix=m_g_mix, w_in=m_w_in, sg_ln_g=m_sg_ln_g, sg_ln_b=m_sg_ln_b, sg_w=m_sg_w, sg_b=m_sg_b,
                lru_conv_w=m_lru_conv_w, lru_conv_b=m_lru_conv_b, lru_wa=m_lru_wa, lru_ba=m_lru_ba, lru_wx=m_lru_wx,
                lru_bx=m_lru_bx, lru_lam=m_lru_lam, p_sg=m_p_sg, p_lru=m_p_lru, w_out=m_w_out, g_ffn=m_g_ffn,
                w_up=m_w_up, ffn_conv_w=m_ffn_conv_w, ffn_conv_b=m_ffn_conv_b, w_down=m_w_down, g_final=m_g_final)
    v_in = dict(g_mix=v_g_mix, w_in=v_w_in, sg_ln_g=v_sg_ln_g, sg_ln_b=v_sg_ln_b, sg_w=v_sg_w, sg_b=v_sg_b,
                lru_conv_w=v_lru_conv_w, lru_conv_b=v_lru_conv_b, lru_wa=v_lru_wa, lru_ba=v_lru_ba, lru_wx=v_lru_wx,
                lru_bx=v_lru_bx, lru_lam=v_lru_lam, p_sg=v_p_sg, p_lru=v_p_lru, w_out=v_w_out, g_ffn=v_g_ffn,
                w_up=v_w_up, ffn_conv_w=v_ffn_conv_w, ffn_conv_b=v_ffn_conv_b, w_down=v_w_down, g_final=v_g_final)
    order = list(weights)

    n_seq, seq, d = x.shape
    t = n_seq * seq
    sgw = sg_ln_g.shape[-1]
    lw = lru_lam.shape[-1]
    hd = lw // LRU_HEADS
    f2 = ffn_conv_b.shape[-1]
    gate_col = (2 * sgw + 2 * lw) // d
    xi, yi, ci = _place()
    dev = 4 * xi + 2 * yi + ci

    core = jnp.reshape(ci, (1,)).astype(jnp.int32)
    first_leg = functools.partial(_gather_first_leg, place_own=False)
    chip_swap = functools.partial(_swap_with_chips, place_own=False)
    shards, in_flight = {}, {}

    def landed(keys, after, name):
        return _await_exchange(first_leg, [in_flight[k] for k in keys], after, name=name)[1]

    needed_next = ["w_in_bottom", "lru_wa", "lru_wx", "p_sg", "taps"]
    by_need = ["w_in_top"] + needed_next + ["p_lru", "w_out", "w_up", "w_down"]
    def token():
        return jnp.zeros((SUBLANES, LANES), F32)

    def start_gather(keys, before, name):
        started, before = _start_exchange(first_leg, [shards[k] for k in keys],
                                          [_own_block_in_place(shards[k], dev) for k in keys], before, name=name)
        in_flight.update(zip(keys, started))
        return before

    shards["w_in_top"] = _cast_bf16(w_in[0], name="cast_w_in_top", part=0, n_parts=2)
    top_started = start_gather(["w_in_top"], token(), "start_gather_w_in_top")
    shards["w_in_bottom"] = _cast_bf16(w_in[0], name="cast_w_in_bottom", part=1, n_parts=2, after=top_started)
    shards.update({k: _cast_bf16(weights[k][0], name=f"cast_{k}", after=top_started) for k in by_need if k in weights})
    shards["taps"], tap_sizes = _pack([lru_conv_w[0], ffn_conv_w[0]])
    g_mix_after_start = start_gather(by_need[1:], g_mix, "start_gather_rest")

    def rows_in_order(g8):
        return g8.reshape(1, -1, g8.shape[-1])

    x2d = x.reshape(t, d)
    h1 = _rmsnorm_fwd(x2d, g_mix_after_start, name="norm_mix")
    ((w_in_top_g,),) = _exchange_now([_gather_second_leg(landed(["w_in_top"], h1, "await_w_in_top"))],
                                     name="second_leg_w_in_top")
    proj_top, _ = _mm_nn(h1, w_in_top_g, out_dtype=F32, name="proj_in_top")
    ((w_in_bottom_g, wa_8, wx_8, p_sg_g, taps_8),) = _exchange_now(
        [_gather_second_leg(landed(needed_next, proj_top, "await_w_in_bottom"))], name="second_leg_w_in_bottom")
    proj, _ = _mm_nn(h1, w_in_bottom_g, out_dtype=F32, residual=proj_top, a_part=1, name="proj_in")
    wa_g, wx_g = (jnp.swapaxes(w8, 0, 1).reshape(LRU_HEADS, hd, hd) for w8 in (wa_8, wx_8))
    wa_t, wx_t = jnp.swapaxes(wa_g, 1, 2), jnp.swapaxes(wx_g, 1, 2)
    tap_parts = [_unpack(taps_8[k], tap_sizes, [lru_conv_w.shape[1:], ffn_conv_w.shape[1:]]) for k in range(N_DEV)]
    lru_cw = jnp.concatenate([p[0] for p in tap_parts], axis=1)
    ffn_cw = jnp.concatenate([p[1] for p in tap_parts], axis=1)
    sg_w0 = sg_w[0]
    sg_w_t = jnp.swapaxes(sg_w0, 1, 2)
    sg_b_t = sg_b[0].T
    y_a = _mixer_a_fwd(proj, sg_ln_g, sg_ln_b, sg_w0, sg_b_t, sgw, name="mixer_a_fwd")
    y_b = _mixer_b_fwd(proj, lru_cw, lru_conv_b, wa_g, lru_ba, wx_g, lru_bx, lru_lam, seq=seq, sgw=sgw, lw=lw,
                       name="mixer_b_fwd")
    m_a, ((p_lru_8, w_out_8),) = _mm_nn(
        y_a, p_sg_g, out_dtype=F32, name="proj_sg",
        exchanges=[_gather_second_leg(landed(["p_lru", "w_out"], y_b, "await_p_lru_w_out"))])
    p_lru_g, w_out_g = rows_in_order(p_lru_8), rows_in_order(w_out_8)
    m_b, _ = _mm_nn(y_b, p_lru_g, out_dtype=F32, name="proj_lru")
    merged = _merge_fwd(m_a, m_b, proj, gate_col, name="merge_fwd")
    x1, ((w_up_g,),) = _mm_nn(merged, w_out_g, out_dtype=F32, residual=x2d, name="proj_out",
                              exchanges=[_gather_second_leg(landed(["w_up"], merged, "await_w_up"))])
    h2 = _rmsnorm_fwd(x1, g_ffn, name="norm_ffn")
    up_pre, _ = _mm_nn(h2, w_up_g, out_dtype=F32, name="ffn_up")
    act, ((w_down_8,),) = _ffn_mid_fwd(up_pre, ffn_cw, ffn_conv_b, seq=seq, name="ffn_mid_fwd",
                                       exchanges=[_gather_second_leg(landed(["w_down"], up_pre, "await_w_down"))])
    w_down_g = rows_in_order(w_down_8)
    x2, _ = _mm_nn(act, w_down_g, out_dtype=F32, residual=x1, name="ffn_down")
    d_x2, d_x2_b, d_g_final, loss_part = _loss_head(x2, g_final.reshape(1, d), loss_target.reshape(t, d), name="loss_head")
    loss = lax.psum(loss_part[0, 0], ("x", "y", "c"))

    def by_rows(g):
        return g.reshape(N_DEV, -1, g.shape[-1])

    def by_head_rows(g):
        return jnp.swapaxes(g.reshape(LRU_HEADS, N_DEV, hd // N_DEV, hd), 0, 1)

    def start_sibling_swap(views, key):
        return _start_exchange(_swap_with_sibling, views, [lax.empty((4,) + v.shape[2:], v.dtype) for v in views], token(),
                               name=f"start_sibling_{key}")

    chip_swaps = {}

    def sum_and_start_chip_swap(keys, sibling_swap, done, ride=None):
        views, from_sibling = _await_exchange(_swap_with_sibling, sibling_swap, done, name=f"await_sibling_{keys[0]}")
        sums = [_add_sibling_part(v, core, s, name=f"chip_sum_{k}") for k, v, s in zip(keys, views, from_sibling)]
        started, ride = _start_exchange(chip_swap, sums, [_own_block_first(s, 2 * xi + yi) for s in sums],
                                        token() if ride is None else ride, name=f"start_chips_{keys[0]}")
        chip_swaps.update(zip(keys, started))
        return ride

    d_w_down, _ = _mm_tn(act, d_x2_b, 1, name="grad_w_down")
    v_down = _by_chip_and_core(by_rows(d_w_down))
    swap, started = start_sibling_swap([v_down], "w_down")
    d_act, _ = _mm_nt(d_x2_b, w_down_g, name="bwd_ffn_down", after=started)
    ffn_cb = sum_and_start_chip_swap(["w_down"], swap, d_act, ride=ffn_conv_b)
    d_up_pre, d_ffn_cw, d_ffn_cb = _ffn_mid_bwd(up_pre, d_act, ffn_cw, ffn_cb, seq=seq, name="ffn_mid_bwd")
    d_w_up, _ = _mm_tn(h2, d_up_pre, N_DEV, name="grad_w_up")
    v_up = _by_chip_and_core(d_w_up)
    swap, started = start_sibling_swap([v_up], "w_up")
    d_h2, _ = _mm_nt(d_up_pre, w_up_g, name="bwd_ffn_up", after=started)
    g_ffn_then = sum_and_start_chip_swap(["w_up"], swap, d_h2, ride=g_ffn)
    d_x1, d_x1_b, d_g_ffn = _rmsnorm_bwd(x1, g_ffn_then, [d_h2], d_x2, name="norm_ffn_bwd")
    d_w_out, _ = _mm_tn(merged, d_x1_b, 1, name="grad_w_out")
    v_out = _by_chip_and_core(by_rows(d_w_out))
    swap, started = start_sibling_swap([v_out], "w_out")
    d_merged, _ = _mm_nt(d_x1_b, w_out_g, name="bwd_proj_out", after=started)
    started = sum_and_start_chip_swap(["w_out"], swap, d_merged)
    d_m_a, d_m_b, d_gates = _merge_bwd(d_merged, m_a, m_b, proj, gate_col, name="merge_bwd", after=started)
    d_p_sg, _ = _mm_tn(y_a, d_m_a, N_DEV, name="grad_p_sg")
    d_p_lru, _ = _mm_tn(y_b, d_m_b, 1, name="grad_p_lru")
    v_sg, v_lru = _by_chip_and_core(d_p_sg), _by_chip_and_core(by_rows(d_p_lru))
    swap, started = start_sibling_swap([v_sg, v_lru], "p_sg")
    d_y_a, _ = _mm_nt(d_m_a, p_sg_g, name="bwd_proj_sg", after=started)
    d_y_b, _ = _mm_nt(d_m_b, p_lru_g, name="bwd_proj_lru")
    lru_cb = sum_and_start_chip_swap(["p_sg", "p_lru"], swap, d_y_a, ride=lru_conv_b)
    d_zuv, d_sg_w, d_sg_b_t, d_ln_g, d_ln_b = _mixer_a_bwd(proj, d_y_a, sg_ln_g, sg_ln_b, sg_w0, sg_w_t, sg_b_t, sgw,
                                                           name="mixer_a_bwd")
    (d_xr, d_yr, d_lru_cw, d_lru_cb, d_wa, d_ba, d_wx, d_bx, d_lam), _ = _mixer_b_bwd(
        proj, d_y_b, lru_cw, lru_cb, wa_g, wa_t, lru_ba, wx_g, wx_t, lru_bx, lru_lam, seq=seq, sgw=sgw, lw=lw,
        name="mixer_b_bwd")
    d_proj = jnp.concatenate([d_zuv, d_xr, d_yr, d_gates], axis=1)
    v_wa = _by_chip_and_core(_cast_bf16(by_head_rows(d_wa), name="cast_grad_wa"))
    v_wx = _by_chip_and_core(_cast_bf16(by_head_rows(d_wx), name="cast_grad_wx"))
    d_w_in_top, _ = _mm_tn(h1, d_proj, N_DEV, name="grad_w_in_top", part=0, n_parts=2)
    swap_top, started = start_sibling_swap([_by_chip_and_core(d_w_in_top)], "w_in_top")
    d_w_in_bottom, _ = _mm_tn(h1, d_proj, N_DEV, name="grad_w_in_bottom", part=1, n_parts=2, after=started)
    swap_bottom, started = start_sibling_swap([_by_chip_and_core(d_w_in_bottom), v_wa, v_wx], "w_in_bottom")
    started = sum_and_start_chip_swap(["w_in_top"], swap_top, started)
    d_h1_left, _ = _mm_nt(d_proj, w_in_top_g, name="bwd_proj_in_top", after=started)
    d_h1_right, _ = _mm_nt(d_proj, w_in_bottom_g, name="bwd_proj_in_bottom")
    g_mix_then = sum_and_start_chip_swap(["w_in_bottom", "lru_wa", "lru_wx"], swap_bottom, d_h1_right, ride=g_mix)
    grad_x, _, d_g_mix = _rmsnorm_bwd(x2d, g_mix_then, [d_h1_left, d_h1_right], d_x1, name="norm_mix_bwd")

    small = ["g_mix", "sg_ln_g", "sg_ln_b", "sg_w", "sg_b", "lru_conv_b", "lru_ba", "lru_bx", "lru_lam", "g_ffn",
             "ffn_conv_b", "g_final", "lru_conv_w", "ffn_conv_w"]
    small_parts = dict(g_mix=d_g_mix, sg_ln_g=d_ln_g, sg_ln_b=d_ln_b, sg_w=d_sg_w, sg_b=d_sg_b_t.T, lru_conv_b=d_lru_cb,
                       lru_ba=d_ba, lru_bx=d_bx, lru_lam=d_lam, g_ffn=d_g_ffn, ffn_conv_b=d_ffn_cb, g_final=d_g_final,
                       lru_conv_w=d_lru_cw, ffn_conv_w=d_ffn_cw)
    packed, sizes = _pack([small_parts[k] for k in small])
    (all_small,) = _all_gather([packed], name="gather_small_grads")
    small_sum = _sum_parts(all_small, name="sum_small_grads")
    small_grads = dict(zip(small, _unpack(small_sum, sizes, [small_parts[k].shape for k in small])))
    for k in ("lru_conv_w", "ffn_conv_w"):
        n_loc = weights[k].shape[-1]
        small_grads[k] = lax.dynamic_slice_in_dim(small_grads[k], dev * n_loc, n_loc, axis=1)

    grads, deltas, new_m, new_v = {}, {}, {}, {}

    def update(k, chunks):
        grads[k], deltas[k], new_m[k], new_v[k] = _adamw(weights[k], m_in[k], v_in[k], chunks, name=f"adamw_{k}")

    def chip_sums_landed(keys, after):
        _, parts = _await_exchange(chip_swap, [chip_swaps[k] for k in keys], after, name=f"await_chips_{keys[0]}")
        return dict(zip(keys, parts))

    for keys in (["w_down"], ["w_up"], ["w_out"], ["p_sg", "p_lru"]):
        for k, parts in chip_sums_landed(keys, grad_x).items():
            update(k, [parts])
    for k in small:
        update(k, [small_grads[k][None]])
    last = chip_sums_landed(["w_in_top", "w_in_bottom", "lru_wa", "lru_wx"], deltas["w_up"])
    update("w_in", [last["w_in_top"], last["w_in_bottom"]])
    update("lru_wa", [last["lru_wa"]])
    update("lru_wx", [last["lru_wx"]])

    return (loss, grad_x.reshape(x.shape), *[grads[k] for k in order], *[deltas[k] for k in order],
            *[new_m[k] for k in order], *[new_v[k] for k in order])
```
